```python
import math
import jax, jax.numpy as jnp
from jax import lax
import numpy as np

D_MODEL = 1024
BATCH = 8
SEQ = 2048
DEPTH = 1

N_META = 16
C_CONV = D_MODEL
CONV_WIDTH = 31
N_HEADS = 16
HEAD_DIM = 64
ATTN_W = N_HEADS * HEAD_DIM
Q_BLOCK = 128
D_FF = ((8 * D_MODEL // 3 + 255) // 256) * 256
N_BRANCH = 2
IN_W = 2 * C_CONV + 3 * ATTN_W + N_BRANCH * D_MODEL
RMS_EPS = 1e-6
LN_EPS = 1e-5

kernel_name = "hybrid_conformer_stickbreaking_block"


def rms_norm(x, g):
    xf = x.astype(jnp.float32)
    y = xf * lax.rsqrt(jnp.mean(xf * xf, axis=-1, keepdims=True) + RMS_EPS)
    return (y * g.astype(jnp.float32)).astype(x.dtype)


def layer_norm(x, g, b):
    xf = x.astype(jnp.float32)
    mu = jnp.mean(xf, axis=-1, keepdims=True)
    xc = xf - mu
    var = jnp.mean(xc * xc, axis=-1, keepdims=True)
    y = xc * lax.rsqrt(var + LN_EPS) * g.astype(jnp.float32) + b.astype(jnp.float32)
    return y.astype(x.dtype)


def conformer_conv(u_glu, dw_w, dw_b, ln_g, ln_b, w_out):
    a, gate = jnp.split(u_glu, 2, axis=-1)
    u = a * jax.nn.sigmoid(gate)
    y = lax.conv_general_dilated(
        u, dw_w[:, None, :], window_strides=(1,),
        padding=[(CONV_WIDTH - 1, 0)],
        dimension_numbers=("NWC", "WIO", "NWC"),
        feature_group_count=C_CONV) + dw_b
    y = jax.nn.silu(layer_norm(y, ln_g, ln_b))
    return y @ w_out


def stick_breaking_block(q_blk, k_pre, v_pre, q_start):
    nq, nk = q_blk.shape[1], k_pre.shape[1]
    scale = 1.0 / math.sqrt(HEAD_DIM)
    z = jnp.einsum("bqhd,bkhd->bhqk", q_blk.astype(jnp.float32),
                   k_pre.astype(jnp.float32)) * scale
    t = q_start + jnp.arange(nq)
    s = jnp.arange(nk)
    mask = s[None, :] < t[:, None]
    sp = jnp.where(mask, jax.nn.softplus(z), 0.0)
    r = lax.cumsum(sp, axis=3, reverse=True)
    log_a = jnp.where(mask, z - r, -jnp.inf)
    a = jnp.exp(log_a)
    out = jnp.einsum("bhqk,bkhd->bqhd", a, v_pre.astype(jnp.float32))
    return out.astype(q_blk.dtype)


def stick_breaking_attention(q, k, v):
    L = q.shape[1]
    n_real_blocks = (L - N_META) // Q_BLOCK
    bounds = [(0, N_META)] + [(N_META + i * Q_BLOCK, N_META + (i + 1) * Q_BLOCK)
                              for i in range(n_real_blocks)]
    outs = []
    for (st, en) in bounds:
        outs.append(stick_breaking_block(q[:, st:en], k[:, :en], v[:, :en], st))
    return jnp.concatenate(outs, axis=1)


def _fwd_setup_inputs(seed: int = 0) -> dict:
    key = jax.random.key(seed)
    ks = jax.random.split(key, 20)
    f32 = jnp.float32
    nrm = lambda k, shape, s: jax.random.normal(k, shape, f32) * s
    gain = lambda k, shape: 1.0 + 0.02 * jax.random.normal(k, shape, f32)
    return {
        "x": jax.random.normal(ks[0], (BATCH, SEQ, D_MODEL), f32),
        "meta_tokens": nrm(ks[1], (N_META, D_MODEL), 1.0),
        "pre_mix_g": gain(ks[2], (DEPTH, D_MODEL)),
        "w_in": nrm(ks[3], (DEPTH, D_MODEL, IN_W), D_MODEL ** -0.5),
        "gate_b": nrm(ks[4], (DEPTH, N_BRANCH * D_MODEL), 0.02),
        "dw_w": nrm(ks[5], (DEPTH, CONV_WIDTH, C_CONV), CONV_WIDTH ** -0.5),
        "dw_b": nrm(ks[6], (DEPTH, C_CONV), 0.02),
        "conv_ln_g": gain(ks[7], (DEPTH, C_CONV)),
        "conv_ln_b": nrm(ks[8], (DEPTH, C_CONV), 0.02),
        "w_conv_out": nrm(ks[9], (DEPTH, C_CONV, D_MODEL), C_CONV ** -0.5),
        "w_attn_out": nrm(ks[10], (DEPTH, ATTN_W, D_MODEL), ATTN_W ** -0.5),
        "w_o": nrm(ks[11], (DEPTH, D_MODEL, D_MODEL), D_MODEL ** -0.5),
        "post_mix_g": gain(ks[12], (DEPTH, D_MODEL)),
        "pre_ffn_g": gain(ks[13], (DEPTH, D_MODEL)),
        "w_ffn_in": nrm(ks[14], (DEPTH, D_MODEL, 2 * D_FF), D_MODEL ** -0.5),
        "w_ffn_out": nrm(ks[15], (DEPTH, D_FF, D_MODEL), D_FF ** -0.5),
        "post_ffn_g": gain(ks[16], (DEPTH, D_MODEL)),
    }


def _fwd_reference(x, meta_tokens, pre_mix_g, w_in, gate_b, dw_w, dw_b, conv_ln_g, conv_ln_b,
              w_conv_out, w_attn_out, w_o, post_mix_g, pre_ffn_g, w_ffn_in, w_ffn_out,
              post_ffn_g):
    B = x.shape[0]
    meta = jnp.broadcast_to(meta_tokens[None].astype(x.dtype), (B, N_META, D_MODEL))
    h = jnp.concatenate([meta, x], axis=1)
    L = h.shape[1]
    for l in range(DEPTH):
        u = rms_norm(h, pre_mix_g[l])
        p = u @ w_in[l]
        o1 = 2 * C_CONV
        o2 = o1 + ATTN_W
        o3 = o2 + ATTN_W
        o4 = o3 + ATTN_W
        p_glu = p[..., :o1]
        q = p[..., o1:o2].reshape(B, L, N_HEADS, HEAD_DIM)
        k = p[..., o2:o3].reshape(B, L, N_HEADS, HEAD_DIM)
        v = p[..., o3:o4].reshape(B, L, N_HEADS, HEAD_DIM)
        gates = jax.nn.sigmoid(p[..., o4:] + gate_b[l])
        g_conv, g_attn = jnp.split(gates, 2, axis=-1)

        y_conv = conformer_conv(p_glu, dw_w[l], dw_b[l], conv_ln_g[l], conv_ln_b[l],
                                w_conv_out[l])
        y_attn = stick_breaking_attention(q, k, v).reshape(B, L, ATTN_W) @ w_attn_out[l]

        mix = (g_conv * y_conv + g_attn * y_attn) @ w_o[l]
        h = h + rms_norm(mix, post_mix_g[l])

        u = rms_norm(h, pre_ffn_g[l])
        a, b = jnp.split(u @ w_ffn_in[l], 2, axis=-1)
        f = (jax.nn.silu(a) * b) @ w_ffn_out[l]
        h = h + rms_norm(f, post_ffn_g[l])
    return h[:, N_META:]


import jax as _jax
import jax.numpy as _jnp

TWIN_FORMAT = 'train_step'
FWD_PARAMS = ['x', 'meta_tokens', 'pre_mix_g', 'w_in', 'gate_b', 'dw_w', 'dw_b', 'conv_ln_g', 'conv_ln_b', 'w_conv_out', 'w_attn_out', 'w_o', 'post_mix_g', 'pre_ffn_g', 'w_ffn_in', 'w_ffn_out', 'post_ffn_g']
TWIN_WEIGHTS = ['meta_tokens', 'pre_mix_g', 'w_in', 'gate_b', 'dw_w', 'dw_b', 'conv_ln_g', 'conv_ln_b', 'w_conv_out', 'w_attn_out', 'w_o', 'post_mix_g', 'pre_ffn_g', 'w_ffn_in', 'w_ffn_out', 'post_ffn_g']
TWIN_DIFF_INPUT = 'x'
TWIN_INPUTS = ['x', 'meta_tokens', 'pre_mix_g', 'w_in', 'gate_b', 'dw_w', 'dw_b', 'conv_ln_g', 'conv_ln_b', 'w_conv_out', 'w_attn_out', 'w_o', 'post_mix_g', 'pre_ffn_g', 'w_ffn_in', 'w_ffn_out', 'post_ffn_g', 'loss_target', 'm_meta_tokens', 'm_pre_mix_g', 'm_w_in', 'm_gate_b', 'm_dw_w', 'm_dw_b', 'm_conv_ln_g', 'm_conv_ln_b', 'm_w_conv_out', 'm_w_attn_out', 'm_w_o', 'm_post_mix_g', 'm_pre_ffn_g', 'm_w_ffn_in', 'm_w_ffn_out', 'm_post_ffn_g', 'v_meta_tokens', 'v_pre_mix_g', 'v_w_in', 'v_gate_b', 'v_dw_w', 'v_dw_b', 'v_conv_ln_g', 'v_conv_ln_b', 'v_w_conv_out', 'v_w_attn_out', 'v_w_o', 'v_post_mix_g', 'v_pre_ffn_g', 'v_w_ffn_in', 'v_w_ffn_out', 'v_post_ffn_g']
TWIN_OUTPUTS = ['loss', 'grad_x', 'grad_meta_tokens', 'grad_pre_mix_g', 'grad_w_in', 'grad_gate_b', 'grad_dw_w', 'grad_dw_b', 'grad_conv_ln_g', 'grad_conv_ln_b', 'grad_w_conv_out', 'grad_w_attn_out', 'grad_w_o', 'grad_post_mix_g', 'grad_pre_ffn_g', 'grad_w_ffn_in', 'grad_w_ffn_out', 'grad_post_ffn_g', 'delta_meta_tokens', 'delta_pre_mix_g', 'delta_w_in', 'delta_gate_b', 'delta_dw_w', 'delta_dw_b', 'delta_conv_ln_g', 'delta_conv_ln_b', 'delta_w_conv_out', 'delta_w_attn_out', 'delta_w_o', 'delta_post_mix_g', 'delta_pre_ffn_g', 'delta_w_ffn_in', 'delta_w_ffn_out', 'delta_post_ffn_g', 'new_m_meta_tokens', 'new_m_pre_mix_g', 'new_m_w_in', 'new_m_gate_b', 'new_m_dw_w', 'new_m_dw_b', 'new_m_conv_ln_g', 'new_m_conv_ln_b', 'new_m_w_conv_out', 'new_m_w_attn_out', 'new_m_w_o', 'new_m_post_mix_g', 'new_m_pre_ffn_g', 'new_m_w_ffn_in', 'new_m_w_ffn_out', 'new_m_post_ffn_g', 'new_v_meta_tokens', 'new_v_pre_mix_g', 'new_v_w_in', 'new_v_gate_b', 'new_v_dw_w', 'new_v_dw_b', 'new_v_conv_ln_g', 'new_v_conv_ln_b', 'new_v_w_conv_out', 'new_v_w_attn_out', 'new_v_w_o', 'new_v_post_mix_g', 'new_v_pre_ffn_g', 'new_v_w_ffn_in', 'new_v_w_ffn_out', 'new_v_post_ffn_g']
TWIN_LEAF_KINDS = {'loss': 'loss', 'grad_x': 'grad_x', 'grad_meta_tokens': 'grad_w', 'grad_pre_mix_g': 'grad_w', 'grad_w_in': 'grad_w', 'grad_gate_b': 'grad_w', 'grad_dw_w': 'grad_w', 'grad_dw_b': 'grad_w', 'grad_conv_ln_g': 'grad_w', 'grad_conv_ln_b': 'grad_w', 'grad_w_conv_out': 'grad_w', 'grad_w_attn_out': 'grad_w', 'grad_w_o': 'grad_w', 'grad_post_mix_g': 'grad_w', 'grad_pre_ffn_g': 'grad_w', 'grad_w_ffn_in': 'grad_w', 'grad_w_ffn_out': 'grad_w', 'grad_post_ffn_g': 'grad_w', 'delta_meta_tokens': 'delta_w', 'delta_pre_mix_g': 'delta_w', 'delta_w_in': 'delta_w', 'delta_gate_b': 'delta_w', 'delta_dw_w': 'delta_w', 'delta_dw_b': 'delta_w', 'delta_conv_ln_g': 'delta_w', 'delta_conv_ln_b': 'delta_w', 'delta_w_conv_out': 'delta_w', 'delta_w_attn_out': 'delta_w', 'delta_w_o': 'delta_w', 'delta_post_mix_g': 'delta_w', 'delta_pre_ffn_g': 'delta_w', 'delta_w_ffn_in': 'delta_w', 'delta_w_ffn_out': 'delta_w', 'delta_post_ffn_g': 'delta_w', 'new_m_meta_tokens': 'new_m', 'new_m_pre_mix_g': 'new_m', 'new_m_w_in': 'new_m', 'new_m_gate_b': 'new_m', 'new_m_dw_w': 'new_m', 'new_m_dw_b': 'new_m', 'new_m_conv_ln_g': 'new_m', 'new_m_conv_ln_b': 'new_m', 'new_m_w_conv_out': 'new_m', 'new_m_w_attn_out': 'new_m', 'new_m_w_o': 'new_m', 'new_m_post_mix_g': 'new_m', 'new_m_pre_ffn_g': 'new_m', 'new_m_w_ffn_in': 'new_m', 'new_m_w_ffn_out': 'new_m', 'new_m_post_ffn_g': 'new_m', 'new_v_meta_tokens': 'new_v', 'new_v_pre_mix_g': 'new_v', 'new_v_w_in': 'new_v', 'new_v_gate_b': 'new_v', 'new_v_dw_w': 'new_v', 'new_v_dw_b': 'new_v', 'new_v_conv_ln_g': 'new_v', 'new_v_conv_ln_b': 'new_v', 'new_v_w_conv_out': 'new_v', 'new_v_w_attn_out': 'new_v', 'new_v_w_o': 'new_v', 'new_v_post_mix_g': 'new_v', 'new_v_pre_ffn_g': 'new_v', 'new_v_w_ffn_in': 'new_v', 'new_v_w_ffn_out': 'new_v', 'new_v_post_ffn_g': 'new_v'}


def _forward(args):
    return _fwd_reference(*[args[k] for k in FWD_PARAMS])


def _output_shape():
    out = _jax.eval_shape(lambda: _forward(_fwd_setup_inputs(0)))
    return out.shape, out.dtype

N_MICROBATCH = 1
ADAM_LR = 0.001
ADAM_B1 = 0.9
ADAM_B2 = 0.999
ADAM_EPS = 1e-08
ADAM_WD = 0.01
ADAM_STEP = 10
PER_EXAMPLE_BATCH_AXIS = {'x': 0, 'loss_target': 0}
SHARED_INPUTS = []
_WEIGHT_DTYPES = {'meta_tokens': _jnp.float32, 'pre_mix_g': _jnp.float32, 'w_in': _jnp.float32, 'gate_b': _jnp.float32, 'dw_w': _jnp.float32, 'dw_b': _jnp.float32, 'conv_ln_g': _jnp.float32, 'conv_ln_b': _jnp.float32, 'w_conv_out': _jnp.float32, 'w_attn_out': _jnp.float32, 'w_o': _jnp.float32, 'post_mix_g': _jnp.float32, 'pre_ffn_g': _jnp.float32, 'w_ffn_in': _jnp.float32, 'w_ffn_out': _jnp.float32, 'post_ffn_g': _jnp.float32}
MOMENT_SCALE = {'meta_tokens': 1.461601e-02, 'pre_mix_g': 4.007288e-01, 'w_in': 1.457303e-01, 'gate_b': 1.194796e-01, 'dw_w': 2.349994e-01, 'dw_b': 1.412176e+00, 'conv_ln_g': 5.520939e-01, 'conv_ln_b': 7.678628e-01, 'w_conv_out': 3.476594e-01, 'w_attn_out': 2.428618e-01, 'w_o': 4.468833e-01, 'post_mix_g': 1.604133e+01, 'pre_ffn_g': 4.560073e-01, 'w_ffn_in': 1.917714e-01, 'w_ffn_out': 3.659778e-01, 'post_ffn_g': 1.597655e+01}


def _to_microbatches(a, axis):
    t = _jnp.moveaxis(a, axis, 0)
    t = t.reshape((N_MICROBATCH, t.shape[0] // N_MICROBATCH) + t.shape[1:])
    return _jnp.moveaxis(t, 1, axis + 1)


def setup_inputs(seed: int = 0) -> dict:
    inp = _fwd_setup_inputs(seed)
    key = _jax.random.fold_in(_jax.random.key(seed), 7919)
    shape, _ = _output_shape()
    out = dict(inp)
    out["loss_target"] = _jax.random.normal(_jax.random.fold_in(key, 0), shape, _jnp.float32)
    for i, name in enumerate(TWIN_WEIGHTS):
        w = inp[name].astype(_jnp.float32)
        if MOMENT_SCALE is None:
            s = _jnp.sqrt(_jnp.mean(_jnp.square(w)) + 1e-30)
        else:
            s = MOMENT_SCALE[name]
        km, kv = _jax.random.split(_jax.random.fold_in(key, i + 1))
        out[name] = w
        out["m_" + name] = s * _jax.random.normal(km, w.shape, _jnp.float32)
        out["v_" + name] = (s * s) * _jax.random.uniform(kv, w.shape, _jnp.float32, 0.5, 1.5)
    if N_MICROBATCH > 1:
        for name, axis in PER_EXAMPLE_BATCH_AXIS.items():
            out[name] = _to_microbatches(out[name], axis)
    return {'x': out['x'], 'meta_tokens': out['meta_tokens'], 'pre_mix_g': out['pre_mix_g'], 'w_in': out['w_in'], 'gate_b': out['gate_b'], 'dw_w': out['dw_w'], 'dw_b': out['dw_b'], 'conv_ln_g': out['conv_ln_g'], 'conv_ln_b': out['conv_ln_b'], 'w_conv_out': out['w_conv_out'], 'w_attn_out': out['w_attn_out'], 'w_o': out['w_o'], 'post_mix_g': out['post_mix_g'], 'pre_ffn_g': out['pre_ffn_g'], 'w_ffn_in': out['w_ffn_in'], 'w_ffn_out': out['w_ffn_out'], 'post_ffn_g': out['post_ffn_g'], 'loss_target': out['loss_target'], 'm_meta_tokens': out['m_meta_tokens'], 'm_pre_mix_g': out['m_pre_mix_g'], 'm_w_in': out['m_w_in'], 'm_gate_b': out['m_gate_b'], 'm_dw_w': out['m_dw_w'], 'm_dw_b': out['m_dw_b'], 'm_conv_ln_g': out['m_conv_ln_g'], 'm_conv_ln_b': out['m_conv_ln_b'], 'm_w_conv_out': out['m_w_conv_out'], 'm_w_attn_out': out['m_w_attn_out'], 'm_w_o': out['m_w_o'], 'm_post_mix_g': out['m_post_mix_g'], 'm_pre_ffn_g': out['m_pre_ffn_g'], 'm_w_ffn_in': out['m_w_ffn_in'], 'm_w_ffn_out': out['m_w_ffn_out'], 'm_post_ffn_g': out['m_post_ffn_g'], 'v_meta_tokens': out['v_meta_tokens'], 'v_pre_mix_g': out['v_pre_mix_g'], 'v_w_in': out['v_w_in'], 'v_gate_b': out['v_gate_b'], 'v_dw_w': out['v_dw_w'], 'v_dw_b': out['v_dw_b'], 'v_conv_ln_g': out['v_conv_ln_g'], 'v_conv_ln_b': out['v_conv_ln_b'], 'v_w_conv_out': out['v_w_conv_out'], 'v_w_attn_out': out['v_w_attn_out'], 'v_w_o': out['v_w_o'], 'v_post_mix_g': out['v_post_mix_g'], 'v_pre_ffn_g': out['v_pre_ffn_g'], 'v_w_ffn_in': out['v_w_ffn_in'], 'v_w_ffn_out': out['v_w_ffn_out'], 'v_post_ffn_g': out['v_post_ffn_g']}


def _loss(weights, diff, rest, loss_target):
    with _jax.named_scope("forward"):
        args = {**rest, TWIN_DIFF_INPUT: diff, **{k: w.astype(_WEIGHT_DTYPES[k]) for k, w in weights.items()}}
        y = _forward(args)
    with _jax.named_scope("loss_head"):
        err = _jnp.square(y.astype(_jnp.float32) - loss_target)
        return 0.5 * _jnp.sum(_jnp.mean(err, axis=-1)) if err.ndim else 0.5 * err


def _adamw(w, g, m, v):
    m = ADAM_B1 * m + (1.0 - ADAM_B1) * g
    v = ADAM_B2 * v + (1.0 - ADAM_B2) * _jnp.square(g)
    m_hat = m / (1.0 - ADAM_B1 ** ADAM_STEP)
    v_hat = v / (1.0 - ADAM_B2 ** ADAM_STEP)
    delta = -ADAM_LR * (m_hat / (_jnp.sqrt(v_hat) + ADAM_EPS) + ADAM_WD * w)
    return delta, m, v


def reference(x, meta_tokens, pre_mix_g, w_in, gate_b, dw_w, dw_b, conv_ln_g, conv_ln_b, w_conv_out, w_attn_out, w_o, post_mix_g, pre_ffn_g, w_ffn_in, w_ffn_out, post_ffn_g, loss_target, m_meta_tokens, m_pre_mix_g, m_w_in, m_gate_b, m_dw_w, m_dw_b, m_conv_ln_g, m_conv_ln_b, m_w_conv_out, m_w_attn_out, m_w_o, m_post_mix_g, m_pre_ffn_g, m_w_ffn_in, m_w_ffn_out, m_post_ffn_g, v_meta_tokens, v_pre_mix_g, v_w_in, v_gate_b, v_dw_w, v_dw_b, v_conv_ln_g, v_conv_ln_b, v_w_conv_out, v_w_attn_out, v_w_o, v_post_mix_g, v_pre_ffn_g, v_w_ffn_in, v_w_ffn_out, v_post_ffn_g):
    given = dict(x=x, meta_tokens=meta_tokens, pre_mix_g=pre_mix_g, w_in=w_in, gate_b=gate_b, dw_w=dw_w, dw_b=dw_b, conv_ln_g=conv_ln_g, conv_ln_b=conv_ln_b, w_conv_out=w_conv_out, w_attn_out=w_attn_out, w_o=w_o, post_mix_g=post_mix_g, pre_ffn_g=pre_ffn_g, w_ffn_in=w_ffn_in, w_ffn_out=w_ffn_out, post_ffn_g=post_ffn_g, loss_target=loss_target, m_meta_tokens=m_meta_tokens, m_pre_mix_g=m_pre_mix_g, m_w_in=m_w_in, m_gate_b=m_gate_b, m_dw_w=m_dw_w, m_dw_b=m_dw_b, m_conv_ln_g=m_conv_ln_g, m_conv_ln_b=m_conv_ln_b, m_w_conv_out=m_w_conv_out, m_w_attn_out=m_w_attn_out, m_w_o=m_w_o, m_post_mix_g=m_post_mix_g, m_pre_ffn_g=m_pre_ffn_g, m_w_ffn_in=m_w_ffn_in, m_w_ffn_out=m_w_ffn_out, m_post_ffn_g=m_post_ffn_g, v_meta_tokens=v_meta_tokens, v_pre_mix_g=v_pre_mix_g, v_w_in=v_w_in, v_gate_b=v_gate_b, v_dw_w=v_dw_w, v_dw_b=v_dw_b, v_conv_ln_g=v_conv_ln_g, v_conv_ln_b=v_conv_ln_b, v_w_conv_out=v_w_conv_out, v_w_attn_out=v_w_attn_out, v_w_o=v_w_o, v_post_mix_g=v_post_mix_g, v_pre_ffn_g=v_pre_ffn_g, v_w_ffn_in=v_w_ffn_in, v_w_ffn_out=v_w_ffn_out, v_post_ffn_g=v_post_ffn_g)
    weights = {n: given[n] for n in TWIN_WEIGHTS}
    shared = {n: given[n] for n in SHARED_INPUTS}
    per_example = {n: given[n] for n in ['x']}
    grad_fn = _jax.value_and_grad(_loss, argnums=(0, 1))

    def one_microbatch(ex, loss_target):
        ex = dict(ex)
        diff = ex.pop(TWIN_DIFF_INPUT)
        return grad_fn(weights, diff, {**shared, **ex}, loss_target)

    if N_MICROBATCH == 1:
        loss, (grad_w, grad_x) = one_microbatch(per_example, given["loss_target"])
    else:
        def body(carry, xs):
            loss_sum, grad_sum = carry
            l_k, (gw_k, gx_k) = one_microbatch(xs[0], xs[1])
            with _jax.named_scope("update"):
                return (loss_sum + l_k, _jax.tree.map(_jnp.add, grad_sum, gw_k)), gx_k

        init = (_jnp.zeros((), _jnp.float32), _jax.tree.map(_jnp.zeros_like, weights))
        (loss, grad_w), grad_x = _jax.lax.scan(body, init, (per_example, given["loss_target"]))
    with _jax.named_scope("update"):
        delta_w, new_m, new_v = {}, {}, {}
        for n in TWIN_WEIGHTS:
            delta_w[n], new_m[n], new_v[n] = _adamw(weights[n], grad_w[n], given["m_" + n], given["v_" + n])
    return (loss, grad_x, *[grad_w[n] for n in TWIN_WEIGHTS], *[delta_w[n] for n in TWIN_WEIGHTS],
            *[new_m[n] for n in TWIN_WEIGHTS], *[new_v[n] for n in TWIN_WEIGHTS])
```

```python
import jax
import jax.numpy as jnp
from jax import lax
from jax.experimental import pallas as pl
from jax.experimental.pallas import tpu as pltpu

F32 = jnp.float32
BF16 = jnp.bfloat16
I32 = jnp.int32

N_DEV = 8
LANE = 128
HEAD_DIM = 64
QB = 128
CONV_WIDTH = 31
CONV_PAD = 32
ROW_CHUNK = 16
RMS_EPS = 1e-6
LN_EPS = 1e-5
ADAM_LR = 0.001
ADAM_B1 = 0.9
ADAM_B2 = 0.999
ADAM_EPS = 1e-08
ADAM_WD = 0.01
ADAM_STEP = 10
VMEM_LIMIT = 56 * 1024 * 1024

NN = (((1,), (0,)), ((), ()))
NT = (((1,), (1,)), ((), ()))
TN = (((0,), (0,)), ((), ()))
MESH = pl.DeviceIdType.MESH
ANY = pl.BlockSpec(memory_space=pl.ANY)
VMEM_WHOLE = pl.BlockSpec(memory_space=pltpu.VMEM)


def _params(n_axes):
    return pltpu.CompilerParams(dimension_semantics=("arbitrary",) * n_axes, vmem_limit_bytes=VMEM_LIMIT)


def _row_tile(m):
    return 272 if m % 272 == 0 else 128


def _matmul(name, dims, a, b, a_spec, b_spec, o_spec, out_shape, grid, acc_axis=None):
    def body(a_ref, b_ref, o_ref):
        r = lax.dot_general(a_ref[...], b_ref[...], dims, preferred_element_type=F32)
        if acc_axis is None:
            o_ref[...] = r.astype(o_ref.dtype)
        else:
            k = pl.program_id(acc_axis)

            @pl.when(k == 0)
            def _():
                o_ref[...] = r

            @pl.when(k > 0)
            def _():
                o_ref[...] += r

    return pl.pallas_call(body, name=name, grid=grid, in_specs=[a_spec, b_spec], out_specs=o_spec,
                          out_shape=out_shape, compiler_params=_params(len(grid)))(a, b)


def _dense_fwd(name, a, w, out_dtype=F32):
    m, k = a.shape
    n = w.shape[1]
    tn = 512
    return _matmul(name, NN, a, w, pl.BlockSpec((m, k), lambda j: (0, 0)), pl.BlockSpec((k, tn), lambda j: (0, j)),
                   pl.BlockSpec((m, tn), lambda j: (0, j)), jax.ShapeDtypeStruct((m, n), out_dtype), (n // tn,))


def _dense_dx(name, dy, w, out_dtype):
    m, n = dy.shape
    k = w.shape[0]
    tk = 512
    return _matmul(name, NT, dy, w, pl.BlockSpec((m, n), lambda j: (0, 0)), pl.BlockSpec((tk, n), lambda j: (j, 0)),
                   pl.BlockSpec((m, tk), lambda j: (0, j)), jax.ShapeDtypeStruct((m, k), out_dtype), (k // tk,))


def _dense_dw(name, a, dy):
    m, k = a.shape
    n = dy.shape[1]
    tn = 512
    return _matmul(name, TN, a, dy, pl.BlockSpec((m, k), lambda j: (0, 0)), pl.BlockSpec((m, tn), lambda j: (0, j)),
                   pl.BlockSpec((k, tn), lambda j: (0, j)), jax.ShapeDtypeStruct((k, n), F32), (n // tn,))


def _rowwise(name, fn, row_ins, par_ins, row_outs, par_outs, *, grid, in_specs, out_specs, tm, row_axis):
    n_ri, n_pi, n_ro, n_po = len(row_ins), len(par_ins), len(row_outs), len(par_outs)
    n_steps = tm // ROW_CHUNK

    def body(*refs):
        ri = refs[:n_ri]
        pi = refs[n_ri:n_ri + n_pi]
        ro = refs[n_ri + n_pi:n_ri + n_pi + n_ro]
        po = refs[n_ri + n_pi + n_ro:]
        ps = [r[...] for r in pi]
        base = pl.program_id(row_axis) * tm

        def step(i, carry):
            r0 = pl.multiple_of(i * ROW_CHUNK, ROW_CHUNK)
            xs = [r[pl.ds(r0, ROW_CHUNK), :] for r in ri]
            outs, pouts = fn(base + r0, xs, ps)
            for r, o in zip(ro, outs):
                r[pl.ds(r0, ROW_CHUNK), :] = o.astype(r.dtype)
            return tuple(c + q for c, q in zip(carry, pouts))

        acc = lax.fori_loop(0, n_steps, step, tuple(jnp.zeros(s.shape, F32) for s in par_outs))
        if n_po:
            first = pl.program_id(0) == 0
            for ax in range(1, len(grid)):
                first = first & (pl.program_id(ax) == 0)

            @pl.when(first)
            def _():
                for r in po:
                    r[...] = jnp.zeros_like(r)

            for r, a in zip(po, acc):
                r[...] += a

    return pl.pallas_call(body, name=name, grid=grid, in_specs=in_specs, out_specs=out_specs,
                          out_shape=tuple(row_outs) + tuple(par_outs),
                          compiler_params=_params(len(grid)))(*row_ins, *par_ins)


def _rows(name, fn, row_ins, par_ins, row_out_dtypes, par_out_widths, row_in_cols=None, row_out_widths=None):
    m = row_ins[0].shape[0]
    tm = _row_tile(m)
    in_specs = []
    for k, a in enumerate(row_ins):
        if row_in_cols is not None and row_in_cols[k] is not None:
            width, cb = row_in_cols[k]
            in_specs.append(pl.BlockSpec((tm, width), lambda i, cb=cb: (i, cb)))
        else:
            in_specs.append(pl.BlockSpec((tm, a.shape[1]), lambda i: (i, 0)))
    for a in par_ins:
        in_specs.append(pl.BlockSpec(a.shape, lambda i: (0, 0)))
    if row_out_widths is None:
        row_out_widths = [row_ins[0].shape[1]] * len(row_out_dtypes)
    row_outs = [jax.ShapeDtypeStruct((m, w), dt) for w, dt in zip(row_out_widths, row_out_dtypes)]
    par_outs = [jax.ShapeDtypeStruct((1, w), F32) for w in par_out_widths]
    out_specs = [pl.BlockSpec((tm, s.shape[1]), lambda i: (i, 0)) for s in row_outs]
    out_specs += [pl.BlockSpec(s.shape, lambda i: (0, 0)) for s in par_outs]
    return _rowwise(name, fn, row_ins, par_ins, row_outs, par_outs, grid=(m // tm,), in_specs=in_specs,
                    out_specs=out_specs, tm=tm, row_axis=0)


def _rms(x, g):
    return x * lax.rsqrt(jnp.mean(x * x, axis=-1, keepdims=True) + RMS_EPS) * g


def _ln_silu(y, g, b):
    mu = jnp.mean(y, axis=-1, keepdims=True)
    yc = y - mu
    var = jnp.mean(yc * yc, axis=-1, keepdims=True)
    return jax.nn.silu(yc * lax.rsqrt(var + LN_EPS) * g + b)


def _gate_mix(pgc, pga, yc, ya, gb):
    d = pgc.shape[1]
    return jax.nn.sigmoid(pgc + gb[:, :d]) * yc + jax.nn.sigmoid(pga + gb[:, d:]) * ya


def _post_mix(h0, mix, g_post, g_pre):
    h1 = h0 + _rms(mix, g_post)
    return h1, _rms(h1, g_pre)


def _swiglu(a, b):
    return jax.nn.silu(a) * b


def _conv_taps():
    taps = []
    for b in range(8):
        for a in range(CONV_PAD // 8):
            s = 8 * a + b
            if s < CONV_WIDTH:
                taps.append((b, a, CONV_WIDTH - 1 - s))
    return taps


def _conv_fwd(p, dww, dwb, d_model):
    m = p.shape[0]
    nch = d_model // LANE
    n_chunk = m // QB
    taps = _conv_taps()

    def body(a_ref, g_ref, w_ref, b_ref, y_ref, upad):
        upad[0:CONV_PAD, :] = jnp.zeros((CONV_PAD, LANE), F32)

        def fill(i, c):
            r0 = pl.multiple_of(i * QB, QB)
            u = a_ref[pl.ds(r0, QB), :] * jax.nn.sigmoid(g_ref[pl.ds(r0, QB), :])
            upad[pl.ds(pl.multiple_of(r0 + CONV_PAD, 8), QB), :] = u
            return c

        lax.fori_loop(0, n_chunk, fill, 0)

        def conv(i, c):
            r0 = pl.multiple_of(i * QB, QB)
            win = upad[pl.ds(r0, QB + CONV_PAD), :]
            acc = jnp.broadcast_to(b_ref[...], (QB, LANE))
            rolled = {}
            for b, a, j in taps:
                if b not in rolled:
                    rolled[b] = win if b == 0 else pltpu.roll(win, b, axis=0)
                lo = CONV_PAD - 8 * a
                acc = acc + w_ref[j:j + 1, :] * rolled[b][lo:lo + QB, :]
            y_ref[pl.ds(r0, QB), :] = acc
            return c

        lax.fori_loop(0, n_chunk, conv, 0)

    col = lambda off: pl.BlockSpec((m, LANE), lambda c: (0, off + c))
    return pl.pallas_call(
        body, name="conv_fwd", grid=(nch,),
        in_specs=[col(0), col(nch), pl.BlockSpec((CONV_PAD, LANE), lambda c: (0, c)), pl.BlockSpec((1, LANE), lambda c: (0, c))],
        out_specs=col(0), out_shape=jax.ShapeDtypeStruct((m, d_model), F32),
        scratch_shapes=[pltpu.VMEM((m + CONV_PAD, LANE), F32)], compiler_params=_params(1))(p, p, dww, dwb)


def _conv_bwd(p, dy, dww, d_model):
    m = p.shape[0]
    nch = d_model // LANE
    n_chunk = m // QB
    taps = _conv_taps()
    win_rows = QB + CONV_PAD

    def body(a_ref, g_ref, dy_ref, w_ref, da_ref, dg_ref, dw_ref, db_ref, upad, dypad, wacc, bacc):
        upad[0:CONV_PAD, :] = jnp.zeros((CONV_PAD, LANE), F32)
        dypad[m:m + CONV_PAD, :] = jnp.zeros((CONV_PAD, LANE), F32)
        wacc[...] = jnp.zeros_like(wacc)
        bacc[...] = jnp.zeros_like(bacc)

        def fill(i, c):
            r0 = pl.multiple_of(i * QB, QB)
            u = a_ref[pl.ds(r0, QB), :] * jax.nn.sigmoid(g_ref[pl.ds(r0, QB), :])
            upad[pl.ds(pl.multiple_of(r0 + CONV_PAD, 8), QB), :] = u
            dypad[pl.ds(r0, QB), :] = dy_ref[pl.ds(r0, QB), :]
            return c

        lax.fori_loop(0, n_chunk, fill, 0)

        def chunk(i, c):
            r0 = pl.multiple_of(i * QB, QB)
            dwin = dypad[pl.ds(r0, win_rows), :]
            du = jnp.zeros((QB, LANE), F32)
            rolled = {}
            for b, a, j in taps:
                if b not in rolled:
                    rolled[b] = dwin if b == 0 else pltpu.roll(dwin, win_rows - b, axis=0)
                du = du + w_ref[j:j + 1, :] * rolled[b][8 * a:8 * a + QB, :]
            av = a_ref[pl.ds(r0, QB), :]
            sg = jax.nn.sigmoid(g_ref[pl.ds(r0, QB), :])
            da_ref[pl.ds(r0, QB), :] = (du * sg).astype(da_ref.dtype)
            dg_ref[pl.ds(r0, QB), :] = (du * av * sg * (1.0 - sg)).astype(dg_ref.dtype)
            dyc = dy_ref[pl.ds(r0, QB), :]
            uwin = upad[pl.ds(r0, win_rows), :]
            rolled = {}
            for b, a, j in taps:
                if b not in rolled:
                    rolled[b] = uwin if b == 0 else pltpu.roll(uwin, b, axis=0)
                lo = CONV_PAD - 8 * a
                prod = dyc * rolled[b][lo:lo + QB, :]
                wacc[j] += prod.reshape(QB // 8, 8, LANE).sum(axis=0)
            bacc[...] += dyc.reshape(QB // 8, 8, LANE).sum(axis=0)
            return c

        lax.fori_loop(0, n_chunk, chunk, 0)
        for j in range(CONV_WIDTH):
            dw_ref[j:j + 1, :] = jnp.sum(wacc[j], axis=0, keepdims=True)
        dw_ref[CONV_WIDTH:CONV_PAD, :] = jnp.zeros((CONV_PAD - CONV_WIDTH, LANE), F32)
        db_ref[...] = jnp.sum(bacc[...], axis=0, keepdims=True)

    col = lambda off: pl.BlockSpec((m, LANE), lambda c: (0, off + c))
    return pl.pallas_call(
        body, name="conv_bwd", grid=(nch,),
        in_specs=[col(0), col(nch), col(0), pl.BlockSpec((CONV_PAD, LANE), lambda c: (0, c))],
        out_specs=[col(0), col(0), pl.BlockSpec((CONV_PAD, LANE), lambda c: (0, c)), pl.BlockSpec((1, LANE), lambda c: (0, c))],
        out_shape=(jax.ShapeDtypeStruct((m, d_model), BF16), jax.ShapeDtypeStruct((m, d_model), BF16),
                   jax.ShapeDtypeStruct((CONV_PAD, d_model), F32), jax.ShapeDtypeStruct((1, d_model), F32)),
        scratch_shapes=[pltpu.VMEM((m + CONV_PAD, LANE), F32), pltpu.VMEM((m + CONV_PAD, LANE), F32),
                        pltpu.VMEM((CONV_PAD, 8, LANE), F32), pltpu.VMEM((8, LANE), F32)],
        compiler_params=_params(1))(p, p, dy, dww)


def _softplus_sigmoid(z):
    e = jnp.exp(-jnp.abs(z))
    r = 1.0 / (1.0 + e)
    return jnp.maximum(z, 0.0) - jnp.log(r), jnp.where(z >= 0.0, r, e * r)


def _split_dot(x, tri):
    hi = x.astype(BF16)
    lo = (x - hi.astype(F32)).astype(BF16)
    return jnp.dot(hi, tri, preferred_element_type=F32) + jnp.dot(lo, tri, preferred_element_type=F32)


def _tri(kind):
    jj = lax.broadcasted_iota(I32, (QB, 2 * QB), 0)
    ss = lax.broadcasted_iota(I32, (QB, 2 * QB), 1)
    keep = {"ge": jj >= ss, "lt": jj < ss, "le": jj <= ss}[kind]
    return jnp.where((ss >= QB) | keep, 1.0, 0.0).astype(BF16)


def _attn_fwd(p, d_model):
    m = p.shape[0]
    nqb = m // QB
    npair = d_model // LANE
    qo, ko, vo = 2 * npair, 3 * npair, 4 * npair
    scale = HEAD_DIM ** -0.5

    def body(q_ref, k_ref, v_ref, o_ref, t_ref, acc_ref, car_ref):
        qb = pl.program_id(1)
        lane = lax.broadcasted_iota(I32, (QB, LANE), 1)
        head0 = lane < HEAD_DIM
        causal = lax.broadcasted_iota(I32, (QB, QB), 1) < lax.broadcasted_iota(I32, (QB, QB), 0)
        tri = _tri("ge")
        q2 = (q_ref[...] * scale).astype(BF16)
        zero = jnp.zeros_like(q2)
        qh = (jnp.where(head0, q2, zero), jnp.where(head0, zero, q2))
        acc_ref[...] = jnp.zeros_like(acc_ref)
        car_ref[...] = jnp.zeros_like(car_ref)

        def tile(kb, diag):
            r0 = pl.multiple_of(kb * QB, QB)
            k2 = k_ref[pl.ds(r0, QB), :].astype(BF16)
            v2 = v_ref[pl.ds(r0, QB), :].astype(BF16)
            for h in range(2):
                z = lax.dot_general(qh[h], k2, NT, preferred_element_type=F32)
                sp, _ = _softplus_sigmoid(z)
                if diag:
                    sp = jnp.where(causal, sp, 0.0)
                cr = _split_dot(sp, tri)
                a = jnp.exp(z - (cr[:, :QB] + car_ref[h]))
                if diag:
                    a = jnp.where(causal, a, 0.0)
                acc_ref[h] += jnp.dot(a.astype(BF16), v2, preferred_element_type=F32)
                car_ref[h] += cr[:, QB:]

        tile(qb, True)

        def step(i, c):
            tile(qb - 1 - i, False)
            return c

        lax.fori_loop(0, qb, step, 0)
        o_ref[...] = jnp.where(head0, acc_ref[0], acc_ref[1]).astype(o_ref.dtype)
        t_ref[:, :QB] = car_ref[0]
        t_ref[:, QB:] = car_ref[1]

    return pl.pallas_call(
        body, name="attn_fwd", grid=(npair, nqb),
        in_specs=[pl.BlockSpec((QB, LANE), lambda hp, qb: (qb, qo + hp)),
                  pl.BlockSpec((m, LANE), lambda hp, qb: (0, ko + hp)),
                  pl.BlockSpec((m, LANE), lambda hp, qb: (0, vo + hp))],
        out_specs=[pl.BlockSpec((QB, LANE), lambda hp, qb: (qb, hp)),
                   pl.BlockSpec((QB, 2 * LANE), lambda hp, qb: (qb, hp))],
        out_shape=(jax.ShapeDtypeStruct((m, d_model), BF16), jax.ShapeDtypeStruct((m, 2 * d_model), F32)),
        scratch_shapes=[pltpu.VMEM((2, QB, LANE), F32), pltpu.VMEM((2, QB, LANE), F32)],
        compiler_params=_params(2))(p, p, p)


def _attn_bwd(p, d_o, tot, d_model):
    m = p.shape[0]
    nqb = m // QB
    npair = d_model // LANE
    qo, ko, vo = 2 * npair, 3 * npair, 4 * npair
    scale = HEAD_DIM ** -0.5

    def body(q_ref, k_ref, v_ref, do_ref, t_ref, dq_ref, dk_ref, dv_ref, dkacc, dvacc, dqacc, csp, cg):
        qb = pl.program_id(1)
        lane = lax.broadcasted_iota(I32, (QB, LANE), 1)
        head0 = lane < HEAD_DIM
        causal = lax.broadcasted_iota(I32, (QB, QB), 1) < lax.broadcasted_iota(I32, (QB, QB), 0)
        tri_lt = _tri("lt")
        tri_le = _tri("le")
        q2 = (q_ref[...] * scale).astype(BF16)
        do2 = do_ref[...]
        zero = jnp.zeros_like(q2)
        qh = (jnp.where(head0, q2, zero), jnp.where(head0, zero, q2))
        doh = (jnp.where(head0, do2, zero), jnp.where(head0, zero, do2))

        @pl.when(qb == 0)
        def _():
            dkacc[...] = jnp.zeros_like(dkacc)
            dvacc[...] = jnp.zeros_like(dvacc)

        dqacc[...] = jnp.zeros_like(dqacc)
        csp[...] = jnp.zeros_like(csp)
        cg[...] = jnp.zeros_like(cg)

        def tile(kb, diag):
            r0 = pl.multiple_of(kb * QB, QB)
            k2 = k_ref[pl.ds(r0, QB), :].astype(BF16)
            v2 = v_ref[pl.ds(r0, QB), :].astype(BF16)
            dkp, dvp = [], []
            for h in range(2):
                z = lax.dot_general(qh[h], k2, NT, preferred_element_type=F32)
                sp, sg = _softplus_sigmoid(z)
                if diag:
                    sp = jnp.where(causal, sp, 0.0)
                cr = _split_dot(sp, tri_lt)
                tot_h = t_ref[:, h * QB:(h + 1) * QB]
                a = jnp.exp(z - (tot_h - (csp[h] + cr[:, :QB])))
                if diag:
                    a = jnp.where(causal, a, 0.0)
                da = lax.dot_general(doh[h], v2, NT, preferred_element_type=F32)
                g = a * da
                gr = _split_dot(g, tri_le)
                dz = g - sg * (cg[h] + gr[:, :QB])
                if diag:
                    dz = jnp.where(causal, dz, 0.0)
                dzb = dz.astype(BF16)
                dqacc[h] += jnp.dot(dzb, k2, preferred_element_type=F32)
                dkp.append(lax.dot_general(dzb, q2, TN, preferred_element_type=F32))
                dvp.append(lax.dot_general(a.astype(BF16), do2, TN, preferred_element_type=F32))
                csp[h] += cr[:, QB:]
                cg[h] += gr[:, QB:]
            dkacc[pl.ds(r0, QB), :] += jnp.where(head0, dkp[0], dkp[1])
            dvacc[pl.ds(r0, QB), :] += jnp.where(head0, dvp[0], dvp[1])

        def step(kb, c):
            tile(kb, False)
            return c

        lax.fori_loop(0, qb, step, 0)
        tile(qb, True)
        dq_ref[...] = (jnp.where(head0, dqacc[0], dqacc[1]) * scale).astype(dq_ref.dtype)

        @pl.when(qb == nqb - 1)
        def _():
            dk_ref[...] = dkacc[...].astype(dk_ref.dtype)
            dv_ref[...] = dvacc[...].astype(dv_ref.dtype)

    out = jax.ShapeDtypeStruct((m, d_model), BF16)
    return pl.pallas_call(
        body, name="attn_bwd", grid=(npair, nqb),
        in_specs=[pl.BlockSpec((QB, LANE), lambda hp, qb: (qb, qo + hp)),
                  pl.BlockSpec((m, LANE), lambda hp, qb: (0, ko + hp)),
                  pl.BlockSpec((m, LANE), lambda hp, qb: (0, vo + hp)),
                  pl.BlockSpec((QB, LANE), lambda hp, qb: (qb, hp)),
                  pl.BlockSpec((QB, 2 * LANE), lambda hp, qb: (qb, hp))],
        out_specs=[pl.BlockSpec((QB, LANE), lambda hp, qb: (qb, hp)),
                   pl.BlockSpec((m, LANE), lambda hp, qb: (0, hp)),
                   pl.BlockSpec((m, LANE), lambda hp, qb: (0, hp))],
        out_shape=(out, out, out),
        scratch_shapes=[pltpu.VMEM((m, LANE), F32), pltpu.VMEM((m, LANE), F32), pltpu.VMEM((2, QB, LANE), F32),
                        pltpu.VMEM((2, QB, LANE), F32), pltpu.VMEM((2, QB, LANE), F32)],
        compiler_params=_params(2))(p, p, p, d_o, tot)


def _mesh_pos():
    return lax.axis_index("x"), lax.axis_index("y"), lax.axis_index("c")


def _other_chips(x, y):
    return [(1 - x, y), (x, 1 - y), (1 - x, 1 - y)]


def _dev(x, y, c):
    return 4 * x + 2 * y + c


def _all_gather(shards):
    n = len(shards)

    def body(*refs):
        ins, outs = refs[:n], refs[n:2 * n]
        send_sems, recv_sems, local_sems = refs[2 * n:]
        x, y, c = _mesh_pos()
        sibling = (x, y, 1 - c)
        chips = _other_chips(x, y)

        def copy(k, s, src, block, to):
            return pltpu.make_async_remote_copy(src_ref=src, dst_ref=outs[k].at[_dev(*block)], send_sem=send_sems.at[k, s],
                                                recv_sem=recv_sems.at[k, s], device_id=to, device_id_type=MESH)

        def held(k, block):
            return outs[k].at[_dev(*block)]

        mine = [pltpu.make_async_copy(ins[k], held(k, (x, y, c)), local_sems.at[k]) for k in range(n)]
        for cp in mine:
            cp.start()
        first = []
        for k in range(n):
            first.append(copy(k, 0, ins[k], (x, y, c), sibling))
            for j, chip in enumerate(chips):
                first.append(copy(k, 1 + j, ins[k], (x, y, c), (*chip, c)))
        for cp in first:
            cp.start()
        passed = []
        for j, chip in enumerate(chips):
            for k in range(n):
                copy(k, 1 + j, held(k, (*chip, c)), (*chip, c), (x, y, c)).wait_recv()
                fwd = copy(k, 4 + j, held(k, (*chip, c)), (*chip, c), sibling)
                fwd.start()
                passed.append(fwd)
        for k in range(n):
            copy(k, 0, held(k, (x, y, 1 - c)), (x, y, 1 - c), (x, y, c)).wait_recv()
            for j, chip in enumerate(chips):
                copy(k, 4 + j, held(k, (*chip, 1 - c)), (*chip, 1 - c), (x, y, c)).wait_recv()
        for cp in first + passed:
            cp.wait_send()
        for cp in mine:
            cp.wait()

    return pl.pallas_call(
        body, name="comm_all_gather", in_specs=[ANY] * n, out_specs=[ANY] * n,
        out_shape=[jax.ShapeDtypeStruct((N_DEV,) + s.shape, s.dtype) for s in shards],
        scratch_shapes=[pltpu.SemaphoreType.DMA((n, 7)), pltpu.SemaphoreType.DMA((n, 7)), pltpu.SemaphoreType.DMA((n,))],
    )(*shards)


def _exchange_sibling(grads):
    n = len(grads)

    def body(*refs):
        ins, outs = refs[:n], refs[n:2 * n]
        send_sems, recv_sems = refs[2 * n:]
        x, y, c = _mesh_pos()
        sibling = (x, y, 1 - c)
        chips = _other_chips(x, y) + [(x, y)]
        copies = []
        for k in range(n):
            for r, chip in enumerate(chips):
                copies.append(pltpu.make_async_remote_copy(
                    src_ref=ins[k].at[_dev(*chip, 1 - c)], dst_ref=outs[k].at[r], send_sem=send_sems.at[k, r],
                    recv_sem=recv_sems.at[k, r], device_id=sibling, device_id_type=MESH))
        for cp in copies:
            cp.start()
        for cp in copies:
            cp.wait_recv()
        for cp in copies:
            cp.wait_send()

    return pl.pallas_call(
        body, name="comm_rs_sibling", in_specs=[ANY] * n, out_specs=[ANY] * n,
        out_shape=[jax.ShapeDtypeStruct((4,) + g.shape[1:], g.dtype) for g in grads],
        scratch_shapes=[pltpu.SemaphoreType.DMA((n, 4)), pltpu.SemaphoreType.DMA((n, 4))],
    )(*grads)


def _exchange_chips(parts):
    n = len(parts)

    def body(*refs):
        ins, outs = refs[:n], refs[n:2 * n]
        send_sems, recv_sems = refs[2 * n:]
        x, y, c = _mesh_pos()
        copies = []
        for k in range(n):
            for r, chip in enumerate(_other_chips(x, y)):
                copies.append(pltpu.make_async_remote_copy(
                    src_ref=ins[k].at[r], dst_ref=outs[k].at[r], send_sem=send_sems.at[k, r],
                    recv_sem=recv_sems.at[k, r], device_id=(*chip, c), device_id_type=MESH))
        for cp in copies:
            cp.start()
        for cp in copies:
            cp.wait_recv()
        for cp in copies:
            cp.wait_send()

    return pl.pallas_call(
        body, name="comm_rs_chips", in_specs=[ANY] * n, out_specs=[ANY] * n,
        out_shape=[jax.ShapeDtypeStruct(a.shape, a.dtype) for a in parts],
        scratch_shapes=[pltpu.SemaphoreType.DMA((n, 3)), pltpu.SemaphoreType.DMA((n, 3))],
    )(*parts)


def _shard_tile(rows):
    for tr in range(min(rows, 352), 0, -1):
        if rows % tr == 0 and (tr % 16 == 0 or tr == rows):
            return tr


def _pair_sum(slab_idx, grad, from_sibling):
    _, rows, cols = grad.shape
    tr = _shard_tile(rows)

    def body(idx_ref, g_ref, s_ref, o_ref):
        o_ref[...] = (g_ref[...] + s_ref[...]).astype(o_ref.dtype)

    gs = pltpu.PrefetchScalarGridSpec(
        num_scalar_prefetch=1, grid=(3, rows // tr),
        in_specs=[pl.BlockSpec((None, tr, cols), lambda r, i, idx: (idx[r], i, 0)),
                  pl.BlockSpec((None, tr, cols), lambda r, i, idx: (r, i, 0))],
        out_specs=pl.BlockSpec((None, tr, cols), lambda r, i, idx: (r, i, 0)))
    return pl.pallas_call(body, name="rs_pair_sum", grid_spec=gs, out_shape=jax.ShapeDtypeStruct((3, rows, cols), BF16),
                          compiler_params=_params(2))(slab_idx, grad, from_sibling)


def _adamw_math(w, g, m, v):
    m = ADAM_B1 * m + (1.0 - ADAM_B1) * g
    v = ADAM_B2 * v + (1.0 - ADAM_B2) * (g * g)
    m_hat = m / (1.0 - ADAM_B1 ** ADAM_STEP)
    v_hat = v / (1.0 - ADAM_B2 ** ADAM_STEP)
    delta = -ADAM_LR * (m_hat / (jnp.sqrt(v_hat) + ADAM_EPS) + ADAM_WD * w)
    return delta, m, v


def _adamw_shard(me, grad, from_sibling, from_chips, w, m, v):
    rows, cols = w.shape
    tr = _shard_tile(rows)

    def body(me_ref, g_ref, s_ref, c_ref, w_ref, m_ref, v_ref, go_ref, do_ref, mo_ref, vo_ref):
        g = g_ref[...] + s_ref[...]
        for r in range(3):
            g = g + c_ref[r].astype(F32)
        delta, m_new, v_new = _adamw_math(w_ref[...], g, m_ref[...], v_ref[...])
        go_ref[...] = g
        do_ref[...] = delta
        mo_ref[...] = m_new
        vo_ref[...] = v_new

    flat = pl.BlockSpec((tr, cols), lambda i, me: (i, 0))
    gs = pltpu.PrefetchScalarGridSpec(
        num_scalar_prefetch=1, grid=(rows // tr,),
        in_specs=[pl.BlockSpec((None, tr, cols), lambda i, me: (me[0], i, 0)),
                  pl.BlockSpec((None, tr, cols), lambda i, me: (3, i, 0)),
                  pl.BlockSpec((3, tr, cols), lambda i, me: (0, i, 0)), flat, flat, flat],
        out_specs=[flat, flat, flat, flat])
    out = jax.ShapeDtypeStruct((rows, cols), F32)
    return pl.pallas_call(body, name="adamw_shard", grid_spec=gs, out_shape=(out, out, out, out),
                          compiler_params=_params(1))(me, grad, from_sibling, from_chips, w, m, v)


def _small_reduce_adamw(slabs, w, m, v):
    _, rows, _ = slabs.shape

    def body(s_ref, w_ref, m_ref, v_ref, g_ref, d_ref, mo_ref, vo_ref, land, send_sems, recv_sems):
        x, y, c = _mesh_pos()
        me = _dev(x, y, c)
        copies = []
        for mask in range(1, N_DEV):
            px, py, pc = x ^ (mask >> 2), y ^ ((mask >> 1) & 1), c ^ (mask & 1)
            copies.append(pltpu.make_async_remote_copy(
                src_ref=s_ref.at[_dev(px, py, pc)], dst_ref=land.at[me], send_sem=send_sems.at[mask - 1],
                recv_sem=recv_sems.at[mask - 1], device_id=(px, py, pc), device_id_type=MESH))
        for cp in copies:
            cp.start()
        land[me] = s_ref[me]
        for mask in range(1, N_DEV):
            px, py, pc = x ^ (mask >> 2), y ^ ((mask >> 1) & 1), c ^ (mask & 1)
            pltpu.make_async_remote_copy(
                src_ref=s_ref.at[me], dst_ref=land.at[_dev(px, py, pc)], send_sem=send_sems.at[mask - 1],
                recv_sem=recv_sems.at[mask - 1], device_id=(px, py, pc), device_id_type=MESH).wait_recv()
        for cp in copies:
            cp.wait_send()
        g = land[0]
        for d in range(1, N_DEV):
            g = g + land[d]
        delta, m_new, v_new = _adamw_math(w_ref[...], g, m_ref[...], v_ref[...])
        g_ref[...] = g
        d_ref[...] = delta
        mo_ref[...] = m_new
        vo_ref[...] = v_new

    out = jax.ShapeDtypeStruct((rows, LANE), F32)
    return pl.pallas_call(
        body, name="comm_small_reduce_adamw", in_specs=[VMEM_WHOLE] * 4, out_specs=[VMEM_WHOLE] * 4, out_shape=(out, out, out, out),
        scratch_shapes=[pltpu.VMEM((N_DEV, rows, LANE), F32), pltpu.SemaphoreType.DMA((N_DEV - 1,)),
                        pltpu.SemaphoreType.DMA((N_DEV - 1,))],
    )(slabs, w, m, v)


def _cast_bf16(arrs):
    n = len(arrs)

    def body(*refs):
        for i_ref, o_ref in zip(refs[:n], refs[n:]):
            o_ref[...] = i_ref[...].astype(BF16)

    return pl.pallas_call(body, name="cast_bf16", in_specs=[VMEM_WHOLE] * n, out_specs=[VMEM_WHOLE] * n,
                          out_shape=[jax.ShapeDtypeStruct(a.shape, BF16) for a in arrs],
                          compiler_params=pltpu.CompilerParams(vmem_limit_bytes=VMEM_LIMIT))(*arrs)


REPLICATED = ("pre_mix_g", "gate_b", "dw_b", "conv_ln_g", "conv_ln_b", "post_mix_g", "pre_ffn_g", "post_ffn_g")
SHARDED = ("w_in", "w_conv_out", "w_attn_out", "w_o", "w_ffn_in", "w_ffn_out")
WEIGHTS = ("meta_tokens", "pre_mix_g", "w_in", "gate_b", "dw_w", "dw_b", "conv_ln_g", "conv_ln_b", "w_conv_out",
           "w_attn_out", "w_o", "post_mix_g", "pre_ffn_g", "w_ffn_in", "w_ffn_out", "post_ffn_g")


def kernel(x, meta_tokens, pre_mix_g, w_in, gate_b, dw_w, dw_b, conv_ln_g, conv_ln_b, w_conv_out, w_attn_out, w_o, post_mix_g, pre_ffn_g, w_ffn_in, w_ffn_out, post_ffn_g, loss_target, m_meta_tokens, m_pre_mix_g, m_w_in, m_gate_b, m_dw_w, m_dw_b, m_conv_ln_g, m_conv_ln_b, m_w_conv_out, m_w_attn_out, m_w_o, m_post_mix_g, m_pre_ffn_g, m_w_ffn_in, m_w_ffn_out, m_post_ffn_g, v_meta_tokens, v_pre_mix_g, v_w_in, v_gate_b, v_dw_w, v_dw_b, v_conv_ln_g, v_conv_ln_b, v_w_conv_out, v_w_attn_out, v_w_o, v_post_mix_g, v_pre_ffn_g, v_w_ffn_in, v_w_ffn_out, v_post_ffn_g):
    given = dict(locals())
    seq, d = x.shape[1], x.shape[2]
    n_meta = meta_tokens.shape[0]
    length = n_meta + seq
    m_rows = -(-length // QB) * QB
    dc = d // N_DEV
    assert dc == LANE and n_meta % 8 == 0 and seq % 8 == 0
    fs = w_ffn_in.shape[2]
    fr = w_ffn_out.shape[1]
    assert 2 * fr == fs

    local = {k: given[k][0] for k in SHARDED}
    cast = _cast_bf16([local[k] for k in SHARDED])
    dww_pad = jnp.pad(dw_w[0], ((0, CONV_PAD - CONV_WIDTH), (0, 0)))
    gathered = _all_gather(list(cast) + [meta_tokens, dww_pad])
    wi = gathered[0]
    wco, wao, wo = (g.reshape(d, d) for g in gathered[1:4])
    wfi = gathered[4]
    wfo = gathered[5].reshape(N_DEV // 2, fs, d)
    meta_full = jnp.concatenate([gathered[6][j] for j in range(N_DEV)], axis=1)
    dww_full = jnp.concatenate([gathered[7][j] for j in range(N_DEV)], axis=1)
    ns = wi.shape[2]

    tail = jnp.zeros((m_rows - length, d), F32)
    h0 = jnp.concatenate([meta_full, x[0], tail], axis=0)
    target = jnp.concatenate([jnp.zeros((n_meta, d), F32), loss_target[0], tail], axis=0)

    (u,) = _rows("pre_mix_norm", lambda r0, xs, ps: ([_rms(xs[0], ps[0])], []), [h0], [pre_mix_g], [BF16], [])
    p = _matmul("in_proj", NN, u, wi, pl.BlockSpec((m_rows, d), lambda i: (0, 0)), pl.BlockSpec((None, d, ns), lambda i: (i, 0, 0)),
                pl.BlockSpec((m_rows, ns), lambda i: (0, i)), jax.ShapeDtypeStruct((m_rows, N_DEV * ns), F32), (N_DEV,))
    y = _conv_fwd(p, dww_full, dw_b, d)
    (yc,) = _rows("conv_norm", lambda r0, xs, ps: ([_ln_silu(xs[0], ps[0], ps[1])], []), [y], [conv_ln_g, conv_ln_b], [BF16], [])
    y_conv = _dense_fwd("conv_out", yc, wco)
    o, tot = _attn_fwd(p, d)
    y_attn = _dense_fwd("attn_out", o, wao)
    gate_cols = [(d, 5), (d, 6), None, None]
    (mixin,) = _rows("gate_mix", lambda r0, xs, ps: ([_gate_mix(*xs, ps[0])], []), [p, p, y_conv, y_attn], [gate_b], [BF16], [],
                     row_in_cols=gate_cols, row_out_widths=[d])
    mix = _dense_fwd("mix_out", mixin, wo)
    h1, u2 = _rows("post_mix", lambda r0, xs, ps: (list(_post_mix(xs[0], xs[1], ps[0], ps[1])), []), [h0, mix],
                   [post_mix_g, pre_ffn_g], [F32, BF16], [])
    ab = _matmul("ffn_in", NN, u2, wfi, pl.BlockSpec((m_rows, d), lambda i: (0, 0)), pl.BlockSpec((None, d, fs), lambda i: (i, 0, 0)),
                 pl.BlockSpec((None, m_rows, fs), lambda i: (i, 0, 0)), jax.ShapeDtypeStruct((N_DEV, m_rows, fs), F32), (N_DEV,))
    half = N_DEV // 2
    tm = _row_tile(m_rows)
    pair = lambda off: pl.BlockSpec((None, tm, fs), lambda j, i, off=off: (j + off, i, 0))
    (f_in,) = _rowwise("swiglu", lambda r0, xs, ps: ([_swiglu(xs[0], xs[1])], []), [ab, ab], [],
                       [jax.ShapeDtypeStruct((half, m_rows, fs), BF16)], [], grid=(half, m_rows // tm),
                       in_specs=[pair(0), pair(half)], out_specs=[pair(0)], tm=tm, row_axis=1)
    f = _matmul("ffn_out", NN, f_in, wfo, pl.BlockSpec((None, m_rows, fs), lambda j: (j, 0, 0)), pl.BlockSpec((None, fs, d), lambda j: (j, 0, 0)),
                pl.BlockSpec((m_rows, d), lambda j: (0, 0)), jax.ShapeDtypeStruct((m_rows, d), F32), (half,), acc_axis=0)

    def loss_head(r0, xs, ps):
        h1_, f_, t_ = xs
        r, vjp = jax.vjp(_rms, f_, ps[0])
        rows = r0 + lax.broadcasted_iota(I32, (ROW_CHUNK, 1), 0)
        real = (rows >= n_meta) & (rows < length)
        err = jnp.where(real, h1_ + r - t_, 0.0)
        dh2 = err * (1.0 / d)
        d_f, dg = vjp(dh2)
        part = jnp.sum(0.5 * jnp.mean(err * err, axis=-1, keepdims=True), axis=0, keepdims=True)
        return [d_f, dh2], [dg, jnp.broadcast_to(part, (1, LANE))]

    d_f, dh2, g_post_ffn, loss_part = _rows("loss_head", loss_head, [h1, f, target], [post_ffn_g], [BF16, F32], [d, LANE])

    d_fin = _matmul("ffn_out_dx", NT, d_f, wfo, pl.BlockSpec((m_rows, d), lambda j: (0, 0)), pl.BlockSpec((None, fs, d), lambda j: (j, 0, 0)),
                    pl.BlockSpec((None, m_rows, fs), lambda j: (j, 0, 0)), jax.ShapeDtypeStruct((half, m_rows, fs), F32), (half,))
    g_wfo = _matmul("ffn_out_dw", TN, f_in, d_f, pl.BlockSpec((None, m_rows, fs), lambda j: (j, 0, 0)), pl.BlockSpec((m_rows, d), lambda j: (0, 0)),
                    pl.BlockSpec((None, fs, d), lambda j: (j, 0, 0)), jax.ShapeDtypeStruct((half, fs, d), F32), (half,))

    def swiglu_bwd(r0, xs, ps):
        _, vjp = jax.vjp(_swiglu, xs[0], xs[1])
        return list(vjp(xs[2])), []

    out8 = lambda off: pl.BlockSpec((None, tm, fs), lambda j, i, off=off: (j + off, i, 0))
    d_a, d_b = _rowwise("swiglu_bwd", swiglu_bwd, [ab, ab, d_fin], [],
                        [jax.ShapeDtypeStruct((half, m_rows, fs), BF16)] * 2, [], grid=(half, m_rows // tm),
                        in_specs=[pair(0), pair(half), pair(0)], out_specs=[out8(0), out8(0)], tm=tm, row_axis=1)
    d_ab = jnp.concatenate([d_a, d_b], axis=0)
    du2 = _matmul("ffn_in_dx", NT, d_ab, wfi, pl.BlockSpec((None, m_rows, fs), lambda i: (i, 0, 0)), pl.BlockSpec((None, d, fs), lambda i: (i, 0, 0)),
                  pl.BlockSpec((m_rows, d), lambda i: (0, 0)), jax.ShapeDtypeStruct((m_rows, d), F32), (N_DEV,), acc_axis=0)
    g_wfi = _matmul("ffn_in_dw", TN, u2, d_ab, pl.BlockSpec((m_rows, d), lambda i: (0, 0)), pl.BlockSpec((None, m_rows, fs), lambda i: (i, 0, 0)),
                    pl.BlockSpec((None, d, fs), lambda i: (i, 0, 0)), jax.ShapeDtypeStruct((N_DEV, d, fs), F32), (N_DEV,))

    def post_mix_bwd(r0, xs, ps):
        h0_, mix_, dh2_, du2_ = xs
        _, vjp = jax.vjp(_post_mix, h0_, mix_, ps[0], ps[1])
        dh0_, dmix_, dg1, dg2 = vjp((dh2_, du2_))
        return [dmix_, dh0_], [dg1, dg2]

    d_mix, dh1, g_post_mix, g_pre_ffn = _rows("post_mix_bwd", post_mix_bwd, [h0, mix, dh2, du2], [post_mix_g, pre_ffn_g],
                                              [BF16, F32], [d, d])
    d_mixin = _dense_dx("mix_out_dx", d_mix, wo, F32)
    g_wo = _dense_dw("mix_out_dw", mixin, d_mix)

    def gate_mix_bwd(r0, xs, ps):
        _, vjp = jax.vjp(_gate_mix, xs[0], xs[1], xs[2], xs[3], ps[0])
        dpgc, dpga, dyc_, dya_, dgb = vjp(xs[4])
        return [dpgc, dpga, dyc_, dya_], [dgb]

    dp_gc, dp_ga, d_yconv, d_yattn, g_gate_b = _rows(
        "gate_mix_bwd", gate_mix_bwd, [p, p, y_conv, y_attn, d_mixin], [gate_b], [BF16] * 4, [2 * d],
        row_in_cols=gate_cols + [None], row_out_widths=[d] * 4)
    d_o = _dense_dx("attn_out_dx", d_yattn, wao, BF16)
    g_wao = _dense_dw("attn_out_dw", o, d_yattn)
    dq, dk, dv = _attn_bwd(p, d_o, tot, d)
    d_yc = _dense_dx("conv_out_dx", d_yconv, wco, F32)
    g_wco = _dense_dw("conv_out_dw", yc, d_yconv)

    def conv_norm_bwd(r0, xs, ps):
        _, vjp = jax.vjp(_ln_silu, xs[0], ps[0], ps[1])
        dy_, dg, db = vjp(xs[1])
        return [dy_], [dg, db]

    d_y, g_ln_g, g_ln_b = _rows("conv_norm_bwd", conv_norm_bwd, [y, d_yc], [conv_ln_g, conv_ln_b], [F32], [d, d])
    dp_a, dp_g, g_dww, g_dwb = _conv_bwd(p, d_y, dww_full, d)
    dp = jnp.concatenate([dp_a, dp_g, dq, dk, dv, dp_gc, dp_ga], axis=1)
    du = _matmul("in_proj_dx", NT, dp, wi, pl.BlockSpec((m_rows, ns), lambda i: (0, i)), pl.BlockSpec((None, d, ns), lambda i: (i, 0, 0)),
                 pl.BlockSpec((m_rows, d), lambda i: (0, 0)), jax.ShapeDtypeStruct((m_rows, d), F32), (N_DEV,), acc_axis=0)
    g_wi = _matmul("in_proj_dw", TN, u, dp, pl.BlockSpec((m_rows, d), lambda i: (0, 0)), pl.BlockSpec((m_rows, ns), lambda i: (0, i)),
                   pl.BlockSpec((None, d, ns), lambda i: (i, 0, 0)), jax.ShapeDtypeStruct((N_DEV, d, ns), F32), (N_DEV,))

    def pre_mix_bwd(r0, xs, ps):
        _, vjp = jax.vjp(_rms, xs[0], ps[0])
        dx, dg = vjp(xs[1])
        return [xs[2] + dx], [dg]

    dh0, g_pre_mix = _rows("pre_mix_bwd", pre_mix_bwd, [h0, du, dh1], [pre_mix_g], [F32], [d])
    grad_x = dh0[n_meta:length][None]

    x_i, y_i, c_i = _mesh_pos()
    me = _dev(x_i, y_i, c_i)
    me_arr = jnp.reshape(me, (1,)).astype(I32)
    slab_idx = jnp.stack([_dev(*chip, c_i) for chip in _other_chips(x_i, y_i)]).astype(I32)
    big = {"w_in": g_wi, "w_conv_out": g_wco.reshape(N_DEV, dc, d), "w_attn_out": g_wao.reshape(N_DEV, dc, d),
           "w_o": g_wo.reshape(N_DEV, dc, d), "w_ffn_in": g_wfi, "w_ffn_out": g_wfo.reshape(N_DEV, fr, d)}
    from_sibling = _exchange_sibling([big[k] for k in SHARDED])
    partial = [_pair_sum(slab_idx, big[k], s) for k, s in zip(SHARDED, from_sibling)]
    from_chips = _exchange_chips(partial)
    results = {}
    for k, s, ch in zip(SHARDED, from_sibling, from_chips):
        outs = _adamw_shard(me_arr, big[k], s, ch, local[k], given["m_" + k][0], given["v_" + k][0])
        results[k] = tuple(a[None] for a in outs)

    rep_grads = {"pre_mix_g": g_pre_mix, "gate_b": g_gate_b, "dw_b": g_dwb, "conv_ln_g": g_ln_g, "conv_ln_b": g_ln_b,
                 "post_mix_g": g_post_mix, "pre_ffn_g": g_pre_ffn, "post_ffn_g": g_post_ffn}

    def pack_rep(get):
        return jnp.concatenate([get(k) for k in REPLICATED], axis=1).reshape(-1, LANE)

    rep_rows = pack_rep(lambda k: rep_grads[k])
    n_rep = rep_rows.shape[0]
    loss_rows = jnp.broadcast_to(loss_part, (8, LANE))
    g_meta = dh0[0:n_meta]
    slabs = jnp.stack([jnp.concatenate([rep_rows, loss_rows, g_dww[:, j * LANE:(j + 1) * LANE], g_meta[:, j * LANE:(j + 1) * LANE]], axis=0)
                       for j in range(N_DEV)])

    def pack_small(prefix):
        dww_own = jnp.pad(given[prefix + "dw_w"][0], ((0, CONV_PAD - CONV_WIDTH), (0, 0)))
        return jnp.concatenate([pack_rep(lambda k: given[prefix + k]), jnp.zeros((8, LANE), F32), dww_own,
                                given[prefix + "meta_tokens"]], axis=0)

    small = _small_reduce_adamw(slabs, pack_small(""), pack_small("m_"), pack_small("v_"))
    loss = small[0][n_rep, 0]

    def unpack(arr):
        out = {}
        flat = arr[:n_rep].reshape(1, -1)
        off = 0
        for k in REPLICATED:
            w = given[k].shape[1]
            out[k] = flat[:, off:off + w]
            off += w
        out["dw_w"] = arr[n_rep + 8:n_rep + 8 + CONV_WIDTH][None]
        out["meta_tokens"] = arr[n_rep + 8 + CONV_PAD:n_rep + 8 + CONV_PAD + n_meta]
        return out

    small_out = [unpack(a) for a in small]
    for k in WEIGHTS:
        if k not in results:
            results[k] = tuple(s[k] for s in small_out)
    return (loss, grad_x, *[results[k][0] for k in WEIGHTS], *[results[k][1] for k in WEIGHTS],
            *[results[k][2] for k in WEIGHTS], *[results[k][3] for k in WEIGHTS])
```

```python
import jax
import jax.numpy as jnp
from jax import lax
from jax.experimental import pallas as pl
from jax.experimental.pallas import tpu as pltpu

F32 = jnp.float32
BF16 = jnp.bfloat16
I32 = jnp.int32

N_DEV = 8
LANE = 128
HEAD_DIM = 64
QB = 128
KEY_SHIFT = 2
KEY_TILES = 1 << KEY_SHIFT
KEY_CHUNK = KEY_TILES * QB
CONV_WIDTH = 31
CONV_PAD = 32
ROW_CHUNK = 16
RMS_EPS = 1e-6
LN_EPS = 1e-5
ADAM_LR = 0.001
ADAM_B1 = 0.9
ADAM_B2 = 0.999
ADAM_EPS = 1e-08
ADAM_WD = 0.01
ADAM_STEP = 10
VMEM_LIMIT = 56 * 1024 * 1024

NN = (((1,), (0,)), ((), ()))
NT = (((1,), (1,)), ((), ()))
TN = (((0,), (0,)), ((), ()))
MESH = pl.DeviceIdType.MESH
ANY = pl.BlockSpec(memory_space=pl.ANY)
VMEM_WHOLE = pl.BlockSpec(memory_space=pltpu.VMEM)


def _params(n_axes):
    return pltpu.CompilerParams(dimension_semantics=("arbitrary",) * n_axes, vmem_limit_bytes=VMEM_LIMIT)


def _row_tile(m):
    return 272 if m % 272 == 0 else 128


def _matmul(name, dims, a, b, a_spec, b_spec, o_spec, out_shape, grid, acc_axis=None):
    def body(a_ref, b_ref, o_ref):
        r = lax.dot_general(a_ref[...], b_ref[...], dims, preferred_element_type=F32)
        if acc_axis is None:
            o_ref[...] = r.astype(o_ref.dtype)
        else:
            k = pl.program_id(acc_axis)

            @pl.when(k == 0)
            def _():
                o_ref[...] = r

            @pl.when(k > 0)
            def _():
                o_ref[...] += r

    return pl.pallas_call(body, name=name, grid=grid, in_specs=[a_spec, b_spec], out_specs=o_spec,
                          out_shape=out_shape, compiler_params=_params(len(grid)))(a, b)


def _dense_fwd(name, a, w, out_dtype=F32):
    m, k = a.shape
    n = w.shape[1]
    tn = 512
    return _matmul(name, NN, a, w, pl.BlockSpec((m, k), lambda j: (0, 0)), pl.BlockSpec((k, tn), lambda j: (0, j)),
                   pl.BlockSpec((m, tn), lambda j: (0, j)), jax.ShapeDtypeStruct((m, n), out_dtype), (n // tn,))


def _dense_dx(name, dy, w, out_dtype):
    m, n = dy.shape
    k = w.shape[0]
    tk = 512
    return _matmul(name, NT, dy, w, pl.BlockSpec((m, n), lambda j: (0, 0)), pl.BlockSpec((tk, n), lambda j: (j, 0)),
                   pl.BlockSpec((m, tk), lambda j: (0, j)), jax.ShapeDtypeStruct((m, k), out_dtype), (k // tk,))


def _dense_dw(name, a, dy):
    m, k = a.shape
    n = dy.shape[1]
    tn = 512
    return _matmul(name, TN, a, dy, pl.BlockSpec((m, k), lambda j: (0, 0)), pl.BlockSpec((m, tn), lambda j: (0, j)),
                   pl.BlockSpec((k, tn), lambda j: (0, j)), jax.ShapeDtypeStruct((k, n), F32), (n // tn,))


def _rowwise(name, fn, row_ins, par_ins, row_outs, par_outs, *, grid, in_specs, out_specs, tm, row_axis):
    n_ri, n_pi, n_ro, n_po = len(row_ins), len(par_ins), len(row_outs), len(par_outs)
    n_steps = tm // ROW_CHUNK

    def body(*refs):
        ri = refs[:n_ri]
        pi = refs[n_ri:n_ri + n_pi]
        ro = refs[n_ri + n_pi:n_ri + n_pi + n_ro]
        po = refs[n_ri + n_pi + n_ro:]
        ps = [r[...] for r in pi]
        base = pl.program_id(row_axis) * tm

        def step(i, carry):
            r0 = pl.multiple_of(i * ROW_CHUNK, ROW_CHUNK)
            xs = [r[pl.ds(r0, ROW_CHUNK), :] for r in ri]
            outs, pouts = fn(base + r0, xs, ps)
            for r, o in zip(ro, outs):
                r[pl.ds(r0, ROW_CHUNK), :] = o.astype(r.dtype)
            return tuple(c + q for c, q in zip(carry, pouts))

        acc = lax.fori_loop(0, n_steps, step, tuple(jnp.zeros(s.shape, F32) for s in par_outs))
        if n_po:
            first = pl.program_id(0) == 0
            for ax in range(1, len(grid)):
                first = first & (pl.program_id(ax) == 0)

            @pl.when(first)
            def _():
                for r in po:
                    r[...] = jnp.zeros_like(r)

            for r, a in zip(po, acc):
                r[...] += a

    return pl.pallas_call(body, name=name, grid=grid, in_specs=in_specs, out_specs=out_specs,
                          out_shape=tuple(row_outs) + tuple(par_outs),
                          compiler_params=_params(len(grid)))(*row_ins, *par_ins)


def _rows(name, fn, row_ins, par_ins, row_out_dtypes, par_out_widths, row_in_cols=None, row_out_widths=None):
    m = row_ins[0].shape[0]
    tm = _row_tile(m)
    in_specs = []
    for k, a in enumerate(row_ins):
        if row_in_cols is not None and row_in_cols[k] is not None:
            width, cb = row_in_cols[k]
            in_specs.append(pl.BlockSpec((tm, width), lambda i, cb=cb: (i, cb)))
        else:
            in_specs.append(pl.BlockSpec((tm, a.shape[1]), lambda i: (i, 0)))
    for a in par_ins:
        in_specs.append(pl.BlockSpec(a.shape, lambda i: (0, 0)))
    if row_out_widths is None:
        row_out_widths = [row_ins[0].shape[1]] * len(row_out_dtypes)
    row_outs = [jax.ShapeDtypeStruct((m, w), dt) for w, dt in zip(row_out_widths, row_out_dtypes)]
    par_outs = [jax.ShapeDtypeStruct((1, w), F32) for w in par_out_widths]
    out_specs = [pl.BlockSpec((tm, s.shape[1]), lambda i: (i, 0)) for s in row_outs]
    out_specs += [pl.BlockSpec(s.shape, lambda i: (0, 0)) for s in par_outs]
    return _rowwise(name, fn, row_ins, par_ins, row_outs, par_outs, grid=(m // tm,), in_specs=in_specs,
                    out_specs=out_specs, tm=tm, row_axis=0)


def _rms(x, g):
    return x * lax.rsqrt(jnp.mean(x * x, axis=-1, keepdims=True) + RMS_EPS) * g


def _ln_silu(y, g, b):
    mu = jnp.mean(y, axis=-1, keepdims=True)
    yc = y - mu
    var = jnp.mean(yc * yc, axis=-1, keepdims=True)
    return jax.nn.silu(yc * lax.rsqrt(var + LN_EPS) * g + b)


def _gate_mix(pgc, pga, yc, ya, gb):
    d = pgc.shape[1]
    return jax.nn.sigmoid(pgc + gb[:, :d]) * yc + jax.nn.sigmoid(pga + gb[:, d:]) * ya


def _post_mix(h0, mix, g_post, g_pre):
    h1 = h0 + _rms(mix, g_post)
    return h1, _rms(h1, g_pre)


def _swiglu(a, b):
    return jax.nn.silu(a) * b


def _conv_taps():
    taps = []
    for b in range(8):
        for a in range(CONV_PAD // 8):
            s = 8 * a + b
            if s < CONV_WIDTH:
                taps.append((b, a, CONV_WIDTH - 1 - s))
    return taps


def _conv_fwd(p, dww, dwb, d_model):
    m = p.shape[0]
    nch = d_model // LANE
    n_chunk = m // QB
    taps = _conv_taps()

    def body(a_ref, g_ref, w_ref, b_ref, y_ref, upad):
        upad[0:CONV_PAD, :] = jnp.zeros((CONV_PAD, LANE), F32)

        def fill(i, c):
            r0 = pl.multiple_of(i * QB, QB)
            u = a_ref[pl.ds(r0, QB), :] * jax.nn.sigmoid(g_ref[pl.ds(r0, QB), :])
            upad[pl.ds(pl.multiple_of(r0 + CONV_PAD, 8), QB), :] = u
            return c

        lax.fori_loop(0, n_chunk, fill, 0)

        def conv(i, c):
            r0 = pl.multiple_of(i * QB, QB)
            win = upad[pl.ds(r0, QB + CONV_PAD), :]
            acc = jnp.broadcast_to(b_ref[...], (QB, LANE))
            rolled = {}
            for b, a, j in taps:
                if b not in rolled:
                    rolled[b] = win if b == 0 else pltpu.roll(win, b, axis=0)
                lo = CONV_PAD - 8 * a
                acc = acc + w_ref[j:j + 1, :] * rolled[b][lo:lo + QB, :]
            y_ref[pl.ds(r0, QB), :] = acc
            return c

        lax.fori_loop(0, n_chunk, conv, 0)

    col = lambda off: pl.BlockSpec((m, LANE), lambda c: (0, off + c))
    return pl.pallas_call(
        body, name="conv_fwd", grid=(nch,),
        in_specs=[col(0), col(nch), pl.BlockSpec((CONV_PAD, LANE), lambda c: (0, c)), pl.BlockSpec((1, LANE), lambda c: (0, c))],
        out_specs=col(0), out_shape=jax.ShapeDtypeStruct((m, d_model), F32),
        scratch_shapes=[pltpu.VMEM((m + CONV_PAD, LANE), F32)], compiler_params=_params(1))(p, p, dww, dwb)


def _conv_bwd(p, dy, dww, d_model):
    m = p.shape[0]
    nch = d_model // LANE
    n_chunk = m // QB
    taps = _conv_taps()
    win_rows = QB + CONV_PAD

    def body(a_ref, g_ref, dy_ref, w_ref, da_ref, dg_ref, dw_ref, db_ref, upad, dypad, wacc, bacc):
        upad[0:CONV_PAD, :] = jnp.zeros((CONV_PAD, LANE), F32)
        dypad[m:m + CONV_PAD, :] = jnp.zeros((CONV_PAD, LANE), F32)
        wacc[...] = jnp.zeros_like(wacc)
        bacc[...] = jnp.zeros_like(bacc)

        def fill(i, c):
            r0 = pl.multiple_of(i * QB, QB)
            u = a_ref[pl.ds(r0, QB), :] * jax.nn.sigmoid(g_ref[pl.ds(r0, QB), :])
            upad[pl.ds(pl.multiple_of(r0 + CONV_PAD, 8), QB), :] = u
            dypad[pl.ds(r0, QB), :] = dy_ref[pl.ds(r0, QB), :]
            return c

        lax.fori_loop(0, n_chunk, fill, 0)

        def chunk(i, c):
            r0 = pl.multiple_of(i * QB, QB)
            dwin = dypad[pl.ds(r0, win_rows), :]
            du = jnp.zeros((QB, LANE), F32)
            rolled = {}
            for b, a, j in taps:
                if b not in rolled:
                    rolled[b] = dwin if b == 0 else pltpu.roll(dwin, win_rows - b, axis=0)
                du = du + w_ref[j:j + 1, :] * rolled[b][8 * a:8 * a + QB, :]
            av = a_ref[pl.ds(r0, QB), :]
            sg = jax.nn.sigmoid(g_ref[pl.ds(r0, QB), :])
            da_ref[pl.ds(r0, QB), :] = (du * sg).astype(da_ref.dtype)
            dg_ref[pl.ds(r0, QB), :] = (du * av * sg * (1.0 - sg)).astype(dg_ref.dtype)
            dyc = dy_ref[pl.ds(r0, QB), :]
            uwin = upad[pl.ds(r0, win_rows), :]
            rolled = {}
            for b, a, j in taps:
                if b not in rolled:
                    rolled[b] = uwin if b == 0 else pltpu.roll(uwin, b, axis=0)
                lo = CONV_PAD - 8 * a
                prod = dyc * rolled[b][lo:lo + QB, :]
                wacc[j] += prod.reshape(QB // 8, 8, LANE).sum(axis=0)
            bacc[...] += dyc.reshape(QB // 8, 8, LANE).sum(axis=0)
            return c

        lax.fori_loop(0, n_chunk, chunk, 0)
        for j in range(CONV_WIDTH):
            dw_ref[j:j + 1, :] = jnp.sum(wacc[j], axis=0, keepdims=True)
        dw_ref[CONV_WIDTH:CONV_PAD, :] = jnp.zeros((CONV_PAD - CONV_WIDTH, LANE), F32)
        db_ref[...] = jnp.sum(bacc[...], axis=0, keepdims=True)

    col = lambda off: pl.BlockSpec((m, LANE), lambda c: (0, off + c))
    return pl.pallas_call(
        body, name="conv_bwd", grid=(nch,),
        in_specs=[col(0), col(nch), col(0), pl.BlockSpec((CONV_PAD, LANE), lambda c: (0, c))],
        out_specs=[col(0), col(0), pl.BlockSpec((CONV_PAD, LANE), lambda c: (0, c)), pl.BlockSpec((1, LANE), lambda c: (0, c))],
        out_shape=(jax.ShapeDtypeStruct((m, d_model), BF16), jax.ShapeDtypeStruct((m, d_model), BF16),
                   jax.ShapeDtypeStruct((CONV_PAD, d_model), F32), jax.ShapeDtypeStruct((1, d_model), F32)),
        scratch_shapes=[pltpu.VMEM((m + CONV_PAD, LANE), F32), pltpu.VMEM((m + CONV_PAD, LANE), F32),
                        pltpu.VMEM((CONV_PAD, 8, LANE), F32), pltpu.VMEM((8, LANE), F32)],
        compiler_params=_params(1))(p, p, dy, dww)


def _softplus(z):
    return jnp.maximum(z, 0.0) + jnp.log(1.0 + jnp.exp(-jnp.abs(z)))


def _softplus_sigmoid(z):
    e = jnp.exp(-jnp.abs(z))
    r = 1.0 / (1.0 + e)
    return jnp.maximum(z, 0.0) - jnp.log(r), jnp.where(z >= 0.0, r, e * r)


def _split_dot(x, tri):
    hi = x.astype(BF16)
    lo = (x - hi.astype(F32)).astype(BF16)
    return jnp.dot(jnp.concatenate([hi, lo], axis=1), tri, preferred_element_type=F32)


def _tri(kind):
    jj = lax.broadcasted_iota(I32, (2 * QB, 2 * QB), 0) & (QB - 1)
    ss = lax.broadcasted_iota(I32, (2 * QB, 2 * QB), 1)
    keep = {"ge": jj >= ss, "lt": jj < ss, "le": jj <= ss}[kind]
    return jnp.where((ss >= QB) | keep, 1.0, 0.0).astype(BF16)


def _attn_fwd(p, d_model):
    m = p.shape[0]
    nqb = m // QB
    npair = d_model // LANE
    qo, ko, vo = 2 * npair, 3 * npair, 4 * npair
    scale = HEAD_DIM ** -0.5

    assert nqb >= KEY_TILES

    def body(q_ref, k_ref, v_ref, o_ref, t_ref, acc_ref, car_ref):
        qb = pl.program_id(1)
        lane = lax.broadcasted_iota(I32, (QB, LANE), 1)
        head0 = lane < HEAD_DIM
        row_g = qb * QB + lax.broadcasted_iota(I32, (QB, KEY_CHUNK), 0)
        col_l = lax.broadcasted_iota(I32, (QB, KEY_CHUNK), 1)
        tri = _tri("ge")
        q2 = (q_ref[...] * scale).astype(BF16)
        zero = jnp.zeros_like(q2)
        qh = (jnp.where(head0, q2, zero), jnp.where(head0, zero, q2))
        acc_ref[...] = jnp.zeros_like(acc_ref)
        car_ref[...] = jnp.zeros_like(car_ref)

        def chunk(first_tile, bound):
            r0 = pl.multiple_of(first_tile * QB, QB)
            kc = k_ref[pl.ds(r0, KEY_CHUNK), :].astype(BF16)
            vc = v_ref[pl.ds(r0, KEY_CHUNK), :].astype(BF16)
            valid = None if bound is None else (col_l + r0) < bound
            for h in range(2):
                z = lax.dot_general(qh[h], kc, NT, preferred_element_type=F32)
                sp = _softplus(z)
                if valid is not None:
                    sp = jnp.where(valid, sp, 0.0)
                car = car_ref[h]
                a_tiles = [None] * KEY_TILES
                for i in reversed(range(KEY_TILES)):
                    cr = _split_dot(sp[:, i * QB:(i + 1) * QB], tri)
                    a_tiles[i] = jnp.exp(z[:, i * QB:(i + 1) * QB] - (cr[:, :QB] + car))
                    car = car + cr[:, QB:]
                a = jnp.concatenate(a_tiles, axis=1)
                if valid is not None:
                    a = jnp.where(valid, a, 0.0)
                acc_ref[h] += jnp.dot(a.astype(BF16), vc, preferred_element_type=F32)
                car_ref[h] = car

        near = jnp.maximum(qb - (KEY_TILES - 1), 0)
        chunk(near, row_g)
        n_full = lax.shift_right_logical(near, KEY_SHIFT)

        def step(i, c):
            chunk(near - KEY_TILES * (i + 1), None)
            return c

        lax.fori_loop(0, n_full, step, 0)
        left = near - KEY_TILES * n_full

        @pl.when(left > 0)
        def _():
            chunk(0, left * QB)

        o_ref[...] = jnp.where(head0, acc_ref[0], acc_ref[1]).astype(o_ref.dtype)
        t_ref[:, :QB] = car_ref[0]
        t_ref[:, QB:] = car_ref[1]

    return pl.pallas_call(
        body, name="attn_fwd", grid=(npair, nqb),
        in_specs=[pl.BlockSpec((QB, LANE), lambda hp, qb: (qb, qo + hp)),
                  pl.BlockSpec((m, LANE), lambda hp, qb: (0, ko + hp)),
                  pl.BlockSpec((m, LANE), lambda hp, qb: (0, vo + hp))],
        out_specs=[pl.BlockSpec((QB, LANE), lambda hp, qb: (qb, hp)),
                   pl.BlockSpec((QB, 2 * LANE), lambda hp, qb: (qb, hp))],
        out_shape=(jax.ShapeDtypeStruct((m, d_model), BF16), jax.ShapeDtypeStruct((m, 2 * d_model), F32)),
        scratch_shapes=[pltpu.VMEM((2, QB, LANE), F32), pltpu.VMEM((2, QB, LANE), F32)],
        compiler_params=_params(2))(p, p, p)


def _attn_bwd(p, d_o, tot, d_model):
    m = p.shape[0]
    nqb = m // QB
    npair = d_model // LANE
    qo, ko, vo = 2 * npair, 3 * npair, 4 * npair
    scale = HEAD_DIM ** -0.5

    assert nqb >= KEY_TILES

    def body(q_ref, k_ref, v_ref, do_ref, t_ref, dq_ref, dk_ref, dv_ref, dkacc, dvacc, dqacc, csp, cg):
        qb = pl.program_id(1)
        lane = lax.broadcasted_iota(I32, (QB, LANE), 1)
        head0 = lane < HEAD_DIM
        head0_keys = lax.broadcasted_iota(I32, (KEY_CHUNK, LANE), 1) < HEAD_DIM
        row_g = qb * QB + lax.broadcasted_iota(I32, (QB, KEY_CHUNK), 0)
        col_l = lax.broadcasted_iota(I32, (QB, KEY_CHUNK), 1)
        tri_lt = _tri("lt")
        tri_le = _tri("le")
        q2 = (q_ref[...] * scale).astype(BF16)
        do2 = do_ref[...]
        zero = jnp.zeros_like(q2)
        qh = (jnp.where(head0, q2, zero), jnp.where(head0, zero, q2))
        doh = (jnp.where(head0, do2, zero), jnp.where(head0, zero, do2))

        @pl.when(qb == 0)
        def _():
            dkacc[...] = jnp.zeros_like(dkacc)
            dvacc[...] = jnp.zeros_like(dvacc)

        dqacc[...] = jnp.zeros_like(dqacc)
        csp[...] = jnp.zeros_like(csp)
        cg[...] = jnp.zeros_like(cg)

        def chunk(first_tile, bound):
            r0 = pl.multiple_of(first_tile * QB, QB)
            kc = k_ref[pl.ds(r0, KEY_CHUNK), :].astype(BF16)
            vc = v_ref[pl.ds(r0, KEY_CHUNK), :].astype(BF16)
            valid = None if bound is None else (col_l + r0) < bound
            dkp, dvp = [], []
            for h in range(2):
                z = lax.dot_general(qh[h], kc, NT, preferred_element_type=F32)
                sp, sg = _softplus_sigmoid(z)
                if valid is not None:
                    sp = jnp.where(valid, sp, 0.0)
                da = lax.dot_general(doh[h], vc, NT, preferred_element_type=F32)
                tot_h = t_ref[:, h * QB:(h + 1) * QB]
                before, g_before = csp[h], cg[h]
                a_tiles, dz_tiles = [], []
                for i in range(KEY_TILES):
                    cols = slice(i * QB, (i + 1) * QB)
                    cr = _split_dot(sp[:, cols], tri_lt)
                    a = jnp.exp(z[:, cols] - (tot_h - (before + cr[:, :QB])))
                    if valid is not None:
                        a = jnp.where(valid[:, cols], a, 0.0)
                    g = a * da[:, cols]
                    gr = _split_dot(g, tri_le)
                    dz = g - sg[:, cols] * (g_before + gr[:, :QB])
                    if valid is not None:
                        dz = jnp.where(valid[:, cols], dz, 0.0)
                    a_tiles.append(a)
                    dz_tiles.append(dz)
                    before = before + cr[:, QB:]
                    g_before = g_before + gr[:, QB:]
                dzb = jnp.concatenate(dz_tiles, axis=1).astype(BF16)
                ab = jnp.concatenate(a_tiles, axis=1).astype(BF16)
                dqacc[h] += jnp.dot(dzb, kc, preferred_element_type=F32)
                dkp.append(lax.dot_general(dzb, q2, TN, preferred_element_type=F32))
                dvp.append(lax.dot_general(ab, do2, TN, preferred_element_type=F32))
                csp[h] = before
                cg[h] = g_before
            dkacc[pl.ds(r0, KEY_CHUNK), :] += jnp.where(head0_keys, dkp[0], dkp[1])
            dvacc[pl.ds(r0, KEY_CHUNK), :] += jnp.where(head0_keys, dvp[0], dvp[1])

        near = jnp.maximum(qb - (KEY_TILES - 1), 0)
        n_full = lax.shift_right_logical(near, KEY_SHIFT)

        def step(i, c):
            chunk(KEY_TILES * i, None)
            return c

        lax.fori_loop(0, n_full, step, 0)

        @pl.when(near > KEY_TILES * n_full)
        def _():
            chunk(KEY_TILES * n_full, near * QB)

        chunk(near, row_g)
        dq_ref[...] = (jnp.where(head0, dqacc[0], dqacc[1]) * scale).astype(dq_ref.dtype)

        @pl.when(qb == nqb - 1)
        def _():
            dk_ref[...] = dkacc[...].astype(dk_ref.dtype)
            dv_ref[...] = dvacc[...].astype(dv_ref.dtype)

    out = jax.ShapeDtypeStruct((m, d_model), BF16)
    return pl.pallas_call(
        body, name="attn_bwd", grid=(npair, nqb),
        in_specs=[pl.BlockSpec((QB, LANE), lambda hp, qb: (qb, qo + hp)),
                  pl.BlockSpec((m, LANE), lambda hp, qb: (0, ko + hp)),
                  pl.BlockSpec((m, LANE), lambda hp, qb: (0, vo + hp)),
                  pl.BlockSpec((QB, LANE), lambda hp, qb: (qb, hp)),
                  pl.BlockSpec((QB, 2 * LANE), lambda hp, qb: (qb, hp))],
        out_specs=[pl.BlockSpec((QB, LANE), lambda hp, qb: (qb, hp)),
                   pl.BlockSpec((m, LANE), lambda hp, qb: (0, hp)),
                   pl.BlockSpec((m, LANE), lambda hp, qb: (0, hp))],
        out_shape=(out, out, out),
        scratch_shapes=[pltpu.VMEM((m, LANE), F32), pltpu.VMEM((m, LANE), F32), pltpu.VMEM((2, QB, LANE), F32),
                        pltpu.VMEM((2, QB, LANE), F32), pltpu.VMEM((2, QB, LANE), F32)],
        compiler_params=_params(2))(p, p, p, d_o, tot)


def _mesh_pos():
    return lax.axis_index("x"), lax.axis_index("y"), lax.axis_index("c")


def _other_chips(x, y):
    return [(1 - x, y), (x, 1 - y), (1 - x, 1 - y)]


def _dev(x, y, c):
    return 4 * x + 2 * y + c


def _all_gather(shards):
    n = len(shards)

    def body(*refs):
        ins, outs = refs[:n], refs[n:2 * n]
        send_sems, recv_sems, local_sems = refs[2 * n:]
        x, y, c = _mesh_pos()
        sibling = (x, y, 1 - c)
        chips = _other_chips(x, y)

        def copy(k, s, src, block, to):
            return pltpu.make_async_remote_copy(src_ref=src, dst_ref=outs[k].at[_dev(*block)], send_sem=send_sems.at[k, s],
                                                recv_sem=recv_sems.at[k, s], device_id=to, device_id_type=MESH)

        def held(k, block):
            return outs[k].at[_dev(*block)]

        mine = [pltpu.make_async_copy(ins[k], held(k, (x, y, c)), local_sems.at[k]) for k in range(n)]
        for cp in mine:
            cp.start()
        first = []
        for k in range(n):
            first.append(copy(k, 0, ins[k], (x, y, c), sibling))
            for j, chip in enumerate(chips):
                first.append(copy(k, 1 + j, ins[k], (x, y, c), (*chip, c)))
        for cp in first:
            cp.start()
        passed = []
        for j, chip in enumerate(chips):
            for k in range(n):
                copy(k, 1 + j, held(k, (*chip, c)), (*chip, c), (x, y, c)).wait_recv()
                fwd = copy(k, 4 + j, held(k, (*chip, c)), (*chip, c), sibling)
                fwd.start()
                passed.append(fwd)
        for k in range(n):
            copy(k, 0, held(k, (x, y, 1 - c)), (x, y, 1 - c), (x, y, c)).wait_recv()
            for j, chip in enumerate(chips):
                copy(k, 4 + j, held(k, (*chip, 1 - c)), (*chip, 1 - c), (x, y, c)).wait_recv()
        for cp in first + passed:
            cp.wait_send()
        for cp in mine:
            cp.wait()

    return pl.pallas_call(
        body, name="comm_all_gather", in_specs=[ANY] * n, out_specs=[ANY] * n,
        out_shape=[jax.ShapeDtypeStruct((N_DEV,) + s.shape, s.dtype) for s in shards],
        scratch_shapes=[pltpu.SemaphoreType.DMA((n, 7)), pltpu.SemaphoreType.DMA((n, 7)), pltpu.SemaphoreType.DMA((n,))],
    )(*shards)


def _exchange_sibling(grads):
    n = len(grads)

    def body(*refs):
        ins, outs = refs[:n], refs[n:2 * n]
        send_sems, recv_sems = refs[2 * n:]
        x, y, c = _mesh_pos()
        sibling = (x, y, 1 - c)
        chips = _other_chips(x, y) + [(x, y)]
        copies = []
        for k in range(n):
            for r, chip in enumerate(chips):
                copies.append(pltpu.make_async_remote_copy(
                    src_ref=ins[k].at[_dev(*chip, 1 - c)], dst_ref=outs[k].at[r], send_sem=send_sems.at[k, r],
                    recv_sem=recv_sems.at[k, r], device_id=sibling, device_id_type=MESH))
        for cp in copies:
            cp.start()
        for cp in copies:
            cp.wait_recv()
        for cp in copies:
            cp.wait_send()

    return pl.pallas_call(
        body, name="comm_rs_sibling", in_specs=[ANY] * n, out_specs=[ANY] * n,
        out_shape=[jax.ShapeDtypeStruct((4,) + g.shape[1:], g.dtype) for g in grads],
        scratch_shapes=[pltpu.SemaphoreType.DMA((n, 4)), pltpu.SemaphoreType.DMA((n, 4))],
    )(*grads)


def _exchange_chips(parts):
    n = len(parts)

    def body(*refs):
        ins, outs = refs[:n], refs[n:2 * n]
        send_sems, recv_sems = refs[2 * n:]
        x, y, c = _mesh_pos()
        copies = []
        for k in range(n):
            for r, chip in enumerate(_other_chips(x, y)):
                copies.append(pltpu.make_async_remote_copy(
                    src_ref=ins[k].at[r], dst_ref=outs[k].at[r], send_sem=send_sems.at[k, r],
                    recv_sem=recv_sems.at[k, r], device_id=(*chip, c), device_id_type=MESH))
        for cp in copies:
            cp.start()
        for cp in copies:
            cp.wait_recv()
        for cp in copies:
            cp.wait_send()

    return pl.pallas_call(
        body, name="comm_rs_chips", in_specs=[ANY] * n, out_specs=[ANY] * n,
        out_shape=[jax.ShapeDtypeStruct(a.shape, a.dtype) for a in parts],
        scratch_shapes=[pltpu.SemaphoreType.DMA((n, 3)), pltpu.SemaphoreType.DMA((n, 3))],
    )(*parts)


def _shard_tile(rows):
    for tr in range(min(rows, 352), 0, -1):
        if rows % tr == 0 and (tr % 16 == 0 or tr == rows):
            return tr


def _pair_sum(slab_idx, grad, from_sibling):
    _, rows, cols = grad.shape
    tr = _shard_tile(rows)

    def body(idx_ref, g_ref, s_ref, o_ref):
        o_ref[...] = (g_ref[...] + s_ref[...]).astype(o_ref.dtype)

    gs = pltpu.PrefetchScalarGridSpec(
        num_scalar_prefetch=1, grid=(3, rows // tr),
        in_specs=[pl.BlockSpec((None, tr, cols), lambda r, i, idx: (idx[r], i, 0)),
                  pl.BlockSpec((None, tr, cols), lambda r, i, idx: (r, i, 0))],
        out_specs=pl.BlockSpec((None, tr, cols), lambda r, i, idx: (r, i, 0)))
    return pl.pallas_call(body, name="rs_pair_sum", grid_spec=gs, out_shape=jax.ShapeDtypeStruct((3, rows, cols), BF16),
                          compiler_params=_params(2))(slab_idx, grad, from_sibling)


def _adamw_math(w, g, m, v):
    m = ADAM_B1 * m + (1.0 - ADAM_B1) * g
    v = ADAM_B2 * v + (1.0 - ADAM_B2) * (g * g)
    m_hat = m / (1.0 - ADAM_B1 ** ADAM_STEP)
    v_hat = v / (1.0 - ADAM_B2 ** ADAM_STEP)
    delta = -ADAM_LR * (m_hat / (jnp.sqrt(v_hat) + ADAM_EPS) + ADAM_WD * w)
    return delta, m, v


def _adamw_shard(me, grad, from_sibling, from_chips, w, m, v):
    rows, cols = w.shape
    tr = _shard_tile(rows)

    def body(me_ref, g_ref, s_ref, c_ref, w_ref, m_ref, v_ref, go_ref, do_ref, mo_ref, vo_ref):
        g = g_ref[...] + s_ref[...]
        for r in range(3):
            g = g + c_ref[r].astype(F32)
        delta, m_new, v_new = _adamw_math(w_ref[...], g, m_ref[...], v_ref[...])
        go_ref[...] = g
        do_ref[...] = delta
        mo_ref[...] = m_new
        vo_ref[...] = v_new

    flat = pl.BlockSpec((tr, cols), lambda i, me: (i, 0))
    gs = pltpu.PrefetchScalarGridSpec(
        num_scalar_prefetch=1, grid=(rows // tr,),
        in_specs=[pl.BlockSpec((None, tr, cols), lambda i, me: (me[0], i, 0)),
                  pl.BlockSpec((None, tr, cols), lambda i, me: (3, i, 0)),
                  pl.BlockSpec((3, tr, cols), lambda i, me: (0, i, 0)), flat, flat, flat],
        out_specs=[flat, flat, flat, flat])
    out = jax.ShapeDtypeStruct((rows, cols), F32)
    return pl.pallas_call(body, name="adamw_shard", grid_spec=gs, out_shape=(out, out, out, out),
                          compiler_params=_params(1))(me, grad, from_sibling, from_chips, w, m, v)


def _small_reduce_adamw(slabs, w, m, v):
    _, rows, _ = slabs.shape

    def body(s_ref, w_ref, m_ref, v_ref, g_ref, d_ref, mo_ref, vo_ref, land, send_sems, recv_sems):
        x, y, c = _mesh_pos()
        me = _dev(x, y, c)
        copies = []
        for mask in range(1, N_DEV):
            px, py, pc = x ^ (mask >> 2), y ^ ((mask >> 1) & 1), c ^ (mask & 1)
            copies.append(pltpu.make_async_remote_copy(
                src_ref=s_ref.at[_dev(px, py, pc)], dst_ref=land.at[me], send_sem=send_sems.at[mask - 1],
                recv_sem=recv_sems.at[mask - 1], device_id=(px, py, pc), device_id_type=MESH))
        for cp in copies:
            cp.start()
        land[me] = s_ref[me]
        for mask in range(1, N_DEV):
            px, py, pc = x ^ (mask >> 2), y ^ ((mask >> 1) & 1), c ^ (mask & 1)
            pltpu.make_async_remote_copy(
                src_ref=s_ref.at[me], dst_ref=land.at[_dev(px, py, pc)], send_sem=send_sems.at[mask - 1],
                recv_sem=recv_sems.at[mask - 1], device_id=(px, py, pc), device_id_type=MESH).wait_recv()
        for cp in copies:
            cp.wait_send()
        g = land[0]
        for d in range(1, N_DEV):
            g = g + land[d]
        delta, m_new, v_new = _adamw_math(w_ref[...], g, m_ref[...], v_ref[...])
        g_ref[...] = g
        d_ref[...] = delta
        mo_ref[...] = m_new
        vo_ref[...] = v_new

    out = jax.ShapeDtypeStruct((rows, LANE), F32)
    return pl.pallas_call(
        body, name="comm_small_reduce_adamw", in_specs=[VMEM_WHOLE] * 4, out_specs=[VMEM_WHOLE] * 4, out_shape=(out, out, out, out),
        scratch_shapes=[pltpu.VMEM((N_DEV, rows, LANE), F32), pltpu.SemaphoreType.DMA((N_DEV - 1,)),
                        pltpu.SemaphoreType.DMA((N_DEV - 1,))],
    )(slabs, w, m, v)


def _cast_bf16(arrs):
    n = len(arrs)

    def body(*refs):
        for i_ref, o_ref in zip(refs[:n], refs[n:]):
            o_ref[...] = i_ref[...].astype(BF16)

    return pl.pallas_call(body, name="cast_bf16", in_specs=[VMEM_WHOLE] * n, out_specs=[VMEM_WHOLE] * n,
                          out_shape=[jax.ShapeDtypeStruct(a.shape, BF16) for a in arrs],
                          compiler_params=pltpu.CompilerParams(vmem_limit_bytes=VMEM_LIMIT))(*arrs)


REPLICATED = ("pre_mix_g", "gate_b", "dw_b", "conv_ln_g", "conv_ln_b", "post_mix_g", "pre_ffn_g", "post_ffn_g")
SHARDED = ("w_in", "w_conv_out", "w_attn_out", "w_o", "w_ffn_in", "w_ffn_out")
WEIGHTS = ("meta_tokens", "pre_mix_g", "w_in", "gate_b", "dw_w", "dw_b", "conv_ln_g", "conv_ln_b", "w_conv_out",
           "w_attn_out", "w_o", "post_mix_g", "pre_ffn_g", "w_ffn_in", "w_ffn_out", "post_ffn_g")


def kernel(x, meta_tokens, pre_mix_g, w_in, gate_b, dw_w, dw_b, conv_ln_g, conv_ln_b, w_conv_out, w_attn_out, w_o, post_mix_g, pre_ffn_g, w_ffn_in, w_ffn_out, post_ffn_g, loss_target, m_meta_tokens, m_pre_mix_g, m_w_in, m_gate_b, m_dw_w, m_dw_b, m_conv_ln_g, m_conv_ln_b, m_w_conv_out, m_w_attn_out, m_w_o, m_post_mix_g, m_pre_ffn_g, m_w_ffn_in, m_w_ffn_out, m_post_ffn_g, v_meta_tokens, v_pre_mix_g, v_w_in, v_gate_b, v_dw_w, v_dw_b, v_conv_ln_g, v_conv_ln_b, v_w_conv_out, v_w_attn_out, v_w_o, v_post_mix_g, v_pre_ffn_g, v_w_ffn_in, v_w_ffn_out, v_post_ffn_g):
    given = dict(locals())
    seq, d = x.shape[1], x.shape[2]
    n_meta = meta_tokens.shape[0]
    length = n_meta + seq
    m_rows = -(-length // QB) * QB
    dc = d // N_DEV
    assert dc == LANE and n_meta % 8 == 0 and seq % 8 == 0
    fs = w_ffn_in.shape[2]
    fr = w_ffn_out.shape[1]
    assert 2 * fr == fs

    local = {k: given[k][0] for k in SHARDED}
    cast = _cast_bf16([local[k] for k in SHARDED])
    dww_pad = jnp.pad(dw_w[0], ((0, CONV_PAD - CONV_WIDTH), (0, 0)))
    gathered = _all_gather(list(cast) + [meta_tokens, dww_pad])
    wi = gathered[0]
    wco, wao, wo = (g.reshape(d, d) for g in gathered[1:4])
    wfi = gathered[4]
    wfo = gathered[5].reshape(N_DEV // 2, fs, d)
    meta_full = jnp.concatenate([gathered[6][j] for j in range(N_DEV)], axis=1)
    dww_full = jnp.concatenate([gathered[7][j] for j in range(N_DEV)], axis=1)
    ns = wi.shape[2]

    tail = jnp.zeros((m_rows - length, d), F32)
    h0 = jnp.concatenate([meta_full, x[0], tail], axis=0)
    target = jnp.concatenate([jnp.zeros((n_meta, d), F32), loss_target[0], tail], axis=0)

    (u,) = _rows("pre_mix_norm", lambda r0, xs, ps: ([_rms(xs[0], ps[0])], []), [h0], [pre_mix_g], [BF16], [])
    p = _matmul("in_proj", NN, u, wi, pl.BlockSpec((m_rows, d), lambda i: (0, 0)), pl.BlockSpec((None, d, ns), lambda i: (i, 0, 0)),
                pl.BlockSpec((m_rows, ns), lambda i: (0, i)), jax.ShapeDtypeStruct((m_rows, N_DEV * ns), F32), (N_DEV,))
    y = _conv_fwd(p, dww_full, dw_b, d)
    (yc,) = _rows("conv_norm", lambda r0, xs, ps: ([_ln_silu(xs[0], ps[0], ps[1])], []), [y], [conv_ln_g, conv_ln_b], [BF16], [])
    y_conv = _dense_fwd("conv_out", yc, wco)
    o, tot = _attn_fwd(p, d)
    y_attn = _dense_fwd("attn_out", o, wao)
    gate_cols = [(d, 5), (d, 6), None, None]
    (mixin,) = _rows("gate_mix", lambda r0, xs, ps: ([_gate_mix(*xs, ps[0])], []), [p, p, y_conv, y_attn], [gate_b], [BF16], [],
                     row_in_cols=gate_cols, row_out_widths=[d])
    mix = _dense_fwd("mix_out", mixin, wo)
    h1, u2 = _rows("post_mix", lambda r0, xs, ps: (list(_post_mix(xs[0], xs[1], ps[0], ps[1])), []), [h0, mix],
                   [post_mix_g, pre_ffn_g], [F32, BF16], [])
    ab = _matmul("ffn_in", NN, u2, wfi, pl.BlockSpec((m_rows, d), lambda i: (0, 0)), pl.BlockSpec((None, d, fs), lambda i: (i, 0, 0)),
                 pl.BlockSpec((None, m_rows, fs), lambda i: (i, 0, 0)), jax.ShapeDtypeStruct((N_DEV, m_rows, fs), F32), (N_DEV,))
    half = N_DEV // 2
    tm = _row_tile(m_rows)
    pair = lambda off: pl.BlockSpec((None, tm, fs), lambda j, i, off=off: (j + off, i, 0))
    (f_in,) = _rowwise("swiglu", lambda r0, xs, ps: ([_swiglu(xs[0], xs[1])], []), [ab, ab], [],
                       [jax.ShapeDtypeStruct((half, m_rows, fs), BF16)], [], grid=(half, m_rows // tm),
                       in_specs=[pair(0), pair(half)], out_specs=[pair(0)], tm=tm, row_axis=1)
    f = _matmul("ffn_out", NN, f_in, wfo, pl.BlockSpec((None, m_rows, fs), lambda j: (j, 0, 0)), pl.BlockSpec((None, fs, d), lambda j: (j, 0, 0)),
                pl.BlockSpec((m_rows, d), lambda j: (0, 0)), jax.ShapeDtypeStruct((m_rows, d), F32), (half,), acc_axis=0)

    def loss_head(r0, xs, ps):
        h1_, f_, t_ = xs
        r, vjp = jax.vjp(_rms, f_, ps[0])
        rows = r0 + lax.broadcasted_iota(I32, (ROW_CHUNK, 1), 0)
        real = (rows >= n_meta) & (rows < length)
        err = jnp.where(real, h1_ + r - t_, 0.0)
        dh2 = err * (1.0 / d)
        d_f, dg = vjp(dh2)
        part = jnp.sum(0.5 * jnp.mean(err * err, axis=-1, keepdims=True), axis=0, keepdims=True)
        return [d_f, dh2], [dg, jnp.broadcast_to(part, (1, LANE))]

    d_f, dh2, g_post_ffn, loss_part = _rows("loss_head", loss_head, [h1, f, target], [post_ffn_g], [BF16, F32], [d, LANE])

    d_fin = _matmul("ffn_out_dx", NT, d_f, wfo, pl.BlockSpec((m_rows, d), lambda j: (0, 0)), pl.BlockSpec((None, fs, d), lambda j: (j, 0, 0)),
                    pl.BlockSpec((None, m_rows, fs), lambda j: (j, 0, 0)), jax.ShapeDtypeStruct((half, m_rows, fs), F32), (half,))
    g_wfo = _matmul("ffn_out_dw", TN, f_in, d_f, pl.BlockSpec((None, m_rows, fs), lambda j: (j, 0, 0)), pl.BlockSpec((m_rows, d), lambda j: (0, 0)),
                    pl.BlockSpec((None, fs, d), lambda j: (j, 0, 0)), jax.ShapeDtypeStruct((half, fs, d), F32), (half,))

    def swiglu_bwd(r0, xs, ps):
        _, vjp = jax.vjp(_swiglu, xs[0], xs[1])
        return list(vjp(xs[2])), []

    out8 = lambda off: pl.BlockSpec((None, tm, fs), lambda j, i, off=off: (j + off, i, 0))
    d_a, d_b = _rowwise("swiglu_bwd", swiglu_bwd, [ab, ab, d_fin], [],
                        [jax.ShapeDtypeStruct((half, m_rows, fs), BF16)] * 2, [], grid=(half, m_rows // tm),
                        in_specs=[pair(0), pair(half), pair(0)], out_specs=[out8(0), out8(0)], tm=tm, row_axis=1)
    d_ab = jnp.concatenate([d_a, d_b], axis=0)
    du2 = _matmul("ffn_in_dx", NT, d_ab, wfi, pl.BlockSpec((None, m_rows, fs), lambda i: (i, 0, 0)), pl.BlockSpec((None, d, fs), lambda i: (i, 0, 0)),
                  pl.BlockSpec((m_rows, d), lambda i: (0, 0)), jax.ShapeDtypeStruct((m_rows, d), F32), (N_DEV,), acc_axis=0)
    g_wfi = _matmul("ffn_in_dw", TN, u2, d_ab, pl.BlockSpec((m_rows, d), lambda i: (0, 0)), pl.BlockSpec((None, m_rows, fs), lambda i: (i, 0, 0)),
                    pl.BlockSpec((None, d, fs), lambda i: (i, 0, 0)), jax.ShapeDtypeStruct((N_DEV, d, fs), F32), (N_DEV,))

    def post_mix_bwd(r0, xs, ps):
        h0_, mix_, dh2_, du2_ = xs
        _, vjp = jax.vjp(_post_mix, h0_, mix_, ps[0], ps[1])
        dh0_, dmix_, dg1, dg2 = vjp((dh2_, du2_))
        return [dmix_, dh0_], [dg1, dg2]

    d_mix, dh1, g_post_mix, g_pre_ffn = _rows("post_mix_bwd", post_mix_bwd, [h0, mix, dh2, du2], [post_mix_g, pre_ffn_g],
                                              [BF16, F32], [d, d])
    d_mixin = _dense_dx("mix_out_dx", d_mix, wo, F32)
    g_wo = _dense_dw("mix_out_dw", mixin, d_mix)

    def gate_mix_bwd(r0, xs, ps):
        _, vjp = jax.vjp(_gate_mix, xs[0], xs[1], xs[2], xs[3], ps[0])
        dpgc, dpga, dyc_, dya_, dgb = vjp(xs[4])
        return [dpgc, dpga, dyc_, dya_], [dgb]

    dp_gc, dp_ga, d_yconv, d_yattn, g_gate_b = _rows(
        "gate_mix_bwd", gate_mix_bwd, [p, p, y_conv, y_attn, d_mixin], [gate_b], [BF16] * 4, [2 * d],
        row_in_cols=gate_cols + [None], row_out_widths=[d] * 4)
    d_o = _dense_dx("attn_out_dx", d_yattn, wao, BF16)
    g_wao = _dense_dw("attn_out_dw", o, d_yattn)
    dq, dk, dv = _attn_bwd(p, d_o, tot, d)
    d_yc = _dense_dx("conv_out_dx", d_yconv, wco, F32)
    g_wco = _dense_dw("conv_out_dw", yc, d_yconv)

    def conv_norm_bwd(r0, xs, ps):
        _, vjp = jax.vjp(_ln_silu, xs[0], ps[0], ps[1])
        dy_, dg, db = vjp(xs[1])
        return [dy_], [dg, db]

    d_y, g_ln_g, g_ln_b = _rows("conv_norm_bwd", conv_norm_bwd, [y, d_yc], [conv_ln_g, conv_ln_b], [F32], [d, d])
    dp_a, dp_g, g_dww, g_dwb = _conv_bwd(p, d_y, dww_full, d)
    dp = jnp.concatenate([dp_a, dp_g, dq, dk, dv, dp_gc, dp_ga], axis=1)
    du = _matmul("in_proj_dx", NT, dp, wi, pl.BlockSpec((m_rows, ns), lambda i: (0, i)), pl.BlockSpec((None, d, ns), lambda i: (i, 0, 0)),
                 pl.BlockSpec((m_rows, d), lambda i: (0, 0)), jax.ShapeDtypeStruct((m_rows, d), F32), (N_DEV,), acc_axis=0)
    g_wi = _matmul("in_proj_dw", TN, u, dp, pl.BlockSpec((m_rows, d), lambda i: (0, 0)), pl.BlockSpec((m_rows, ns), lambda i: (0, i)),
                   pl.BlockSpec((None, d, ns), lambda i: (i, 0, 0)), jax.ShapeDtypeStruct((N_DEV, d, ns), F32), (N_DEV,))

    def pre_mix_bwd(r0, xs, ps):
        _, vjp = jax.vjp(_rms, xs[0], ps[0])
        dx, dg = vjp(xs[1])
        return [xs[2] + dx], [dg]

    dh0, g_pre_mix = _rows("pre_mix_bwd", pre_mix_bwd, [h0, du, dh1], [pre_mix_g], [F32], [d])
    grad_x = dh0[n_meta:length][None]

    x_i, y_i, c_i = _mesh_pos()
    me = _dev(x_i, y_i, c_i)
    me_arr = jnp.reshape(me, (1,)).astype(I32)
    slab_idx = jnp.stack([_dev(*chip, c_i) for chip in _other_chips(x_i, y_i)]).astype(I32)
    big = {"w_in": g_wi, "w_conv_out": g_wco.reshape(N_DEV, dc, d), "w_attn_out": g_wao.reshape(N_DEV, dc, d),
           "w_o": g_wo.reshape(N_DEV, dc, d), "w_ffn_in": g_wfi, "w_ffn_out": g_wfo.reshape(N_DEV, fr, d)}
    from_sibling = _exchange_sibling([big[k] for k in SHARDED])
    partial = [_pair_sum(slab_idx, big[k], s) for k, s in zip(SHARDED, from_sibling)]
    from_chips = _exchange_chips(partial)
    results = {}
    for k, s, ch in zip(SHARDED, from_sibling, from_chips):
        outs = _adamw_shard(me_arr, big[k], s, ch, local[k], given["m_" + k][0], given["v_" + k][0])
        results[k] = tuple(a[None] for a in outs)

    rep_grads = {"pre_mix_g": g_pre_mix, "gate_b": g_gate_b, "dw_b": g_dwb, "conv_ln_g": g_ln_g, "conv_ln_b": g_ln_b,
                 "post_mix_g": g_post_mix, "pre_ffn_g": g_pre_ffn, "post_ffn_g": g_post_ffn}

    def pack_rep(get):
        return jnp.concatenate([get(k) for k in REPLICATED], axis=1).reshape(-1, LANE)

    rep_rows = pack_rep(lambda k: rep_grads[k])
    n_rep = rep_rows.shape[0]
    loss_rows = jnp.broadcast_to(loss_part, (8, LANE))
    g_meta = dh0[0:n_meta]
    slabs = jnp.stack([jnp.concatenate([rep_rows, loss_rows, g_dww[:, j * LANE:(j + 1) * LANE], g_meta[:, j * LANE:(j + 1) * LANE]], axis=0)
                       for j in range(N_DEV)])

    def pack_small(prefix):
        dww_own = jnp.pad(given[prefix + "dw_w"][0], ((0, CONV_PAD - CONV_WIDTH), (0, 0)))
        return jnp.concatenate([pack_rep(lambda k: given[prefix + k]), jnp.zeros((8, LANE), F32), dww_own,
                                given[prefix + "meta_tokens"]], axis=0)

    small = _small_reduce_adamw(slabs, pack_small(""), pack_small("m_"), pack_small("v_"))
    loss = small[0][n_rep, 0]

    def unpack(arr):
        out = {}
        flat = arr[:n_rep].reshape(1, -1)
        off = 0
        for k in REPLICATED:
            w = given[k].shape[1]
            out[k] = flat[:, off:off + w]
            off += w
        out["dw_w"] = arr[n_rep + 8:n_rep + 8 + CONV_WIDTH][None]
        out["meta_tokens"] = arr[n_rep + 8 + CONV_PAD:n_rep + 8 + CONV_PAD + n_meta]
        return out

    small_out = [unpack(a) for a in small]
    for k in WEIGHTS:
        if k not in results:
            results[k] = tuple(s[k] for s in small_out)
    return (loss, grad_x, *[results[k][0] for k in WEIGHTS], *[results[k][1] for k in WEIGHTS],
            *[results[k][2] for k in WEIGHTS], *[results[k][3] for k in WEIGHTS])
```

```python
import jax
import jax.numpy as jnp
from jax import lax
from jax.experimental import pallas as pl
from jax.experimental.pallas import tpu as pltpu

F32 = jnp.float32
BF16 = jnp.bfloat16
I32 = jnp.int32

N_DEV = 8
LANE = 128
HEAD_DIM = 64
QB = 128
KEY_SHIFT = 2
KEY_TILES = 1 << KEY_SHIFT
KEY_CHUNK = KEY_TILES * QB
CONV_WIDTH = 31
CONV_PAD = 32
ROW_CHUNK = 128
RMS_EPS = 1e-6
LN_EPS = 1e-5
ADAM_LR = 0.001
ADAM_B1 = 0.9
ADAM_B2 = 0.999
ADAM_EPS = 1e-08
ADAM_WD = 0.01
ADAM_STEP = 10
VMEM_LIMIT = 56 * 1024 * 1024

NN = (((1,), (0,)), ((), ()))
NT = (((1,), (1,)), ((), ()))
TN = (((0,), (0,)), ((), ()))
MESH = pl.DeviceIdType.MESH
ANY = pl.BlockSpec(memory_space=pl.ANY)
VMEM_WHOLE = pl.BlockSpec(memory_space=pltpu.VMEM)


def _params(n_axes):
    return pltpu.CompilerParams(dimension_semantics=("arbitrary",) * n_axes, vmem_limit_bytes=VMEM_LIMIT)


def _row_tile(m):
    assert m % QB == 0
    return QB


def _matmul(name, dims, a, b, a_spec, b_spec, o_spec, out_shape, grid, acc_axis=None):
    def body(a_ref, b_ref, o_ref):
        r = lax.dot_general(a_ref[...], b_ref[...], dims, preferred_element_type=F32)
        if acc_axis is None:
            o_ref[...] = r.astype(o_ref.dtype)
        else:
            k = pl.program_id(acc_axis)

            @pl.when(k == 0)
            def _():
                o_ref[...] = r

            @pl.when(k > 0)
            def _():
                o_ref[...] += r

    return pl.pallas_call(body, name=name, grid=grid, in_specs=[a_spec, b_spec], out_specs=o_spec,
                          out_shape=out_shape, compiler_params=_params(len(grid)))(a, b)


def _dense_fwd(name, a, w, out_dtype=F32):
    m, k = a.shape
    n = w.shape[1]
    tn = 512
    return _matmul(name, NN, a, w, pl.BlockSpec((m, k), lambda j: (0, 0)), pl.BlockSpec((k, tn), lambda j: (0, j)),
                   pl.BlockSpec((m, tn), lambda j: (0, j)), jax.ShapeDtypeStruct((m, n), out_dtype), (n // tn,))


def _dense_dx(name, dy, w, out_dtype):
    m, n = dy.shape
    k = w.shape[0]
    tk = 512
    return _matmul(name, NT, dy, w, pl.BlockSpec((m, n), lambda j: (0, 0)), pl.BlockSpec((tk, n), lambda j: (j, 0)),
                   pl.BlockSpec((m, tk), lambda j: (0, j)), jax.ShapeDtypeStruct((m, k), out_dtype), (k // tk,))


def _dense_dw(name, a, dy):
    m, k = a.shape
    n = dy.shape[1]
    tn = 512
    return _matmul(name, TN, a, dy, pl.BlockSpec((m, k), lambda j: (0, 0)), pl.BlockSpec((m, tn), lambda j: (0, j)),
                   pl.BlockSpec((k, tn), lambda j: (0, j)), jax.ShapeDtypeStruct((k, n), F32), (n // tn,))


def _rowwise(name, fn, row_ins, par_ins, row_outs, par_outs, *, grid, in_specs, out_specs, tm, row_axis):
    n_ri, n_pi, n_ro, n_po = len(row_ins), len(par_ins), len(row_outs), len(par_outs)
    n_steps = tm // ROW_CHUNK

    def body(*refs):
        ri = refs[:n_ri]
        pi = refs[n_ri:n_ri + n_pi]
        ro = refs[n_ri + n_pi:n_ri + n_pi + n_ro]
        po = refs[n_ri + n_pi + n_ro:]
        ps = [r[...] for r in pi]
        base = pl.program_id(row_axis) * tm

        def step(i, carry):
            r0 = pl.multiple_of(i * ROW_CHUNK, ROW_CHUNK)
            xs = [r[pl.ds(r0, ROW_CHUNK), :] for r in ri]
            outs, pouts = fn(base + r0, xs, ps)
            for r, o in zip(ro, outs):
                r[pl.ds(r0, ROW_CHUNK), :] = o.astype(r.dtype)
            return tuple(c + q for c, q in zip(carry, pouts))

        acc = lax.fori_loop(0, n_steps, step, tuple(jnp.zeros(s.shape, F32) for s in par_outs))
        if n_po:
            first = pl.program_id(0) == 0
            for ax in range(1, len(grid)):
                first = first & (pl.program_id(ax) == 0)

            @pl.when(first)
            def _():
                for r in po:
                    r[...] = jnp.zeros_like(r)

            for r, a in zip(po, acc):
                r[...] += a

    return pl.pallas_call(body, name=name, grid=grid, in_specs=in_specs, out_specs=out_specs,
                          out_shape=tuple(row_outs) + tuple(par_outs),
                          compiler_params=_params(len(grid)))(*row_ins, *par_ins)


def _rows(name, fn, row_ins, par_ins, row_out_dtypes, par_out_widths, row_in_cols=None, row_out_widths=None):
    m = row_ins[0].shape[0]
    tm = _row_tile(m)
    in_specs = []
    for k, a in enumerate(row_ins):
        if row_in_cols is not None and row_in_cols[k] is not None:
            width, cb = row_in_cols[k]
            in_specs.append(pl.BlockSpec((tm, width), lambda i, cb=cb: (i, cb)))
        else:
            in_specs.append(pl.BlockSpec((tm, a.shape[1]), lambda i: (i, 0)))
    for a in par_ins:
        in_specs.append(pl.BlockSpec(a.shape, lambda i: (0, 0)))
    if row_out_widths is None:
        row_out_widths = [row_ins[0].shape[1]] * len(row_out_dtypes)
    row_outs = [jax.ShapeDtypeStruct((m, w), dt) for w, dt in zip(row_out_widths, row_out_dtypes)]
    par_outs = [jax.ShapeDtypeStruct((1, w), F32) for w in par_out_widths]
    out_specs = [pl.BlockSpec((tm, s.shape[1]), lambda i: (i, 0)) for s in row_outs]
    out_specs += [pl.BlockSpec(s.shape, lambda i: (0, 0)) for s in par_outs]
    return _rowwise(name, fn, row_ins, par_ins, row_outs, par_outs, grid=(m // tm,), in_specs=in_specs,
                    out_specs=out_specs, tm=tm, row_axis=0)


def _rms(x, g):
    return x * lax.rsqrt(jnp.mean(x * x, axis=-1, keepdims=True) + RMS_EPS) * g


def _ln_silu(y, g, b):
    mu = jnp.mean(y, axis=-1, keepdims=True)
    yc = y - mu
    var = jnp.mean(yc * yc, axis=-1, keepdims=True)
    return jax.nn.silu(yc * lax.rsqrt(var + LN_EPS) * g + b)


def _gate_mix(pgc, pga, yc, ya, gb):
    d = pgc.shape[1]
    return jax.nn.sigmoid(pgc + gb[:, :d]) * yc + jax.nn.sigmoid(pga + gb[:, d:]) * ya


def _post_mix(h0, mix, g_post, g_pre):
    h1 = h0 + _rms(mix, g_post)
    return h1, _rms(h1, g_pre)


def _swiglu(a, b):
    return jax.nn.silu(a) * b


def _conv_taps():
    taps = []
    for b in range(8):
        for a in range(CONV_PAD // 8):
            s = 8 * a + b
            if s < CONV_WIDTH:
                taps.append((b, a, CONV_WIDTH - 1 - s))
    return taps


def _conv_fwd(p, dww, dwb, d_model):
    m = p.shape[0]
    nch = d_model // LANE
    n_chunk = m // QB
    taps = _conv_taps()

    def body(a_ref, g_ref, w_ref, b_ref, y_ref, upad):
        upad[0:CONV_PAD, :] = jnp.zeros((CONV_PAD, LANE), F32)

        def fill(i, c):
            r0 = pl.multiple_of(i * QB, QB)
            u = a_ref[pl.ds(r0, QB), :] * jax.nn.sigmoid(g_ref[pl.ds(r0, QB), :])
            upad[pl.ds(pl.multiple_of(r0 + CONV_PAD, 8), QB), :] = u
            return c

        lax.fori_loop(0, n_chunk, fill, 0)

        def conv(i, c):
            r0 = pl.multiple_of(i * QB, QB)
            win = upad[pl.ds(r0, QB + CONV_PAD), :]
            acc = jnp.broadcast_to(b_ref[...], (QB, LANE))
            rolled = {}
            for b, a, j in taps:
                if b not in rolled:
                    rolled[b] = win if b == 0 else pltpu.roll(win, b, axis=0)
                lo = CONV_PAD - 8 * a
                acc = acc + w_ref[j:j + 1, :] * rolled[b][lo:lo + QB, :]
            y_ref[pl.ds(r0, QB), :] = acc
            return c

        lax.fori_loop(0, n_chunk, conv, 0)

    col = lambda off: pl.BlockSpec((m, LANE), lambda c: (0, off + c))
    return pl.pallas_call(
        body, name="conv_fwd", grid=(nch,),
        in_specs=[col(0), col(nch), pl.BlockSpec((CONV_PAD, LANE), lambda c: (0, c)), pl.BlockSpec((1, LANE), lambda c: (0, c))],
        out_specs=col(0), out_shape=jax.ShapeDtypeStruct((m, d_model), F32),
        scratch_shapes=[pltpu.VMEM((m + CONV_PAD, LANE), F32)], compiler_params=_params(1))(p, p, dww, dwb)


def _conv_bwd(p, dy, dww, d_model):
    m = p.shape[0]
    nch = d_model // LANE
    n_chunk = m // QB
    taps = _conv_taps()
    win_rows = QB + CONV_PAD

    def body(a_ref, g_ref, dy_ref, w_ref, da_ref, dg_ref, dw_ref, db_ref, upad, dypad, wacc, bacc):
        upad[0:CONV_PAD, :] = jnp.zeros((CONV_PAD, LANE), F32)
        dypad[m:m + CONV_PAD, :] = jnp.zeros((CONV_PAD, LANE), F32)
        wacc[...] = jnp.zeros_like(wacc)
        bacc[...] = jnp.zeros_like(bacc)

        def fill(i, c):
            r0 = pl.multiple_of(i * QB, QB)
            u = a_ref[pl.ds(r0, QB), :] * jax.nn.sigmoid(g_ref[pl.ds(r0, QB), :])
            upad[pl.ds(pl.multiple_of(r0 + CONV_PAD, 8), QB), :] = u
            dypad[pl.ds(r0, QB), :] = dy_ref[pl.ds(r0, QB), :]
            return c

        lax.fori_loop(0, n_chunk, fill, 0)

        def chunk(i, c):
            r0 = pl.multiple_of(i * QB, QB)
            dwin = dypad[pl.ds(r0, win_rows), :]
            du = jnp.zeros((QB, LANE), F32)
            rolled = {}
            for b, a, j in taps:
                if b not in rolled:
                    rolled[b] = dwin if b == 0 else pltpu.roll(dwin, win_rows - b, axis=0)
                du = du + w_ref[j:j + 1, :] * rolled[b][8 * a:8 * a + QB, :]
            av = a_ref[pl.ds(r0, QB), :]
            sg = jax.nn.sigmoid(g_ref[pl.ds(r0, QB), :])
            da_ref[pl.ds(r0, QB), :] = (du * sg).astype(da_ref.dtype)
            dg_ref[pl.ds(r0, QB), :] = (du * av * sg * (1.0 - sg)).astype(dg_ref.dtype)
            dyc = dy_ref[pl.ds(r0, QB), :]
            uwin = upad[pl.ds(r0, win_rows), :]
            rolled = {}
            for b, a, j in taps:
                if b not in rolled:
                    rolled[b] = uwin if b == 0 else pltpu.roll(uwin, b, axis=0)
                lo = CONV_PAD - 8 * a
                prod = dyc * rolled[b][lo:lo + QB, :]
                wacc[j] += prod.reshape(QB // 8, 8, LANE).sum(axis=0)
            bacc[...] += dyc.reshape(QB // 8, 8, LANE).sum(axis=0)
            return c

        lax.fori_loop(0, n_chunk, chunk, 0)
        for j in range(CONV_WIDTH):
            dw_ref[j:j + 1, :] = jnp.sum(wacc[j], axis=0, keepdims=True)
        dw_ref[CONV_WIDTH:CONV_PAD, :] = jnp.zeros((CONV_PAD - CONV_WIDTH, LANE), F32)
        db_ref[...] = jnp.sum(bacc[...], axis=0, keepdims=True)

    col = lambda off: pl.BlockSpec((m, LANE), lambda c: (0, off + c))
    return pl.pallas_call(
        body, name="conv_bwd", grid=(nch,),
        in_specs=[col(0), col(nch), col(0), pl.BlockSpec((CONV_PAD, LANE), lambda c: (0, c))],
        out_specs=[col(0), col(0), pl.BlockSpec((CONV_PAD, LANE), lambda c: (0, c)), pl.BlockSpec((1, LANE), lambda c: (0, c))],
        out_shape=(jax.ShapeDtypeStruct((m, d_model), BF16), jax.ShapeDtypeStruct((m, d_model), BF16),
                   jax.ShapeDtypeStruct((CONV_PAD, d_model), F32), jax.ShapeDtypeStruct((1, d_model), F32)),
        scratch_shapes=[pltpu.VMEM((m + CONV_PAD, LANE), F32), pltpu.VMEM((m + CONV_PAD, LANE), F32),
                        pltpu.VMEM((CONV_PAD, 8, LANE), F32), pltpu.VMEM((8, LANE), F32)],
        compiler_params=_params(1))(p, p, dy, dww)


def _softplus(z):
    return jnp.maximum(z, 0.0) + jnp.log(1.0 + jnp.exp(-jnp.abs(z)))


def _softplus_sigmoid(z):
    e = jnp.exp(-jnp.abs(z))
    r = 1.0 / (1.0 + e)
    return jnp.maximum(z, 0.0) - jnp.log(r), jnp.where(z >= 0.0, r, e * r)


def _split_dot(x, tri):
    hi = x.astype(BF16)
    lo = (x - hi.astype(F32)).astype(BF16)
    return jnp.dot(jnp.concatenate([hi, lo], axis=1), tri, preferred_element_type=F32)


def _tri(kind):
    jj = lax.broadcasted_iota(I32, (2 * QB, 2 * QB), 0) & (QB - 1)
    ss = lax.broadcasted_iota(I32, (2 * QB, 2 * QB), 1)
    keep = {"ge": jj >= ss, "lt": jj < ss, "le": jj <= ss}[kind]
    return jnp.where((ss >= QB) | keep, 1.0, 0.0).astype(BF16)


def _attn_fwd(p, d_model):
    m = p.shape[0]
    nqb = m // QB
    npair = d_model // LANE
    qo, ko, vo = 2 * npair, 3 * npair, 4 * npair
    scale = HEAD_DIM ** -0.5

    assert nqb >= KEY_TILES

    def body(q_ref, k_ref, v_ref, o_ref, t_ref, acc_ref, car_ref):
        qb = pl.program_id(1)
        lane = lax.broadcasted_iota(I32, (QB, LANE), 1)
        head0 = lane < HEAD_DIM
        row_g = qb * QB + lax.broadcasted_iota(I32, (QB, KEY_CHUNK), 0)
        col_l = lax.broadcasted_iota(I32, (QB, KEY_CHUNK), 1)
        tri = _tri("ge")
        q2 = (q_ref[...] * scale).astype(BF16)
        zero = jnp.zeros_like(q2)
        qh = (jnp.where(head0, q2, zero), jnp.where(head0, zero, q2))
        acc_ref[...] = jnp.zeros_like(acc_ref)
        car_ref[...] = jnp.zeros_like(car_ref)

        def chunk(first_tile, bound):
            r0 = pl.multiple_of(first_tile * QB, QB)
            kc = k_ref[pl.ds(r0, KEY_CHUNK), :].astype(BF16)
            vc = v_ref[pl.ds(r0, KEY_CHUNK), :].astype(BF16)
            valid = None if bound is None else (col_l + r0) < bound
            for h in range(2):
                z = lax.dot_general(qh[h], kc, NT, preferred_element_type=F32)
                sp = _softplus(z)
                if valid is not None:
                    sp = jnp.where(valid, sp, 0.0)
                car = car_ref[h]
                a_tiles = [None] * KEY_TILES
                for i in reversed(range(KEY_TILES)):
                    cr = _split_dot(sp[:, i * QB:(i + 1) * QB], tri)
                    a_tiles[i] = jnp.exp(z[:, i * QB:(i + 1) * QB] - (cr[:, :QB] + car))
                    car = car + cr[:, QB:]
                a = jnp.concatenate(a_tiles, axis=1)
                if valid is not None:
                    a = jnp.where(valid, a, 0.0)
                acc_ref[h] += jnp.dot(a.astype(BF16), vc, preferred_element_type=F32)
                car_ref[h] = car

        near = jnp.maximum(qb - (KEY_TILES - 1), 0)
        chunk(near, row_g)
        n_full = lax.shift_right_logical(near, KEY_SHIFT)

        def step(i, c):
            chunk(near - KEY_TILES * (i + 1), None)
            return c

        lax.fori_loop(0, n_full, step, 0)
        left = near - KEY_TILES * n_full

        @pl.when(left > 0)
        def _():
            chunk(0, left * QB)

        o_ref[...] = jnp.where(head0, acc_ref[0], acc_ref[1]).astype(o_ref.dtype)
        t_ref[:, :QB] = car_ref[0]
        t_ref[:, QB:] = car_ref[1]

    return pl.pallas_call(
        body, name="attn_fwd", grid=(npair, nqb),
        in_specs=[pl.BlockSpec((QB, LANE), lambda hp, qb: (qb, qo + hp)),
                  pl.BlockSpec((m, LANE), lambda hp, qb: (0, ko + hp)),
                  pl.BlockSpec((m, LANE), lambda hp, qb: (0, vo + hp))],
        out_specs=[pl.BlockSpec((QB, LANE), lambda hp, qb: (qb, hp)),
                   pl.BlockSpec((QB, 2 * LANE), lambda hp, qb: (qb, hp))],
        out_shape=(jax.ShapeDtypeStruct((m, d_model), BF16), jax.ShapeDtypeStruct((m, 2 * d_model), F32)),
        scratch_shapes=[pltpu.VMEM((2, QB, LANE), F32), pltpu.VMEM((2, QB, LANE), F32)],
        compiler_params=_params(2))(p, p, p)


def _attn_bwd(p, d_o, tot, d_model):
    m = p.shape[0]
    nqb = m // QB
    npair = d_model // LANE
    qo, ko, vo = 2 * npair, 3 * npair, 4 * npair
    scale = HEAD_DIM ** -0.5

    assert nqb >= KEY_TILES

    def body(q_ref, k_ref, v_ref, do_ref, t_ref, dq_ref, dk_ref, dv_ref, dkacc, dvacc, dqacc, csp, cg):
        qb = pl.program_id(1)
        lane = lax.broadcasted_iota(I32, (QB, LANE), 1)
        head0 = lane < HEAD_DIM
        head0_keys = lax.broadcasted_iota(I32, (KEY_CHUNK, LANE), 1) < HEAD_DIM
        row_g = qb * QB + lax.broadcasted_iota(I32, (QB, KEY_CHUNK), 0)
        col_l = lax.broadcasted_iota(I32, (QB, KEY_CHUNK), 1)
        tri_lt = _tri("lt")
        tri_le = _tri("le")
        q2 = (q_ref[...] * scale).astype(BF16)
        do2 = do_ref[...]
        zero = jnp.zeros_like(q2)
        qh = (jnp.where(head0, q2, zero), jnp.where(head0, zero, q2))
        doh = (jnp.where(head0, do2, zero), jnp.where(head0, zero, do2))

        @pl.when(qb == 0)
        def _():
            dkacc[...] = jnp.zeros_like(dkacc)
            dvacc[...] = jnp.zeros_like(dvacc)

        dqacc[...] = jnp.zeros_like(dqacc)
        csp[...] = jnp.zeros_like(csp)
        cg[...] = jnp.zeros_like(cg)

        def chunk(first_tile, bound):
            r0 = pl.multiple_of(first_tile * QB, QB)
            kc = k_ref[pl.ds(r0, KEY_CHUNK), :].astype(BF16)
            vc = v_ref[pl.ds(r0, KEY_CHUNK), :].astype(BF16)
            valid = None if bound is None else (col_l + r0) < bound
            dkp, dvp = [], []
            for h in range(2):
                z = lax.dot_general(qh[h], kc, NT, preferred_element_type=F32)
                sp, sg = _softplus_sigmoid(z)
                if valid is not None:
                    sp = jnp.where(valid, sp, 0.0)
                da = lax.dot_general(doh[h], vc, NT, preferred_element_type=F32)
                tot_h = t_ref[:, h * QB:(h + 1) * QB]
                before, g_before = csp[h], cg[h]
                a_tiles, dz_tiles = [], []
                for i in range(KEY_TILES):
                    cols = slice(i * QB, (i + 1) * QB)
                    cr = _split_dot(sp[:, cols], tri_lt)
                    a = jnp.exp(z[:, cols] - (tot_h - (before + cr[:, :QB])))
                    if valid is not None:
                        a = jnp.where(valid[:, cols], a, 0.0)
                    g = a * da[:, cols]
                    gr = _split_dot(g, tri_le)
                    dz = g - sg[:, cols] * (g_before + gr[:, :QB])
                    if valid is not None:
                        dz = jnp.where(valid[:, cols], dz, 0.0)
                    a_tiles.append(a)
                    dz_tiles.append(dz)
                    before = before + cr[:, QB:]
                    g_before = g_before + gr[:, QB:]
                dzb = jnp.concatenate(dz_tiles, axis=1).astype(BF16)
                ab = jnp.concatenate(a_tiles, axis=1).astype(BF16)
                dqacc[h] += jnp.dot(dzb, kc, preferred_element_type=F32)
                dkp.append(lax.dot_general(dzb, q2, TN, preferred_element_type=F32))
                dvp.append(lax.dot_general(ab, do2, TN, preferred_element_type=F32))
                csp[h] = before
                cg[h] = g_before
            dkacc[pl.ds(r0, KEY_CHUNK), :] += jnp.where(head0_keys, dkp[0], dkp[1])
            dvacc[pl.ds(r0, KEY_CHUNK), :] += jnp.where(head0_keys, dvp[0], dvp[1])

        near = jnp.maximum(qb - (KEY_TILES - 1), 0)
        n_full = lax.shift_right_logical(near, KEY_SHIFT)

        def step(i, c):
            chunk(KEY_TILES * i, None)
            return c

        lax.fori_loop(0, n_full, step, 0)

        @pl.when(near > KEY_TILES * n_full)
        def _():
            chunk(KEY_TILES * n_full, near * QB)

        chunk(near, row_g)
        dq_ref[...] = (jnp.where(head0, dqacc[0], dqacc[1]) * scale).astype(dq_ref.dtype)

        @pl.when(qb == nqb - 1)
        def _():
            dk_ref[...] = dkacc[...].astype(dk_ref.dtype)
            dv_ref[...] = dvacc[...].astype(dv_ref.dtype)

    out = jax.ShapeDtypeStruct((m, d_model), BF16)
    return pl.pallas_call(
        body, name="attn_bwd", grid=(npair, nqb),
        in_specs=[pl.BlockSpec((QB, LANE), lambda hp, qb: (qb, qo + hp)),
                  pl.BlockSpec((m, LANE), lambda hp, qb: (0, ko + hp)),
                  pl.BlockSpec((m, LANE), lambda hp, qb: (0, vo + hp)),
                  pl.BlockSpec((QB, LANE), lambda hp, qb: (qb, hp)),
                  pl.BlockSpec((QB, 2 * LANE), lambda hp, qb: (qb, hp))],
        out_specs=[pl.BlockSpec((QB, LANE), lambda hp, qb: (qb, hp)),
                   pl.BlockSpec((m, LANE), lambda hp, qb: (0, hp)),
                   pl.BlockSpec((m, LANE), lambda hp, qb: (0, hp))],
        out_shape=(out, out, out),
        scratch_shapes=[pltpu.VMEM((m, LANE), F32), pltpu.VMEM((m, LANE), F32), pltpu.VMEM((2, QB, LANE), F32),
                        pltpu.VMEM((2, QB, LANE), F32), pltpu.VMEM((2, QB, LANE), F32)],
        compiler_params=_params(2))(p, p, p, d_o, tot)


def _mesh_pos():
    return lax.axis_index("x"), lax.axis_index("y"), lax.axis_index("c")


def _other_chips(x, y):
    return [(1 - x, y), (x, 1 - y), (1 - x, 1 - y)]


def _dev(x, y, c):
    return 4 * x + 2 * y + c


def _all_gather(shards):
    n = len(shards)

    def body(*refs):
        ins, outs = refs[:n], refs[n:2 * n]
        send_sems, recv_sems, local_sems = refs[2 * n:]
        x, y, c = _mesh_pos()
        sibling = (x, y, 1 - c)
        chips = _other_chips(x, y)

        def copy(k, s, src, block, to):
            return pltpu.make_async_remote_copy(src_ref=src, dst_ref=outs[k].at[_dev(*block)], send_sem=send_sems.at[k, s],
                                                recv_sem=recv_sems.at[k, s], device_id=to, device_id_type=MESH)

        def held(k, block):
            return outs[k].at[_dev(*block)]

        mine = [pltpu.make_async_copy(ins[k], held(k, (x, y, c)), local_sems.at[k]) for k in range(n)]
        for cp in mine:
            cp.start()
        first = []
        for k in range(n):
            first.append(copy(k, 0, ins[k], (x, y, c), sibling))
            for j, chip in enumerate(chips):
                first.append(copy(k, 1 + j, ins[k], (x, y, c), (*chip, c)))
        for cp in first:
            cp.start()
        passed = []
        for j, chip in enumerate(chips):
            for k in range(n):
                copy(k, 1 + j, held(k, (*chip, c)), (*chip, c), (x, y, c)).wait_recv()
                fwd = copy(k, 4 + j, held(k, (*chip, c)), (*chip, c), sibling)
                fwd.start()
                passed.append(fwd)
        for k in range(n):
            copy(k, 0, held(k, (x, y, 1 - c)), (x, y, 1 - c), (x, y, c)).wait_recv()
            for j, chip in enumerate(chips):
                copy(k, 4 + j, held(k, (*chip, 1 - c)), (*chip, 1 - c), (x, y, c)).wait_recv()
        for cp in first + passed:
            cp.wait_send()
        for cp in mine:
            cp.wait()

    return pl.pallas_call(
        body, name="comm_all_gather", in_specs=[ANY] * n, out_specs=[ANY] * n,
        out_shape=[jax.ShapeDtypeStruct((N_DEV,) + s.shape, s.dtype) for s in shards],
        scratch_shapes=[pltpu.SemaphoreType.DMA((n, 7)), pltpu.SemaphoreType.DMA((n, 7)), pltpu.SemaphoreType.DMA((n,))],
    )(*shards)


def _exchange_sibling(grads):
    n = len(grads)

    def body(*refs):
        ins, outs = refs[:n], refs[n:2 * n]
        send_sems, recv_sems = refs[2 * n:]
        x, y, c = _mesh_pos()
        sibling = (x, y, 1 - c)
        chips = _other_chips(x, y) + [(x, y)]
        copies = []
        for k in range(n):
            for r, chip in enumerate(chips):
                copies.append(pltpu.make_async_remote_copy(
                    src_ref=ins[k].at[_dev(*chip, 1 - c)], dst_ref=outs[k].at[r], send_sem=send_sems.at[k, r],
                    recv_sem=recv_sems.at[k, r], device_id=sibling, device_id_type=MESH))
        for cp in copies:
            cp.start()
        for cp in copies:
            cp.wait_recv()
        for cp in copies:
            cp.wait_send()

    return pl.pallas_call(
        body, name="comm_rs_sibling", in_specs=[ANY] * n, out_specs=[ANY] * n,
        out_shape=[jax.ShapeDtypeStruct((4,) + g.shape[1:], g.dtype) for g in grads],
        scratch_shapes=[pltpu.SemaphoreType.DMA((n, 4)), pltpu.SemaphoreType.DMA((n, 4))],
    )(*grads)


def _exchange_chips(parts):
    n = len(parts)

    def body(*refs):
        ins, outs = refs[:n], refs[n:2 * n]
        send_sems, recv_sems = refs[2 * n:]
        x, y, c = _mesh_pos()
        copies = []
        for k in range(n):
            for r, chip in enumerate(_other_chips(x, y)):
                copies.append(pltpu.make_async_remote_copy(
                    src_ref=ins[k].at[r], dst_ref=outs[k].at[r], send_sem=send_sems.at[k, r],
                    recv_sem=recv_sems.at[k, r], device_id=(*chip, c), device_id_type=MESH))
        for cp in copies:
            cp.start()
        for cp in copies:
            cp.wait_recv()
        for cp in copies:
            cp.wait_send()

    return pl.pallas_call(
        body, name="comm_rs_chips", in_specs=[ANY] * n, out_specs=[ANY] * n,
        out_shape=[jax.ShapeDtypeStruct(a.shape, a.dtype) for a in parts],
        scratch_shapes=[pltpu.SemaphoreType.DMA((n, 3)), pltpu.SemaphoreType.DMA((n, 3))],
    )(*parts)


def _shard_tile(rows):
    for tr in range(min(rows, 352), 0, -1):
        if rows % tr == 0 and (tr % 16 == 0 or tr == rows):
            return tr


def _pair_sum(slab_idx, grad, from_sibling):
    _, rows, cols = grad.shape
    tr = _shard_tile(rows)

    def body(idx_ref, g_ref, s_ref, o_ref):
        o_ref[...] = (g_ref[...] + s_ref[...]).astype(o_ref.dtype)

    gs = pltpu.PrefetchScalarGridSpec(
        num_scalar_prefetch=1, grid=(3, rows // tr),
        in_specs=[pl.BlockSpec((None, tr, cols), lambda r, i, idx: (idx[r], i, 0)),
                  pl.BlockSpec((None, tr, cols), lambda r, i, idx: (r, i, 0))],
        out_specs=pl.BlockSpec((None, tr, cols), lambda r, i, idx: (r, i, 0)))
    return pl.pallas_call(body, name="rs_pair_sum", grid_spec=gs, out_shape=jax.ShapeDtypeStruct((3, rows, cols), BF16),
                          compiler_params=_params(2))(slab_idx, grad, from_sibling)


def _adamw_math(w, g, m, v):
    m = ADAM_B1 * m + (1.0 - ADAM_B1) * g
    v = ADAM_B2 * v + (1.0 - ADAM_B2) * (g * g)
    m_hat = m / (1.0 - ADAM_B1 ** ADAM_STEP)
    v_hat = v / (1.0 - ADAM_B2 ** ADAM_STEP)
    delta = -ADAM_LR * (m_hat / (jnp.sqrt(v_hat) + ADAM_EPS) + ADAM_WD * w)
    return delta, m, v


def _adamw_shard(me, grad, from_sibling, from_chips, w, m, v):
    rows, cols = w.shape
    tr = _shard_tile(rows)

    def body(me_ref, g_ref, s_ref, c_ref, w_ref, m_ref, v_ref, go_ref, do_ref, mo_ref, vo_ref):
        g = g_ref[...] + s_ref[...]
        for r in range(3):
            g = g + c_ref[r].astype(F32)
        delta, m_new, v_new = _adamw_math(w_ref[...], g, m_ref[...], v_ref[...])
        go_ref[...] = g
        do_ref[...] = delta
        mo_ref[...] = m_new
        vo_ref[...] = v_new

    flat = pl.BlockSpec((tr, cols), lambda i, me: (i, 0))
    gs = pltpu.PrefetchScalarGridSpec(
        num_scalar_prefetch=1, grid=(rows // tr,),
        in_specs=[pl.BlockSpec((None, tr, cols), lambda i, me: (me[0], i, 0)),
                  pl.BlockSpec((None, tr, cols), lambda i, me: (3, i, 0)),
                  pl.BlockSpec((3, tr, cols), lambda i, me: (0, i, 0)), flat, flat, flat],
        out_specs=[flat, flat, flat, flat])
    out = jax.ShapeDtypeStruct((rows, cols), F32)
    return pl.pallas_call(body, name="adamw_shard", grid_spec=gs, out_shape=(out, out, out, out),
                          compiler_params=_params(1))(me, grad, from_sibling, from_chips, w, m, v)


def _small_reduce_adamw(slabs, w, m, v):
    _, rows, _ = slabs.shape

    def body(s_ref, w_ref, m_ref, v_ref, g_ref, d_ref, mo_ref, vo_ref, land, send_sems, recv_sems):
        x, y, c = _mesh_pos()
        me = _dev(x, y, c)
        copies = []
        for mask in range(1, N_DEV):
            px, py, pc = x ^ (mask >> 2), y ^ ((mask >> 1) & 1), c ^ (mask & 1)
            copies.append(pltpu.make_async_remote_copy(
                src_ref=s_ref.at[_dev(px, py, pc)], dst_ref=land.at[me], send_sem=send_sems.at[mask - 1],
                recv_sem=recv_sems.at[mask - 1], device_id=(px, py, pc), device_id_type=MESH))
        for cp in copies:
            cp.start()
        land[me] = s_ref[me]
        for mask in range(1, N_DEV):
            px, py, pc = x ^ (mask >> 2), y ^ ((mask >> 1) & 1), c ^ (mask & 1)
            pltpu.make_async_remote_copy(
                src_ref=s_ref.at[me], dst_ref=land.at[_dev(px, py, pc)], send_sem=send_sems.at[mask - 1],
                recv_sem=recv_sems.at[mask - 1], device_id=(px, py, pc), device_id_type=MESH).wait_recv()
        for cp in copies:
            cp.wait_send()
        g = land[0]
        for d in range(1, N_DEV):
            g = g + land[d]
        delta, m_new, v_new = _adamw_math(w_ref[...], g, m_ref[...], v_ref[...])
        g_ref[...] = g
        d_ref[...] = delta
        mo_ref[...] = m_new
        vo_ref[...] = v_new

    out = jax.ShapeDtypeStruct((rows, LANE), F32)
    return pl.pallas_call(
        body, name="comm_small_reduce_adamw", in_specs=[VMEM_WHOLE] * 4, out_specs=[VMEM_WHOLE] * 4, out_shape=(out, out, out, out),
        scratch_shapes=[pltpu.VMEM((N_DEV, rows, LANE), F32), pltpu.SemaphoreType.DMA((N_DEV - 1,)),
                        pltpu.SemaphoreType.DMA((N_DEV - 1,))],
    )(slabs, w, m, v)


def _cast_bf16(arrs):
    n = len(arrs)

    def body(*refs):
        for i_ref, o_ref in zip(refs[:n], refs[n:]):
            o_ref[...] = i_ref[...].astype(BF16)

    return pl.pallas_call(body, name="cast_bf16", in_specs=[VMEM_WHOLE] * n, out_specs=[VMEM_WHOLE] * n,
                          out_shape=[jax.ShapeDtypeStruct(a.shape, BF16) for a in arrs],
                          compiler_params=pltpu.CompilerParams(vmem_limit_bytes=VMEM_LIMIT))(*arrs)


REPLICATED = ("pre_mix_g", "gate_b", "dw_b", "conv_ln_g", "conv_ln_b", "post_mix_g", "pre_ffn_g", "post_ffn_g")
SHARDED = ("w_in", "w_conv_out", "w_attn_out", "w_o", "w_ffn_in", "w_ffn_out")
WEIGHTS = ("meta_tokens", "pre_mix_g", "w_in", "gate_b", "dw_w", "dw_b", "conv_ln_g", "conv_ln_b", "w_conv_out",
           "w_attn_out", "w_o", "post_mix_g", "pre_ffn_g", "w_ffn_in", "w_ffn_out", "post_ffn_g")


def kernel(x, meta_tokens, pre_mix_g, w_in, gate_b, dw_w, dw_b, conv_ln_g, conv_ln_b, w_conv_out, w_attn_out, w_o, post_mix_g, pre_ffn_g, w_ffn_in, w_ffn_out, post_ffn_g, loss_target, m_meta_tokens, m_pre_mix_g, m_w_in, m_gate_b, m_dw_w, m_dw_b, m_conv_ln_g, m_conv_ln_b, m_w_conv_out, m_w_attn_out, m_w_o, m_post_mix_g, m_pre_ffn_g, m_w_ffn_in, m_w_ffn_out, m_post_ffn_g, v_meta_tokens, v_pre_mix_g, v_w_in, v_gate_b, v_dw_w, v_dw_b, v_conv_ln_g, v_conv_ln_b, v_w_conv_out, v_w_attn_out, v_w_o, v_post_mix_g, v_pre_ffn_g, v_w_ffn_in, v_w_ffn_out, v_post_ffn_g):
    given = dict(locals())
    seq, d = x.shape[1], x.shape[2]
    n_meta = meta_tokens.shape[0]
    length = n_meta + seq
    m_rows = -(-length // QB) * QB
    dc = d // N_DEV
    assert dc == LANE and n_meta % 8 == 0 and seq % 8 == 0
    fs = w_ffn_in.shape[2]
    fr = w_ffn_out.shape[1]
    assert 2 * fr == fs

    local = {k: given[k][0] for k in SHARDED}
    cast = _cast_bf16([local[k] for k in SHARDED])
    dww_pad = jnp.pad(dw_w[0], ((0, CONV_PAD - CONV_WIDTH), (0, 0)))
    gathered = _all_gather(list(cast) + [meta_tokens, dww_pad])
    wi = gathered[0]
    wco, wao, wo = (g.reshape(d, d) for g in gathered[1:4])
    wfi = gathered[4]
    wfo = gathered[5].reshape(N_DEV // 2, fs, d)
    meta_full = jnp.concatenate([gathered[6][j] for j in range(N_DEV)], axis=1)
    dww_full = jnp.concatenate([gathered[7][j] for j in range(N_DEV)], axis=1)
    ns = wi.shape[2]

    tail = jnp.zeros((m_rows - length, d), F32)
    h0 = jnp.concatenate([meta_full, x[0], tail], axis=0)
    target = jnp.concatenate([jnp.zeros((n_meta, d), F32), loss_target[0], tail], axis=0)

    (u,) = _rows("pre_mix_norm", lambda r0, xs, ps: ([_rms(xs[0], ps[0])], []), [h0], [pre_mix_g], [BF16], [])
    p = _matmul("in_proj", NN, u, wi, pl.BlockSpec((m_rows, d), lambda i: (0, 0)), pl.BlockSpec((None, d, ns), lambda i: (i, 0, 0)),
                pl.BlockSpec((m_rows, ns), lambda i: (0, i)), jax.ShapeDtypeStruct((m_rows, N_DEV * ns), F32), (N_DEV,))
    y = _conv_fwd(p, dww_full, dw_b, d)
    (yc,) = _rows("conv_norm", lambda r0, xs, ps: ([_ln_silu(xs[0], ps[0], ps[1])], []), [y], [conv_ln_g, conv_ln_b], [BF16], [])
    y_conv = _dense_fwd("conv_out", yc, wco)
    o, tot = _attn_fwd(p, d)
    y_attn = _dense_fwd("attn_out", o, wao)
    gate_cols = [(d, 5), (d, 6), None, None]
    (mixin,) = _rows("gate_mix", lambda r0, xs, ps: ([_gate_mix(*xs, ps[0])], []), [p, p, y_conv, y_attn], [gate_b], [BF16], [],
                     row_in_cols=gate_cols, row_out_widths=[d])
    mix = _dense_fwd("mix_out", mixin, wo)
    h1, u2 = _rows("post_mix", lambda r0, xs, ps: (list(_post_mix(xs[0], xs[1], ps[0], ps[1])), []), [h0, mix],
                   [post_mix_g, pre_ffn_g], [F32, BF16], [])
    ab = _matmul("ffn_in", NN, u2, wfi, pl.BlockSpec((m_rows, d), lambda i: (0, 0)), pl.BlockSpec((None, d, fs), lambda i: (i, 0, 0)),
                 pl.BlockSpec((None, m_rows, fs), lambda i: (i, 0, 0)), jax.ShapeDtypeStruct((N_DEV, m_rows, fs), F32), (N_DEV,))
    half = N_DEV // 2
    tm = _row_tile(m_rows)
    pair = lambda off: pl.BlockSpec((None, tm, fs), lambda j, i, off=off: (j + off, i, 0))
    (f_in,) = _rowwise("swiglu", lambda r0, xs, ps: ([_swiglu(xs[0], xs[1])], []), [ab, ab], [],
                       [jax.ShapeDtypeStruct((half, m_rows, fs), BF16)], [], grid=(half, m_rows // tm),
                       in_specs=[pair(0), pair(half)], out_specs=[pair(0)], tm=tm, row_axis=1)
    f = _matmul("ffn_out", NN, f_in, wfo, pl.BlockSpec((None, m_rows, fs), lambda j: (j, 0, 0)), pl.BlockSpec((None, fs, d), lambda j: (j, 0, 0)),
                pl.BlockSpec((m_rows, d), lambda j: (0, 0)), jax.ShapeDtypeStruct((m_rows, d), F32), (half,), acc_axis=0)

    def loss_head(r0, xs, ps):
        h1_, f_, t_ = xs
        r, vjp = jax.vjp(_rms, f_, ps[0])
        rows = r0 + lax.broadcasted_iota(I32, (ROW_CHUNK, 1), 0)
        real = (rows >= n_meta) & (rows < length)
        err = jnp.where(real, h1_ + r - t_, 0.0)
        dh2 = err * (1.0 / d)
        d_f, dg = vjp(dh2)
        part = jnp.sum(0.5 * jnp.mean(err * err, axis=-1, keepdims=True), axis=0, keepdims=True)
        return [d_f, dh2], [dg, jnp.broadcast_to(part, (1, LANE))]

    d_f, dh2, g_post_ffn, loss_part = _rows("loss_head", loss_head, [h1, f, target], [post_ffn_g], [BF16, F32], [d, LANE])

    d_fin = _matmul("ffn_out_dx", NT, d_f, wfo, pl.BlockSpec((m_rows, d), lambda j: (0, 0)), pl.BlockSpec((None, fs, d), lambda j: (j, 0, 0)),
                    pl.BlockSpec((None, m_rows, fs), lambda j: (j, 0, 0)), jax.ShapeDtypeStruct((half, m_rows, fs), F32), (half,))
    g_wfo = _matmul("ffn_out_dw", TN, f_in, d_f, pl.BlockSpec((None, m_rows, fs), lambda j: (j, 0, 0)), pl.BlockSpec((m_rows, d), lambda j: (0, 0)),
                    pl.BlockSpec((None, fs, d), lambda j: (j, 0, 0)), jax.ShapeDtypeStruct((half, fs, d), F32), (half,))

    def swiglu_bwd(r0, xs, ps):
        _, vjp = jax.vjp(_swiglu, xs[0], xs[1])
        return list(vjp(xs[2])), []

    out8 = lambda off: pl.BlockSpec((None, tm, fs), lambda j, i, off=off: (j + off, i, 0))
    d_a, d_b = _rowwise("swiglu_bwd", swiglu_bwd, [ab, ab, d_fin], [],
                        [jax.ShapeDtypeStruct((half, m_rows, fs), BF16)] * 2, [], grid=(half, m_rows // tm),
                        in_specs=[pair(0), pair(half), pair(0)], out_specs=[out8(0), out8(0)], tm=tm, row_axis=1)
    d_ab = jnp.concatenate([d_a, d_b], axis=0)
    du2 = _matmul("ffn_in_dx", NT, d_ab, wfi, pl.BlockSpec((None, m_rows, fs), lambda i: (i, 0, 0)), pl.BlockSpec((None, d, fs), lambda i: (i, 0, 0)),
                  pl.BlockSpec((m_rows, d), lambda i: (0, 0)), jax.ShapeDtypeStruct((m_rows, d), F32), (N_DEV,), acc_axis=0)
    g_wfi = _matmul("ffn_in_dw", TN, u2, d_ab, pl.BlockSpec((m_rows, d), lambda i: (0, 0)), pl.BlockSpec((None, m_rows, fs), lambda i: (i, 0, 0)),
                    pl.BlockSpec((None, d, fs), lambda i: (i, 0, 0)), jax.ShapeDtypeStruct((N_DEV, d, fs), F32), (N_DEV,))

    def post_mix_bwd(r0, xs, ps):
        h0_, mix_, dh2_, du2_ = xs
        _, vjp = jax.vjp(_post_mix, h0_, mix_, ps[0], ps[1])
        dh0_, dmix_, dg1, dg2 = vjp((dh2_, du2_))
        return [dmix_, dh0_], [dg1, dg2]

    d_mix, dh1, g_post_mix, g_pre_ffn = _rows("post_mix_bwd", post_mix_bwd, [h0, mix, dh2, du2], [post_mix_g, pre_ffn_g],
                                              [BF16, F32], [d, d])
    d_mixin = _dense_dx("mix_out_dx", d_mix, wo, F32)
    g_wo = _dense_dw("mix_out_dw", mixin, d_mix)

    def gate_mix_bwd(r0, xs, ps):
        _, vjp = jax.vjp(_gate_mix, xs[0], xs[1], xs[2], xs[3], ps[0])
        dpgc, dpga, dyc_, dya_, dgb = vjp(xs[4])
        return [dpgc, dpga, dyc_, dya_], [dgb]

    dp_gc, dp_ga, d_yconv, d_yattn, g_gate_b = _rows(
        "gate_mix_bwd", gate_mix_bwd, [p, p, y_conv, y_attn, d_mixin], [gate_b], [BF16] * 4, [2 * d],
        row_in_cols=gate_cols + [None], row_out_widths=[d] * 4)
    d_o = _dense_dx("attn_out_dx", d_yattn, wao, BF16)
    g_wao = _dense_dw("attn_out_dw", o, d_yattn)
    dq, dk, dv = _attn_bwd(p, d_o, tot, d)
    d_yc = _dense_dx("conv_out_dx", d_yconv, wco, F32)
    g_wco = _dense_dw("conv_out_dw", yc, d_yconv)

    def conv_norm_bwd(r0, xs, ps):
        _, vjp = jax.vjp(_ln_silu, xs[0], ps[0], ps[1])
        dy_, dg, db = vjp(xs[1])
        return [dy_], [dg, db]

    d_y, g_ln_g, g_ln_b = _rows("conv_norm_bwd", conv_norm_bwd, [y, d_yc], [conv_ln_g, conv_ln_b], [F32], [d, d])
    dp_a, dp_g, g_dww, g_dwb = _conv_bwd(p, d_y, dww_full, d)
    dp = jnp.concatenate([dp_a, dp_g, dq, dk, dv, dp_gc, dp_ga], axis=1)
    du = _matmul("in_proj_dx", NT, dp, wi, pl.BlockSpec((m_rows, ns), lambda i: (0, i)), pl.BlockSpec((None, d, ns), lambda i: (i, 0, 0)),
                 pl.BlockSpec((m_rows, d), lambda i: (0, 0)), jax.ShapeDtypeStruct((m_rows, d), F32), (N_DEV,), acc_axis=0)
    g_wi = _matmul("in_proj_dw", TN, u, dp, pl.BlockSpec((m_rows, d), lambda i: (0, 0)), pl.BlockSpec((m_rows, ns), lambda i: (0, i)),
                   pl.BlockSpec((None, d, ns), lambda i: (i, 0, 0)), jax.ShapeDtypeStruct((N_DEV, d, ns), F32), (N_DEV,))

    def pre_mix_bwd(r0, xs, ps):
        _, vjp = jax.vjp(_rms, xs[0], ps[0])
        dx, dg = vjp(xs[1])
        return [xs[2] + dx], [dg]

    dh0, g_pre_mix = _rows("pre_mix_bwd", pre_mix_bwd, [h0, du, dh1], [pre_mix_g], [F32], [d])
    grad_x = dh0[n_meta:length][None]

    x_i, y_i, c_i = _mesh_pos()
    me = _dev(x_i, y_i, c_i)
    me_arr = jnp.reshape(me, (1,)).astype(I32)
    slab_idx = jnp.stack([_dev(*chip, c_i) for chip in _other_chips(x_i, y_i)]).astype(I32)
    big = {"w_in": g_wi, "w_conv_out": g_wco.reshape(N_DEV, dc, d), "w_attn_out": g_wao.reshape(N_DEV, dc, d),
           "w_o": g_wo.reshape(N_DEV, dc, d), "w_ffn_in": g_wfi, "w_ffn_out": g_wfo.reshape(N_DEV, fr, d)}
    from_sibling = _exchange_sibling([big[k] for k in SHARDED])
    partial = [_pair_sum(slab_idx, big[k], s) for k, s in zip(SHARDED, from_sibling)]
    from_chips = _exchange_chips(partial)
    results = {}
    for k, s, ch in zip(SHARDED, from_sibling, from_chips):
        outs = _adamw_shard(me_arr, big[k], s, ch, local[k], given["m_" + k][0], given["v_" + k][0])
        results[k] = tuple(a[None] for a in outs)

    rep_grads = {"pre_mix_g": g_pre_mix, "gate_b": g_gate_b, "dw_b": g_dwb, "conv_ln_g": g_ln_g, "conv_ln_b": g_ln_b,
                 "post_mix_g": g_post_mix, "pre_ffn_g": g_pre_ffn, "post_ffn_g": g_post_ffn}

    def pack_rep(get):
        return jnp.concatenate([get(k) for k in REPLICATED], axis=1).reshape(-1, LANE)

    rep_rows = pack_rep(lambda k: rep_grads[k])
    n_rep = rep_rows.shape[0]
    loss_rows = jnp.broadcast_to(loss_part, (8, LANE))
    g_meta = dh0[0:n_meta]
    slabs = jnp.stack([jnp.concatenate([rep_rows, loss_rows, g_dww[:, j * LANE:(j + 1) * LANE], g_meta[:, j * LANE:(j + 1) * LANE]], axis=0)
                       for j in range(N_DEV)])

    def pack_small(prefix):
        dww_own = jnp.pad(given[prefix + "dw_w"][0], ((0, CONV_PAD - CONV_WIDTH), (0, 0)))
        return jnp.concatenate([pack_rep(lambda k: given[prefix + k]), jnp.zeros((8, LANE), F32), dww_own,
                                given[prefix + "meta_tokens"]], axis=0)

    small = _small_reduce_adamw(slabs, pack_small(""), pack_small("m_"), pack_small("v_"))
    loss = small[0][n_rep, 0]

    def unpack(arr):
        out = {}
        flat = arr[:n_rep].reshape(1, -1)
        off = 0
        for k in REPLICATED:
            w = given[k].shape[1]
            out[k] = flat[:, off:off + w]
            off += w
        out["dw_w"] = arr[n_rep + 8:n_rep + 8 + CONV_WIDTH][None]
        out["meta_tokens"] = arr[n_rep + 8 + CONV_PAD:n_rep + 8 + CONV_PAD + n_meta]
        return out

    small_out = [unpack(a) for a in small]
    for k in WEIGHTS:
        if k not in results:
            results[k] = tuple(s[k] for s in small_out)
    return (loss, grad_x, *[results[k][0] for k in WEIGHTS], *[results[k][1] for k in WEIGHTS],
            *[results[k][2] for k in WEIGHTS], *[results[k][3] for k in WEIGHTS])
```

```python
import jax
import jax.numpy as jnp
from jax import lax
from jax.experimental import pallas as pl
from jax.experimental.pallas import tpu as pltpu

F32 = jnp.float32
BF16 = jnp.bfloat16
I32 = jnp.int32

N_DEV = 8
LANE = 128
HEAD_DIM = 64
QB = 128
KEY_SHIFT = 2
KEY_TILES = 1 << KEY_SHIFT
KEY_CHUNK = KEY_TILES * QB
CONV_WIDTH = 31
CONV_PAD = 32
ROW_CHUNK = 128
RMS_EPS = 1e-6
LN_EPS = 1e-5
ADAM_LR = 0.001
ADAM_B1 = 0.9
ADAM_B2 = 0.999
ADAM_EPS = 1e-08
ADAM_WD = 0.01
ADAM_STEP = 10
VMEM_LIMIT = 56 * 1024 * 1024

NN = (((1,), (0,)), ((), ()))
NT = (((1,), (1,)), ((), ()))
TN = (((0,), (0,)), ((), ()))
MESH = pl.DeviceIdType.MESH
ANY = pl.BlockSpec(memory_space=pl.ANY)
VMEM_WHOLE = pl.BlockSpec(memory_space=pltpu.VMEM)


def _params(n_axes):
    return pltpu.CompilerParams(dimension_semantics=("arbitrary",) * n_axes, vmem_limit_bytes=VMEM_LIMIT)


def _row_tile(m):
    assert m % QB == 0
    return m // 4 if m % 64 == 0 else QB


def _matmul(name, dims, a, b, a_spec, b_spec, o_spec, out_shape, grid, acc_axis=None, twin_bf16=False):
    def body(a_ref, b_ref, o_ref, *twin):
        r = lax.dot_general(a_ref[...], b_ref[...], dims, preferred_element_type=F32)
        if acc_axis is None:
            o_ref[...] = r.astype(o_ref.dtype)
            for t_ref in twin:
                t_ref[...] = r.astype(t_ref.dtype)
        else:
            k = pl.program_id(acc_axis)

            @pl.when(k == 0)
            def _():
                o_ref[...] = r

            @pl.when(k > 0)
            def _():
                o_ref[...] += r

    if twin_bf16:
        assert acc_axis is None
        o_spec = [o_spec, o_spec]
        out_shape = [out_shape, jax.ShapeDtypeStruct(out_shape.shape, BF16)]
    return pl.pallas_call(body, name=name, grid=grid, in_specs=[a_spec, b_spec], out_specs=o_spec,
                          out_shape=out_shape, compiler_params=_params(len(grid)))(a, b)


def _dense_fwd(name, a, w, out_dtype=F32):
    m, k = a.shape
    n = w.shape[1]
    tn = 512
    return _matmul(name, NN, a, w, pl.BlockSpec((m, k), lambda j: (0, 0)), pl.BlockSpec((k, tn), lambda j: (0, j)),
                   pl.BlockSpec((m, tn), lambda j: (0, j)), jax.ShapeDtypeStruct((m, n), out_dtype), (n // tn,))


def _dense_dx(name, dy, w, out_dtype):
    m, n = dy.shape
    k = w.shape[0]
    tk = 512
    return _matmul(name, NT, dy, w, pl.BlockSpec((m, n), lambda j: (0, 0)), pl.BlockSpec((tk, n), lambda j: (j, 0)),
                   pl.BlockSpec((m, tk), lambda j: (0, j)), jax.ShapeDtypeStruct((m, k), out_dtype), (k // tk,))


def _dense_dw(name, a, dy):
    m, k = a.shape
    n = dy.shape[1]
    tn = 512
    return _matmul(name, TN, a, dy, pl.BlockSpec((m, k), lambda j: (0, 0)), pl.BlockSpec((m, tn), lambda j: (0, j)),
                   pl.BlockSpec((k, tn), lambda j: (0, j)), jax.ShapeDtypeStruct((k, n), F32), (n // tn,), twin_bf16=True)


def _rowwise(name, fn, row_ins, par_ins, row_outs, par_outs, *, grid, in_specs, out_specs, tm, row_axis):
    n_ri, n_pi, n_ro, n_po = len(row_ins), len(par_ins), len(row_outs), len(par_outs)
    n_steps, tail = divmod(tm, ROW_CHUNK)
    assert tail % 16 == 0

    def body(*refs):
        ri = refs[:n_ri]
        pi = refs[n_ri:n_ri + n_pi]
        ro = refs[n_ri + n_pi:n_ri + n_pi + n_ro]
        po = refs[n_ri + n_pi + n_ro:]
        ps = [r[...] for r in pi]
        base = pl.program_id(row_axis) * tm

        def chunk(r0, rows, carry):
            xs = [r[pl.ds(r0, rows), :] for r in ri]
            outs, pouts = fn(base + r0, xs, ps)
            for r, o in zip(ro, outs):
                r[pl.ds(r0, rows), :] = o.astype(r.dtype)
            return tuple(c + q for c, q in zip(carry, pouts))

        def step(i, carry):
            return chunk(pl.multiple_of(i * ROW_CHUNK, ROW_CHUNK), ROW_CHUNK, carry)

        acc = lax.fori_loop(0, n_steps, step, tuple(jnp.zeros(s.shape, F32) for s in par_outs))
        if tail:
            acc = chunk(n_steps * ROW_CHUNK, tail, acc)
        if n_po:
            first = pl.program_id(0) == 0
            for ax in range(1, len(grid)):
                first = first & (pl.program_id(ax) == 0)

            @pl.when(first)
            def _():
                for r in po:
                    r[...] = jnp.zeros_like(r)

            for r, a in zip(po, acc):
                r[...] += a

    return pl.pallas_call(body, name=name, grid=grid, in_specs=in_specs, out_specs=out_specs,
                          out_shape=tuple(row_outs) + tuple(par_outs),
                          compiler_params=_params(len(grid)))(*row_ins, *par_ins)


def _rows(name, fn, row_ins, par_ins, row_out_dtypes, par_out_widths, row_in_cols=None, row_out_widths=None):
    m = row_ins[0].shape[0]
    tm = _row_tile(m)
    in_specs = []
    for k, a in enumerate(row_ins):
        if row_in_cols is not None and row_in_cols[k] is not None:
            width, cb = row_in_cols[k]
            in_specs.append(pl.BlockSpec((tm, width), lambda i, cb=cb: (i, cb)))
        else:
            in_specs.append(pl.BlockSpec((tm, a.shape[1]), lambda i: (i, 0)))
    for a in par_ins:
        in_specs.append(pl.BlockSpec(a.shape, lambda i: (0, 0)))
    if row_out_widths is None:
        row_out_widths = [row_ins[0].shape[1]] * len(row_out_dtypes)
    row_outs = [jax.ShapeDtypeStruct((m, w), dt) for w, dt in zip(row_out_widths, row_out_dtypes)]
    par_outs = [jax.ShapeDtypeStruct((1, w), F32) for w in par_out_widths]
    out_specs = [pl.BlockSpec((tm, s.shape[1]), lambda i: (i, 0)) for s in row_outs]
    out_specs += [pl.BlockSpec(s.shape, lambda i: (0, 0)) for s in par_outs]
    return _rowwise(name, fn, row_ins, par_ins, row_outs, par_outs, grid=(m // tm,), in_specs=in_specs,
                    out_specs=out_specs, tm=tm, row_axis=0)


def _rms(x, g):
    return x * lax.rsqrt(jnp.mean(x * x, axis=-1, keepdims=True) + RMS_EPS) * g


def _ln_silu(y, g, b):
    mu = jnp.mean(y, axis=-1, keepdims=True)
    yc = y - mu
    var = jnp.mean(yc * yc, axis=-1, keepdims=True)
    return jax.nn.silu(yc * lax.rsqrt(var + LN_EPS) * g + b)


def _gate_mix(pgc, pga, yc, ya, gb):
    d = pgc.shape[1]
    return jax.nn.sigmoid(pgc + gb[:, :d]) * yc + jax.nn.sigmoid(pga + gb[:, d:]) * ya


def _post_mix(h0, mix, g_post, g_pre):
    h1 = h0 + _rms(mix, g_post)
    return h1, _rms(h1, g_pre)


def _swiglu(a, b):
    return jax.nn.silu(a) * b


def _conv_taps():
    taps = []
    for b in range(8):
        for a in range(CONV_PAD // 8):
            s = 8 * a + b
            if s < CONV_WIDTH:
                taps.append((b, a, CONV_WIDTH - 1 - s))
    return taps


def _conv_fwd(p, dww, dwb, d_model):
    m = p.shape[0]
    nch = d_model // LANE
    n_chunk = m // QB
    taps = _conv_taps()

    def body(a_ref, g_ref, w_ref, b_ref, y_ref, upad):
        upad[0:CONV_PAD, :] = jnp.zeros((CONV_PAD, LANE), F32)

        def fill(i, c):
            r0 = pl.multiple_of(i * QB, QB)
            u = a_ref[pl.ds(r0, QB), :] * jax.nn.sigmoid(g_ref[pl.ds(r0, QB), :])
            upad[pl.ds(pl.multiple_of(r0 + CONV_PAD, 8), QB), :] = u
            return c

        lax.fori_loop(0, n_chunk, fill, 0)

        def conv(i, c):
            r0 = pl.multiple_of(i * QB, QB)
            win = upad[pl.ds(r0, QB + CONV_PAD), :]
            acc = jnp.broadcast_to(b_ref[...], (QB, LANE))
            rolled = {}
            for b, a, j in taps:
                if b not in rolled:
                    rolled[b] = win if b == 0 else pltpu.roll(win, b, axis=0)
                lo = CONV_PAD - 8 * a
                acc = acc + w_ref[j:j + 1, :] * rolled[b][lo:lo + QB, :]
            y_ref[pl.ds(r0, QB), :] = acc
            return c

        lax.fori_loop(0, n_chunk, conv, 0)

    col = lambda off: pl.BlockSpec((m, LANE), lambda c: (0, off + c))
    return pl.pallas_call(
        body, name="conv_fwd", grid=(nch,),
        in_specs=[col(0), col(nch), pl.BlockSpec((CONV_PAD, LANE), lambda c: (0, c)), pl.BlockSpec((1, LANE), lambda c: (0, c))],
        out_specs=col(0), out_shape=jax.ShapeDtypeStruct((m, d_model), F32),
        scratch_shapes=[pltpu.VMEM((m + CONV_PAD, LANE), F32)], compiler_params=_params(1))(p, p, dww, dwb)


def _conv_bwd(p, dy, dww, d_model):
    m = p.shape[0]
    nch = d_model // LANE
    n_chunk = m // QB
    taps = _conv_taps()
    win_rows = QB + CONV_PAD

    def body(a_ref, g_ref, dy_ref, w_ref, da_ref, dg_ref, dw_ref, db_ref, upad, dypad, wacc, bacc):
        upad[0:CONV_PAD, :] = jnp.zeros((CONV_PAD, LANE), F32)
        dypad[m:m + CONV_PAD, :] = jnp.zeros((CONV_PAD, LANE), F32)
        wacc[...] = jnp.zeros_like(wacc)
        bacc[...] = jnp.zeros_like(bacc)

        def fill(i, c):
            r0 = pl.multiple_of(i * QB, QB)
            u = a_ref[pl.ds(r0, QB), :] * jax.nn.sigmoid(g_ref[pl.ds(r0, QB), :])
            upad[pl.ds(pl.multiple_of(r0 + CONV_PAD, 8), QB), :] = u
            dypad[pl.ds(r0, QB), :] = dy_ref[pl.ds(r0, QB), :]
            return c

        lax.fori_loop(0, n_chunk, fill, 0)

        def chunk(i, c):
            r0 = pl.multiple_of(i * QB, QB)
            dwin = dypad[pl.ds(r0, win_rows), :]
            du = jnp.zeros((QB, LANE), F32)
            rolled = {}
            for b, a, j in taps:
                if b not in rolled:
                    rolled[b] = dwin if b == 0 else pltpu.roll(dwin, win_rows - b, axis=0)
                du = du + w_ref[j:j + 1, :] * rolled[b][8 * a:8 * a + QB, :]
            av = a_ref[pl.ds(r0, QB), :]
            sg = jax.nn.sigmoid(g_ref[pl.ds(r0, QB), :])
            da_ref[pl.ds(r0, QB), :] = (du * sg).astype(da_ref.dtype)
            dg_ref[pl.ds(r0, QB), :] = (du * av * sg * (1.0 - sg)).astype(dg_ref.dtype)
            dyc = dy_ref[pl.ds(r0, QB), :]
            uwin = upad[pl.ds(r0, win_rows), :]
            rolled = {}
            for b, a, j in taps:
                if b not in rolled:
                    rolled[b] = uwin if b == 0 else pltpu.roll(uwin, b, axis=0)
                lo = CONV_PAD - 8 * a
                prod = dyc * rolled[b][lo:lo + QB, :]
                wacc[j] += prod.reshape(QB // 8, 8, LANE).sum(axis=0)
            bacc[...] += dyc.reshape(QB // 8, 8, LANE).sum(axis=0)
            return c

        lax.fori_loop(0, n_chunk, chunk, 0)
        for j in range(CONV_WIDTH):
            dw_ref[j:j + 1, :] = jnp.sum(wacc[j], axis=0, keepdims=True)
        dw_ref[CONV_WIDTH:CONV_PAD, :] = jnp.zeros((CONV_PAD - CONV_WIDTH, LANE), F32)
        db_ref[...] = jnp.sum(bacc[...], axis=0, keepdims=True)

    col = lambda off: pl.BlockSpec((m, LANE), lambda c: (0, off + c))
    return pl.pallas_call(
        body, name="conv_bwd", grid=(nch,),
        in_specs=[col(0), col(nch), col(0), pl.BlockSpec((CONV_PAD, LANE), lambda c: (0, c))],
        out_specs=[col(0), col(0), pl.BlockSpec((CONV_PAD, LANE), lambda c: (0, c)), pl.BlockSpec((1, LANE), lambda c: (0, c))],
        out_shape=(jax.ShapeDtypeStruct((m, d_model), BF16), jax.ShapeDtypeStruct((m, d_model), BF16),
                   jax.ShapeDtypeStruct((CONV_PAD, d_model), F32), jax.ShapeDtypeStruct((1, d_model), F32)),
        scratch_shapes=[pltpu.VMEM((m + CONV_PAD, LANE), F32), pltpu.VMEM((m + CONV_PAD, LANE), F32),
                        pltpu.VMEM((CONV_PAD, 8, LANE), F32), pltpu.VMEM((8, LANE), F32)],
        compiler_params=_params(1))(p, p, dy, dww)


def _softplus(z):
    return jnp.maximum(z, 0.0) + jnp.log(1.0 + jnp.exp(-jnp.abs(z)))


def _softplus_sigmoid(z):
    e = jnp.exp(-jnp.abs(z))
    r = 1.0 / (1.0 + e)
    return jnp.maximum(z, 0.0) - jnp.log(r), jnp.where(z >= 0.0, r, e * r)


def _split_dot(x, tri):
    hi = x.astype(BF16)
    lo = (x - hi.astype(F32)).astype(BF16)
    return jnp.dot(jnp.concatenate([hi, lo], axis=1), tri, preferred_element_type=F32)


def _tri(kind):
    jj = lax.broadcasted_iota(I32, (2 * QB, 2 * QB), 0) & (QB - 1)
    ss = lax.broadcasted_iota(I32, (2 * QB, 2 * QB), 1)
    keep = {"ge": jj >= ss, "lt": jj < ss, "le": jj <= ss}[kind]
    return jnp.where((ss >= QB) | keep, 1.0, 0.0).astype(BF16)


def _attn_fwd(p, d_model, shards):
    m = p.shape[0]
    nqb = m // QB
    npair = d_model // LANE
    qo, ko, vo = 2 * npair, 3 * npair, 4 * npair
    scale = HEAD_DIM ** -0.5
    n_sh = len(shards)

    assert nqb >= KEY_TILES

    def body(q_ref, k_ref, v_ref, *rest):
        shard_refs, (o_ref, t_ref), rest = rest[:n_sh], rest[n_sh:n_sh + 2], rest[n_sh + 2:]
        gathered_refs, (acc_ref, car_ref), sems = rest[:n_sh], rest[n_sh:n_sh + 2], rest[n_sh + 2:]
        starts, waits = _gather_copies(shard_refs, gathered_refs, *sems)

        @pl.when((pl.program_id(0) == 0) & (pl.program_id(1) == 0))
        def _():
            for f in starts:
                f()

        qb = pl.program_id(1)
        lane = lax.broadcasted_iota(I32, (QB, LANE), 1)
        head0 = lane < HEAD_DIM
        row_g = qb * QB + lax.broadcasted_iota(I32, (QB, KEY_CHUNK), 0)
        col_l = lax.broadcasted_iota(I32, (QB, KEY_CHUNK), 1)
        tri = _tri("ge")
        q2 = (q_ref[...] * scale).astype(BF16)
        zero = jnp.zeros_like(q2)
        qh = (jnp.where(head0, q2, zero), jnp.where(head0, zero, q2))
        acc_ref[...] = jnp.zeros_like(acc_ref)
        car_ref[...] = jnp.zeros_like(car_ref)

        def chunk(first_tile, bound):
            r0 = pl.multiple_of(first_tile * QB, QB)
            kc = k_ref[pl.ds(r0, KEY_CHUNK), :].astype(BF16)
            vc = v_ref[pl.ds(r0, KEY_CHUNK), :].astype(BF16)
            valid = None if bound is None else (col_l + r0) < bound
            for h in range(2):
                z = lax.dot_general(qh[h], kc, NT, preferred_element_type=F32)
                sp = _softplus(z)
                if valid is not None:
                    sp = jnp.where(valid, sp, 0.0)
                car = car_ref[h]
                a_tiles = [None] * KEY_TILES
                for i in reversed(range(KEY_TILES)):
                    cr = _split_dot(sp[:, i * QB:(i + 1) * QB], tri)
                    a_tiles[i] = jnp.exp(z[:, i * QB:(i + 1) * QB] - (cr[:, :QB] + car))
                    car = car + cr[:, QB:]
                a = jnp.concatenate(a_tiles, axis=1)
                if valid is not None:
                    a = jnp.where(valid, a, 0.0)
                acc_ref[h] += jnp.dot(a.astype(BF16), vc, preferred_element_type=F32)
                car_ref[h] = car

        near = jnp.maximum(qb - (KEY_TILES - 1), 0)
        chunk(near, row_g)
        n_full = lax.shift_right_logical(near, KEY_SHIFT)

        def step(i, c):
            chunk(near - KEY_TILES * (i + 1), None)
            return c

        lax.fori_loop(0, n_full, step, 0)
        left = near - KEY_TILES * n_full

        @pl.when(left > 0)
        def _():
            chunk(0, left * QB)

        o_ref[...] = jnp.where(head0, acc_ref[0], acc_ref[1]).astype(o_ref.dtype)
        t_ref[:, :QB] = car_ref[0]
        t_ref[:, QB:] = car_ref[1]

        @pl.when((pl.program_id(0) == npair - 1) & (pl.program_id(1) == nqb - 1))
        def _():
            for f in waits:
                f()

    outs = pl.pallas_call(
        body, name="attn_fwd", grid=(npair, nqb),
        in_specs=[pl.BlockSpec((QB, LANE), lambda hp, qb: (qb, qo + hp)),
                  pl.BlockSpec((m, LANE), lambda hp, qb: (0, ko + hp)),
                  pl.BlockSpec((m, LANE), lambda hp, qb: (0, vo + hp))] + [ANY] * n_sh,
        out_specs=[pl.BlockSpec((QB, LANE), lambda hp, qb: (qb, hp)),
                   pl.BlockSpec((QB, 2 * LANE), lambda hp, qb: (qb, hp))] + [ANY] * n_sh,
        out_shape=[jax.ShapeDtypeStruct((m, d_model), BF16), jax.ShapeDtypeStruct((m, 2 * d_model), F32)]
        + [jax.ShapeDtypeStruct((N_DEV,) + s.shape, s.dtype) for s in shards],
        scratch_shapes=[pltpu.VMEM((2, QB, LANE), F32), pltpu.VMEM((2, QB, LANE), F32)] + _exchange_sems(n_sh)
        + [pltpu.SemaphoreType.DMA((n_sh,))],
        compiler_params=_params(2))(p, p, p, *shards)
    return outs[0], outs[1], outs[2:]


def _attn_bwd(p, d_o, tot, d_model, slabs):
    m = p.shape[0]
    nqb = m // QB
    npair = d_model // LANE
    qo, ko, vo = 2 * npair, 3 * npair, 4 * npair
    scale = HEAD_DIM ** -0.5
    n_sl = len(slabs)

    assert nqb >= KEY_TILES

    def body(q_ref, k_ref, v_ref, do_ref, t_ref, *rest):
        slab_refs, (dq_ref, dk_ref, dv_ref), rest = rest[:n_sl], rest[n_sl:n_sl + 3], rest[n_sl + 3:]
        recv_refs, (dkacc, dvacc, dqacc, csp, cg), sems = rest[:n_sl], rest[n_sl:n_sl + 5], rest[n_sl + 5:]
        starts, waits = _scatter_copies(slab_refs, recv_refs, *sems)

        @pl.when((pl.program_id(0) == 0) & (pl.program_id(1) == 0))
        def _():
            for f in starts:
                f()

        qb = pl.program_id(1)
        lane = lax.broadcasted_iota(I32, (QB, LANE), 1)
        head0 = lane < HEAD_DIM
        head0_keys = lax.broadcasted_iota(I32, (KEY_CHUNK, LANE), 1) < HEAD_DIM
        row_g = qb * QB + lax.broadcasted_iota(I32, (QB, KEY_CHUNK), 0)
        col_l = lax.broadcasted_iota(I32, (QB, KEY_CHUNK), 1)
        tri_lt = _tri("lt")
        tri_le = _tri("le")
        q2 = (q_ref[...] * scale).astype(BF16)
        do2 = do_ref[...]
        zero = jnp.zeros_like(q2)
        qh = (jnp.where(head0, q2, zero), jnp.where(head0, zero, q2))
        doh = (jnp.where(head0, do2, zero), jnp.where(head0, zero, do2))

        @pl.when(qb == 0)
        def _():
            dkacc[...] = jnp.zeros_like(dkacc)
            dvacc[...] = jnp.zeros_like(dvacc)

        dqacc[...] = jnp.zeros_like(dqacc)
        csp[...] = jnp.zeros_like(csp)
        cg[...] = jnp.zeros_like(cg)

        def chunk(first_tile, bound):
            r0 = pl.multiple_of(first_tile * QB, QB)
            kc = k_ref[pl.ds(r0, KEY_CHUNK), :].astype(BF16)
            vc = v_ref[pl.ds(r0, KEY_CHUNK), :].astype(BF16)
            valid = None if bound is None else (col_l + r0) < bound
            dkp, dvp = [], []
            for h in range(2):
                z = lax.dot_general(qh[h], kc, NT, preferred_element_type=F32)
                sp, sg = _softplus_sigmoid(z)
                if valid is not None:
                    sp = jnp.where(valid, sp, 0.0)
                da = lax.dot_general(doh[h], vc, NT, preferred_element_type=F32)
                tot_h = t_ref[:, h * QB:(h + 1) * QB]
                before, g_before = csp[h], cg[h]
                a_tiles, dz_tiles = [], []
                for i in range(KEY_TILES):
                    cols = slice(i * QB, (i + 1) * QB)
                    cr = _split_dot(sp[:, cols], tri_lt)
                    a = jnp.exp(z[:, cols] - (tot_h - (before + cr[:, :QB])))
                    if valid is not None:
                        a = jnp.where(valid[:, cols], a, 0.0)
                    g = a * da[:, cols]
                    gr = _split_dot(g, tri_le)
                    dz = g - sg[:, cols] * (g_before + gr[:, :QB])
                    if valid is not None:
                        dz = jnp.where(valid[:, cols], dz, 0.0)
                    a_tiles.append(a)
                    dz_tiles.append(dz)
                    before = before + cr[:, QB:]
                    g_before = g_before + gr[:, QB:]
                dzb = jnp.concatenate(dz_tiles, axis=1).astype(BF16)
                ab = jnp.concatenate(a_tiles, axis=1).astype(BF16)
                dqacc[h] += jnp.dot(dzb, kc, preferred_element_type=F32)
                dkp.append(lax.dot_general(dzb, q2, TN, preferred_element_type=F32))
                dvp.append(lax.dot_general(ab, do2, TN, preferred_element_type=F32))
                csp[h] = before
                cg[h] = g_before
            dkacc[pl.ds(r0, KEY_CHUNK), :] += jnp.where(head0_keys, dkp[0], dkp[1])
            dvacc[pl.ds(r0, KEY_CHUNK), :] += jnp.where(head0_keys, dvp[0], dvp[1])

        near = jnp.maximum(qb - (KEY_TILES - 1), 0)
        n_full = lax.shift_right_logical(near, KEY_SHIFT)

        def step(i, c):
            chunk(KEY_TILES * i, None)
            return c

        lax.fori_loop(0, n_full, step, 0)

        @pl.when(near > KEY_TILES * n_full)
        def _():
            chunk(KEY_TILES * n_full, near * QB)

        chunk(near, row_g)
        dq_ref[...] = (jnp.where(head0, dqacc[0], dqacc[1]) * scale).astype(dq_ref.dtype)

        @pl.when(qb == nqb - 1)
        def _():
            dk_ref[...] = dkacc[...].astype(dk_ref.dtype)
            dv_ref[...] = dvacc[...].astype(dv_ref.dtype)

        @pl.when((pl.program_id(0) == npair - 1) & (qb == nqb - 1))
        def _():
            for f in waits:
                f()

    out = jax.ShapeDtypeStruct((m, d_model), BF16)
    outs = pl.pallas_call(
        body, name="attn_bwd", grid=(npair, nqb),
        in_specs=[pl.BlockSpec((QB, LANE), lambda hp, qb: (qb, qo + hp)),
                  pl.BlockSpec((m, LANE), lambda hp, qb: (0, ko + hp)),
                  pl.BlockSpec((m, LANE), lambda hp, qb: (0, vo + hp)),
                  pl.BlockSpec((QB, LANE), lambda hp, qb: (qb, hp)),
                  pl.BlockSpec((QB, 2 * LANE), lambda hp, qb: (qb, hp))] + [ANY] * n_sl,
        out_specs=[pl.BlockSpec((QB, LANE), lambda hp, qb: (qb, hp)),
                   pl.BlockSpec((m, LANE), lambda hp, qb: (0, hp)),
                   pl.BlockSpec((m, LANE), lambda hp, qb: (0, hp))] + [ANY] * n_sl,
        out_shape=[out, out, out] + _received_shapes(slabs),
        scratch_shapes=[pltpu.VMEM((m, LANE), F32), pltpu.VMEM((m, LANE), F32), pltpu.VMEM((2, QB, LANE), F32),
                        pltpu.VMEM((2, QB, LANE), F32), pltpu.VMEM((2, QB, LANE), F32)] + _exchange_sems(n_sl),
        compiler_params=_params(2))(p, p, p, d_o, tot, *slabs)
    return outs[0], outs[1], outs[2], outs[3:]


def _mesh_pos():
    return lax.axis_index("x"), lax.axis_index("y"), lax.axis_index("c")


def _other_chips(x, y):
    return [(1 - x, y), (x, 1 - y), (1 - x, 1 - y)]


def _dev(x, y, c):
    return 4 * x + 2 * y + c


def _all_gather(shards):
    n = len(shards)

    def body(*refs):
        ins, outs = refs[:n], refs[n:2 * n]
        send_sems, recv_sems, local_sems = refs[2 * n:]
        x, y, c = _mesh_pos()
        sibling = (x, y, 1 - c)
        chips = _other_chips(x, y)

        def copy(k, s, src, block, to):
            return pltpu.make_async_remote_copy(src_ref=src, dst_ref=outs[k].at[_dev(*block)], send_sem=send_sems.at[k, s],
                                                recv_sem=recv_sems.at[k, s], device_id=to, device_id_type=MESH)

        def held(k, block):
            return outs[k].at[_dev(*block)]

        mine = [pltpu.make_async_copy(ins[k], held(k, (x, y, c)), local_sems.at[k]) for k in range(n)]
        for cp in mine:
            cp.start()
        first = []
        for k in range(n):
            first.append(copy(k, 0, ins[k], (x, y, c), sibling))
            for j, chip in enumerate(chips):
                first.append(copy(k, 1 + j, ins[k], (x, y, c), (*chip, c)))
        for cp in first:
            cp.start()
        passed = []
        for j, chip in enumerate(chips):
            for k in range(n):
                copy(k, 1 + j, held(k, (*chip, c)), (*chip, c), (x, y, c)).wait_recv()
                fwd = copy(k, 4 + j, held(k, (*chip, c)), (*chip, c), sibling)
                fwd.start()
                passed.append(fwd)
        for k in range(n):
            copy(k, 0, held(k, (x, y, 1 - c)), (x, y, 1 - c), (x, y, c)).wait_recv()
            for j, chip in enumerate(chips):
                copy(k, 4 + j, held(k, (*chip, 1 - c)), (*chip, 1 - c), (x, y, c)).wait_recv()
        for cp in first + passed:
            cp.wait_send()
        for cp in mine:
            cp.wait()

    return pl.pallas_call(
        body, name="comm_all_gather", in_specs=[ANY] * n, out_specs=[ANY] * n,
        out_shape=[jax.ShapeDtypeStruct((N_DEV,) + s.shape, s.dtype) for s in shards],
        scratch_shapes=[pltpu.SemaphoreType.DMA((n, 7)), pltpu.SemaphoreType.DMA((n, 7)), pltpu.SemaphoreType.DMA((n,))],
    )(*shards)


def _peers(x, y, c):
    out = []
    for mask in range(1, N_DEV):
        px, py, pc = x ^ (mask >> 2), y ^ ((mask >> 1) & 1), c ^ (mask & 1)
        out.append((mask - 1, (px, py, pc), _dev(px, py, pc)))
    return out


def _remote(src, dst, send_sems, recv_sems, k, s, peer):
    return pltpu.make_async_remote_copy(src_ref=src, dst_ref=dst, send_sem=send_sems.at[k, s], recv_sem=recv_sems.at[k, s],
                                        device_id=peer, device_id_type=MESH)


def _gather_copies(ins, outs, send_sems, recv_sems, local_sems):
    x, y, c = _mesh_pos()
    me = _dev(x, y, c)
    starts, waits = [], []
    for k in range(len(ins)):
        own = pltpu.make_async_copy(ins[k], outs[k].at[me], local_sems.at[k])
        starts.append(own.start)
        waits.append(own.wait)
        for s, peer, idx in _peers(x, y, c):
            send = _remote(ins[k], outs[k].at[me], send_sems, recv_sems, k, s, peer)
            land = _remote(ins[k], outs[k].at[idx], send_sems, recv_sems, k, s, peer)
            starts.append(send.start)
            waits += [land.wait_recv, send.wait_send]
    return starts, waits


def _scatter_copies(ins, outs, send_sems, recv_sems):
    x, y, c = _mesh_pos()
    starts, waits = [], []
    for k in range(len(ins)):
        for s, peer, idx in _peers(x, y, c):
            send = _remote(ins[k].at[idx], outs[k].at[s], send_sems, recv_sems, k, s, peer)
            starts.append(send.start)
            waits += [send.wait_recv, send.wait_send]
    return starts, waits


def _exchange_sems(n):
    return [pltpu.SemaphoreType.DMA((n, N_DEV - 1)), pltpu.SemaphoreType.DMA((n, N_DEV - 1))]


def _received_shapes(slabs):
    return [jax.ShapeDtypeStruct((N_DEV - 1,) + a.shape[1:], a.dtype) for a in slabs]


def _scatter_exchange(slabs):
    n = len(slabs)

    def body(*refs):
        starts, waits = _scatter_copies(refs[:n], refs[n:2 * n], *refs[2 * n:])
        for f in starts + waits:
            f()

    return pl.pallas_call(body, name="comm_scatter_exchange", in_specs=[ANY] * n, out_specs=[ANY] * n,
                          out_shape=_received_shapes(slabs), scratch_shapes=_exchange_sems(n))(*slabs)


def _shard_tile(rows):
    for tr in range(min(rows, 352), 0, -1):
        if rows % tr == 0 and (tr % 16 == 0 or tr == rows):
            return tr


def _adamw_math(w, g, m, v):
    m = ADAM_B1 * m + (1.0 - ADAM_B1) * g
    v = ADAM_B2 * v + (1.0 - ADAM_B2) * (g * g)
    m_hat = m / (1.0 - ADAM_B1 ** ADAM_STEP)
    v_hat = v / (1.0 - ADAM_B2 ** ADAM_STEP)
    delta = -ADAM_LR * (m_hat / (jnp.sqrt(v_hat) + ADAM_EPS) + ADAM_WD * w)
    return delta, m, v


def _adamw_shard(me, grad, received, w, m, v):
    rows, cols = w.shape
    tr = _shard_tile(rows)

    def body(me_ref, g_ref, r_ref, w_ref, m_ref, v_ref, go_ref, do_ref, mo_ref, vo_ref):
        g = g_ref[...]
        for s in range(N_DEV - 1):
            g = g + r_ref[s].astype(F32)
        delta, m_new, v_new = _adamw_math(w_ref[...], g, m_ref[...], v_ref[...])
        go_ref[...] = g
        do_ref[...] = delta
        mo_ref[...] = m_new
        vo_ref[...] = v_new

    flat = pl.BlockSpec((tr, cols), lambda i, me: (i, 0))
    gs = pltpu.PrefetchScalarGridSpec(
        num_scalar_prefetch=1, grid=(rows // tr,),
        in_specs=[pl.BlockSpec((None, tr, cols), lambda i, me: (me[0], i, 0)),
                  pl.BlockSpec((N_DEV - 1, tr, cols), lambda i, me: (0, i, 0)), flat, flat, flat],
        out_specs=[flat, flat, flat, flat])
    out = jax.ShapeDtypeStruct((rows, cols), F32)
    return pl.pallas_call(body, name="adamw_shard", grid_spec=gs, out_shape=(out, out, out, out),
                          compiler_params=_params(1))(me, grad, received, w, m, v)


def _small_reduce_adamw(slabs, w, m, v):
    _, rows, _ = slabs.shape

    def body(s_ref, w_ref, m_ref, v_ref, g_ref, d_ref, mo_ref, vo_ref, land, send_sems, recv_sems):
        x, y, c = _mesh_pos()
        me = _dev(x, y, c)
        copies = []
        for mask in range(1, N_DEV):
            px, py, pc = x ^ (mask >> 2), y ^ ((mask >> 1) & 1), c ^ (mask & 1)
            copies.append(pltpu.make_async_remote_copy(
                src_ref=s_ref.at[_dev(px, py, pc)], dst_ref=land.at[me], send_sem=send_sems.at[mask - 1],
                recv_sem=recv_sems.at[mask - 1], device_id=(px, py, pc), device_id_type=MESH))
        for cp in copies:
            cp.start()
        land[me] = s_ref[me]
        for mask in range(1, N_DEV):
            px, py, pc = x ^ (mask >> 2), y ^ ((mask >> 1) & 1), c ^ (mask & 1)
            pltpu.make_async_remote_copy(
                src_ref=s_ref.at[me], dst_ref=land.at[_dev(px, py, pc)], send_sem=send_sems.at[mask - 1],
                recv_sem=recv_sems.at[mask - 1], device_id=(px, py, pc), device_id_type=MESH).wait_recv()
        for cp in copies:
            cp.wait_send()
        g = land[0]
        for d in range(1, N_DEV):
            g = g + land[d]
        delta, m_new, v_new = _adamw_math(w_ref[...], g, m_ref[...], v_ref[...])
        g_ref[...] = g
        d_ref[...] = delta
        mo_ref[...] = m_new
        vo_ref[...] = v_new

    out = jax.ShapeDtypeStruct((rows, LANE), F32)
    return pl.pallas_call(
        body, name="comm_small_reduce_adamw", in_specs=[VMEM_WHOLE] * 4, out_specs=[VMEM_WHOLE] * 4, out_shape=(out, out, out, out),
        scratch_shapes=[pltpu.VMEM((N_DEV, rows, LANE), F32), pltpu.SemaphoreType.DMA((N_DEV - 1,)),
                        pltpu.SemaphoreType.DMA((N_DEV - 1,))],
    )(slabs, w, m, v)


def _cast_bf16(arrs):
    n = len(arrs)

    def body(*refs):
        for i_ref, o_ref in zip(refs[:n], refs[n:]):
            o_ref[...] = i_ref[...].astype(BF16)

    return pl.pallas_call(body, name="cast_bf16", in_specs=[VMEM_WHOLE] * n, out_specs=[VMEM_WHOLE] * n,
                          out_shape=[jax.ShapeDtypeStruct(a.shape, BF16) for a in arrs],
                          compiler_params=pltpu.CompilerParams(vmem_limit_bytes=VMEM_LIMIT))(*arrs)


REPLICATED = ("pre_mix_g", "gate_b", "dw_b", "conv_ln_g", "conv_ln_b", "post_mix_g", "pre_ffn_g", "post_ffn_g")
SHARDED = ("w_in", "w_conv_out", "w_attn_out", "w_o", "w_ffn_in", "w_ffn_out")
WEIGHTS = ("meta_tokens", "pre_mix_g", "w_in", "gate_b", "dw_w", "dw_b", "conv_ln_g", "conv_ln_b", "w_conv_out",
           "w_attn_out", "w_o", "post_mix_g", "pre_ffn_g", "w_ffn_in", "w_ffn_out", "post_ffn_g")


def kernel(x, meta_tokens, pre_mix_g, w_in, gate_b, dw_w, dw_b, conv_ln_g, conv_ln_b, w_conv_out, w_attn_out, w_o, post_mix_g, pre_ffn_g, w_ffn_in, w_ffn_out, post_ffn_g, loss_target, m_meta_tokens, m_pre_mix_g, m_w_in, m_gate_b, m_dw_w, m_dw_b, m_conv_ln_g, m_conv_ln_b, m_w_conv_out, m_w_attn_out, m_w_o, m_post_mix_g, m_pre_ffn_g, m_w_ffn_in, m_w_ffn_out, m_post_ffn_g, v_meta_tokens, v_pre_mix_g, v_w_in, v_gate_b, v_dw_w, v_dw_b, v_conv_ln_g, v_conv_ln_b, v_w_conv_out, v_w_attn_out, v_w_o, v_post_mix_g, v_pre_ffn_g, v_w_ffn_in, v_w_ffn_out, v_post_ffn_g):
    given = dict(locals())
    seq, d = x.shape[1], x.shape[2]
    n_meta = meta_tokens.shape[0]
    length = n_meta + seq
    m_rows = -(-length // QB) * QB
    dc = d // N_DEV
    assert dc == LANE and n_meta % 8 == 0 and seq % 8 == 0
    fs = w_ffn_in.shape[2]
    fr = w_ffn_out.shape[1]
    assert 2 * fr == fs

    local = {k: given[k][0] for k in SHARDED}
    cast = _cast_bf16([local[k] for k in SHARDED])
    dww_pad = jnp.pad(dw_w[0], ((0, CONV_PAD - CONV_WIDTH), (0, 0)))
    wi, meta_g, dww_g = _all_gather([cast[0], meta_tokens, dww_pad])
    meta_full = jnp.concatenate([meta_g[j] for j in range(N_DEV)], axis=1)
    dww_full = jnp.concatenate([dww_g[j] for j in range(N_DEV)], axis=1)
    ns = wi.shape[2]

    tail = jnp.zeros((m_rows - length, d), F32)
    h0 = jnp.concatenate([meta_full, x[0], tail], axis=0)
    target = jnp.concatenate([jnp.zeros((n_meta, d), F32), loss_target[0], tail], axis=0)

    (u,) = _rows("pre_mix_norm", lambda r0, xs, ps: ([_rms(xs[0], ps[0])], []), [h0], [pre_mix_g], [BF16], [])
    p = _matmul("in_proj", NN, u, wi, pl.BlockSpec((m_rows, d), lambda i: (0, 0)), pl.BlockSpec((None, d, ns), lambda i: (i, 0, 0)),
                pl.BlockSpec((m_rows, ns), lambda i: (0, i)), jax.ShapeDtypeStruct((m_rows, N_DEV * ns), F32), (N_DEV,))
    o, tot, gathered = _attn_fwd(p, d, list(cast[1:]))
    wco, wao, wo = (g.reshape(d, d) for g in gathered[0:3])
    wfi = gathered[3]
    wfo = gathered[4].reshape(N_DEV // 2, fs, d)
    y = _conv_fwd(p, dww_full, dw_b, d)
    (yc,) = _rows("conv_norm", lambda r0, xs, ps: ([_ln_silu(xs[0], ps[0], ps[1])], []), [y], [conv_ln_g, conv_ln_b], [BF16], [])
    y_conv = _dense_fwd("conv_out", yc, wco)
    y_attn = _dense_fwd("attn_out", o, wao)
    gate_cols = [(d, 5), (d, 6), None, None]
    (mixin,) = _rows("gate_mix", lambda r0, xs, ps: ([_gate_mix(*xs, ps[0])], []), [p, p, y_conv, y_attn], [gate_b], [BF16], [],
                     row_in_cols=gate_cols, row_out_widths=[d])
    mix = _dense_fwd("mix_out", mixin, wo)
    h1, u2 = _rows("post_mix", lambda r0, xs, ps: (list(_post_mix(xs[0], xs[1], ps[0], ps[1])), []), [h0, mix],
                   [post_mix_g, pre_ffn_g], [F32, BF16], [])
    ab = _matmul("ffn_in", NN, u2, wfi, pl.BlockSpec((m_rows, d), lambda i: (0, 0)), pl.BlockSpec((None, d, fs), lambda i: (i, 0, 0)),
                 pl.BlockSpec((None, m_rows, fs), lambda i: (i, 0, 0)), jax.ShapeDtypeStruct((N_DEV, m_rows, fs), F32), (N_DEV,))
    half = N_DEV // 2
    tm = _row_tile(m_rows)
    pair = lambda off: pl.BlockSpec((None, tm, fs), lambda j, i, off=off: (j + off, i, 0))
    (f_in,) = _rowwise("swiglu", lambda r0, xs, ps: ([_swiglu(xs[0], xs[1])], []), [ab, ab], [],
                       [jax.ShapeDtypeStruct((half, m_rows, fs), BF16)], [], grid=(half, m_rows // tm),
                       in_specs=[pair(0), pair(half)], out_specs=[pair(0)], tm=tm, row_axis=1)
    f = _matmul("ffn_out", NN, f_in, wfo, pl.BlockSpec((None, m_rows, fs), lambda j: (j, 0, 0)), pl.BlockSpec((None, fs, d), lambda j: (j, 0, 0)),
                pl.BlockSpec((m_rows, d), lambda j: (0, 0)), jax.ShapeDtypeStruct((m_rows, d), F32), (half,), acc_axis=0)

    def loss_head(r0, xs, ps):
        h1_, f_, t_ = xs
        r, vjp = jax.vjp(_rms, f_, ps[0])
        rows = r0 + lax.broadcasted_iota(I32, (h1_.shape[0], 1), 0)
        real = (rows >= n_meta) & (rows < length)
        err = jnp.where(real, h1_ + r - t_, 0.0)
        dh2 = err * (1.0 / d)
        d_f, dg = vjp(dh2)
        part = jnp.sum(0.5 * jnp.mean(err * err, axis=-1, keepdims=True), axis=0, keepdims=True)
        return [d_f, dh2], [dg, jnp.broadcast_to(part, (1, LANE))]

    d_f, dh2, g_post_ffn, loss_part = _rows("loss_head", loss_head, [h1, f, target], [post_ffn_g], [BF16, F32], [d, LANE])

    d_fin = _matmul("ffn_out_dx", NT, d_f, wfo, pl.BlockSpec((m_rows, d), lambda j: (0, 0)), pl.BlockSpec((None, fs, d), lambda j: (j, 0, 0)),
                    pl.BlockSpec((None, m_rows, fs), lambda j: (j, 0, 0)), jax.ShapeDtypeStruct((half, m_rows, fs), F32), (half,))
    g_wfo = _matmul("ffn_out_dw", TN, f_in, d_f, pl.BlockSpec((None, m_rows, fs), lambda j: (j, 0, 0)), pl.BlockSpec((m_rows, d), lambda j: (0, 0)),
                    pl.BlockSpec((None, fs, d), lambda j: (j, 0, 0)), jax.ShapeDtypeStruct((half, fs, d), F32), (half,), twin_bf16=True)

    def swiglu_bwd(r0, xs, ps):
        _, vjp = jax.vjp(_swiglu, xs[0], xs[1])
        return list(vjp(xs[2])), []

    out8 = lambda off: pl.BlockSpec((None, tm, fs), lambda j, i, off=off: (j + off, i, 0))
    d_a, d_b = _rowwise("swiglu_bwd", swiglu_bwd, [ab, ab, d_fin], [],
                        [jax.ShapeDtypeStruct((half, m_rows, fs), BF16)] * 2, [], grid=(half, m_rows // tm),
                        in_specs=[pair(0), pair(half), pair(0)], out_specs=[out8(0), out8(0)], tm=tm, row_axis=1)
    d_ab = jnp.concatenate([d_a, d_b], axis=0)
    du2 = _matmul("ffn_in_dx", NT, d_ab, wfi, pl.BlockSpec((None, m_rows, fs), lambda i: (i, 0, 0)), pl.BlockSpec((None, d, fs), lambda i: (i, 0, 0)),
                  pl.BlockSpec((m_rows, d), lambda i: (0, 0)), jax.ShapeDtypeStruct((m_rows, d), F32), (N_DEV,), acc_axis=0)
    g_wfi = _matmul("ffn_in_dw", TN, u2, d_ab, pl.BlockSpec((m_rows, d), lambda i: (0, 0)), pl.BlockSpec((None, m_rows, fs), lambda i: (i, 0, 0)),
                    pl.BlockSpec((None, d, fs), lambda i: (i, 0, 0)), jax.ShapeDtypeStruct((N_DEV, d, fs), F32), (N_DEV,), twin_bf16=True)

    def post_mix_bwd(r0, xs, ps):
        h0_, mix_, dh2_, du2_ = xs
        _, vjp = jax.vjp(_post_mix, h0_, mix_, ps[0], ps[1])
        dh0_, dmix_, dg1, dg2 = vjp((dh2_, du2_))
        return [dmix_, dh0_], [dg1, dg2]

    d_mix, dh1, g_post_mix, g_pre_ffn = _rows("post_mix_bwd", post_mix_bwd, [h0, mix, dh2, du2], [post_mix_g, pre_ffn_g],
                                              [BF16, F32], [d, d])
    d_mixin = _dense_dx("mix_out_dx", d_mix, wo, F32)
    g_wo = _dense_dw("mix_out_dw", mixin, d_mix)

    def gate_mix_bwd(r0, xs, ps):
        _, vjp = jax.vjp(_gate_mix, xs[0], xs[1], xs[2], xs[3], ps[0])
        dpgc, dpga, dyc_, dya_, dgb = vjp(xs[4])
        return [dpgc, dpga, dyc_, dya_], [dgb]

    dp_gc, dp_ga, d_yconv, d_yattn, g_gate_b = _rows(
        "gate_mix_bwd", gate_mix_bwd, [p, p, y_conv, y_attn, d_mixin], [gate_b], [BF16] * 4, [2 * d],
        row_in_cols=gate_cols + [None], row_out_widths=[d] * 4)
    d_o = _dense_dx("attn_out_dx", d_yattn, wao, BF16)
    g_wao = _dense_dw("attn_out_dw", o, d_yattn)
    big = {"w_ffn_out": [g.reshape(N_DEV, fr, d) for g in g_wfo], "w_ffn_in": g_wfi,
           "w_o": [g.reshape(N_DEV, dc, d) for g in g_wo], "w_attn_out": [g.reshape(N_DEV, dc, d) for g in g_wao]}
    early = ("w_ffn_out", "w_ffn_in", "w_o", "w_attn_out")
    dq, dk, dv, received_early = _attn_bwd(p, d_o, tot, d, [big[k][1] for k in early])
    d_yc = _dense_dx("conv_out_dx", d_yconv, wco, F32)
    g_wco = _dense_dw("conv_out_dw", yc, d_yconv)

    def conv_norm_bwd(r0, xs, ps):
        _, vjp = jax.vjp(_ln_silu, xs[0], ps[0], ps[1])
        dy_, dg, db = vjp(xs[1])
        return [dy_], [dg, db]

    d_y, g_ln_g, g_ln_b = _rows("conv_norm_bwd", conv_norm_bwd, [y, d_yc], [conv_ln_g, conv_ln_b], [F32], [d, d])
    dp_a, dp_g, g_dww, g_dwb = _conv_bwd(p, d_y, dww_full, d)
    dp = jnp.concatenate([dp_a, dp_g, dq, dk, dv, dp_gc, dp_ga], axis=1)
    du = _matmul("in_proj_dx", NT, dp, wi, pl.BlockSpec((m_rows, ns), lambda i: (0, i)), pl.BlockSpec((None, d, ns), lambda i: (i, 0, 0)),
                 pl.BlockSpec((m_rows, d), lambda i: (0, 0)), jax.ShapeDtypeStruct((m_rows, d), F32), (N_DEV,), acc_axis=0)
    g_wi = _matmul("in_proj_dw", TN, u, dp, pl.BlockSpec((m_rows, d), lambda i: (0, 0)), pl.BlockSpec((m_rows, ns), lambda i: (0, i)),
                   pl.BlockSpec((None, d, ns), lambda i: (i, 0, 0)), jax.ShapeDtypeStruct((N_DEV, d, ns), F32), (N_DEV,), twin_bf16=True)

    def pre_mix_bwd(r0, xs, ps):
        _, vjp = jax.vjp(_rms, xs[0], ps[0])
        dx, dg = vjp(xs[1])
        return [xs[2] + dx], [dg]

    dh0, g_pre_mix = _rows("pre_mix_bwd", pre_mix_bwd, [h0, du, dh1], [pre_mix_g], [F32], [d])
    grad_x = dh0[n_meta:length][None]

    x_i, y_i, c_i = _mesh_pos()
    me = _dev(x_i, y_i, c_i)
    me_arr = jnp.reshape(me, (1,)).astype(I32)
    big["w_in"] = g_wi
    big["w_conv_out"] = [g.reshape(N_DEV, dc, d) for g in g_wco]
    late = ("w_conv_out", "w_in")
    received = dict(zip(early, received_early))
    received.update(zip(late, _scatter_exchange([big[k][1] for k in late])))
    results = {}
    for k in SHARDED:
        outs = _adamw_shard(me_arr, big[k][0], received[k], local[k], given["m_" + k][0], given["v_" + k][0])
        results[k] = tuple(a[None] for a in outs)

    rep_grads = {"pre_mix_g": g_pre_mix, "gate_b": g_gate_b, "dw_b": g_dwb, "conv_ln_g": g_ln_g, "conv_ln_b": g_ln_b,
                 "post_mix_g": g_post_mix, "pre_ffn_g": g_pre_ffn, "post_ffn_g": g_post_ffn}

    def pack_rep(get):
        return jnp.concatenate([get(k) for k in REPLICATED], axis=1).reshape(-1, LANE)

    rep_rows = pack_rep(lambda k: rep_grads[k])
    n_rep = rep_rows.shape[0]
    loss_rows = jnp.broadcast_to(loss_part, (8, LANE))
    g_meta = dh0[0:n_meta]
    slabs = jnp.stack([jnp.concatenate([rep_rows, loss_rows, g_dww[:, j * LANE:(j + 1) * LANE], g_meta[:, j * LANE:(j + 1) * LANE]], axis=0)
                       for j in range(N_DEV)])

    def pack_small(prefix):
        dww_own = jnp.pad(given[prefix + "dw_w"][0], ((0, CONV_PAD - CONV_WIDTH), (0, 0)))
        return jnp.concatenate([pack_rep(lambda k: given[prefix + k]), jnp.zeros((8, LANE), F32), dww_own,
                                given[prefix + "meta_tokens"]], axis=0)

    small = _small_reduce_adamw(slabs, pack_small(""), pack_small("m_"), pack_small("v_"))
    loss = small[0][n_rep, 0]

    def unpack(arr):
        out = {}
        flat = arr[:n_rep].reshape(1, -1)
        off = 0
        for k in REPLICATED:
            w = given[k].shape[1]
            out[k] = flat[:, off:off + w]
            off += w
        out["dw_w"] = arr[n_rep + 8:n_rep + 8 + CONV_WIDTH][None]
        out["meta_tokens"] = arr[n_rep + 8 + CONV_PAD:n_rep + 8 + CONV_PAD + n_meta]
        return out

    small_out = [unpack(a) for a in small]
    for k in WEIGHTS:
        if k not in results:
            results[k] = tuple(s[k] for s in small_out)
    return (loss, grad_x, *[results[k][0] for k in WEIGHTS], *[results[k][1] for k in WEIGHTS],
            *[results[k][2] for k in WEIGHTS], *[results[k][3] for k in WEIGHTS])
```

```python
import jax
import jax.numpy as jnp
from jax import lax
from jax.experimental import pallas as pl
from jax.experimental.pallas import tpu as pltpu

F32 = jnp.float32
BF16 = jnp.bfloat16
I32 = jnp.int32

N_DEV = 8
LANE = 128
HEAD_DIM = 64
QB = 128
KEY_SHIFT = 2
KEY_TILES = 1 << KEY_SHIFT
KEY_CHUNK = KEY_TILES * QB
LANE_BLOCKS = 2
BWD_LANE_BLOCKS = 2
CONV_WIDTH = 31
CONV_PAD = 32
ROW_CHUNK = 128
RMS_EPS = 1e-6
LN_EPS = 1e-5
ADAM_LR = 0.001
ADAM_B1 = 0.9
ADAM_B2 = 0.999
ADAM_EPS = 1e-08
ADAM_WD = 0.01
ADAM_STEP = 10
VMEM_LIMIT = 56 * 1024 * 1024

NN = (((1,), (0,)), ((), ()))
NT = (((1,), (1,)), ((), ()))
TN = (((0,), (0,)), ((), ()))
MESH = pl.DeviceIdType.MESH
ANY = pl.BlockSpec(memory_space=pl.ANY)
VMEM_WHOLE = pl.BlockSpec(memory_space=pltpu.VMEM)


def _params(n_axes):
    return pltpu.CompilerParams(dimension_semantics=("arbitrary",) * n_axes, vmem_limit_bytes=VMEM_LIMIT)


def _row_tile(m):
    assert m % QB == 0
    return m // 4 if m % 64 == 0 else QB


def _matmul(name, dims, a, b, a_spec, b_spec, o_spec, out_shape, grid, acc_axis=None, twin_bf16=False):
    def body(a_ref, b_ref, o_ref, *twin):
        r = lax.dot_general(a_ref[...], b_ref[...], dims, preferred_element_type=F32)
        if acc_axis is None:
            o_ref[...] = r.astype(o_ref.dtype)
            for t_ref in twin:
                t_ref[...] = r.astype(t_ref.dtype)
        else:
            k = pl.program_id(acc_axis)

            @pl.when(k == 0)
            def _():
                o_ref[...] = r

            @pl.when(k > 0)
            def _():
                o_ref[...] += r

    if twin_bf16:
        assert acc_axis is None
        o_spec = [o_spec, o_spec]
        out_shape = [out_shape, jax.ShapeDtypeStruct(out_shape.shape, BF16)]
    return pl.pallas_call(body, name=name, grid=grid, in_specs=[a_spec, b_spec], out_specs=o_spec,
                          out_shape=out_shape, compiler_params=_params(len(grid)))(a, b)


def _dense_fwd(name, a, w, out_dtype=F32):
    m, k = a.shape
    n = w.shape[1]
    tn = 512
    return _matmul(name, NN, a, w, pl.BlockSpec((m, k), lambda j: (0, 0)), pl.BlockSpec((k, tn), lambda j: (0, j)),
                   pl.BlockSpec((m, tn), lambda j: (0, j)), jax.ShapeDtypeStruct((m, n), out_dtype), (n // tn,))


def _dense_dx(name, dy, w, out_dtype):
    m, n = dy.shape
    k = w.shape[0]
    tk = 512
    return _matmul(name, NT, dy, w, pl.BlockSpec((m, n), lambda j: (0, 0)), pl.BlockSpec((tk, n), lambda j: (j, 0)),
                   pl.BlockSpec((m, tk), lambda j: (0, j)), jax.ShapeDtypeStruct((m, k), out_dtype), (k // tk,))


def _dense_dw(name, a, dy):
    m, k = a.shape
    n = dy.shape[1]
    tn = 512
    return _matmul(name, TN, a, dy, pl.BlockSpec((m, k), lambda j: (0, 0)), pl.BlockSpec((m, tn), lambda j: (0, j)),
                   pl.BlockSpec((k, tn), lambda j: (0, j)), jax.ShapeDtypeStruct((k, n), F32), (n // tn,), twin_bf16=True)


def _rowwise(name, fn, row_ins, par_ins, row_outs, par_outs, *, grid, in_specs, out_specs, tm, row_axis):
    n_ri, n_pi, n_ro, n_po = len(row_ins), len(par_ins), len(row_outs), len(par_outs)
    n_steps, tail = divmod(tm, ROW_CHUNK)
    assert tail % 16 == 0

    def body(*refs):
        ri = refs[:n_ri]
        pi = refs[n_ri:n_ri + n_pi]
        ro = refs[n_ri + n_pi:n_ri + n_pi + n_ro]
        po = refs[n_ri + n_pi + n_ro:]
        ps = [r[...] for r in pi]
        base = pl.program_id(row_axis) * tm

        def chunk(r0, rows, carry):
            xs = [r[pl.ds(r0, rows), :] for r in ri]
            outs, pouts = fn(base + r0, xs, ps)
            for r, o in zip(ro, outs):
                r[pl.ds(r0, rows), :] = o.astype(r.dtype)
            return tuple(c + q for c, q in zip(carry, pouts))

        def step(i, carry):
            return chunk(pl.multiple_of(i * ROW_CHUNK, ROW_CHUNK), ROW_CHUNK, carry)

        acc = lax.fori_loop(0, n_steps, step, tuple(jnp.zeros(s.shape, F32) for s in par_outs))
        if tail:
            acc = chunk(n_steps * ROW_CHUNK, tail, acc)
        if n_po:
            first = pl.program_id(0) == 0
            for ax in range(1, len(grid)):
                first = first & (pl.program_id(ax) == 0)

            @pl.when(first)
            def _():
                for r in po:
                    r[...] = jnp.zeros_like(r)

            for r, a in zip(po, acc):
                r[...] += a

    return pl.pallas_call(body, name=name, grid=grid, in_specs=in_specs, out_specs=out_specs,
                          out_shape=tuple(row_outs) + tuple(par_outs),
                          compiler_params=_params(len(grid)))(*row_ins, *par_ins)


def _rows(name, fn, row_ins, par_ins, row_out_dtypes, par_out_widths, row_in_cols=None, row_out_widths=None):
    m = row_ins[0].shape[0]
    tm = _row_tile(m)
    in_specs = []
    for k, a in enumerate(row_ins):
        if row_in_cols is not None and row_in_cols[k] is not None:
            width, cb = row_in_cols[k]
            in_specs.append(pl.BlockSpec((tm, width), lambda i, cb=cb: (i, cb)))
        else:
            in_specs.append(pl.BlockSpec((tm, a.shape[1]), lambda i: (i, 0)))
    for a in par_ins:
        in_specs.append(pl.BlockSpec(a.shape, lambda i: (0, 0)))
    if row_out_widths is None:
        row_out_widths = [row_ins[0].shape[1]] * len(row_out_dtypes)
    row_outs = [jax.ShapeDtypeStruct((m, w), dt) for w, dt in zip(row_out_widths, row_out_dtypes)]
    par_outs = [jax.ShapeDtypeStruct((1, w), F32) for w in par_out_widths]
    out_specs = [pl.BlockSpec((tm, s.shape[1]), lambda i: (i, 0)) for s in row_outs]
    out_specs += [pl.BlockSpec(s.shape, lambda i: (0, 0)) for s in par_outs]
    return _rowwise(name, fn, row_ins, par_ins, row_outs, par_outs, grid=(m // tm,), in_specs=in_specs,
                    out_specs=out_specs, tm=tm, row_axis=0)


def _rms(x, g):
    return x * lax.rsqrt(jnp.mean(x * x, axis=-1, keepdims=True) + RMS_EPS) * g


def _ln_silu(y, g, b):
    mu = jnp.mean(y, axis=-1, keepdims=True)
    yc = y - mu
    var = jnp.mean(yc * yc, axis=-1, keepdims=True)
    return jax.nn.silu(yc * lax.rsqrt(var + LN_EPS) * g + b)


def _gate_mix(pgc, pga, yc, ya, gb):
    d = pgc.shape[1]
    return jax.nn.sigmoid(pgc + gb[:, :d]) * yc + jax.nn.sigmoid(pga + gb[:, d:]) * ya


def _post_mix(h0, mix, g_post, g_pre):
    h1 = h0 + _rms(mix, g_post)
    return h1, _rms(h1, g_pre)


def _swiglu(a, b):
    return jax.nn.silu(a) * b


def _conv_taps():
    taps = []
    for b in range(8):
        for a in range(CONV_PAD // 8):
            s = 8 * a + b
            if s < CONV_WIDTH:
                taps.append((b, a, CONV_WIDTH - 1 - s))
    return taps


def _conv_fwd(p, dww, dwb, d_model):
    m = p.shape[0]
    nch = d_model // LANE
    n_chunk = m // QB
    taps = _conv_taps()

    def body(a_ref, g_ref, w_ref, b_ref, y_ref, upad):
        upad[0:CONV_PAD, :] = jnp.zeros((CONV_PAD, LANE), F32)

        def fill(i, c):
            r0 = pl.multiple_of(i * QB, QB)
            u = a_ref[pl.ds(r0, QB), :] * jax.nn.sigmoid(g_ref[pl.ds(r0, QB), :])
            upad[pl.ds(pl.multiple_of(r0 + CONV_PAD, 8), QB), :] = u
            return c

        lax.fori_loop(0, n_chunk, fill, 0)

        def conv(i, c):
            r0 = pl.multiple_of(i * QB, QB)
            win = upad[pl.ds(r0, QB + CONV_PAD), :]
            acc = jnp.broadcast_to(b_ref[...], (QB, LANE))
            rolled = {}
            for b, a, j in taps:
                if b not in rolled:
                    rolled[b] = win if b == 0 else pltpu.roll(win, b, axis=0)
                lo = CONV_PAD - 8 * a
                acc = acc + w_ref[j:j + 1, :] * rolled[b][lo:lo + QB, :]
            y_ref[pl.ds(r0, QB), :] = acc
            return c

        lax.fori_loop(0, n_chunk, conv, 0)

    col = lambda off: pl.BlockSpec((m, LANE), lambda c: (0, off + c))
    return pl.pallas_call(
        body, name="conv_fwd", grid=(nch,),
        in_specs=[col(0), col(nch), pl.BlockSpec((CONV_PAD, LANE), lambda c: (0, c)), pl.BlockSpec((1, LANE), lambda c: (0, c))],
        out_specs=col(0), out_shape=jax.ShapeDtypeStruct((m, d_model), F32),
        scratch_shapes=[pltpu.VMEM((m + CONV_PAD, LANE), F32)], compiler_params=_params(1))(p, p, dww, dwb)


def _conv_bwd(p, dy, dww, d_model):
    m = p.shape[0]
    nch = d_model // LANE
    n_chunk = m // QB
    taps = _conv_taps()
    win_rows = QB + CONV_PAD

    def body(a_ref, g_ref, dy_ref, w_ref, da_ref, dg_ref, dw_ref, db_ref, upad, dypad, wacc, bacc):
        upad[0:CONV_PAD, :] = jnp.zeros((CONV_PAD, LANE), F32)
        dypad[m:m + CONV_PAD, :] = jnp.zeros((CONV_PAD, LANE), F32)
        wacc[...] = jnp.zeros_like(wacc)
        bacc[...] = jnp.zeros_like(bacc)

        def fill(i, c):
            r0 = pl.multiple_of(i * QB, QB)
            u = a_ref[pl.ds(r0, QB), :] * jax.nn.sigmoid(g_ref[pl.ds(r0, QB), :])
            upad[pl.ds(pl.multiple_of(r0 + CONV_PAD, 8), QB), :] = u
            dypad[pl.ds(r0, QB), :] = dy_ref[pl.ds(r0, QB), :]
            return c

        lax.fori_loop(0, n_chunk, fill, 0)

        def chunk(i, c):
            r0 = pl.multiple_of(i * QB, QB)
            dwin = dypad[pl.ds(r0, win_rows), :]
            du = jnp.zeros((QB, LANE), F32)
            rolled = {}
            for b, a, j in taps:
                if b not in rolled:
                    rolled[b] = dwin if b == 0 else pltpu.roll(dwin, win_rows - b, axis=0)
                du = du + w_ref[j:j + 1, :] * rolled[b][8 * a:8 * a + QB, :]
            av = a_ref[pl.ds(r0, QB), :]
            sg = jax.nn.sigmoid(g_ref[pl.ds(r0, QB), :])
            da_ref[pl.ds(r0, QB), :] = (du * sg).astype(da_ref.dtype)
            dg_ref[pl.ds(r0, QB), :] = (du * av * sg * (1.0 - sg)).astype(dg_ref.dtype)
            dyc = dy_ref[pl.ds(r0, QB), :]
            uwin = upad[pl.ds(r0, win_rows), :]
            rolled = {}
            for b, a, j in taps:
                if b not in rolled:
                    rolled[b] = uwin if b == 0 else pltpu.roll(uwin, b, axis=0)
                lo = CONV_PAD - 8 * a
                prod = dyc * rolled[b][lo:lo + QB, :]
                wacc[j] += prod.reshape(QB // 8, 8, LANE).sum(axis=0)
            bacc[...] += dyc.reshape(QB // 8, 8, LANE).sum(axis=0)
            return c

        lax.fori_loop(0, n_chunk, chunk, 0)
        for j in range(CONV_WIDTH):
            dw_ref[j:j + 1, :] = jnp.sum(wacc[j], axis=0, keepdims=True)
        dw_ref[CONV_WIDTH:CONV_PAD, :] = jnp.zeros((CONV_PAD - CONV_WIDTH, LANE), F32)
        db_ref[...] = jnp.sum(bacc[...], axis=0, keepdims=True)

    col = lambda off: pl.BlockSpec((m, LANE), lambda c: (0, off + c))
    return pl.pallas_call(
        body, name="conv_bwd", grid=(nch,),
        in_specs=[col(0), col(nch), col(0), pl.BlockSpec((CONV_PAD, LANE), lambda c: (0, c))],
        out_specs=[col(0), col(0), pl.BlockSpec((CONV_PAD, LANE), lambda c: (0, c)), pl.BlockSpec((1, LANE), lambda c: (0, c))],
        out_shape=(jax.ShapeDtypeStruct((m, d_model), BF16), jax.ShapeDtypeStruct((m, d_model), BF16),
                   jax.ShapeDtypeStruct((CONV_PAD, d_model), F32), jax.ShapeDtypeStruct((1, d_model), F32)),
        scratch_shapes=[pltpu.VMEM((m + CONV_PAD, LANE), F32), pltpu.VMEM((m + CONV_PAD, LANE), F32),
                        pltpu.VMEM((CONV_PAD, 8, LANE), F32), pltpu.VMEM((8, LANE), F32)],
        compiler_params=_params(1))(p, p, dy, dww)


EXP_CLAMP = 80.0


def _one_plus_exp(z):
    return 1.0 + jnp.exp(jnp.minimum(z, EXP_CLAMP))


def _softplus(z):
    return jnp.maximum(jnp.log(_one_plus_exp(z)), z)


def _softplus_sigmoid(z):
    s = _one_plus_exp(z)
    return jnp.maximum(jnp.log(s), z), 1.0 - 1.0 / s


def _split_dot(x, tri):
    hi = pltpu.bitcast(pltpu.bitcast(x, jnp.uint32) & jnp.uint32(0xFFFF0000), F32)
    lo = x - hi
    return jnp.dot(jnp.concatenate([hi.astype(BF16), lo.astype(BF16)], axis=1), tri, preferred_element_type=F32)


def _tri(kind):
    jj = lax.broadcasted_iota(I32, (2 * QB, 2 * QB), 0) & (QB - 1)
    ss = lax.broadcasted_iota(I32, (2 * QB, 2 * QB), 1)
    keep = {"ge": jj >= ss, "lt": jj < ss, "le": jj <= ss}[kind]
    return jnp.where((ss >= QB) | keep, 1.0, 0.0).astype(BF16)


def _attn_fwd(p, d_model, shards):
    m = p.shape[0]
    nqb = m // QB
    ngrp = d_model // (LANE_BLOCKS * LANE)
    qo, ko, vo = 2 * ngrp, 3 * ngrp, 4 * ngrp
    scale = HEAD_DIM ** -0.5
    n_sh = len(shards)

    assert nqb >= KEY_TILES

    def body(q_ref, k_ref, v_ref, *rest):
        shard_refs, (o_ref, t_ref), rest = rest[:n_sh], rest[n_sh:n_sh + 2], rest[n_sh + 2:]
        gathered_refs, (acc_ref, car_ref), sems = rest[:n_sh], rest[n_sh:n_sh + 2], rest[n_sh + 2:]
        starts, waits = _gather_copies(shard_refs, gathered_refs, *sems)

        @pl.when((pl.program_id(0) == 0) & (pl.program_id(1) == 0))
        def _():
            for f in starts:
                f()

        qb = pl.program_id(1)
        lane = lax.broadcasted_iota(I32, (QB, LANE), 1)
        head0 = lane < HEAD_DIM
        row_g = qb * QB + lax.broadcasted_iota(I32, (QB, KEY_CHUNK), 0)
        col_l = lax.broadcasted_iota(I32, (QB, KEY_CHUNK), 1)
        tri = _tri("ge")
        heads = range(2 * LANE_BLOCKS)
        qh = []
        for lb in range(LANE_BLOCKS):
            q2 = (q_ref[:, lb * LANE:(lb + 1) * LANE] * scale).astype(BF16)
            zero = jnp.zeros_like(q2)
            qh += [jnp.where(head0, q2, zero), jnp.where(head0, zero, q2)]
        acc_ref[...] = jnp.zeros_like(acc_ref)
        car_ref[...] = jnp.zeros_like(car_ref)

        def chunk(first_tile, bound):
            r0 = pl.multiple_of(first_tile * QB, QB)
            kcs = [k_ref[pl.ds(r0, KEY_CHUNK), lb * LANE:(lb + 1) * LANE].astype(BF16) for lb in range(LANE_BLOCKS)]
            vcs = [v_ref[pl.ds(r0, KEY_CHUNK), lb * LANE:(lb + 1) * LANE].astype(BF16) for lb in range(LANE_BLOCKS)]
            valid = None if bound is None else (col_l + r0) < bound
            zs = [lax.dot_general(qh[h], kcs[h // 2], NT, preferred_element_type=F32) for h in heads]
            sps = [_softplus(z) for z in zs]
            if valid is not None:
                sps = [jnp.where(valid, sp, 0.0) for sp in sps]
            crs = [[_split_dot(sp[:, i * QB:(i + 1) * QB], tri) for i in range(KEY_TILES)] for sp in sps]
            cars = [car_ref[h] for h in heads]
            a_tiles = [[None] * KEY_TILES for h in heads]
            for i in reversed(range(KEY_TILES)):
                for h in heads:
                    a_tiles[h][i] = jnp.exp(zs[h][:, i * QB:(i + 1) * QB] - (crs[h][i][:, :QB] + cars[h]))
                    cars[h] = cars[h] + crs[h][i][:, QB:]
            for h in heads:
                a = jnp.concatenate(a_tiles[h], axis=1)
                if valid is not None:
                    a = jnp.where(valid, a, 0.0)
                acc_ref[h] += jnp.dot(a.astype(BF16), vcs[h // 2], preferred_element_type=F32)
                car_ref[h] = cars[h]

        near = jnp.maximum(qb - (KEY_TILES - 1), 0)
        chunk(near, row_g)
        n_full = lax.shift_right_logical(near, KEY_SHIFT)

        def step(i, c):
            chunk(near - KEY_TILES * (i + 1), None)
            return c

        lax.fori_loop(0, n_full, step, 0)
        left = near - KEY_TILES * n_full

        @pl.when(left > 0)
        def _():
            chunk(0, left * QB)

        for lb in range(LANE_BLOCKS):
            o_ref[:, lb * LANE:(lb + 1) * LANE] = jnp.where(head0, acc_ref[2 * lb], acc_ref[2 * lb + 1]).astype(o_ref.dtype)
        for h in heads:
            t_ref[:, h * QB:(h + 1) * QB] = car_ref[h]

        @pl.when((pl.program_id(0) == ngrp - 1) & (pl.program_id(1) == nqb - 1))
        def _():
            for f in waits:
                f()

    wide = LANE_BLOCKS * LANE
    outs = pl.pallas_call(
        body, name="attn_fwd", grid=(ngrp, nqb),
        in_specs=[pl.BlockSpec((QB, wide), lambda g, qb: (qb, qo + g)),
                  pl.BlockSpec((m, wide), lambda g, qb: (0, ko + g)),
                  pl.BlockSpec((m, wide), lambda g, qb: (0, vo + g))] + [ANY] * n_sh,
        out_specs=[pl.BlockSpec((QB, wide), lambda g, qb: (qb, g)),
                   pl.BlockSpec((QB, 2 * wide), lambda g, qb: (qb, g))] + [ANY] * n_sh,
        out_shape=[jax.ShapeDtypeStruct((m, d_model), BF16), jax.ShapeDtypeStruct((m, 2 * d_model), F32)]
        + [jax.ShapeDtypeStruct((N_DEV,) + s.shape, s.dtype) for s in shards],
        scratch_shapes=[pltpu.VMEM((2 * LANE_BLOCKS, QB, LANE), F32), pltpu.VMEM((2 * LANE_BLOCKS, QB, LANE), F32)]
        + _exchange_sems(n_sh) + [pltpu.SemaphoreType.DMA((n_sh,))],
        compiler_params=_params(2))(p, p, p, *shards)
    return outs[0], outs[1], outs[2:]


def _attn_bwd(p, d_o, tot, d_model, slabs):
    m = p.shape[0]
    nqb = m // QB
    blocks = BWD_LANE_BLOCKS
    ngrp = d_model // (blocks * LANE)
    qo, ko, vo = 2 * ngrp, 3 * ngrp, 4 * ngrp
    scale = HEAD_DIM ** -0.5
    n_sl = len(slabs)

    assert nqb >= KEY_TILES

    def body(q_ref, k_ref, v_ref, do_ref, t_ref, *rest):
        slab_refs, (dq_ref, dk_ref, dv_ref), rest = rest[:n_sl], rest[n_sl:n_sl + 3], rest[n_sl + 3:]
        recv_refs, (dkacc, dvacc, dqacc, csp, cg), sems = rest[:n_sl], rest[n_sl:n_sl + 5], rest[n_sl + 5:]
        starts, waits = _scatter_copies(slab_refs, recv_refs, *sems)

        @pl.when((pl.program_id(0) == 0) & (pl.program_id(1) == 0))
        def _():
            for f in starts:
                f()

        qb = pl.program_id(1)
        lane = lax.broadcasted_iota(I32, (QB, LANE), 1)
        head0 = lane < HEAD_DIM
        row_g = qb * QB + lax.broadcasted_iota(I32, (QB, KEY_CHUNK), 0)
        col_l = lax.broadcasted_iota(I32, (QB, KEY_CHUNK), 1)
        tri_lt = _tri("lt")
        tri_le = _tri("le")
        heads = range(2 * blocks)
        qh, doh = [], []
        for lb in range(blocks):
            q2 = (q_ref[:, lb * LANE:(lb + 1) * LANE] * scale).astype(BF16)
            do2 = do_ref[:, lb * LANE:(lb + 1) * LANE]
            zero = jnp.zeros_like(q2)
            qh += [jnp.where(head0, q2, zero), jnp.where(head0, zero, q2)]
            doh += [jnp.where(head0, do2, zero), jnp.where(head0, zero, do2)]
        q_pairs = [jnp.concatenate(qh[2 * lb:2 * lb + 2], axis=0) for lb in range(blocks)]
        do_pairs = [jnp.concatenate(doh[2 * lb:2 * lb + 2], axis=0) for lb in range(blocks)]

        @pl.when(qb == 0)
        def _():
            dkacc[...] = jnp.zeros_like(dkacc)
            dvacc[...] = jnp.zeros_like(dvacc)

        dqacc[...] = jnp.zeros_like(dqacc)
        csp[...] = jnp.zeros_like(csp)
        cg[...] = jnp.zeros_like(cg)

        def chunk(first_tile, bound):
            r0 = pl.multiple_of(first_tile * QB, QB)
            kcs = [k_ref[pl.ds(r0, KEY_CHUNK), lb * LANE:(lb + 1) * LANE].astype(BF16) for lb in range(blocks)]
            vcs = [v_ref[pl.ds(r0, KEY_CHUNK), lb * LANE:(lb + 1) * LANE].astype(BF16) for lb in range(blocks)]
            valid = None if bound is None else (col_l + r0) < bound
            tiles = [slice(i * QB, (i + 1) * QB) for i in range(KEY_TILES)]
            zs = [lax.dot_general(qh[h], kcs[h // 2], NT, preferred_element_type=F32) for h in heads]
            das = [lax.dot_general(doh[h], vcs[h // 2], NT, preferred_element_type=F32) for h in heads]
            sps, sgs = zip(*[_softplus_sigmoid(z) for z in zs])
            if valid is not None:
                sps = [jnp.where(valid, sp, 0.0) for sp in sps]
            crs = [[_split_dot(sp[:, c], tri_lt) for c in tiles] for sp in sps]
            a_tiles, g_tiles = [[] for h in heads], [[] for h in heads]
            for h in heads:
                tot_h = t_ref[:, h * QB:(h + 1) * QB]
                before = csp[h]
                for i, c in enumerate(tiles):
                    a = jnp.exp(zs[h][:, c] - (tot_h - (before + crs[h][i][:, :QB])))
                    if valid is not None:
                        a = jnp.where(valid[:, c], a, 0.0)
                    a_tiles[h].append(a)
                    g_tiles[h].append(a * das[h][:, c])
                    before = before + crs[h][i][:, QB:]
                csp[h] = before
            grs = [[_split_dot(g, tri_le) for g in g_tiles[h]] for h in heads]
            dzbs, abs_ = [], []
            for h in heads:
                g_before = cg[h]
                dz_tiles = []
                for i, c in enumerate(tiles):
                    dz = g_tiles[h][i] - sgs[h][:, c] * (g_before + grs[h][i][:, :QB])
                    if valid is not None:
                        dz = jnp.where(valid[:, c], dz, 0.0)
                    dz_tiles.append(dz)
                    g_before = g_before + grs[h][i][:, QB:]
                cg[h] = g_before
                dzbs.append(jnp.concatenate(dz_tiles, axis=1).astype(BF16))
                abs_.append(jnp.concatenate(a_tiles[h], axis=1).astype(BF16))
            for h in heads:
                dqacc[h] += jnp.dot(dzbs[h], kcs[h // 2], preferred_element_type=F32)
            for lb in range(blocks):
                dz_pair = jnp.concatenate(dzbs[2 * lb:2 * lb + 2], axis=0)
                a_pair = jnp.concatenate(abs_[2 * lb:2 * lb + 2], axis=0)
                dkacc[pl.ds(r0, KEY_CHUNK), lb * LANE:(lb + 1) * LANE] += lax.dot_general(
                    dz_pair, q_pairs[lb], TN, preferred_element_type=F32)
                dvacc[pl.ds(r0, KEY_CHUNK), lb * LANE:(lb + 1) * LANE] += lax.dot_general(
                    a_pair, do_pairs[lb], TN, preferred_element_type=F32)

        near = jnp.maximum(qb - (KEY_TILES - 1), 0)
        n_full = lax.shift_right_logical(near, KEY_SHIFT)

        def step(i, c):
            chunk(KEY_TILES * i, None)
            return c

        lax.fori_loop(0, n_full, step, 0)

        @pl.when(near > KEY_TILES * n_full)
        def _():
            chunk(KEY_TILES * n_full, near * QB)

        chunk(near, row_g)
        for lb in range(blocks):
            dq2 = jnp.where(head0, dqacc[2 * lb], dqacc[2 * lb + 1]) * scale
            dq_ref[:, lb * LANE:(lb + 1) * LANE] = dq2.astype(dq_ref.dtype)

        @pl.when(qb == nqb - 1)
        def _():
            dk_ref[...] = dkacc[...].astype(dk_ref.dtype)
            dv_ref[...] = dvacc[...].astype(dv_ref.dtype)

        @pl.when((pl.program_id(0) == ngrp - 1) & (qb == nqb - 1))
        def _():
            for f in waits:
                f()

    out = jax.ShapeDtypeStruct((m, d_model), BF16)
    wide = blocks * LANE
    carry = pltpu.VMEM((2 * blocks, QB, LANE), F32)
    outs = pl.pallas_call(
        body, name="attn_bwd", grid=(ngrp, nqb),
        in_specs=[pl.BlockSpec((QB, wide), lambda g, qb: (qb, qo + g)),
                  pl.BlockSpec((m, wide), lambda g, qb: (0, ko + g)),
                  pl.BlockSpec((m, wide), lambda g, qb: (0, vo + g)),
                  pl.BlockSpec((QB, wide), lambda g, qb: (qb, g)),
                  pl.BlockSpec((QB, 2 * wide), lambda g, qb: (qb, g))] + [ANY] * n_sl,
        out_specs=[pl.BlockSpec((QB, wide), lambda g, qb: (qb, g)),
                   pl.BlockSpec((m, wide), lambda g, qb: (0, g)),
                   pl.BlockSpec((m, wide), lambda g, qb: (0, g))] + [ANY] * n_sl,
        out_shape=[out, out, out] + _received_shapes(slabs),
        scratch_shapes=[pltpu.VMEM((m, wide), F32), pltpu.VMEM((m, wide), F32), carry, carry, carry] + _exchange_sems(n_sl),
        compiler_params=_params(2))(p, p, p, d_o, tot, *slabs)
    return outs[0], outs[1], outs[2], outs[3:]


def _mesh_pos():
    return lax.axis_index("x"), lax.axis_index("y"), lax.axis_index("c")


def _other_chips(x, y):
    return [(1 - x, y), (x, 1 - y), (1 - x, 1 - y)]


def _dev(x, y, c):
    return 4 * x + 2 * y + c


def _all_gather(shards):
    n = len(shards)

    def body(*refs):
        ins, outs = refs[:n], refs[n:2 * n]
        send_sems, recv_sems, local_sems = refs[2 * n:]
        x, y, c = _mesh_pos()
        sibling = (x, y, 1 - c)
        chips = _other_chips(x, y)

        def copy(k, s, src, block, to):
            return pltpu.make_async_remote_copy(src_ref=src, dst_ref=outs[k].at[_dev(*block)], send_sem=send_sems.at[k, s],
                                                recv_sem=recv_sems.at[k, s], device_id=to, device_id_type=MESH)

        def held(k, block):
            return outs[k].at[_dev(*block)]

        mine = [pltpu.make_async_copy(ins[k], held(k, (x, y, c)), local_sems.at[k]) for k in range(n)]
        for cp in mine:
            cp.start()
        first = []
        for k in range(n):
            first.append(copy(k, 0, ins[k], (x, y, c), sibling))
            for j, chip in enumerate(chips):
                first.append(copy(k, 1 + j, ins[k], (x, y, c), (*chip, c)))
        for cp in first:
            cp.start()
        passed = []
        for j, chip in enumerate(chips):
            for k in range(n):
                copy(k, 1 + j, held(k, (*chip, c)), (*chip, c), (x, y, c)).wait_recv()
                fwd = copy(k, 4 + j, held(k, (*chip, c)), (*chip, c), sibling)
                fwd.start()
                passed.append(fwd)
        for k in range(n):
            copy(k, 0, held(k, (x, y, 1 - c)), (x, y, 1 - c), (x, y, c)).wait_recv()
            for j, chip in enumerate(chips):
                copy(k, 4 + j, held(k, (*chip, 1 - c)), (*chip, 1 - c), (x, y, c)).wait_recv()
        for cp in first + passed:
            cp.wait_send()
        for cp in mine:
            cp.wait()

    return pl.pallas_call(
        body, name="comm_all_gather", in_specs=[ANY] * n, out_specs=[ANY] * n,
        out_shape=[jax.ShapeDtypeStruct((N_DEV,) + s.shape, s.dtype) for s in shards],
        scratch_shapes=[pltpu.SemaphoreType.DMA((n, 7)), pltpu.SemaphoreType.DMA((n, 7)), pltpu.SemaphoreType.DMA((n,))],
    )(*shards)


def _peers(x, y, c):
    out = []
    for mask in range(1, N_DEV):
        px, py, pc = x ^ (mask >> 2), y ^ ((mask >> 1) & 1), c ^ (mask & 1)
        out.append((mask - 1, (px, py, pc), _dev(px, py, pc)))
    return out


def _remote(src, dst, send_sems, recv_sems, k, s, peer):
    return pltpu.make_async_remote_copy(src_ref=src, dst_ref=dst, send_sem=send_sems.at[k, s], recv_sem=recv_sems.at[k, s],
                                        device_id=peer, device_id_type=MESH)


def _gather_copies(ins, outs, send_sems, recv_sems, local_sems):
    x, y, c = _mesh_pos()
    me = _dev(x, y, c)
    starts, waits = [], []
    for k in range(len(ins)):
        own = pltpu.make_async_copy(ins[k], outs[k].at[me], local_sems.at[k])
        starts.append(own.start)
        waits.append(own.wait)
        for s, peer, idx in _peers(x, y, c):
            send = _remote(ins[k], outs[k].at[me], send_sems, recv_sems, k, s, peer)
            land = _remote(ins[k], outs[k].at[idx], send_sems, recv_sems, k, s, peer)
            starts.append(send.start)
            waits += [land.wait_recv, send.wait_send]
    return starts, waits


def _scatter_copies(ins, outs, send_sems, recv_sems):
    x, y, c = _mesh_pos()
    starts, waits = [], []
    for k in range(len(ins)):
        for s, peer, idx in _peers(x, y, c):
            send = _remote(ins[k].at[idx], outs[k].at[s], send_sems, recv_sems, k, s, peer)
            starts.append(send.start)
            waits += [send.wait_recv, send.wait_send]
    return starts, waits


def _exchange_sems(n):
    return [pltpu.SemaphoreType.DMA((n, N_DEV - 1)), pltpu.SemaphoreType.DMA((n, N_DEV - 1))]


def _received_shapes(slabs):
    return [jax.ShapeDtypeStruct((N_DEV - 1,) + a.shape[1:], a.dtype) for a in slabs]


def _scatter_exchange(slabs):
    n = len(slabs)

    def body(*refs):
        starts, waits = _scatter_copies(refs[:n], refs[n:2 * n], *refs[2 * n:])
        for f in starts + waits:
            f()

    return pl.pallas_call(body, name="comm_scatter_exchange", in_specs=[ANY] * n, out_specs=[ANY] * n,
                          out_shape=_received_shapes(slabs), scratch_shapes=_exchange_sems(n))(*slabs)


def _shard_tile(rows):
    for tr in range(min(rows, 352), 0, -1):
        if rows % tr == 0 and (tr % 16 == 0 or tr == rows):
            return tr


def _adamw_math(w, g, m, v):
    m = ADAM_B1 * m + (1.0 - ADAM_B1) * g
    v = ADAM_B2 * v + (1.0 - ADAM_B2) * (g * g)
    m_hat = m / (1.0 - ADAM_B1 ** ADAM_STEP)
    v_hat = v / (1.0 - ADAM_B2 ** ADAM_STEP)
    delta = -ADAM_LR * (m_hat / (jnp.sqrt(v_hat) + ADAM_EPS) + ADAM_WD * w)
    return delta, m, v


def _adamw_shard(me, grad, received, w, m, v):
    rows, cols = w.shape
    tr = _shard_tile(rows)

    def body(me_ref, g_ref, r_ref, w_ref, m_ref, v_ref, go_ref, do_ref, mo_ref, vo_ref):
        g = g_ref[...]
        for s in range(N_DEV - 1):
            g = g + r_ref[s].astype(F32)
        delta, m_new, v_new = _adamw_math(w_ref[...], g, m_ref[...], v_ref[...])
        go_ref[...] = g
        do_ref[...] = delta
        mo_ref[...] = m_new
        vo_ref[...] = v_new

    flat = pl.BlockSpec((tr, cols), lambda i, me: (i, 0))
    gs = pltpu.PrefetchScalarGridSpec(
        num_scalar_prefetch=1, grid=(rows // tr,),
        in_specs=[pl.BlockSpec((None, tr, cols), lambda i, me: (me[0], i, 0)),
                  pl.BlockSpec((N_DEV - 1, tr, cols), lambda i, me: (0, i, 0)), flat, flat, flat],
        out_specs=[flat, flat, flat, flat])
    out = jax.ShapeDtypeStruct((rows, cols), F32)
    return pl.pallas_call(body, name="adamw_shard", grid_spec=gs, out_shape=(out, out, out, out),
                          compiler_params=_params(1))(me, grad, received, w, m, v)


def _small_reduce_adamw(slabs, w, m, v):
    _, rows, _ = slabs.shape

    def body(s_ref, w_ref, m_ref, v_ref, g_ref, d_ref, mo_ref, vo_ref, land, send_sems, recv_sems):
        x, y, c = _mesh_pos()
        me = _dev(x, y, c)
        copies = []
        for mask in range(1, N_DEV):
            px, py, pc = x ^ (mask >> 2), y ^ ((mask >> 1) & 1), c ^ (mask & 1)
            copies.append(pltpu.make_async_remote_copy(
                src_ref=s_ref.at[_dev(px, py, pc)], dst_ref=land.at[me], send_sem=send_sems.at[mask - 1],
                recv_sem=recv_sems.at[mask - 1], device_id=(px, py, pc), device_id_type=MESH))
        for cp in copies:
            cp.start()
        land[me] = s_ref[me]
        for mask in range(1, N_DEV):
            px, py, pc = x ^ (mask >> 2), y ^ ((mask >> 1) & 1), c ^ (mask & 1)
            pltpu.make_async_remote_copy(
                src_ref=s_ref.at[me], dst_ref=land.at[_dev(px, py, pc)], send_sem=send_sems.at[mask - 1],
                recv_sem=recv_sems.at[mask - 1], device_id=(px, py, pc), device_id_type=MESH).wait_recv()
        for cp in copies:
            cp.wait_send()
        g = land[0]
        for d in range(1, N_DEV):
            g = g + land[d]
        delta, m_new, v_new = _adamw_math(w_ref[...], g, m_ref[...], v_ref[...])
        g_ref[...] = g
        d_ref[...] = delta
        mo_ref[...] = m_new
        vo_ref[...] = v_new

    out = jax.ShapeDtypeStruct((rows, LANE), F32)
    return pl.pallas_call(
        body, name="comm_small_reduce_adamw", in_specs=[VMEM_WHOLE] * 4, out_specs=[VMEM_WHOLE] * 4, out_shape=(out, out, out, out),
        scratch_shapes=[pltpu.VMEM((N_DEV, rows, LANE), F32), pltpu.SemaphoreType.DMA((N_DEV - 1,)),
                        pltpu.SemaphoreType.DMA((N_DEV - 1,))],
    )(slabs, w, m, v)


def _cast_bf16(arrs):
    n = len(arrs)

    def body(*refs):
        for i_ref, o_ref in zip(refs[:n], refs[n:]):
            o_ref[...] = i_ref[...].astype(BF16)

    return pl.pallas_call(body, name="cast_bf16", in_specs=[VMEM_WHOLE] * n, out_specs=[VMEM_WHOLE] * n,
                          out_shape=[jax.ShapeDtypeStruct(a.shape, BF16) for a in arrs],
                          compiler_params=pltpu.CompilerParams(vmem_limit_bytes=VMEM_LIMIT))(*arrs)


REPLICATED = ("pre_mix_g", "gate_b", "dw_b", "conv_ln_g", "conv_ln_b", "post_mix_g", "pre_ffn_g", "post_ffn_g")
SHARDED = ("w_in", "w_conv_out", "w_attn_out", "w_o", "w_ffn_in", "w_ffn_out")
WEIGHTS = ("meta_tokens", "pre_mix_g", "w_in", "gate_b", "dw_w", "dw_b", "conv_ln_g", "conv_ln_b", "w_conv_out",
           "w_attn_out", "w_o", "post_mix_g", "pre_ffn_g", "w_ffn_in", "w_ffn_out", "post_ffn_g")


def kernel(x, meta_tokens, pre_mix_g, w_in, gate_b, dw_w, dw_b, conv_ln_g, conv_ln_b, w_conv_out, w_attn_out, w_o, post_mix_g, pre_ffn_g, w_ffn_in, w_ffn_out, post_ffn_g, loss_target, m_meta_tokens, m_pre_mix_g, m_w_in, m_gate_b, m_dw_w, m_dw_b, m_conv_ln_g, m_conv_ln_b, m_w_conv_out, m_w_attn_out, m_w_o, m_post_mix_g, m_pre_ffn_g, m_w_ffn_in, m_w_ffn_out, m_post_ffn_g, v_meta_tokens, v_pre_mix_g, v_w_in, v_gate_b, v_dw_w, v_dw_b, v_conv_ln_g, v_conv_ln_b, v_w_conv_out, v_w_attn_out, v_w_o, v_post_mix_g, v_pre_ffn_g, v_w_ffn_in, v_w_ffn_out, v_post_ffn_g):
    given = dict(locals())
    seq, d = x.shape[1], x.shape[2]
    n_meta = meta_tokens.shape[0]
    length = n_meta + seq
    m_rows = -(-length // QB) * QB
    dc = d // N_DEV
    assert dc == LANE and n_meta % 8 == 0 and seq % 8 == 0
    fs = w_ffn_in.shape[2]
    fr = w_ffn_out.shape[1]
    assert 2 * fr == fs

    local = {k: given[k][0] for k in SHARDED}
    cast = _cast_bf16([local[k] for k in SHARDED])
    dww_pad = jnp.pad(dw_w[0], ((0, CONV_PAD - CONV_WIDTH), (0, 0)))
    wi, meta_g, dww_g = _all_gather([cast[0], meta_tokens, dww_pad])
    meta_full = jnp.concatenate([meta_g[j] for j in range(N_DEV)], axis=1)
    dww_full = jnp.concatenate([dww_g[j] for j in range(N_DEV)], axis=1)
    ns = wi.shape[2]

    tail = jnp.zeros((m_rows - length, d), F32)
    h0 = jnp.concatenate([meta_full, x[0], tail], axis=0)
    target = jnp.concatenate([jnp.zeros((n_meta, d), F32), loss_target[0], tail], axis=0)

    (u,) = _rows("pre_mix_norm", lambda r0, xs, ps: ([_rms(xs[0], ps[0])], []), [h0], [pre_mix_g], [BF16], [])
    p = _matmul("in_proj", NN, u, wi, pl.BlockSpec((m_rows, d), lambda i: (0, 0)), pl.BlockSpec((None, d, ns), lambda i: (i, 0, 0)),
                pl.BlockSpec((m_rows, ns), lambda i: (0, i)), jax.ShapeDtypeStruct((m_rows, N_DEV * ns), F32), (N_DEV,))
    o, tot, gathered = _attn_fwd(p, d, list(cast[1:]))
    wco, wao, wo = (g.reshape(d, d) for g in gathered[0:3])
    wfi = gathered[3]
    wfo = gathered[4].reshape(N_DEV // 2, fs, d)
    y = _conv_fwd(p, dww_full, dw_b, d)
    (yc,) = _rows("conv_norm", lambda r0, xs, ps: ([_ln_silu(xs[0], ps[0], ps[1])], []), [y], [conv_ln_g, conv_ln_b], [BF16], [])
    y_conv = _dense_fwd("conv_out", yc, wco)
    y_attn = _dense_fwd("attn_out", o, wao)
    gate_cols = [(d, 5), (d, 6), None, None]
    (mixin,) = _rows("gate_mix", lambda r0, xs, ps: ([_gate_mix(*xs, ps[0])], []), [p, p, y_conv, y_attn], [gate_b], [BF16], [],
                     row_in_cols=gate_cols, row_out_widths=[d])
    mix = _dense_fwd("mix_out", mixin, wo)
    h1, u2 = _rows("post_mix", lambda r0, xs, ps: (list(_post_mix(xs[0], xs[1], ps[0], ps[1])), []), [h0, mix],
                   [post_mix_g, pre_ffn_g], [F32, BF16], [])
    ab = _matmul("ffn_in", NN, u2, wfi, pl.BlockSpec((m_rows, d), lambda i: (0, 0)), pl.BlockSpec((None, d, fs), lambda i: (i, 0, 0)),
                 pl.BlockSpec((None, m_rows, fs), lambda i: (i, 0, 0)), jax.ShapeDtypeStruct((N_DEV, m_rows, fs), F32), (N_DEV,))
    half = N_DEV // 2
    tm = _row_tile(m_rows)
    pair = lambda off: pl.BlockSpec((None, tm, fs), lambda j, i, off=off: (j + off, i, 0))
    (f_in,) = _rowwise("swiglu", lambda r0, xs, ps: ([_swiglu(xs[0], xs[1])], []), [ab, ab], [],
                       [jax.ShapeDtypeStruct((half, m_rows, fs), BF16)], [], grid=(half, m_rows // tm),
                       in_specs=[pair(0), pair(half)], out_specs=[pair(0)], tm=tm, row_axis=1)
    f = _matmul("ffn_out", NN, f_in, wfo, pl.BlockSpec((None, m_rows, fs), lambda j: (j, 0, 0)), pl.BlockSpec((None, fs, d), lambda j: (j, 0, 0)),
                pl.BlockSpec((m_rows, d), lambda j: (0, 0)), jax.ShapeDtypeStruct((m_rows, d), F32), (half,), acc_axis=0)

    def loss_head(r0, xs, ps):
        h1_, f_, t_ = xs
        r, vjp = jax.vjp(_rms, f_, ps[0])
        rows = r0 + lax.broadcasted_iota(I32, (h1_.shape[0], 1), 0)
        real = (rows >= n_meta) & (rows < length)
        err = jnp.where(real, h1_ + r - t_, 0.0)
        dh2 = err * (1.0 / d)
        d_f, dg = vjp(dh2)
        part = jnp.sum(0.5 * jnp.mean(err * err, axis=-1, keepdims=True), axis=0, keepdims=True)
        return [d_f, dh2], [dg, jnp.broadcast_to(part, (1, LANE))]

    d_f, dh2, g_post_ffn, loss_part = _rows("loss_head", loss_head, [h1, f, target], [post_ffn_g], [BF16, F32], [d, LANE])

    d_fin = _matmul("ffn_out_dx", NT, d_f, wfo, pl.BlockSpec((m_rows, d), lambda j: (0, 0)), pl.BlockSpec((None, fs, d), lambda j: (j, 0, 0)),
                    pl.BlockSpec((None, m_rows, fs), lambda j: (j, 0, 0)), jax.ShapeDtypeStruct((half, m_rows, fs), F32), (half,))
    g_wfo = _matmul("ffn_out_dw", TN, f_in, d_f, pl.BlockSpec((None, m_rows, fs), lambda j: (j, 0, 0)), pl.BlockSpec((m_rows, d), lambda j: (0, 0)),
                    pl.BlockSpec((None, fs, d), lambda j: (j, 0, 0)), jax.ShapeDtypeStruct((half, fs, d), F32), (half,), twin_bf16=True)

    def swiglu_bwd(r0, xs, ps):
        _, vjp = jax.vjp(_swiglu, xs[0], xs[1])
        return list(vjp(xs[2])), []

    out8 = lambda off: pl.BlockSpec((None, tm, fs), lambda j, i, off=off: (j + off, i, 0))
    d_a, d_b = _rowwise("swiglu_bwd", swiglu_bwd, [ab, ab, d_fin], [],
                        [jax.ShapeDtypeStruct((half, m_rows, fs), BF16)] * 2, [], grid=(half, m_rows // tm),
                        in_specs=[pair(0), pair(half), pair(0)], out_specs=[out8(0), out8(0)], tm=tm, row_axis=1)
    d_ab = jnp.concatenate([d_a, d_b], axis=0)
    du2 = _matmul("ffn_in_dx", NT, d_ab, wfi, pl.BlockSpec((None, m_rows, fs), lambda i: (i, 0, 0)), pl.BlockSpec((None, d, fs), lambda i: (i, 0, 0)),
                  pl.BlockSpec((m_rows, d), lambda i: (0, 0)), jax.ShapeDtypeStruct((m_rows, d), F32), (N_DEV,), acc_axis=0)
    g_wfi = _matmul("ffn_in_dw", TN, u2, d_ab, pl.BlockSpec((m_rows, d), lambda i: (0, 0)), pl.BlockSpec((None, m_rows, fs), lambda i: (i, 0, 0)),
                    pl.BlockSpec((None, d, fs), lambda i: (i, 0, 0)), jax.ShapeDtypeStruct((N_DEV, d, fs), F32), (N_DEV,), twin_bf16=True)

    def post_mix_bwd(r0, xs, ps):
        h0_, mix_, dh2_, du2_ = xs
        _, vjp = jax.vjp(_post_mix, h0_, mix_, ps[0], ps[1])
        dh0_, dmix_, dg1, dg2 = vjp((dh2_, du2_))
        return [dmix_, dh0_], [dg1, dg2]

    d_mix, dh1, g_post_mix, g_pre_ffn = _rows("post_mix_bwd", post_mix_bwd, [h0, mix, dh2, du2], [post_mix_g, pre_ffn_g],
                                              [BF16, F32], [d, d])
    d_mixin = _dense_dx("mix_out_dx", d_mix, wo, F32)
    g_wo = _dense_dw("mix_out_dw", mixin, d_mix)

    def gate_mix_bwd(r0, xs, ps):
        _, vjp = jax.vjp(_gate_mix, xs[0], xs[1], xs[2], xs[3], ps[0])
        dpgc, dpga, dyc_, dya_, dgb = vjp(xs[4])
        return [dpgc, dpga, dyc_, dya_], [dgb]

    dp_gc, dp_ga, d_yconv, d_yattn, g_gate_b = _rows(
        "gate_mix_bwd", gate_mix_bwd, [p, p, y_conv, y_attn, d_mixin], [gate_b], [BF16] * 4, [2 * d],
        row_in_cols=gate_cols + [None], row_out_widths=[d] * 4)
    d_o = _dense_dx("attn_out_dx", d_yattn, wao, BF16)
    g_wao = _dense_dw("attn_out_dw", o, d_yattn)
    big = {"w_ffn_out": [g.reshape(N_DEV, fr, d) for g in g_wfo], "w_ffn_in": g_wfi,
           "w_o": [g.reshape(N_DEV, dc, d) for g in g_wo], "w_attn_out": [g.reshape(N_DEV, dc, d) for g in g_wao]}
    early = ("w_ffn_out", "w_ffn_in", "w_o", "w_attn_out")
    dq, dk, dv, received_early = _attn_bwd(p, d_o, tot, d, [big[k][1] for k in early])
    d_yc = _dense_dx("conv_out_dx", d_yconv, wco, F32)
    g_wco = _dense_dw("conv_out_dw", yc, d_yconv)

    def conv_norm_bwd(r0, xs, ps):
        _, vjp = jax.vjp(_ln_silu, xs[0], ps[0], ps[1])
        dy_, dg, db = vjp(xs[1])
        return [dy_], [dg, db]

    d_y, g_ln_g, g_ln_b = _rows("conv_norm_bwd", conv_norm_bwd, [y, d_yc], [conv_ln_g, conv_ln_b], [F32], [d, d])
    dp_a, dp_g, g_dww, g_dwb = _conv_bwd(p, d_y, dww_full, d)
    dp = jnp.concatenate([dp_a, dp_g, dq, dk, dv, dp_gc, dp_ga], axis=1)
    du = _matmul("in_proj_dx", NT, dp, wi, pl.BlockSpec((m_rows, ns), lambda i: (0, i)), pl.BlockSpec((None, d, ns), lambda i: (i, 0, 0)),
                 pl.BlockSpec((m_rows, d), lambda i: (0, 0)), jax.ShapeDtypeStruct((m_rows, d), F32), (N_DEV,), acc_axis=0)
    g_wi = _matmul("in_proj_dw", TN, u, dp, pl.BlockSpec((m_rows, d), lambda i: (0, 0)), pl.BlockSpec((m_rows, ns), lambda i: (0, i)),
                   pl.BlockSpec((None, d, ns), lambda i: (i, 0, 0)), jax.ShapeDtypeStruct((N_DEV, d, ns), F32), (N_DEV,), twin_bf16=True)

    def pre_mix_bwd(r0, xs, ps):
        _, vjp = jax.vjp(_rms, xs[0], ps[0])
        dx, dg = vjp(xs[1])
        return [xs[2] + dx], [dg]

    dh0, g_pre_mix = _rows("pre_mix_bwd", pre_mix_bwd, [h0, du, dh1], [pre_mix_g], [F32], [d])
    grad_x = dh0[n_meta:length][None]

    x_i, y_i, c_i = _mesh_pos()
    me = _dev(x_i, y_i, c_i)
    me_arr = jnp.reshape(me, (1,)).astype(I32)
    big["w_in"] = g_wi
    big["w_conv_out"] = [g.reshape(N_DEV, dc, d) for g in g_wco]
    late = ("w_conv_out", "w_in")
    received = dict(zip(early, received_early))
    received.update(zip(late, _scatter_exchange([big[k][1] for k in late])))
    results = {}
    for k in SHARDED:
        outs = _adamw_shard(me_arr, big[k][0], received[k], local[k], given["m_" + k][0], given["v_" + k][0])
        results[k] = tuple(a[None] for a in outs)

    rep_grads = {"pre_mix_g": g_pre_mix, "gate_b": g_gate_b, "dw_b": g_dwb, "conv_ln_g": g_ln_g, "conv_ln_b": g_ln_b,
                 "post_mix_g": g_post_mix, "pre_ffn_g": g_pre_ffn, "post_ffn_g": g_post_ffn}

    def pack_rep(get):
        return jnp.concatenate([get(k) for k in REPLICATED], axis=1).reshape(-1, LANE)

    rep_rows = pack_rep(lambda k: rep_grads[k])
    n_rep = rep_rows.shape[0]
    loss_rows = jnp.broadcast_to(loss_part, (8, LANE))
    g_meta = dh0[0:n_meta]
    slabs = jnp.stack([jnp.concatenate([rep_rows, loss_rows, g_dww[:, j * LANE:(j + 1) * LANE], g_meta[:, j * LANE:(j + 1) * LANE]], axis=0)
                       for j in range(N_DEV)])

    def pack_small(prefix):
        dww_own = jnp.pad(given[prefix + "dw_w"][0], ((0, CONV_PAD - CONV_WIDTH), (0, 0)))
        return jnp.concatenate([pack_rep(lambda k: given[prefix + k]), jnp.zeros((8, LANE), F32), dww_own,
                                given[prefix + "meta_tokens"]], axis=0)

    small = _small_reduce_adamw(slabs, pack_small(""), pack_small("m_"), pack_small("v_"))
    loss = small[0][n_rep, 0]

    def unpack(arr):
        out = {}
        flat = arr[:n_rep].reshape(1, -1)
        off = 0
        for k in REPLICATED:
            w = given[k].shape[1]
            out[k] = flat[:, off:off + w]
            off += w
        out["dw_w"] = arr[n_rep + 8:n_rep + 8 + CONV_WIDTH][None]
        out["meta_tokens"] = arr[n_rep + 8 + CONV_PAD:n_rep + 8 + CONV_PAD + n_meta]
        return out

    small_out = [unpack(a) for a in small]
    for k in WEIGHTS:
        if k not in results:
            results[k] = tuple(s[k] for s in small_out)
    return (loss, grad_x, *[results[k][0] for k in WEIGHTS], *[results[k][1] for k in WEIGHTS],
            *[results[k][2] for k in WEIGHTS], *[results[k][3] for k in WEIGHTS])
```

```python
import jax
import jax.numpy as jnp
from jax import lax
from jax.experimental import pallas as pl
from jax.experimental.pallas import tpu as pltpu

F32 = jnp.float32
BF16 = jnp.bfloat16
I32 = jnp.int32

N_DEV = 8
LANE = 128
HEAD_DIM = 64
QB = 128
KEY_SHIFT = 2
KEY_TILES = 1 << KEY_SHIFT
KEY_CHUNK = KEY_TILES * QB
LANE_BLOCKS = 2
BWD_LANE_BLOCKS = 2
CONV_WIDTH = 31
CONV_PAD = 32
ROW_CHUNK = 128
RMS_EPS = 1e-6
LN_EPS = 1e-5
ADAM_LR = 0.001
ADAM_B1 = 0.9
ADAM_B2 = 0.999
ADAM_EPS = 1e-08
ADAM_WD = 0.01
ADAM_STEP = 10
VMEM_LIMIT = 56 * 1024 * 1024

NN = (((1,), (0,)), ((), ()))
NT = (((1,), (1,)), ((), ()))
TN = (((0,), (0,)), ((), ()))
MESH = pl.DeviceIdType.MESH
ANY = pl.BlockSpec(memory_space=pl.ANY)
VMEM_WHOLE = pl.BlockSpec(memory_space=pltpu.VMEM)


def _params(n_axes):
    return pltpu.CompilerParams(dimension_semantics=("arbitrary",) * n_axes, vmem_limit_bytes=VMEM_LIMIT)


def _row_tile(m):
    assert m % QB == 0
    return m // 4 if m % 64 == 0 else QB


def _matmul(name, dims, a, b, a_spec, b_spec, o_spec, out_shape, grid, acc_axis=None, twin_bf16=False):
    def body(a_ref, b_ref, o_ref, *twin):
        r = lax.dot_general(a_ref[...], b_ref[...], dims, preferred_element_type=F32)
        if acc_axis is None:
            o_ref[...] = r.astype(o_ref.dtype)
            for t_ref in twin:
                t_ref[...] = r.astype(t_ref.dtype)
        else:
            k = pl.program_id(acc_axis)

            @pl.when(k == 0)
            def _():
                o_ref[...] = r

            @pl.when(k > 0)
            def _():
                o_ref[...] += r

    if twin_bf16:
        assert acc_axis is None
        o_spec = [o_spec, o_spec]
        out_shape = [out_shape, jax.ShapeDtypeStruct(out_shape.shape, BF16)]
    return pl.pallas_call(body, name=name, grid=grid, in_specs=[a_spec, b_spec], out_specs=o_spec,
                          out_shape=out_shape, compiler_params=_params(len(grid)))(a, b)


def _dense_fwd(name, a, w, out_dtype=F32):
    m, k = a.shape
    n = w.shape[1]
    tn = 512
    return _matmul(name, NN, a, w, pl.BlockSpec((m, k), lambda j: (0, 0)), pl.BlockSpec((k, tn), lambda j: (0, j)),
                   pl.BlockSpec((m, tn), lambda j: (0, j)), jax.ShapeDtypeStruct((m, n), out_dtype), (n // tn,))


def _dense_dx(name, dy, w, out_dtype):
    m, n = dy.shape
    k = w.shape[0]
    tk = 512
    return _matmul(name, NT, dy, w, pl.BlockSpec((m, n), lambda j: (0, 0)), pl.BlockSpec((tk, n), lambda j: (j, 0)),
                   pl.BlockSpec((m, tk), lambda j: (0, j)), jax.ShapeDtypeStruct((m, k), out_dtype), (k // tk,))


def _dense_dw(name, a, dy):
    m, k = a.shape
    n = dy.shape[1]
    tn = 512
    return _matmul(name, TN, a, dy, pl.BlockSpec((m, k), lambda j: (0, 0)), pl.BlockSpec((m, tn), lambda j: (0, j)),
                   pl.BlockSpec((k, tn), lambda j: (0, j)), jax.ShapeDtypeStruct((k, n), F32), (n // tn,), twin_bf16=True)


def _rowwise(name, fn, row_ins, par_ins, row_outs, par_outs, *, grid, in_specs, out_specs, tm, row_axis):
    n_ri, n_pi, n_ro, n_po = len(row_ins), len(par_ins), len(row_outs), len(par_outs)
    n_steps, tail = divmod(tm, ROW_CHUNK)
    assert tail % 16 == 0

    def body(*refs):
        ri = refs[:n_ri]
        pi = refs[n_ri:n_ri + n_pi]
        ro = refs[n_ri + n_pi:n_ri + n_pi + n_ro]
        po = refs[n_ri + n_pi + n_ro:]
        ps = [r[...] for r in pi]
        base = pl.program_id(row_axis) * tm

        def chunk(r0, rows, carry):
            xs = [r[pl.ds(r0, rows), :] for r in ri]
            outs, pouts = fn(base + r0, xs, ps)
            for r, o in zip(ro, outs):
                if isinstance(o, (list, tuple)):
                    for j, part in enumerate(o):
                        r[j, pl.ds(r0, rows), :] = part.astype(r.dtype)
                else:
                    r[pl.ds(r0, rows), :] = o.astype(r.dtype)
            return tuple(c + q for c, q in zip(carry, pouts))

        def step(i, carry):
            return chunk(pl.multiple_of(i * ROW_CHUNK, ROW_CHUNK), ROW_CHUNK, carry)

        acc = lax.fori_loop(0, n_steps, step, tuple(jnp.zeros(s.shape, F32) for s in par_outs))
        if tail:
            acc = chunk(n_steps * ROW_CHUNK, tail, acc)
        if n_po:
            first = pl.program_id(0) == 0
            for ax in range(1, len(grid)):
                first = first & (pl.program_id(ax) == 0)

            @pl.when(first)
            def _():
                for r in po:
                    r[...] = jnp.zeros_like(r)

            for r, a in zip(po, acc):
                r[...] += a

    return pl.pallas_call(body, name=name, grid=grid, in_specs=in_specs, out_specs=out_specs,
                          out_shape=tuple(row_outs) + tuple(par_outs),
                          compiler_params=_params(len(grid)))(*row_ins, *par_ins)


def _rows(name, fn, row_ins, par_ins, row_out_dtypes, par_out_widths, row_in_cols=None, row_out_widths=None):
    m = row_ins[0].shape[0]
    tm = _row_tile(m)
    in_specs = []
    for k, a in enumerate(row_ins):
        if row_in_cols is not None and row_in_cols[k] is not None:
            width, cb = row_in_cols[k]
            in_specs.append(pl.BlockSpec((tm, width), lambda i, cb=cb: (i, cb)))
        else:
            in_specs.append(pl.BlockSpec((tm, a.shape[1]), lambda i: (i, 0)))
    for a in par_ins:
        in_specs.append(pl.BlockSpec(a.shape, lambda i: (0, 0)))
    if row_out_widths is None:
        row_out_widths = [row_ins[0].shape[1]] * len(row_out_dtypes)
    row_outs = [jax.ShapeDtypeStruct((m, w), dt) for w, dt in zip(row_out_widths, row_out_dtypes)]
    par_outs = [jax.ShapeDtypeStruct((1, w), F32) for w in par_out_widths]
    out_specs = [pl.BlockSpec((tm, s.shape[1]), lambda i: (i, 0)) for s in row_outs]
    out_specs += [pl.BlockSpec(s.shape, lambda i: (0, 0)) for s in par_outs]
    return _rowwise(name, fn, row_ins, par_ins, row_outs, par_outs, grid=(m // tm,), in_specs=in_specs,
                    out_specs=out_specs, tm=tm, row_axis=0)


def _rms(x, g):
    return x * lax.rsqrt(jnp.mean(x * x, axis=-1, keepdims=True) + RMS_EPS) * g


def _ln_silu(y, g, b):
    mu = jnp.mean(y, axis=-1, keepdims=True)
    yc = y - mu
    var = jnp.mean(yc * yc, axis=-1, keepdims=True)
    return jax.nn.silu(yc * lax.rsqrt(var + LN_EPS) * g + b)


def _gate_mix(pgc, pga, yc, ya, gb):
    d = pgc.shape[1]
    return jax.nn.sigmoid(pgc + gb[:, :d]) * yc + jax.nn.sigmoid(pga + gb[:, d:]) * ya


def _post_mix(h0, mix, g_post, g_pre):
    h1 = h0 + _rms(mix, g_post)
    return h1, _rms(h1, g_pre)


def _swiglu(a, b):
    return jax.nn.silu(a) * b


def _conv_taps():
    taps = []
    for b in range(8):
        for a in range(CONV_PAD // 8):
            s = 8 * a + b
            if s < CONV_WIDTH:
                taps.append((b, a, CONV_WIDTH - 1 - s))
    return taps


def _conv_fwd(p, dww, dwb, d_model):
    m = p.shape[0]
    nch = d_model // LANE
    n_chunk = m // QB
    taps = _conv_taps()

    def body(a_ref, g_ref, w_ref, b_ref, y_ref, upad):
        upad[0:CONV_PAD, :] = jnp.zeros((CONV_PAD, LANE), F32)

        def fill(i, c):
            r0 = pl.multiple_of(i * QB, QB)
            u = a_ref[pl.ds(r0, QB), :] * jax.nn.sigmoid(g_ref[pl.ds(r0, QB), :])
            upad[pl.ds(pl.multiple_of(r0 + CONV_PAD, 8), QB), :] = u
            return c

        lax.fori_loop(0, n_chunk, fill, 0)

        def conv(i, c):
            r0 = pl.multiple_of(i * QB, QB)
            win = upad[pl.ds(r0, QB + CONV_PAD), :]
            acc = jnp.broadcast_to(b_ref[...], (QB, LANE))
            rolled = {}
            for b, a, j in taps:
                if b not in rolled:
                    rolled[b] = win if b == 0 else pltpu.roll(win, b, axis=0)
                lo = CONV_PAD - 8 * a
                acc = acc + w_ref[j:j + 1, :] * rolled[b][lo:lo + QB, :]
            y_ref[pl.ds(r0, QB), :] = acc
            return c

        lax.fori_loop(0, n_chunk, conv, 0)

    col = lambda off: pl.BlockSpec((m, LANE), lambda c: (0, off + c))
    return pl.pallas_call(
        body, name="conv_fwd", grid=(nch,),
        in_specs=[col(0), col(nch), pl.BlockSpec((CONV_PAD, LANE), lambda c: (0, c)), pl.BlockSpec((1, LANE), lambda c: (0, c))],
        out_specs=col(0), out_shape=jax.ShapeDtypeStruct((m, d_model), F32),
        scratch_shapes=[pltpu.VMEM((m + CONV_PAD, LANE), F32)], compiler_params=_params(1))(p, p, dww, dwb)


def _conv_bwd(p, dy, dww, d_model):
    m = p.shape[0]
    nch = d_model // LANE
    n_chunk = m // QB
    taps = _conv_taps()
    win_rows = QB + CONV_PAD

    def body(a_ref, g_ref, dy_ref, w_ref, da_ref, dg_ref, dw_ref, db_ref, upad, dypad, wacc, bacc):
        upad[0:CONV_PAD, :] = jnp.zeros((CONV_PAD, LANE), F32)
        dypad[m:m + CONV_PAD, :] = jnp.zeros((CONV_PAD, LANE), F32)
        wacc[...] = jnp.zeros_like(wacc)
        bacc[...] = jnp.zeros_like(bacc)

        def fill(i, c):
            r0 = pl.multiple_of(i * QB, QB)
            u = a_ref[pl.ds(r0, QB), :] * jax.nn.sigmoid(g_ref[pl.ds(r0, QB), :])
            upad[pl.ds(pl.multiple_of(r0 + CONV_PAD, 8), QB), :] = u
            dypad[pl.ds(r0, QB), :] = dy_ref[pl.ds(r0, QB), :]
            return c

        lax.fori_loop(0, n_chunk, fill, 0)

        def chunk(i, c):
            r0 = pl.multiple_of(i * QB, QB)
            dwin = dypad[pl.ds(r0, win_rows), :]
            du = jnp.zeros((QB, LANE), F32)
            rolled = {}
            for b, a, j in taps:
                if b not in rolled:
                    rolled[b] = dwin if b == 0 else pltpu.roll(dwin, win_rows - b, axis=0)
                du = du + w_ref[j:j + 1, :] * rolled[b][8 * a:8 * a + QB, :]
            av = a_ref[pl.ds(r0, QB), :]
            sg = jax.nn.sigmoid(g_ref[pl.ds(r0, QB), :])
            da_ref[pl.ds(r0, QB), :] = (du * sg).astype(da_ref.dtype)
            dg_ref[pl.ds(r0, QB), :] = (du * av * sg * (1.0 - sg)).astype(dg_ref.dtype)
            dyc = dy_ref[pl.ds(r0, QB), :]
            uwin = upad[pl.ds(r0, win_rows), :]
            rolled = {}
            for b, a, j in taps:
                if b not in rolled:
                    rolled[b] = uwin if b == 0 else pltpu.roll(uwin, b, axis=0)
                lo = CONV_PAD - 8 * a
                prod = dyc * rolled[b][lo:lo + QB, :]
                wacc[j] += prod.reshape(QB // 8, 8, LANE).sum(axis=0)
            bacc[...] += dyc.reshape(QB // 8, 8, LANE).sum(axis=0)
            return c

        lax.fori_loop(0, n_chunk, chunk, 0)
        for j in range(CONV_WIDTH):
            dw_ref[j:j + 1, :] = jnp.sum(wacc[j], axis=0, keepdims=True)
        dw_ref[CONV_WIDTH:CONV_PAD, :] = jnp.zeros((CONV_PAD - CONV_WIDTH, LANE), F32)
        db_ref[...] = jnp.sum(bacc[...], axis=0, keepdims=True)

    col = lambda off: pl.BlockSpec((m, LANE), lambda c: (0, off + c))
    return pl.pallas_call(
        body, name="conv_bwd", grid=(nch,),
        in_specs=[col(0), col(nch), col(0), pl.BlockSpec((CONV_PAD, LANE), lambda c: (0, c))],
        out_specs=[col(0), col(0), pl.BlockSpec((CONV_PAD, LANE), lambda c: (0, c)), pl.BlockSpec((1, LANE), lambda c: (0, c))],
        out_shape=(jax.ShapeDtypeStruct((m, d_model), BF16), jax.ShapeDtypeStruct((m, d_model), BF16),
                   jax.ShapeDtypeStruct((CONV_PAD, d_model), F32), jax.ShapeDtypeStruct((1, d_model), F32)),
        scratch_shapes=[pltpu.VMEM((m + CONV_PAD, LANE), F32), pltpu.VMEM((m + CONV_PAD, LANE), F32),
                        pltpu.VMEM((CONV_PAD, 8, LANE), F32), pltpu.VMEM((8, LANE), F32)],
        compiler_params=_params(1))(p, p, dy, dww)


EXP_CLAMP = 80.0


def _one_plus_exp(z):
    return 1.0 + jnp.exp(jnp.minimum(z, EXP_CLAMP))


def _softplus(z):
    return jnp.maximum(jnp.log(_one_plus_exp(z)), z)


def _softplus_sigmoid(z):
    s = _one_plus_exp(z)
    return jnp.maximum(jnp.log(s), z), 1.0 - 1.0 / s


def _split_dot(x, tri):
    hi = pltpu.bitcast(pltpu.bitcast(x, jnp.uint32) & jnp.uint32(0xFFFF0000), F32)
    lo = x - hi
    return jnp.dot(jnp.concatenate([hi.astype(BF16), lo.astype(BF16)], axis=1), tri, preferred_element_type=F32)


def _tri(kind):
    jj = lax.broadcasted_iota(I32, (2 * QB, 2 * QB), 0) & (QB - 1)
    ss = lax.broadcasted_iota(I32, (2 * QB, 2 * QB), 1)
    keep = {"ge": jj >= ss, "lt": jj < ss, "le": jj <= ss}[kind]
    return jnp.where((ss >= QB) | keep, 1.0, 0.0).astype(BF16)


def _attn_fwd(p, d_model, shards):
    m = p.shape[0]
    nqb = m // QB
    ngrp = d_model // (LANE_BLOCKS * LANE)
    qo, ko, vo = 2 * ngrp, 3 * ngrp, 4 * ngrp
    scale = HEAD_DIM ** -0.5
    n_sh = len(shards)

    assert nqb >= KEY_TILES

    def body(q_ref, k_ref, v_ref, *rest):
        shard_refs, (o_ref, t_ref), rest = rest[:n_sh], rest[n_sh:n_sh + 2], rest[n_sh + 2:]
        gathered_refs, (acc_ref, car_ref), sems = rest[:n_sh], rest[n_sh:n_sh + 2], rest[n_sh + 2:]
        starts, relays, waits = _gather_copies(shard_refs, gathered_refs, *sems)

        @pl.when((pl.program_id(0) == 0) & (pl.program_id(1) == 0))
        def _():
            for f in starts:
                f()

        @pl.when((pl.program_id(0) == ngrp - 1) & (pl.program_id(1) == 0))
        def _():
            for f in relays:
                f()

        qb = pl.program_id(1)
        lane = lax.broadcasted_iota(I32, (QB, LANE), 1)
        head0 = lane < HEAD_DIM
        row_g = qb * QB + lax.broadcasted_iota(I32, (QB, KEY_CHUNK), 0)
        col_l = lax.broadcasted_iota(I32, (QB, KEY_CHUNK), 1)
        tri = _tri("ge")
        heads = range(2 * LANE_BLOCKS)
        qh = []
        for lb in range(LANE_BLOCKS):
            q2 = (q_ref[:, lb * LANE:(lb + 1) * LANE] * scale).astype(BF16)
            zero = jnp.zeros_like(q2)
            qh += [jnp.where(head0, q2, zero), jnp.where(head0, zero, q2)]
        acc_ref[...] = jnp.zeros_like(acc_ref)
        car_ref[...] = jnp.zeros_like(car_ref)

        def chunk(first_tile, bound):
            r0 = pl.multiple_of(first_tile * QB, QB)
            kcs = [k_ref[pl.ds(r0, KEY_CHUNK), lb * LANE:(lb + 1) * LANE].astype(BF16) for lb in range(LANE_BLOCKS)]
            vcs = [v_ref[pl.ds(r0, KEY_CHUNK), lb * LANE:(lb + 1) * LANE].astype(BF16) for lb in range(LANE_BLOCKS)]
            valid = None if bound is None else (col_l + r0) < bound
            zs = [lax.dot_general(qh[h], kcs[h // 2], NT, preferred_element_type=F32) for h in heads]
            sps = [_softplus(z) for z in zs]
            if valid is not None:
                sps = [jnp.where(valid, sp, 0.0) for sp in sps]
            crs = [[_split_dot(sp[:, i * QB:(i + 1) * QB], tri) for i in range(KEY_TILES)] for sp in sps]
            cars = [car_ref[h] for h in heads]
            a_tiles = [[None] * KEY_TILES for h in heads]
            for i in reversed(range(KEY_TILES)):
                for h in heads:
                    a_tiles[h][i] = jnp.exp(zs[h][:, i * QB:(i + 1) * QB] - (crs[h][i][:, :QB] + cars[h]))
                    cars[h] = cars[h] + crs[h][i][:, QB:]
            for h in heads:
                a = jnp.concatenate(a_tiles[h], axis=1)
                if valid is not None:
                    a = jnp.where(valid, a, 0.0)
                acc_ref[h] += jnp.dot(a.astype(BF16), vcs[h // 2], preferred_element_type=F32)
                car_ref[h] = cars[h]

        near = jnp.maximum(qb - (KEY_TILES - 1), 0)
        chunk(near, row_g)
        n_full = lax.shift_right_logical(near, KEY_SHIFT)

        def step(i, c):
            chunk(near - KEY_TILES * (i + 1), None)
            return c

        lax.fori_loop(0, n_full, step, 0)
        left = near - KEY_TILES * n_full

        @pl.when(left > 0)
        def _():
            chunk(0, left * QB)

        for lb in range(LANE_BLOCKS):
            o_ref[:, lb * LANE:(lb + 1) * LANE] = jnp.where(head0, acc_ref[2 * lb], acc_ref[2 * lb + 1]).astype(o_ref.dtype)
        for h in heads:
            t_ref[:, h * QB:(h + 1) * QB] = car_ref[h]

        @pl.when((pl.program_id(0) == ngrp - 1) & (pl.program_id(1) == nqb - 1))
        def _():
            for f in waits:
                f()

    wide = LANE_BLOCKS * LANE
    outs = pl.pallas_call(
        body, name="attn_fwd", grid=(ngrp, nqb),
        in_specs=[pl.BlockSpec((QB, wide), lambda g, qb: (qb, qo + g)),
                  pl.BlockSpec((m, wide), lambda g, qb: (0, ko + g)),
                  pl.BlockSpec((m, wide), lambda g, qb: (0, vo + g))] + [ANY] * n_sh,
        out_specs=[pl.BlockSpec((QB, wide), lambda g, qb: (qb, g)),
                   pl.BlockSpec((QB, 2 * wide), lambda g, qb: (qb, g))] + [ANY] * n_sh,
        out_shape=[jax.ShapeDtypeStruct((m, d_model), BF16), jax.ShapeDtypeStruct((m, 2 * d_model), F32)]
        + [jax.ShapeDtypeStruct((N_DEV,) + s.shape, s.dtype) for s in shards],
        scratch_shapes=[pltpu.VMEM((2 * LANE_BLOCKS, QB, LANE), F32), pltpu.VMEM((2 * LANE_BLOCKS, QB, LANE), F32)]
        + _exchange_sems(n_sh) + [pltpu.SemaphoreType.DMA((n_sh,))],
        compiler_params=_params(2))(p, p, p, *shards)
    return outs[0], outs[1], outs[2:]


def _attn_bwd(p, d_o, tot, d_model, slabs):
    m = p.shape[0]
    nqb = m // QB
    blocks = BWD_LANE_BLOCKS
    ngrp = d_model // (blocks * LANE)
    qo, ko, vo = 2 * ngrp, 3 * ngrp, 4 * ngrp
    scale = HEAD_DIM ** -0.5
    n_sl = len(slabs)

    assert nqb >= KEY_TILES

    def body(q_ref, k_ref, v_ref, do_ref, t_ref, *rest):
        slab_refs, (dq_ref, dk_ref, dv_ref), rest = rest[:n_sl], rest[n_sl:n_sl + 3], rest[n_sl + 3:]
        recv_refs, (dkacc, dvacc, dqacc, csp, cg), sems = rest[:n_sl], rest[n_sl:n_sl + 5], rest[n_sl + 5:]
        starts, waits = _scatter_copies(slab_refs, recv_refs, *sems)

        @pl.when((pl.program_id(0) == 0) & (pl.program_id(1) == 0))
        def _():
            for f in starts:
                f()

        qb = pl.program_id(1)
        lane = lax.broadcasted_iota(I32, (QB, LANE), 1)
        head0 = lane < HEAD_DIM
        row_g = qb * QB + lax.broadcasted_iota(I32, (QB, KEY_CHUNK), 0)
        col_l = lax.broadcasted_iota(I32, (QB, KEY_CHUNK), 1)
        tri_lt = _tri("lt")
        tri_le = _tri("le")
        heads = range(2 * blocks)
        qh, doh = [], []
        for lb in range(blocks):
            q2 = (q_ref[:, lb * LANE:(lb + 1) * LANE] * scale).astype(BF16)
            do2 = do_ref[:, lb * LANE:(lb + 1) * LANE]
            zero = jnp.zeros_like(q2)
            qh += [jnp.where(head0, q2, zero), jnp.where(head0, zero, q2)]
            doh += [jnp.where(head0, do2, zero), jnp.where(head0, zero, do2)]
        q_pairs = [jnp.concatenate(qh[2 * lb:2 * lb + 2], axis=0) for lb in range(blocks)]
        do_pairs = [jnp.concatenate(doh[2 * lb:2 * lb + 2], axis=0) for lb in range(blocks)]

        @pl.when(qb == 0)
        def _():
            dkacc[...] = jnp.zeros_like(dkacc)
            dvacc[...] = jnp.zeros_like(dvacc)

        dqacc[...] = jnp.zeros_like(dqacc)
        csp[...] = jnp.zeros_like(csp)
        cg[...] = jnp.zeros_like(cg)

        def chunk(first_tile, bound):
            r0 = pl.multiple_of(first_tile * QB, QB)
            kcs = [k_ref[pl.ds(r0, KEY_CHUNK), lb * LANE:(lb + 1) * LANE].astype(BF16) for lb in range(blocks)]
            vcs = [v_ref[pl.ds(r0, KEY_CHUNK), lb * LANE:(lb + 1) * LANE].astype(BF16) for lb in range(blocks)]
            valid = None if bound is None else (col_l + r0) < bound
            tiles = [slice(i * QB, (i + 1) * QB) for i in range(KEY_TILES)]
            zs = [lax.dot_general(qh[h], kcs[h // 2], NT, preferred_element_type=F32) for h in heads]
            das = [lax.dot_general(doh[h], vcs[h // 2], NT, preferred_element_type=F32) for h in heads]
            sps, sgs = zip(*[_softplus_sigmoid(z) for z in zs])
            if valid is not None:
                sps = [jnp.where(valid, sp, 0.0) for sp in sps]
            crs = [[_split_dot(sp[:, c], tri_lt) for c in tiles] for sp in sps]
            a_tiles, g_tiles = [[] for h in heads], [[] for h in heads]
            for h in heads:
                tot_h = t_ref[:, h * QB:(h + 1) * QB]
                before = csp[h]
                for i, c in enumerate(tiles):
                    a = jnp.exp(zs[h][:, c] - (tot_h - (before + crs[h][i][:, :QB])))
                    if valid is not None:
                        a = jnp.where(valid[:, c], a, 0.0)
                    a_tiles[h].append(a)
                    g_tiles[h].append(a * das[h][:, c])
                    before = before + crs[h][i][:, QB:]
                csp[h] = before
            grs = [[_split_dot(g, tri_le) for g in g_tiles[h]] for h in heads]
            dzbs, abs_ = [], []
            for h in heads:
                g_before = cg[h]
                dz_tiles = []
                for i, c in enumerate(tiles):
                    dz = g_tiles[h][i] - sgs[h][:, c] * (g_before + grs[h][i][:, :QB])
                    if valid is not None:
                        dz = jnp.where(valid[:, c], dz, 0.0)
                    dz_tiles.append(dz)
                    g_before = g_before + grs[h][i][:, QB:]
                cg[h] = g_before
                dzbs.append(jnp.concatenate(dz_tiles, axis=1).astype(BF16))
                abs_.append(jnp.concatenate(a_tiles[h], axis=1).astype(BF16))
            for h in heads:
                dqacc[h] += jnp.dot(dzbs[h], kcs[h // 2], preferred_element_type=F32)
            for lb in range(blocks):
                dz_pair = jnp.concatenate(dzbs[2 * lb:2 * lb + 2], axis=0)
                a_pair = jnp.concatenate(abs_[2 * lb:2 * lb + 2], axis=0)
                dkacc[pl.ds(r0, KEY_CHUNK), lb * LANE:(lb + 1) * LANE] += lax.dot_general(
                    dz_pair, q_pairs[lb], TN, preferred_element_type=F32)
                dvacc[pl.ds(r0, KEY_CHUNK), lb * LANE:(lb + 1) * LANE] += lax.dot_general(
                    a_pair, do_pairs[lb], TN, preferred_element_type=F32)

        near = jnp.maximum(qb - (KEY_TILES - 1), 0)
        n_full = lax.shift_right_logical(near, KEY_SHIFT)

        def step(i, c):
            chunk(KEY_TILES * i, None)
            return c

        lax.fori_loop(0, n_full, step, 0)

        @pl.when(near > KEY_TILES * n_full)
        def _():
            chunk(KEY_TILES * n_full, near * QB)

        chunk(near, row_g)
        for lb in range(blocks):
            dq2 = jnp.where(head0, dqacc[2 * lb], dqacc[2 * lb + 1]) * scale
            dq_ref[:, lb * LANE:(lb + 1) * LANE] = dq2.astype(dq_ref.dtype)

        @pl.when(qb == nqb - 1)
        def _():
            dk_ref[...] = dkacc[...].astype(dk_ref.dtype)
            dv_ref[...] = dvacc[...].astype(dv_ref.dtype)

        @pl.when((pl.program_id(0) == ngrp - 1) & (qb == nqb - 1))
        def _():
            for f in waits:
                f()

    out = jax.ShapeDtypeStruct((m, d_model), BF16)
    wide = blocks * LANE
    carry = pltpu.VMEM((2 * blocks, QB, LANE), F32)
    outs = pl.pallas_call(
        body, name="attn_bwd", grid=(ngrp, nqb),
        in_specs=[pl.BlockSpec((QB, wide), lambda g, qb: (qb, qo + g)),
                  pl.BlockSpec((m, wide), lambda g, qb: (0, ko + g)),
                  pl.BlockSpec((m, wide), lambda g, qb: (0, vo + g)),
                  pl.BlockSpec((QB, wide), lambda g, qb: (qb, g)),
                  pl.BlockSpec((QB, 2 * wide), lambda g, qb: (qb, g))] + [ANY] * n_sl,
        out_specs=[pl.BlockSpec((QB, wide), lambda g, qb: (qb, g)),
                   pl.BlockSpec((m, wide), lambda g, qb: (0, g)),
                   pl.BlockSpec((m, wide), lambda g, qb: (0, g))] + [ANY] * n_sl,
        out_shape=[out, out, out] + _received_shapes(slabs),
        scratch_shapes=[pltpu.VMEM((m, wide), F32), pltpu.VMEM((m, wide), F32), carry, carry, carry] + _exchange_sems(n_sl),
        compiler_params=_params(2))(p, p, p, d_o, tot, *slabs)
    return outs[0], outs[1], outs[2], outs[3:]


def _mesh_pos():
    return lax.axis_index("x"), lax.axis_index("y"), lax.axis_index("c")


def _other_chips(x, y):
    return [(1 - x, y), (x, 1 - y), (1 - x, 1 - y)]


def _dev(x, y, c):
    return 4 * x + 2 * y + c


def _all_gather(shards):
    n = len(shards)

    def body(*refs):
        starts, relays, waits = _gather_copies(refs[:n], refs[n:2 * n], *refs[2 * n:])
        for f in starts + relays + waits:
            f()

    return pl.pallas_call(
        body, name="comm_all_gather", in_specs=[ANY] * n, out_specs=[ANY] * n,
        out_shape=[jax.ShapeDtypeStruct((N_DEV,) + s.shape, s.dtype) for s in shards],
        scratch_shapes=[pltpu.SemaphoreType.DMA((n, 7)), pltpu.SemaphoreType.DMA((n, 7)), pltpu.SemaphoreType.DMA((n,))],
    )(*shards)


def _peers(x, y, c):
    out = []
    for mask in range(1, N_DEV):
        px, py, pc = x ^ (mask >> 2), y ^ ((mask >> 1) & 1), c ^ (mask & 1)
        out.append((mask - 1, (px, py, pc), _dev(px, py, pc)))
    return out


def _remote(src, dst, send_sems, recv_sems, k, s, peer):
    return pltpu.make_async_remote_copy(src_ref=src, dst_ref=dst, send_sem=send_sems.at[k, s], recv_sem=recv_sems.at[k, s],
                                        device_id=peer, device_id_type=MESH)


def _gather_copies(ins, outs, send_sems, recv_sems, local_sems):
    x, y, c = _mesh_pos()
    sibling = (x, y, 1 - c)
    chips = _other_chips(x, y)
    starts, relays, waits = [], [], []
    for k in range(len(ins)):
        def slot(block, k=k):
            return outs[k].at[_dev(*block)]

        def copy(s, src, block, to, k=k):
            return _remote(src, slot(block), send_sems, recv_sems, k, s, to)

        own = pltpu.make_async_copy(ins[k], slot((x, y, c)), local_sems.at[k])
        to_sibling = copy(0, ins[k], (x, y, c), sibling)
        starts += [own.start, to_sibling.start]
        waits += [own.wait, to_sibling.wait_send, copy(0, ins[k], (x, y, 1 - c), sibling).wait_recv]
        for j, chip in enumerate(chips):
            out = copy(1 + j, ins[k], (x, y, c), (*chip, c))
            relay = copy(4 + j, slot((*chip, c)), (*chip, c), sibling)
            starts.append(out.start)
            relays += [copy(1 + j, ins[k], (*chip, c), sibling).wait_recv, relay.start]
            waits += [out.wait_send, relay.wait_send, copy(4 + j, ins[k], (*chip, 1 - c), sibling).wait_recv]
    return starts, relays, waits


def _scatter_copies(ins, outs, send_sems, recv_sems):
    x, y, c = _mesh_pos()
    starts, waits = [], []
    for k in range(len(ins)):
        for s, peer, idx in _peers(x, y, c):
            send = _remote(ins[k].at[idx], outs[k].at[s], send_sems, recv_sems, k, s, peer)
            starts.append(send.start)
            waits += [send.wait_recv, send.wait_send]
    return starts, waits


def _exchange_sems(n):
    return [pltpu.SemaphoreType.DMA((n, N_DEV - 1)), pltpu.SemaphoreType.DMA((n, N_DEV - 1))]


def _received_shapes(slabs):
    return [jax.ShapeDtypeStruct((N_DEV - 1,) + a.shape[1:], a.dtype) for a in slabs]


def _scatter_exchange(slabs):
    n = len(slabs)

    def body(*refs):
        starts, waits = _scatter_copies(refs[:n], refs[n:2 * n], *refs[2 * n:])
        for f in starts + waits:
            f()

    return pl.pallas_call(body, name="comm_scatter_exchange", in_specs=[ANY] * n, out_specs=[ANY] * n,
                          out_shape=_received_shapes(slabs), scratch_shapes=_exchange_sems(n))(*slabs)


def _shard_tile(rows):
    for tr in range(min(rows, 352), 0, -1):
        if rows % tr == 0 and (tr % 16 == 0 or tr == rows):
            return tr


def _adamw_math(w, g, m, v):
    m = ADAM_B1 * m + (1.0 - ADAM_B1) * g
    v = ADAM_B2 * v + (1.0 - ADAM_B2) * (g * g)
    m_hat = m / (1.0 - ADAM_B1 ** ADAM_STEP)
    v_hat = v / (1.0 - ADAM_B2 ** ADAM_STEP)
    delta = -ADAM_LR * (m_hat / (jnp.sqrt(v_hat) + ADAM_EPS) + ADAM_WD * w)
    return delta, m, v


def _adamw_shard(me, grad, received, w, m, v):
    rows, cols = w.shape
    tr = _shard_tile(rows)

    def body(me_ref, g_ref, r_ref, w_ref, m_ref, v_ref, go_ref, do_ref, mo_ref, vo_ref):
        g = g_ref[...]
        for s in range(N_DEV - 1):
            g = g + r_ref[s].astype(F32)
        delta, m_new, v_new = _adamw_math(w_ref[...], g, m_ref[...], v_ref[...])
        go_ref[...] = g
        do_ref[...] = delta
        mo_ref[...] = m_new
        vo_ref[...] = v_new

    flat = pl.BlockSpec((tr, cols), lambda i, me: (i, 0))
    gs = pltpu.PrefetchScalarGridSpec(
        num_scalar_prefetch=1, grid=(rows // tr,),
        in_specs=[pl.BlockSpec((None, tr, cols), lambda i, me: (me[0], i, 0)),
                  pl.BlockSpec((N_DEV - 1, tr, cols), lambda i, me: (0, i, 0)), flat, flat, flat],
        out_specs=[flat, flat, flat, flat])
    out = jax.ShapeDtypeStruct((rows, cols), F32)
    return pl.pallas_call(body, name="adamw_shard", grid_spec=gs, out_shape=(out, out, out, out),
                          compiler_params=_params(1))(me, grad, received, w, m, v)


def _small_reduce_adamw(slabs, w, m, v):
    _, rows, _ = slabs.shape

    def body(s_ref, w_ref, m_ref, v_ref, g_ref, d_ref, mo_ref, vo_ref, land, send_sems, recv_sems):
        x, y, c = _mesh_pos()
        me = _dev(x, y, c)
        copies = []
        for mask in range(1, N_DEV):
            px, py, pc = x ^ (mask >> 2), y ^ ((mask >> 1) & 1), c ^ (mask & 1)
            copies.append(pltpu.make_async_remote_copy(
                src_ref=s_ref.at[_dev(px, py, pc)], dst_ref=land.at[me], send_sem=send_sems.at[mask - 1],
                recv_sem=recv_sems.at[mask - 1], device_id=(px, py, pc), device_id_type=MESH))
        for cp in copies:
            cp.start()
        land[me] = s_ref[me]
        for mask in range(1, N_DEV):
            px, py, pc = x ^ (mask >> 2), y ^ ((mask >> 1) & 1), c ^ (mask & 1)
            pltpu.make_async_remote_copy(
                src_ref=s_ref.at[me], dst_ref=land.at[_dev(px, py, pc)], send_sem=send_sems.at[mask - 1],
                recv_sem=recv_sems.at[mask - 1], device_id=(px, py, pc), device_id_type=MESH).wait_recv()
        for cp in copies:
            cp.wait_send()
        g = land[0]
        for d in range(1, N_DEV):
            g = g + land[d]
        delta, m_new, v_new = _adamw_math(w_ref[...], g, m_ref[...], v_ref[...])
        g_ref[...] = g
        d_ref[...] = delta
        mo_ref[...] = m_new
        vo_ref[...] = v_new

    out = jax.ShapeDtypeStruct((rows, LANE), F32)
    return pl.pallas_call(
        body, name="comm_small_reduce_adamw", in_specs=[VMEM_WHOLE] * 4, out_specs=[VMEM_WHOLE] * 4, out_shape=(out, out, out, out),
        scratch_shapes=[pltpu.VMEM((N_DEV, rows, LANE), F32), pltpu.SemaphoreType.DMA((N_DEV - 1,)),
                        pltpu.SemaphoreType.DMA((N_DEV - 1,))],
    )(slabs, w, m, v)


def _cast_bf16(arrs):
    n = len(arrs)

    def body(*refs):
        for i_ref, o_ref in zip(refs[:n], refs[n:]):
            o_ref[...] = i_ref[...].astype(BF16)

    return pl.pallas_call(body, name="cast_bf16", in_specs=[VMEM_WHOLE] * n, out_specs=[VMEM_WHOLE] * n,
                          out_shape=[jax.ShapeDtypeStruct(a.shape, BF16) for a in arrs],
                          compiler_params=pltpu.CompilerParams(vmem_limit_bytes=VMEM_LIMIT))(*arrs)


REPLICATED = ("pre_mix_g", "gate_b", "dw_b", "conv_ln_g", "conv_ln_b", "post_mix_g", "pre_ffn_g", "post_ffn_g")
SHARDED = ("w_in", "w_conv_out", "w_attn_out", "w_o", "w_ffn_in", "w_ffn_out")
WEIGHTS = ("meta_tokens", "pre_mix_g", "w_in", "gate_b", "dw_w", "dw_b", "conv_ln_g", "conv_ln_b", "w_conv_out",
           "w_attn_out", "w_o", "post_mix_g", "pre_ffn_g", "w_ffn_in", "w_ffn_out", "post_ffn_g")


def kernel(x, meta_tokens, pre_mix_g, w_in, gate_b, dw_w, dw_b, conv_ln_g, conv_ln_b, w_conv_out, w_attn_out, w_o, post_mix_g, pre_ffn_g, w_ffn_in, w_ffn_out, post_ffn_g, loss_target, m_meta_tokens, m_pre_mix_g, m_w_in, m_gate_b, m_dw_w, m_dw_b, m_conv_ln_g, m_conv_ln_b, m_w_conv_out, m_w_attn_out, m_w_o, m_post_mix_g, m_pre_ffn_g, m_w_ffn_in, m_w_ffn_out, m_post_ffn_g, v_meta_tokens, v_pre_mix_g, v_w_in, v_gate_b, v_dw_w, v_dw_b, v_conv_ln_g, v_conv_ln_b, v_w_conv_out, v_w_attn_out, v_w_o, v_post_mix_g, v_pre_ffn_g, v_w_ffn_in, v_w_ffn_out, v_post_ffn_g):
    given = dict(locals())
    seq, d = x.shape[1], x.shape[2]
    n_meta = meta_tokens.shape[0]
    length = n_meta + seq
    m_rows = -(-length // QB) * QB
    dc = d // N_DEV
    assert dc == LANE and n_meta % 8 == 0 and seq % 8 == 0
    fs = w_ffn_in.shape[2]
    fr = w_ffn_out.shape[1]
    assert 2 * fr == fs

    local = {k: given[k][0] for k in SHARDED}
    cast = _cast_bf16([local[k] for k in SHARDED])
    dww_pad = jnp.pad(dw_w[0], ((0, CONV_PAD - CONV_WIDTH), (0, 0)))
    wi, meta_g, dww_g = _all_gather([cast[0], meta_tokens, dww_pad])
    meta_full = jnp.concatenate([meta_g[j] for j in range(N_DEV)], axis=1)
    dww_full = jnp.concatenate([dww_g[j] for j in range(N_DEV)], axis=1)
    ns = wi.shape[2]

    tail = jnp.zeros((m_rows - length, d), F32)
    h0 = jnp.concatenate([meta_full, x[0], tail], axis=0)
    target = jnp.concatenate([jnp.zeros((n_meta, d), F32), loss_target[0], tail], axis=0)

    (u,) = _rows("pre_mix_norm", lambda r0, xs, ps: ([_rms(xs[0], ps[0])], []), [h0], [pre_mix_g], [BF16], [])
    p, p16 = _matmul("in_proj", NN, u, wi, pl.BlockSpec((m_rows, d), lambda i: (0, 0)), pl.BlockSpec((None, d, ns), lambda i: (i, 0, 0)),
                     pl.BlockSpec((m_rows, ns), lambda i: (0, i)), jax.ShapeDtypeStruct((m_rows, N_DEV * ns), F32), (N_DEV,),
                     twin_bf16=True)
    o, tot, gathered = _attn_fwd(p16, d, list(cast[1:]))
    wco, wao, wo = (g.reshape(d, d) for g in gathered[0:3])
    wfi = gathered[3]
    wfo = gathered[4].reshape(N_DEV // 2, fs, d)
    y = _conv_fwd(p, dww_full, dw_b, d)
    (yc,) = _rows("conv_norm", lambda r0, xs, ps: ([_ln_silu(xs[0], ps[0], ps[1])], []), [y], [conv_ln_g, conv_ln_b], [BF16], [])
    y_conv = _dense_fwd("conv_out", yc, wco)
    y_attn = _dense_fwd("attn_out", o, wao)
    gate_cols = [(d, 5), (d, 6), None, None]
    (mixin,) = _rows("gate_mix", lambda r0, xs, ps: ([_gate_mix(*xs, ps[0])], []), [p, p, y_conv, y_attn], [gate_b], [BF16], [],
                     row_in_cols=gate_cols, row_out_widths=[d])
    mix = _dense_fwd("mix_out", mixin, wo)
    h1, u2 = _rows("post_mix", lambda r0, xs, ps: (list(_post_mix(xs[0], xs[1], ps[0], ps[1])), []), [h0, mix],
                   [post_mix_g, pre_ffn_g], [F32, BF16], [])
    ab = _matmul("ffn_in", NN, u2, wfi, pl.BlockSpec((m_rows, d), lambda i: (0, 0)), pl.BlockSpec((None, d, fs), lambda i: (i, 0, 0)),
                 pl.BlockSpec((None, m_rows, fs), lambda i: (i, 0, 0)), jax.ShapeDtypeStruct((N_DEV, m_rows, fs), F32), (N_DEV,))
    half = N_DEV // 2
    tm = _row_tile(m_rows)
    pair = lambda off: pl.BlockSpec((None, tm, fs), lambda j, i, off=off: (j + off, i, 0))
    (f_in,) = _rowwise("swiglu", lambda r0, xs, ps: ([_swiglu(xs[0], xs[1])], []), [ab, ab], [],
                       [jax.ShapeDtypeStruct((half, m_rows, fs), BF16)], [], grid=(half, m_rows // tm),
                       in_specs=[pair(0), pair(half)], out_specs=[pair(0)], tm=tm, row_axis=1)
    f = _matmul("ffn_out", NN, f_in, wfo, pl.BlockSpec((None, m_rows, fs), lambda j: (j, 0, 0)), pl.BlockSpec((None, fs, d), lambda j: (j, 0, 0)),
                pl.BlockSpec((m_rows, d), lambda j: (0, 0)), jax.ShapeDtypeStruct((m_rows, d), F32), (half,), acc_axis=0)

    def loss_head(r0, xs, ps):
        h1_, f_, t_ = xs
        r, vjp = jax.vjp(_rms, f_, ps[0])
        rows = r0 + lax.broadcasted_iota(I32, (h1_.shape[0], 1), 0)
        real = (rows >= n_meta) & (rows < length)
        err = jnp.where(real, h1_ + r - t_, 0.0)
        dh2 = err * (1.0 / d)
        d_f, dg = vjp(dh2)
        part = jnp.sum(0.5 * jnp.mean(err * err, axis=-1, keepdims=True), axis=0, keepdims=True)
        return [d_f, dh2], [dg, jnp.broadcast_to(part, (1, LANE))]

    d_f, dh2, g_post_ffn, loss_part = _rows("loss_head", loss_head, [h1, f, target], [post_ffn_g], [BF16, F32], [d, LANE])

    d_fin = _matmul("ffn_out_dx", NT, d_f, wfo, pl.BlockSpec((m_rows, d), lambda j: (0, 0)), pl.BlockSpec((None, fs, d), lambda j: (j, 0, 0)),
                    pl.BlockSpec((None, m_rows, fs), lambda j: (j, 0, 0)), jax.ShapeDtypeStruct((half, m_rows, fs), F32), (half,))
    g_wfo = _matmul("ffn_out_dw", TN, f_in, d_f, pl.BlockSpec((None, m_rows, fs), lambda j: (j, 0, 0)), pl.BlockSpec((m_rows, d), lambda j: (0, 0)),
                    pl.BlockSpec((None, fs, d), lambda j: (j, 0, 0)), jax.ShapeDtypeStruct((half, fs, d), F32), (half,), twin_bf16=True)

    def swiglu_bwd(r0, xs, ps):
        _, vjp = jax.vjp(_swiglu, xs[0], xs[1])
        return [vjp(xs[2])], []

    (d_ab,) = _rowwise("swiglu_bwd", swiglu_bwd, [ab, ab, d_fin], [],
                       [jax.ShapeDtypeStruct((2, half, m_rows, fs), BF16)], [], grid=(half, m_rows // tm),
                       in_specs=[pair(0), pair(half), pair(0)],
                       out_specs=[pl.BlockSpec((2, None, tm, fs), lambda j, i: (0, j, i, 0))], tm=tm, row_axis=1)
    d_ab = d_ab.reshape(N_DEV, m_rows, fs)
    du2 = _matmul("ffn_in_dx", NT, d_ab, wfi, pl.BlockSpec((None, m_rows, fs), lambda i: (i, 0, 0)), pl.BlockSpec((None, d, fs), lambda i: (i, 0, 0)),
                  pl.BlockSpec((m_rows, d), lambda i: (0, 0)), jax.ShapeDtypeStruct((m_rows, d), F32), (N_DEV,), acc_axis=0)
    g_wfi = _matmul("ffn_in_dw", TN, u2, d_ab, pl.BlockSpec((m_rows, d), lambda i: (0, 0)), pl.BlockSpec((None, m_rows, fs), lambda i: (i, 0, 0)),
                    pl.BlockSpec((None, d, fs), lambda i: (i, 0, 0)), jax.ShapeDtypeStruct((N_DEV, d, fs), F32), (N_DEV,), twin_bf16=True)

    def post_mix_bwd(r0, xs, ps):
        h0_, mix_, dh2_, du2_ = xs
        _, vjp = jax.vjp(_post_mix, h0_, mix_, ps[0], ps[1])
        dh0_, dmix_, dg1, dg2 = vjp((dh2_, du2_))
        return [dmix_, dh0_], [dg1, dg2]

    d_mix, dh1, g_post_mix, g_pre_ffn = _rows("post_mix_bwd", post_mix_bwd, [h0, mix, dh2, du2], [post_mix_g, pre_ffn_g],
                                              [BF16, F32], [d, d])
    d_mixin = _dense_dx("mix_out_dx", d_mix, wo, F32)
    g_wo = _dense_dw("mix_out_dw", mixin, d_mix)

    def gate_mix_bwd(r0, xs, ps):
        _, vjp = jax.vjp(_gate_mix, xs[0], xs[1], xs[2], xs[3], ps[0])
        dpgc, dpga, dyc_, dya_, dgb = vjp(xs[4])
        return [dpgc, dpga, dyc_, dya_], [dgb]

    dp_gc, dp_ga, d_yconv, d_yattn, g_gate_b = _rows(
        "gate_mix_bwd", gate_mix_bwd, [p, p, y_conv, y_attn, d_mixin], [gate_b], [BF16] * 4, [2 * d],
        row_in_cols=gate_cols + [None], row_out_widths=[d] * 4)
    d_o = _dense_dx("attn_out_dx", d_yattn, wao, BF16)
    g_wao = _dense_dw("attn_out_dw", o, d_yattn)
    big = {"w_ffn_out": [g.reshape(N_DEV, fr, d) for g in g_wfo], "w_ffn_in": g_wfi,
           "w_o": [g.reshape(N_DEV, dc, d) for g in g_wo], "w_attn_out": [g.reshape(N_DEV, dc, d) for g in g_wao]}
    early = ("w_ffn_out", "w_ffn_in", "w_o", "w_attn_out")
    dq, dk, dv, received_early = _attn_bwd(p16, d_o, tot, d, [big[k][1] for k in early])
    d_yc = _dense_dx("conv_out_dx", d_yconv, wco, F32)
    g_wco = _dense_dw("conv_out_dw", yc, d_yconv)

    def conv_norm_bwd(r0, xs, ps):
        _, vjp = jax.vjp(_ln_silu, xs[0], ps[0], ps[1])
        dy_, dg, db = vjp(xs[1])
        return [dy_], [dg, db]

    d_y, g_ln_g, g_ln_b = _rows("conv_norm_bwd", conv_norm_bwd, [y, d_yc], [conv_ln_g, conv_ln_b], [F32], [d, d])
    dp_a, dp_g, g_dww, g_dwb = _conv_bwd(p, d_y, dww_full, d)
    dp = jnp.concatenate([dp_a, dp_g, dq, dk, dv, dp_gc, dp_ga], axis=1)
    du = _matmul("in_proj_dx", NT, dp, wi, pl.BlockSpec((m_rows, ns), lambda i: (0, i)), pl.BlockSpec((None, d, ns), lambda i: (i, 0, 0)),
                 pl.BlockSpec((m_rows, d), lambda i: (0, 0)), jax.ShapeDtypeStruct((m_rows, d), F32), (N_DEV,), acc_axis=0)
    g_wi = _matmul("in_proj_dw", TN, u, dp, pl.BlockSpec((m_rows, d), lambda i: (0, 0)), pl.BlockSpec((m_rows, ns), lambda i: (0, i)),
                   pl.BlockSpec((None, d, ns), lambda i: (i, 0, 0)), jax.ShapeDtypeStruct((N_DEV, d, ns), F32), (N_DEV,), twin_bf16=True)

    def pre_mix_bwd(r0, xs, ps):
        _, vjp = jax.vjp(_rms, xs[0], ps[0])
        dx, dg = vjp(xs[1])
        return [xs[2] + dx], [dg]

    dh0, g_pre_mix = _rows("pre_mix_bwd", pre_mix_bwd, [h0, du, dh1], [pre_mix_g], [F32], [d])
    grad_x = dh0[n_meta:length][None]

    x_i, y_i, c_i = _mesh_pos()
    me = _dev(x_i, y_i, c_i)
    me_arr = jnp.reshape(me, (1,)).astype(I32)
    big["w_in"] = g_wi
    big["w_conv_out"] = [g.reshape(N_DEV, dc, d) for g in g_wco]
    late = ("w_conv_out", "w_in")
    received = dict(zip(early, received_early))
    received.update(zip(late, _scatter_exchange([big[k][1] for k in late])))
    results = {}
    for k in SHARDED:
        outs = _adamw_shard(me_arr, big[k][0], received[k], local[k], given["m_" + k][0], given["v_" + k][0])
        results[k] = tuple(a[None] for a in outs)

    rep_grads = {"pre_mix_g": g_pre_mix, "gate_b": g_gate_b, "dw_b": g_dwb, "conv_ln_g": g_ln_g, "conv_ln_b": g_ln_b,
                 "post_mix_g": g_post_mix, "pre_ffn_g": g_pre_ffn, "post_ffn_g": g_post_ffn}

    def pack_rep(get):
        return jnp.concatenate([get(k) for k in REPLICATED], axis=1).reshape(-1, LANE)

    rep_rows = pack_rep(lambda k: rep_grads[k])
    n_rep = rep_rows.shape[0]
    loss_rows = jnp.broadcast_to(loss_part, (8, LANE))
    g_meta = dh0[0:n_meta]
    slabs = jnp.stack([jnp.concatenate([rep_rows, loss_rows, g_dww[:, j * LANE:(j + 1) * LANE], g_meta[:, j * LANE:(j + 1) * LANE]], axis=0)
                       for j in range(N_DEV)])

    def pack_small(prefix):
        dww_own = jnp.pad(given[prefix + "dw_w"][0], ((0, CONV_PAD - CONV_WIDTH), (0, 0)))
        return jnp.concatenate([pack_rep(lambda k: given[prefix + k]), jnp.zeros((8, LANE), F32), dww_own,
                                given[prefix + "meta_tokens"]], axis=0)

    small = _small_reduce_adamw(slabs, pack_small(""), pack_small("m_"), pack_small("v_"))
    loss = small[0][n_rep, 0]

    def unpack(arr):
        out = {}
        flat = arr[:n_rep].reshape(1, -1)
        off = 0
        for k in REPLICATED:
            w = given[k].shape[1]
            out[k] = flat[:, off:off + w]
            off += w
        out["dw_w"] = arr[n_rep + 8:n_rep + 8 + CONV_WIDTH][None]
        out["meta_tokens"] = arr[n_rep + 8 + CONV_PAD:n_rep + 8 + CONV_PAD + n_meta]
        return out

    small_out = [unpack(a) for a in small]
    for k in WEIGHTS:
        if k not in results:
            results[k] = tuple(s[k] for s in small_out)
    return (loss, grad_x, *[results[k][0] for k in WEIGHTS], *[results[k][1] for k in WEIGHTS],
            *[results[k][2] for k in WEIGHTS], *[results[k][3] for k in WEIGHTS])
```

```python
import jax
import jax.numpy as jnp
from jax import lax
from jax.experimental import pallas as pl
from jax.experimental.pallas import tpu as pltpu

F32 = jnp.float32
BF16 = jnp.bfloat16
I32 = jnp.int32

N_DEV = 8
LANE = 128
HEAD_DIM = 64
QB = 128
KEY_SHIFT = 2
KEY_TILES = 1 << KEY_SHIFT
KEY_CHUNK = KEY_TILES * QB
LANE_BLOCKS = 2
BWD_LANE_BLOCKS = 2
CONV_WIDTH = 31
CONV_PAD = 32
ROW_CHUNK = 128
RMS_EPS = 1e-6
LN_EPS = 1e-5
ADAM_LR = 0.001
ADAM_B1 = 0.9
ADAM_B2 = 0.999
ADAM_EPS = 1e-08
ADAM_WD = 0.01
ADAM_STEP = 10
VMEM_LIMIT = 56 * 1024 * 1024

NN = (((1,), (0,)), ((), ()))
NT = (((1,), (1,)), ((), ()))
TN = (((0,), (0,)), ((), ()))
MESH = pl.DeviceIdType.MESH
ANY = pl.BlockSpec(memory_space=pl.ANY)
VMEM_WHOLE = pl.BlockSpec(memory_space=pltpu.VMEM)


def _params(n_axes):
    return pltpu.CompilerParams(dimension_semantics=("arbitrary",) * n_axes, vmem_limit_bytes=VMEM_LIMIT)


def _row_tile(m):
    assert m % QB == 0
    return m // 4 if m % 64 == 0 else QB


def _matmul(name, dims, a, b, a_spec, b_spec, o_spec, out_shape, grid, acc_axis=None, twin_bf16=False, carried=None):
    n_car = 0 if carried is None else len(carried[1])

    def body(a_ref, b_ref, *rest):
        car_ins, o_ref, rest = rest[:n_car], rest[n_car], rest[n_car + 1:]
        if carried is not None:
            starts, waits = carried[0](car_ins, rest[:n_car], *rest[n_car:])

            @pl.when(pl.program_id(0) == 0)
            def _():
                for f in starts:
                    f()

        r = lax.dot_general(a_ref[...], b_ref[...], dims, preferred_element_type=F32)
        if acc_axis is None:
            o_ref[...] = r.astype(o_ref.dtype)
            if twin_bf16:
                rest[0][...] = r.astype(BF16)
        else:
            k = pl.program_id(acc_axis)

            @pl.when(k == 0)
            def _():
                o_ref[...] = r

            @pl.when(k > 0)
            def _():
                o_ref[...] += r

        if carried is not None:
            @pl.when(pl.program_id(0) == grid[0] - 1)
            def _():
                for f in waits:
                    f()

    in_specs, out_specs, out_shapes, scratch = [a_spec, b_spec], [o_spec], [out_shape], []
    if twin_bf16:
        assert acc_axis is None and carried is None
        out_specs.append(o_spec)
        out_shapes.append(jax.ShapeDtypeStruct(out_shape.shape, BF16))
    if carried is not None:
        assert len(grid) == 1
        in_specs += [ANY] * n_car
        out_specs += [ANY] * n_car
        out_shapes += list(carried[2])
        scratch = list(carried[3])
    outs = pl.pallas_call(body, name=name, grid=grid, in_specs=in_specs, out_specs=out_specs, out_shape=out_shapes,
                          scratch_shapes=scratch, compiler_params=_params(len(grid)))(a, b, *(carried[1] if carried else ()))
    return outs[0] if len(outs) == 1 else outs


def _dense_fwd(name, a, w, out_dtype=F32):
    m, k = a.shape
    n = w.shape[1]
    tn = 512
    return _matmul(name, NN, a, w, pl.BlockSpec((m, k), lambda j: (0, 0)), pl.BlockSpec((k, tn), lambda j: (0, j)),
                   pl.BlockSpec((m, tn), lambda j: (0, j)), jax.ShapeDtypeStruct((m, n), out_dtype), (n // tn,))


def _dense_dx(name, dy, w, out_dtype):
    m, n = dy.shape
    k = w.shape[0]
    tk = 512
    return _matmul(name, NT, dy, w, pl.BlockSpec((m, n), lambda j: (0, 0)), pl.BlockSpec((tk, n), lambda j: (j, 0)),
                   pl.BlockSpec((m, tk), lambda j: (0, j)), jax.ShapeDtypeStruct((m, k), out_dtype), (k // tk,))


def _dense_dw(name, a, dy):
    m, k = a.shape
    n = dy.shape[1]
    tn = 512
    return _matmul(name, TN, a, dy, pl.BlockSpec((m, k), lambda j: (0, 0)), pl.BlockSpec((m, tn), lambda j: (0, j)),
                   pl.BlockSpec((k, tn), lambda j: (0, j)), jax.ShapeDtypeStruct((k, n), F32), (n // tn,), twin_bf16=True)


def _rowwise(name, fn, row_ins, par_ins, row_outs, par_outs, *, grid, in_specs, out_specs, tm, row_axis):
    n_ri, n_pi, n_ro, n_po = len(row_ins), len(par_ins), len(row_outs), len(par_outs)
    n_steps, tail = divmod(tm, ROW_CHUNK)
    assert tail % 16 == 0

    def body(*refs):
        ri = refs[:n_ri]
        pi = refs[n_ri:n_ri + n_pi]
        ro = refs[n_ri + n_pi:n_ri + n_pi + n_ro]
        po = refs[n_ri + n_pi + n_ro:]
        ps = [r[...] for r in pi]
        base = pl.program_id(row_axis) * tm

        def chunk(r0, rows, carry):
            xs = [r[pl.ds(r0, rows), :] for r in ri]
            outs, pouts = fn(base + r0, xs, ps)
            for r, o in zip(ro, outs):
                if isinstance(o, (list, tuple)):
                    for j, part in enumerate(o):
                        r[j, pl.ds(r0, rows), :] = part.astype(r.dtype)
                else:
                    r[pl.ds(r0, rows), :] = o.astype(r.dtype)
            return tuple(c + q for c, q in zip(carry, pouts))

        def step(i, carry):
            return chunk(pl.multiple_of(i * ROW_CHUNK, ROW_CHUNK), ROW_CHUNK, carry)

        acc = lax.fori_loop(0, n_steps, step, tuple(jnp.zeros(s.shape, F32) for s in par_outs))
        if tail:
            acc = chunk(n_steps * ROW_CHUNK, tail, acc)
        if n_po:
            first = pl.program_id(0) == 0
            for ax in range(1, len(grid)):
                first = first & (pl.program_id(ax) == 0)

            @pl.when(first)
            def _():
                for r in po:
                    r[...] = jnp.zeros_like(r)

            for r, a in zip(po, acc):
                r[...] += a

    return pl.pallas_call(body, name=name, grid=grid, in_specs=in_specs, out_specs=out_specs,
                          out_shape=tuple(row_outs) + tuple(par_outs),
                          compiler_params=_params(len(grid)))(*row_ins, *par_ins)


def _rows(name, fn, row_ins, par_ins, row_out_dtypes, par_out_widths, row_in_cols=None, row_out_widths=None):
    m = row_ins[0].shape[0]
    tm = _row_tile(m)
    in_specs = []
    for k, a in enumerate(row_ins):
        if row_in_cols is not None and row_in_cols[k] is not None:
            width, cb = row_in_cols[k]
            in_specs.append(pl.BlockSpec((tm, width), lambda i, cb=cb: (i, cb)))
        else:
            in_specs.append(pl.BlockSpec((tm, a.shape[1]), lambda i: (i, 0)))
    for a in par_ins:
        in_specs.append(pl.BlockSpec(a.shape, lambda i: (0, 0)))
    if row_out_widths is None:
        row_out_widths = [row_ins[0].shape[1]] * len(row_out_dtypes)
    row_outs = [jax.ShapeDtypeStruct((m, w), dt) for w, dt in zip(row_out_widths, row_out_dtypes)]
    par_outs = [jax.ShapeDtypeStruct((1, w), F32) for w in par_out_widths]
    out_specs = [pl.BlockSpec((tm, s.shape[1]), lambda i: (i, 0)) for s in row_outs]
    out_specs += [pl.BlockSpec(s.shape, lambda i: (0, 0)) for s in par_outs]
    return _rowwise(name, fn, row_ins, par_ins, row_outs, par_outs, grid=(m // tm,), in_specs=in_specs,
                    out_specs=out_specs, tm=tm, row_axis=0)


def _rms(x, g):
    return x * lax.rsqrt(jnp.mean(x * x, axis=-1, keepdims=True) + RMS_EPS) * g


def _ln_silu(y, g, b):
    mu = jnp.mean(y, axis=-1, keepdims=True)
    yc = y - mu
    var = jnp.mean(yc * yc, axis=-1, keepdims=True)
    return jax.nn.silu(yc * lax.rsqrt(var + LN_EPS) * g + b)


def _gate_mix(pgc, pga, yc, ya, gb):
    d = pgc.shape[1]
    return jax.nn.sigmoid(pgc + gb[:, :d]) * yc + jax.nn.sigmoid(pga + gb[:, d:]) * ya


def _post_mix(h0, mix, g_post, g_pre):
    h1 = h0 + _rms(mix, g_post)
    return h1, _rms(h1, g_pre)


def _swiglu(a, b):
    return jax.nn.silu(a) * b


def _conv_taps():
    taps = []
    for b in range(8):
        for a in range(CONV_PAD // 8):
            s = 8 * a + b
            if s < CONV_WIDTH:
                taps.append((b, a, CONV_WIDTH - 1 - s))
    return taps


def _conv_fwd(p, dww, dwb, d_model):
    m = p.shape[0]
    nch = d_model // LANE
    n_chunk = m // QB
    taps = _conv_taps()

    def body(a_ref, g_ref, w_ref, b_ref, y_ref, upad):
        upad[0:CONV_PAD, :] = jnp.zeros((CONV_PAD, LANE), F32)

        def fill(i, c):
            r0 = pl.multiple_of(i * QB, QB)
            u = a_ref[pl.ds(r0, QB), :] * jax.nn.sigmoid(g_ref[pl.ds(r0, QB), :])
            upad[pl.ds(pl.multiple_of(r0 + CONV_PAD, 8), QB), :] = u
            return c

        lax.fori_loop(0, n_chunk, fill, 0)

        def conv(i, c):
            r0 = pl.multiple_of(i * QB, QB)
            win = upad[pl.ds(r0, QB + CONV_PAD), :]
            acc = jnp.broadcast_to(b_ref[...], (QB, LANE))
            rolled = {}
            for b, a, j in taps:
                if b not in rolled:
                    rolled[b] = win if b == 0 else pltpu.roll(win, b, axis=0)
                lo = CONV_PAD - 8 * a
                acc = acc + w_ref[j:j + 1, :] * rolled[b][lo:lo + QB, :]
            y_ref[pl.ds(r0, QB), :] = acc
            return c

        lax.fori_loop(0, n_chunk, conv, 0)

    col = lambda off: pl.BlockSpec((m, LANE), lambda c: (0, off + c))
    return pl.pallas_call(
        body, name="conv_fwd", grid=(nch,),
        in_specs=[col(0), col(nch), pl.BlockSpec((CONV_PAD, LANE), lambda c: (0, c)), pl.BlockSpec((1, LANE), lambda c: (0, c))],
        out_specs=col(0), out_shape=jax.ShapeDtypeStruct((m, d_model), F32),
        scratch_shapes=[pltpu.VMEM((m + CONV_PAD, LANE), F32)], compiler_params=_params(1))(p, p, dww, dwb)


def _conv_bwd(p, dy, dww, d_model):
    m = p.shape[0]
    nch = d_model // LANE
    n_chunk = m // QB
    taps = _conv_taps()
    win_rows = QB + CONV_PAD

    def body(a_ref, g_ref, dy_ref, w_ref, da_ref, dg_ref, dw_ref, db_ref, upad, dypad, wacc, bacc):
        upad[0:CONV_PAD, :] = jnp.zeros((CONV_PAD, LANE), F32)
        dypad[m:m + CONV_PAD, :] = jnp.zeros((CONV_PAD, LANE), F32)
        wacc[...] = jnp.zeros_like(wacc)
        bacc[...] = jnp.zeros_like(bacc)

        def fill(i, c):
            r0 = pl.multiple_of(i * QB, QB)
            u = a_ref[pl.ds(r0, QB), :] * jax.nn.sigmoid(g_ref[pl.ds(r0, QB), :])
            upad[pl.ds(pl.multiple_of(r0 + CONV_PAD, 8), QB), :] = u
            dypad[pl.ds(r0, QB), :] = dy_ref[pl.ds(r0, QB), :]
            return c

        lax.fori_loop(0, n_chunk, fill, 0)

        def chunk(i, c):
            r0 = pl.multiple_of(i * QB, QB)
            dwin = dypad[pl.ds(r0, win_rows), :]
            du = jnp.zeros((QB, LANE), F32)
            rolled = {}
            for b, a, j in taps:
                if b not in rolled:
                    rolled[b] = dwin if b == 0 else pltpu.roll(dwin, win_rows - b, axis=0)
                du = du + w_ref[j:j + 1, :] * rolled[b][8 * a:8 * a + QB, :]
            av = a_ref[pl.ds(r0, QB), :]
            sg = jax.nn.sigmoid(g_ref[pl.ds(r0, QB), :])
            da_ref[pl.ds(r0, QB), :] = (du * sg).astype(da_ref.dtype)
            dg_ref[pl.ds(r0, QB), :] = (du * av * sg * (1.0 - sg)).astype(dg_ref.dtype)
            dyc = dy_ref[pl.ds(r0, QB), :]
            uwin = upad[pl.ds(r0, win_rows), :]
            rolled = {}
            for b, a, j in taps:
                if b not in rolled:
                    rolled[b] = uwin if b == 0 else pltpu.roll(uwin, b, axis=0)
                lo = CONV_PAD - 8 * a
                prod = dyc * rolled[b][lo:lo + QB, :]
                wacc[j] += prod.reshape(QB // 8, 8, LANE).sum(axis=0)
            bacc[...] += dyc.reshape(QB // 8, 8, LANE).sum(axis=0)
            return c

        lax.fori_loop(0, n_chunk, chunk, 0)
        for j in range(CONV_WIDTH):
            dw_ref[j:j + 1, :] = jnp.sum(wacc[j], axis=0, keepdims=True)
        dw_ref[CONV_WIDTH:CONV_PAD, :] = jnp.zeros((CONV_PAD - CONV_WIDTH, LANE), F32)
        db_ref[...] = jnp.sum(bacc[...], axis=0, keepdims=True)

    col = lambda off: pl.BlockSpec((m, LANE), lambda c: (0, off + c))
    return pl.pallas_call(
        body, name="conv_bwd", grid=(nch,),
        in_specs=[col(0), col(nch), col(0), pl.BlockSpec((CONV_PAD, LANE), lambda c: (0, c))],
        out_specs=[col(0), col(0), pl.BlockSpec((CONV_PAD, LANE), lambda c: (0, c)), pl.BlockSpec((1, LANE), lambda c: (0, c))],
        out_shape=(jax.ShapeDtypeStruct((m, d_model), BF16), jax.ShapeDtypeStruct((m, d_model), BF16),
                   jax.ShapeDtypeStruct((CONV_PAD, d_model), F32), jax.ShapeDtypeStruct((1, d_model), F32)),
        scratch_shapes=[pltpu.VMEM((m + CONV_PAD, LANE), F32), pltpu.VMEM((m + CONV_PAD, LANE), F32),
                        pltpu.VMEM((CONV_PAD, 8, LANE), F32), pltpu.VMEM((8, LANE), F32)],
        compiler_params=_params(1))(p, p, dy, dww)


EXP_CLAMP = 80.0


def _one_plus_exp(z):
    return 1.0 + jnp.exp(jnp.minimum(z, EXP_CLAMP))


def _softplus(z):
    return jnp.maximum(jnp.log(_one_plus_exp(z)), z)


def _softplus_sigmoid(z):
    s = _one_plus_exp(z)
    return jnp.maximum(jnp.log(s), z), 1.0 - 1.0 / s


def _split_dot(x, tri):
    hi = pltpu.bitcast(pltpu.bitcast(x, jnp.uint32) & jnp.uint32(0xFFFF0000), F32)
    lo = x - hi
    return jnp.dot(jnp.concatenate([hi.astype(BF16), lo.astype(BF16)], axis=1), tri, preferred_element_type=F32)


def _tri(kind):
    jj = lax.broadcasted_iota(I32, (2 * QB, 2 * QB), 0) & (QB - 1)
    ss = lax.broadcasted_iota(I32, (2 * QB, 2 * QB), 1)
    keep = {"ge": jj >= ss, "lt": jj < ss, "le": jj <= ss}[kind]
    return jnp.where((ss >= QB) | keep, 1.0, 0.0).astype(BF16)


def _attn_fwd(p, d_model, shards):
    m = p.shape[0]
    nqb = m // QB
    ngrp = d_model // (LANE_BLOCKS * LANE)
    qo, ko, vo = 2 * ngrp, 3 * ngrp, 4 * ngrp
    scale = HEAD_DIM ** -0.5
    n_sh = len(shards)

    assert nqb >= KEY_TILES

    def body(q_ref, k_ref, v_ref, *rest):
        shard_refs, (o_ref, t_ref), rest = rest[:n_sh], rest[n_sh:n_sh + 2], rest[n_sh + 2:]
        gathered_refs, (acc_ref, car_ref), sems = rest[:n_sh], rest[n_sh:n_sh + 2], rest[n_sh + 2:]
        starts, relays, waits = _gather_copies(shard_refs, gathered_refs, *sems)

        @pl.when((pl.program_id(0) == 0) & (pl.program_id(1) == 0))
        def _():
            for f in starts:
                f()

        @pl.when((pl.program_id(0) == ngrp - 1) & (pl.program_id(1) == 0))
        def _():
            for f in relays:
                f()

        qb = pl.program_id(1)
        lane = lax.broadcasted_iota(I32, (QB, LANE), 1)
        head0 = lane < HEAD_DIM
        row_g = qb * QB + lax.broadcasted_iota(I32, (QB, KEY_CHUNK), 0)
        col_l = lax.broadcasted_iota(I32, (QB, KEY_CHUNK), 1)
        tri = _tri("ge")
        heads = range(2 * LANE_BLOCKS)
        qh = []
        for lb in range(LANE_BLOCKS):
            q2 = (q_ref[:, lb * LANE:(lb + 1) * LANE] * scale).astype(BF16)
            zero = jnp.zeros_like(q2)
            qh += [jnp.where(head0, q2, zero), jnp.where(head0, zero, q2)]
        acc_ref[...] = jnp.zeros_like(acc_ref)
        car_ref[...] = jnp.zeros_like(car_ref)

        def chunk(first_tile, bound):
            r0 = pl.multiple_of(first_tile * QB, QB)
            kcs = [k_ref[pl.ds(r0, KEY_CHUNK), lb * LANE:(lb + 1) * LANE].astype(BF16) for lb in range(LANE_BLOCKS)]
            vcs = [v_ref[pl.ds(r0, KEY_CHUNK), lb * LANE:(lb + 1) * LANE].astype(BF16) for lb in range(LANE_BLOCKS)]
            valid = None if bound is None else (col_l + r0) < bound
            zs = [lax.dot_general(qh[h], kcs[h // 2], NT, preferred_element_type=F32) for h in heads]
            sps = [_softplus(z) for z in zs]
            if valid is not None:
                sps = [jnp.where(valid, sp, 0.0) for sp in sps]
            crs = [[_split_dot(sp[:, i * QB:(i + 1) * QB], tri) for i in range(KEY_TILES)] for sp in sps]
            cars = [car_ref[h] for h in heads]
            a_tiles = [[None] * KEY_TILES for h in heads]
            for i in reversed(range(KEY_TILES)):
                for h in heads:
                    a_tiles[h][i] = jnp.exp(zs[h][:, i * QB:(i + 1) * QB] - (crs[h][i][:, :QB] + cars[h]))
                    cars[h] = cars[h] + crs[h][i][:, QB:]
            for h in heads:
                a = jnp.concatenate(a_tiles[h], axis=1)
                if valid is not None:
                    a = jnp.where(valid, a, 0.0)
                acc_ref[h] += jnp.dot(a.astype(BF16), vcs[h // 2], preferred_element_type=F32)
                car_ref[h] = cars[h]

        near = jnp.maximum(qb - (KEY_TILES - 1), 0)
        chunk(near, row_g)
        n_full = lax.shift_right_logical(near, KEY_SHIFT)

        def step(i, c):
            chunk(near - KEY_TILES * (i + 1), None)
            return c

        lax.fori_loop(0, n_full, step, 0)
        left = near - KEY_TILES * n_full

        @pl.when(left > 0)
        def _():
            chunk(0, left * QB)

        for lb in range(LANE_BLOCKS):
            o_ref[:, lb * LANE:(lb + 1) * LANE] = jnp.where(head0, acc_ref[2 * lb], acc_ref[2 * lb + 1]).astype(o_ref.dtype)
        for h in heads:
            t_ref[:, h * QB:(h + 1) * QB] = car_ref[h]

        @pl.when((pl.program_id(0) == ngrp - 1) & (pl.program_id(1) == nqb - 1))
        def _():
            for f in waits:
                f()

    wide = LANE_BLOCKS * LANE
    outs = pl.pallas_call(
        body, name="attn_fwd", grid=(ngrp, nqb),
        in_specs=[pl.BlockSpec((QB, wide), lambda g, qb: (qb, qo + g)),
                  pl.BlockSpec((m, wide), lambda g, qb: (0, ko + g)),
                  pl.BlockSpec((m, wide), lambda g, qb: (0, vo + g))] + [ANY] * n_sh,
        out_specs=[pl.BlockSpec((QB, wide), lambda g, qb: (qb, g)),
                   pl.BlockSpec((QB, 2 * wide), lambda g, qb: (qb, g))] + [ANY] * n_sh,
        out_shape=[jax.ShapeDtypeStruct((m, d_model), BF16), jax.ShapeDtypeStruct((m, 2 * d_model), F32)]
        + [jax.ShapeDtypeStruct((N_DEV,) + s.shape, s.dtype) for s in shards],
        scratch_shapes=[pltpu.VMEM((2 * LANE_BLOCKS, QB, LANE), F32), pltpu.VMEM((2 * LANE_BLOCKS, QB, LANE), F32)]
        + _exchange_sems(n_sh) + [pltpu.SemaphoreType.DMA((n_sh,))],
        compiler_params=_params(2))(p, p, p, *shards)
    return outs[0], outs[1], outs[2:]


def _attn_bwd(p, d_o, tot, d_model, slabs):
    m = p.shape[0]
    nqb = m // QB
    blocks = BWD_LANE_BLOCKS
    ngrp = d_model // (blocks * LANE)
    qo, ko, vo = 2 * ngrp, 3 * ngrp, 4 * ngrp
    scale = HEAD_DIM ** -0.5
    n_sl = len(slabs)

    assert nqb >= KEY_TILES

    def body(q_ref, k_ref, v_ref, do_ref, t_ref, *rest):
        slab_refs, (dq_ref, dk_ref, dv_ref), rest = rest[:n_sl], rest[n_sl:n_sl + 3], rest[n_sl + 3:]
        recv_refs, (dkacc, dvacc, dqacc, csp, cg), sems = rest[:n_sl], rest[n_sl:n_sl + 5], rest[n_sl + 5:]
        starts, waits = _scatter_copies(slab_refs, recv_refs, *sems)

        @pl.when((pl.program_id(0) == 0) & (pl.program_id(1) == 0))
        def _():
            for f in starts:
                f()

        qb = pl.program_id(1)
        lane = lax.broadcasted_iota(I32, (QB, LANE), 1)
        head0 = lane < HEAD_DIM
        row_g = qb * QB + lax.broadcasted_iota(I32, (QB, KEY_CHUNK), 0)
        col_l = lax.broadcasted_iota(I32, (QB, KEY_CHUNK), 1)
        tri_lt = _tri("lt")
        tri_le = _tri("le")
        heads = range(2 * blocks)
        qh, doh = [], []
        for lb in range(blocks):
            q2 = (q_ref[:, lb * LANE:(lb + 1) * LANE] * scale).astype(BF16)
            do2 = do_ref[:, lb * LANE:(lb + 1) * LANE]
            zero = jnp.zeros_like(q2)
            qh += [jnp.where(head0, q2, zero), jnp.where(head0, zero, q2)]
            doh += [jnp.where(head0, do2, zero), jnp.where(head0, zero, do2)]
        q_pairs = [jnp.concatenate(qh[2 * lb:2 * lb + 2], axis=0) for lb in range(blocks)]
        do_pairs = [jnp.concatenate(doh[2 * lb:2 * lb + 2], axis=0) for lb in range(blocks)]

        @pl.when(qb == 0)
        def _():
            dkacc[...] = jnp.zeros_like(dkacc)
            dvacc[...] = jnp.zeros_like(dvacc)

        dqacc[...] = jnp.zeros_like(dqacc)
        csp[...] = jnp.zeros_like(csp)
        cg[...] = jnp.zeros_like(cg)

        def chunk(first_tile, bound):
            r0 = pl.multiple_of(first_tile * QB, QB)
            kcs = [k_ref[pl.ds(r0, KEY_CHUNK), lb * LANE:(lb + 1) * LANE].astype(BF16) for lb in range(blocks)]
            vcs = [v_ref[pl.ds(r0, KEY_CHUNK), lb * LANE:(lb + 1) * LANE].astype(BF16) for lb in range(blocks)]
            valid = None if bound is None else (col_l + r0) < bound
            tiles = [slice(i * QB, (i + 1) * QB) for i in range(KEY_TILES)]
            zs = [lax.dot_general(qh[h], kcs[h // 2], NT, preferred_element_type=F32) for h in heads]
            das = [lax.dot_general(doh[h], vcs[h // 2], NT, preferred_element_type=F32) for h in heads]
            sps, sgs = zip(*[_softplus_sigmoid(z) for z in zs])
            if valid is not None:
                sps = [jnp.where(valid, sp, 0.0) for sp in sps]
            crs = [[_split_dot(sp[:, c], tri_lt) for c in tiles] for sp in sps]
            a_tiles, g_tiles = [[] for h in heads], [[] for h in heads]
            for h in heads:
                tot_h = t_ref[:, h * QB:(h + 1) * QB]
                before = csp[h]
                for i, c in enumerate(tiles):
                    a = jnp.exp(zs[h][:, c] - (tot_h - (before + crs[h][i][:, :QB])))
                    if valid is not None:
                        a = jnp.where(valid[:, c], a, 0.0)
                    a_tiles[h].append(a)
                    g_tiles[h].append(a * das[h][:, c])
                    before = before + crs[h][i][:, QB:]
                csp[h] = before
            grs = [[_split_dot(g, tri_le) for g in g_tiles[h]] for h in heads]
            dzbs, abs_ = [], []
            for h in heads:
                g_before = cg[h]
                dz_tiles = []
                for i, c in enumerate(tiles):
                    dz = g_tiles[h][i] - sgs[h][:, c] * (g_before + grs[h][i][:, :QB])
                    if valid is not None:
                        dz = jnp.where(valid[:, c], dz, 0.0)
                    dz_tiles.append(dz)
                    g_before = g_before + grs[h][i][:, QB:]
                cg[h] = g_before
                dzbs.append(jnp.concatenate(dz_tiles, axis=1).astype(BF16))
                abs_.append(jnp.concatenate(a_tiles[h], axis=1).astype(BF16))
            for h in heads:
                dqacc[h] += jnp.dot(dzbs[h], kcs[h // 2], preferred_element_type=F32)
            for lb in range(blocks):
                dz_pair = jnp.concatenate(dzbs[2 * lb:2 * lb + 2], axis=0)
                a_pair = jnp.concatenate(abs_[2 * lb:2 * lb + 2], axis=0)
                dkacc[pl.ds(r0, KEY_CHUNK), lb * LANE:(lb + 1) * LANE] += lax.dot_general(
                    dz_pair, q_pairs[lb], TN, preferred_element_type=F32)
                dvacc[pl.ds(r0, KEY_CHUNK), lb * LANE:(lb + 1) * LANE] += lax.dot_general(
                    a_pair, do_pairs[lb], TN, preferred_element_type=F32)

        near = jnp.maximum(qb - (KEY_TILES - 1), 0)
        n_full = lax.shift_right_logical(near, KEY_SHIFT)

        def step(i, c):
            chunk(KEY_TILES * i, None)
            return c

        lax.fori_loop(0, n_full, step, 0)

        @pl.when(near > KEY_TILES * n_full)
        def _():
            chunk(KEY_TILES * n_full, near * QB)

        chunk(near, row_g)
        for lb in range(blocks):
            dq2 = jnp.where(head0, dqacc[2 * lb], dqacc[2 * lb + 1]) * scale
            dq_ref[:, lb * LANE:(lb + 1) * LANE] = dq2.astype(dq_ref.dtype)

        @pl.when(qb == nqb - 1)
        def _():
            dk_ref[...] = dkacc[...].astype(dk_ref.dtype)
            dv_ref[...] = dvacc[...].astype(dv_ref.dtype)

        @pl.when((pl.program_id(0) == ngrp - 1) & (qb == nqb - 1))
        def _():
            for f in waits:
                f()

    out = jax.ShapeDtypeStruct((m, d_model), BF16)
    wide = blocks * LANE
    carry = pltpu.VMEM((2 * blocks, QB, LANE), F32)
    outs = pl.pallas_call(
        body, name="attn_bwd", grid=(ngrp, nqb),
        in_specs=[pl.BlockSpec((QB, wide), lambda g, qb: (qb, qo + g)),
                  pl.BlockSpec((m, wide), lambda g, qb: (0, ko + g)),
                  pl.BlockSpec((m, wide), lambda g, qb: (0, vo + g)),
                  pl.BlockSpec((QB, wide), lambda g, qb: (qb, g)),
                  pl.BlockSpec((QB, 2 * wide), lambda g, qb: (qb, g))] + [ANY] * n_sl,
        out_specs=[pl.BlockSpec((QB, wide), lambda g, qb: (qb, g)),
                   pl.BlockSpec((m, wide), lambda g, qb: (0, g)),
                   pl.BlockSpec((m, wide), lambda g, qb: (0, g))] + [ANY] * n_sl,
        out_shape=[out, out, out] + _received_shapes(slabs),
        scratch_shapes=[pltpu.VMEM((m, wide), F32), pltpu.VMEM((m, wide), F32), carry, carry, carry] + _exchange_sems(n_sl),
        compiler_params=_params(2))(p, p, p, d_o, tot, *slabs)
    return outs[0], outs[1], outs[2], outs[3:]


def _mesh_pos():
    return lax.axis_index("x"), lax.axis_index("y"), lax.axis_index("c")


def _other_chips(x, y):
    return [(1 - x, y), (x, 1 - y), (1 - x, 1 - y)]


def _dev(x, y, c):
    return 4 * x + 2 * y + c


def _all_gather(shards):
    n = len(shards)

    def body(*refs):
        starts, relays, waits = _gather_copies(refs[:n], refs[n:2 * n], *refs[2 * n:])
        for f in starts + relays + waits:
            f()

    return pl.pallas_call(
        body, name="comm_all_gather", in_specs=[ANY] * n, out_specs=[ANY] * n,
        out_shape=[jax.ShapeDtypeStruct((N_DEV,) + s.shape, s.dtype) for s in shards],
        scratch_shapes=[pltpu.SemaphoreType.DMA((n, 7)), pltpu.SemaphoreType.DMA((n, 7)), pltpu.SemaphoreType.DMA((n,))],
    )(*shards)


def _peers(x, y, c):
    out = []
    for mask in range(1, N_DEV):
        px, py, pc = x ^ (mask >> 2), y ^ ((mask >> 1) & 1), c ^ (mask & 1)
        out.append((mask - 1, (px, py, pc), _dev(px, py, pc)))
    return out


def _remote(src, dst, send_sems, recv_sems, k, s, peer):
    return pltpu.make_async_remote_copy(src_ref=src, dst_ref=dst, send_sem=send_sems.at[k, s], recv_sem=recv_sems.at[k, s],
                                        device_id=peer, device_id_type=MESH)


def _gather_copies(ins, outs, send_sems, recv_sems, local_sems):
    x, y, c = _mesh_pos()
    sibling = (x, y, 1 - c)
    chips = _other_chips(x, y)
    starts, relays, waits = [], [], []
    for k in range(len(ins)):
        def slot(block, k=k):
            return outs[k].at[_dev(*block)]

        def copy(s, src, block, to, k=k):
            return _remote(src, slot(block), send_sems, recv_sems, k, s, to)

        own = pltpu.make_async_copy(ins[k], slot((x, y, c)), local_sems.at[k])
        to_sibling = copy(0, ins[k], (x, y, c), sibling)
        starts += [own.start, to_sibling.start]
        waits += [own.wait, to_sibling.wait_send, copy(0, ins[k], (x, y, 1 - c), sibling).wait_recv]
        for j, chip in enumerate(chips):
            out = copy(1 + j, ins[k], (x, y, c), (*chip, c))
            relay = copy(4 + j, slot((*chip, c)), (*chip, c), sibling)
            starts.append(out.start)
            relays += [copy(1 + j, ins[k], (*chip, c), sibling).wait_recv, relay.start]
            waits += [out.wait_send, relay.wait_send, copy(4 + j, ins[k], (*chip, 1 - c), sibling).wait_recv]
    return starts, relays, waits


def _scatter_copies(ins, outs, send_sems, recv_sems):
    x, y, c = _mesh_pos()
    starts, waits = [], []
    for k in range(len(ins)):
        for s, peer, idx in _peers(x, y, c):
            send = _remote(ins[k].at[idx], outs[k].at[s], send_sems, recv_sems, k, s, peer)
            starts.append(send.start)
            waits += [send.wait_recv, send.wait_send]
    return starts, waits


def _exchange_sems(n):
    return [pltpu.SemaphoreType.DMA((n, N_DEV - 1)), pltpu.SemaphoreType.DMA((n, N_DEV - 1))]


def _received_shapes(slabs):
    return [jax.ShapeDtypeStruct((N_DEV - 1,) + a.shape[1:], a.dtype) for a in slabs]


def _chips_and_own(x, y):
    return _other_chips(x, y) + [(x, y)]


def _sibling_exchange(slabs):
    n = len(slabs)

    def body(*refs):
        ins, outs, (send_sems, recv_sems) = refs[:n], refs[n:2 * n], refs[2 * n:]
        x, y, c = _mesh_pos()
        copies = [_remote(ins[k].at[_dev(*chip, 1 - c)], outs[k].at[r], send_sems, recv_sems, k, r, (x, y, 1 - c))
                  for k in range(n) for r, chip in enumerate(_chips_and_own(x, y))]
        for cp in copies:
            cp.start()
        for cp in copies:
            cp.wait_recv()
        for cp in copies:
            cp.wait_send()

    return pl.pallas_call(body, name="comm_rs_sibling", in_specs=[ANY] * n, out_specs=[ANY] * n,
                          out_shape=[jax.ShapeDtypeStruct((4,) + a.shape[1:], a.dtype) for a in slabs],
                          scratch_shapes=[pltpu.SemaphoreType.DMA((n, 4)), pltpu.SemaphoreType.DMA((n, 4))])(*slabs)


def _chip_copies(ins, outs, send_sems, recv_sems):
    x, y, c = _mesh_pos()
    starts, waits = [], []
    for k in range(len(ins)):
        for r, chip in enumerate(_other_chips(x, y)):
            cp = _remote(ins[k].at[r], outs[k].at[r], send_sems, recv_sems, k, r, (*chip, c))
            starts.append(cp.start)
            waits += [cp.wait_recv, cp.wait_send]
    return starts, waits


def _pair_sum(slab_idx, grad, from_sibling):
    _, rows, cols = grad.shape
    tr = _shard_tile(rows)

    def body(idx_ref, g_ref, s_ref, o_ref):
        o_ref[...] = (g_ref[...] + s_ref[...].astype(F32)).astype(o_ref.dtype)

    gs = pltpu.PrefetchScalarGridSpec(
        num_scalar_prefetch=1, grid=(3, rows // tr),
        in_specs=[pl.BlockSpec((None, tr, cols), lambda r, i, idx: (idx[r], i, 0)),
                  pl.BlockSpec((None, tr, cols), lambda r, i, idx: (r, i, 0))],
        out_specs=pl.BlockSpec((None, tr, cols), lambda r, i, idx: (r, i, 0)))
    return pl.pallas_call(body, name="rs_pair_sum", grid_spec=gs, out_shape=jax.ShapeDtypeStruct((3, rows, cols), BF16),
                          compiler_params=_params(2))(slab_idx, grad, from_sibling)


def _shard_tile(rows):
    for tr in range(min(rows, 352), 0, -1):
        if rows % tr == 0 and (tr % 16 == 0 or tr == rows):
            return tr


def _adamw_math(w, g, m, v):
    m = ADAM_B1 * m + (1.0 - ADAM_B1) * g
    v = ADAM_B2 * v + (1.0 - ADAM_B2) * (g * g)
    m_hat = m / (1.0 - ADAM_B1 ** ADAM_STEP)
    v_hat = v / (1.0 - ADAM_B2 ** ADAM_STEP)
    delta = -ADAM_LR * (m_hat / (jnp.sqrt(v_hat) + ADAM_EPS) + ADAM_WD * w)
    return delta, m, v


def _adamw_shard(me, grad, received, w, m, v):
    rows, cols = w.shape
    tr = _shard_tile(rows)
    n_rec = len(received)

    def body(me_ref, g_ref, *rest):
        r_refs, (w_ref, m_ref, v_ref, go_ref, do_ref, mo_ref, vo_ref) = rest[:n_rec], rest[n_rec:]
        g = g_ref[...]
        for r_ref in r_refs:
            for s in range(r_ref.shape[0]):
                g = g + r_ref[s].astype(F32)
        delta, m_new, v_new = _adamw_math(w_ref[...], g, m_ref[...], v_ref[...])
        go_ref[...] = g
        do_ref[...] = delta
        mo_ref[...] = m_new
        vo_ref[...] = v_new

    flat = pl.BlockSpec((tr, cols), lambda i, me: (i, 0))
    gs = pltpu.PrefetchScalarGridSpec(
        num_scalar_prefetch=1, grid=(rows // tr,),
        in_specs=[pl.BlockSpec((None, tr, cols), lambda i, me: (me[0], i, 0))]
        + [pl.BlockSpec((r.shape[0], tr, cols), lambda i, me: (0, i, 0)) for r in received] + [flat, flat, flat],
        out_specs=[flat, flat, flat, flat])
    out = jax.ShapeDtypeStruct((rows, cols), F32)
    return pl.pallas_call(body, name="adamw_shard", grid_spec=gs, out_shape=(out, out, out, out),
                          compiler_params=_params(1))(me, grad, *received, w, m, v)


def _small_reduce_adamw(slabs, w, m, v):
    _, rows, _ = slabs.shape

    def body(s_ref, w_ref, m_ref, v_ref, g_ref, d_ref, mo_ref, vo_ref, land, send_sems, recv_sems):
        x, y, c = _mesh_pos()
        me = _dev(x, y, c)
        copies = []
        for mask in range(1, N_DEV):
            px, py, pc = x ^ (mask >> 2), y ^ ((mask >> 1) & 1), c ^ (mask & 1)
            copies.append(pltpu.make_async_remote_copy(
                src_ref=s_ref.at[_dev(px, py, pc)], dst_ref=land.at[me], send_sem=send_sems.at[mask - 1],
                recv_sem=recv_sems.at[mask - 1], device_id=(px, py, pc), device_id_type=MESH))
        for cp in copies:
            cp.start()
        land[me] = s_ref[me]
        for mask in range(1, N_DEV):
            px, py, pc = x ^ (mask >> 2), y ^ ((mask >> 1) & 1), c ^ (mask & 1)
            pltpu.make_async_remote_copy(
                src_ref=s_ref.at[me], dst_ref=land.at[_dev(px, py, pc)], send_sem=send_sems.at[mask - 1],
                recv_sem=recv_sems.at[mask - 1], device_id=(px, py, pc), device_id_type=MESH).wait_recv()
        for cp in copies:
            cp.wait_send()
        g = land[0]
        for d in range(1, N_DEV):
            g = g + land[d]
        delta, m_new, v_new = _adamw_math(w_ref[...], g, m_ref[...], v_ref[...])
        g_ref[...] = g
        d_ref[...] = delta
        mo_ref[...] = m_new
        vo_ref[...] = v_new

    out = jax.ShapeDtypeStruct((rows, LANE), F32)
    return pl.pallas_call(
        body, name="comm_small_reduce_adamw", in_specs=[VMEM_WHOLE] * 4, out_specs=[VMEM_WHOLE] * 4, out_shape=(out, out, out, out),
        scratch_shapes=[pltpu.VMEM((N_DEV, rows, LANE), F32), pltpu.SemaphoreType.DMA((N_DEV - 1,)),
                        pltpu.SemaphoreType.DMA((N_DEV - 1,))],
    )(slabs, w, m, v)


def _cast_bf16(arrs):
    n = len(arrs)

    def body(*refs):
        for i_ref, o_ref in zip(refs[:n], refs[n:]):
            o_ref[...] = i_ref[...].astype(BF16)

    return pl.pallas_call(body, name="cast_bf16", in_specs=[VMEM_WHOLE] * n, out_specs=[VMEM_WHOLE] * n,
                          out_shape=[jax.ShapeDtypeStruct(a.shape, BF16) for a in arrs],
                          compiler_params=pltpu.CompilerParams(vmem_limit_bytes=VMEM_LIMIT))(*arrs)


REPLICATED = ("pre_mix_g", "gate_b", "dw_b", "conv_ln_g", "conv_ln_b", "post_mix_g", "pre_ffn_g", "post_ffn_g")
SHARDED = ("w_in", "w_conv_out", "w_attn_out", "w_o", "w_ffn_in", "w_ffn_out")
WEIGHTS = ("meta_tokens", "pre_mix_g", "w_in", "gate_b", "dw_w", "dw_b", "conv_ln_g", "conv_ln_b", "w_conv_out",
           "w_attn_out", "w_o", "post_mix_g", "pre_ffn_g", "w_ffn_in", "w_ffn_out", "post_ffn_g")


def kernel(x, meta_tokens, pre_mix_g, w_in, gate_b, dw_w, dw_b, conv_ln_g, conv_ln_b, w_conv_out, w_attn_out, w_o, post_mix_g, pre_ffn_g, w_ffn_in, w_ffn_out, post_ffn_g, loss_target, m_meta_tokens, m_pre_mix_g, m_w_in, m_gate_b, m_dw_w, m_dw_b, m_conv_ln_g, m_conv_ln_b, m_w_conv_out, m_w_attn_out, m_w_o, m_post_mix_g, m_pre_ffn_g, m_w_ffn_in, m_w_ffn_out, m_post_ffn_g, v_meta_tokens, v_pre_mix_g, v_w_in, v_gate_b, v_dw_w, v_dw_b, v_conv_ln_g, v_conv_ln_b, v_w_conv_out, v_w_attn_out, v_w_o, v_post_mix_g, v_pre_ffn_g, v_w_ffn_in, v_w_ffn_out, v_post_ffn_g):
    given = dict(locals())
    seq, d = x.shape[1], x.shape[2]
    n_meta = meta_tokens.shape[0]
    length = n_meta + seq
    m_rows = -(-length // QB) * QB
    dc = d // N_DEV
    assert dc == LANE and n_meta % 8 == 0 and seq % 8 == 0
    fs = w_ffn_in.shape[2]
    fr = w_ffn_out.shape[1]
    assert 2 * fr == fs

    local = {k: given[k][0] for k in SHARDED}
    cast = _cast_bf16([local[k] for k in SHARDED])
    dww_pad = jnp.pad(dw_w[0], ((0, CONV_PAD - CONV_WIDTH), (0, 0)))
    wi, meta_g, dww_g = _all_gather([cast[0], meta_tokens, dww_pad])
    meta_full = jnp.concatenate([meta_g[j] for j in range(N_DEV)], axis=1)
    dww_full = jnp.concatenate([dww_g[j] for j in range(N_DEV)], axis=1)
    ns = wi.shape[2]

    tail = jnp.zeros((m_rows - length, d), F32)
    h0 = jnp.concatenate([meta_full, x[0], tail], axis=0)
    target = jnp.concatenate([jnp.zeros((n_meta, d), F32), loss_target[0], tail], axis=0)

    (u,) = _rows("pre_mix_norm", lambda r0, xs, ps: ([_rms(xs[0], ps[0])], []), [h0], [pre_mix_g], [BF16], [])
    p, p16 = _matmul("in_proj", NN, u, wi, pl.BlockSpec((m_rows, d), lambda i: (0, 0)), pl.BlockSpec((None, d, ns), lambda i: (i, 0, 0)),
                     pl.BlockSpec((m_rows, ns), lambda i: (0, i)), jax.ShapeDtypeStruct((m_rows, N_DEV * ns), F32), (N_DEV,),
                     twin_bf16=True)
    o, tot, gathered = _attn_fwd(p16, d, list(cast[1:]))
    wco, wao, wo = (g.reshape(d, d) for g in gathered[0:3])
    wfi = gathered[3]
    wfo = gathered[4].reshape(N_DEV // 2, fs, d)
    y = _conv_fwd(p, dww_full, dw_b, d)
    (yc,) = _rows("conv_norm", lambda r0, xs, ps: ([_ln_silu(xs[0], ps[0], ps[1])], []), [y], [conv_ln_g, conv_ln_b], [BF16], [])
    y_conv = _dense_fwd("conv_out", yc, wco)
    y_attn = _dense_fwd("attn_out", o, wao)
    gate_cols = [(d, 5), (d, 6), None, None]
    (mixin,) = _rows("gate_mix", lambda r0, xs, ps: ([_gate_mix(*xs, ps[0])], []), [p, p, y_conv, y_attn], [gate_b], [BF16], [],
                     row_in_cols=gate_cols, row_out_widths=[d])
    mix = _dense_fwd("mix_out", mixin, wo)
    h1, u2 = _rows("post_mix", lambda r0, xs, ps: (list(_post_mix(xs[0], xs[1], ps[0], ps[1])), []), [h0, mix],
                   [post_mix_g, pre_ffn_g], [F32, BF16], [])
    ab = _matmul("ffn_in", NN, u2, wfi, pl.BlockSpec((m_rows, d), lambda i: (0, 0)), pl.BlockSpec((None, d, fs), lambda i: (i, 0, 0)),
                 pl.BlockSpec((None, m_rows, fs), lambda i: (i, 0, 0)), jax.ShapeDtypeStruct((N_DEV, m_rows, fs), F32), (N_DEV,))
    half = N_DEV // 2
    tm = _row_tile(m_rows)
    pair = lambda off: pl.BlockSpec((None, tm, fs), lambda j, i, off=off: (j + off, i, 0))
    (f_in,) = _rowwise("swiglu", lambda r0, xs, ps: ([_swiglu(xs[0], xs[1])], []), [ab, ab], [],
                       [jax.ShapeDtypeStruct((half, m_rows, fs), BF16)], [], grid=(half, m_rows // tm),
                       in_specs=[pair(0), pair(half)], out_specs=[pair(0)], tm=tm, row_axis=1)
    f = _matmul("ffn_out", NN, f_in, wfo, pl.BlockSpec((None, m_rows, fs), lambda j: (j, 0, 0)), pl.BlockSpec((None, fs, d), lambda j: (j, 0, 0)),
                pl.BlockSpec((m_rows, d), lambda j: (0, 0)), jax.ShapeDtypeStruct((m_rows, d), F32), (half,), acc_axis=0)

    def loss_head(r0, xs, ps):
        h1_, f_, t_ = xs
        r, vjp = jax.vjp(_rms, f_, ps[0])
        rows = r0 + lax.broadcasted_iota(I32, (h1_.shape[0], 1), 0)
        real = (rows >= n_meta) & (rows < length)
        err = jnp.where(real, h1_ + r - t_, 0.0)
        dh2 = err * (1.0 / d)
        d_f, dg = vjp(dh2)
        part = jnp.sum(0.5 * jnp.mean(err * err, axis=-1, keepdims=True), axis=0, keepdims=True)
        return [d_f, dh2], [dg, jnp.broadcast_to(part, (1, LANE))]

    d_f, dh2, g_post_ffn, loss_part = _rows("loss_head", loss_head, [h1, f, target], [post_ffn_g], [BF16, F32], [d, LANE])

    d_fin = _matmul("ffn_out_dx", NT, d_f, wfo, pl.BlockSpec((m_rows, d), lambda j: (0, 0)), pl.BlockSpec((None, fs, d), lambda j: (j, 0, 0)),
                    pl.BlockSpec((None, m_rows, fs), lambda j: (j, 0, 0)), jax.ShapeDtypeStruct((half, m_rows, fs), F32), (half,))
    g_wfo = _matmul("ffn_out_dw", TN, f_in, d_f, pl.BlockSpec((None, m_rows, fs), lambda j: (j, 0, 0)), pl.BlockSpec((m_rows, d), lambda j: (0, 0)),
                    pl.BlockSpec((None, fs, d), lambda j: (j, 0, 0)), jax.ShapeDtypeStruct((half, fs, d), F32), (half,), twin_bf16=True)

    def swiglu_bwd(r0, xs, ps):
        _, vjp = jax.vjp(_swiglu, xs[0], xs[1])
        return [vjp(xs[2])], []

    (d_ab,) = _rowwise("swiglu_bwd", swiglu_bwd, [ab, ab, d_fin], [],
                       [jax.ShapeDtypeStruct((2, half, m_rows, fs), BF16)], [], grid=(half, m_rows // tm),
                       in_specs=[pair(0), pair(half), pair(0)],
                       out_specs=[pl.BlockSpec((2, None, tm, fs), lambda j, i: (0, j, i, 0))], tm=tm, row_axis=1)
    d_ab = d_ab.reshape(N_DEV, m_rows, fs)
    du2 = _matmul("ffn_in_dx", NT, d_ab, wfi, pl.BlockSpec((None, m_rows, fs), lambda i: (i, 0, 0)), pl.BlockSpec((None, d, fs), lambda i: (i, 0, 0)),
                  pl.BlockSpec((m_rows, d), lambda i: (0, 0)), jax.ShapeDtypeStruct((m_rows, d), F32), (N_DEV,), acc_axis=0)
    g_wfi = _matmul("ffn_in_dw", TN, u2, d_ab, pl.BlockSpec((m_rows, d), lambda i: (0, 0)), pl.BlockSpec((None, m_rows, fs), lambda i: (i, 0, 0)),
                    pl.BlockSpec((None, d, fs), lambda i: (i, 0, 0)), jax.ShapeDtypeStruct((N_DEV, d, fs), F32), (N_DEV,), twin_bf16=True)

    def post_mix_bwd(r0, xs, ps):
        h0_, mix_, dh2_, du2_ = xs
        _, vjp = jax.vjp(_post_mix, h0_, mix_, ps[0], ps[1])
        dh0_, dmix_, dg1, dg2 = vjp((dh2_, du2_))
        return [dmix_, dh0_], [dg1, dg2]

    d_mix, dh1, g_post_mix, g_pre_ffn = _rows("post_mix_bwd", post_mix_bwd, [h0, mix, dh2, du2], [post_mix_g, pre_ffn_g],
                                              [BF16, F32], [d, d])
    d_mixin = _dense_dx("mix_out_dx", d_mix, wo, F32)
    g_wo = _dense_dw("mix_out_dw", mixin, d_mix)

    def gate_mix_bwd(r0, xs, ps):
        _, vjp = jax.vjp(_gate_mix, xs[0], xs[1], xs[2], xs[3], ps[0])
        dpgc, dpga, dyc_, dya_, dgb = vjp(xs[4])
        return [dpgc, dpga, dyc_, dya_], [dgb]

    dp_gc, dp_ga, d_yconv, d_yattn, g_gate_b = _rows(
        "gate_mix_bwd", gate_mix_bwd, [p, p, y_conv, y_attn, d_mixin], [gate_b], [BF16] * 4, [2 * d],
        row_in_cols=gate_cols + [None], row_out_widths=[d] * 4)
    d_o = _dense_dx("attn_out_dx", d_yattn, wao, BF16)
    g_wao = _dense_dw("attn_out_dw", o, d_yattn)
    d_yc = _dense_dx("conv_out_dx", d_yconv, wco, F32)
    g_wco = _dense_dw("conv_out_dw", yc, d_yconv)
    big = {"w_ffn_out": [g.reshape(N_DEV, fr, d) for g in g_wfo], "w_ffn_in": g_wfi,
           "w_o": [g.reshape(N_DEV, dc, d) for g in g_wo], "w_attn_out": [g.reshape(N_DEV, dc, d) for g in g_wao],
           "w_conv_out": [g.reshape(N_DEV, dc, d) for g in g_wco]}
    early = ("w_ffn_out", "w_ffn_in", "w_o", "w_attn_out", "w_conv_out")
    dq, dk, dv, received_early = _attn_bwd(p16, d_o, tot, d, [big[k][1] for k in early])

    def conv_norm_bwd(r0, xs, ps):
        _, vjp = jax.vjp(_ln_silu, xs[0], ps[0], ps[1])
        dy_, dg, db = vjp(xs[1])
        return [dy_], [dg, db]

    d_y, g_ln_g, g_ln_b = _rows("conv_norm_bwd", conv_norm_bwd, [y, d_yc], [conv_ln_g, conv_ln_b], [F32], [d, d])
    dp_a, dp_g, g_dww, g_dwb = _conv_bwd(p, d_y, dww_full, d)
    dp = jnp.concatenate([dp_a, dp_g, dq, dk, dv, dp_gc, dp_ga], axis=1)
    g_wi = _matmul("in_proj_dw", TN, u, dp, pl.BlockSpec((m_rows, d), lambda i: (0, 0)), pl.BlockSpec((m_rows, ns), lambda i: (0, i)),
                   pl.BlockSpec((None, d, ns), lambda i: (i, 0, 0)), jax.ShapeDtypeStruct((N_DEV, d, ns), F32), (N_DEV,), twin_bf16=True)
    x_i, y_i, c_i = _mesh_pos()
    slab_idx = jnp.stack([_dev(*chip, c_i) for chip in _other_chips(x_i, y_i)]).astype(I32)
    (wi_sibling,) = _sibling_exchange([g_wi[1]])
    wi_pairs = _pair_sum(slab_idx, g_wi[0], wi_sibling)
    du, wi_chips = _matmul(
        "in_proj_dx", NT, dp, wi, pl.BlockSpec((m_rows, ns), lambda i: (0, i)), pl.BlockSpec((None, d, ns), lambda i: (i, 0, 0)),
        pl.BlockSpec((m_rows, d), lambda i: (0, 0)), jax.ShapeDtypeStruct((m_rows, d), F32), (N_DEV,), acc_axis=0,
        carried=(_chip_copies, [wi_pairs], [jax.ShapeDtypeStruct(wi_pairs.shape, BF16)],
                 [pltpu.SemaphoreType.DMA((1, 3)), pltpu.SemaphoreType.DMA((1, 3))]))

    def pre_mix_bwd(r0, xs, ps):
        _, vjp = jax.vjp(_rms, xs[0], ps[0])
        dx, dg = vjp(xs[1])
        return [xs[2] + dx], [dg]

    dh0, g_pre_mix = _rows("pre_mix_bwd", pre_mix_bwd, [h0, du, dh1], [pre_mix_g], [F32], [d])
    grad_x = dh0[n_meta:length][None]

    me = _dev(x_i, y_i, c_i)
    me_arr = jnp.reshape(me, (1,)).astype(I32)
    big["w_in"] = g_wi
    received = {k: [r] for k, r in zip(early, received_early)}
    received["w_in"] = [wi_sibling[3:4], wi_chips]
    results = {}
    for k in SHARDED:
        outs = _adamw_shard(me_arr, big[k][0], received[k], local[k], given["m_" + k][0], given["v_" + k][0])
        results[k] = tuple(a[None] for a in outs)

    rep_grads = {"pre_mix_g": g_pre_mix, "gate_b": g_gate_b, "dw_b": g_dwb, "conv_ln_g": g_ln_g, "conv_ln_b": g_ln_b,
                 "post_mix_g": g_post_mix, "pre_ffn_g": g_pre_ffn, "post_ffn_g": g_post_ffn}

    def pack_rep(get):
        return jnp.concatenate([get(k) for k in REPLICATED], axis=1).reshape(-1, LANE)

    rep_rows = pack_rep(lambda k: rep_grads[k])
    n_rep = rep_rows.shape[0]
    loss_rows = jnp.broadcast_to(loss_part, (8, LANE))
    g_meta = dh0[0:n_meta]
    slabs = jnp.stack([jnp.concatenate([rep_rows, loss_rows, g_dww[:, j * LANE:(j + 1) * LANE], g_meta[:, j * LANE:(j + 1) * LANE]], axis=0)
                       for j in range(N_DEV)])

    def pack_small(prefix):
        dww_own = jnp.pad(given[prefix + "dw_w"][0], ((0, CONV_PAD - CONV_WIDTH), (0, 0)))
        return jnp.concatenate([pack_rep(lambda k: given[prefix + k]), jnp.zeros((8, LANE), F32), dww_own,
                                given[prefix + "meta_tokens"]], axis=0)

    small = _small_reduce_adamw(slabs, pack_small(""), pack_small("m_"), pack_small("v_"))
    loss = small[0][n_rep, 0]

    def unpack(arr):
        out = {}
        flat = arr[:n_rep].reshape(1, -1)
        off = 0
        for k in REPLICATED:
            w = given[k].shape[1]
            out[k] = flat[:, off:off + w]
            off += w
        out["dw_w"] = arr[n_rep + 8:n_rep + 8 + CONV_WIDTH][None]
        out["meta_tokens"] = arr[n_rep + 8 + CONV_PAD:n_rep + 8 + CONV_PAD + n_meta]
        return out

    small_out = [unpack(a) for a in small]
    for k in WEIGHTS:
        if k not in results:
            results[k] = tuple(s[k] for s in small_out)
    return (loss, grad_x, *[results[k][0] for k in WEIGHTS], *[results[k][1] for k in WEIGHTS],
            *[results[k][2] for k in WEIGHTS], *[results[k][3] for k in WEIGHTS])
```

```python
import jax
import jax.numpy as jnp
from jax import lax
from jax.experimental import pallas as pl
from jax.experimental.pallas import tpu as pltpu

F32 = jnp.float32
BF16 = jnp.bfloat16
I32 = jnp.int32

N_DEV = 8
LANE = 128
HEAD_DIM = 64
QB = 128
KEY_SHIFT = 2
KEY_TILES = 1 << KEY_SHIFT
KEY_CHUNK = KEY_TILES * QB
LANE_BLOCKS = 2
BWD_LANE_BLOCKS = 2
CONV_WIDTH = 31
CONV_PAD = 32
ROW_CHUNK = 128
RMS_EPS = 1e-6
LN_EPS = 1e-5
ADAM_LR = 0.001
ADAM_B1 = 0.9
ADAM_B2 = 0.999
ADAM_EPS = 1e-08
ADAM_WD = 0.01
ADAM_STEP = 10
VMEM_LIMIT = 56 * 1024 * 1024

NN = (((1,), (0,)), ((), ()))
NT = (((1,), (1,)), ((), ()))
TN = (((0,), (0,)), ((), ()))
MESH = pl.DeviceIdType.MESH
ANY = pl.BlockSpec(memory_space=pl.ANY)
VMEM_WHOLE = pl.BlockSpec(memory_space=pltpu.VMEM)


def _params(n_axes):
    return pltpu.CompilerParams(dimension_semantics=("arbitrary",) * n_axes, vmem_limit_bytes=VMEM_LIMIT)


def _row_tile(m):
    assert m % QB == 0
    return m // 4 if m % 64 == 0 else QB


def _matmul(name, dims, a, b, a_spec, b_spec, o_spec, out_shape, grid, acc_axis=None, twin_bf16=False, carried=None):
    n_car = 0 if carried is None else len(carried[1])

    def body(a_ref, b_ref, *rest):
        car_ins, o_ref, rest = rest[:n_car], rest[n_car], rest[n_car + 1:]
        if carried is not None:
            starts, waits = carried[0](car_ins, rest[:n_car], *rest[n_car:])

            @pl.when(pl.program_id(0) == 0)
            def _():
                for f in starts:
                    f()

        r = lax.dot_general(a_ref[...], b_ref[...], dims, preferred_element_type=F32)
        if acc_axis is None:
            o_ref[...] = r.astype(o_ref.dtype)
            if twin_bf16:
                rest[0][...] = r.astype(BF16)
        else:
            k = pl.program_id(acc_axis)

            @pl.when(k == 0)
            def _():
                o_ref[...] = r

            @pl.when(k > 0)
            def _():
                o_ref[...] += r

        if carried is not None:
            @pl.when(pl.program_id(0) == grid[0] - 1)
            def _():
                for f in waits:
                    f()

    in_specs, out_specs, out_shapes, scratch = [a_spec, b_spec], [o_spec], [out_shape], []
    if twin_bf16:
        assert acc_axis is None and carried is None
        out_specs.append(o_spec)
        out_shapes.append(jax.ShapeDtypeStruct(out_shape.shape, BF16))
    if carried is not None:
        assert len(grid) == 1
        in_specs += [ANY] * n_car
        out_specs += [ANY] * n_car
        out_shapes += list(carried[2])
        scratch = list(carried[3])
    outs = pl.pallas_call(body, name=name, grid=grid, in_specs=in_specs, out_specs=out_specs, out_shape=out_shapes,
                          scratch_shapes=scratch, compiler_params=_params(len(grid)))(a, b, *(carried[1] if carried else ()))
    return outs[0] if len(outs) == 1 else outs


DENSE_TILE = 256


def _dense_fwd(name, a, w, out_dtype=F32):
    m, k = a.shape
    n = w.shape[1]
    tn = DENSE_TILE
    return _matmul(name, NN, a, w, pl.BlockSpec((m, k), lambda j: (0, 0)), pl.BlockSpec((k, tn), lambda j: (0, j)),
                   pl.BlockSpec((m, tn), lambda j: (0, j)), jax.ShapeDtypeStruct((m, n), out_dtype), (n // tn,))


def _dense_dx(name, dy, w, out_dtype):
    m, n = dy.shape
    k = w.shape[0]
    tk = DENSE_TILE
    return _matmul(name, NT, dy, w, pl.BlockSpec((m, n), lambda j: (0, 0)), pl.BlockSpec((tk, n), lambda j: (j, 0)),
                   pl.BlockSpec((m, tk), lambda j: (0, j)), jax.ShapeDtypeStruct((m, k), out_dtype), (k // tk,))


def _dense_dw(name, a, dy):
    m, k = a.shape
    n = dy.shape[1]
    tn = DENSE_TILE
    return _matmul(name, TN, a, dy, pl.BlockSpec((m, k), lambda j: (0, 0)), pl.BlockSpec((m, tn), lambda j: (0, j)),
                   pl.BlockSpec((k, tn), lambda j: (0, j)), jax.ShapeDtypeStruct((k, n), F32), (n // tn,), twin_bf16=True)


def _rowwise(name, fn, row_ins, par_ins, row_outs, par_outs, *, grid, in_specs, out_specs, tm, row_axis):
    n_ri, n_pi, n_ro, n_po = len(row_ins), len(par_ins), len(row_outs), len(par_outs)
    n_steps, tail = divmod(tm, ROW_CHUNK)
    assert tail % 16 == 0

    def body(*refs):
        ri = refs[:n_ri]
        pi = refs[n_ri:n_ri + n_pi]
        ro = refs[n_ri + n_pi:n_ri + n_pi + n_ro]
        po = refs[n_ri + n_pi + n_ro:]
        ps = [r[...] for r in pi]
        base = pl.program_id(row_axis) * tm

        def chunk(r0, rows, carry):
            xs = [r[pl.ds(r0, rows), :] for r in ri]
            outs, pouts = fn(base + r0, xs, ps)
            for r, o in zip(ro, outs):
                if isinstance(o, (list, tuple)):
                    for j, part in enumerate(o):
                        r[j, pl.ds(r0, rows), :] = part.astype(r.dtype)
                else:
                    r[pl.ds(r0, rows), :] = o.astype(r.dtype)
            return tuple(c + q for c, q in zip(carry, pouts))

        def step(i, carry):
            return chunk(pl.multiple_of(i * ROW_CHUNK, ROW_CHUNK), ROW_CHUNK, carry)

        acc = lax.fori_loop(0, n_steps, step, tuple(jnp.zeros(s.shape, F32) for s in par_outs))
        if tail:
            acc = chunk(n_steps * ROW_CHUNK, tail, acc)
        if n_po:
            first = pl.program_id(0) == 0
            for ax in range(1, len(grid)):
                first = first & (pl.program_id(ax) == 0)

            @pl.when(first)
            def _():
                for r in po:
                    r[...] = jnp.zeros_like(r)

            for r, a in zip(po, acc):
                r[...] += a

    return pl.pallas_call(body, name=name, grid=grid, in_specs=in_specs, out_specs=out_specs,
                          out_shape=tuple(row_outs) + tuple(par_outs),
                          compiler_params=_params(len(grid)))(*row_ins, *par_ins)


def _rows(name, fn, row_ins, par_ins, row_out_dtypes, par_out_widths, row_in_cols=None, row_out_widths=None):
    m = row_ins[0].shape[0]
    tm = _row_tile(m)
    in_specs = []
    for k, a in enumerate(row_ins):
        if row_in_cols is not None and row_in_cols[k] is not None:
            width, cb = row_in_cols[k]
            in_specs.append(pl.BlockSpec((tm, width), lambda i, cb=cb: (i, cb)))
        else:
            in_specs.append(pl.BlockSpec((tm, a.shape[1]), lambda i: (i, 0)))
    for a in par_ins:
        in_specs.append(pl.BlockSpec(a.shape, lambda i: (0, 0)))
    if row_out_widths is None:
        row_out_widths = [row_ins[0].shape[1]] * len(row_out_dtypes)
    row_outs = [jax.ShapeDtypeStruct((m, w), dt) for w, dt in zip(row_out_widths, row_out_dtypes)]
    par_outs = [jax.ShapeDtypeStruct((1, w), F32) for w in par_out_widths]
    out_specs = [pl.BlockSpec((tm, s.shape[1]), lambda i: (i, 0)) for s in row_outs]
    out_specs += [pl.BlockSpec(s.shape, lambda i: (0, 0)) for s in par_outs]
    return _rowwise(name, fn, row_ins, par_ins, row_outs, par_outs, grid=(m // tm,), in_specs=in_specs,
                    out_specs=out_specs, tm=tm, row_axis=0)


def _rms(x, g):
    return x * lax.rsqrt(jnp.mean(x * x, axis=-1, keepdims=True) + RMS_EPS) * g


def _ln_silu(y, g, b):
    mu = jnp.mean(y, axis=-1, keepdims=True)
    yc = y - mu
    var = jnp.mean(yc * yc, axis=-1, keepdims=True)
    return jax.nn.silu(yc * lax.rsqrt(var + LN_EPS) * g + b)


def _gate_mix(pgc, pga, yc, ya, gb):
    d = pgc.shape[1]
    return jax.nn.sigmoid(pgc + gb[:, :d]) * yc + jax.nn.sigmoid(pga + gb[:, d:]) * ya


def _post_mix(h0, mix, g_post, g_pre):
    h1 = h0 + _rms(mix, g_post)
    return h1, _rms(h1, g_pre)


def _swiglu(a, b):
    return jax.nn.silu(a) * b


def _conv_taps():
    taps = []
    for b in range(8):
        for a in range(CONV_PAD // 8):
            s = 8 * a + b
            if s < CONV_WIDTH:
                taps.append((b, a, CONV_WIDTH - 1 - s))
    return taps


def _conv_fwd(p, dww, dwb, d_model):
    m = p.shape[0]
    nch = d_model // LANE
    n_chunk = m // QB
    taps = _conv_taps()

    def body(a_ref, g_ref, w_ref, b_ref, y_ref, upad):
        upad[0:CONV_PAD, :] = jnp.zeros((CONV_PAD, LANE), F32)

        def fill(i, c):
            r0 = pl.multiple_of(i * QB, QB)
            u = a_ref[pl.ds(r0, QB), :] * jax.nn.sigmoid(g_ref[pl.ds(r0, QB), :])
            upad[pl.ds(pl.multiple_of(r0 + CONV_PAD, 8), QB), :] = u
            return c

        lax.fori_loop(0, n_chunk, fill, 0)

        def conv(i, c):
            r0 = pl.multiple_of(i * QB, QB)
            win = upad[pl.ds(r0, QB + CONV_PAD), :]
            acc = jnp.broadcast_to(b_ref[...], (QB, LANE))
            rolled = {}
            for b, a, j in taps:
                if b not in rolled:
                    rolled[b] = win if b == 0 else pltpu.roll(win, b, axis=0)
                lo = CONV_PAD - 8 * a
                acc = acc + w_ref[j:j + 1, :] * rolled[b][lo:lo + QB, :]
            y_ref[pl.ds(r0, QB), :] = acc
            return c

        lax.fori_loop(0, n_chunk, conv, 0)

    col = lambda off: pl.BlockSpec((m, LANE), lambda c: (0, off + c))
    return pl.pallas_call(
        body, name="conv_fwd", grid=(nch,),
        in_specs=[col(0), col(nch), pl.BlockSpec((CONV_PAD, LANE), lambda c: (0, c)), pl.BlockSpec((1, LANE), lambda c: (0, c))],
        out_specs=col(0), out_shape=jax.ShapeDtypeStruct((m, d_model), F32),
        scratch_shapes=[pltpu.VMEM((m + CONV_PAD, LANE), F32)], compiler_params=_params(1))(p, p, dww, dwb)


def _conv_bwd(p, dy, dww, d_model):
    m = p.shape[0]
    nch = d_model // LANE
    n_chunk = m // QB
    taps = _conv_taps()
    win_rows = QB + CONV_PAD

    def body(a_ref, g_ref, dy_ref, w_ref, da_ref, dg_ref, dw_ref, db_ref, upad, dypad, wacc, bacc):
        upad[0:CONV_PAD, :] = jnp.zeros((CONV_PAD, LANE), F32)
        dypad[m:m + CONV_PAD, :] = jnp.zeros((CONV_PAD, LANE), F32)
        wacc[...] = jnp.zeros_like(wacc)
        bacc[...] = jnp.zeros_like(bacc)

        def fill(i, c):
            r0 = pl.multiple_of(i * QB, QB)
            u = a_ref[pl.ds(r0, QB), :] * jax.nn.sigmoid(g_ref[pl.ds(r0, QB), :])
            upad[pl.ds(pl.multiple_of(r0 + CONV_PAD, 8), QB), :] = u
            dypad[pl.ds(r0, QB), :] = dy_ref[pl.ds(r0, QB), :]
            return c

        lax.fori_loop(0, n_chunk, fill, 0)

        def chunk(i, c):
            r0 = pl.multiple_of(i * QB, QB)
            dwin = dypad[pl.ds(r0, win_rows), :]
            du = jnp.zeros((QB, LANE), F32)
            rolled = {}
            for b, a, j in taps:
                if b not in rolled:
                    rolled[b] = dwin if b == 0 else pltpu.roll(dwin, win_rows - b, axis=0)
                du = du + w_ref[j:j + 1, :] * rolled[b][8 * a:8 * a + QB, :]
            av = a_ref[pl.ds(r0, QB), :]
            sg = jax.nn.sigmoid(g_ref[pl.ds(r0, QB), :])
            da_ref[pl.ds(r0, QB), :] = (du * sg).astype(da_ref.dtype)
            dg_ref[pl.ds(r0, QB), :] = (du * av * sg * (1.0 - sg)).astype(dg_ref.dtype)
            dyc = dy_ref[pl.ds(r0, QB), :]
            uwin = upad[pl.ds(r0, win_rows), :]
            rolled = {}
            for b, a, j in taps:
                if b not in rolled:
                    rolled[b] = uwin if b == 0 else pltpu.roll(uwin, b, axis=0)
                lo = CONV_PAD - 8 * a
                prod = dyc * rolled[b][lo:lo + QB, :]
                wacc[j] += prod.reshape(QB // 8, 8, LANE).sum(axis=0)
            bacc[...] += dyc.reshape(QB // 8, 8, LANE).sum(axis=0)
            return c

        lax.fori_loop(0, n_chunk, chunk, 0)
        for j in range(CONV_WIDTH):
            dw_ref[j:j + 1, :] = jnp.sum(wacc[j], axis=0, keepdims=True)
        dw_ref[CONV_WIDTH:CONV_PAD, :] = jnp.zeros((CONV_PAD - CONV_WIDTH, LANE), F32)
        db_ref[...] = jnp.sum(bacc[...], axis=0, keepdims=True)

    col = lambda off: pl.BlockSpec((m, LANE), lambda c: (0, off + c))
    return pl.pallas_call(
        body, name="conv_bwd", grid=(nch,),
        in_specs=[col(0), col(nch), col(0), pl.BlockSpec((CONV_PAD, LANE), lambda c: (0, c))],
        out_specs=[col(0), col(0), pl.BlockSpec((CONV_PAD, LANE), lambda c: (0, c)), pl.BlockSpec((1, LANE), lambda c: (0, c))],
        out_shape=(jax.ShapeDtypeStruct((m, d_model), BF16), jax.ShapeDtypeStruct((m, d_model), BF16),
                   jax.ShapeDtypeStruct((CONV_PAD, d_model), F32), jax.ShapeDtypeStruct((1, d_model), F32)),
        scratch_shapes=[pltpu.VMEM((m + CONV_PAD, LANE), F32), pltpu.VMEM((m + CONV_PAD, LANE), F32),
                        pltpu.VMEM((CONV_PAD, 8, LANE), F32), pltpu.VMEM((8, LANE), F32)],
        compiler_params=_params(1))(p, p, dy, dww)


EXP_CLAMP = 80.0


def _one_plus_exp(z):
    return 1.0 + jnp.exp(jnp.minimum(z, EXP_CLAMP))


def _softplus(z):
    return jnp.maximum(jnp.log(_one_plus_exp(z)), z)


def _softplus_sigmoid(z):
    s = _one_plus_exp(z)
    return jnp.maximum(jnp.log(s), z), 1.0 - 1.0 / s


def _split_dot(x, tri):
    hi = pltpu.bitcast(pltpu.bitcast(x, jnp.uint32) & jnp.uint32(0xFFFF0000), F32)
    lo = x - hi
    return jnp.dot(jnp.concatenate([hi.astype(BF16), lo.astype(BF16)], axis=1), tri, preferred_element_type=F32)


def _tri(kind):
    jj = lax.broadcasted_iota(I32, (2 * QB, 2 * QB), 0) & (QB - 1)
    ss = lax.broadcasted_iota(I32, (2 * QB, 2 * QB), 1)
    keep = {"ge": jj >= ss, "lt": jj < ss, "le": jj <= ss}[kind]
    return jnp.where((ss >= QB) | keep, 1.0, 0.0).astype(BF16)


def _attn_fwd(p, d_model, shards):
    m = p.shape[0]
    nqb = m // QB
    ngrp = d_model // (LANE_BLOCKS * LANE)
    qo, ko, vo = 2 * ngrp, 3 * ngrp, 4 * ngrp
    scale = HEAD_DIM ** -0.5
    n_sh = len(shards)

    assert nqb >= KEY_TILES

    def body(q_ref, k_ref, v_ref, *rest):
        shard_refs, (o_ref, t_ref), rest = rest[:n_sh], rest[n_sh:n_sh + 2], rest[n_sh + 2:]
        gathered_refs, (acc_ref, car_ref), sems = rest[:n_sh], rest[n_sh:n_sh + 2], rest[n_sh + 2:]
        starts, relays, waits = _gather_copies(shard_refs, gathered_refs, *sems)

        @pl.when((pl.program_id(0) == 0) & (pl.program_id(1) == 0))
        def _():
            for f in starts:
                f()

        @pl.when((pl.program_id(0) == ngrp - 1) & (pl.program_id(1) == 0))
        def _():
            for f in relays:
                f()

        qb = pl.program_id(1)
        lane = lax.broadcasted_iota(I32, (QB, LANE), 1)
        head0 = lane < HEAD_DIM
        row_g = qb * QB + lax.broadcasted_iota(I32, (QB, KEY_CHUNK), 0)
        col_l = lax.broadcasted_iota(I32, (QB, KEY_CHUNK), 1)
        tri = _tri("ge")
        heads = range(2 * LANE_BLOCKS)
        qh = []
        for lb in range(LANE_BLOCKS):
            q2 = (q_ref[:, lb * LANE:(lb + 1) * LANE] * scale).astype(BF16)
            zero = jnp.zeros_like(q2)
            qh += [jnp.where(head0, q2, zero), jnp.where(head0, zero, q2)]
        acc_ref[...] = jnp.zeros_like(acc_ref)
        car_ref[...] = jnp.zeros_like(car_ref)

        def chunk(first_tile, bound):
            r0 = pl.multiple_of(first_tile * QB, QB)
            kcs = [k_ref[pl.ds(r0, KEY_CHUNK), lb * LANE:(lb + 1) * LANE].astype(BF16) for lb in range(LANE_BLOCKS)]
            vcs = [v_ref[pl.ds(r0, KEY_CHUNK), lb * LANE:(lb + 1) * LANE].astype(BF16) for lb in range(LANE_BLOCKS)]
            valid = None if bound is None else (col_l + r0) < bound
            zs = [lax.dot_general(qh[h], kcs[h // 2], NT, preferred_element_type=F32) for h in heads]
            sps = [_softplus(z) for z in zs]
            if valid is not None:
                sps = [jnp.where(valid, sp, 0.0) for sp in sps]
            crs = [[_split_dot(sp[:, i * QB:(i + 1) * QB], tri) for i in range(KEY_TILES)] for sp in sps]
            cars = [car_ref[h] for h in heads]
            a_tiles = [[None] * KEY_TILES for h in heads]
            for i in reversed(range(KEY_TILES)):
                for h in heads:
                    a_tiles[h][i] = jnp.exp(zs[h][:, i * QB:(i + 1) * QB] - (crs[h][i][:, :QB] + cars[h]))
                    cars[h] = cars[h] + crs[h][i][:, QB:]
            for h in heads:
                a = jnp.concatenate(a_tiles[h], axis=1)
                if valid is not None:
                    a = jnp.where(valid, a, 0.0)
                acc_ref[h] += jnp.dot(a.astype(BF16), vcs[h // 2], preferred_element_type=F32)
                car_ref[h] = cars[h]

        near = jnp.maximum(qb - (KEY_TILES - 1), 0)
        chunk(near, row_g)
        n_full = lax.shift_right_logical(near, KEY_SHIFT)

        def step(i, c):
            chunk(near - KEY_TILES * (i + 1), None)
            return c

        lax.fori_loop(0, n_full, step, 0)
        left = near - KEY_TILES * n_full

        @pl.when(left > 0)
        def _():
            chunk(0, left * QB)

        for lb in range(LANE_BLOCKS):
            o_ref[:, lb * LANE:(lb + 1) * LANE] = jnp.where(head0, acc_ref[2 * lb], acc_ref[2 * lb + 1]).astype(o_ref.dtype)
        for h in heads:
            t_ref[:, h * QB:(h + 1) * QB] = car_ref[h]

        @pl.when((pl.program_id(0) == ngrp - 1) & (pl.program_id(1) == nqb - 1))
        def _():
            for f in waits:
                f()

    wide = LANE_BLOCKS * LANE
    outs = pl.pallas_call(
        body, name="attn_fwd", grid=(ngrp, nqb),
        in_specs=[pl.BlockSpec((QB, wide), lambda g, qb: (qb, qo + g)),
                  pl.BlockSpec((m, wide), lambda g, qb: (0, ko + g)),
                  pl.BlockSpec((m, wide), lambda g, qb: (0, vo + g))] + [ANY] * n_sh,
        out_specs=[pl.BlockSpec((QB, wide), lambda g, qb: (qb, g)),
                   pl.BlockSpec((QB, 2 * wide), lambda g, qb: (qb, g))] + [ANY] * n_sh,
        out_shape=[jax.ShapeDtypeStruct((m, d_model), BF16), jax.ShapeDtypeStruct((m, 2 * d_model), F32)]
        + [jax.ShapeDtypeStruct((N_DEV,) + s.shape, s.dtype) for s in shards],
        scratch_shapes=[pltpu.VMEM((2 * LANE_BLOCKS, QB, LANE), F32), pltpu.VMEM((2 * LANE_BLOCKS, QB, LANE), F32)]
        + _exchange_sems(n_sh) + [pltpu.SemaphoreType.DMA((n_sh,))],
        compiler_params=_params(2))(p, p, p, *shards)
    return outs[0], outs[1], outs[2:]


def _attn_bwd(p, d_o, tot, d_model, slabs):
    m = p.shape[0]
    nqb = m // QB
    blocks = BWD_LANE_BLOCKS
    ngrp = d_model // (blocks * LANE)
    qo, ko, vo = 2 * ngrp, 3 * ngrp, 4 * ngrp
    scale = HEAD_DIM ** -0.5
    n_sl = len(slabs)

    assert nqb >= KEY_TILES

    def body(q_ref, k_ref, v_ref, do_ref, t_ref, *rest):
        slab_refs, (dq_ref, dk_ref, dv_ref), rest = rest[:n_sl], rest[n_sl:n_sl + 3], rest[n_sl + 3:]
        recv_refs, (dkacc, dvacc, dqacc, csp, cg), sems = rest[:n_sl], rest[n_sl:n_sl + 5], rest[n_sl + 5:]
        starts, waits = _scatter_copies(slab_refs, recv_refs, *sems)

        @pl.when((pl.program_id(0) == 0) & (pl.program_id(1) == 0))
        def _():
            for f in starts:
                f()

        qb = pl.program_id(1)
        lane = lax.broadcasted_iota(I32, (QB, LANE), 1)
        head0 = lane < HEAD_DIM
        row_g = qb * QB + lax.broadcasted_iota(I32, (QB, KEY_CHUNK), 0)
        col_l = lax.broadcasted_iota(I32, (QB, KEY_CHUNK), 1)
        tri_lt = _tri("lt")
        tri_le = _tri("le")
        heads = range(2 * blocks)
        qh, doh = [], []
        for lb in range(blocks):
            q2 = (q_ref[:, lb * LANE:(lb + 1) * LANE] * scale).astype(BF16)
            do2 = do_ref[:, lb * LANE:(lb + 1) * LANE]
            zero = jnp.zeros_like(q2)
            qh += [jnp.where(head0, q2, zero), jnp.where(head0, zero, q2)]
            doh += [jnp.where(head0, do2, zero), jnp.where(head0, zero, do2)]
        q_pairs = [jnp.concatenate(qh[2 * lb:2 * lb + 2], axis=0) for lb in range(blocks)]
        do_pairs = [jnp.concatenate(doh[2 * lb:2 * lb + 2], axis=0) for lb in range(blocks)]

        @pl.when(qb == 0)
        def _():
            dkacc[...] = jnp.zeros_like(dkacc)
            dvacc[...] = jnp.zeros_like(dvacc)

        dqacc[...] = jnp.zeros_like(dqacc)
        csp[...] = jnp.zeros_like(csp)
        cg[...] = jnp.zeros_like(cg)

        def chunk(first_tile, bound):
            r0 = pl.multiple_of(first_tile * QB, QB)
            kcs = [k_ref[pl.ds(r0, KEY_CHUNK), lb * LANE:(lb + 1) * LANE].astype(BF16) for lb in range(blocks)]
            vcs = [v_ref[pl.ds(r0, KEY_CHUNK), lb * LANE:(lb + 1) * LANE].astype(BF16) for lb in range(blocks)]
            valid = None if bound is None else (col_l + r0) < bound
            tiles = [slice(i * QB, (i + 1) * QB) for i in range(KEY_TILES)]
            zs = [lax.dot_general(qh[h], kcs[h // 2], NT, preferred_element_type=F32) for h in heads]
            das = [lax.dot_general(doh[h], vcs[h // 2], NT, preferred_element_type=F32) for h in heads]
            sps, sgs = zip(*[_softplus_sigmoid(z) for z in zs])
            if valid is not None:
                sps = [jnp.where(valid, sp, 0.0) for sp in sps]
            crs = [[_split_dot(sp[:, c], tri_lt) for c in tiles] for sp in sps]
            a_tiles, g_tiles = [[] for h in heads], [[] for h in heads]
            for h in heads:
                tot_h = t_ref[:, h * QB:(h + 1) * QB]
                before = csp[h]
                for i, c in enumerate(tiles):
                    a = jnp.exp(zs[h][:, c] - (tot_h - (before + crs[h][i][:, :QB])))
                    if valid is not None:
                        a = jnp.where(valid[:, c], a, 0.0)
                    a_tiles[h].append(a)
                    g_tiles[h].append(a * das[h][:, c])
                    before = before + crs[h][i][:, QB:]
                csp[h] = before
            grs = [[_split_dot(g, tri_le) for g in g_tiles[h]] for h in heads]
            dzbs, abs_ = [], []
            for h in heads:
                g_before = cg[h]
                dz_tiles = []
                for i, c in enumerate(tiles):
                    dz = g_tiles[h][i] - sgs[h][:, c] * (g_before + grs[h][i][:, :QB])
                    if valid is not None:
                        dz = jnp.where(valid[:, c], dz, 0.0)
                    dz_tiles.append(dz)
                    g_before = g_before + grs[h][i][:, QB:]
                cg[h] = g_before
                dzbs.append(jnp.concatenate(dz_tiles, axis=1).astype(BF16))
                abs_.append(jnp.concatenate(a_tiles[h], axis=1).astype(BF16))
            for h in heads:
                dqacc[h] += jnp.dot(dzbs[h], kcs[h // 2], preferred_element_type=F32)
            for lb in range(blocks):
                dz_pair = jnp.concatenate(dzbs[2 * lb:2 * lb + 2], axis=0)
                a_pair = jnp.concatenate(abs_[2 * lb:2 * lb + 2], axis=0)
                dkacc[pl.ds(r0, KEY_CHUNK), lb * LANE:(lb + 1) * LANE] += lax.dot_general(
                    dz_pair, q_pairs[lb], TN, preferred_element_type=F32)
                dvacc[pl.ds(r0, KEY_CHUNK), lb * LANE:(lb + 1) * LANE] += lax.dot_general(
                    a_pair, do_pairs[lb], TN, preferred_element_type=F32)

        near = jnp.maximum(qb - (KEY_TILES - 1), 0)
        n_full = lax.shift_right_logical(near, KEY_SHIFT)

        def step(i, c):
            chunk(KEY_TILES * i, None)
            return c

        lax.fori_loop(0, n_full, step, 0)

        @pl.when(near > KEY_TILES * n_full)
        def _():
            chunk(KEY_TILES * n_full, near * QB)

        chunk(near, row_g)
        for lb in range(blocks):
            dq2 = jnp.where(head0, dqacc[2 * lb], dqacc[2 * lb + 1]) * scale
            dq_ref[:, lb * LANE:(lb + 1) * LANE] = dq2.astype(dq_ref.dtype)

        @pl.when(qb == nqb - 1)
        def _():
            dk_ref[...] = dkacc[...].astype(dk_ref.dtype)
            dv_ref[...] = dvacc[...].astype(dv_ref.dtype)

        @pl.when((pl.program_id(0) == ngrp - 1) & (qb == nqb - 1))
        def _():
            for f in waits:
                f()

    out = jax.ShapeDtypeStruct((m, d_model), BF16)
    wide = blocks * LANE
    carry = pltpu.VMEM((2 * blocks, QB, LANE), F32)
    outs = pl.pallas_call(
        body, name="attn_bwd", grid=(ngrp, nqb),
        in_specs=[pl.BlockSpec((QB, wide), lambda g, qb: (qb, qo + g)),
                  pl.BlockSpec((m, wide), lambda g, qb: (0, ko + g)),
                  pl.BlockSpec((m, wide), lambda g, qb: (0, vo + g)),
                  pl.BlockSpec((QB, wide), lambda g, qb: (qb, g)),
                  pl.BlockSpec((QB, 2 * wide), lambda g, qb: (qb, g))] + [ANY] * n_sl,
        out_specs=[pl.BlockSpec((QB, wide), lambda g, qb: (qb, g)),
                   pl.BlockSpec((m, wide), lambda g, qb: (0, g)),
                   pl.BlockSpec((m, wide), lambda g, qb: (0, g))] + [ANY] * n_sl,
        out_shape=[out, out, out] + _received_shapes(slabs),
        scratch_shapes=[pltpu.VMEM((m, wide), F32), pltpu.VMEM((m, wide), F32), carry, carry, carry] + _exchange_sems(n_sl),
        compiler_params=_params(2))(p, p, p, d_o, tot, *slabs)
    return outs[0], outs[1], outs[2], outs[3:]


def _mesh_pos():
    return lax.axis_index("x"), lax.axis_index("y"), lax.axis_index("c")


def _other_chips(x, y):
    return [(1 - x, y), (x, 1 - y), (1 - x, 1 - y)]


def _dev(x, y, c):
    return 4 * x + 2 * y + c


def _all_gather(shards):
    n = len(shards)

    def body(*refs):
        starts, relays, waits = _gather_copies(refs[:n], refs[n:2 * n], *refs[2 * n:])
        for f in starts + relays + waits:
            f()

    return pl.pallas_call(
        body, name="comm_all_gather", in_specs=[ANY] * n, out_specs=[ANY] * n,
        out_shape=[jax.ShapeDtypeStruct((N_DEV,) + s.shape, s.dtype) for s in shards],
        scratch_shapes=[pltpu.SemaphoreType.DMA((n, 7)), pltpu.SemaphoreType.DMA((n, 7)), pltpu.SemaphoreType.DMA((n,))],
    )(*shards)


def _peers(x, y, c):
    out = []
    for mask in range(1, N_DEV):
        px, py, pc = x ^ (mask >> 2), y ^ ((mask >> 1) & 1), c ^ (mask & 1)
        out.append((mask - 1, (px, py, pc), _dev(px, py, pc)))
    return out


def _remote(src, dst, send_sems, recv_sems, k, s, peer):
    return pltpu.make_async_remote_copy(src_ref=src, dst_ref=dst, send_sem=send_sems.at[k, s], recv_sem=recv_sems.at[k, s],
                                        device_id=peer, device_id_type=MESH)


def _gather_copies(ins, outs, send_sems, recv_sems, local_sems):
    x, y, c = _mesh_pos()
    sibling = (x, y, 1 - c)
    chips = _other_chips(x, y)
    starts, relays, waits = [], [], []
    for k in range(len(ins)):
        def slot(block, k=k):
            return outs[k].at[_dev(*block)]

        def copy(s, src, block, to, k=k):
            return _remote(src, slot(block), send_sems, recv_sems, k, s, to)

        own = pltpu.make_async_copy(ins[k], slot((x, y, c)), local_sems.at[k])
        to_sibling = copy(0, ins[k], (x, y, c), sibling)
        starts += [own.start, to_sibling.start]
        waits += [own.wait, to_sibling.wait_send, copy(0, ins[k], (x, y, 1 - c), sibling).wait_recv]
        for j, chip in enumerate(chips):
            out = copy(1 + j, ins[k], (x, y, c), (*chip, c))
            relay = copy(4 + j, slot((*chip, c)), (*chip, c), sibling)
            starts.append(out.start)
            relays += [copy(1 + j, ins[k], (*chip, c), sibling).wait_recv, relay.start]
            waits += [out.wait_send, relay.wait_send, copy(4 + j, ins[k], (*chip, 1 - c), sibling).wait_recv]
    return starts, relays, waits


def _scatter_copies(ins, outs, send_sems, recv_sems):
    x, y, c = _mesh_pos()
    starts, waits = [], []
    for k in range(len(ins)):
        for s, peer, idx in _peers(x, y, c):
            send = _remote(ins[k].at[idx], outs[k].at[s], send_sems, recv_sems, k, s, peer)
            starts.append(send.start)
            waits += [send.wait_recv, send.wait_send]
    return starts, waits


def _exchange_sems(n):
    return [pltpu.SemaphoreType.DMA((n, N_DEV - 1)), pltpu.SemaphoreType.DMA((n, N_DEV - 1))]


def _received_shapes(slabs):
    return [jax.ShapeDtypeStruct((N_DEV - 1,) + a.shape[1:], a.dtype) for a in slabs]


def _chips_and_own(x, y):
    return _other_chips(x, y) + [(x, y)]


def _sibling_exchange(slabs):
    n = len(slabs)

    def body(*refs):
        ins, outs, (send_sems, recv_sems) = refs[:n], refs[n:2 * n], refs[2 * n:]
        x, y, c = _mesh_pos()
        copies = [_remote(ins[k].at[_dev(*chip, 1 - c)], outs[k].at[r], send_sems, recv_sems, k, r, (x, y, 1 - c))
                  for k in range(n) for r, chip in enumerate(_chips_and_own(x, y))]
        for cp in copies:
            cp.start()
        for cp in copies:
            cp.wait_recv()
        for cp in copies:
            cp.wait_send()

    return pl.pallas_call(body, name="comm_rs_sibling", in_specs=[ANY] * n, out_specs=[ANY] * n,
                          out_shape=[jax.ShapeDtypeStruct((4,) + a.shape[1:], a.dtype) for a in slabs],
                          scratch_shapes=[pltpu.SemaphoreType.DMA((n, 4)), pltpu.SemaphoreType.DMA((n, 4))])(*slabs)


def _chip_copies(ins, outs, send_sems, recv_sems):
    x, y, c = _mesh_pos()
    starts, waits = [], []
    for k in range(len(ins)):
        for r, chip in enumerate(_other_chips(x, y)):
            cp = _remote(ins[k].at[r], outs[k].at[r], send_sems, recv_sems, k, r, (*chip, c))
            starts.append(cp.start)
            waits += [cp.wait_recv, cp.wait_send]
    return starts, waits


def _pair_sum(slab_idx, grad, from_sibling):
    _, rows, cols = grad.shape
    tr = _shard_tile(rows)

    def body(idx_ref, g_ref, s_ref, o_ref):
        o_ref[...] = (g_ref[...] + s_ref[...].astype(F32)).astype(o_ref.dtype)

    gs = pltpu.PrefetchScalarGridSpec(
        num_scalar_prefetch=1, grid=(3, rows // tr),
        in_specs=[pl.BlockSpec((None, tr, cols), lambda r, i, idx: (idx[r], i, 0)),
                  pl.BlockSpec((None, tr, cols), lambda r, i, idx: (r, i, 0))],
        out_specs=pl.BlockSpec((None, tr, cols), lambda r, i, idx: (r, i, 0)))
    return pl.pallas_call(body, name="rs_pair_sum", grid_spec=gs, out_shape=jax.ShapeDtypeStruct((3, rows, cols), BF16),
                          compiler_params=_params(2))(slab_idx, grad, from_sibling)


def _shard_tile(rows):
    for tr in range(min(rows, 352), 0, -1):
        if rows % tr == 0 and (tr % 16 == 0 or tr == rows):
            return tr


def _adamw_math(w, g, m, v):
    m = ADAM_B1 * m + (1.0 - ADAM_B1) * g
    v = ADAM_B2 * v + (1.0 - ADAM_B2) * (g * g)
    m_hat = m / (1.0 - ADAM_B1 ** ADAM_STEP)
    v_hat = v / (1.0 - ADAM_B2 ** ADAM_STEP)
    delta = -ADAM_LR * (m_hat / (jnp.sqrt(v_hat) + ADAM_EPS) + ADAM_WD * w)
    return delta, m, v


def _adamw_shard(me, grad, received, w, m, v):
    rows, cols = w.shape
    tr = _shard_tile(rows)
    n_rec = len(received)

    def body(me_ref, g_ref, *rest):
        r_refs, (w_ref, m_ref, v_ref, go_ref, do_ref, mo_ref, vo_ref) = rest[:n_rec], rest[n_rec:]
        g = g_ref[...]
        for r_ref in r_refs:
            for s in range(r_ref.shape[0]):
                g = g + r_ref[s].astype(F32)
        delta, m_new, v_new = _adamw_math(w_ref[...], g, m_ref[...], v_ref[...])
        go_ref[...] = g
        do_ref[...] = delta
        mo_ref[...] = m_new
        vo_ref[...] = v_new

    flat = pl.BlockSpec((tr, cols), lambda i, me: (i, 0))
    gs = pltpu.PrefetchScalarGridSpec(
        num_scalar_prefetch=1, grid=(rows // tr,),
        in_specs=[pl.BlockSpec((None, tr, cols), lambda i, me: (me[0], i, 0))]
        + [pl.BlockSpec((r.shape[0], tr, cols), lambda i, me: (0, i, 0)) for r in received] + [flat, flat, flat],
        out_specs=[flat, flat, flat, flat])
    out = jax.ShapeDtypeStruct((rows, cols), F32)
    return pl.pallas_call(body, name="adamw_shard", grid_spec=gs, out_shape=(out, out, out, out),
                          compiler_params=_params(1))(me, grad, *received, w, m, v)


def _small_reduce_adamw(slabs, w, m, v):
    _, rows, _ = slabs.shape

    def body(s_ref, w_ref, m_ref, v_ref, g_ref, d_ref, mo_ref, vo_ref, land, send_sems, recv_sems):
        x, y, c = _mesh_pos()
        me = _dev(x, y, c)
        copies = []
        for mask in range(1, N_DEV):
            px, py, pc = x ^ (mask >> 2), y ^ ((mask >> 1) & 1), c ^ (mask & 1)
            copies.append(pltpu.make_async_remote_copy(
                src_ref=s_ref.at[_dev(px, py, pc)], dst_ref=land.at[me], send_sem=send_sems.at[mask - 1],
                recv_sem=recv_sems.at[mask - 1], device_id=(px, py, pc), device_id_type=MESH))
        for cp in copies:
            cp.start()
        land[me] = s_ref[me]
        for mask in range(1, N_DEV):
            px, py, pc = x ^ (mask >> 2), y ^ ((mask >> 1) & 1), c ^ (mask & 1)
            pltpu.make_async_remote_copy(
                src_ref=s_ref.at[me], dst_ref=land.at[_dev(px, py, pc)], send_sem=send_sems.at[mask - 1],
                recv_sem=recv_sems.at[mask - 1], device_id=(px, py, pc), device_id_type=MESH).wait_recv()
        for cp in copies:
            cp.wait_send()
        g = land[0]
        for d in range(1, N_DEV):
            g = g + land[d]
        delta, m_new, v_new = _adamw_math(w_ref[...], g, m_ref[...], v_ref[...])
        g_ref[...] = g
        d_ref[...] = delta
        mo_ref[...] = m_new
        vo_ref[...] = v_new

    out = jax.ShapeDtypeStruct((rows, LANE), F32)
    return pl.pallas_call(
        body, name="comm_small_reduce_adamw", in_specs=[VMEM_WHOLE] * 4, out_specs=[VMEM_WHOLE] * 4, out_shape=(out, out, out, out),
        scratch_shapes=[pltpu.VMEM((N_DEV, rows, LANE), F32), pltpu.SemaphoreType.DMA((N_DEV - 1,)),
                        pltpu.SemaphoreType.DMA((N_DEV - 1,))],
    )(slabs, w, m, v)


def _cast_bf16(arrs):
    n = len(arrs)

    def body(*refs):
        for i_ref, o_ref in zip(refs[:n], refs[n:]):
            o_ref[...] = i_ref[...].astype(BF16)

    return pl.pallas_call(body, name="cast_bf16", in_specs=[VMEM_WHOLE] * n, out_specs=[VMEM_WHOLE] * n,
                          out_shape=[jax.ShapeDtypeStruct(a.shape, BF16) for a in arrs],
                          compiler_params=pltpu.CompilerParams(vmem_limit_bytes=VMEM_LIMIT))(*arrs)


REPLICATED = ("pre_mix_g", "gate_b", "dw_b", "conv_ln_g", "conv_ln_b", "post_mix_g", "pre_ffn_g", "post_ffn_g")
SHARDED = ("w_in", "w_conv_out", "w_attn_out", "w_o", "w_ffn_in", "w_ffn_out")
WEIGHTS = ("meta_tokens", "pre_mix_g", "w_in", "gate_b", "dw_w", "dw_b", "conv_ln_g", "conv_ln_b", "w_conv_out",
           "w_attn_out", "w_o", "post_mix_g", "pre_ffn_g", "w_ffn_in", "w_ffn_out", "post_ffn_g")


def kernel(x, meta_tokens, pre_mix_g, w_in, gate_b, dw_w, dw_b, conv_ln_g, conv_ln_b, w_conv_out, w_attn_out, w_o, post_mix_g, pre_ffn_g, w_ffn_in, w_ffn_out, post_ffn_g, loss_target, m_meta_tokens, m_pre_mix_g, m_w_in, m_gate_b, m_dw_w, m_dw_b, m_conv_ln_g, m_conv_ln_b, m_w_conv_out, m_w_attn_out, m_w_o, m_post_mix_g, m_pre_ffn_g, m_w_ffn_in, m_w_ffn_out, m_post_ffn_g, v_meta_tokens, v_pre_mix_g, v_w_in, v_gate_b, v_dw_w, v_dw_b, v_conv_ln_g, v_conv_ln_b, v_w_conv_out, v_w_attn_out, v_w_o, v_post_mix_g, v_pre_ffn_g, v_w_ffn_in, v_w_ffn_out, v_post_ffn_g):
    given = dict(locals())
    seq, d = x.shape[1], x.shape[2]
    n_meta = meta_tokens.shape[0]
    length = n_meta + seq
    m_rows = -(-length // QB) * QB
    dc = d // N_DEV
    assert dc == LANE and n_meta % 8 == 0 and seq % 8 == 0
    fs = w_ffn_in.shape[2]
    fr = w_ffn_out.shape[1]
    assert 2 * fr == fs

    local = {k: given[k][0] for k in SHARDED}
    cast = _cast_bf16([local[k] for k in SHARDED])
    dww_pad = jnp.pad(dw_w[0], ((0, CONV_PAD - CONV_WIDTH), (0, 0)))
    wi, meta_g, dww_g = _all_gather([cast[0], meta_tokens, dww_pad])
    meta_full = jnp.concatenate([meta_g[j] for j in range(N_DEV)], axis=1)
    dww_full = jnp.concatenate([dww_g[j] for j in range(N_DEV)], axis=1)
    ns = wi.shape[2]

    tail = jnp.zeros((m_rows - length, d), F32)
    h0 = jnp.concatenate([meta_full, x[0], tail], axis=0)
    target = jnp.concatenate([jnp.zeros((n_meta, d), F32), loss_target[0], tail], axis=0)

    (u,) = _rows("pre_mix_norm", lambda r0, xs, ps: ([_rms(xs[0], ps[0])], []), [h0], [pre_mix_g], [BF16], [])
    p, p16 = _matmul("in_proj", NN, u, wi, pl.BlockSpec((m_rows, d), lambda i: (0, 0)), pl.BlockSpec((None, d, ns), lambda i: (i, 0, 0)),
                     pl.BlockSpec((m_rows, ns), lambda i: (0, i)), jax.ShapeDtypeStruct((m_rows, N_DEV * ns), F32), (N_DEV,),
                     twin_bf16=True)
    o, tot, gathered = _attn_fwd(p16, d, list(cast[1:]))
    wco, wao, wo = (g.reshape(d, d) for g in gathered[0:3])
    wfi = gathered[3]
    wfo = gathered[4].reshape(N_DEV // 2, fs, d)
    y = _conv_fwd(p, dww_full, dw_b, d)
    (yc,) = _rows("conv_norm", lambda r0, xs, ps: ([_ln_silu(xs[0], ps[0], ps[1])], []), [y], [conv_ln_g, conv_ln_b], [BF16], [])
    y_conv = _dense_fwd("conv_out", yc, wco)
    y_attn = _dense_fwd("attn_out", o, wao)
    gate_cols = [(d, 5), (d, 6), None, None]
    (mixin,) = _rows("gate_mix", lambda r0, xs, ps: ([_gate_mix(*xs, ps[0])], []), [p, p, y_conv, y_attn], [gate_b], [BF16], [],
                     row_in_cols=gate_cols, row_out_widths=[d])
    mix = _dense_fwd("mix_out", mixin, wo)
    h1, u2 = _rows("post_mix", lambda r0, xs, ps: (list(_post_mix(xs[0], xs[1], ps[0], ps[1])), []), [h0, mix],
                   [post_mix_g, pre_ffn_g], [F32, BF16], [])
    ab = _matmul("ffn_in", NN, u2, wfi, pl.BlockSpec((m_rows, d), lambda i: (0, 0)), pl.BlockSpec((None, d, fs), lambda i: (i, 0, 0)),
                 pl.BlockSpec((None, m_rows, fs), lambda i: (i, 0, 0)), jax.ShapeDtypeStruct((N_DEV, m_rows, fs), BF16), (N_DEV,))
    half = N_DEV // 2
    tm = _row_tile(m_rows)
    pair = lambda off: pl.BlockSpec((None, tm, fs), lambda j, i, off=off: (j + off, i, 0))
    (f_in,) = _rowwise("swiglu", lambda r0, xs, ps: ([_swiglu(xs[0].astype(F32), xs[1].astype(F32))], []), [ab, ab], [],
                       [jax.ShapeDtypeStruct((half, m_rows, fs), BF16)], [], grid=(half, m_rows // tm),
                       in_specs=[pair(0), pair(half)], out_specs=[pair(0)], tm=tm, row_axis=1)
    f = _matmul("ffn_out", NN, f_in, wfo, pl.BlockSpec((None, m_rows, fs), lambda j: (j, 0, 0)), pl.BlockSpec((None, fs, d), lambda j: (j, 0, 0)),
                pl.BlockSpec((m_rows, d), lambda j: (0, 0)), jax.ShapeDtypeStruct((m_rows, d), F32), (half,), acc_axis=0)

    def loss_head(r0, xs, ps):
        h1_, f_, t_ = xs
        r, vjp = jax.vjp(_rms, f_, ps[0])
        rows = r0 + lax.broadcasted_iota(I32, (h1_.shape[0], 1), 0)
        real = (rows >= n_meta) & (rows < length)
        err = jnp.where(real, h1_ + r - t_, 0.0)
        dh2 = err * (1.0 / d)
        d_f, dg = vjp(dh2)
        part = jnp.sum(0.5 * jnp.mean(err * err, axis=-1, keepdims=True), axis=0, keepdims=True)
        return [d_f, dh2], [dg, jnp.broadcast_to(part, (1, LANE))]

    d_f, dh2, g_post_ffn, loss_part = _rows("loss_head", loss_head, [h1, f, target], [post_ffn_g], [BF16, F32], [d, LANE])

    d_fin = _matmul("ffn_out_dx", NT, d_f, wfo, pl.BlockSpec((m_rows, d), lambda j: (0, 0)), pl.BlockSpec((None, fs, d), lambda j: (j, 0, 0)),
                    pl.BlockSpec((None, m_rows, fs), lambda j: (j, 0, 0)), jax.ShapeDtypeStruct((half, m_rows, fs), BF16), (half,))
    g_wfo = _matmul("ffn_out_dw", TN, f_in, d_f, pl.BlockSpec((None, m_rows, fs), lambda j: (j, 0, 0)), pl.BlockSpec((m_rows, d), lambda j: (0, 0)),
                    pl.BlockSpec((None, fs, d), lambda j: (j, 0, 0)), jax.ShapeDtypeStruct((half, fs, d), F32), (half,), twin_bf16=True)

    def swiglu_bwd(r0, xs, ps):
        _, vjp = jax.vjp(_swiglu, xs[0].astype(F32), xs[1].astype(F32))
        return [vjp(xs[2].astype(F32))], []

    (d_ab,) = _rowwise("swiglu_bwd", swiglu_bwd, [ab, ab, d_fin], [],
                       [jax.ShapeDtypeStruct((2, half, m_rows, fs), BF16)], [], grid=(half, m_rows // tm),
                       in_specs=[pair(0), pair(half), pair(0)],
                       out_specs=[pl.BlockSpec((2, None, tm, fs), lambda j, i: (0, j, i, 0))], tm=tm, row_axis=1)
    d_ab = d_ab.reshape(N_DEV, m_rows, fs)
    du2 = _matmul("ffn_in_dx", NT, d_ab, wfi, pl.BlockSpec((None, m_rows, fs), lambda i: (i, 0, 0)), pl.BlockSpec((None, d, fs), lambda i: (i, 0, 0)),
                  pl.BlockSpec((m_rows, d), lambda i: (0, 0)), jax.ShapeDtypeStruct((m_rows, d), F32), (N_DEV,), acc_axis=0)
    g_wfi = _matmul("ffn_in_dw", TN, u2, d_ab, pl.BlockSpec((m_rows, d), lambda i: (0, 0)), pl.BlockSpec((None, m_rows, fs), lambda i: (i, 0, 0)),
                    pl.BlockSpec((None, d, fs), lambda i: (i, 0, 0)), jax.ShapeDtypeStruct((N_DEV, d, fs), F32), (N_DEV,), twin_bf16=True)

    def post_mix_bwd(r0, xs, ps):
        h0_, mix_, dh2_, du2_ = xs
        _, vjp = jax.vjp(_post_mix, h0_, mix_, ps[0], ps[1])
        dh0_, dmix_, dg1, dg2 = vjp((dh2_, du2_))
        return [dmix_, dh0_], [dg1, dg2]

    d_mix, dh1, g_post_mix, g_pre_ffn = _rows("post_mix_bwd", post_mix_bwd, [h0, mix, dh2, du2], [post_mix_g, pre_ffn_g],
                                              [BF16, F32], [d, d])
    d_mixin = _dense_dx("mix_out_dx", d_mix, wo, F32)
    g_wo = _dense_dw("mix_out_dw", mixin, d_mix)

    def gate_mix_bwd(r0, xs, ps):
        _, vjp = jax.vjp(_gate_mix, xs[0], xs[1], xs[2], xs[3], ps[0])
        dpgc, dpga, dyc_, dya_, dgb = vjp(xs[4])
        return [dpgc, dpga, dyc_, dya_], [dgb]

    dp_gc, dp_ga, d_yconv, d_yattn, g_gate_b = _rows(
        "gate_mix_bwd", gate_mix_bwd, [p, p, y_conv, y_attn, d_mixin], [gate_b], [BF16] * 4, [2 * d],
        row_in_cols=gate_cols + [None], row_out_widths=[d] * 4)
    d_o = _dense_dx("attn_out_dx", d_yattn, wao, BF16)
    g_wao = _dense_dw("attn_out_dw", o, d_yattn)
    d_yc = _dense_dx("conv_out_dx", d_yconv, wco, F32)
    g_wco = _dense_dw("conv_out_dw", yc, d_yconv)
    big = {"w_ffn_out": [g.reshape(N_DEV, fr, d) for g in g_wfo], "w_ffn_in": g_wfi,
           "w_o": [g.reshape(N_DEV, dc, d) for g in g_wo], "w_attn_out": [g.reshape(N_DEV, dc, d) for g in g_wao],
           "w_conv_out": [g.reshape(N_DEV, dc, d) for g in g_wco]}
    early = ("w_ffn_out", "w_ffn_in", "w_o", "w_attn_out", "w_conv_out")
    dq, dk, dv, received_early = _attn_bwd(p16, d_o, tot, d, [big[k][1] for k in early])

    def conv_norm_bwd(r0, xs, ps):
        _, vjp = jax.vjp(_ln_silu, xs[0], ps[0], ps[1])
        dy_, dg, db = vjp(xs[1])
        return [dy_], [dg, db]

    d_y, g_ln_g, g_ln_b = _rows("conv_norm_bwd", conv_norm_bwd, [y, d_yc], [conv_ln_g, conv_ln_b], [F32], [d, d])
    dp_a, dp_g, g_dww, g_dwb = _conv_bwd(p, d_y, dww_full, d)
    dp = jnp.concatenate([dp_a, dp_g, dq, dk, dv, dp_gc, dp_ga], axis=1)
    g_wi = _matmul("in_proj_dw", TN, u, dp, pl.BlockSpec((m_rows, d), lambda i: (0, 0)), pl.BlockSpec((m_rows, ns), lambda i: (0, i)),
                   pl.BlockSpec((None, d, ns), lambda i: (i, 0, 0)), jax.ShapeDtypeStruct((N_DEV, d, ns), F32), (N_DEV,), twin_bf16=True)
    x_i, y_i, c_i = _mesh_pos()
    slab_idx = jnp.stack([_dev(*chip, c_i) for chip in _other_chips(x_i, y_i)]).astype(I32)
    (wi_sibling,) = _sibling_exchange([g_wi[1]])
    wi_pairs = _pair_sum(slab_idx, g_wi[0], wi_sibling)
    du, wi_chips = _matmul(
        "in_proj_dx", NT, dp, wi, pl.BlockSpec((m_rows, ns), lambda i: (0, i)), pl.BlockSpec((None, d, ns), lambda i: (i, 0, 0)),
        pl.BlockSpec((m_rows, d), lambda i: (0, 0)), jax.ShapeDtypeStruct((m_rows, d), F32), (N_DEV,), acc_axis=0,
        carried=(_chip_copies, [wi_pairs], [jax.ShapeDtypeStruct(wi_pairs.shape, BF16)],
                 [pltpu.SemaphoreType.DMA((1, 3)), pltpu.SemaphoreType.DMA((1, 3))]))

    def pre_mix_bwd(r0, xs, ps):
        _, vjp = jax.vjp(_rms, xs[0], ps[0])
        dx, dg = vjp(xs[1])
        return [xs[2] + dx], [dg]

    dh0, g_pre_mix = _rows("pre_mix_bwd", pre_mix_bwd, [h0, du, dh1], [pre_mix_g], [F32], [d])
    grad_x = dh0[n_meta:length][None]

    me = _dev(x_i, y_i, c_i)
    me_arr = jnp.reshape(me, (1,)).astype(I32)
    big["w_in"] = g_wi
    received = {k: [r] for k, r in zip(early, received_early)}
    received["w_in"] = [wi_sibling[3:4], wi_chips]
    results = {}
    for k in SHARDED:
        outs = _adamw_shard(me_arr, big[k][0], received[k], local[k], given["m_" + k][0], given["v_" + k][0])
        results[k] = tuple(a[None] for a in outs)

    rep_grads = {"pre_mix_g": g_pre_mix, "gate_b": g_gate_b, "dw_b": g_dwb, "conv_ln_g": g_ln_g, "conv_ln_b": g_ln_b,
                 "post_mix_g": g_post_mix, "pre_ffn_g": g_pre_ffn, "post_ffn_g": g_post_ffn}

    def pack_rep(get):
        return jnp.concatenate([get(k) for k in REPLICATED], axis=1).reshape(-1, LANE)

    rep_rows = pack_rep(lambda k: rep_grads[k])
    n_rep = rep_rows.shape[0]
    loss_rows = jnp.broadcast_to(loss_part, (8, LANE))
    g_meta = dh0[0:n_meta]
    slabs = jnp.stack([jnp.concatenate([rep_rows, loss_rows, g_dww[:, j * LANE:(j + 1) * LANE], g_meta[:, j * LANE:(j + 1) * LANE]], axis=0)
                       for j in range(N_DEV)])

    def pack_small(prefix):
        dww_own = jnp.pad(given[prefix + "dw_w"][0], ((0, CONV_PAD - CONV_WIDTH), (0, 0)))
        return jnp.concatenate([pack_rep(lambda k: given[prefix + k]), jnp.zeros((8, LANE), F32), dww_own,
                                given[prefix + "meta_tokens"]], axis=0)

    small = _small_reduce_adamw(slabs, pack_small(""), pack_small("m_"), pack_small("v_"))
    loss = small[0][n_rep, 0]

    def unpack(arr):
        out = {}
        flat = arr[:n_rep].reshape(1, -1)
        off = 0
        for k in REPLICATED:
            w = given[k].shape[1]
            out[k] = flat[:, off:off + w]
            off += w
        out["dw_w"] = arr[n_rep + 8:n_rep + 8 + CONV_WIDTH][None]
        out["meta_tokens"] = arr[n_rep + 8 + CONV_PAD:n_rep + 8 + CONV_PAD + n_meta]
        return out

    small_out = [unpack(a) for a in small]
    for k in WEIGHTS:
        if k not in results:
            results[k] = tuple(s[k] for s in small_out)
    return (loss, grad_x, *[results[k][0] for k in WEIGHTS], *[results[k][1] for k in WEIGHTS],
            *[results[k][2] for k in WEIGHTS], *[results[k][3] for k in WEIGHTS])
```

```python
import jax
import jax.numpy as jnp
from jax import lax
from jax.experimental import pallas as pl
from jax.experimental.pallas import tpu as pltpu

F32 = jnp.float32
BF16 = jnp.bfloat16
I32 = jnp.int32

N_DEV = 8
LANE = 128
HEAD_DIM = 64
QB = 128
KEY_SHIFT = 2
KEY_TILES = 1 << KEY_SHIFT
KEY_CHUNK = KEY_TILES * QB
LANE_BLOCKS = 2
BWD_LANE_BLOCKS = 2
CONV_WIDTH = 31
CONV_PAD = 32
ROW_CHUNK = 128
RMS_EPS = 1e-6
LN_EPS = 1e-5
ADAM_LR = 0.001
ADAM_B1 = 0.9
ADAM_B2 = 0.999
ADAM_EPS = 1e-08
ADAM_WD = 0.01
ADAM_STEP = 10
VMEM_LIMIT = 56 * 1024 * 1024

NN = (((1,), (0,)), ((), ()))
NT = (((1,), (1,)), ((), ()))
TN = (((0,), (0,)), ((), ()))
MESH = pl.DeviceIdType.MESH
ANY = pl.BlockSpec(memory_space=pl.ANY)
VMEM_WHOLE = pl.BlockSpec(memory_space=pltpu.VMEM)


def _params(n_axes):
    return pltpu.CompilerParams(dimension_semantics=("arbitrary",) * n_axes, vmem_limit_bytes=VMEM_LIMIT)


def _row_tile(m):
    assert m % QB == 0
    return m // 4 if m % 64 == 0 else QB


def _matmul(name, dims, a, b, a_spec, b_spec, o_spec, out_shape, grid, acc_axis=None, twin_bf16=False, carried=None):
    n_car = 0 if carried is None else len(carried[1])

    def body(a_ref, b_ref, *rest):
        car_ins, o_ref, rest = rest[:n_car], rest[n_car], rest[n_car + 1:]
        if carried is not None:
            starts, waits = carried[0](car_ins, rest[:n_car], *rest[n_car:])

            @pl.when(pl.program_id(0) == 0)
            def _():
                for f in starts:
                    f()

        r = lax.dot_general(a_ref[...], b_ref[...], dims, preferred_element_type=F32)
        if acc_axis is None:
            o_ref[...] = r.astype(o_ref.dtype)
            if twin_bf16:
                rest[0][...] = r.astype(BF16)
        else:
            k = pl.program_id(acc_axis)

            @pl.when(k == 0)
            def _():
                o_ref[...] = r

            @pl.when(k > 0)
            def _():
                o_ref[...] += r

        if carried is not None:
            @pl.when(pl.program_id(0) == grid[0] - 1)
            def _():
                for f in waits:
                    f()

    in_specs, out_specs, out_shapes, scratch = [a_spec, b_spec], [o_spec], [out_shape], []
    if twin_bf16:
        assert acc_axis is None and carried is None
        out_specs.append(o_spec)
        out_shapes.append(jax.ShapeDtypeStruct(out_shape.shape, BF16))
    if carried is not None:
        assert len(grid) == 1
        in_specs += [ANY] * n_car
        out_specs += [ANY] * n_car
        out_shapes += list(carried[2])
        scratch = list(carried[3])
    outs = pl.pallas_call(body, name=name, grid=grid, in_specs=in_specs, out_specs=out_specs, out_shape=out_shapes,
                          scratch_shapes=scratch, compiler_params=_params(len(grid)))(a, b, *(carried[1] if carried else ()))
    return outs[0] if len(outs) == 1 else outs


DENSE_TILE = 256


def _dense_fwd(name, a, w, out_dtype=F32):
    m, k = a.shape
    n = w.shape[1]
    tn = DENSE_TILE
    return _matmul(name, NN, a, w, pl.BlockSpec((m, k), lambda j: (0, 0)), pl.BlockSpec((k, tn), lambda j: (0, j)),
                   pl.BlockSpec((m, tn), lambda j: (0, j)), jax.ShapeDtypeStruct((m, n), out_dtype), (n // tn,))


def _dense_dx(name, dy, w, out_dtype):
    m, n = dy.shape
    k = w.shape[0]
    tk = DENSE_TILE
    return _matmul(name, NT, dy, w, pl.BlockSpec((m, n), lambda j: (0, 0)), pl.BlockSpec((tk, n), lambda j: (j, 0)),
                   pl.BlockSpec((m, tk), lambda j: (0, j)), jax.ShapeDtypeStruct((m, k), out_dtype), (k // tk,))


def _dense_dw(name, a, dy):
    m, k = a.shape
    n = dy.shape[1]
    tn = DENSE_TILE
    return _matmul(name, TN, a, dy, pl.BlockSpec((m, k), lambda j: (0, 0)), pl.BlockSpec((m, tn), lambda j: (0, j)),
                   pl.BlockSpec((k, tn), lambda j: (0, j)), jax.ShapeDtypeStruct((k, n), F32), (n // tn,), twin_bf16=True)


def _rowwise(name, fn, row_ins, par_ins, row_outs, par_outs, *, grid, in_specs, out_specs, tm, row_axis):
    n_ri, n_pi, n_ro, n_po = len(row_ins), len(par_ins), len(row_outs), len(par_outs)
    n_steps, tail = divmod(tm, ROW_CHUNK)
    assert tail % 16 == 0

    def body(*refs):
        ri = refs[:n_ri]
        pi = refs[n_ri:n_ri + n_pi]
        ro = refs[n_ri + n_pi:n_ri + n_pi + n_ro]
        po = refs[n_ri + n_pi + n_ro:]
        ps = [r[...] for r in pi]
        base = pl.program_id(row_axis) * tm

        def chunk(r0, rows, carry):
            xs = [r[pl.ds(r0, rows), :] for r in ri]
            outs, pouts = fn(base + r0, xs, ps)
            for r, o in zip(ro, outs):
                if isinstance(o, (list, tuple)):
                    for j, part in enumerate(o):
                        r[j, pl.ds(r0, rows), :] = part.astype(r.dtype)
                else:
                    r[pl.ds(r0, rows), :] = o.astype(r.dtype)
            return tuple(c + q for c, q in zip(carry, pouts))

        def step(i, carry):
            return chunk(pl.multiple_of(i * ROW_CHUNK, ROW_CHUNK), ROW_CHUNK, carry)

        acc = lax.fori_loop(0, n_steps, step, tuple(jnp.zeros(s.shape, F32) for s in par_outs))
        if tail:
            acc = chunk(n_steps * ROW_CHUNK, tail, acc)
        if n_po:
            first = pl.program_id(0) == 0
            for ax in range(1, len(grid)):
                first = first & (pl.program_id(ax) == 0)

            @pl.when(first)
            def _():
                for r in po:
                    r[...] = jnp.zeros_like(r)

            for r, a in zip(po, acc):
                r[...] += a

    return pl.pallas_call(body, name=name, grid=grid, in_specs=in_specs, out_specs=out_specs,
                          out_shape=tuple(row_outs) + tuple(par_outs),
                          compiler_params=_params(len(grid)))(*row_ins, *par_ins)


def _rows(name, fn, row_ins, par_ins, row_out_dtypes, par_out_widths, row_in_cols=None, row_out_widths=None):
    m = row_ins[0].shape[0]
    tm = _row_tile(m)
    in_specs = []
    for k, a in enumerate(row_ins):
        if row_in_cols is not None and row_in_cols[k] is not None:
            width, cb = row_in_cols[k]
            in_specs.append(pl.BlockSpec((tm, width), lambda i, cb=cb: (i, cb)))
        else:
            in_specs.append(pl.BlockSpec((tm, a.shape[1]), lambda i: (i, 0)))
    for a in par_ins:
        in_specs.append(pl.BlockSpec(a.shape, lambda i: (0, 0)))
    if row_out_widths is None:
        row_out_widths = [row_ins[0].shape[1]] * len(row_out_dtypes)
    row_outs = [jax.ShapeDtypeStruct((m, w), dt) for w, dt in zip(row_out_widths, row_out_dtypes)]
    par_outs = [jax.ShapeDtypeStruct((1, w), F32) for w in par_out_widths]
    out_specs = [pl.BlockSpec((tm, s.shape[1]), lambda i: (i, 0)) for s in row_outs]
    out_specs += [pl.BlockSpec(s.shape, lambda i: (0, 0)) for s in par_outs]
    return _rowwise(name, fn, row_ins, par_ins, row_outs, par_outs, grid=(m // tm,), in_specs=in_specs,
                    out_specs=out_specs, tm=tm, row_axis=0)


def _rms(x, g):
    return x * lax.rsqrt(jnp.mean(x * x, axis=-1, keepdims=True) + RMS_EPS) * g


def _ln_silu(y, g, b):
    mu = jnp.mean(y, axis=-1, keepdims=True)
    yc = y - mu
    var = jnp.mean(yc * yc, axis=-1, keepdims=True)
    return jax.nn.silu(yc * lax.rsqrt(var + LN_EPS) * g + b)


def _gate_mix(pgc, pga, yc, ya, gb):
    d = pgc.shape[1]
    return jax.nn.sigmoid(pgc + gb[:, :d]) * yc + jax.nn.sigmoid(pga + gb[:, d:]) * ya


def _post_mix(h0, mix, g_post, g_pre):
    h1 = h0 + _rms(mix, g_post)
    return h1, _rms(h1, g_pre)


def _swiglu(a, b):
    return jax.nn.silu(a) * b


def _conv_taps():
    taps = []
    for b in range(8):
        for a in range(CONV_PAD // 8):
            s = 8 * a + b
            if s < CONV_WIDTH:
                taps.append((b, a, CONV_WIDTH - 1 - s))
    return taps


def _conv_fwd(p, dww, dwb, d_model):
    m = p.shape[0]
    nch = d_model // LANE
    n_chunk = m // QB
    taps = _conv_taps()

    def body(a_ref, g_ref, w_ref, b_ref, y_ref, upad):
        upad[0:CONV_PAD, :] = jnp.zeros((CONV_PAD, LANE), F32)

        def fill(i, c):
            r0 = pl.multiple_of(i * QB, QB)
            u = a_ref[pl.ds(r0, QB), :] * jax.nn.sigmoid(g_ref[pl.ds(r0, QB), :])
            upad[pl.ds(pl.multiple_of(r0 + CONV_PAD, 8), QB), :] = u
            return c

        lax.fori_loop(0, n_chunk, fill, 0)

        def conv(i, c):
            r0 = pl.multiple_of(i * QB, QB)
            win = upad[pl.ds(r0, QB + CONV_PAD), :]
            acc = jnp.broadcast_to(b_ref[...], (QB, LANE))
            rolled = {}
            for b, a, j in taps:
                if b not in rolled:
                    rolled[b] = win if b == 0 else pltpu.roll(win, b, axis=0)
                lo = CONV_PAD - 8 * a
                acc = acc + w_ref[j:j + 1, :] * rolled[b][lo:lo + QB, :]
            y_ref[pl.ds(r0, QB), :] = acc
            return c

        lax.fori_loop(0, n_chunk, conv, 0)

    col = lambda off: pl.BlockSpec((m, LANE), lambda c: (0, off + c))
    return pl.pallas_call(
        body, name="conv_fwd", grid=(nch,),
        in_specs=[col(0), col(nch), pl.BlockSpec((CONV_PAD, LANE), lambda c: (0, c)), pl.BlockSpec((1, LANE), lambda c: (0, c))],
        out_specs=col(0), out_shape=jax.ShapeDtypeStruct((m, d_model), F32),
        scratch_shapes=[pltpu.VMEM((m + CONV_PAD, LANE), F32)], compiler_params=_params(1))(p, p, dww, dwb)


def _conv_bwd(p, dy, dww, d_model):
    m = p.shape[0]
    nch = d_model // LANE
    n_chunk = m // QB
    taps = _conv_taps()
    win_rows = QB + CONV_PAD

    def body(a_ref, g_ref, dy_ref, w_ref, da_ref, dg_ref, dw_ref, db_ref, upad, dypad, wacc, bacc):
        upad[0:CONV_PAD, :] = jnp.zeros((CONV_PAD, LANE), F32)
        dypad[m:m + CONV_PAD, :] = jnp.zeros((CONV_PAD, LANE), F32)
        wacc[...] = jnp.zeros_like(wacc)
        bacc[...] = jnp.zeros_like(bacc)

        def fill(i, c):
            r0 = pl.multiple_of(i * QB, QB)
            u = a_ref[pl.ds(r0, QB), :] * jax.nn.sigmoid(g_ref[pl.ds(r0, QB), :])
            upad[pl.ds(pl.multiple_of(r0 + CONV_PAD, 8), QB), :] = u
            dypad[pl.ds(r0, QB), :] = dy_ref[pl.ds(r0, QB), :]
            return c

        lax.fori_loop(0, n_chunk, fill, 0)

        def chunk(i, c):
            r0 = pl.multiple_of(i * QB, QB)
            dwin = dypad[pl.ds(r0, win_rows), :]
            du = jnp.zeros((QB, LANE), F32)
            rolled = {}
            for b, a, j in taps:
                if b not in rolled:
                    rolled[b] = dwin if b == 0 else pltpu.roll(dwin, win_rows - b, axis=0)
                du = du + w_ref[j:j + 1, :] * rolled[b][8 * a:8 * a + QB, :]
            av = a_ref[pl.ds(r0, QB), :]
            sg = jax.nn.sigmoid(g_ref[pl.ds(r0, QB), :])
            da_ref[pl.ds(r0, QB), :] = (du * sg).astype(da_ref.dtype)
            dg_ref[pl.ds(r0, QB), :] = (du * av * sg * (1.0 - sg)).astype(dg_ref.dtype)
            dyc = dy_ref[pl.ds(r0, QB), :]
            uwin = upad[pl.ds(r0, win_rows), :]
            rolled = {}
            for b, a, j in taps:
                if b not in rolled:
                    rolled[b] = uwin if b == 0 else pltpu.roll(uwin, b, axis=0)
                lo = CONV_PAD - 8 * a
                prod = dyc * rolled[b][lo:lo + QB, :]
                wacc[j] += prod.reshape(QB // 8, 8, LANE).sum(axis=0)
            bacc[...] += dyc.reshape(QB // 8, 8, LANE).sum(axis=0)
            return c

        lax.fori_loop(0, n_chunk, chunk, 0)
        for j in range(CONV_WIDTH):
            dw_ref[j:j + 1, :] = jnp.sum(wacc[j], axis=0, keepdims=True)
        dw_ref[CONV_WIDTH:CONV_PAD, :] = jnp.zeros((CONV_PAD - CONV_WIDTH, LANE), F32)
        db_ref[...] = jnp.sum(bacc[...], axis=0, keepdims=True)

    col = lambda off: pl.BlockSpec((m, LANE), lambda c: (0, off + c))
    return pl.pallas_call(
        body, name="conv_bwd", grid=(nch,),
        in_specs=[col(0), col(nch), col(0), pl.BlockSpec((CONV_PAD, LANE), lambda c: (0, c))],
        out_specs=[col(0), col(0), pl.BlockSpec((CONV_PAD, LANE), lambda c: (0, c)), pl.BlockSpec((1, LANE), lambda c: (0, c))],
        out_shape=(jax.ShapeDtypeStruct((m, d_model), BF16), jax.ShapeDtypeStruct((m, d_model), BF16),
                   jax.ShapeDtypeStruct((CONV_PAD, d_model), F32), jax.ShapeDtypeStruct((1, d_model), F32)),
        scratch_shapes=[pltpu.VMEM((m + CONV_PAD, LANE), F32), pltpu.VMEM((m + CONV_PAD, LANE), F32),
                        pltpu.VMEM((CONV_PAD, 8, LANE), F32), pltpu.VMEM((8, LANE), F32)],
        compiler_params=_params(1))(p, p, dy, dww)


EXP_CLAMP = 80.0


def _one_plus_exp(z):
    return 1.0 + jnp.exp(jnp.minimum(z, EXP_CLAMP))


def _softplus(z):
    return jnp.maximum(jnp.log(_one_plus_exp(z)), z)


def _softplus_sigmoid(z):
    s = _one_plus_exp(z)
    return jnp.maximum(jnp.log(s), z), 1.0 - 1.0 / s


def _tile_cumsums(x, tri2):
    xb = x.astype(BF16)
    out = []
    for i in range(0, KEY_TILES, 2):
        both = jnp.dot(xb[:, i * QB:(i + 2) * QB], tri2, preferred_element_type=F32)
        out += [both[:, :QB], both[:, QB:]]
    return out


def _tri2(kind):
    jj = lax.broadcasted_iota(I32, (2 * QB, 2 * QB), 0)
    ss = lax.broadcasted_iota(I32, (2 * QB, 2 * QB), 1)
    same = (jj >= QB) == (ss >= QB)
    keep = {"ge": jj >= ss, "le": jj <= ss}[kind]
    return jnp.where(same & keep, 1.0, 0.0).astype(BF16)


def _attn_fwd(p, d_model, shards):
    m = p.shape[0]
    nqb = m // QB
    ngrp = d_model // (LANE_BLOCKS * LANE)
    qo, ko, vo = 2 * ngrp, 3 * ngrp, 4 * ngrp
    scale = HEAD_DIM ** -0.5
    n_sh = len(shards)

    assert nqb >= KEY_TILES

    def body(q_ref, k_ref, v_ref, *rest):
        shard_refs, (o_ref, t_ref), rest = rest[:n_sh], rest[n_sh:n_sh + 2], rest[n_sh + 2:]
        gathered_refs, (acc_ref, car_ref), sems = rest[:n_sh], rest[n_sh:n_sh + 2], rest[n_sh + 2:]
        starts, relays, waits = _gather_copies(shard_refs, gathered_refs, *sems)

        @pl.when((pl.program_id(0) == 0) & (pl.program_id(1) == 0))
        def _():
            for f in starts:
                f()

        @pl.when((pl.program_id(0) == ngrp - 1) & (pl.program_id(1) == 0))
        def _():
            for f in relays:
                f()

        qb = pl.program_id(1)
        lane = lax.broadcasted_iota(I32, (QB, LANE), 1)
        head0 = lane < HEAD_DIM
        row_g = qb * QB + lax.broadcasted_iota(I32, (QB, KEY_CHUNK), 0)
        col_l = lax.broadcasted_iota(I32, (QB, KEY_CHUNK), 1)
        tri = _tri2("ge")
        heads = range(2 * LANE_BLOCKS)
        qh = []
        for lb in range(LANE_BLOCKS):
            q2 = (q_ref[:, lb * LANE:(lb + 1) * LANE] * scale).astype(BF16)
            zero = jnp.zeros_like(q2)
            qh += [jnp.where(head0, q2, zero), jnp.where(head0, zero, q2)]
        acc_ref[...] = jnp.zeros_like(acc_ref)
        car_ref[...] = jnp.zeros_like(car_ref)

        def chunk(first_tile, bound):
            r0 = pl.multiple_of(first_tile * QB, QB)
            kcs = [k_ref[pl.ds(r0, KEY_CHUNK), lb * LANE:(lb + 1) * LANE].astype(BF16) for lb in range(LANE_BLOCKS)]
            vcs = [v_ref[pl.ds(r0, KEY_CHUNK), lb * LANE:(lb + 1) * LANE].astype(BF16) for lb in range(LANE_BLOCKS)]
            valid = None if bound is None else (col_l + r0) < bound
            zs = [lax.dot_general(qh[h], kcs[h // 2], NT, preferred_element_type=F32) for h in heads]
            sps = [_softplus(z) for z in zs]
            if valid is not None:
                sps = [jnp.where(valid, sp, 0.0) for sp in sps]
            cums = [_tile_cumsums(sp, tri) for sp in sps]
            cars = [car_ref[h] for h in heads]
            a_tiles = [[None] * KEY_TILES for h in heads]
            for i in reversed(range(KEY_TILES)):
                for h in heads:
                    cum = cums[h][i]
                    a_tiles[h][i] = jnp.exp(zs[h][:, i * QB:(i + 1) * QB] - (cum + cars[h]))
                    cars[h] = cars[h] + jnp.broadcast_to(cum[:, 0:1], cum.shape)
            for h in heads:
                a = jnp.concatenate(a_tiles[h], axis=1)
                if valid is not None:
                    a = jnp.where(valid, a, 0.0)
                acc_ref[h] += jnp.dot(a.astype(BF16), vcs[h // 2], preferred_element_type=F32)
                car_ref[h] = cars[h]

        near = jnp.maximum(qb - (KEY_TILES - 1), 0)
        chunk(near, row_g)
        n_full = lax.shift_right_logical(near, KEY_SHIFT)

        def step(i, c):
            chunk(near - KEY_TILES * (i + 1), None)
            return c

        lax.fori_loop(0, n_full, step, 0)
        left = near - KEY_TILES * n_full

        @pl.when(left > 0)
        def _():
            chunk(0, left * QB)

        for lb in range(LANE_BLOCKS):
            o_ref[:, lb * LANE:(lb + 1) * LANE] = jnp.where(head0, acc_ref[2 * lb], acc_ref[2 * lb + 1]).astype(o_ref.dtype)
        for h in heads:
            t_ref[:, h * QB:(h + 1) * QB] = car_ref[h]

        @pl.when((pl.program_id(0) == ngrp - 1) & (pl.program_id(1) == nqb - 1))
        def _():
            for f in waits:
                f()

    wide = LANE_BLOCKS * LANE
    outs = pl.pallas_call(
        body, name="attn_fwd", grid=(ngrp, nqb),
        in_specs=[pl.BlockSpec((QB, wide), lambda g, qb: (qb, qo + g)),
                  pl.BlockSpec((m, wide), lambda g, qb: (0, ko + g)),
                  pl.BlockSpec((m, wide), lambda g, qb: (0, vo + g))] + [ANY] * n_sh,
        out_specs=[pl.BlockSpec((QB, wide), lambda g, qb: (qb, g)),
                   pl.BlockSpec((QB, 2 * wide), lambda g, qb: (qb, g))] + [ANY] * n_sh,
        out_shape=[jax.ShapeDtypeStruct((m, d_model), BF16), jax.ShapeDtypeStruct((m, 2 * d_model), F32)]
        + [jax.ShapeDtypeStruct((N_DEV,) + s.shape, s.dtype) for s in shards],
        scratch_shapes=[pltpu.VMEM((2 * LANE_BLOCKS, QB, LANE), F32), pltpu.VMEM((2 * LANE_BLOCKS, QB, LANE), F32)]
        + _exchange_sems(n_sh) + [pltpu.SemaphoreType.DMA((n_sh,))],
        compiler_params=_params(2))(p, p, p, *shards)
    return outs[0], outs[1], outs[2:]


def _attn_bwd(p, d_o, tot, d_model, slabs):
    m = p.shape[0]
    nqb = m // QB
    blocks = BWD_LANE_BLOCKS
    ngrp = d_model // (blocks * LANE)
    qo, ko, vo = 2 * ngrp, 3 * ngrp, 4 * ngrp
    scale = HEAD_DIM ** -0.5
    n_sl = len(slabs)

    assert nqb >= KEY_TILES

    def body(q_ref, k_ref, v_ref, do_ref, t_ref, *rest):
        slab_refs, (dq_ref, dk_ref, dv_ref), rest = rest[:n_sl], rest[n_sl:n_sl + 3], rest[n_sl + 3:]
        recv_refs, (dkacc, dvacc, dqacc, csp, cg), sems = rest[:n_sl], rest[n_sl:n_sl + 5], rest[n_sl + 5:]
        starts, waits = _scatter_copies(slab_refs, recv_refs, *sems)

        @pl.when((pl.program_id(0) == 0) & (pl.program_id(1) == 0))
        def _():
            for f in starts:
                f()

        qb = pl.program_id(1)
        lane = lax.broadcasted_iota(I32, (QB, LANE), 1)
        head0 = lane < HEAD_DIM
        row_g = qb * QB + lax.broadcasted_iota(I32, (QB, KEY_CHUNK), 0)
        col_l = lax.broadcasted_iota(I32, (QB, KEY_CHUNK), 1)
        tri_ge = _tri2("ge")
        tri_le = _tri2("le")
        heads = range(2 * blocks)
        qh, doh = [], []
        for lb in range(blocks):
            q2 = (q_ref[:, lb * LANE:(lb + 1) * LANE] * scale).astype(BF16)
            do2 = do_ref[:, lb * LANE:(lb + 1) * LANE]
            zero = jnp.zeros_like(q2)
            qh += [jnp.where(head0, q2, zero), jnp.where(head0, zero, q2)]
            doh += [jnp.where(head0, do2, zero), jnp.where(head0, zero, do2)]
        q_pairs = [jnp.concatenate(qh[2 * lb:2 * lb + 2], axis=0) for lb in range(blocks)]
        do_pairs = [jnp.concatenate(doh[2 * lb:2 * lb + 2], axis=0) for lb in range(blocks)]

        @pl.when(qb == 0)
        def _():
            dkacc[...] = jnp.zeros_like(dkacc)
            dvacc[...] = jnp.zeros_like(dvacc)

        dqacc[...] = jnp.zeros_like(dqacc)
        for h in heads:
            csp[h] = t_ref[:, h * QB:(h + 1) * QB]
        cg[...] = jnp.zeros_like(cg)

        def chunk(first_tile, bound):
            r0 = pl.multiple_of(first_tile * QB, QB)
            kcs = [k_ref[pl.ds(r0, KEY_CHUNK), lb * LANE:(lb + 1) * LANE].astype(BF16) for lb in range(blocks)]
            vcs = [v_ref[pl.ds(r0, KEY_CHUNK), lb * LANE:(lb + 1) * LANE].astype(BF16) for lb in range(blocks)]
            valid = None if bound is None else (col_l + r0) < bound
            tiles = [slice(i * QB, (i + 1) * QB) for i in range(KEY_TILES)]
            zs = [lax.dot_general(qh[h], kcs[h // 2], NT, preferred_element_type=F32) for h in heads]
            das = [lax.dot_general(doh[h], vcs[h // 2], NT, preferred_element_type=F32) for h in heads]
            sps, sgs = zip(*[_softplus_sigmoid(z) for z in zs])
            if valid is not None:
                sps = [jnp.where(valid, sp, 0.0) for sp in sps]
            cums = [_tile_cumsums(sp, tri_ge) for sp in sps]
            a_tiles, g_tiles = [[] for h in heads], [[] for h in heads]
            for h in heads:
                rest = csp[h]
                for i, c in enumerate(tiles):
                    cum = cums[h][i]
                    rest = rest - jnp.broadcast_to(cum[:, 0:1], cum.shape)
                    a = jnp.exp(zs[h][:, c] - (cum + rest))
                    if valid is not None:
                        a = jnp.where(valid[:, c], a, 0.0)
                    a_tiles[h].append(a)
                    g_tiles[h].append(a * das[h][:, c])
                csp[h] = rest
            gcums = [_tile_cumsums(jnp.concatenate(g_tiles[h], axis=1), tri_le) for h in heads]
            dzbs, abs_ = [], []
            for h in heads:
                g_before = cg[h]
                dz_tiles = []
                for i, c in enumerate(tiles):
                    gcum = gcums[h][i]
                    dz = g_tiles[h][i] - sgs[h][:, c] * (g_before + gcum)
                    if valid is not None:
                        dz = jnp.where(valid[:, c], dz, 0.0)
                    dz_tiles.append(dz)
                    g_before = g_before + jnp.broadcast_to(gcum[:, QB - 1:QB], gcum.shape)
                cg[h] = g_before
                dzbs.append(jnp.concatenate(dz_tiles, axis=1).astype(BF16))
                abs_.append(jnp.concatenate(a_tiles[h], axis=1).astype(BF16))
            for h in heads:
                dqacc[h] += jnp.dot(dzbs[h], kcs[h // 2], preferred_element_type=F32)
            for lb in range(blocks):
                dz_pair = jnp.concatenate(dzbs[2 * lb:2 * lb + 2], axis=0)
                a_pair = jnp.concatenate(abs_[2 * lb:2 * lb + 2], axis=0)
                dkacc[pl.ds(r0, KEY_CHUNK), lb * LANE:(lb + 1) * LANE] += lax.dot_general(
                    dz_pair, q_pairs[lb], TN, preferred_element_type=F32)
                dvacc[pl.ds(r0, KEY_CHUNK), lb * LANE:(lb + 1) * LANE] += lax.dot_general(
                    a_pair, do_pairs[lb], TN, preferred_element_type=F32)

        near = jnp.maximum(qb - (KEY_TILES - 1), 0)
        n_full = lax.shift_right_logical(near, KEY_SHIFT)

        def step(i, c):
            chunk(KEY_TILES * i, None)
            return c

        lax.fori_loop(0, n_full, step, 0)

        @pl.when(near > KEY_TILES * n_full)
        def _():
            chunk(KEY_TILES * n_full, near * QB)

        chunk(near, row_g)
        for lb in range(blocks):
            dq2 = jnp.where(head0, dqacc[2 * lb], dqacc[2 * lb + 1]) * scale
            dq_ref[:, lb * LANE:(lb + 1) * LANE] = dq2.astype(dq_ref.dtype)

        @pl.when(qb == nqb - 1)
        def _():
            dk_ref[...] = dkacc[...].astype(dk_ref.dtype)
            dv_ref[...] = dvacc[...].astype(dv_ref.dtype)

        @pl.when((pl.program_id(0) == ngrp - 1) & (qb == nqb - 1))
        def _():
            for f in waits:
                f()

    out = jax.ShapeDtypeStruct((m, d_model), BF16)
    wide = blocks * LANE
    carry = pltpu.VMEM((2 * blocks, QB, LANE), F32)
    outs = pl.pallas_call(
        body, name="attn_bwd", grid=(ngrp, nqb),
        in_specs=[pl.BlockSpec((QB, wide), lambda g, qb: (qb, qo + g)),
                  pl.BlockSpec((m, wide), lambda g, qb: (0, ko + g)),
                  pl.BlockSpec((m, wide), lambda g, qb: (0, vo + g)),
                  pl.BlockSpec((QB, wide), lambda g, qb: (qb, g)),
                  pl.BlockSpec((QB, 2 * wide), lambda g, qb: (qb, g))] + [ANY] * n_sl,
        out_specs=[pl.BlockSpec((QB, wide), lambda g, qb: (qb, g)),
                   pl.BlockSpec((m, wide), lambda g, qb: (0, g)),
                   pl.BlockSpec((m, wide), lambda g, qb: (0, g))] + [ANY] * n_sl,
        out_shape=[out, out, out] + _received_shapes(slabs),
        scratch_shapes=[pltpu.VMEM((m, wide), F32), pltpu.VMEM((m, wide), F32), carry, carry, carry] + _exchange_sems(n_sl),
        compiler_params=_params(2))(p, p, p, d_o, tot, *slabs)
    return outs[0], outs[1], outs[2], outs[3:]


def _mesh_pos():
    return lax.axis_index("x"), lax.axis_index("y"), lax.axis_index("c")


def _other_chips(x, y):
    return [(1 - x, y), (x, 1 - y), (1 - x, 1 - y)]


def _dev(x, y, c):
    return 4 * x + 2 * y + c


def _all_gather(shards):
    n = len(shards)

    def body(*refs):
        starts, relays, waits = _gather_copies(refs[:n], refs[n:2 * n], *refs[2 * n:])
        for f in starts + relays + waits:
            f()

    return pl.pallas_call(
        body, name="comm_all_gather", in_specs=[ANY] * n, out_specs=[ANY] * n,
        out_shape=[jax.ShapeDtypeStruct((N_DEV,) + s.shape, s.dtype) for s in shards],
        scratch_shapes=[pltpu.SemaphoreType.DMA((n, 7)), pltpu.SemaphoreType.DMA((n, 7)), pltpu.SemaphoreType.DMA((n,))],
    )(*shards)


def _peers(x, y, c):
    out = []
    for mask in range(1, N_DEV):
        px, py, pc = x ^ (mask >> 2), y ^ ((mask >> 1) & 1), c ^ (mask & 1)
        out.append((mask - 1, (px, py, pc), _dev(px, py, pc)))
    return out


def _remote(src, dst, send_sems, recv_sems, k, s, peer):
    return pltpu.make_async_remote_copy(src_ref=src, dst_ref=dst, send_sem=send_sems.at[k, s], recv_sem=recv_sems.at[k, s],
                                        device_id=peer, device_id_type=MESH)


def _gather_copies(ins, outs, send_sems, recv_sems, local_sems):
    x, y, c = _mesh_pos()
    sibling = (x, y, 1 - c)
    chips = _other_chips(x, y)
    starts, relays, waits = [], [], []
    for k in range(len(ins)):
        def slot(block, k=k):
            return outs[k].at[_dev(*block)]

        def copy(s, src, block, to, k=k):
            return _remote(src, slot(block), send_sems, recv_sems, k, s, to)

        own = pltpu.make_async_copy(ins[k], slot((x, y, c)), local_sems.at[k])
        to_sibling = copy(0, ins[k], (x, y, c), sibling)
        starts += [own.start, to_sibling.start]
        waits += [own.wait, to_sibling.wait_send, copy(0, ins[k], (x, y, 1 - c), sibling).wait_recv]
        for j, chip in enumerate(chips):
            out = copy(1 + j, ins[k], (x, y, c), (*chip, c))
            relay = copy(4 + j, slot((*chip, c)), (*chip, c), sibling)
            starts.append(out.start)
            relays += [copy(1 + j, ins[k], (*chip, c), sibling).wait_recv, relay.start]
            waits += [out.wait_send, relay.wait_send, copy(4 + j, ins[k], (*chip, 1 - c), sibling).wait_recv]
    return starts, relays, waits


def _scatter_copies(ins, outs, send_sems, recv_sems):
    x, y, c = _mesh_pos()
    starts, waits = [], []
    for k in range(len(ins)):
        for s, peer, idx in _peers(x, y, c):
            send = _remote(ins[k].at[idx], outs[k].at[s], send_sems, recv_sems, k, s, peer)
            starts.append(send.start)
            waits += [send.wait_recv, send.wait_send]
    return starts, waits


def _exchange_sems(n):
    return [pltpu.SemaphoreType.DMA((n, N_DEV - 1)), pltpu.SemaphoreType.DMA((n, N_DEV - 1))]


def _received_shapes(slabs):
    return [jax.ShapeDtypeStruct((N_DEV - 1,) + a.shape[1:], a.dtype) for a in slabs]


def _chips_and_own(x, y):
    return _other_chips(x, y) + [(x, y)]


def _sibling_exchange(slabs):
    n = len(slabs)

    def body(*refs):
        ins, outs, (send_sems, recv_sems) = refs[:n], refs[n:2 * n], refs[2 * n:]
        x, y, c = _mesh_pos()
        copies = [_remote(ins[k].at[_dev(*chip, 1 - c)], outs[k].at[r], send_sems, recv_sems, k, r, (x, y, 1 - c))
                  for k in range(n) for r, chip in enumerate(_chips_and_own(x, y))]
        for cp in copies:
            cp.start()
        for cp in copies:
            cp.wait_recv()
        for cp in copies:
            cp.wait_send()

    return pl.pallas_call(body, name="comm_rs_sibling", in_specs=[ANY] * n, out_specs=[ANY] * n,
                          out_shape=[jax.ShapeDtypeStruct((4,) + a.shape[1:], a.dtype) for a in slabs],
                          scratch_shapes=[pltpu.SemaphoreType.DMA((n, 4)), pltpu.SemaphoreType.DMA((n, 4))])(*slabs)


def _chip_copies(ins, outs, send_sems, recv_sems):
    x, y, c = _mesh_pos()
    starts, waits = [], []
    for k in range(len(ins)):
        for r, chip in enumerate(_other_chips(x, y)):
            cp = _remote(ins[k].at[r], outs[k].at[r], send_sems, recv_sems, k, r, (*chip, c))
            starts.append(cp.start)
            waits += [cp.wait_recv, cp.wait_send]
    return starts, waits


def _pair_sum(slab_idx, grad, from_sibling):
    _, rows, cols = grad.shape
    tr = _shard_tile(rows)

    def body(idx_ref, g_ref, s_ref, o_ref):
        o_ref[...] = (g_ref[...] + s_ref[...].astype(F32)).astype(o_ref.dtype)

    gs = pltpu.PrefetchScalarGridSpec(
        num_scalar_prefetch=1, grid=(3, rows // tr),
        in_specs=[pl.BlockSpec((None, tr, cols), lambda r, i, idx: (idx[r], i, 0)),
                  pl.BlockSpec((None, tr, cols), lambda r, i, idx: (r, i, 0))],
        out_specs=pl.BlockSpec((None, tr, cols), lambda r, i, idx: (r, i, 0)))
    return pl.pallas_call(body, name="rs_pair_sum", grid_spec=gs, out_shape=jax.ShapeDtypeStruct((3, rows, cols), BF16),
                          compiler_params=_params(2))(slab_idx, grad, from_sibling)


def _shard_tile(rows):
    for tr in range(min(rows, 352), 0, -1):
        if rows % tr == 0 and (tr % 16 == 0 or tr == rows):
            return tr


def _adamw_math(w, g, m, v):
    m = ADAM_B1 * m + (1.0 - ADAM_B1) * g
    v = ADAM_B2 * v + (1.0 - ADAM_B2) * (g * g)
    m_hat = m / (1.0 - ADAM_B1 ** ADAM_STEP)
    v_hat = v / (1.0 - ADAM_B2 ** ADAM_STEP)
    delta = -ADAM_LR * (m_hat / (jnp.sqrt(v_hat) + ADAM_EPS) + ADAM_WD * w)
    return delta, m, v


def _adamw_shard(me, grad, received, w, m, v):
    rows, cols = w.shape
    tr = _shard_tile(rows)
    n_rec = len(received)

    def body(me_ref, g_ref, *rest):
        r_refs, (w_ref, m_ref, v_ref, go_ref, do_ref, mo_ref, vo_ref) = rest[:n_rec], rest[n_rec:]
        g = g_ref[...]
        for r_ref in r_refs:
            for s in range(r_ref.shape[0]):
                g = g + r_ref[s].astype(F32)
        delta, m_new, v_new = _adamw_math(w_ref[...], g, m_ref[...], v_ref[...])
        go_ref[...] = g
        do_ref[...] = delta
        mo_ref[...] = m_new
        vo_ref[...] = v_new

    flat = pl.BlockSpec((tr, cols), lambda i, me: (i, 0))
    gs = pltpu.PrefetchScalarGridSpec(
        num_scalar_prefetch=1, grid=(rows // tr,),
        in_specs=[pl.BlockSpec((None, tr, cols), lambda i, me: (me[0], i, 0))]
        + [pl.BlockSpec((r.shape[0], tr, cols), lambda i, me: (0, i, 0)) for r in received] + [flat, flat, flat],
        out_specs=[flat, flat, flat, flat])
    out = jax.ShapeDtypeStruct((rows, cols), F32)
    return pl.pallas_call(body, name="adamw_shard", grid_spec=gs, out_shape=(out, out, out, out),
                          compiler_params=_params(1))(me, grad, *received, w, m, v)


def _small_reduce_adamw(slabs, w, m, v):
    _, rows, _ = slabs.shape

    def body(s_ref, w_ref, m_ref, v_ref, g_ref, d_ref, mo_ref, vo_ref, land, send_sems, recv_sems):
        x, y, c = _mesh_pos()
        me = _dev(x, y, c)
        copies = []
        for mask in range(1, N_DEV):
            px, py, pc = x ^ (mask >> 2), y ^ ((mask >> 1) & 1), c ^ (mask & 1)
            copies.append(pltpu.make_async_remote_copy(
                src_ref=s_ref.at[_dev(px, py, pc)], dst_ref=land.at[me], send_sem=send_sems.at[mask - 1],
                recv_sem=recv_sems.at[mask - 1], device_id=(px, py, pc), device_id_type=MESH))
        for cp in copies:
            cp.start()
        land[me] = s_ref[me]
        for mask in range(1, N_DEV):
            px, py, pc = x ^ (mask >> 2), y ^ ((mask >> 1) & 1), c ^ (mask & 1)
            pltpu.make_async_remote_copy(
                src_ref=s_ref.at[me], dst_ref=land.at[_dev(px, py, pc)], send_sem=send_sems.at[mask - 1],
                recv_sem=recv_sems.at[mask - 1], device_id=(px, py, pc), device_id_type=MESH).wait_recv()
        for cp in copies:
            cp.wait_send()
        g = land[0]
        for d in range(1, N_DEV):
            g = g + land[d]
        delta, m_new, v_new = _adamw_math(w_ref[...], g, m_ref[...], v_ref[...])
        g_ref[...] = g
        d_ref[...] = delta
        mo_ref[...] = m_new
        vo_ref[...] = v_new

    out = jax.ShapeDtypeStruct((rows, LANE), F32)
    return pl.pallas_call(
        body, name="comm_small_reduce_adamw", in_specs=[VMEM_WHOLE] * 4, out_specs=[VMEM_WHOLE] * 4, out_shape=(out, out, out, out),
        scratch_shapes=[pltpu.VMEM((N_DEV, rows, LANE), F32), pltpu.SemaphoreType.DMA((N_DEV - 1,)),
                        pltpu.SemaphoreType.DMA((N_DEV - 1,))],
    )(slabs, w, m, v)


def _cast_bf16(arrs):
    n = len(arrs)

    def body(*refs):
        for i_ref, o_ref in zip(refs[:n], refs[n:]):
            o_ref[...] = i_ref[...].astype(BF16)

    return pl.pallas_call(body, name="cast_bf16", in_specs=[VMEM_WHOLE] * n, out_specs=[VMEM_WHOLE] * n,
                          out_shape=[jax.ShapeDtypeStruct(a.shape, BF16) for a in arrs],
                          compiler_params=pltpu.CompilerParams(vmem_limit_bytes=VMEM_LIMIT))(*arrs)


REPLICATED = ("pre_mix_g", "gate_b", "dw_b", "conv_ln_g", "conv_ln_b", "post_mix_g", "pre_ffn_g", "post_ffn_g")
SHARDED = ("w_in", "w_conv_out", "w_attn_out", "w_o", "w_ffn_in", "w_ffn_out")
WEIGHTS = ("meta_tokens", "pre_mix_g", "w_in", "gate_b", "dw_w", "dw_b", "conv_ln_g", "conv_ln_b", "w_conv_out",
           "w_attn_out", "w_o", "post_mix_g", "pre_ffn_g", "w_ffn_in", "w_ffn_out", "post_ffn_g")


def kernel(x, meta_tokens, pre_mix_g, w_in, gate_b, dw_w, dw_b, conv_ln_g, conv_ln_b, w_conv_out, w_attn_out, w_o, post_mix_g, pre_ffn_g, w_ffn_in, w_ffn_out, post_ffn_g, loss_target, m_meta_tokens, m_pre_mix_g, m_w_in, m_gate_b, m_dw_w, m_dw_b, m_conv_ln_g, m_conv_ln_b, m_w_conv_out, m_w_attn_out, m_w_o, m_post_mix_g, m_pre_ffn_g, m_w_ffn_in, m_w_ffn_out, m_post_ffn_g, v_meta_tokens, v_pre_mix_g, v_w_in, v_gate_b, v_dw_w, v_dw_b, v_conv_ln_g, v_conv_ln_b, v_w_conv_out, v_w_attn_out, v_w_o, v_post_mix_g, v_pre_ffn_g, v_w_ffn_in, v_w_ffn_out, v_post_ffn_g):
    given = dict(locals())
    seq, d = x.shape[1], x.shape[2]
    n_meta = meta_tokens.shape[0]
    length = n_meta + seq
    m_rows = -(-length // QB) * QB
    dc = d // N_DEV
    assert dc == LANE and n_meta % 8 == 0 and seq % 8 == 0
    fs = w_ffn_in.shape[2]
    fr = w_ffn_out.shape[1]
    assert 2 * fr == fs

    local = {k: given[k][0] for k in SHARDED}
    cast = _cast_bf16([local[k] for k in SHARDED])
    dww_pad = jnp.pad(dw_w[0], ((0, CONV_PAD - CONV_WIDTH), (0, 0)))
    wi, meta_g, dww_g = _all_gather([cast[0], meta_tokens, dww_pad])
    meta_full = jnp.concatenate([meta_g[j] for j in range(N_DEV)], axis=1)
    dww_full = jnp.concatenate([dww_g[j] for j in range(N_DEV)], axis=1)
    ns = wi.shape[2]

    tail = jnp.zeros((m_rows - length, d), F32)
    h0 = jnp.concatenate([meta_full, x[0], tail], axis=0)
    target = jnp.concatenate([jnp.zeros((n_meta, d), F32), loss_target[0], tail], axis=0)

    (u,) = _rows("pre_mix_norm", lambda r0, xs, ps: ([_rms(xs[0], ps[0])], []), [h0], [pre_mix_g], [BF16], [])
    p, p16 = _matmul("in_proj", NN, u, wi, pl.BlockSpec((m_rows, d), lambda i: (0, 0)), pl.BlockSpec((None, d, ns), lambda i: (i, 0, 0)),
                     pl.BlockSpec((m_rows, ns), lambda i: (0, i)), jax.ShapeDtypeStruct((m_rows, N_DEV * ns), F32), (N_DEV,),
                     twin_bf16=True)
    o, tot, gathered = _attn_fwd(p16, d, list(cast[1:]))
    wco, wao, wo = (g.reshape(d, d) for g in gathered[0:3])
    wfi = gathered[3]
    wfo = gathered[4].reshape(N_DEV // 2, fs, d)
    y = _conv_fwd(p, dww_full, dw_b, d)
    (yc,) = _rows("conv_norm", lambda r0, xs, ps: ([_ln_silu(xs[0], ps[0], ps[1])], []), [y], [conv_ln_g, conv_ln_b], [BF16], [])
    y_conv = _dense_fwd("conv_out", yc, wco)
    y_attn = _dense_fwd("attn_out", o, wao)
    gate_cols = [(d, 5), (d, 6), None, None]
    (mixin,) = _rows("gate_mix", lambda r0, xs, ps: ([_gate_mix(*xs, ps[0])], []), [p, p, y_conv, y_attn], [gate_b], [BF16], [],
                     row_in_cols=gate_cols, row_out_widths=[d])
    mix = _dense_fwd("mix_out", mixin, wo)
    h1, u2 = _rows("post_mix", lambda r0, xs, ps: (list(_post_mix(xs[0], xs[1], ps[0], ps[1])), []), [h0, mix],
                   [post_mix_g, pre_ffn_g], [F32, BF16], [])
    ab = _matmul("ffn_in", NN, u2, wfi, pl.BlockSpec((m_rows, d), lambda i: (0, 0)), pl.BlockSpec((None, d, fs), lambda i: (i, 0, 0)),
                 pl.BlockSpec((None, m_rows, fs), lambda i: (i, 0, 0)), jax.ShapeDtypeStruct((N_DEV, m_rows, fs), BF16), (N_DEV,))
    half = N_DEV // 2
    tm = _row_tile(m_rows)
    pair = lambda off: pl.BlockSpec((None, tm, fs), lambda j, i, off=off: (j + off, i, 0))
    (f_in,) = _rowwise("swiglu", lambda r0, xs, ps: ([_swiglu(xs[0].astype(F32), xs[1].astype(F32))], []), [ab, ab], [],
                       [jax.ShapeDtypeStruct((half, m_rows, fs), BF16)], [], grid=(half, m_rows // tm),
                       in_specs=[pair(0), pair(half)], out_specs=[pair(0)], tm=tm, row_axis=1)
    f = _matmul("ffn_out", NN, f_in, wfo, pl.BlockSpec((None, m_rows, fs), lambda j: (j, 0, 0)), pl.BlockSpec((None, fs, d), lambda j: (j, 0, 0)),
                pl.BlockSpec((m_rows, d), lambda j: (0, 0)), jax.ShapeDtypeStruct((m_rows, d), F32), (half,), acc_axis=0)

    def loss_head(r0, xs, ps):
        h1_, f_, t_ = xs
        r, vjp = jax.vjp(_rms, f_, ps[0])
        rows = r0 + lax.broadcasted_iota(I32, (h1_.shape[0], 1), 0)
        real = (rows >= n_meta) & (rows < length)
        err = jnp.where(real, h1_ + r - t_, 0.0)
        dh2 = err * (1.0 / d)
        d_f, dg = vjp(dh2)
        part = jnp.sum(0.5 * jnp.mean(err * err, axis=-1, keepdims=True), axis=0, keepdims=True)
        return [d_f, dh2], [dg, jnp.broadcast_to(part, (1, LANE))]

    d_f, dh2, g_post_ffn, loss_part = _rows("loss_head", loss_head, [h1, f, target], [post_ffn_g], [BF16, F32], [d, LANE])

    d_fin = _matmul("ffn_out_dx", NT, d_f, wfo, pl.BlockSpec((m_rows, d), lambda j: (0, 0)), pl.BlockSpec((None, fs, d), lambda j: (j, 0, 0)),
                    pl.BlockSpec((None, m_rows, fs), lambda j: (j, 0, 0)), jax.ShapeDtypeStruct((half, m_rows, fs), BF16), (half,))
    g_wfo = _matmul("ffn_out_dw", TN, f_in, d_f, pl.BlockSpec((None, m_rows, fs), lambda j: (j, 0, 0)), pl.BlockSpec((m_rows, d), lambda j: (0, 0)),
                    pl.BlockSpec((None, fs, d), lambda j: (j, 0, 0)), jax.ShapeDtypeStruct((half, fs, d), F32), (half,), twin_bf16=True)

    def swiglu_bwd(r0, xs, ps):
        _, vjp = jax.vjp(_swiglu, xs[0].astype(F32), xs[1].astype(F32))
        return [vjp(xs[2].astype(F32))], []

    (d_ab,) = _rowwise("swiglu_bwd", swiglu_bwd, [ab, ab, d_fin], [],
                       [jax.ShapeDtypeStruct((2, half, m_rows, fs), BF16)], [], grid=(half, m_rows // tm),
                       in_specs=[pair(0), pair(half), pair(0)],
                       out_specs=[pl.BlockSpec((2, None, tm, fs), lambda j, i: (0, j, i, 0))], tm=tm, row_axis=1)
    d_ab = d_ab.reshape(N_DEV, m_rows, fs)
    du2 = _matmul("ffn_in_dx", NT, d_ab, wfi, pl.BlockSpec((None, m_rows, fs), lambda i: (i, 0, 0)), pl.BlockSpec((None, d, fs), lambda i: (i, 0, 0)),
                  pl.BlockSpec((m_rows, d), lambda i: (0, 0)), jax.ShapeDtypeStruct((m_rows, d), F32), (N_DEV,), acc_axis=0)
    g_wfi = _matmul("ffn_in_dw", TN, u2, d_ab, pl.BlockSpec((m_rows, d), lambda i: (0, 0)), pl.BlockSpec((None, m_rows, fs), lambda i: (i, 0, 0)),
                    pl.BlockSpec((None, d, fs), lambda i: (i, 0, 0)), jax.ShapeDtypeStruct((N_DEV, d, fs), F32), (N_DEV,), twin_bf16=True)

    def post_mix_bwd(r0, xs, ps):
        h0_, mix_, dh2_, du2_ = xs
        _, vjp = jax.vjp(_post_mix, h0_, mix_, ps[0], ps[1])
        dh0_, dmix_, dg1, dg2 = vjp((dh2_, du2_))
        return [dmix_, dh0_], [dg1, dg2]

    d_mix, dh1, g_post_mix, g_pre_ffn = _rows("post_mix_bwd", post_mix_bwd, [h0, mix, dh2, du2], [post_mix_g, pre_ffn_g],
                                              [BF16, F32], [d, d])
    d_mixin = _dense_dx("mix_out_dx", d_mix, wo, F32)
    g_wo = _dense_dw("mix_out_dw", mixin, d_mix)

    def gate_mix_bwd(r0, xs, ps):
        _, vjp = jax.vjp(_gate_mix, xs[0], xs[1], xs[2], xs[3], ps[0])
        dpgc, dpga, dyc_, dya_, dgb = vjp(xs[4])
        return [dpgc, dpga, dyc_, dya_], [dgb]

    dp_gc, dp_ga, d_yconv, d_yattn, g_gate_b = _rows(
        "gate_mix_bwd", gate_mix_bwd, [p, p, y_conv, y_attn, d_mixin], [gate_b], [BF16] * 4, [2 * d],
        row_in_cols=gate_cols + [None], row_out_widths=[d] * 4)
    d_o = _dense_dx("attn_out_dx", d_yattn, wao, BF16)
    g_wao = _dense_dw("attn_out_dw", o, d_yattn)
    d_yc = _dense_dx("conv_out_dx", d_yconv, wco, F32)
    g_wco = _dense_dw("conv_out_dw", yc, d_yconv)
    big = {"w_ffn_out": [g.reshape(N_DEV, fr, d) for g in g_wfo], "w_ffn_in": g_wfi,
           "w_o": [g.reshape(N_DEV, dc, d) for g in g_wo], "w_attn_out": [g.reshape(N_DEV, dc, d) for g in g_wao],
           "w_conv_out": [g.reshape(N_DEV, dc, d) for g in g_wco]}
    early = ("w_ffn_out", "w_ffn_in", "w_o", "w_attn_out", "w_conv_out")
    dq, dk, dv, received_early = _attn_bwd(p16, d_o, tot, d, [big[k][1] for k in early])

    def conv_norm_bwd(r0, xs, ps):
        _, vjp = jax.vjp(_ln_silu, xs[0], ps[0], ps[1])
        dy_, dg, db = vjp(xs[1])
        return [dy_], [dg, db]

    d_y, g_ln_g, g_ln_b = _rows("conv_norm_bwd", conv_norm_bwd, [y, d_yc], [conv_ln_g, conv_ln_b], [F32], [d, d])
    dp_a, dp_g, g_dww, g_dwb = _conv_bwd(p, d_y, dww_full, d)
    dp = jnp.concatenate([dp_a, dp_g, dq, dk, dv, dp_gc, dp_ga], axis=1)
    g_wi = _matmul("in_proj_dw", TN, u, dp, pl.BlockSpec((m_rows, d), lambda i: (0, 0)), pl.BlockSpec((m_rows, ns), lambda i: (0, i)),
                   pl.BlockSpec((None, d, ns), lambda i: (i, 0, 0)), jax.ShapeDtypeStruct((N_DEV, d, ns), F32), (N_DEV,), twin_bf16=True)
    x_i, y_i, c_i = _mesh_pos()
    slab_idx = jnp.stack([_dev(*chip, c_i) for chip in _other_chips(x_i, y_i)]).astype(I32)
    (wi_sibling,) = _sibling_exchange([g_wi[1]])
    wi_pairs = _pair_sum(slab_idx, g_wi[0], wi_sibling)
    du, wi_chips = _matmul(
        "in_proj_dx", NT, dp, wi, pl.BlockSpec((m_rows, ns), lambda i: (0, i)), pl.BlockSpec((None, d, ns), lambda i: (i, 0, 0)),
        pl.BlockSpec((m_rows, d), lambda i: (0, 0)), jax.ShapeDtypeStruct((m_rows, d), F32), (N_DEV,), acc_axis=0,
        carried=(_chip_copies, [wi_pairs], [jax.ShapeDtypeStruct(wi_pairs.shape, BF16)],
                 [pltpu.SemaphoreType.DMA((1, 3)), pltpu.SemaphoreType.DMA((1, 3))]))

    def pre_mix_bwd(r0, xs, ps):
        _, vjp = jax.vjp(_rms, xs[0], ps[0])
        dx, dg = vjp(xs[1])
        return [xs[2] + dx], [dg]

    dh0, g_pre_mix = _rows("pre_mix_bwd", pre_mix_bwd, [h0, du, dh1], [pre_mix_g], [F32], [d])
    grad_x = dh0[n_meta:length][None]

    me = _dev(x_i, y_i, c_i)
    me_arr = jnp.reshape(me, (1,)).astype(I32)
    big["w_in"] = g_wi
    received = {k: [r] for k, r in zip(early, received_early)}
    received["w_in"] = [wi_sibling[3:4], wi_chips]
    results = {}
    for k in SHARDED:
        outs = _adamw_shard(me_arr, big[k][0], received[k], local[k], given["m_" + k][0], given["v_" + k][0])
        results[k] = tuple(a[None] for a in outs)

    rep_grads = {"pre_mix_g": g_pre_mix, "gate_b": g_gate_b, "dw_b": g_dwb, "conv_ln_g": g_ln_g, "conv_ln_b": g_ln_b,
                 "post_mix_g": g_post_mix, "pre_ffn_g": g_pre_ffn, "post_ffn_g": g_post_ffn}

    def pack_rep(get):
        return jnp.concatenate([get(k) for k in REPLICATED], axis=1).reshape(-1, LANE)

    rep_rows = pack_rep(lambda k: rep_grads[k])
    n_rep = rep_rows.shape[0]
    loss_rows = jnp.broadcast_to(loss_part, (8, LANE))
    g_meta = dh0[0:n_meta]
    slabs = jnp.stack([jnp.concatenate([rep_rows, loss_rows, g_dww[:, j * LANE:(j + 1) * LANE], g_meta[:, j * LANE:(j + 1) * LANE]], axis=0)
                       for j in range(N_DEV)])

    def pack_small(prefix):
        dww_own = jnp.pad(given[prefix + "dw_w"][0], ((0, CONV_PAD - CONV_WIDTH), (0, 0)))
        return jnp.concatenate([pack_rep(lambda k: given[prefix + k]), jnp.zeros((8, LANE), F32), dww_own,
                                given[prefix + "meta_tokens"]], axis=0)

    small = _small_reduce_adamw(slabs, pack_small(""), pack_small("m_"), pack_small("v_"))
    loss = small[0][n_rep, 0]

    def unpack(arr):
        out = {}
        flat = arr[:n_rep].reshape(1, -1)
        off = 0
        for k in REPLICATED:
            w = given[k].shape[1]
            out[k] = flat[:, off:off + w]
            off += w
        out["dw_w"] = arr[n_rep + 8:n_rep + 8 + CONV_WIDTH][None]
        out["meta_tokens"] = arr[n_rep + 8 + CONV_PAD:n_rep + 8 + CONV_PAD + n_meta]
        return out

    small_out = [unpack(a) for a in small]
    for k in WEIGHTS:
        if k not in results:
            results[k] = tuple(s[k] for s in small_out)
    return (loss, grad_x, *[results[k][0] for k in WEIGHTS], *[results[k][1] for k in WEIGHTS],
            *[results[k][2] for k in WEIGHTS], *[results[k][3] for k in WEIGHTS])
```

```python
import jax
import jax.numpy as jnp
from jax import lax
from jax.experimental import pallas as pl
from jax.experimental.pallas import tpu as pltpu

F32 = jnp.float32
BF16 = jnp.bfloat16
I32 = jnp.int32

N_DEV = 8
LANE = 128
HEAD_DIM = 64
QB = 128
KEY_SHIFT = 2
KEY_TILES = 1 << KEY_SHIFT
KEY_CHUNK = KEY_TILES * QB
LANE_BLOCKS = 4
BWD_LANE_BLOCKS = 4
CONV_WIDTH = 31
CONV_PAD = 32
ROW_CHUNK = 128
RMS_EPS = 1e-6
LN_EPS = 1e-5
ADAM_LR = 0.001
ADAM_B1 = 0.9
ADAM_B2 = 0.999
ADAM_EPS = 1e-08
ADAM_WD = 0.01
ADAM_STEP = 10
VMEM_LIMIT = 56 * 1024 * 1024

NN = (((1,), (0,)), ((), ()))
NT = (((1,), (1,)), ((), ()))
TN = (((0,), (0,)), ((), ()))
MESH = pl.DeviceIdType.MESH
ANY = pl.BlockSpec(memory_space=pl.ANY)
VMEM_WHOLE = pl.BlockSpec(memory_space=pltpu.VMEM)


def _params(n_axes):
    return pltpu.CompilerParams(dimension_semantics=("arbitrary",) * n_axes, vmem_limit_bytes=VMEM_LIMIT)


def _row_tile(m):
    assert m % QB == 0
    return m // 4 if m % 64 == 0 else QB


def _matmul(name, dims, a, b, a_spec, b_spec, o_spec, out_shape, grid, acc_axis=None, twin_bf16=False, carried=None):
    n_car = 0 if carried is None else len(carried[1])

    def body(a_ref, b_ref, *rest):
        car_ins, o_ref, rest = rest[:n_car], rest[n_car], rest[n_car + 1:]
        if carried is not None:
            starts, waits = carried[0](car_ins, rest[:n_car], *rest[n_car:])

            @pl.when(pl.program_id(0) == 0)
            def _():
                for f in starts:
                    f()

        r = lax.dot_general(a_ref[...], b_ref[...], dims, preferred_element_type=F32)
        if acc_axis is None:
            o_ref[...] = r.astype(o_ref.dtype)
            if twin_bf16:
                rest[0][...] = r.astype(BF16)
        else:
            k = pl.program_id(acc_axis)

            @pl.when(k == 0)
            def _():
                o_ref[...] = r

            @pl.when(k > 0)
            def _():
                o_ref[...] += r

        if carried is not None:
            @pl.when(pl.program_id(0) == grid[0] - 1)
            def _():
                for f in waits:
                    f()

    in_specs, out_specs, out_shapes, scratch = [a_spec, b_spec], [o_spec], [out_shape], []
    if twin_bf16:
        assert acc_axis is None and carried is None
        out_specs.append(o_spec)
        out_shapes.append(jax.ShapeDtypeStruct(out_shape.shape, BF16))
    if carried is not None:
        assert len(grid) == 1
        in_specs += [ANY] * n_car
        out_specs += [ANY] * n_car
        out_shapes += list(carried[2])
        scratch = list(carried[3])
    outs = pl.pallas_call(body, name=name, grid=grid, in_specs=in_specs, out_specs=out_specs, out_shape=out_shapes,
                          scratch_shapes=scratch, compiler_params=_params(len(grid)))(a, b, *(carried[1] if carried else ()))
    return outs[0] if len(outs) == 1 else outs


DENSE_TILE = 256


def _dense_fwd(name, a, w, out_dtype=F32):
    m, k = a.shape
    n = w.shape[1]
    tn = DENSE_TILE
    return _matmul(name, NN, a, w, pl.BlockSpec((m, k), lambda j: (0, 0)), pl.BlockSpec((k, tn), lambda j: (0, j)),
                   pl.BlockSpec((m, tn), lambda j: (0, j)), jax.ShapeDtypeStruct((m, n), out_dtype), (n // tn,))


def _dense_dx(name, dy, w, out_dtype):
    m, n = dy.shape
    k = w.shape[0]
    tk = DENSE_TILE
    return _matmul(name, NT, dy, w, pl.BlockSpec((m, n), lambda j: (0, 0)), pl.BlockSpec((tk, n), lambda j: (j, 0)),
                   pl.BlockSpec((m, tk), lambda j: (0, j)), jax.ShapeDtypeStruct((m, k), out_dtype), (k // tk,))


def _dense_dw(name, a, dy):
    m, k = a.shape
    n = dy.shape[1]
    tn = DENSE_TILE
    return _matmul(name, TN, a, dy, pl.BlockSpec((m, k), lambda j: (0, 0)), pl.BlockSpec((m, tn), lambda j: (0, j)),
                   pl.BlockSpec((k, tn), lambda j: (0, j)), jax.ShapeDtypeStruct((k, n), F32), (n // tn,), twin_bf16=True)


def _rowwise(name, fn, row_ins, par_ins, row_outs, par_outs, *, grid, in_specs, out_specs, tm, row_axis):
    n_ri, n_pi, n_ro, n_po = len(row_ins), len(par_ins), len(row_outs), len(par_outs)
    n_steps, tail = divmod(tm, ROW_CHUNK)
    assert tail % 16 == 0

    def body(*refs):
        ri = refs[:n_ri]
        pi = refs[n_ri:n_ri + n_pi]
        ro = refs[n_ri + n_pi:n_ri + n_pi + n_ro]
        po = refs[n_ri + n_pi + n_ro:]
        ps = [r[...] for r in pi]
        base = pl.program_id(row_axis) * tm

        def chunk(r0, rows, carry):
            xs = [r[pl.ds(r0, rows), :] for r in ri]
            outs, pouts = fn(base + r0, xs, ps)
            for r, o in zip(ro, outs):
                if isinstance(o, (list, tuple)):
                    for j, part in enumerate(o):
                        r[j, pl.ds(r0, rows), :] = part.astype(r.dtype)
                else:
                    r[pl.ds(r0, rows), :] = o.astype(r.dtype)
            return tuple(c + q for c, q in zip(carry, pouts))

        def step(i, carry):
            return chunk(pl.multiple_of(i * ROW_CHUNK, ROW_CHUNK), ROW_CHUNK, carry)

        acc = lax.fori_loop(0, n_steps, step, tuple(jnp.zeros(s.shape, F32) for s in par_outs))
        if tail:
            acc = chunk(n_steps * ROW_CHUNK, tail, acc)
        if n_po:
            first = pl.program_id(0) == 0
            for ax in range(1, len(grid)):
                first = first & (pl.program_id(ax) == 0)

            @pl.when(first)
            def _():
                for r in po:
                    r[...] = jnp.zeros_like(r)

            for r, a in zip(po, acc):
                r[...] += a

    return pl.pallas_call(body, name=name, grid=grid, in_specs=in_specs, out_specs=out_specs,
                          out_shape=tuple(row_outs) + tuple(par_outs),
                          compiler_params=_params(len(grid)))(*row_ins, *par_ins)


def _rows(name, fn, row_ins, par_ins, row_out_dtypes, par_out_widths, row_in_cols=None, row_out_widths=None):
    m = row_ins[0].shape[0]
    tm = _row_tile(m)
    in_specs = []
    for k, a in enumerate(row_ins):
        if row_in_cols is not None and row_in_cols[k] is not None:
            width, cb = row_in_cols[k]
            in_specs.append(pl.BlockSpec((tm, width), lambda i, cb=cb: (i, cb)))
        else:
            in_specs.append(pl.BlockSpec((tm, a.shape[1]), lambda i: (i, 0)))
    for a in par_ins:
        in_specs.append(pl.BlockSpec(a.shape, lambda i: (0, 0)))
    if row_out_widths is None:
        row_out_widths = [row_ins[0].shape[1]] * len(row_out_dtypes)
    row_outs = [jax.ShapeDtypeStruct((m, w), dt) for w, dt in zip(row_out_widths, row_out_dtypes)]
    par_outs = [jax.ShapeDtypeStruct((1, w), F32) for w in par_out_widths]
    out_specs = [pl.BlockSpec((tm, s.shape[1]), lambda i: (i, 0)) for s in row_outs]
    out_specs += [pl.BlockSpec(s.shape, lambda i: (0, 0)) for s in par_outs]
    return _rowwise(name, fn, row_ins, par_ins, row_outs, par_outs, grid=(m // tm,), in_specs=in_specs,
                    out_specs=out_specs, tm=tm, row_axis=0)


def _rms(x, g):
    return x * lax.rsqrt(jnp.mean(x * x, axis=-1, keepdims=True) + RMS_EPS) * g


def _ln_silu(y, g, b):
    mu = jnp.mean(y, axis=-1, keepdims=True)
    yc = y - mu
    var = jnp.mean(yc * yc, axis=-1, keepdims=True)
    return jax.nn.silu(yc * lax.rsqrt(var + LN_EPS) * g + b)


def _gate_mix(pgc, pga, yc, ya, gb):
    d = pgc.shape[1]
    return jax.nn.sigmoid(pgc + gb[:, :d]) * yc + jax.nn.sigmoid(pga + gb[:, d:]) * ya


def _post_mix(h0, mix, g_post, g_pre):
    h1 = h0 + _rms(mix, g_post)
    return h1, _rms(h1, g_pre)


def _swiglu(a, b):
    return jax.nn.silu(a) * b


def _conv_taps():
    taps = []
    for b in range(8):
        for a in range(CONV_PAD // 8):
            s = 8 * a + b
            if s < CONV_WIDTH:
                taps.append((b, a, CONV_WIDTH - 1 - s))
    return taps


def _conv_fwd(p, dww, dwb, d_model):
    m = p.shape[0]
    nch = d_model // LANE
    n_chunk = m // QB
    taps = _conv_taps()

    def body(a_ref, g_ref, w_ref, b_ref, y_ref, upad):
        upad[0:CONV_PAD, :] = jnp.zeros((CONV_PAD, LANE), F32)

        def fill(i, c):
            r0 = pl.multiple_of(i * QB, QB)
            u = a_ref[pl.ds(r0, QB), :] * jax.nn.sigmoid(g_ref[pl.ds(r0, QB), :])
            upad[pl.ds(pl.multiple_of(r0 + CONV_PAD, 8), QB), :] = u
            return c

        lax.fori_loop(0, n_chunk, fill, 0)

        def conv(i, c):
            r0 = pl.multiple_of(i * QB, QB)
            win = upad[pl.ds(r0, QB + CONV_PAD), :]
            acc = jnp.broadcast_to(b_ref[...], (QB, LANE))
            rolled = {}
            for b, a, j in taps:
                if b not in rolled:
                    rolled[b] = win if b == 0 else pltpu.roll(win, b, axis=0)
                lo = CONV_PAD - 8 * a
                acc = acc + w_ref[j:j + 1, :] * rolled[b][lo:lo + QB, :]
            y_ref[pl.ds(r0, QB), :] = acc
            return c

        lax.fori_loop(0, n_chunk, conv, 0)

    col = lambda off: pl.BlockSpec((m, LANE), lambda c: (0, off + c))
    return pl.pallas_call(
        body, name="conv_fwd", grid=(nch,),
        in_specs=[col(0), col(nch), pl.BlockSpec((CONV_PAD, LANE), lambda c: (0, c)), pl.BlockSpec((1, LANE), lambda c: (0, c))],
        out_specs=col(0), out_shape=jax.ShapeDtypeStruct((m, d_model), F32),
        scratch_shapes=[pltpu.VMEM((m + CONV_PAD, LANE), F32)], compiler_params=_params(1))(p, p, dww, dwb)


def _conv_bwd(p, dy, dww, d_model):
    m = p.shape[0]
    nch = d_model // LANE
    n_chunk = m // QB
    taps = _conv_taps()
    win_rows = QB + CONV_PAD

    def body(a_ref, g_ref, dy_ref, w_ref, da_ref, dg_ref, dw_ref, db_ref, upad, dypad, wacc, bacc):
        upad[0:CONV_PAD, :] = jnp.zeros((CONV_PAD, LANE), F32)
        dypad[m:m + CONV_PAD, :] = jnp.zeros((CONV_PAD, LANE), F32)
        wacc[...] = jnp.zeros_like(wacc)
        bacc[...] = jnp.zeros_like(bacc)

        def fill(i, c):
            r0 = pl.multiple_of(i * QB, QB)
            u = a_ref[pl.ds(r0, QB), :] * jax.nn.sigmoid(g_ref[pl.ds(r0, QB), :])
            upad[pl.ds(pl.multiple_of(r0 + CONV_PAD, 8), QB), :] = u
            dypad[pl.ds(r0, QB), :] = dy_ref[pl.ds(r0, QB), :]
            return c

        lax.fori_loop(0, n_chunk, fill, 0)

        def chunk(i, c):
            r0 = pl.multiple_of(i * QB, QB)
            dwin = dypad[pl.ds(r0, win_rows), :]
            du = jnp.zeros((QB, LANE), F32)
            rolled = {}
            for b, a, j in taps:
                if b not in rolled:
                    rolled[b] = dwin if b == 0 else pltpu.roll(dwin, win_rows - b, axis=0)
                du = du + w_ref[j:j + 1, :] * rolled[b][8 * a:8 * a + QB, :]
            av = a_ref[pl.ds(r0, QB), :]
            sg = jax.nn.sigmoid(g_ref[pl.ds(r0, QB), :])
            da_ref[pl.ds(r0, QB), :] = (du * sg).astype(da_ref.dtype)
            dg_ref[pl.ds(r0, QB), :] = (du * av * sg * (1.0 - sg)).astype(dg_ref.dtype)
            dyc = dy_ref[pl.ds(r0, QB), :]
            uwin = upad[pl.ds(r0, win_rows), :]
            rolled = {}
            for b, a, j in taps:
                if b not in rolled:
                    rolled[b] = uwin if b == 0 else pltpu.roll(uwin, b, axis=0)
                lo = CONV_PAD - 8 * a
                prod = dyc * rolled[b][lo:lo + QB, :]
                wacc[j] += prod.reshape(QB // 8, 8, LANE).sum(axis=0)
            bacc[...] += dyc.reshape(QB // 8, 8, LANE).sum(axis=0)
            return c

        lax.fori_loop(0, n_chunk, chunk, 0)
        for j in range(CONV_WIDTH):
            dw_ref[j:j + 1, :] = jnp.sum(wacc[j], axis=0, keepdims=True)
        dw_ref[CONV_WIDTH:CONV_PAD, :] = jnp.zeros((CONV_PAD - CONV_WIDTH, LANE), F32)
        db_ref[...] = jnp.sum(bacc[...], axis=0, keepdims=True)

    col = lambda off: pl.BlockSpec((m, LANE), lambda c: (0, off + c))
    return pl.pallas_call(
        body, name="conv_bwd", grid=(nch,),
        in_specs=[col(0), col(nch), col(0), pl.BlockSpec((CONV_PAD, LANE), lambda c: (0, c))],
        out_specs=[col(0), col(0), pl.BlockSpec((CONV_PAD, LANE), lambda c: (0, c)), pl.BlockSpec((1, LANE), lambda c: (0, c))],
        out_shape=(jax.ShapeDtypeStruct((m, d_model), BF16), jax.ShapeDtypeStruct((m, d_model), BF16),
                   jax.ShapeDtypeStruct((CONV_PAD, d_model), F32), jax.ShapeDtypeStruct((1, d_model), F32)),
        scratch_shapes=[pltpu.VMEM((m + CONV_PAD, LANE), F32), pltpu.VMEM((m + CONV_PAD, LANE), F32),
                        pltpu.VMEM((CONV_PAD, 8, LANE), F32), pltpu.VMEM((8, LANE), F32)],
        compiler_params=_params(1))(p, p, dy, dww)


EXP_CLAMP = 80.0


def _one_plus_exp(z):
    return 1.0 + jnp.exp(jnp.minimum(z, EXP_CLAMP))


def _softplus(z):
    return jnp.maximum(jnp.log(_one_plus_exp(z)), z)


def _softplus_sigmoid(z):
    s = _one_plus_exp(z)
    return jnp.maximum(jnp.log(s), z), 1.0 - 1.0 / s


def _tile_cumsums(x, tri2):
    xb = x.astype(BF16)
    out = []
    for i in range(0, KEY_TILES, 2):
        both = jnp.dot(xb[:, i * QB:(i + 2) * QB], tri2, preferred_element_type=F32)
        out += [both[:, :QB], both[:, QB:]]
    return out


def _tri2(kind):
    jj = lax.broadcasted_iota(I32, (2 * QB, 2 * QB), 0)
    ss = lax.broadcasted_iota(I32, (2 * QB, 2 * QB), 1)
    same = (jj >= QB) == (ss >= QB)
    keep = {"ge": jj >= ss, "le": jj <= ss}[kind]
    return jnp.where(same & keep, 1.0, 0.0).astype(BF16)


def _attn_fwd(p, d_model, shards):
    m = p.shape[0]
    nqb = m // QB
    ngrp = d_model // (LANE_BLOCKS * LANE)
    qo, ko, vo = 2 * ngrp, 3 * ngrp, 4 * ngrp
    scale = HEAD_DIM ** -0.5
    n_sh = len(shards)

    assert nqb >= KEY_TILES

    def body(q_ref, k_ref, v_ref, *rest):
        shard_refs, (o_ref, t_ref), rest = rest[:n_sh], rest[n_sh:n_sh + 2], rest[n_sh + 2:]
        gathered_refs, (acc_ref, car_ref), sems = rest[:n_sh], rest[n_sh:n_sh + 2], rest[n_sh + 2:]
        starts, relays, waits = _gather_copies(shard_refs, gathered_refs, *sems)

        @pl.when((pl.program_id(0) == 0) & (pl.program_id(1) == 0))
        def _():
            for f in starts:
                f()

        @pl.when((pl.program_id(0) == ngrp - 1) & (pl.program_id(1) == (3 * nqb) // 4))
        def _():
            for f in relays:
                f()

        qb = pl.program_id(1)
        lane = lax.broadcasted_iota(I32, (QB, LANE), 1)
        head0 = lane < HEAD_DIM
        row_g = qb * QB + lax.broadcasted_iota(I32, (QB, KEY_CHUNK), 0)
        col_l = lax.broadcasted_iota(I32, (QB, KEY_CHUNK), 1)
        tri = _tri2("ge")
        heads = range(2 * LANE_BLOCKS)
        qh = []
        for lb in range(LANE_BLOCKS):
            q2 = (q_ref[:, lb * LANE:(lb + 1) * LANE] * scale).astype(BF16)
            zero = jnp.zeros_like(q2)
            qh += [jnp.where(head0, q2, zero), jnp.where(head0, zero, q2)]
        acc_ref[...] = jnp.zeros_like(acc_ref)
        car_ref[...] = jnp.zeros_like(car_ref)

        def chunk(first_tile, bound):
            r0 = pl.multiple_of(first_tile * QB, QB)
            kcs = [k_ref[pl.ds(r0, KEY_CHUNK), lb * LANE:(lb + 1) * LANE].astype(BF16) for lb in range(LANE_BLOCKS)]
            vcs = [v_ref[pl.ds(r0, KEY_CHUNK), lb * LANE:(lb + 1) * LANE].astype(BF16) for lb in range(LANE_BLOCKS)]
            valid = None if bound is None else (col_l + r0) < bound
            zs = [lax.dot_general(qh[h], kcs[h // 2], NT, preferred_element_type=F32) for h in heads]
            sps = [_softplus(z) for z in zs]
            if valid is not None:
                sps = [jnp.where(valid, sp, 0.0) for sp in sps]
            cums = [_tile_cumsums(sp, tri) for sp in sps]
            cars = [car_ref[h] for h in heads]
            a_tiles = [[None] * KEY_TILES for h in heads]
            for i in reversed(range(KEY_TILES)):
                for h in heads:
                    cum = cums[h][i]
                    a_tiles[h][i] = jnp.exp(zs[h][:, i * QB:(i + 1) * QB] - (cum + cars[h]))
                    cars[h] = cars[h] + jnp.broadcast_to(cum[:, 0:1], cum.shape)
            for h in heads:
                a = jnp.concatenate(a_tiles[h], axis=1)
                if valid is not None:
                    a = jnp.where(valid, a, 0.0)
                acc_ref[h] += jnp.dot(a.astype(BF16), vcs[h // 2], preferred_element_type=F32)
                car_ref[h] = cars[h]

        near = jnp.maximum(qb - (KEY_TILES - 1), 0)
        chunk(near, row_g)
        n_full = lax.shift_right_logical(near, KEY_SHIFT)

        def step(i, c):
            chunk(near - KEY_TILES * (i + 1), None)
            return c

        lax.fori_loop(0, n_full, step, 0)
        left = near - KEY_TILES * n_full

        @pl.when(left > 0)
        def _():
            chunk(0, left * QB)

        for lb in range(LANE_BLOCKS):
            o_ref[:, lb * LANE:(lb + 1) * LANE] = jnp.where(head0, acc_ref[2 * lb], acc_ref[2 * lb + 1]).astype(o_ref.dtype)
        for h in heads:
            t_ref[:, h * QB:(h + 1) * QB] = car_ref[h]

        @pl.when((pl.program_id(0) == ngrp - 1) & (pl.program_id(1) == nqb - 1))
        def _():
            for f in waits:
                f()

    wide = LANE_BLOCKS * LANE
    outs = pl.pallas_call(
        body, name="attn_fwd", grid=(ngrp, nqb),
        in_specs=[pl.BlockSpec((QB, wide), lambda g, qb: (qb, qo + g)),
                  pl.BlockSpec((m, wide), lambda g, qb: (0, ko + g)),
                  pl.BlockSpec((m, wide), lambda g, qb: (0, vo + g))] + [ANY] * n_sh,
        out_specs=[pl.BlockSpec((QB, wide), lambda g, qb: (qb, g)),
                   pl.BlockSpec((QB, 2 * wide), lambda g, qb: (qb, g))] + [ANY] * n_sh,
        out_shape=[jax.ShapeDtypeStruct((m, d_model), BF16), jax.ShapeDtypeStruct((m, 2 * d_model), F32)]
        + [jax.ShapeDtypeStruct((N_DEV,) + s.shape, s.dtype) for s in shards],
        scratch_shapes=[pltpu.VMEM((2 * LANE_BLOCKS, QB, LANE), F32), pltpu.VMEM((2 * LANE_BLOCKS, QB, LANE), F32)]
        + _exchange_sems(n_sh) + [pltpu.SemaphoreType.DMA((n_sh,))],
        compiler_params=_params(2))(p, p, p, *shards)
    return outs[0], outs[1], outs[2:]


def _attn_bwd(p, d_o, tot, d_model, slabs):
    m = p.shape[0]
    nqb = m // QB
    blocks = BWD_LANE_BLOCKS
    ngrp = d_model // (blocks * LANE)
    qo, ko, vo = 2 * ngrp, 3 * ngrp, 4 * ngrp
    scale = HEAD_DIM ** -0.5
    n_sl = len(slabs)

    assert nqb >= KEY_TILES

    def body(q_ref, k_ref, v_ref, do_ref, t_ref, *rest):
        slab_refs, (dq_ref, dk_ref, dv_ref), rest = rest[:n_sl], rest[n_sl:n_sl + 3], rest[n_sl + 3:]
        recv_refs, (dkacc, dvacc, dqacc, csp, cg), sems = rest[:n_sl], rest[n_sl:n_sl + 5], rest[n_sl + 5:]
        starts, waits = _scatter_copies(slab_refs, recv_refs, *sems)

        @pl.when((pl.program_id(0) == 0) & (pl.program_id(1) == 0))
        def _():
            for f in starts:
                f()

        qb = pl.program_id(1)
        lane = lax.broadcasted_iota(I32, (QB, LANE), 1)
        head0 = lane < HEAD_DIM
        row_g = qb * QB + lax.broadcasted_iota(I32, (QB, KEY_CHUNK), 0)
        col_l = lax.broadcasted_iota(I32, (QB, KEY_CHUNK), 1)
        tri_ge = _tri2("ge")
        tri_le = _tri2("le")
        heads = range(2 * blocks)
        qh, doh = [], []
        for lb in range(blocks):
            q2 = (q_ref[:, lb * LANE:(lb + 1) * LANE] * scale).astype(BF16)
            do2 = do_ref[:, lb * LANE:(lb + 1) * LANE]
            zero = jnp.zeros_like(q2)
            qh += [jnp.where(head0, q2, zero), jnp.where(head0, zero, q2)]
            doh += [jnp.where(head0, do2, zero), jnp.where(head0, zero, do2)]
        q_pairs = [jnp.concatenate(qh[2 * lb:2 * lb + 2], axis=0) for lb in range(blocks)]
        do_pairs = [jnp.concatenate(doh[2 * lb:2 * lb + 2], axis=0) for lb in range(blocks)]

        @pl.when(qb == 0)
        def _():
            dkacc[...] = jnp.zeros_like(dkacc)
            dvacc[...] = jnp.zeros_like(dvacc)

        dqacc[...] = jnp.zeros_like(dqacc)
        for h in heads:
            csp[h] = t_ref[:, h * QB:(h + 1) * QB]
        cg[...] = jnp.zeros_like(cg)

        def chunk(first_tile, bound):
            r0 = pl.multiple_of(first_tile * QB, QB)
            kcs = [k_ref[pl.ds(r0, KEY_CHUNK), lb * LANE:(lb + 1) * LANE].astype(BF16) for lb in range(blocks)]
            vcs = [v_ref[pl.ds(r0, KEY_CHUNK), lb * LANE:(lb + 1) * LANE].astype(BF16) for lb in range(blocks)]
            valid = None if bound is None else (col_l + r0) < bound
            tiles = [slice(i * QB, (i + 1) * QB) for i in range(KEY_TILES)]
            zs = [lax.dot_general(qh[h], kcs[h // 2], NT, preferred_element_type=F32) for h in heads]
            das = [lax.dot_general(doh[h], vcs[h // 2], NT, preferred_element_type=F32) for h in heads]
            sps, sgs = zip(*[_softplus_sigmoid(z) for z in zs])
            if valid is not None:
                sps = [jnp.where(valid, sp, 0.0) for sp in sps]
            cums = [_tile_cumsums(sp, tri_ge) for sp in sps]
            a_tiles, g_tiles = [[] for h in heads], [[] for h in heads]
            for h in heads:
                rest = csp[h]
                for i, c in enumerate(tiles):
                    cum = cums[h][i]
                    rest = rest - jnp.broadcast_to(cum[:, 0:1], cum.shape)
                    a = jnp.exp(zs[h][:, c] - (cum + rest))
                    if valid is not None:
                        a = jnp.where(valid[:, c], a, 0.0)
                    a_tiles[h].append(a)
                    g_tiles[h].append(a * das[h][:, c])
                csp[h] = rest
            gcums = [_tile_cumsums(jnp.concatenate(g_tiles[h], axis=1), tri_le) for h in heads]
            dzbs, abs_ = [], []
            for h in heads:
                g_before = cg[h]
                dz_tiles = []
                for i, c in enumerate(tiles):
                    gcum = gcums[h][i]
                    dz = g_tiles[h][i] - sgs[h][:, c] * (g_before + gcum)
                    if valid is not None:
                        dz = jnp.where(valid[:, c], dz, 0.0)
                    dz_tiles.append(dz)
                    g_before = g_before + jnp.broadcast_to(gcum[:, QB - 1:QB], gcum.shape)
                cg[h] = g_before
                dzbs.append(jnp.concatenate(dz_tiles, axis=1).astype(BF16))
                abs_.append(jnp.concatenate(a_tiles[h], axis=1).astype(BF16))
            for h in heads:
                dqacc[h] += jnp.dot(dzbs[h], kcs[h // 2], preferred_element_type=F32)
            for lb in range(blocks):
                dz_pair = jnp.concatenate(dzbs[2 * lb:2 * lb + 2], axis=0)
                a_pair = jnp.concatenate(abs_[2 * lb:2 * lb + 2], axis=0)
                dkacc[pl.ds(r0, KEY_CHUNK), lb * LANE:(lb + 1) * LANE] += lax.dot_general(
                    dz_pair, q_pairs[lb], TN, preferred_element_type=F32)
                dvacc[pl.ds(r0, KEY_CHUNK), lb * LANE:(lb + 1) * LANE] += lax.dot_general(
                    a_pair, do_pairs[lb], TN, preferred_element_type=F32)

        near = jnp.maximum(qb - (KEY_TILES - 1), 0)
        n_full = lax.shift_right_logical(near, KEY_SHIFT)

        def step(i, c):
            chunk(KEY_TILES * i, None)
            return c

        lax.fori_loop(0, n_full, step, 0)

        @pl.when(near > KEY_TILES * n_full)
        def _():
            chunk(KEY_TILES * n_full, near * QB)

        chunk(near, row_g)
        for lb in range(blocks):
            dq2 = jnp.where(head0, dqacc[2 * lb], dqacc[2 * lb + 1]) * scale
            dq_ref[:, lb * LANE:(lb + 1) * LANE] = dq2.astype(dq_ref.dtype)

        @pl.when(qb == nqb - 1)
        def _():
            dk_ref[...] = dkacc[...].astype(dk_ref.dtype)
            dv_ref[...] = dvacc[...].astype(dv_ref.dtype)

        @pl.when((pl.program_id(0) == ngrp - 1) & (qb == nqb - 1))
        def _():
            for f in waits:
                f()

    out = jax.ShapeDtypeStruct((m, d_model), BF16)
    wide = blocks * LANE
    carry = pltpu.VMEM((2 * blocks, QB, LANE), F32)
    outs = pl.pallas_call(
        body, name="attn_bwd", grid=(ngrp, nqb),
        in_specs=[pl.BlockSpec((QB, wide), lambda g, qb: (qb, qo + g)),
                  pl.BlockSpec((m, wide), lambda g, qb: (0, ko + g)),
                  pl.BlockSpec((m, wide), lambda g, qb: (0, vo + g)),
                  pl.BlockSpec((QB, wide), lambda g, qb: (qb, g)),
                  pl.BlockSpec((QB, 2 * wide), lambda g, qb: (qb, g))] + [ANY] * n_sl,
        out_specs=[pl.BlockSpec((QB, wide), lambda g, qb: (qb, g)),
                   pl.BlockSpec((m, wide), lambda g, qb: (0, g)),
                   pl.BlockSpec((m, wide), lambda g, qb: (0, g))] + [ANY] * n_sl,
        out_shape=[out, out, out] + _received_shapes(slabs),
        scratch_shapes=[pltpu.VMEM((m, wide), F32), pltpu.VMEM((m, wide), F32), carry, carry, carry] + _exchange_sems(n_sl),
        compiler_params=_params(2))(p, p, p, d_o, tot, *slabs)
    return outs[0], outs[1], outs[2], outs[3:]


def _mesh_pos():
    return lax.axis_index("x"), lax.axis_index("y"), lax.axis_index("c")


def _other_chips(x, y):
    return [(1 - x, y), (x, 1 - y), (1 - x, 1 - y)]


def _dev(x, y, c):
    return 4 * x + 2 * y + c


def _all_gather(shards):
    n = len(shards)

    def body(*refs):
        starts, relays, waits = _gather_copies(refs[:n], refs[n:2 * n], *refs[2 * n:])
        for f in starts + relays + waits:
            f()

    return pl.pallas_call(
        body, name="comm_all_gather", in_specs=[ANY] * n, out_specs=[ANY] * n,
        out_shape=[jax.ShapeDtypeStruct((N_DEV,) + s.shape, s.dtype) for s in shards],
        scratch_shapes=[pltpu.SemaphoreType.DMA((n, 7)), pltpu.SemaphoreType.DMA((n, 7)), pltpu.SemaphoreType.DMA((n,))],
    )(*shards)


def _peers(x, y, c):
    out = []
    for mask in range(1, N_DEV):
        px, py, pc = x ^ (mask >> 2), y ^ ((mask >> 1) & 1), c ^ (mask & 1)
        out.append((mask - 1, (px, py, pc), _dev(px, py, pc)))
    return out


def _remote(src, dst, send_sems, recv_sems, k, s, peer):
    return pltpu.make_async_remote_copy(src_ref=src, dst_ref=dst, send_sem=send_sems.at[k, s], recv_sem=recv_sems.at[k, s],
                                        device_id=peer, device_id_type=MESH)


def _gather_copies(ins, outs, send_sems, recv_sems, local_sems):
    x, y, c = _mesh_pos()
    sibling = (x, y, 1 - c)
    chips = _other_chips(x, y)
    starts, relays, waits = [], [], []
    for k in range(len(ins)):
        def slot(block, k=k):
            return outs[k].at[_dev(*block)]

        def copy(s, src, block, to, k=k):
            return _remote(src, slot(block), send_sems, recv_sems, k, s, to)

        own = pltpu.make_async_copy(ins[k], slot((x, y, c)), local_sems.at[k])
        to_sibling = copy(0, ins[k], (x, y, c), sibling)
        starts += [own.start, to_sibling.start]
        waits += [own.wait, to_sibling.wait_send, copy(0, ins[k], (x, y, 1 - c), sibling).wait_recv]
        for j, chip in enumerate(chips):
            out = copy(1 + j, ins[k], (x, y, c), (*chip, c))
            relay = copy(4 + j, slot((*chip, c)), (*chip, c), sibling)
            starts.append(out.start)
            relays += [copy(1 + j, ins[k], (*chip, c), sibling).wait_recv, relay.start]
            waits += [out.wait_send, relay.wait_send, copy(4 + j, ins[k], (*chip, 1 - c), sibling).wait_recv]
    return starts, relays, waits


def _scatter_copies(ins, outs, send_sems, recv_sems):
    x, y, c = _mesh_pos()
    starts, waits = [], []
    for k in range(len(ins)):
        for s, peer, idx in _peers(x, y, c):
            send = _remote(ins[k].at[idx], outs[k].at[s], send_sems, recv_sems, k, s, peer)
            starts.append(send.start)
            waits += [send.wait_recv, send.wait_send]
    return starts, waits


def _exchange_sems(n):
    return [pltpu.SemaphoreType.DMA((n, N_DEV - 1)), pltpu.SemaphoreType.DMA((n, N_DEV - 1))]


def _received_shapes(slabs):
    return [jax.ShapeDtypeStruct((N_DEV - 1,) + a.shape[1:], a.dtype) for a in slabs]


def _chips_and_own(x, y):
    return _other_chips(x, y) + [(x, y)]


def _sibling_exchange(slabs):
    n = len(slabs)

    def body(*refs):
        ins, outs, (send_sems, recv_sems) = refs[:n], refs[n:2 * n], refs[2 * n:]
        x, y, c = _mesh_pos()
        copies = [_remote(ins[k].at[_dev(*chip, 1 - c)], outs[k].at[r], send_sems, recv_sems, k, r, (x, y, 1 - c))
                  for k in range(n) for r, chip in enumerate(_chips_and_own(x, y))]
        for cp in copies:
            cp.start()
        for cp in copies:
            cp.wait_recv()
        for cp in copies:
            cp.wait_send()

    return pl.pallas_call(body, name="comm_rs_sibling", in_specs=[ANY] * n, out_specs=[ANY] * n,
                          out_shape=[jax.ShapeDtypeStruct((4,) + a.shape[1:], a.dtype) for a in slabs],
                          scratch_shapes=[pltpu.SemaphoreType.DMA((n, 4)), pltpu.SemaphoreType.DMA((n, 4))])(*slabs)


def _chip_copies(ins, outs, send_sems, recv_sems):
    x, y, c = _mesh_pos()
    starts, waits = [], []
    for k in range(len(ins)):
        for r, chip in enumerate(_other_chips(x, y)):
            cp = _remote(ins[k].at[r], outs[k].at[r], send_sems, recv_sems, k, r, (*chip, c))
            starts.append(cp.start)
            waits += [cp.wait_recv, cp.wait_send]
    return starts, waits


def _pair_sum(slab_idx, grad, from_sibling):
    _, rows, cols = grad.shape
    tr = _shard_tile(rows)

    def body(idx_ref, g_ref, s_ref, o_ref):
        o_ref[...] = (g_ref[...] + s_ref[...].astype(F32)).astype(o_ref.dtype)

    gs = pltpu.PrefetchScalarGridSpec(
        num_scalar_prefetch=1, grid=(3, rows // tr),
        in_specs=[pl.BlockSpec((None, tr, cols), lambda r, i, idx: (idx[r], i, 0)),
                  pl.BlockSpec((None, tr, cols), lambda r, i, idx: (r, i, 0))],
        out_specs=pl.BlockSpec((None, tr, cols), lambda r, i, idx: (r, i, 0)))
    return pl.pallas_call(body, name="rs_pair_sum", grid_spec=gs, out_shape=jax.ShapeDtypeStruct((3, rows, cols), BF16),
                          compiler_params=_params(2))(slab_idx, grad, from_sibling)


def _shard_tile(rows):
    for tr in range(min(rows, 352), 0, -1):
        if rows % tr == 0 and (tr % 16 == 0 or tr == rows):
            return tr


def _adamw_math(w, g, m, v):
    m = ADAM_B1 * m + (1.0 - ADAM_B1) * g
    v = ADAM_B2 * v + (1.0 - ADAM_B2) * (g * g)
    m_hat = m / (1.0 - ADAM_B1 ** ADAM_STEP)
    v_hat = v / (1.0 - ADAM_B2 ** ADAM_STEP)
    delta = -ADAM_LR * (m_hat / (jnp.sqrt(v_hat) + ADAM_EPS) + ADAM_WD * w)
    return delta, m, v


def _adamw_shard(me, grad, received, w, m, v):
    rows, cols = w.shape
    tr = _shard_tile(rows)
    n_rec = len(received)

    def body(me_ref, g_ref, *rest):
        r_refs, (w_ref, m_ref, v_ref, go_ref, do_ref, mo_ref, vo_ref) = rest[:n_rec], rest[n_rec:]
        g = g_ref[...]
        for r_ref in r_refs:
            for s in range(r_ref.shape[0]):
                g = g + r_ref[s].astype(F32)
        delta, m_new, v_new = _adamw_math(w_ref[...], g, m_ref[...], v_ref[...])
        go_ref[...] = g
        do_ref[...] = delta
        mo_ref[...] = m_new
        vo_ref[...] = v_new

    flat = pl.BlockSpec((tr, cols), lambda i, me: (i, 0))
    gs = pltpu.PrefetchScalarGridSpec(
        num_scalar_prefetch=1, grid=(rows // tr,),
        in_specs=[pl.BlockSpec((None, tr, cols), lambda i, me: (me[0], i, 0))]
        + [pl.BlockSpec((r.shape[0], tr, cols), lambda i, me: (0, i, 0)) for r in received] + [flat, flat, flat],
        out_specs=[flat, flat, flat, flat])
    out = jax.ShapeDtypeStruct((rows, cols), F32)
    return pl.pallas_call(body, name="adamw_shard", grid_spec=gs, out_shape=(out, out, out, out),
                          compiler_params=_params(1))(me, grad, *received, w, m, v)


def _small_reduce_adamw(slabs, w, m, v):
    _, rows, _ = slabs.shape

    def body(s_ref, w_ref, m_ref, v_ref, g_ref, d_ref, mo_ref, vo_ref, land, send_sems, recv_sems):
        x, y, c = _mesh_pos()
        me = _dev(x, y, c)
        copies = []
        for mask in range(1, N_DEV):
            px, py, pc = x ^ (mask >> 2), y ^ ((mask >> 1) & 1), c ^ (mask & 1)
            copies.append(pltpu.make_async_remote_copy(
                src_ref=s_ref.at[_dev(px, py, pc)], dst_ref=land.at[me], send_sem=send_sems.at[mask - 1],
                recv_sem=recv_sems.at[mask - 1], device_id=(px, py, pc), device_id_type=MESH))
        for cp in copies:
            cp.start()
        land[me] = s_ref[me]
        for mask in range(1, N_DEV):
            px, py, pc = x ^ (mask >> 2), y ^ ((mask >> 1) & 1), c ^ (mask & 1)
            pltpu.make_async_remote_copy(
                src_ref=s_ref.at[me], dst_ref=land.at[_dev(px, py, pc)], send_sem=send_sems.at[mask - 1],
                recv_sem=recv_sems.at[mask - 1], device_id=(px, py, pc), device_id_type=MESH).wait_recv()
        for cp in copies:
            cp.wait_send()
        g = land[0]
        for d in range(1, N_DEV):
            g = g + land[d]
        delta, m_new, v_new = _adamw_math(w_ref[...], g, m_ref[...], v_ref[...])
        g_ref[...] = g
        d_ref[...] = delta
        mo_ref[...] = m_new
        vo_ref[...] = v_new

    out = jax.ShapeDtypeStruct((rows, LANE), F32)
    return pl.pallas_call(
        body, name="comm_small_reduce_adamw", in_specs=[VMEM_WHOLE] * 4, out_specs=[VMEM_WHOLE] * 4, out_shape=(out, out, out, out),
        scratch_shapes=[pltpu.VMEM((N_DEV, rows, LANE), F32), pltpu.SemaphoreType.DMA((N_DEV - 1,)),
                        pltpu.SemaphoreType.DMA((N_DEV - 1,))],
    )(slabs, w, m, v)


def _cast_bf16(arrs):
    n = len(arrs)

    def body(*refs):
        for i_ref, o_ref in zip(refs[:n], refs[n:]):
            o_ref[...] = i_ref[...].astype(BF16)

    return pl.pallas_call(body, name="cast_bf16", in_specs=[VMEM_WHOLE] * n, out_specs=[VMEM_WHOLE] * n,
                          out_shape=[jax.ShapeDtypeStruct(a.shape, BF16) for a in arrs],
                          compiler_params=pltpu.CompilerParams(vmem_limit_bytes=VMEM_LIMIT))(*arrs)


REPLICATED = ("pre_mix_g", "gate_b", "dw_b", "conv_ln_g", "conv_ln_b", "post_mix_g", "pre_ffn_g", "post_ffn_g")
SHARDED = ("w_in", "w_conv_out", "w_attn_out", "w_o", "w_ffn_in", "w_ffn_out")
WEIGHTS = ("meta_tokens", "pre_mix_g", "w_in", "gate_b", "dw_w", "dw_b", "conv_ln_g", "conv_ln_b", "w_conv_out",
           "w_attn_out", "w_o", "post_mix_g", "pre_ffn_g", "w_ffn_in", "w_ffn_out", "post_ffn_g")


def kernel(x, meta_tokens, pre_mix_g, w_in, gate_b, dw_w, dw_b, conv_ln_g, conv_ln_b, w_conv_out, w_attn_out, w_o, post_mix_g, pre_ffn_g, w_ffn_in, w_ffn_out, post_ffn_g, loss_target, m_meta_tokens, m_pre_mix_g, m_w_in, m_gate_b, m_dw_w, m_dw_b, m_conv_ln_g, m_conv_ln_b, m_w_conv_out, m_w_attn_out, m_w_o, m_post_mix_g, m_pre_ffn_g, m_w_ffn_in, m_w_ffn_out, m_post_ffn_g, v_meta_tokens, v_pre_mix_g, v_w_in, v_gate_b, v_dw_w, v_dw_b, v_conv_ln_g, v_conv_ln_b, v_w_conv_out, v_w_attn_out, v_w_o, v_post_mix_g, v_pre_ffn_g, v_w_ffn_in, v_w_ffn_out, v_post_ffn_g):
    given = dict(locals())
    seq, d = x.shape[1], x.shape[2]
    n_meta = meta_tokens.shape[0]
    length = n_meta + seq
    m_rows = -(-length // QB) * QB
    dc = d // N_DEV
    assert dc == LANE and n_meta % 8 == 0 and seq % 8 == 0
    fs = w_ffn_in.shape[2]
    fr = w_ffn_out.shape[1]
    assert 2 * fr == fs

    local = {k: given[k][0] for k in SHARDED}
    cast = _cast_bf16([local[k] for k in SHARDED])
    dww_pad = jnp.pad(dw_w[0], ((0, CONV_PAD - CONV_WIDTH), (0, 0)))
    wi, meta_g, dww_g = _all_gather([cast[0], meta_tokens, dww_pad])
    meta_full = jnp.concatenate([meta_g[j] for j in range(N_DEV)], axis=1)
    dww_full = jnp.concatenate([dww_g[j] for j in range(N_DEV)], axis=1)
    ns = wi.shape[2]

    tail = jnp.zeros((m_rows - length, d), F32)
    h0 = jnp.concatenate([meta_full, x[0], tail], axis=0)
    target = jnp.concatenate([jnp.zeros((n_meta, d), F32), loss_target[0], tail], axis=0)

    (u,) = _rows("pre_mix_norm", lambda r0, xs, ps: ([_rms(xs[0], ps[0])], []), [h0], [pre_mix_g], [BF16], [])
    p, p16 = _matmul("in_proj", NN, u, wi, pl.BlockSpec((m_rows, d), lambda i: (0, 0)), pl.BlockSpec((None, d, ns), lambda i: (i, 0, 0)),
                     pl.BlockSpec((m_rows, ns), lambda i: (0, i)), jax.ShapeDtypeStruct((m_rows, N_DEV * ns), F32), (N_DEV,),
                     twin_bf16=True)
    o, tot, gathered = _attn_fwd(p16, d, list(cast[1:]))
    wco, wao, wo = (g.reshape(d, d) for g in gathered[0:3])
    wfi = gathered[3]
    wfo = gathered[4].reshape(N_DEV // 2, fs, d)
    y = _conv_fwd(p, dww_full, dw_b, d)
    (yc,) = _rows("conv_norm", lambda r0, xs, ps: ([_ln_silu(xs[0], ps[0], ps[1])], []), [y], [conv_ln_g, conv_ln_b], [BF16], [])
    y_conv = _dense_fwd("conv_out", yc, wco)
    y_attn = _dense_fwd("attn_out", o, wao)
    gate_cols = [(d, 5), (d, 6), None, None]
    (mixin,) = _rows("gate_mix", lambda r0, xs, ps: ([_gate_mix(*xs, ps[0])], []), [p, p, y_conv, y_attn], [gate_b], [BF16], [],
                     row_in_cols=gate_cols, row_out_widths=[d])
    mix = _dense_fwd("mix_out", mixin, wo)
    h1, u2 = _rows("post_mix", lambda r0, xs, ps: (list(_post_mix(xs[0], xs[1], ps[0], ps[1])), []), [h0, mix],
                   [post_mix_g, pre_ffn_g], [F32, BF16], [])
    ab = _matmul("ffn_in", NN, u2, wfi, pl.BlockSpec((m_rows, d), lambda i: (0, 0)), pl.BlockSpec((None, d, fs), lambda i: (i, 0, 0)),
                 pl.BlockSpec((None, m_rows, fs), lambda i: (i, 0, 0)), jax.ShapeDtypeStruct((N_DEV, m_rows, fs), BF16), (N_DEV,))
    half = N_DEV // 2
    tm = _row_tile(m_rows)
    pair = lambda off: pl.BlockSpec((None, tm, fs), lambda j, i, off=off: (j + off, i, 0))
    (f_in,) = _rowwise("swiglu", lambda r0, xs, ps: ([_swiglu(xs[0].astype(F32), xs[1].astype(F32))], []), [ab, ab], [],
                       [jax.ShapeDtypeStruct((half, m_rows, fs), BF16)], [], grid=(half, m_rows // tm),
                       in_specs=[pair(0), pair(half)], out_specs=[pair(0)], tm=tm, row_axis=1)
    f = _matmul("ffn_out", NN, f_in, wfo, pl.BlockSpec((None, m_rows, fs), lambda j: (j, 0, 0)), pl.BlockSpec((None, fs, d), lambda j: (j, 0, 0)),
                pl.BlockSpec((m_rows, d), lambda j: (0, 0)), jax.ShapeDtypeStruct((m_rows, d), F32), (half,), acc_axis=0)

    def loss_head(r0, xs, ps):
        h1_, f_, t_ = xs
        r, vjp = jax.vjp(_rms, f_, ps[0])
        rows = r0 + lax.broadcasted_iota(I32, (h1_.shape[0], 1), 0)
        real = (rows >= n_meta) & (rows < length)
        err = jnp.where(real, h1_ + r - t_, 0.0)
        dh2 = err * (1.0 / d)
        d_f, dg = vjp(dh2)
        part = jnp.sum(0.5 * jnp.mean(err * err, axis=-1, keepdims=True), axis=0, keepdims=True)
        return [d_f, dh2], [dg, jnp.broadcast_to(part, (1, LANE))]

    d_f, dh2, g_post_ffn, loss_part = _rows("loss_head", loss_head, [h1, f, target], [post_ffn_g], [BF16, F32], [d, LANE])

    d_fin = _matmul("ffn_out_dx", NT, d_f, wfo, pl.BlockSpec((m_rows, d), lambda j: (0, 0)), pl.BlockSpec((None, fs, d), lambda j: (j, 0, 0)),
                    pl.BlockSpec((None, m_rows, fs), lambda j: (j, 0, 0)), jax.ShapeDtypeStruct((half, m_rows, fs), BF16), (half,))
    g_wfo = _matmul("ffn_out_dw", TN, f_in, d_f, pl.BlockSpec((None, m_rows, fs), lambda j: (j, 0, 0)), pl.BlockSpec((m_rows, d), lambda j: (0, 0)),
                    pl.BlockSpec((None, fs, d), lambda j: (j, 0, 0)), jax.ShapeDtypeStruct((half, fs, d), F32), (half,), twin_bf16=True)

    def swiglu_bwd(r0, xs, ps):
        _, vjp = jax.vjp(_swiglu, xs[0].astype(F32), xs[1].astype(F32))
        return [vjp(xs[2].astype(F32))], []

    (d_ab,) = _rowwise("swiglu_bwd", swiglu_bwd, [ab, ab, d_fin], [],
                       [jax.ShapeDtypeStruct((2, half, m_rows, fs), BF16)], [], grid=(half, m_rows // tm),
                       in_specs=[pair(0), pair(half), pair(0)],
                       out_specs=[pl.BlockSpec((2, None, tm, fs), lambda j, i: (0, j, i, 0))], tm=tm, row_axis=1)
    d_ab = d_ab.reshape(N_DEV, m_rows, fs)
    du2 = _matmul("ffn_in_dx", NT, d_ab, wfi, pl.BlockSpec((None, m_rows, fs), lambda i: (i, 0, 0)), pl.BlockSpec((None, d, fs), lambda i: (i, 0, 0)),
                  pl.BlockSpec((m_rows, d), lambda i: (0, 0)), jax.ShapeDtypeStruct((m_rows, d), F32), (N_DEV,), acc_axis=0)
    g_wfi = _matmul("ffn_in_dw", TN, u2, d_ab, pl.BlockSpec((m_rows, d), lambda i: (0, 0)), pl.BlockSpec((None, m_rows, fs), lambda i: (i, 0, 0)),
                    pl.BlockSpec((None, d, fs), lambda i: (i, 0, 0)), jax.ShapeDtypeStruct((N_DEV, d, fs), F32), (N_DEV,), twin_bf16=True)

    def post_mix_bwd(r0, xs, ps):
        h0_, mix_, dh2_, du2_ = xs
        _, vjp = jax.vjp(_post_mix, h0_, mix_, ps[0], ps[1])
        dh0_, dmix_, dg1, dg2 = vjp((dh2_, du2_))
        return [dmix_, dh0_], [dg1, dg2]

    d_mix, dh1, g_post_mix, g_pre_ffn = _rows("post_mix_bwd", post_mix_bwd, [h0, mix, dh2, du2], [post_mix_g, pre_ffn_g],
                                              [BF16, F32], [d, d])
    d_mixin = _dense_dx("mix_out_dx", d_mix, wo, F32)
    g_wo = _dense_dw("mix_out_dw", mixin, d_mix)

    def gate_mix_bwd(r0, xs, ps):
        _, vjp = jax.vjp(_gate_mix, xs[0], xs[1], xs[2], xs[3], ps[0])
        dpgc, dpga, dyc_, dya_, dgb = vjp(xs[4])
        return [dpgc, dpga, dyc_, dya_], [dgb]

    dp_gc, dp_ga, d_yconv, d_yattn, g_gate_b = _rows(
        "gate_mix_bwd", gate_mix_bwd, [p, p, y_conv, y_attn, d_mixin], [gate_b], [BF16] * 4, [2 * d],
        row_in_cols=gate_cols + [None], row_out_widths=[d] * 4)
    d_o = _dense_dx("attn_out_dx", d_yattn, wao, BF16)
    g_wao = _dense_dw("attn_out_dw", o, d_yattn)
    d_yc = _dense_dx("conv_out_dx", d_yconv, wco, F32)
    g_wco = _dense_dw("conv_out_dw", yc, d_yconv)
    big = {"w_ffn_out": [g.reshape(N_DEV, fr, d) for g in g_wfo], "w_ffn_in": g_wfi,
           "w_o": [g.reshape(N_DEV, dc, d) for g in g_wo], "w_attn_out": [g.reshape(N_DEV, dc, d) for g in g_wao],
           "w_conv_out": [g.reshape(N_DEV, dc, d) for g in g_wco]}
    early = ("w_ffn_out", "w_ffn_in", "w_o", "w_attn_out", "w_conv_out")
    dq, dk, dv, received_early = _attn_bwd(p16, d_o, tot, d, [big[k][1] for k in early])

    def conv_norm_bwd(r0, xs, ps):
        _, vjp = jax.vjp(_ln_silu, xs[0], ps[0], ps[1])
        dy_, dg, db = vjp(xs[1])
        return [dy_], [dg, db]

    d_y, g_ln_g, g_ln_b = _rows("conv_norm_bwd", conv_norm_bwd, [y, d_yc], [conv_ln_g, conv_ln_b], [F32], [d, d])
    dp_a, dp_g, g_dww, g_dwb = _conv_bwd(p, d_y, dww_full, d)
    dp = jnp.concatenate([dp_a, dp_g, dq, dk, dv, dp_gc, dp_ga], axis=1)
    g_wi = _matmul("in_proj_dw", TN, u, dp, pl.BlockSpec((m_rows, d), lambda i: (0, 0)), pl.BlockSpec((m_rows, ns), lambda i: (0, i)),
                   pl.BlockSpec((None, d, ns), lambda i: (i, 0, 0)), jax.ShapeDtypeStruct((N_DEV, d, ns), F32), (N_DEV,), twin_bf16=True)
    x_i, y_i, c_i = _mesh_pos()
    slab_idx = jnp.stack([_dev(*chip, c_i) for chip in _other_chips(x_i, y_i)]).astype(I32)
    (wi_sibling,) = _sibling_exchange([g_wi[1]])
    wi_pairs = _pair_sum(slab_idx, g_wi[0], wi_sibling)
    du, wi_chips = _matmul(
        "in_proj_dx", NT, dp, wi, pl.BlockSpec((m_rows, ns), lambda i: (0, i)), pl.BlockSpec((None, d, ns), lambda i: (i, 0, 0)),
        pl.BlockSpec((m_rows, d), lambda i: (0, 0)), jax.ShapeDtypeStruct((m_rows, d), F32), (N_DEV,), acc_axis=0,
        carried=(_chip_copies, [wi_pairs], [jax.ShapeDtypeStruct(wi_pairs.shape, BF16)],
                 [pltpu.SemaphoreType.DMA((1, 3)), pltpu.SemaphoreType.DMA((1, 3))]))

    def pre_mix_bwd(r0, xs, ps):
        _, vjp = jax.vjp(_rms, xs[0], ps[0])
        dx, dg = vjp(xs[1])
        return [xs[2] + dx], [dg]

    dh0, g_pre_mix = _rows("pre_mix_bwd", pre_mix_bwd, [h0, du, dh1], [pre_mix_g], [F32], [d])
    grad_x = dh0[n_meta:length][None]

    me = _dev(x_i, y_i, c_i)
    me_arr = jnp.reshape(me, (1,)).astype(I32)
    big["w_in"] = g_wi
    received = {k: [r] for k, r in zip(early, received_early)}
    received["w_in"] = [wi_sibling[3:4], wi_chips]
    results = {}
    for k in SHARDED:
        outs = _adamw_shard(me_arr, big[k][0], received[k], local[k], given["m_" + k][0], given["v_" + k][0])
        results[k] = tuple(a[None] for a in outs)

    rep_grads = {"pre_mix_g": g_pre_mix, "gate_b": g_gate_b, "dw_b": g_dwb, "conv_ln_g": g_ln_g, "conv_ln_b": g_ln_b,
                 "post_mix_g": g_post_mix, "pre_ffn_g": g_pre_ffn, "post_ffn_g": g_post_ffn}

    def pack_rep(get):
        return jnp.concatenate([get(k) for k in REPLICATED], axis=1).reshape(-1, LANE)

    rep_rows = pack_rep(lambda k: rep_grads[k])
    n_rep = rep_rows.shape[0]
    loss_rows = jnp.broadcast_to(loss_part, (8, LANE))
    g_meta = dh0[0:n_meta]
    slabs = jnp.stack([jnp.concatenate([rep_rows, loss_rows, g_dww[:, j * LANE:(j + 1) * LANE], g_meta[:, j * LANE:(j + 1) * LANE]], axis=0)
                       for j in range(N_DEV)])

    def pack_small(prefix):
        dww_own = jnp.pad(given[prefix + "dw_w"][0], ((0, CONV_PAD - CONV_WIDTH), (0, 0)))
        return jnp.concatenate([pack_rep(lambda k: given[prefix + k]), jnp.zeros((8, LANE), F32), dww_own,
                                given[prefix + "meta_tokens"]], axis=0)

    small = _small_reduce_adamw(slabs, pack_small(""), pack_small("m_"), pack_small("v_"))
    loss = small[0][n_rep, 0]

    def unpack(arr):
        out = {}
        flat = arr[:n_rep].reshape(1, -1)
        off = 0
        for k in REPLICATED:
            w = given[k].shape[1]
            out[k] = flat[:, off:off + w]
            off += w
        out["dw_w"] = arr[n_rep + 8:n_rep + 8 + CONV_WIDTH][None]
        out["meta_tokens"] = arr[n_rep + 8 + CONV_PAD:n_rep + 8 + CONV_PAD + n_meta]
        return out

    small_out = [unpack(a) for a in small]
    for k in WEIGHTS:
        if k not in results:
            results[k] = tuple(s[k] for s in small_out)
    return (loss, grad_x, *[results[k][0] for k in WEIGHTS], *[results[k][1] for k in WEIGHTS],
            *[results[k][2] for k in WEIGHTS], *[results[k][3] for k in WEIGHTS])
```

```python
import jax
import jax.numpy as jnp
from jax import lax
from jax.experimental import pallas as pl
from jax.experimental.pallas import tpu as pltpu

F32 = jnp.float32
BF16 = jnp.bfloat16
I32 = jnp.int32

N_DEV = 8
LANE = 128
HEAD_DIM = 64
QB = 128
KEY_SHIFT = 2
KEY_TILES = 1 << KEY_SHIFT
KEY_CHUNK = KEY_TILES * QB
LANE_BLOCKS = 4
BWD_LANE_BLOCKS = 4
CONV_WIDTH = 31
CONV_PAD = 32
ROW_CHUNK = 128
RMS_EPS = 1e-6
LN_EPS = 1e-5
ADAM_LR = 0.001
ADAM_B1 = 0.9
ADAM_B2 = 0.999
ADAM_EPS = 1e-08
ADAM_WD = 0.01
ADAM_STEP = 10
VMEM_LIMIT = 56 * 1024 * 1024

NN = (((1,), (0,)), ((), ()))
NT = (((1,), (1,)), ((), ()))
TN = (((0,), (0,)), ((), ()))
MESH = pl.DeviceIdType.MESH
ANY = pl.BlockSpec(memory_space=pl.ANY)
VMEM_WHOLE = pl.BlockSpec(memory_space=pltpu.VMEM)


def _params(n_axes):
    return pltpu.CompilerParams(dimension_semantics=("arbitrary",) * n_axes, vmem_limit_bytes=VMEM_LIMIT)


def _row_tile(m):
    assert m % QB == 0
    return m // 4 if m % 64 == 0 else QB


def _matmul(name, dims, a, b, a_spec, b_spec, o_spec, out_shape, grid, acc_axis=None, twin_bf16=False, carried=None):
    n_car = 0 if carried is None else len(carried[1])

    def body(a_ref, b_ref, *rest):
        car_ins, o_ref, rest = rest[:n_car], rest[n_car], rest[n_car + 1:]
        if carried is not None:
            starts, waits = carried[0](car_ins, rest[:n_car], *rest[n_car:])

            @pl.when(pl.program_id(0) == 0)
            def _():
                for f in starts:
                    f()

        r = lax.dot_general(a_ref[...], b_ref[...], dims, preferred_element_type=F32)
        if acc_axis is None:
            o_ref[...] = r.astype(o_ref.dtype)
            if twin_bf16:
                rest[0][...] = r.astype(BF16)
        else:
            k = pl.program_id(acc_axis)

            @pl.when(k == 0)
            def _():
                o_ref[...] = r

            @pl.when(k > 0)
            def _():
                o_ref[...] += r

        if carried is not None:
            @pl.when(pl.program_id(0) == grid[0] - 1)
            def _():
                for f in waits:
                    f()

    in_specs, out_specs, out_shapes, scratch = [a_spec, b_spec], [o_spec], [out_shape], []
    if twin_bf16:
        assert acc_axis is None and carried is None
        out_specs.append(o_spec)
        out_shapes.append(jax.ShapeDtypeStruct(out_shape.shape, BF16))
    if carried is not None:
        assert len(grid) == 1
        in_specs += [ANY] * n_car
        out_specs += [ANY] * n_car
        out_shapes += list(carried[2])
        scratch = list(carried[3])
    outs = pl.pallas_call(body, name=name, grid=grid, in_specs=in_specs, out_specs=out_specs, out_shape=out_shapes,
                          scratch_shapes=scratch, compiler_params=_params(len(grid)))(a, b, *(carried[1] if carried else ()))
    return outs[0] if len(outs) == 1 else outs


DENSE_TILE = 256


def _dense_fwd(name, a, w, out_dtype=F32):
    m, k = a.shape
    n = w.shape[1]
    tn = DENSE_TILE
    return _matmul(name, NN, a, w, pl.BlockSpec((m, k), lambda j: (0, 0)), pl.BlockSpec((k, tn), lambda j: (0, j)),
                   pl.BlockSpec((m, tn), lambda j: (0, j)), jax.ShapeDtypeStruct((m, n), out_dtype), (n // tn,))


def _dense_dx(name, dy, w, out_dtype):
    m, n = dy.shape
    k = w.shape[0]
    tk = DENSE_TILE
    return _matmul(name, NT, dy, w, pl.BlockSpec((m, n), lambda j: (0, 0)), pl.BlockSpec((tk, n), lambda j: (j, 0)),
                   pl.BlockSpec((m, tk), lambda j: (0, j)), jax.ShapeDtypeStruct((m, k), out_dtype), (k // tk,))


def _dense_dw(name, a, dy):
    m, k = a.shape
    n = dy.shape[1]
    tn = DENSE_TILE
    return _matmul(name, TN, a, dy, pl.BlockSpec((m, k), lambda j: (0, 0)), pl.BlockSpec((m, tn), lambda j: (0, j)),
                   pl.BlockSpec((k, tn), lambda j: (0, j)), jax.ShapeDtypeStruct((k, n), F32), (n // tn,), twin_bf16=True)


def _rowwise(name, fn, row_ins, par_ins, row_outs, par_outs, *, grid, in_specs, out_specs, tm, row_axis):
    n_ri, n_pi, n_ro, n_po = len(row_ins), len(par_ins), len(row_outs), len(par_outs)
    n_steps, tail = divmod(tm, ROW_CHUNK)
    assert tail % 16 == 0

    def body(*refs):
        ri = refs[:n_ri]
        pi = refs[n_ri:n_ri + n_pi]
        ro = refs[n_ri + n_pi:n_ri + n_pi + n_ro]
        po = refs[n_ri + n_pi + n_ro:]
        ps = [r[...] for r in pi]
        base = pl.program_id(row_axis) * tm

        def chunk(r0, rows, carry):
            xs = [r[pl.ds(r0, rows), :] for r in ri]
            outs, pouts = fn(base + r0, xs, ps)
            for r, o in zip(ro, outs):
                if isinstance(o, (list, tuple)):
                    for j, part in enumerate(o):
                        r[j, pl.ds(r0, rows), :] = part.astype(r.dtype)
                else:
                    r[pl.ds(r0, rows), :] = o.astype(r.dtype)
            return tuple(c + q for c, q in zip(carry, pouts))

        def step(i, carry):
            return chunk(pl.multiple_of(i * ROW_CHUNK, ROW_CHUNK), ROW_CHUNK, carry)

        acc = lax.fori_loop(0, n_steps, step, tuple(jnp.zeros(s.shape, F32) for s in par_outs))
        if tail:
            acc = chunk(n_steps * ROW_CHUNK, tail, acc)
        if n_po:
            first = pl.program_id(0) == 0
            for ax in range(1, len(grid)):
                first = first & (pl.program_id(ax) == 0)

            @pl.when(first)
            def _():
                for r in po:
                    r[...] = jnp.zeros_like(r)

            for r, a in zip(po, acc):
                r[...] += a

    return pl.pallas_call(body, name=name, grid=grid, in_specs=in_specs, out_specs=out_specs,
                          out_shape=tuple(row_outs) + tuple(par_outs),
                          compiler_params=_params(len(grid)))(*row_ins, *par_ins)


def _rows(name, fn, row_ins, par_ins, row_out_dtypes, par_out_widths, row_in_cols=None, row_out_widths=None):
    m = row_ins[0].shape[0]
    tm = _row_tile(m)
    in_specs = []
    for k, a in enumerate(row_ins):
        if row_in_cols is not None and row_in_cols[k] is not None:
            width, cb = row_in_cols[k]
            in_specs.append(pl.BlockSpec((tm, width), lambda i, cb=cb: (i, cb)))
        else:
            in_specs.append(pl.BlockSpec((tm, a.shape[1]), lambda i: (i, 0)))
    for a in par_ins:
        in_specs.append(pl.BlockSpec(a.shape, lambda i: (0, 0)))
    if row_out_widths is None:
        row_out_widths = [row_ins[0].shape[1]] * len(row_out_dtypes)
    row_outs = [jax.ShapeDtypeStruct((m, w), dt) for w, dt in zip(row_out_widths, row_out_dtypes)]
    par_outs = [jax.ShapeDtypeStruct((1, w), F32) for w in par_out_widths]
    out_specs = [pl.BlockSpec((tm, s.shape[1]), lambda i: (i, 0)) for s in row_outs]
    out_specs += [pl.BlockSpec(s.shape, lambda i: (0, 0)) for s in par_outs]
    return _rowwise(name, fn, row_ins, par_ins, row_outs, par_outs, grid=(m // tm,), in_specs=in_specs,
                    out_specs=out_specs, tm=tm, row_axis=0)


def _rms(x, g):
    return x * lax.rsqrt(jnp.mean(x * x, axis=-1, keepdims=True) + RMS_EPS) * g


def _ln_silu(y, g, b):
    mu = jnp.mean(y, axis=-1, keepdims=True)
    yc = y - mu
    var = jnp.mean(yc * yc, axis=-1, keepdims=True)
    return jax.nn.silu(yc * lax.rsqrt(var + LN_EPS) * g + b)


def _gate_mix(pgc, pga, yc, ya, gb):
    d = pgc.shape[1]
    return jax.nn.sigmoid(pgc + gb[:, :d]) * yc + jax.nn.sigmoid(pga + gb[:, d:]) * ya


def _post_mix(h0, mix, g_post, g_pre):
    h1 = h0 + _rms(mix, g_post)
    return h1, _rms(h1, g_pre)


def _swiglu(a, b):
    return jax.nn.silu(a) * b


def _conv_taps():
    taps = []
    for b in range(8):
        for a in range(CONV_PAD // 8):
            s = 8 * a + b
            if s < CONV_WIDTH:
                taps.append((b, a, CONV_WIDTH - 1 - s))
    return taps


def _conv_fwd(p, dww, dwb, d_model):
    m = p.shape[0]
    nch = d_model // LANE
    n_chunk = m // QB
    taps = _conv_taps()

    def body(a_ref, g_ref, w_ref, b_ref, y_ref, upad):
        upad[0:CONV_PAD, :] = jnp.zeros((CONV_PAD, LANE), F32)

        def fill(i, c):
            r0 = pl.multiple_of(i * QB, QB)
            u = a_ref[pl.ds(r0, QB), :] * jax.nn.sigmoid(g_ref[pl.ds(r0, QB), :])
            upad[pl.ds(pl.multiple_of(r0 + CONV_PAD, 8), QB), :] = u
            return c

        lax.fori_loop(0, n_chunk, fill, 0)

        def conv(i, c):
            r0 = pl.multiple_of(i * QB, QB)
            win = upad[pl.ds(r0, QB + CONV_PAD), :]
            acc = jnp.broadcast_to(b_ref[...], (QB, LANE))
            rolled = {}
            for b, a, j in taps:
                if b not in rolled:
                    rolled[b] = win if b == 0 else pltpu.roll(win, b, axis=0)
                lo = CONV_PAD - 8 * a
                acc = acc + w_ref[j:j + 1, :] * rolled[b][lo:lo + QB, :]
            y_ref[pl.ds(r0, QB), :] = acc
            return c

        lax.fori_loop(0, n_chunk, conv, 0)

    col = lambda off: pl.BlockSpec((m, LANE), lambda c: (0, off + c))
    return pl.pallas_call(
        body, name="conv_fwd", grid=(nch,),
        in_specs=[col(0), col(nch), pl.BlockSpec((CONV_PAD, LANE), lambda c: (0, c)), pl.BlockSpec((1, LANE), lambda c: (0, c))],
        out_specs=col(0), out_shape=jax.ShapeDtypeStruct((m, d_model), F32),
        scratch_shapes=[pltpu.VMEM((m + CONV_PAD, LANE), F32)], compiler_params=_params(1))(p, p, dww, dwb)


def _conv_bwd(p, dy, dww, d_model):
    m = p.shape[0]
    nch = d_model // LANE
    n_chunk = m // QB
    taps = _conv_taps()
    win_rows = QB + CONV_PAD

    def body(a_ref, g_ref, dy_ref, w_ref, da_ref, dg_ref, dw_ref, db_ref, upad, dypad, wacc, bacc):
        upad[0:CONV_PAD, :] = jnp.zeros((CONV_PAD, LANE), F32)
        dypad[m:m + CONV_PAD, :] = jnp.zeros((CONV_PAD, LANE), F32)
        wacc[...] = jnp.zeros_like(wacc)
        bacc[...] = jnp.zeros_like(bacc)

        def fill(i, c):
            r0 = pl.multiple_of(i * QB, QB)
            u = a_ref[pl.ds(r0, QB), :] * jax.nn.sigmoid(g_ref[pl.ds(r0, QB), :])
            upad[pl.ds(pl.multiple_of(r0 + CONV_PAD, 8), QB), :] = u
            dypad[pl.ds(r0, QB), :] = dy_ref[pl.ds(r0, QB), :]
            return c

        lax.fori_loop(0, n_chunk, fill, 0)

        def chunk(i, c):
            r0 = pl.multiple_of(i * QB, QB)
            dwin = dypad[pl.ds(r0, win_rows), :]
            du = jnp.zeros((QB, LANE), F32)
            rolled = {}
            for b, a, j in taps:
                if b not in rolled:
                    rolled[b] = dwin if b == 0 else pltpu.roll(dwin, win_rows - b, axis=0)
                du = du + w_ref[j:j + 1, :] * rolled[b][8 * a:8 * a + QB, :]
            av = a_ref[pl.ds(r0, QB), :]
            sg = jax.nn.sigmoid(g_ref[pl.ds(r0, QB), :])
            da_ref[pl.ds(r0, QB), :] = (du * sg).astype(da_ref.dtype)
            dg_ref[pl.ds(r0, QB), :] = (du * av * sg * (1.0 - sg)).astype(dg_ref.dtype)
            dyc = dy_ref[pl.ds(r0, QB), :]
            uwin = upad[pl.ds(r0, win_rows), :]
            rolled = {}
            for b, a, j in taps:
                if b not in rolled:
                    rolled[b] = uwin if b == 0 else pltpu.roll(uwin, b, axis=0)
                lo = CONV_PAD - 8 * a
                prod = dyc * rolled[b][lo:lo + QB, :]
                wacc[j] += prod.reshape(QB // 8, 8, LANE).sum(axis=0)
            bacc[...] += dyc.reshape(QB // 8, 8, LANE).sum(axis=0)
            return c

        lax.fori_loop(0, n_chunk, chunk, 0)
        for j in range(CONV_WIDTH):
            dw_ref[j:j + 1, :] = jnp.sum(wacc[j], axis=0, keepdims=True)
        dw_ref[CONV_WIDTH:CONV_PAD, :] = jnp.zeros((CONV_PAD - CONV_WIDTH, LANE), F32)
        db_ref[...] = jnp.sum(bacc[...], axis=0, keepdims=True)

    col = lambda off: pl.BlockSpec((m, LANE), lambda c: (0, off + c))
    return pl.pallas_call(
        body, name="conv_bwd", grid=(nch,),
        in_specs=[col(0), col(nch), col(0), pl.BlockSpec((CONV_PAD, LANE), lambda c: (0, c))],
        out_specs=[col(0), col(0), pl.BlockSpec((CONV_PAD, LANE), lambda c: (0, c)), pl.BlockSpec((1, LANE), lambda c: (0, c))],
        out_shape=(jax.ShapeDtypeStruct((m, d_model), BF16), jax.ShapeDtypeStruct((m, d_model), BF16),
                   jax.ShapeDtypeStruct((CONV_PAD, d_model), F32), jax.ShapeDtypeStruct((1, d_model), F32)),
        scratch_shapes=[pltpu.VMEM((m + CONV_PAD, LANE), F32), pltpu.VMEM((m + CONV_PAD, LANE), F32),
                        pltpu.VMEM((CONV_PAD, 8, LANE), F32), pltpu.VMEM((8, LANE), F32)],
        compiler_params=_params(1))(p, p, dy, dww)


EXP_CLAMP = 80.0


def _one_plus_exp(z):
    return 1.0 + jnp.exp(jnp.minimum(z, EXP_CLAMP))


def _softplus(z):
    return jnp.maximum(jnp.log(_one_plus_exp(z)), z)


def _softplus_sigmoid(z):
    s = _one_plus_exp(z)
    return jnp.maximum(jnp.log(s), z), 1.0 - 1.0 / s


def _tile_cumsums(x, tri2):
    xb = x.astype(BF16)
    out = []
    for i in range(0, x.shape[1] // QB, 2):
        both = jnp.dot(xb[:, i * QB:(i + 2) * QB], tri2, preferred_element_type=F32)
        out += [both[:, :QB], both[:, QB:]]
    return out


def _tri2(kind):
    jj = lax.broadcasted_iota(I32, (2 * QB, 2 * QB), 0)
    ss = lax.broadcasted_iota(I32, (2 * QB, 2 * QB), 1)
    same = (jj >= QB) == (ss >= QB)
    keep = {"ge": jj >= ss, "le": jj <= ss}[kind]
    return jnp.where(same & keep, 1.0, 0.0).astype(BF16)


def _attn_fwd(p, d_model, shards):
    m = p.shape[0]
    nqb = m // QB
    ngrp = d_model // (LANE_BLOCKS * LANE)
    qo, ko, vo = 2 * ngrp, 3 * ngrp, 4 * ngrp
    scale = HEAD_DIM ** -0.5
    n_sh = len(shards)

    assert nqb >= KEY_TILES

    def body(q_ref, k_ref, v_ref, *rest):
        shard_refs, (o_ref, t_ref), rest = rest[:n_sh], rest[n_sh:n_sh + 2], rest[n_sh + 2:]
        gathered_refs, (acc_ref, car_ref), sems = rest[:n_sh], rest[n_sh:n_sh + 2], rest[n_sh + 2:]
        starts, relays, waits = _gather_copies(shard_refs, gathered_refs, *sems)

        @pl.when((pl.program_id(0) == 0) & (pl.program_id(1) == 0))
        def _():
            for f in starts:
                f()

        @pl.when((pl.program_id(0) == ngrp - 1) & (pl.program_id(1) == (3 * nqb) // 4))
        def _():
            for f in relays:
                f()

        qb = pl.program_id(1)
        lane = lax.broadcasted_iota(I32, (QB, LANE), 1)
        head0 = lane < HEAD_DIM
        row_g = qb * QB + lax.broadcasted_iota(I32, (QB, KEY_CHUNK), 0)
        col_l = lax.broadcasted_iota(I32, (QB, KEY_CHUNK), 1)
        tri = _tri2("ge")
        heads = range(2 * LANE_BLOCKS)
        qh = []
        for lb in range(LANE_BLOCKS):
            q2 = (q_ref[:, lb * LANE:(lb + 1) * LANE] * scale).astype(BF16)
            zero = jnp.zeros_like(q2)
            qh += [jnp.where(head0, q2, zero), jnp.where(head0, zero, q2)]
        acc_ref[...] = jnp.zeros_like(acc_ref)
        car_ref[...] = jnp.zeros_like(car_ref)

        def chunk(first_tile, bound, n_tiles=KEY_TILES):
            r0 = pl.multiple_of(first_tile * QB, QB)
            keys = n_tiles * QB
            kcs = [k_ref[pl.ds(r0, keys), lb * LANE:(lb + 1) * LANE].astype(BF16) for lb in range(LANE_BLOCKS)]
            vcs = [v_ref[pl.ds(r0, keys), lb * LANE:(lb + 1) * LANE].astype(BF16) for lb in range(LANE_BLOCKS)]
            valid = None if bound is None else (col_l[:, :keys] + r0) < bound
            zs = [lax.dot_general(qh[h], kcs[h // 2], NT, preferred_element_type=F32) for h in heads]
            sps = [_softplus(z) for z in zs]
            if valid is not None:
                sps = [jnp.where(valid, sp, 0.0) for sp in sps]
            cums = [_tile_cumsums(sp, tri) for sp in sps]
            cars = [car_ref[h] for h in heads]
            a_tiles = [[None] * n_tiles for h in heads]
            for i in reversed(range(n_tiles)):
                for h in heads:
                    cum = cums[h][i]
                    a_tiles[h][i] = jnp.exp(zs[h][:, i * QB:(i + 1) * QB] - (cum + cars[h]))
                    cars[h] = cars[h] + jnp.broadcast_to(cum[:, 0:1], cum.shape)
            for h in heads:
                a = jnp.concatenate(a_tiles[h], axis=1)
                if valid is not None:
                    a = jnp.where(valid, a, 0.0)
                acc_ref[h] += jnp.dot(a.astype(BF16), vcs[h // 2], preferred_element_type=F32)
                car_ref[h] = cars[h]

        near = jnp.maximum(qb - (KEY_TILES - 1), 0)
        chunk(near, row_g)
        n_full = lax.shift_right_logical(near, KEY_SHIFT)

        def step(i, c):
            chunk(near - KEY_TILES * (i + 1), None)
            return c

        lax.fori_loop(0, n_full, step, 0)
        left = near - KEY_TILES * n_full

        @pl.when((left > 0) & (left <= 2))
        def _():
            chunk(0, left * QB, n_tiles=2)

        @pl.when(left > 2)
        def _():
            chunk(0, left * QB)

        for lb in range(LANE_BLOCKS):
            o_ref[:, lb * LANE:(lb + 1) * LANE] = jnp.where(head0, acc_ref[2 * lb], acc_ref[2 * lb + 1]).astype(o_ref.dtype)
        for h in heads:
            t_ref[:, h * QB:(h + 1) * QB] = car_ref[h]

        @pl.when((pl.program_id(0) == ngrp - 1) & (pl.program_id(1) == nqb - 1))
        def _():
            for f in waits:
                f()

    wide = LANE_BLOCKS * LANE
    outs = pl.pallas_call(
        body, name="attn_fwd", grid=(ngrp, nqb),
        in_specs=[pl.BlockSpec((QB, wide), lambda g, qb: (qb, qo + g)),
                  pl.BlockSpec((m, wide), lambda g, qb: (0, ko + g)),
                  pl.BlockSpec((m, wide), lambda g, qb: (0, vo + g))] + [ANY] * n_sh,
        out_specs=[pl.BlockSpec((QB, wide), lambda g, qb: (qb, g)),
                   pl.BlockSpec((QB, 2 * wide), lambda g, qb: (qb, g))] + [ANY] * n_sh,
        out_shape=[jax.ShapeDtypeStruct((m, d_model), BF16), jax.ShapeDtypeStruct((m, 2 * d_model), F32)]
        + [jax.ShapeDtypeStruct((N_DEV,) + s.shape, s.dtype) for s in shards],
        scratch_shapes=[pltpu.VMEM((2 * LANE_BLOCKS, QB, LANE), F32), pltpu.VMEM((2 * LANE_BLOCKS, QB, LANE), F32)]
        + _exchange_sems(n_sh) + [pltpu.SemaphoreType.DMA((n_sh,))],
        compiler_params=_params(2))(p, p, p, *shards)
    return outs[0], outs[1], outs[2:]


def _attn_bwd(p, d_o, tot, d_model, slabs):
    m = p.shape[0]
    nqb = m // QB
    blocks = BWD_LANE_BLOCKS
    ngrp = d_model // (blocks * LANE)
    qo, ko, vo = 2 * ngrp, 3 * ngrp, 4 * ngrp
    scale = HEAD_DIM ** -0.5
    n_sl = len(slabs)

    assert nqb >= KEY_TILES

    def body(q_ref, k_ref, v_ref, do_ref, t_ref, *rest):
        slab_refs, (dq_ref, dk_ref, dv_ref), rest = rest[:n_sl], rest[n_sl:n_sl + 3], rest[n_sl + 3:]
        recv_refs, (dkacc, dvacc, dqacc, csp, cg), sems = rest[:n_sl], rest[n_sl:n_sl + 5], rest[n_sl + 5:]
        starts, waits = _scatter_copies(slab_refs, recv_refs, *sems)

        @pl.when((pl.program_id(0) == 0) & (pl.program_id(1) == 0))
        def _():
            for f in starts:
                f()

        qb = pl.program_id(1)
        lane = lax.broadcasted_iota(I32, (QB, LANE), 1)
        head0 = lane < HEAD_DIM
        row_g = qb * QB + lax.broadcasted_iota(I32, (QB, KEY_CHUNK), 0)
        col_l = lax.broadcasted_iota(I32, (QB, KEY_CHUNK), 1)
        tri_ge = _tri2("ge")
        tri_le = _tri2("le")
        heads = range(2 * blocks)
        qh, doh = [], []
        for lb in range(blocks):
            q2 = (q_ref[:, lb * LANE:(lb + 1) * LANE] * scale).astype(BF16)
            do2 = do_ref[:, lb * LANE:(lb + 1) * LANE]
            zero = jnp.zeros_like(q2)
            qh += [jnp.where(head0, q2, zero), jnp.where(head0, zero, q2)]
            doh += [jnp.where(head0, do2, zero), jnp.where(head0, zero, do2)]
        q_pairs = [jnp.concatenate(qh[2 * lb:2 * lb + 2], axis=0) for lb in range(blocks)]
        do_pairs = [jnp.concatenate(doh[2 * lb:2 * lb + 2], axis=0) for lb in range(blocks)]

        @pl.when(qb == 0)
        def _():
            dkacc[...] = jnp.zeros_like(dkacc)
            dvacc[...] = jnp.zeros_like(dvacc)

        dqacc[...] = jnp.zeros_like(dqacc)
        for h in heads:
            csp[h] = t_ref[:, h * QB:(h + 1) * QB]
        cg[...] = jnp.zeros_like(cg)

        def chunk(first_tile, bound, n_tiles=KEY_TILES):
            r0 = pl.multiple_of(first_tile * QB, QB)
            keys = n_tiles * QB
            kcs = [k_ref[pl.ds(r0, keys), lb * LANE:(lb + 1) * LANE].astype(BF16) for lb in range(blocks)]
            vcs = [v_ref[pl.ds(r0, keys), lb * LANE:(lb + 1) * LANE].astype(BF16) for lb in range(blocks)]
            valid = None if bound is None else (col_l[:, :keys] + r0) < bound
            tiles = [slice(i * QB, (i + 1) * QB) for i in range(n_tiles)]
            zs = [lax.dot_general(qh[h], kcs[h // 2], NT, preferred_element_type=F32) for h in heads]
            das = [lax.dot_general(doh[h], vcs[h // 2], NT, preferred_element_type=F32) for h in heads]
            sps, sgs = zip(*[_softplus_sigmoid(z) for z in zs])
            if valid is not None:
                sps = [jnp.where(valid, sp, 0.0) for sp in sps]
            cums = [_tile_cumsums(sp, tri_ge) for sp in sps]
            a_tiles, g_tiles = [[] for h in heads], [[] for h in heads]
            for h in heads:
                rest = csp[h]
                for i, c in enumerate(tiles):
                    cum = cums[h][i]
                    rest = rest - jnp.broadcast_to(cum[:, 0:1], cum.shape)
                    a = jnp.exp(zs[h][:, c] - (cum + rest))
                    if valid is not None:
                        a = jnp.where(valid[:, c], a, 0.0)
                    a_tiles[h].append(a)
                    g_tiles[h].append(a * das[h][:, c])
                csp[h] = rest
            gcums = [_tile_cumsums(jnp.concatenate(g_tiles[h], axis=1), tri_le) for h in heads]
            dzbs, abs_ = [], []
            for h in heads:
                g_before = cg[h]
                dz_tiles = []
                for i, c in enumerate(tiles):
                    gcum = gcums[h][i]
                    dz = g_tiles[h][i] - sgs[h][:, c] * (g_before + gcum)
                    if valid is not None:
                        dz = jnp.where(valid[:, c], dz, 0.0)
                    dz_tiles.append(dz)
                    g_before = g_before + jnp.broadcast_to(gcum[:, QB - 1:QB], gcum.shape)
                cg[h] = g_before
                dzbs.append(jnp.concatenate(dz_tiles, axis=1).astype(BF16))
                abs_.append(jnp.concatenate(a_tiles[h], axis=1).astype(BF16))
            for h in heads:
                dqacc[h] += jnp.dot(dzbs[h], kcs[h // 2], preferred_element_type=F32)
            for lb in range(blocks):
                dz_pair = jnp.concatenate(dzbs[2 * lb:2 * lb + 2], axis=0)
                a_pair = jnp.concatenate(abs_[2 * lb:2 * lb + 2], axis=0)
                dkacc[pl.ds(r0, keys), lb * LANE:(lb + 1) * LANE] += lax.dot_general(
                    dz_pair, q_pairs[lb], TN, preferred_element_type=F32)
                dvacc[pl.ds(r0, keys), lb * LANE:(lb + 1) * LANE] += lax.dot_general(
                    a_pair, do_pairs[lb], TN, preferred_element_type=F32)

        near = jnp.maximum(qb - (KEY_TILES - 1), 0)
        n_full = lax.shift_right_logical(near, KEY_SHIFT)

        def step(i, c):
            chunk(KEY_TILES * i, None)
            return c

        lax.fori_loop(0, n_full, step, 0)
        left = near - KEY_TILES * n_full

        @pl.when((left > 0) & (left <= 2))
        def _():
            chunk(KEY_TILES * n_full, near * QB, n_tiles=2)

        @pl.when(left > 2)
        def _():
            chunk(KEY_TILES * n_full, near * QB)

        chunk(near, row_g)
        for lb in range(blocks):
            dq2 = jnp.where(head0, dqacc[2 * lb], dqacc[2 * lb + 1]) * scale
            dq_ref[:, lb * LANE:(lb + 1) * LANE] = dq2.astype(dq_ref.dtype)

        @pl.when(qb == nqb - 1)
        def _():
            dk_ref[...] = dkacc[...].astype(dk_ref.dtype)
            dv_ref[...] = dvacc[...].astype(dv_ref.dtype)

        @pl.when((pl.program_id(0) == ngrp - 1) & (qb == nqb - 1))
        def _():
            for f in waits:
                f()

    out = jax.ShapeDtypeStruct((m, d_model), BF16)
    wide = blocks * LANE
    carry = pltpu.VMEM((2 * blocks, QB, LANE), F32)
    outs = pl.pallas_call(
        body, name="attn_bwd", grid=(ngrp, nqb),
        in_specs=[pl.BlockSpec((QB, wide), lambda g, qb: (qb, qo + g)),
                  pl.BlockSpec((m, wide), lambda g, qb: (0, ko + g)),
                  pl.BlockSpec((m, wide), lambda g, qb: (0, vo + g)),
                  pl.BlockSpec((QB, wide), lambda g, qb: (qb, g)),
                  pl.BlockSpec((QB, 2 * wide), lambda g, qb: (qb, g))] + [ANY] * n_sl,
        out_specs=[pl.BlockSpec((QB, wide), lambda g, qb: (qb, g)),
                   pl.BlockSpec((m, wide), lambda g, qb: (0, g)),
                   pl.BlockSpec((m, wide), lambda g, qb: (0, g))] + [ANY] * n_sl,
        out_shape=[out, out, out] + _received_shapes(slabs),
        scratch_shapes=[pltpu.VMEM((m, wide), F32), pltpu.VMEM((m, wide), F32), carry, carry, carry] + _exchange_sems(n_sl),
        compiler_params=_params(2))(p, p, p, d_o, tot, *slabs)
    return outs[0], outs[1], outs[2], outs[3:]


def _mesh_pos():
    return lax.axis_index("x"), lax.axis_index("y"), lax.axis_index("c")


def _other_chips(x, y):
    return [(1 - x, y), (x, 1 - y), (1 - x, 1 - y)]


def _dev(x, y, c):
    return 4 * x + 2 * y + c


def _all_gather(shards):
    n = len(shards)

    def body(*refs):
        starts, relays, waits = _gather_copies(refs[:n], refs[n:2 * n], *refs[2 * n:])
        for f in starts + relays + waits:
            f()

    return pl.pallas_call(
        body, name="comm_all_gather", in_specs=[ANY] * n, out_specs=[ANY] * n,
        out_shape=[jax.ShapeDtypeStruct((N_DEV,) + s.shape, s.dtype) for s in shards],
        scratch_shapes=[pltpu.SemaphoreType.DMA((n, 7)), pltpu.SemaphoreType.DMA((n, 7)), pltpu.SemaphoreType.DMA((n,))],
    )(*shards)


def _peers(x, y, c):
    out = []
    for mask in range(1, N_DEV):
        px, py, pc = x ^ (mask >> 2), y ^ ((mask >> 1) & 1), c ^ (mask & 1)
        out.append((mask - 1, (px, py, pc), _dev(px, py, pc)))
    return out


def _remote(src, dst, send_sems, recv_sems, k, s, peer):
    return pltpu.make_async_remote_copy(src_ref=src, dst_ref=dst, send_sem=send_sems.at[k, s], recv_sem=recv_sems.at[k, s],
                                        device_id=peer, device_id_type=MESH)


def _gather_copies(ins, outs, send_sems, recv_sems, local_sems):
    x, y, c = _mesh_pos()
    sibling = (x, y, 1 - c)
    chips = _other_chips(x, y)
    starts, relays, waits = [], [], []
    for k in range(len(ins)):
        def slot(block, k=k):
            return outs[k].at[_dev(*block)]

        def copy(s, src, block, to, k=k):
            return _remote(src, slot(block), send_sems, recv_sems, k, s, to)

        own = pltpu.make_async_copy(ins[k], slot((x, y, c)), local_sems.at[k])
        to_sibling = copy(0, ins[k], (x, y, c), sibling)
        starts += [own.start, to_sibling.start]
        waits += [own.wait, to_sibling.wait_send, copy(0, ins[k], (x, y, 1 - c), sibling).wait_recv]
        for j, chip in enumerate(chips):
            out = copy(1 + j, ins[k], (x, y, c), (*chip, c))
            relay = copy(4 + j, slot((*chip, c)), (*chip, c), sibling)
            starts.append(out.start)
            relays += [copy(1 + j, ins[k], (*chip, c), sibling).wait_recv, relay.start]
            waits += [out.wait_send, relay.wait_send, copy(4 + j, ins[k], (*chip, 1 - c), sibling).wait_recv]
    return starts, relays, waits


def _scatter_copies(ins, outs, send_sems, recv_sems):
    x, y, c = _mesh_pos()
    starts, waits = [], []
    for k in range(len(ins)):
        for s, peer, idx in _peers(x, y, c):
            send = _remote(ins[k].at[idx], outs[k].at[s], send_sems, recv_sems, k, s, peer)
            starts.append(send.start)
            waits += [send.wait_recv, send.wait_send]
    return starts, waits


def _exchange_sems(n):
    return [pltpu.SemaphoreType.DMA((n, N_DEV - 1)), pltpu.SemaphoreType.DMA((n, N_DEV - 1))]


def _received_shapes(slabs):
    return [jax.ShapeDtypeStruct((N_DEV - 1,) + a.shape[1:], a.dtype) for a in slabs]


def _chips_and_own(x, y):
    return _other_chips(x, y) + [(x, y)]


def _sibling_exchange(slabs):
    n = len(slabs)

    def body(*refs):
        ins, outs, (send_sems, recv_sems) = refs[:n], refs[n:2 * n], refs[2 * n:]
        x, y, c = _mesh_pos()
        copies = [_remote(ins[k].at[_dev(*chip, 1 - c)], outs[k].at[r], send_sems, recv_sems, k, r, (x, y, 1 - c))
                  for k in range(n) for r, chip in enumerate(_chips_and_own(x, y))]
        for cp in copies:
            cp.start()
        for cp in copies:
            cp.wait_recv()
        for cp in copies:
            cp.wait_send()

    return pl.pallas_call(body, name="comm_rs_sibling", in_specs=[ANY] * n, out_specs=[ANY] * n,
                          out_shape=[jax.ShapeDtypeStruct((4,) + a.shape[1:], a.dtype) for a in slabs],
                          scratch_shapes=[pltpu.SemaphoreType.DMA((n, 4)), pltpu.SemaphoreType.DMA((n, 4))])(*slabs)


def _chip_copies(ins, outs, send_sems, recv_sems):
    x, y, c = _mesh_pos()
    starts, waits = [], []
    for k in range(len(ins)):
        for r, chip in enumerate(_other_chips(x, y)):
            cp = _remote(ins[k].at[r], outs[k].at[r], send_sems, recv_sems, k, r, (*chip, c))
            starts.append(cp.start)
            waits += [cp.wait_recv, cp.wait_send]
    return starts, waits


def _pair_sum(slab_idx, grad, from_sibling):
    _, rows, cols = grad.shape
    tr = _shard_tile(rows)

    def body(idx_ref, g_ref, s_ref, o_ref):
        o_ref[...] = (g_ref[...] + s_ref[...].astype(F32)).astype(o_ref.dtype)

    gs = pltpu.PrefetchScalarGridSpec(
        num_scalar_prefetch=1, grid=(3, rows // tr),
        in_specs=[pl.BlockSpec((None, tr, cols), lambda r, i, idx: (idx[r], i, 0)),
                  pl.BlockSpec((None, tr, cols), lambda r, i, idx: (r, i, 0))],
        out_specs=pl.BlockSpec((None, tr, cols), lambda r, i, idx: (r, i, 0)))
    return pl.pallas_call(body, name="rs_pair_sum", grid_spec=gs, out_shape=jax.ShapeDtypeStruct((3, rows, cols), BF16),
                          compiler_params=_params(2))(slab_idx, grad, from_sibling)


def _shard_tile(rows):
    for tr in range(min(rows, 352), 0, -1):
        if rows % tr == 0 and (tr % 16 == 0 or tr == rows):
            return tr


def _adamw_math(w, g, m, v):
    m = ADAM_B1 * m + (1.0 - ADAM_B1) * g
    v = ADAM_B2 * v + (1.0 - ADAM_B2) * (g * g)
    m_hat = m / (1.0 - ADAM_B1 ** ADAM_STEP)
    v_hat = v / (1.0 - ADAM_B2 ** ADAM_STEP)
    delta = -ADAM_LR * (m_hat / (jnp.sqrt(v_hat) + ADAM_EPS) + ADAM_WD * w)
    return delta, m, v


def _adamw_shard(me, grad, received, w, m, v):
    rows, cols = w.shape
    tr = _shard_tile(rows)
    n_rec = len(received)

    def body(me_ref, g_ref, *rest):
        r_refs, (w_ref, m_ref, v_ref, go_ref, do_ref, mo_ref, vo_ref) = rest[:n_rec], rest[n_rec:]
        g = g_ref[...]
        for r_ref in r_refs:
            for s in range(r_ref.shape[0]):
                g = g + r_ref[s].astype(F32)
        delta, m_new, v_new = _adamw_math(w_ref[...], g, m_ref[...], v_ref[...])
        go_ref[...] = g
        do_ref[...] = delta
        mo_ref[...] = m_new
        vo_ref[...] = v_new

    flat = pl.BlockSpec((tr, cols), lambda i, me: (i, 0))
    gs = pltpu.PrefetchScalarGridSpec(
        num_scalar_prefetch=1, grid=(rows // tr,),
        in_specs=[pl.BlockSpec((None, tr, cols), lambda i, me: (me[0], i, 0))]
        + [pl.BlockSpec((r.shape[0], tr, cols), lambda i, me: (0, i, 0)) for r in received] + [flat, flat, flat],
        out_specs=[flat, flat, flat, flat])
    out = jax.ShapeDtypeStruct((rows, cols), F32)
    return pl.pallas_call(body, name="adamw_shard", grid_spec=gs, out_shape=(out, out, out, out),
                          compiler_params=_params(1))(me, grad, *received, w, m, v)


def _small_reduce_adamw(slabs, w, m, v):
    _, rows, _ = slabs.shape

    def body(s_ref, w_ref, m_ref, v_ref, g_ref, d_ref, mo_ref, vo_ref, land, send_sems, recv_sems):
        x, y, c = _mesh_pos()
        me = _dev(x, y, c)
        copies = []
        for mask in range(1, N_DEV):
            px, py, pc = x ^ (mask >> 2), y ^ ((mask >> 1) & 1), c ^ (mask & 1)
            copies.append(pltpu.make_async_remote_copy(
                src_ref=s_ref.at[_dev(px, py, pc)], dst_ref=land.at[me], send_sem=send_sems.at[mask - 1],
                recv_sem=recv_sems.at[mask - 1], device_id=(px, py, pc), device_id_type=MESH))
        for cp in copies:
            cp.start()
        land[me] = s_ref[me]
        for mask in range(1, N_DEV):
            px, py, pc = x ^ (mask >> 2), y ^ ((mask >> 1) & 1), c ^ (mask & 1)
            pltpu.make_async_remote_copy(
                src_ref=s_ref.at[me], dst_ref=land.at[_dev(px, py, pc)], send_sem=send_sems.at[mask - 1],
                recv_sem=recv_sems.at[mask - 1], device_id=(px, py, pc), device_id_type=MESH).wait_recv()
        for cp in copies:
            cp.wait_send()
        g = land[0]
        for d in range(1, N_DEV):
            g = g + land[d]
        delta, m_new, v_new = _adamw_math(w_ref[...], g, m_ref[...], v_ref[...])
        g_ref[...] = g
        d_ref[...] = delta
        mo_ref[...] = m_new
        vo_ref[...] = v_new

    out = jax.ShapeDtypeStruct((rows, LANE), F32)
    return pl.pallas_call(
        body, name="comm_small_reduce_adamw", in_specs=[VMEM_WHOLE] * 4, out_specs=[VMEM_WHOLE] * 4, out_shape=(out, out, out, out),
        scratch_shapes=[pltpu.VMEM((N_DEV, rows, LANE), F32), pltpu.SemaphoreType.DMA((N_DEV - 1,)),
                        pltpu.SemaphoreType.DMA((N_DEV - 1,))],
    )(slabs, w, m, v)


def _cast_bf16(arrs):
    n = len(arrs)

    def body(*refs):
        for i_ref, o_ref in zip(refs[:n], refs[n:]):
            o_ref[...] = i_ref[...].astype(BF16)

    return pl.pallas_call(body, name="cast_bf16", in_specs=[VMEM_WHOLE] * n, out_specs=[VMEM_WHOLE] * n,
                          out_shape=[jax.ShapeDtypeStruct(a.shape, BF16) for a in arrs],
                          compiler_params=pltpu.CompilerParams(vmem_limit_bytes=VMEM_LIMIT))(*arrs)


REPLICATED = ("pre_mix_g", "gate_b", "dw_b", "conv_ln_g", "conv_ln_b", "post_mix_g", "pre_ffn_g", "post_ffn_g")
SHARDED = ("w_in", "w_conv_out", "w_attn_out", "w_o", "w_ffn_in", "w_ffn_out")
WEIGHTS = ("meta_tokens", "pre_mix_g", "w_in", "gate_b", "dw_w", "dw_b", "conv_ln_g", "conv_ln_b", "w_conv_out",
           "w_attn_out", "w_o", "post_mix_g", "pre_ffn_g", "w_ffn_in", "w_ffn_out", "post_ffn_g")


def kernel(x, meta_tokens, pre_mix_g, w_in, gate_b, dw_w, dw_b, conv_ln_g, conv_ln_b, w_conv_out, w_attn_out, w_o, post_mix_g, pre_ffn_g, w_ffn_in, w_ffn_out, post_ffn_g, loss_target, m_meta_tokens, m_pre_mix_g, m_w_in, m_gate_b, m_dw_w, m_dw_b, m_conv_ln_g, m_conv_ln_b, m_w_conv_out, m_w_attn_out, m_w_o, m_post_mix_g, m_pre_ffn_g, m_w_ffn_in, m_w_ffn_out, m_post_ffn_g, v_meta_tokens, v_pre_mix_g, v_w_in, v_gate_b, v_dw_w, v_dw_b, v_conv_ln_g, v_conv_ln_b, v_w_conv_out, v_w_attn_out, v_w_o, v_post_mix_g, v_pre_ffn_g, v_w_ffn_in, v_w_ffn_out, v_post_ffn_g):
    given = dict(locals())
    seq, d = x.shape[1], x.shape[2]
    n_meta = meta_tokens.shape[0]
    length = n_meta + seq
    m_rows = -(-length // QB) * QB
    dc = d // N_DEV
    assert dc == LANE and n_meta % 8 == 0 and seq % 8 == 0
    fs = w_ffn_in.shape[2]
    fr = w_ffn_out.shape[1]
    assert 2 * fr == fs

    local = {k: given[k][0] for k in SHARDED}
    cast = _cast_bf16([local[k] for k in SHARDED])
    dww_pad = jnp.pad(dw_w[0], ((0, CONV_PAD - CONV_WIDTH), (0, 0)))
    wi, meta_g, dww_g = _all_gather([cast[0], meta_tokens, dww_pad])
    meta_full = jnp.concatenate([meta_g[j] for j in range(N_DEV)], axis=1)
    dww_full = jnp.concatenate([dww_g[j] for j in range(N_DEV)], axis=1)
    ns = wi.shape[2]

    tail = jnp.zeros((m_rows - length, d), F32)
    h0 = jnp.concatenate([meta_full, x[0], tail], axis=0)
    target = jnp.concatenate([jnp.zeros((n_meta, d), F32), loss_target[0], tail], axis=0)

    (u,) = _rows("pre_mix_norm", lambda r0, xs, ps: ([_rms(xs[0], ps[0])], []), [h0], [pre_mix_g], [BF16], [])
    p, p16 = _matmul("in_proj", NN, u, wi, pl.BlockSpec((m_rows, d), lambda i: (0, 0)), pl.BlockSpec((None, d, ns), lambda i: (i, 0, 0)),
                     pl.BlockSpec((m_rows, ns), lambda i: (0, i)), jax.ShapeDtypeStruct((m_rows, N_DEV * ns), F32), (N_DEV,),
                     twin_bf16=True)
    o, tot, gathered = _attn_fwd(p16, d, list(cast[1:]))
    wco, wao, wo = (g.reshape(d, d) for g in gathered[0:3])
    wfi = gathered[3]
    wfo = gathered[4].reshape(N_DEV // 2, fs, d)
    y = _conv_fwd(p, dww_full, dw_b, d)
    (yc,) = _rows("conv_norm", lambda r0, xs, ps: ([_ln_silu(xs[0], ps[0], ps[1])], []), [y], [conv_ln_g, conv_ln_b], [BF16], [])
    y_conv = _dense_fwd("conv_out", yc, wco)
    y_attn = _dense_fwd("attn_out", o, wao)
    gate_cols = [(d, 5), (d, 6), None, None]
    (mixin,) = _rows("gate_mix", lambda r0, xs, ps: ([_gate_mix(*xs, ps[0])], []), [p, p, y_conv, y_attn], [gate_b], [BF16], [],
                     row_in_cols=gate_cols, row_out_widths=[d])
    mix = _dense_fwd("mix_out", mixin, wo)
    h1, u2 = _rows("post_mix", lambda r0, xs, ps: (list(_post_mix(xs[0], xs[1], ps[0], ps[1])), []), [h0, mix],
                   [post_mix_g, pre_ffn_g], [F32, BF16], [])
    ab = _matmul("ffn_in", NN, u2, wfi, pl.BlockSpec((m_rows, d), lambda i: (0, 0)), pl.BlockSpec((None, d, fs), lambda i: (i, 0, 0)),
                 pl.BlockSpec((None, m_rows, fs), lambda i: (i, 0, 0)), jax.ShapeDtypeStruct((N_DEV, m_rows, fs), BF16), (N_DEV,))
    half = N_DEV // 2
    tm = _row_tile(m_rows)
    pair = lambda off: pl.BlockSpec((None, tm, fs), lambda j, i, off=off: (j + off, i, 0))
    (f_in,) = _rowwise("swiglu", lambda r0, xs, ps: ([_swiglu(xs[0].astype(F32), xs[1].astype(F32))], []), [ab, ab], [],
                       [jax.ShapeDtypeStruct((half, m_rows, fs), BF16)], [], grid=(half, m_rows // tm),
                       in_specs=[pair(0), pair(half)], out_specs=[pair(0)], tm=tm, row_axis=1)
    f = _matmul("ffn_out", NN, f_in, wfo, pl.BlockSpec((None, m_rows, fs), lambda j: (j, 0, 0)), pl.BlockSpec((None, fs, d), lambda j: (j, 0, 0)),
                pl.BlockSpec((m_rows, d), lambda j: (0, 0)), jax.ShapeDtypeStruct((m_rows, d), F32), (half,), acc_axis=0)

    def loss_head(r0, xs, ps):
        h1_, f_, t_ = xs
        r, vjp = jax.vjp(_rms, f_, ps[0])
        rows = r0 + lax.broadcasted_iota(I32, (h1_.shape[0], 1), 0)
        real = (rows >= n_meta) & (rows < length)
        err = jnp.where(real, h1_ + r - t_, 0.0)
        dh2 = err * (1.0 / d)
        d_f, dg = vjp(dh2)
        part = jnp.sum(0.5 * jnp.mean(err * err, axis=-1, keepdims=True), axis=0, keepdims=True)
        return [d_f, dh2], [dg, jnp.broadcast_to(part, (1, LANE))]

    d_f, dh2, g_post_ffn, loss_part = _rows("loss_head", loss_head, [h1, f, target], [post_ffn_g], [BF16, F32], [d, LANE])

    d_fin = _matmul("ffn_out_dx", NT, d_f, wfo, pl.BlockSpec((m_rows, d), lambda j: (0, 0)), pl.BlockSpec((None, fs, d), lambda j: (j, 0, 0)),
                    pl.BlockSpec((None, m_rows, fs), lambda j: (j, 0, 0)), jax.ShapeDtypeStruct((half, m_rows, fs), BF16), (half,))
    g_wfo = _matmul("ffn_out_dw", TN, f_in, d_f, pl.BlockSpec((None, m_rows, fs), lambda j: (j, 0, 0)), pl.BlockSpec((m_rows, d), lambda j: (0, 0)),
                    pl.BlockSpec((None, fs, d), lambda j: (j, 0, 0)), jax.ShapeDtypeStruct((half, fs, d), F32), (half,), twin_bf16=True)

    def swiglu_bwd(r0, xs, ps):
        _, vjp = jax.vjp(_swiglu, xs[0].astype(F32), xs[1].astype(F32))
        return [vjp(xs[2].astype(F32))], []

    (d_ab,) = _rowwise("swiglu_bwd", swiglu_bwd, [ab, ab, d_fin], [],
                       [jax.ShapeDtypeStruct((2, half, m_rows, fs), BF16)], [], grid=(half, m_rows // tm),
                       in_specs=[pair(0), pair(half), pair(0)],
                       out_specs=[pl.BlockSpec((2, None, tm, fs), lambda j, i: (0, j, i, 0))], tm=tm, row_axis=1)
    d_ab = d_ab.reshape(N_DEV, m_rows, fs)
    du2 = _matmul("ffn_in_dx", NT, d_ab, wfi, pl.BlockSpec((None, m_rows, fs), lambda i: (i, 0, 0)), pl.BlockSpec((None, d, fs), lambda i: (i, 0, 0)),
                  pl.BlockSpec((m_rows, d), lambda i: (0, 0)), jax.ShapeDtypeStruct((m_rows, d), F32), (N_DEV,), acc_axis=0)
    g_wfi = _matmul("ffn_in_dw", TN, u2, d_ab, pl.BlockSpec((m_rows, d), lambda i: (0, 0)), pl.BlockSpec((None, m_rows, fs), lambda i: (i, 0, 0)),
                    pl.BlockSpec((None, d, fs), lambda i: (i, 0, 0)), jax.ShapeDtypeStruct((N_DEV, d, fs), F32), (N_DEV,), twin_bf16=True)

    def post_mix_bwd(r0, xs, ps):
        h0_, mix_, dh2_, du2_ = xs
        _, vjp = jax.vjp(_post_mix, h0_, mix_, ps[0], ps[1])
        dh0_, dmix_, dg1, dg2 = vjp((dh2_, du2_))
        return [dmix_, dh0_], [dg1, dg2]

    d_mix, dh1, g_post_mix, g_pre_ffn = _rows("post_mix_bwd", post_mix_bwd, [h0, mix, dh2, du2], [post_mix_g, pre_ffn_g],
                                              [BF16, F32], [d, d])
    d_mixin = _dense_dx("mix_out_dx", d_mix, wo, F32)
    g_wo = _dense_dw("mix_out_dw", mixin, d_mix)

    def gate_mix_bwd(r0, xs, ps):
        _, vjp = jax.vjp(_gate_mix, xs[0], xs[1], xs[2], xs[3], ps[0])
        dpgc, dpga, dyc_, dya_, dgb = vjp(xs[4])
        return [dpgc, dpga, dyc_, dya_], [dgb]

    dp_gc, dp_ga, d_yconv, d_yattn, g_gate_b = _rows(
        "gate_mix_bwd", gate_mix_bwd, [p, p, y_conv, y_attn, d_mixin], [gate_b], [BF16] * 4, [2 * d],
        row_in_cols=gate_cols + [None], row_out_widths=[d] * 4)
    d_o = _dense_dx("attn_out_dx", d_yattn, wao, BF16)
    g_wao = _dense_dw("attn_out_dw", o, d_yattn)
    d_yc = _dense_dx("conv_out_dx", d_yconv, wco, F32)
    g_wco = _dense_dw("conv_out_dw", yc, d_yconv)
    big = {"w_ffn_out": [g.reshape(N_DEV, fr, d) for g in g_wfo], "w_ffn_in": g_wfi,
           "w_o": [g.reshape(N_DEV, dc, d) for g in g_wo], "w_attn_out": [g.reshape(N_DEV, dc, d) for g in g_wao],
           "w_conv_out": [g.reshape(N_DEV, dc, d) for g in g_wco]}
    early = ("w_ffn_out", "w_ffn_in", "w_o", "w_attn_out", "w_conv_out")
    dq, dk, dv, received_early = _attn_bwd(p16, d_o, tot, d, [big[k][1] for k in early])

    def conv_norm_bwd(r0, xs, ps):
        _, vjp = jax.vjp(_ln_silu, xs[0], ps[0], ps[1])
        dy_, dg, db = vjp(xs[1])
        return [dy_], [dg, db]

    d_y, g_ln_g, g_ln_b = _rows("conv_norm_bwd", conv_norm_bwd, [y, d_yc], [conv_ln_g, conv_ln_b], [F32], [d, d])
    dp_a, dp_g, g_dww, g_dwb = _conv_bwd(p, d_y, dww_full, d)
    dp = jnp.concatenate([dp_a, dp_g, dq, dk, dv, dp_gc, dp_ga], axis=1)
    g_wi = _matmul("in_proj_dw", TN, u, dp, pl.BlockSpec((m_rows, d), lambda i: (0, 0)), pl.BlockSpec((m_rows, ns), lambda i: (0, i)),
                   pl.BlockSpec((None, d, ns), lambda i: (i, 0, 0)), jax.ShapeDtypeStruct((N_DEV, d, ns), F32), (N_DEV,), twin_bf16=True)
    x_i, y_i, c_i = _mesh_pos()
    slab_idx = jnp.stack([_dev(*chip, c_i) for chip in _other_chips(x_i, y_i)]).astype(I32)
    (wi_sibling,) = _sibling_exchange([g_wi[1]])
    wi_pairs = _pair_sum(slab_idx, g_wi[0], wi_sibling)
    du, wi_chips = _matmul(
        "in_proj_dx", NT, dp, wi, pl.BlockSpec((m_rows, ns), lambda i: (0, i)), pl.BlockSpec((None, d, ns), lambda i: (i, 0, 0)),
        pl.BlockSpec((m_rows, d), lambda i: (0, 0)), jax.ShapeDtypeStruct((m_rows, d), F32), (N_DEV,), acc_axis=0,
        carried=(_chip_copies, [wi_pairs], [jax.ShapeDtypeStruct(wi_pairs.shape, BF16)],
                 [pltpu.SemaphoreType.DMA((1, 3)), pltpu.SemaphoreType.DMA((1, 3))]))

    def pre_mix_bwd(r0, xs, ps):
        _, vjp = jax.vjp(_rms, xs[0], ps[0])
        dx, dg = vjp(xs[1])
        return [xs[2] + dx], [dg]

    dh0, g_pre_mix = _rows("pre_mix_bwd", pre_mix_bwd, [h0, du, dh1], [pre_mix_g], [F32], [d])
    grad_x = dh0[n_meta:length][None]

    me = _dev(x_i, y_i, c_i)
    me_arr = jnp.reshape(me, (1,)).astype(I32)
    big["w_in"] = g_wi
    received = {k: [r] for k, r in zip(early, received_early)}
    received["w_in"] = [wi_sibling[3:4], wi_chips]
    results = {}
    for k in SHARDED:
        outs = _adamw_shard(me_arr, big[k][0], received[k], local[k], given["m_" + k][0], given["v_" + k][0])
        results[k] = tuple(a[None] for a in outs)

    rep_grads = {"pre_mix_g": g_pre_mix, "gate_b": g_gate_b, "dw_b": g_dwb, "conv_ln_g": g_ln_g, "conv_ln_b": g_ln_b,
                 "post_mix_g": g_post_mix, "pre_ffn_g": g_pre_ffn, "post_ffn_g": g_post_ffn}

    def pack_rep(get):
        return jnp.concatenate([get(k) for k in REPLICATED], axis=1).reshape(-1, LANE)

    rep_rows = pack_rep(lambda k: rep_grads[k])
    n_rep = rep_rows.shape[0]
    loss_rows = jnp.broadcast_to(loss_part, (8, LANE))
    g_meta = dh0[0:n_meta]
    slabs = jnp.stack([jnp.concatenate([rep_rows, loss_rows, g_dww[:, j * LANE:(j + 1) * LANE], g_meta[:, j * LANE:(j + 1) * LANE]], axis=0)
                       for j in range(N_DEV)])

    def pack_small(prefix):
        dww_own = jnp.pad(given[prefix + "dw_w"][0], ((0, CONV_PAD - CONV_WIDTH), (0, 0)))
        return jnp.concatenate([pack_rep(lambda k: given[prefix + k]), jnp.zeros((8, LANE), F32), dww_own,
                                given[prefix + "meta_tokens"]], axis=0)

    small = _small_reduce_adamw(slabs, pack_small(""), pack_small("m_"), pack_small("v_"))
    loss = small[0][n_rep, 0]

    def unpack(arr):
        out = {}
        flat = arr[:n_rep].reshape(1, -1)
        off = 0
        for k in REPLICATED:
            w = given[k].shape[1]
            out[k] = flat[:, off:off + w]
            off += w
        out["dw_w"] = arr[n_rep + 8:n_rep + 8 + CONV_WIDTH][None]
        out["meta_tokens"] = arr[n_rep + 8 + CONV_PAD:n_rep + 8 + CONV_PAD + n_meta]
        return out

    small_out = [unpack(a) for a in small]
    for k in WEIGHTS:
        if k not in results:
            results[k] = tuple(s[k] for s in small_out)
    return (loss, grad_x, *[results[k][0] for k in WEIGHTS], *[results[k][1] for k in WEIGHTS],
            *[results[k][2] for k in WEIGHTS], *[results[k][3] for k in WEIGHTS])
```

```python
import jax
import jax.numpy as jnp
from jax import lax
from jax.experimental import pallas as pl
from jax.experimental.pallas import tpu as pltpu

F32 = jnp.float32
BF16 = jnp.bfloat16
I32 = jnp.int32

N_DEV = 8
LANE = 128
HEAD_DIM = 64
QB = 128
KEY_SHIFT = 2
KEY_TILES = 1 << KEY_SHIFT
KEY_CHUNK = KEY_TILES * QB
LANE_BLOCKS = 4
BWD_LANE_BLOCKS = 4
CONV_WIDTH = 31
CONV_PAD = 32
ROW_CHUNK = 128
RMS_EPS = 1e-6
LN_EPS = 1e-5
ADAM_LR = 0.001
ADAM_B1 = 0.9
ADAM_B2 = 0.999
ADAM_EPS = 1e-08
ADAM_WD = 0.01
ADAM_STEP = 10
VMEM_LIMIT = 56 * 1024 * 1024

NN = (((1,), (0,)), ((), ()))
NT = (((1,), (1,)), ((), ()))
TN = (((0,), (0,)), ((), ()))
MESH = pl.DeviceIdType.MESH
ANY = pl.BlockSpec(memory_space=pl.ANY)
VMEM_WHOLE = pl.BlockSpec(memory_space=pltpu.VMEM)


def _params(n_axes):
    return pltpu.CompilerParams(dimension_semantics=("arbitrary",) * n_axes, vmem_limit_bytes=VMEM_LIMIT)


def _row_tile(m):
    assert m % QB == 0
    return m // 4 if m % 64 == 0 else QB


def _matmul(name, dims, a, b, a_spec, b_spec, o_spec, out_shape, grid, acc_axis=None, twin_bf16=False, carried=None):
    n_car = 0 if carried is None else len(carried[1])
    n_twin = 1 if twin_bf16 else 0

    def body(a_ref, b_ref, *rest):
        car_ins, o_ref, twin, rest = rest[:n_car], rest[n_car], rest[n_car + 1:n_car + 1 + n_twin], rest[n_car + 1 + n_twin:]
        if carried is not None:
            starts, relays, waits = carried[0](car_ins, rest[:n_car], *rest[n_car:])

            @pl.when(pl.program_id(0) == 0)
            def _():
                for f in starts:
                    f()

            if relays:
                @pl.when(pl.program_id(0) == (3 * grid[0]) // 4)
                def _():
                    for f in relays:
                        f()

        r = lax.dot_general(a_ref[...], b_ref[...], dims, preferred_element_type=F32)
        if acc_axis is None:
            o_ref[...] = r.astype(o_ref.dtype)
            for t_ref in twin:
                t_ref[...] = r.astype(BF16)
        else:
            k = pl.program_id(acc_axis)

            @pl.when(k == 0)
            def _():
                o_ref[...] = r

            @pl.when(k > 0)
            def _():
                o_ref[...] += r

        if carried is not None:
            @pl.when(pl.program_id(0) == grid[0] - 1)
            def _():
                for f in waits:
                    f()

    in_specs, out_specs, out_shapes, scratch = [a_spec, b_spec], [o_spec], [out_shape], []
    if twin_bf16:
        assert acc_axis is None
        out_specs.append(o_spec)
        out_shapes.append(jax.ShapeDtypeStruct(out_shape.shape, BF16))
    if carried is not None:
        assert len(grid) == 1
        in_specs += [ANY] * n_car
        out_specs += [ANY] * n_car
        out_shapes += list(carried[2])
        scratch = list(carried[3])
    outs = pl.pallas_call(body, name=name, grid=grid, in_specs=in_specs, out_specs=out_specs, out_shape=out_shapes,
                          scratch_shapes=scratch, compiler_params=_params(len(grid)))(a, b, *(carried[1] if carried else ()))
    return outs[0] if len(outs) == 1 else outs


DENSE_TILE = 256


def _dense_fwd(name, a, w, out_dtype=F32):
    m, k = a.shape
    n = w.shape[1]
    tn = DENSE_TILE
    return _matmul(name, NN, a, w, pl.BlockSpec((m, k), lambda j: (0, 0)), pl.BlockSpec((k, tn), lambda j: (0, j)),
                   pl.BlockSpec((m, tn), lambda j: (0, j)), jax.ShapeDtypeStruct((m, n), out_dtype), (n // tn,))


def _dense_dx(name, dy, w, out_dtype):
    m, n = dy.shape
    k = w.shape[0]
    tk = DENSE_TILE
    return _matmul(name, NT, dy, w, pl.BlockSpec((m, n), lambda j: (0, 0)), pl.BlockSpec((tk, n), lambda j: (j, 0)),
                   pl.BlockSpec((m, tk), lambda j: (0, j)), jax.ShapeDtypeStruct((m, k), out_dtype), (k // tk,))


def _dense_dw(name, a, dy):
    m, k = a.shape
    n = dy.shape[1]
    tn = DENSE_TILE
    return _matmul(name, TN, a, dy, pl.BlockSpec((m, k), lambda j: (0, 0)), pl.BlockSpec((m, tn), lambda j: (0, j)),
                   pl.BlockSpec((k, tn), lambda j: (0, j)), jax.ShapeDtypeStruct((k, n), F32), (n // tn,), twin_bf16=True)


def _rowwise(name, fn, row_ins, par_ins, row_outs, par_outs, *, grid, in_specs, out_specs, tm, row_axis):
    n_ri, n_pi, n_ro, n_po = len(row_ins), len(par_ins), len(row_outs), len(par_outs)
    n_steps, tail = divmod(tm, ROW_CHUNK)
    assert tail % 16 == 0

    def body(*refs):
        ri = refs[:n_ri]
        pi = refs[n_ri:n_ri + n_pi]
        ro = refs[n_ri + n_pi:n_ri + n_pi + n_ro]
        po = refs[n_ri + n_pi + n_ro:]
        ps = [r[...] for r in pi]
        base = pl.program_id(row_axis) * tm

        def chunk(r0, rows, carry):
            xs = [r[pl.ds(r0, rows), :] for r in ri]
            outs, pouts = fn(base + r0, xs, ps)
            for r, o in zip(ro, outs):
                if isinstance(o, (list, tuple)):
                    for j, part in enumerate(o):
                        r[j, pl.ds(r0, rows), :] = part.astype(r.dtype)
                else:
                    r[pl.ds(r0, rows), :] = o.astype(r.dtype)
            return tuple(c + q for c, q in zip(carry, pouts))

        def step(i, carry):
            return chunk(pl.multiple_of(i * ROW_CHUNK, ROW_CHUNK), ROW_CHUNK, carry)

        acc = lax.fori_loop(0, n_steps, step, tuple(jnp.zeros(s.shape, F32) for s in par_outs))
        if tail:
            acc = chunk(n_steps * ROW_CHUNK, tail, acc)
        if n_po:
            first = pl.program_id(0) == 0
            for ax in range(1, len(grid)):
                first = first & (pl.program_id(ax) == 0)

            @pl.when(first)
            def _():
                for r in po:
                    r[...] = jnp.zeros_like(r)

            for r, a in zip(po, acc):
                r[...] += a

    return pl.pallas_call(body, name=name, grid=grid, in_specs=in_specs, out_specs=out_specs,
                          out_shape=tuple(row_outs) + tuple(par_outs),
                          compiler_params=_params(len(grid)))(*row_ins, *par_ins)


def _rows(name, fn, row_ins, par_ins, row_out_dtypes, par_out_widths, row_in_cols=None, row_out_widths=None):
    m = row_ins[0].shape[0]
    tm = _row_tile(m)
    in_specs = []
    for k, a in enumerate(row_ins):
        if row_in_cols is not None and row_in_cols[k] is not None:
            width, cb = row_in_cols[k]
            in_specs.append(pl.BlockSpec((tm, width), lambda i, cb=cb: (i, cb)))
        else:
            in_specs.append(pl.BlockSpec((tm, a.shape[1]), lambda i: (i, 0)))
    for a in par_ins:
        in_specs.append(pl.BlockSpec(a.shape, lambda i: (0, 0)))
    if row_out_widths is None:
        row_out_widths = [row_ins[0].shape[1]] * len(row_out_dtypes)
    row_outs = [jax.ShapeDtypeStruct((m, w), dt) for w, dt in zip(row_out_widths, row_out_dtypes)]
    par_outs = [jax.ShapeDtypeStruct((1, w), F32) for w in par_out_widths]
    out_specs = [pl.BlockSpec((tm, s.shape[1]), lambda i: (i, 0)) for s in row_outs]
    out_specs += [pl.BlockSpec(s.shape, lambda i: (0, 0)) for s in par_outs]
    return _rowwise(name, fn, row_ins, par_ins, row_outs, par_outs, grid=(m // tm,), in_specs=in_specs,
                    out_specs=out_specs, tm=tm, row_axis=0)


def _rms(x, g):
    return x * lax.rsqrt(jnp.mean(x * x, axis=-1, keepdims=True) + RMS_EPS) * g


def _ln_silu(y, g, b):
    mu = jnp.mean(y, axis=-1, keepdims=True)
    yc = y - mu
    var = jnp.mean(yc * yc, axis=-1, keepdims=True)
    return jax.nn.silu(yc * lax.rsqrt(var + LN_EPS) * g + b)


def _gate_mix(pgc, pga, yc, ya, gb):
    d = pgc.shape[1]
    return jax.nn.sigmoid(pgc + gb[:, :d]) * yc + jax.nn.sigmoid(pga + gb[:, d:]) * ya


def _post_mix(h0, mix, g_post, g_pre):
    h1 = h0 + _rms(mix, g_post)
    return h1, _rms(h1, g_pre)


def _swiglu(a, b):
    return jax.nn.silu(a) * b


def _conv_taps():
    taps = []
    for b in range(8):
        for a in range(CONV_PAD // 8):
            s = 8 * a + b
            if s < CONV_WIDTH:
                taps.append((b, a, CONV_WIDTH - 1 - s))
    return taps


def _conv_fwd(p, dww, dwb, d_model):
    m = p.shape[0]
    nch = d_model // LANE
    n_chunk = m // QB
    taps = _conv_taps()

    def body(a_ref, g_ref, w_ref, b_ref, y_ref, upad):
        upad[0:CONV_PAD, :] = jnp.zeros((CONV_PAD, LANE), F32)

        def fill(i, c):
            r0 = pl.multiple_of(i * QB, QB)
            u = a_ref[pl.ds(r0, QB), :] * jax.nn.sigmoid(g_ref[pl.ds(r0, QB), :])
            upad[pl.ds(pl.multiple_of(r0 + CONV_PAD, 8), QB), :] = u
            return c

        lax.fori_loop(0, n_chunk, fill, 0)

        def conv(i, c):
            r0 = pl.multiple_of(i * QB, QB)
            win = upad[pl.ds(r0, QB + CONV_PAD), :]
            acc = jnp.broadcast_to(b_ref[...], (QB, LANE))
            rolled = {}
            for b, a, j in taps:
                if b not in rolled:
                    rolled[b] = win if b == 0 else pltpu.roll(win, b, axis=0)
                lo = CONV_PAD - 8 * a
                acc = acc + w_ref[j:j + 1, :] * rolled[b][lo:lo + QB, :]
            y_ref[pl.ds(r0, QB), :] = acc
            return c

        lax.fori_loop(0, n_chunk, conv, 0)

    col = lambda off: pl.BlockSpec((m, LANE), lambda c: (0, off + c))
    return pl.pallas_call(
        body, name="conv_fwd", grid=(nch,),
        in_specs=[col(0), col(nch), pl.BlockSpec((CONV_PAD, LANE), lambda c: (0, c)), pl.BlockSpec((1, LANE), lambda c: (0, c))],
        out_specs=col(0), out_shape=jax.ShapeDtypeStruct((m, d_model), F32),
        scratch_shapes=[pltpu.VMEM((m + CONV_PAD, LANE), F32)], compiler_params=_params(1))(p, p, dww, dwb)


def _conv_bwd(p, dy, dww, d_model):
    m = p.shape[0]
    nch = d_model // LANE
    n_chunk = m // QB
    taps = _conv_taps()
    win_rows = QB + CONV_PAD

    def body(a_ref, g_ref, dy_ref, w_ref, da_ref, dg_ref, dw_ref, db_ref, upad, dypad, wacc, bacc):
        upad[0:CONV_PAD, :] = jnp.zeros((CONV_PAD, LANE), F32)
        dypad[m:m + CONV_PAD, :] = jnp.zeros((CONV_PAD, LANE), F32)
        wacc[...] = jnp.zeros_like(wacc)
        bacc[...] = jnp.zeros_like(bacc)

        def fill(i, c):
            r0 = pl.multiple_of(i * QB, QB)
            u = a_ref[pl.ds(r0, QB), :] * jax.nn.sigmoid(g_ref[pl.ds(r0, QB), :])
            upad[pl.ds(pl.multiple_of(r0 + CONV_PAD, 8), QB), :] = u
            dypad[pl.ds(r0, QB), :] = dy_ref[pl.ds(r0, QB), :]
            return c

        lax.fori_loop(0, n_chunk, fill, 0)

        def chunk(i, c):
            r0 = pl.multiple_of(i * QB, QB)
            dwin = dypad[pl.ds(r0, win_rows), :]
            du = jnp.zeros((QB, LANE), F32)
            rolled = {}
            for b, a, j in taps:
                if b not in rolled:
                    rolled[b] = dwin if b == 0 else pltpu.roll(dwin, win_rows - b, axis=0)
                du = du + w_ref[j:j + 1, :] * rolled[b][8 * a:8 * a + QB, :]
            av = a_ref[pl.ds(r0, QB), :]
            sg = jax.nn.sigmoid(g_ref[pl.ds(r0, QB), :])
            da_ref[pl.ds(r0, QB), :] = (du * sg).astype(da_ref.dtype)
            dg_ref[pl.ds(r0, QB), :] = (du * av * sg * (1.0 - sg)).astype(dg_ref.dtype)
            dyc = dy_ref[pl.ds(r0, QB), :]
            uwin = upad[pl.ds(r0, win_rows), :]
            rolled = {}
            for b, a, j in taps:
                if b not in rolled:
                    rolled[b] = uwin if b == 0 else pltpu.roll(uwin, b, axis=0)
                lo = CONV_PAD - 8 * a
                prod = dyc * rolled[b][lo:lo + QB, :]
                wacc[j] += prod.reshape(QB // 8, 8, LANE).sum(axis=0)
            bacc[...] += dyc.reshape(QB // 8, 8, LANE).sum(axis=0)
            return c

        lax.fori_loop(0, n_chunk, chunk, 0)
        for j in range(CONV_WIDTH):
            dw_ref[j:j + 1, :] = jnp.sum(wacc[j], axis=0, keepdims=True)
        dw_ref[CONV_WIDTH:CONV_PAD, :] = jnp.zeros((CONV_PAD - CONV_WIDTH, LANE), F32)
        db_ref[...] = jnp.sum(bacc[...], axis=0, keepdims=True)

    col = lambda off: pl.BlockSpec((m, LANE), lambda c: (0, off + c))
    return pl.pallas_call(
        body, name="conv_bwd", grid=(nch,),
        in_specs=[col(0), col(nch), col(0), pl.BlockSpec((CONV_PAD, LANE), lambda c: (0, c))],
        out_specs=[col(0), col(0), pl.BlockSpec((CONV_PAD, LANE), lambda c: (0, c)), pl.BlockSpec((1, LANE), lambda c: (0, c))],
        out_shape=(jax.ShapeDtypeStruct((m, d_model), BF16), jax.ShapeDtypeStruct((m, d_model), BF16),
                   jax.ShapeDtypeStruct((CONV_PAD, d_model), F32), jax.ShapeDtypeStruct((1, d_model), F32)),
        scratch_shapes=[pltpu.VMEM((m + CONV_PAD, LANE), F32), pltpu.VMEM((m + CONV_PAD, LANE), F32),
                        pltpu.VMEM((CONV_PAD, 8, LANE), F32), pltpu.VMEM((8, LANE), F32)],
        compiler_params=_params(1))(p, p, dy, dww)


EXP_CLAMP = 80.0


def _one_plus_exp(z):
    return 1.0 + jnp.exp(jnp.minimum(z, EXP_CLAMP))


def _softplus(z):
    return jnp.maximum(jnp.log(_one_plus_exp(z)), z)


def _softplus_sigmoid(z):
    s = _one_plus_exp(z)
    return jnp.maximum(jnp.log(s), z), 1.0 - 1.0 / s


def _tile_cumsums(x, tri2):
    xb = x.astype(BF16)
    out = []
    for i in range(0, x.shape[1] // QB, 2):
        both = jnp.dot(xb[:, i * QB:(i + 2) * QB], tri2, preferred_element_type=F32)
        out += [both[:, :QB], both[:, QB:]]
    return out


def _tri2(kind):
    jj = lax.broadcasted_iota(I32, (2 * QB, 2 * QB), 0)
    ss = lax.broadcasted_iota(I32, (2 * QB, 2 * QB), 1)
    same = (jj >= QB) == (ss >= QB)
    keep = {"ge": jj >= ss, "le": jj <= ss}[kind]
    return jnp.where(same & keep, 1.0, 0.0).astype(BF16)


def _attn_fwd(p, d_model, shards):
    m = p.shape[0]
    nqb = m // QB
    ngrp = d_model // (LANE_BLOCKS * LANE)
    qo, ko, vo = 2 * ngrp, 3 * ngrp, 4 * ngrp
    scale = HEAD_DIM ** -0.5
    n_sh = len(shards)

    assert nqb >= KEY_TILES

    def body(q_ref, k_ref, v_ref, *rest):
        shard_refs, (o_ref, t_ref), rest = rest[:n_sh], rest[n_sh:n_sh + 2], rest[n_sh + 2:]
        gathered_refs, (acc_ref, car_ref), sems = rest[:n_sh], rest[n_sh:n_sh + 2], rest[n_sh + 2:]
        starts, relays, waits = _gather_copies(shard_refs, gathered_refs, *sems)

        @pl.when((pl.program_id(0) == 0) & (pl.program_id(1) == 0))
        def _():
            for f in starts:
                f()

        @pl.when((pl.program_id(0) == ngrp - 1) & (pl.program_id(1) == (5 * nqb) // 8))
        def _():
            for f in relays:
                f()

        qb = pl.program_id(1)
        lane = lax.broadcasted_iota(I32, (QB, LANE), 1)
        head0 = lane < HEAD_DIM
        row_g = qb * QB + lax.broadcasted_iota(I32, (QB, KEY_CHUNK), 0)
        col_l = lax.broadcasted_iota(I32, (QB, KEY_CHUNK), 1)
        tri = _tri2("ge")
        heads = range(2 * LANE_BLOCKS)
        qh = []
        for lb in range(LANE_BLOCKS):
            q2 = (q_ref[:, lb * LANE:(lb + 1) * LANE] * scale).astype(BF16)
            zero = jnp.zeros_like(q2)
            qh += [jnp.where(head0, q2, zero), jnp.where(head0, zero, q2)]
        acc_ref[...] = jnp.zeros_like(acc_ref)
        car_ref[...] = jnp.zeros_like(car_ref)

        def chunk(first_tile, bound, n_tiles=KEY_TILES):
            r0 = pl.multiple_of(first_tile * QB, QB)
            keys = n_tiles * QB
            kcs = [k_ref[pl.ds(r0, keys), lb * LANE:(lb + 1) * LANE].astype(BF16) for lb in range(LANE_BLOCKS)]
            vcs = [v_ref[pl.ds(r0, keys), lb * LANE:(lb + 1) * LANE].astype(BF16) for lb in range(LANE_BLOCKS)]
            valid = None if bound is None else (col_l[:, :keys] + r0) < bound
            zs = [lax.dot_general(qh[h], kcs[h // 2], NT, preferred_element_type=F32) for h in heads]
            sps = [_softplus(z) for z in zs]
            if valid is not None:
                sps = [jnp.where(valid, sp, 0.0) for sp in sps]
            cums = [_tile_cumsums(sp, tri) for sp in sps]
            cars = [car_ref[h] for h in heads]
            a_tiles = [[None] * n_tiles for h in heads]
            for i in reversed(range(n_tiles)):
                for h in heads:
                    cum = cums[h][i]
                    a_tiles[h][i] = jnp.exp(zs[h][:, i * QB:(i + 1) * QB] - (cum + cars[h]))
                    cars[h] = cars[h] + jnp.broadcast_to(cum[:, 0:1], cum.shape)
            for h in heads:
                a = jnp.concatenate(a_tiles[h], axis=1)
                if valid is not None:
                    a = jnp.where(valid, a, 0.0)
                acc_ref[h] += jnp.dot(a.astype(BF16), vcs[h // 2], preferred_element_type=F32)
                car_ref[h] = cars[h]

        near = jnp.maximum(qb - (KEY_TILES - 1), 0)
        chunk(near, row_g)
        n_full = lax.shift_right_logical(near, KEY_SHIFT)

        def step(i, c):
            chunk(near - KEY_TILES * (i + 1), None)
            return c

        lax.fori_loop(0, n_full, step, 0)
        left = near - KEY_TILES * n_full

        @pl.when((left > 0) & (left <= 2))
        def _():
            chunk(0, left * QB, n_tiles=2)

        @pl.when(left > 2)
        def _():
            chunk(0, left * QB)

        for lb in range(LANE_BLOCKS):
            o_ref[:, lb * LANE:(lb + 1) * LANE] = jnp.where(head0, acc_ref[2 * lb], acc_ref[2 * lb + 1]).astype(o_ref.dtype)
        for h in heads:
            t_ref[:, h * QB:(h + 1) * QB] = car_ref[h]

        @pl.when((pl.program_id(0) == ngrp - 1) & (pl.program_id(1) == nqb - 1))
        def _():
            for f in waits:
                f()

    wide = LANE_BLOCKS * LANE
    outs = pl.pallas_call(
        body, name="attn_fwd", grid=(ngrp, nqb),
        in_specs=[pl.BlockSpec((QB, wide), lambda g, qb: (qb, qo + g)),
                  pl.BlockSpec((m, wide), lambda g, qb: (0, ko + g)),
                  pl.BlockSpec((m, wide), lambda g, qb: (0, vo + g))] + [ANY] * n_sh,
        out_specs=[pl.BlockSpec((QB, wide), lambda g, qb: (qb, g)),
                   pl.BlockSpec((QB, 2 * wide), lambda g, qb: (qb, g))] + [ANY] * n_sh,
        out_shape=[jax.ShapeDtypeStruct((m, d_model), BF16), jax.ShapeDtypeStruct((m, 2 * d_model), F32)]
        + [jax.ShapeDtypeStruct((N_DEV,) + s.shape, s.dtype) for s in shards],
        scratch_shapes=[pltpu.VMEM((2 * LANE_BLOCKS, QB, LANE), F32), pltpu.VMEM((2 * LANE_BLOCKS, QB, LANE), F32)]
        + _exchange_sems(n_sh) + [pltpu.SemaphoreType.DMA((n_sh,))],
        compiler_params=_params(2))(p, p, p, *shards)
    return outs[0], outs[1], outs[2:]


def _attn_bwd(p, d_o, tot, d_model, slabs):
    m = p.shape[0]
    nqb = m // QB
    blocks = BWD_LANE_BLOCKS
    ngrp = d_model // (blocks * LANE)
    qo, ko, vo = 2 * ngrp, 3 * ngrp, 4 * ngrp
    scale = HEAD_DIM ** -0.5
    n_sl = len(slabs)

    assert nqb >= KEY_TILES

    def body(q_ref, k_ref, v_ref, do_ref, t_ref, *rest):
        slab_refs, (dq_ref, dk_ref, dv_ref), rest = rest[:n_sl], rest[n_sl:n_sl + 3], rest[n_sl + 3:]
        recv_refs, (dkacc, dvacc, dqacc, csp, cg), sems = rest[:n_sl], rest[n_sl:n_sl + 5], rest[n_sl + 5:]
        starts, waits = _scatter_copies(slab_refs, recv_refs, *sems)

        @pl.when((pl.program_id(0) == 0) & (pl.program_id(1) == 0))
        def _():
            for f in starts:
                f()

        qb = pl.program_id(1)
        lane = lax.broadcasted_iota(I32, (QB, LANE), 1)
        head0 = lane < HEAD_DIM
        row_g = qb * QB + lax.broadcasted_iota(I32, (QB, KEY_CHUNK), 0)
        col_l = lax.broadcasted_iota(I32, (QB, KEY_CHUNK), 1)
        tri_ge = _tri2("ge")
        tri_le = _tri2("le")
        heads = range(2 * blocks)
        qh, doh = [], []
        for lb in range(blocks):
            q2 = (q_ref[:, lb * LANE:(lb + 1) * LANE] * scale).astype(BF16)
            do2 = do_ref[:, lb * LANE:(lb + 1) * LANE]
            zero = jnp.zeros_like(q2)
            qh += [jnp.where(head0, q2, zero), jnp.where(head0, zero, q2)]
            doh += [jnp.where(head0, do2, zero), jnp.where(head0, zero, do2)]
        q_pairs = [jnp.concatenate(qh[2 * lb:2 * lb + 2], axis=0) for lb in range(blocks)]
        do_pairs = [jnp.concatenate(doh[2 * lb:2 * lb + 2], axis=0) for lb in range(blocks)]

        @pl.when(qb == 0)
        def _():
            dkacc[...] = jnp.zeros_like(dkacc)
            dvacc[...] = jnp.zeros_like(dvacc)

        dqacc[...] = jnp.zeros_like(dqacc)
        for h in heads:
            csp[h] = t_ref[:, h * QB:(h + 1) * QB]
        cg[...] = jnp.zeros_like(cg)

        def chunk(first_tile, bound, n_tiles=KEY_TILES):
            r0 = pl.multiple_of(first_tile * QB, QB)
            keys = n_tiles * QB
            kcs = [k_ref[pl.ds(r0, keys), lb * LANE:(lb + 1) * LANE].astype(BF16) for lb in range(blocks)]
            vcs = [v_ref[pl.ds(r0, keys), lb * LANE:(lb + 1) * LANE].astype(BF16) for lb in range(blocks)]
            valid = None if bound is None else (col_l[:, :keys] + r0) < bound
            tiles = [slice(i * QB, (i + 1) * QB) for i in range(n_tiles)]
            zs = [lax.dot_general(qh[h], kcs[h // 2], NT, preferred_element_type=F32) for h in heads]
            das = [lax.dot_general(doh[h], vcs[h // 2], NT, preferred_element_type=F32) for h in heads]
            sps, sgs = zip(*[_softplus_sigmoid(z) for z in zs])
            if valid is not None:
                sps = [jnp.where(valid, sp, 0.0) for sp in sps]
            cums = [_tile_cumsums(sp, tri_ge) for sp in sps]
            a_tiles, g_tiles = [[] for h in heads], [[] for h in heads]
            for h in heads:
                rest = csp[h]
                for i, c in enumerate(tiles):
                    cum = cums[h][i]
                    rest = rest - jnp.broadcast_to(cum[:, 0:1], cum.shape)
                    a = jnp.exp(zs[h][:, c] - (cum + rest))
                    if valid is not None:
                        a = jnp.where(valid[:, c], a, 0.0)
                    a_tiles[h].append(a)
                    g_tiles[h].append(a * das[h][:, c])
                csp[h] = rest
            gcums = [_tile_cumsums(jnp.concatenate(g_tiles[h], axis=1), tri_le) for h in heads]
            dzbs, abs_ = [], []
            for h in heads:
                g_before = cg[h]
                dz_tiles = []
                for i, c in enumerate(tiles):
                    gcum = gcums[h][i]
                    dz = g_tiles[h][i] - sgs[h][:, c] * (g_before + gcum)
                    if valid is not None:
                        dz = jnp.where(valid[:, c], dz, 0.0)
                    dz_tiles.append(dz)
                    g_before = g_before + jnp.broadcast_to(gcum[:, QB - 1:QB], gcum.shape)
                cg[h] = g_before
                dzbs.append(jnp.concatenate(dz_tiles, axis=1).astype(BF16))
                abs_.append(jnp.concatenate(a_tiles[h], axis=1).astype(BF16))
            for h in heads:
                dqacc[h] += jnp.dot(dzbs[h], kcs[h // 2], preferred_element_type=F32)
            for lb in range(blocks):
                dz_pair = jnp.concatenate(dzbs[2 * lb:2 * lb + 2], axis=0)
                a_pair = jnp.concatenate(abs_[2 * lb:2 * lb + 2], axis=0)
                dkacc[pl.ds(r0, keys), lb * LANE:(lb + 1) * LANE] += lax.dot_general(
                    dz_pair, q_pairs[lb], TN, preferred_element_type=F32)
                dvacc[pl.ds(r0, keys), lb * LANE:(lb + 1) * LANE] += lax.dot_general(
                    a_pair, do_pairs[lb], TN, preferred_element_type=F32)

        near = jnp.maximum(qb - (KEY_TILES - 1), 0)
        n_full = lax.shift_right_logical(near, KEY_SHIFT)

        def step(i, c):
            chunk(KEY_TILES * i, None)
            return c

        lax.fori_loop(0, n_full, step, 0)
        left = near - KEY_TILES * n_full

        @pl.when((left > 0) & (left <= 2))
        def _():
            chunk(KEY_TILES * n_full, near * QB, n_tiles=2)

        @pl.when(left > 2)
        def _():
            chunk(KEY_TILES * n_full, near * QB)

        chunk(near, row_g)
        for lb in range(blocks):
            dq2 = jnp.where(head0, dqacc[2 * lb], dqacc[2 * lb + 1]) * scale
            dq_ref[:, lb * LANE:(lb + 1) * LANE] = dq2.astype(dq_ref.dtype)

        @pl.when(qb == nqb - 1)
        def _():
            dk_ref[...] = dkacc[...].astype(dk_ref.dtype)
            dv_ref[...] = dvacc[...].astype(dv_ref.dtype)

        @pl.when((pl.program_id(0) == ngrp - 1) & (qb == nqb - 1))
        def _():
            for f in waits:
                f()

    out = jax.ShapeDtypeStruct((m, d_model), BF16)
    wide = blocks * LANE
    carry = pltpu.VMEM((2 * blocks, QB, LANE), F32)
    outs = pl.pallas_call(
        body, name="attn_bwd", grid=(ngrp, nqb),
        in_specs=[pl.BlockSpec((QB, wide), lambda g, qb: (qb, qo + g)),
                  pl.BlockSpec((m, wide), lambda g, qb: (0, ko + g)),
                  pl.BlockSpec((m, wide), lambda g, qb: (0, vo + g)),
                  pl.BlockSpec((QB, wide), lambda g, qb: (qb, g)),
                  pl.BlockSpec((QB, 2 * wide), lambda g, qb: (qb, g))] + [ANY] * n_sl,
        out_specs=[pl.BlockSpec((QB, wide), lambda g, qb: (qb, g)),
                   pl.BlockSpec((m, wide), lambda g, qb: (0, g)),
                   pl.BlockSpec((m, wide), lambda g, qb: (0, g))] + [ANY] * n_sl,
        out_shape=[out, out, out] + _received_shapes(slabs),
        scratch_shapes=[pltpu.VMEM((m, wide), F32), pltpu.VMEM((m, wide), F32), carry, carry, carry] + _exchange_sems(n_sl),
        compiler_params=_params(2))(p, p, p, d_o, tot, *slabs)
    return outs[0], outs[1], outs[2], outs[3:]


def _mesh_pos():
    return lax.axis_index("x"), lax.axis_index("y"), lax.axis_index("c")


def _other_chips(x, y):
    return [(1 - x, y), (x, 1 - y), (1 - x, 1 - y)]


def _dev(x, y, c):
    return 4 * x + 2 * y + c


def _all_gather(shards):
    n = len(shards)

    def body(*refs):
        starts, relays, waits = _gather_copies(refs[:n], refs[n:2 * n], *refs[2 * n:])
        for f in starts + relays + waits:
            f()

    return pl.pallas_call(
        body, name="comm_all_gather", in_specs=[ANY] * n, out_specs=[ANY] * n,
        out_shape=[jax.ShapeDtypeStruct((N_DEV,) + s.shape, s.dtype) for s in shards],
        scratch_shapes=[pltpu.SemaphoreType.DMA((n, 7)), pltpu.SemaphoreType.DMA((n, 7)), pltpu.SemaphoreType.DMA((n,))],
    )(*shards)


def _peers(x, y, c):
    out = []
    for mask in range(1, N_DEV):
        px, py, pc = x ^ (mask >> 2), y ^ ((mask >> 1) & 1), c ^ (mask & 1)
        out.append((mask - 1, (px, py, pc), _dev(px, py, pc)))
    return out


def _remote(src, dst, send_sems, recv_sems, k, s, peer):
    return pltpu.make_async_remote_copy(src_ref=src, dst_ref=dst, send_sem=send_sems.at[k, s], recv_sem=recv_sems.at[k, s],
                                        device_id=peer, device_id_type=MESH)


def _gather_copies(ins, outs, send_sems, recv_sems, local_sems):
    x, y, c = _mesh_pos()
    sibling = (x, y, 1 - c)
    chips = _other_chips(x, y)
    starts, relays, waits = [], [], []
    for k in range(len(ins)):
        def slot(block, k=k):
            return outs[k].at[_dev(*block)]

        def copy(s, src, block, to, k=k):
            return _remote(src, slot(block), send_sems, recv_sems, k, s, to)

        own = pltpu.make_async_copy(ins[k], slot((x, y, c)), local_sems.at[k])
        to_sibling = copy(0, ins[k], (x, y, c), sibling)
        starts += [own.start, to_sibling.start]
        waits += [own.wait, to_sibling.wait_send, copy(0, ins[k], (x, y, 1 - c), sibling).wait_recv]
        for j, chip in enumerate(chips):
            out = copy(1 + j, ins[k], (x, y, c), (*chip, c))
            relay = copy(4 + j, slot((*chip, c)), (*chip, c), sibling)
            starts.append(out.start)
            relays += [copy(1 + j, ins[k], (*chip, c), sibling).wait_recv, relay.start]
            waits += [out.wait_send, relay.wait_send, copy(4 + j, ins[k], (*chip, 1 - c), sibling).wait_recv]
    return starts, relays, waits


def _scatter_copies(ins, outs, send_sems, recv_sems):
    x, y, c = _mesh_pos()
    starts, waits = [], []
    for k in range(len(ins)):
        for s, peer, idx in _peers(x, y, c):
            send = _remote(ins[k].at[idx], outs[k].at[s], send_sems, recv_sems, k, s, peer)
            starts.append(send.start)
            waits += [send.wait_recv, send.wait_send]
    return starts, waits


def _exchange_sems(n):
    return [pltpu.SemaphoreType.DMA((n, N_DEV - 1)), pltpu.SemaphoreType.DMA((n, N_DEV - 1))]


def _received_shapes(slabs):
    return [jax.ShapeDtypeStruct((N_DEV - 1,) + a.shape[1:], a.dtype) for a in slabs]


def _chips_and_own(x, y):
    return _other_chips(x, y) + [(x, y)]


def _sibling_exchange(slabs):
    n = len(slabs)

    def body(*refs):
        ins, outs, (send_sems, recv_sems) = refs[:n], refs[n:2 * n], refs[2 * n:]
        x, y, c = _mesh_pos()
        copies = [_remote(ins[k].at[_dev(*chip, 1 - c)], outs[k].at[r], send_sems, recv_sems, k, r, (x, y, 1 - c))
                  for k in range(n) for r, chip in enumerate(_chips_and_own(x, y))]
        for cp in copies:
            cp.start()
        for cp in copies:
            cp.wait_recv()
        for cp in copies:
            cp.wait_send()

    return pl.pallas_call(body, name="comm_rs_sibling", in_specs=[ANY] * n, out_specs=[ANY] * n,
                          out_shape=[jax.ShapeDtypeStruct((4,) + a.shape[1:], a.dtype) for a in slabs],
                          scratch_shapes=[pltpu.SemaphoreType.DMA((n, 4)), pltpu.SemaphoreType.DMA((n, 4))])(*slabs)


def _chip_copies(ins, outs, send_sems, recv_sems):
    x, y, c = _mesh_pos()
    starts, waits = [], []
    for k in range(len(ins)):
        for r, chip in enumerate(_other_chips(x, y)):
            cp = _remote(ins[k].at[r], outs[k].at[r], send_sems, recv_sems, k, r, (*chip, c))
            starts.append(cp.start)
            waits += [cp.wait_recv, cp.wait_send]
    return starts, [], waits


def _pair_sum(slab_idx, grad, from_sibling):
    _, rows, cols = grad.shape
    tr = _shard_tile(rows)

    def body(idx_ref, g_ref, s_ref, o_ref):
        o_ref[...] = (g_ref[...] + s_ref[...].astype(F32)).astype(o_ref.dtype)

    gs = pltpu.PrefetchScalarGridSpec(
        num_scalar_prefetch=1, grid=(3, rows // tr),
        in_specs=[pl.BlockSpec((None, tr, cols), lambda r, i, idx: (idx[r], i, 0)),
                  pl.BlockSpec((None, tr, cols), lambda r, i, idx: (r, i, 0))],
        out_specs=pl.BlockSpec((None, tr, cols), lambda r, i, idx: (r, i, 0)))
    return pl.pallas_call(body, name="rs_pair_sum", grid_spec=gs, out_shape=jax.ShapeDtypeStruct((3, rows, cols), BF16),
                          compiler_params=_params(2))(slab_idx, grad, from_sibling)


def _shard_tile(rows):
    for tr in range(min(rows, 352), 0, -1):
        if rows % tr == 0 and (tr % 16 == 0 or tr == rows):
            return tr


def _adamw_math(w, g, m, v):
    m = ADAM_B1 * m + (1.0 - ADAM_B1) * g
    v = ADAM_B2 * v + (1.0 - ADAM_B2) * (g * g)
    m_hat = m / (1.0 - ADAM_B1 ** ADAM_STEP)
    v_hat = v / (1.0 - ADAM_B2 ** ADAM_STEP)
    delta = -ADAM_LR * (m_hat / (jnp.sqrt(v_hat) + ADAM_EPS) + ADAM_WD * w)
    return delta, m, v


def _adamw_shard(me, grad, received, w, m, v):
    rows, cols = w.shape
    tr = _shard_tile(rows)
    n_rec = len(received)

    def body(me_ref, g_ref, *rest):
        r_refs, (w_ref, m_ref, v_ref, go_ref, do_ref, mo_ref, vo_ref) = rest[:n_rec], rest[n_rec:]
        g = g_ref[...]
        for r_ref in r_refs:
            for s in range(r_ref.shape[0]):
                g = g + r_ref[s].astype(F32)
        delta, m_new, v_new = _adamw_math(w_ref[...], g, m_ref[...], v_ref[...])
        go_ref[...] = g
        do_ref[...] = delta
        mo_ref[...] = m_new
        vo_ref[...] = v_new

    flat = pl.BlockSpec((tr, cols), lambda i, me: (i, 0))
    gs = pltpu.PrefetchScalarGridSpec(
        num_scalar_prefetch=1, grid=(rows // tr,),
        in_specs=[pl.BlockSpec((None, tr, cols), lambda i, me: (me[0], i, 0))]
        + [pl.BlockSpec((r.shape[0], tr, cols), lambda i, me: (0, i, 0)) for r in received] + [flat, flat, flat],
        out_specs=[flat, flat, flat, flat])
    out = jax.ShapeDtypeStruct((rows, cols), F32)
    return pl.pallas_call(body, name="adamw_shard", grid_spec=gs, out_shape=(out, out, out, out),
                          compiler_params=_params(1))(me, grad, *received, w, m, v)


def _small_reduce_adamw(slabs, w, m, v):
    _, rows, _ = slabs.shape

    def body(s_ref, w_ref, m_ref, v_ref, g_ref, d_ref, mo_ref, vo_ref, land, send_sems, recv_sems):
        x, y, c = _mesh_pos()
        me = _dev(x, y, c)
        copies = []
        for mask in range(1, N_DEV):
            px, py, pc = x ^ (mask >> 2), y ^ ((mask >> 1) & 1), c ^ (mask & 1)
            copies.append(pltpu.make_async_remote_copy(
                src_ref=s_ref.at[_dev(px, py, pc)], dst_ref=land.at[me], send_sem=send_sems.at[mask - 1],
                recv_sem=recv_sems.at[mask - 1], device_id=(px, py, pc), device_id_type=MESH))
        for cp in copies:
            cp.start()
        land[me] = s_ref[me]
        for mask in range(1, N_DEV):
            px, py, pc = x ^ (mask >> 2), y ^ ((mask >> 1) & 1), c ^ (mask & 1)
            pltpu.make_async_remote_copy(
                src_ref=s_ref.at[me], dst_ref=land.at[_dev(px, py, pc)], send_sem=send_sems.at[mask - 1],
                recv_sem=recv_sems.at[mask - 1], device_id=(px, py, pc), device_id_type=MESH).wait_recv()
        for cp in copies:
            cp.wait_send()
        g = land[0]
        for d in range(1, N_DEV):
            g = g + land[d]
        delta, m_new, v_new = _adamw_math(w_ref[...], g, m_ref[...], v_ref[...])
        g_ref[...] = g
        d_ref[...] = delta
        mo_ref[...] = m_new
        vo_ref[...] = v_new

    out = jax.ShapeDtypeStruct((rows, LANE), F32)
    return pl.pallas_call(
        body, name="comm_small_reduce_adamw", in_specs=[VMEM_WHOLE] * 4, out_specs=[VMEM_WHOLE] * 4, out_shape=(out, out, out, out),
        scratch_shapes=[pltpu.VMEM((N_DEV, rows, LANE), F32), pltpu.SemaphoreType.DMA((N_DEV - 1,)),
                        pltpu.SemaphoreType.DMA((N_DEV - 1,))],
    )(slabs, w, m, v)


def _cast_bf16(arrs):
    n = len(arrs)

    def body(*refs):
        for i_ref, o_ref in zip(refs[:n], refs[n:]):
            o_ref[...] = i_ref[...].astype(BF16)

    return pl.pallas_call(body, name="cast_bf16", in_specs=[VMEM_WHOLE] * n, out_specs=[VMEM_WHOLE] * n,
                          out_shape=[jax.ShapeDtypeStruct(a.shape, BF16) for a in arrs],
                          compiler_params=pltpu.CompilerParams(vmem_limit_bytes=VMEM_LIMIT))(*arrs)


REPLICATED = ("pre_mix_g", "gate_b", "dw_b", "conv_ln_g", "conv_ln_b", "post_mix_g", "pre_ffn_g", "post_ffn_g")
SHARDED = ("w_in", "w_conv_out", "w_attn_out", "w_o", "w_ffn_in", "w_ffn_out")
WEIGHTS = ("meta_tokens", "pre_mix_g", "w_in", "gate_b", "dw_w", "dw_b", "conv_ln_g", "conv_ln_b", "w_conv_out",
           "w_attn_out", "w_o", "post_mix_g", "pre_ffn_g", "w_ffn_in", "w_ffn_out", "post_ffn_g")


def kernel(x, meta_tokens, pre_mix_g, w_in, gate_b, dw_w, dw_b, conv_ln_g, conv_ln_b, w_conv_out, w_attn_out, w_o, post_mix_g, pre_ffn_g, w_ffn_in, w_ffn_out, post_ffn_g, loss_target, m_meta_tokens, m_pre_mix_g, m_w_in, m_gate_b, m_dw_w, m_dw_b, m_conv_ln_g, m_conv_ln_b, m_w_conv_out, m_w_attn_out, m_w_o, m_post_mix_g, m_pre_ffn_g, m_w_ffn_in, m_w_ffn_out, m_post_ffn_g, v_meta_tokens, v_pre_mix_g, v_w_in, v_gate_b, v_dw_w, v_dw_b, v_conv_ln_g, v_conv_ln_b, v_w_conv_out, v_w_attn_out, v_w_o, v_post_mix_g, v_pre_ffn_g, v_w_ffn_in, v_w_ffn_out, v_post_ffn_g):
    given = dict(locals())
    seq, d = x.shape[1], x.shape[2]
    n_meta = meta_tokens.shape[0]
    length = n_meta + seq
    m_rows = -(-length // QB) * QB
    dc = d // N_DEV
    assert dc == LANE and n_meta % 8 == 0 and seq % 8 == 0
    fs = w_ffn_in.shape[2]
    fr = w_ffn_out.shape[1]
    assert 2 * fr == fs

    local = {k: given[k][0] for k in SHARDED}
    cast = _cast_bf16([local[k] for k in SHARDED])
    dww_pad = jnp.pad(dw_w[0], ((0, CONV_PAD - CONV_WIDTH), (0, 0)))
    wi, meta_g, dww_g = _all_gather([cast[0], meta_tokens, dww_pad])
    meta_full = jnp.concatenate([meta_g[j] for j in range(N_DEV)], axis=1)
    dww_full = jnp.concatenate([dww_g[j] for j in range(N_DEV)], axis=1)
    ns = wi.shape[2]

    tail = jnp.zeros((m_rows - length, d), F32)
    h0 = jnp.concatenate([meta_full, x[0], tail], axis=0)
    target = jnp.concatenate([jnp.zeros((n_meta, d), F32), loss_target[0], tail], axis=0)

    (u,) = _rows("pre_mix_norm", lambda r0, xs, ps: ([_rms(xs[0], ps[0])], []), [h0], [pre_mix_g], [BF16], [])
    square = list(cast[1:4])
    p, p16, *gathered = _matmul(
        "in_proj", NN, u, wi, pl.BlockSpec((m_rows, d), lambda i: (0, 0)), pl.BlockSpec((None, d, ns), lambda i: (i, 0, 0)),
        pl.BlockSpec((m_rows, ns), lambda i: (0, i)), jax.ShapeDtypeStruct((m_rows, N_DEV * ns), F32), (N_DEV,), twin_bf16=True,
        carried=(_gather_copies, square, [jax.ShapeDtypeStruct((N_DEV,) + s.shape, s.dtype) for s in square],
                 _exchange_sems(len(square)) + [pltpu.SemaphoreType.DMA((len(square),))]))
    wco, wao, wo = (g.reshape(d, d) for g in gathered)
    o, tot, (wfi, wfo) = _attn_fwd(p16, d, list(cast[4:6]))
    wfo = wfo.reshape(N_DEV // 2, fs, d)
    y = _conv_fwd(p, dww_full, dw_b, d)
    (yc,) = _rows("conv_norm", lambda r0, xs, ps: ([_ln_silu(xs[0], ps[0], ps[1])], []), [y], [conv_ln_g, conv_ln_b], [BF16], [])
    y_conv = _dense_fwd("conv_out", yc, wco)
    y_attn = _dense_fwd("attn_out", o, wao)
    gate_cols = [(d, 5), (d, 6), None, None]
    (mixin,) = _rows("gate_mix", lambda r0, xs, ps: ([_gate_mix(*xs, ps[0])], []), [p, p, y_conv, y_attn], [gate_b], [BF16], [],
                     row_in_cols=gate_cols, row_out_widths=[d])
    mix = _dense_fwd("mix_out", mixin, wo)
    h1, u2 = _rows("post_mix", lambda r0, xs, ps: (list(_post_mix(xs[0], xs[1], ps[0], ps[1])), []), [h0, mix],
                   [post_mix_g, pre_ffn_g], [F32, BF16], [])
    ab = _matmul("ffn_in", NN, u2, wfi, pl.BlockSpec((m_rows, d), lambda i: (0, 0)), pl.BlockSpec((None, d, fs), lambda i: (i, 0, 0)),
                 pl.BlockSpec((None, m_rows, fs), lambda i: (i, 0, 0)), jax.ShapeDtypeStruct((N_DEV, m_rows, fs), BF16), (N_DEV,))
    half = N_DEV // 2
    tm = _row_tile(m_rows)
    pair = lambda off: pl.BlockSpec((None, tm, fs), lambda j, i, off=off: (j + off, i, 0))
    (f_in,) = _rowwise("swiglu", lambda r0, xs, ps: ([_swiglu(xs[0].astype(F32), xs[1].astype(F32))], []), [ab, ab], [],
                       [jax.ShapeDtypeStruct((half, m_rows, fs), BF16)], [], grid=(half, m_rows // tm),
                       in_specs=[pair(0), pair(half)], out_specs=[pair(0)], tm=tm, row_axis=1)
    f = _matmul("ffn_out", NN, f_in, wfo, pl.BlockSpec((None, m_rows, fs), lambda j: (j, 0, 0)), pl.BlockSpec((None, fs, d), lambda j: (j, 0, 0)),
                pl.BlockSpec((m_rows, d), lambda j: (0, 0)), jax.ShapeDtypeStruct((m_rows, d), F32), (half,), acc_axis=0)

    def loss_head(r0, xs, ps):
        h1_, f_, t_ = xs
        r, vjp = jax.vjp(_rms, f_, ps[0])
        rows = r0 + lax.broadcasted_iota(I32, (h1_.shape[0], 1), 0)
        real = (rows >= n_meta) & (rows < length)
        err = jnp.where(real, h1_ + r - t_, 0.0)
        dh2 = err * (1.0 / d)
        d_f, dg = vjp(dh2)
        part = jnp.sum(0.5 * jnp.mean(err * err, axis=-1, keepdims=True), axis=0, keepdims=True)
        return [d_f, dh2], [dg, jnp.broadcast_to(part, (1, LANE))]

    d_f, dh2, g_post_ffn, loss_part = _rows("loss_head", loss_head, [h1, f, target], [post_ffn_g], [BF16, F32], [d, LANE])

    d_fin = _matmul("ffn_out_dx", NT, d_f, wfo, pl.BlockSpec((m_rows, d), lambda j: (0, 0)), pl.BlockSpec((None, fs, d), lambda j: (j, 0, 0)),
                    pl.BlockSpec((None, m_rows, fs), lambda j: (j, 0, 0)), jax.ShapeDtypeStruct((half, m_rows, fs), BF16), (half,))
    g_wfo = _matmul("ffn_out_dw", TN, f_in, d_f, pl.BlockSpec((None, m_rows, fs), lambda j: (j, 0, 0)), pl.BlockSpec((m_rows, d), lambda j: (0, 0)),
                    pl.BlockSpec((None, fs, d), lambda j: (j, 0, 0)), jax.ShapeDtypeStruct((half, fs, d), F32), (half,), twin_bf16=True)

    def swiglu_bwd(r0, xs, ps):
        _, vjp = jax.vjp(_swiglu, xs[0].astype(F32), xs[1].astype(F32))
        return [vjp(xs[2].astype(F32))], []

    (d_ab,) = _rowwise("swiglu_bwd", swiglu_bwd, [ab, ab, d_fin], [],
                       [jax.ShapeDtypeStruct((2, half, m_rows, fs), BF16)], [], grid=(half, m_rows // tm),
                       in_specs=[pair(0), pair(half), pair(0)],
                       out_specs=[pl.BlockSpec((2, None, tm, fs), lambda j, i: (0, j, i, 0))], tm=tm, row_axis=1)
    d_ab = d_ab.reshape(N_DEV, m_rows, fs)
    du2 = _matmul("ffn_in_dx", NT, d_ab, wfi, pl.BlockSpec((None, m_rows, fs), lambda i: (i, 0, 0)), pl.BlockSpec((None, d, fs), lambda i: (i, 0, 0)),
                  pl.BlockSpec((m_rows, d), lambda i: (0, 0)), jax.ShapeDtypeStruct((m_rows, d), F32), (N_DEV,), acc_axis=0)
    g_wfi = _matmul("ffn_in_dw", TN, u2, d_ab, pl.BlockSpec((m_rows, d), lambda i: (0, 0)), pl.BlockSpec((None, m_rows, fs), lambda i: (i, 0, 0)),
                    pl.BlockSpec((None, d, fs), lambda i: (i, 0, 0)), jax.ShapeDtypeStruct((N_DEV, d, fs), F32), (N_DEV,), twin_bf16=True)

    def post_mix_bwd(r0, xs, ps):
        h0_, mix_, dh2_, du2_ = xs
        _, vjp = jax.vjp(_post_mix, h0_, mix_, ps[0], ps[1])
        dh0_, dmix_, dg1, dg2 = vjp((dh2_, du2_))
        return [dmix_, dh0_], [dg1, dg2]

    d_mix, dh1, g_post_mix, g_pre_ffn = _rows("post_mix_bwd", post_mix_bwd, [h0, mix, dh2, du2], [post_mix_g, pre_ffn_g],
                                              [BF16, F32], [d, d])
    d_mixin = _dense_dx("mix_out_dx", d_mix, wo, F32)
    g_wo = _dense_dw("mix_out_dw", mixin, d_mix)

    def gate_mix_bwd(r0, xs, ps):
        _, vjp = jax.vjp(_gate_mix, xs[0], xs[1], xs[2], xs[3], ps[0])
        dpgc, dpga, dyc_, dya_, dgb = vjp(xs[4])
        return [dpgc, dpga, dyc_, dya_], [dgb]

    dp_gc, dp_ga, d_yconv, d_yattn, g_gate_b = _rows(
        "gate_mix_bwd", gate_mix_bwd, [p, p, y_conv, y_attn, d_mixin], [gate_b], [BF16] * 4, [2 * d],
        row_in_cols=gate_cols + [None], row_out_widths=[d] * 4)
    d_o = _dense_dx("attn_out_dx", d_yattn, wao, BF16)
    g_wao = _dense_dw("attn_out_dw", o, d_yattn)
    d_yc = _dense_dx("conv_out_dx", d_yconv, wco, F32)
    g_wco = _dense_dw("conv_out_dw", yc, d_yconv)
    big = {"w_ffn_out": [g.reshape(N_DEV, fr, d) for g in g_wfo], "w_ffn_in": g_wfi,
           "w_o": [g.reshape(N_DEV, dc, d) for g in g_wo], "w_attn_out": [g.reshape(N_DEV, dc, d) for g in g_wao],
           "w_conv_out": [g.reshape(N_DEV, dc, d) for g in g_wco]}
    early = ("w_ffn_out", "w_ffn_in", "w_o", "w_attn_out", "w_conv_out")
    dq, dk, dv, received_early = _attn_bwd(p16, d_o, tot, d, [big[k][1] for k in early])

    def conv_norm_bwd(r0, xs, ps):
        _, vjp = jax.vjp(_ln_silu, xs[0], ps[0], ps[1])
        dy_, dg, db = vjp(xs[1])
        return [dy_], [dg, db]

    d_y, g_ln_g, g_ln_b = _rows("conv_norm_bwd", conv_norm_bwd, [y, d_yc], [conv_ln_g, conv_ln_b], [F32], [d, d])
    dp_a, dp_g, g_dww, g_dwb = _conv_bwd(p, d_y, dww_full, d)
    dp = jnp.concatenate([dp_a, dp_g, dq, dk, dv, dp_gc, dp_ga], axis=1)
    g_wi = _matmul("in_proj_dw", TN, u, dp, pl.BlockSpec((m_rows, d), lambda i: (0, 0)), pl.BlockSpec((m_rows, ns), lambda i: (0, i)),
                   pl.BlockSpec((None, d, ns), lambda i: (i, 0, 0)), jax.ShapeDtypeStruct((N_DEV, d, ns), F32), (N_DEV,), twin_bf16=True)
    x_i, y_i, c_i = _mesh_pos()
    slab_idx = jnp.stack([_dev(*chip, c_i) for chip in _other_chips(x_i, y_i)]).astype(I32)
    (wi_sibling,) = _sibling_exchange([g_wi[1]])
    wi_pairs = _pair_sum(slab_idx, g_wi[0], wi_sibling)
    du, wi_chips = _matmul(
        "in_proj_dx", NT, dp, wi, pl.BlockSpec((m_rows, ns), lambda i: (0, i)), pl.BlockSpec((None, d, ns), lambda i: (i, 0, 0)),
        pl.BlockSpec((m_rows, d), lambda i: (0, 0)), jax.ShapeDtypeStruct((m_rows, d), F32), (N_DEV,), acc_axis=0,
        carried=(_chip_copies, [wi_pairs], [jax.ShapeDtypeStruct(wi_pairs.shape, BF16)],
                 [pltpu.SemaphoreType.DMA((1, 3)), pltpu.SemaphoreType.DMA((1, 3))]))

    def pre_mix_bwd(r0, xs, ps):
        _, vjp = jax.vjp(_rms, xs[0], ps[0])
        dx, dg = vjp(xs[1])
        return [xs[2] + dx], [dg]

    dh0, g_pre_mix = _rows("pre_mix_bwd", pre_mix_bwd, [h0, du, dh1], [pre_mix_g], [F32], [d])
    grad_x = dh0[n_meta:length][None]

    me = _dev(x_i, y_i, c_i)
    me_arr = jnp.reshape(me, (1,)).astype(I32)
    big["w_in"] = g_wi
    received = {k: [r] for k, r in zip(early, received_early)}
    received["w_in"] = [wi_sibling[3:4], wi_chips]
    results = {}
    for k in SHARDED:
        outs = _adamw_shard(me_arr, big[k][0], received[k], local[k], given["m_" + k][0], given["v_" + k][0])
        results[k] = tuple(a[None] for a in outs)

    rep_grads = {"pre_mix_g": g_pre_mix, "gate_b": g_gate_b, "dw_b": g_dwb, "conv_ln_g": g_ln_g, "conv_ln_b": g_ln_b,
                 "post_mix_g": g_post_mix, "pre_ffn_g": g_pre_ffn, "post_ffn_g": g_post_ffn}

    def pack_rep(get):
        return jnp.concatenate([get(k) for k in REPLICATED], axis=1).reshape(-1, LANE)

    rep_rows = pack_rep(lambda k: rep_grads[k])
    n_rep = rep_rows.shape[0]
    loss_rows = jnp.broadcast_to(loss_part, (8, LANE))
    g_meta = dh0[0:n_meta]
    slabs = jnp.stack([jnp.concatenate([rep_rows, loss_rows, g_dww[:, j * LANE:(j + 1) * LANE], g_meta[:, j * LANE:(j + 1) * LANE]], axis=0)
                       for j in range(N_DEV)])

    def pack_small(prefix):
        dww_own = jnp.pad(given[prefix + "dw_w"][0], ((0, CONV_PAD - CONV_WIDTH), (0, 0)))
        return jnp.concatenate([pack_rep(lambda k: given[prefix + k]), jnp.zeros((8, LANE), F32), dww_own,
                                given[prefix + "meta_tokens"]], axis=0)

    small = _small_reduce_adamw(slabs, pack_small(""), pack_small("m_"), pack_small("v_"))
    loss = small[0][n_rep, 0]

    def unpack(arr):
        out = {}
        flat = arr[:n_rep].reshape(1, -1)
        off = 0
        for k in REPLICATED:
            w = given[k].shape[1]
            out[k] = flat[:, off:off + w]
            off += w
        out["dw_w"] = arr[n_rep + 8:n_rep + 8 + CONV_WIDTH][None]
        out["meta_tokens"] = arr[n_rep + 8 + CONV_PAD:n_rep + 8 + CONV_PAD + n_meta]
        return out

    small_out = [unpack(a) for a in small]
    for k in WEIGHTS:
        if k not in results:
            results[k] = tuple(s[k] for s in small_out)
    return (loss, grad_x, *[results[k][0] for k in WEIGHTS], *[results[k][1] for k in WEIGHTS],
            *[results[k][2] for k in WEIGHTS], *[results[k][3] for k in WEIGHTS])
```

```python
import jax
import jax.numpy as jnp
from jax import lax
from jax.experimental import pallas as pl
from jax.experimental.pallas import tpu as pltpu

F32 = jnp.float32
BF16 = jnp.bfloat16
I32 = jnp.int32

N_DEV = 8
LANE = 128
HEAD_DIM = 64
QB = 128
KEY_SHIFT = 2
KEY_TILES = 1 << KEY_SHIFT
KEY_CHUNK = KEY_TILES * QB
LANE_BLOCKS = 4
BWD_LANE_BLOCKS = 4
CONV_WIDTH = 31
CONV_PAD = 32
ROW_CHUNK = 128
RMS_EPS = 1e-6
LN_EPS = 1e-5
ADAM_LR = 0.001
ADAM_B1 = 0.9
ADAM_B2 = 0.999
ADAM_EPS = 1e-08
ADAM_WD = 0.01
ADAM_STEP = 10
VMEM_LIMIT = 56 * 1024 * 1024

NN = (((1,), (0,)), ((), ()))
NT = (((1,), (1,)), ((), ()))
TN = (((0,), (0,)), ((), ()))
MESH = pl.DeviceIdType.MESH
ANY = pl.BlockSpec(memory_space=pl.ANY)
VMEM_WHOLE = pl.BlockSpec(memory_space=pltpu.VMEM)


def _params(n_axes):
    return pltpu.CompilerParams(dimension_semantics=("arbitrary",) * n_axes, vmem_limit_bytes=VMEM_LIMIT)


def _row_tile(m):
    assert m % QB == 0
    return m // 4 if m % 64 == 0 else QB


def _matmul(name, dims, a, b, a_spec, b_spec, o_spec, out_shape, grid, acc_axis=None, twin_bf16=False, carried=None):
    n_car = 0 if carried is None else len(carried[1])
    n_twin = 1 if twin_bf16 else 0

    def body(a_ref, b_ref, *rest):
        car_ins, o_ref, twin, rest = rest[:n_car], rest[n_car], rest[n_car + 1:n_car + 1 + n_twin], rest[n_car + 1 + n_twin:]
        if carried is not None:
            starts, relays, waits = carried[0](car_ins, rest[:n_car], *rest[n_car:])

            @pl.when(pl.program_id(0) == 0)
            def _():
                for f in starts:
                    f()

            if relays:
                @pl.when(pl.program_id(0) == (3 * grid[0]) // 4)
                def _():
                    for f in relays:
                        f()

        r = lax.dot_general(a_ref[...], b_ref[...], dims, preferred_element_type=F32)
        if acc_axis is None:
            o_ref[...] = r.astype(o_ref.dtype)
            for t_ref in twin:
                t_ref[...] = r.astype(BF16)
        else:
            k = pl.program_id(acc_axis)

            @pl.when(k == 0)
            def _():
                o_ref[...] = r

            @pl.when(k > 0)
            def _():
                o_ref[...] += r

        if carried is not None:
            @pl.when(pl.program_id(0) == grid[0] - 1)
            def _():
                for f in waits:
                    f()

    in_specs, out_specs, out_shapes, scratch = [a_spec, b_spec], [o_spec], [out_shape], []
    if twin_bf16:
        assert acc_axis is None
        out_specs.append(o_spec)
        out_shapes.append(jax.ShapeDtypeStruct(out_shape.shape, BF16))
    if carried is not None:
        assert len(grid) == 1
        in_specs += [ANY] * n_car
        out_specs += [ANY] * n_car
        out_shapes += list(carried[2])
        scratch = list(carried[3])
    outs = pl.pallas_call(body, name=name, grid=grid, in_specs=in_specs, out_specs=out_specs, out_shape=out_shapes,
                          scratch_shapes=scratch, compiler_params=_params(len(grid)))(a, b, *(carried[1] if carried else ()))
    return outs[0] if len(outs) == 1 else outs


def _ffn_rows(m):
    return m // 2 if m % 32 == 0 else m


def _ffn_in_swiglu(u2, wfi):
    m, d = u2.shape
    half, fs = N_DEV // 2, wfi.shape[2]
    tm = _ffn_rows(m)

    def body(u_ref, wa_ref, wb_ref, a_ref, b_ref, f_ref):
        u = u_ref[...]
        a = jnp.dot(u, wa_ref[...], preferred_element_type=F32)
        b = jnp.dot(u, wb_ref[...], preferred_element_type=F32)
        a_ref[...] = a.astype(BF16)
        b_ref[...] = b.astype(BF16)
        f_ref[...] = _swiglu(a, b).astype(BF16)

    out = jax.ShapeDtypeStruct((half, m, fs), BF16)
    act = pl.BlockSpec((None, tm, fs), lambda j, i: (j, i, 0))
    return pl.pallas_call(
        body, name="ffn_in_swiglu", grid=(half, m // tm),
        in_specs=[pl.BlockSpec((tm, d), lambda j, i: (i, 0)), pl.BlockSpec((None, d, fs), lambda j, i: (j, 0, 0)),
                  pl.BlockSpec((None, d, fs), lambda j, i: (j + half, 0, 0))],
        out_specs=[act, act, act], out_shape=[out, out, out], compiler_params=_params(2))(u2, wfi, wfi)


def _ffn_out_dx_swiglu(d_f, wfo, a_act, b_act):
    m, d = d_f.shape
    half, fs = N_DEV // 2, wfo.shape[1]
    tm = _ffn_rows(m)

    def body(df_ref, w_ref, a_ref, b_ref, o_ref):
        d_fin = lax.dot_general(df_ref[...], w_ref[...], NT, preferred_element_type=F32)
        _, vjp = jax.vjp(_swiglu, a_ref[...].astype(F32), b_ref[...].astype(F32))
        d_a, d_b = vjp(d_fin)
        o_ref[0] = d_a.astype(BF16)
        o_ref[1] = d_b.astype(BF16)

    act = pl.BlockSpec((None, tm, fs), lambda j, i: (j, i, 0))
    return pl.pallas_call(
        body, name="ffn_out_dx_swiglu", grid=(half, m // tm),
        in_specs=[pl.BlockSpec((tm, d), lambda j, i: (i, 0)), pl.BlockSpec((None, fs, d), lambda j, i: (j, 0, 0)), act, act],
        out_specs=pl.BlockSpec((2, None, tm, fs), lambda j, i: (0, j, i, 0)),
        out_shape=jax.ShapeDtypeStruct((2, half, m, fs), BF16), compiler_params=_params(2))(d_f, wfo, a_act, b_act)


DENSE_TILE = 256


def _dense_fwd(name, a, w, out_dtype=F32):
    m, k = a.shape
    n = w.shape[1]
    tn = DENSE_TILE
    return _matmul(name, NN, a, w, pl.BlockSpec((m, k), lambda j: (0, 0)), pl.BlockSpec((k, tn), lambda j: (0, j)),
                   pl.BlockSpec((m, tn), lambda j: (0, j)), jax.ShapeDtypeStruct((m, n), out_dtype), (n // tn,))


def _dense_dx(name, dy, w, out_dtype):
    m, n = dy.shape
    k = w.shape[0]
    tk = DENSE_TILE
    return _matmul(name, NT, dy, w, pl.BlockSpec((m, n), lambda j: (0, 0)), pl.BlockSpec((tk, n), lambda j: (j, 0)),
                   pl.BlockSpec((m, tk), lambda j: (0, j)), jax.ShapeDtypeStruct((m, k), out_dtype), (k // tk,))


def _dense_dw(name, a, dy):
    m, k = a.shape
    n = dy.shape[1]
    tn = DENSE_TILE
    return _matmul(name, TN, a, dy, pl.BlockSpec((m, k), lambda j: (0, 0)), pl.BlockSpec((m, tn), lambda j: (0, j)),
                   pl.BlockSpec((k, tn), lambda j: (0, j)), jax.ShapeDtypeStruct((k, n), F32), (n // tn,), twin_bf16=True)


def _rowwise(name, fn, row_ins, par_ins, row_outs, par_outs, *, grid, in_specs, out_specs, tm, row_axis):
    n_ri, n_pi, n_ro, n_po = len(row_ins), len(par_ins), len(row_outs), len(par_outs)
    n_steps, tail = divmod(tm, ROW_CHUNK)
    assert tail % 16 == 0

    def body(*refs):
        ri = refs[:n_ri]
        pi = refs[n_ri:n_ri + n_pi]
        ro = refs[n_ri + n_pi:n_ri + n_pi + n_ro]
        po = refs[n_ri + n_pi + n_ro:]
        ps = [r[...] for r in pi]
        base = pl.program_id(row_axis) * tm

        def chunk(r0, rows, carry):
            xs = [r[pl.ds(r0, rows), :] for r in ri]
            outs, pouts = fn(base + r0, xs, ps)
            for r, o in zip(ro, outs):
                if isinstance(o, (list, tuple)):
                    for j, part in enumerate(o):
                        r[j, pl.ds(r0, rows), :] = part.astype(r.dtype)
                else:
                    r[pl.ds(r0, rows), :] = o.astype(r.dtype)
            return tuple(c + q for c, q in zip(carry, pouts))

        def step(i, carry):
            return chunk(pl.multiple_of(i * ROW_CHUNK, ROW_CHUNK), ROW_CHUNK, carry)

        acc = lax.fori_loop(0, n_steps, step, tuple(jnp.zeros(s.shape, F32) for s in par_outs))
        if tail:
            acc = chunk(n_steps * ROW_CHUNK, tail, acc)
        if n_po:
            first = pl.program_id(0) == 0
            for ax in range(1, len(grid)):
                first = first & (pl.program_id(ax) == 0)

            @pl.when(first)
            def _():
                for r in po:
                    r[...] = jnp.zeros_like(r)

            for r, a in zip(po, acc):
                r[...] += a

    return pl.pallas_call(body, name=name, grid=grid, in_specs=in_specs, out_specs=out_specs,
                          out_shape=tuple(row_outs) + tuple(par_outs),
                          compiler_params=_params(len(grid)))(*row_ins, *par_ins)


def _rows(name, fn, row_ins, par_ins, row_out_dtypes, par_out_widths, row_in_cols=None, row_out_widths=None):
    m = row_ins[0].shape[0]
    tm = _row_tile(m)
    in_specs = []
    for k, a in enumerate(row_ins):
        if row_in_cols is not None and row_in_cols[k] is not None:
            width, cb = row_in_cols[k]
            in_specs.append(pl.BlockSpec((tm, width), lambda i, cb=cb: (i, cb)))
        else:
            in_specs.append(pl.BlockSpec((tm, a.shape[1]), lambda i: (i, 0)))
    for a in par_ins:
        in_specs.append(pl.BlockSpec(a.shape, lambda i: (0, 0)))
    if row_out_widths is None:
        row_out_widths = [row_ins[0].shape[1]] * len(row_out_dtypes)
    row_outs = [jax.ShapeDtypeStruct((m, w), dt) for w, dt in zip(row_out_widths, row_out_dtypes)]
    par_outs = [jax.ShapeDtypeStruct((1, w), F32) for w in par_out_widths]
    out_specs = [pl.BlockSpec((tm, s.shape[1]), lambda i: (i, 0)) for s in row_outs]
    out_specs += [pl.BlockSpec(s.shape, lambda i: (0, 0)) for s in par_outs]
    return _rowwise(name, fn, row_ins, par_ins, row_outs, par_outs, grid=(m // tm,), in_specs=in_specs,
                    out_specs=out_specs, tm=tm, row_axis=0)


def _rms(x, g):
    return x * lax.rsqrt(jnp.mean(x * x, axis=-1, keepdims=True) + RMS_EPS) * g


def _ln_silu(y, g, b):
    mu = jnp.mean(y, axis=-1, keepdims=True)
    yc = y - mu
    var = jnp.mean(yc * yc, axis=-1, keepdims=True)
    return jax.nn.silu(yc * lax.rsqrt(var + LN_EPS) * g + b)


def _gate_mix(pgc, pga, yc, ya, gb):
    d = pgc.shape[1]
    return jax.nn.sigmoid(pgc + gb[:, :d]) * yc + jax.nn.sigmoid(pga + gb[:, d:]) * ya


def _post_mix(h0, mix, g_post, g_pre):
    h1 = h0 + _rms(mix, g_post)
    return h1, _rms(h1, g_pre)


def _swiglu(a, b):
    return jax.nn.silu(a) * b


def _conv_taps():
    taps = []
    for b in range(8):
        for a in range(CONV_PAD // 8):
            s = 8 * a + b
            if s < CONV_WIDTH:
                taps.append((b, a, CONV_WIDTH - 1 - s))
    return taps


def _conv_fwd(p, dww, dwb, d_model):
    m = p.shape[0]
    nch = d_model // LANE
    n_chunk = m // QB
    taps = _conv_taps()

    def body(a_ref, g_ref, w_ref, b_ref, y_ref, upad):
        upad[0:CONV_PAD, :] = jnp.zeros((CONV_PAD, LANE), F32)

        def fill(i, c):
            r0 = pl.multiple_of(i * QB, QB)
            u = a_ref[pl.ds(r0, QB), :] * jax.nn.sigmoid(g_ref[pl.ds(r0, QB), :])
            upad[pl.ds(pl.multiple_of(r0 + CONV_PAD, 8), QB), :] = u
            return c

        lax.fori_loop(0, n_chunk, fill, 0)

        def conv(i, c):
            r0 = pl.multiple_of(i * QB, QB)
            win = upad[pl.ds(r0, QB + CONV_PAD), :]
            acc = jnp.broadcast_to(b_ref[...], (QB, LANE))
            rolled = {}
            for b, a, j in taps:
                if b not in rolled:
                    rolled[b] = win if b == 0 else pltpu.roll(win, b, axis=0)
                lo = CONV_PAD - 8 * a
                acc = acc + w_ref[j:j + 1, :] * rolled[b][lo:lo + QB, :]
            y_ref[pl.ds(r0, QB), :] = acc
            return c

        lax.fori_loop(0, n_chunk, conv, 0)

    col = lambda off: pl.BlockSpec((m, LANE), lambda c: (0, off + c))
    return pl.pallas_call(
        body, name="conv_fwd", grid=(nch,),
        in_specs=[col(0), col(nch), pl.BlockSpec((CONV_PAD, LANE), lambda c: (0, c)), pl.BlockSpec((1, LANE), lambda c: (0, c))],
        out_specs=col(0), out_shape=jax.ShapeDtypeStruct((m, d_model), F32),
        scratch_shapes=[pltpu.VMEM((m + CONV_PAD, LANE), F32)], compiler_params=_params(1))(p, p, dww, dwb)


def _conv_bwd(p, dy, dww, d_model):
    m = p.shape[0]
    nch = d_model // LANE
    n_chunk = m // QB
    taps = _conv_taps()
    win_rows = QB + CONV_PAD

    def body(a_ref, g_ref, dy_ref, w_ref, da_ref, dg_ref, dw_ref, db_ref, upad, dypad, wacc, bacc):
        upad[0:CONV_PAD, :] = jnp.zeros((CONV_PAD, LANE), F32)
        dypad[m:m + CONV_PAD, :] = jnp.zeros((CONV_PAD, LANE), F32)
        wacc[...] = jnp.zeros_like(wacc)
        bacc[...] = jnp.zeros_like(bacc)

        def fill(i, c):
            r0 = pl.multiple_of(i * QB, QB)
            u = a_ref[pl.ds(r0, QB), :] * jax.nn.sigmoid(g_ref[pl.ds(r0, QB), :])
            upad[pl.ds(pl.multiple_of(r0 + CONV_PAD, 8), QB), :] = u
            dypad[pl.ds(r0, QB), :] = dy_ref[pl.ds(r0, QB), :]
            return c

        lax.fori_loop(0, n_chunk, fill, 0)

        def chunk(i, c):
            r0 = pl.multiple_of(i * QB, QB)
            dwin = dypad[pl.ds(r0, win_rows), :]
            du = jnp.zeros((QB, LANE), F32)
            rolled = {}
            for b, a, j in taps:
                if b not in rolled:
                    rolled[b] = dwin if b == 0 else pltpu.roll(dwin, win_rows - b, axis=0)
                du = du + w_ref[j:j + 1, :] * rolled[b][8 * a:8 * a + QB, :]
            av = a_ref[pl.ds(r0, QB), :]
            sg = jax.nn.sigmoid(g_ref[pl.ds(r0, QB), :])
            da_ref[pl.ds(r0, QB), :] = (du * sg).astype(da_ref.dtype)
            dg_ref[pl.ds(r0, QB), :] = (du * av * sg * (1.0 - sg)).astype(dg_ref.dtype)
            dyc = dy_ref[pl.ds(r0, QB), :]
            uwin = upad[pl.ds(r0, win_rows), :]
            rolled = {}
            for b, a, j in taps:
                if b not in rolled:
                    rolled[b] = uwin if b == 0 else pltpu.roll(uwin, b, axis=0)
                lo = CONV_PAD - 8 * a
                prod = dyc * rolled[b][lo:lo + QB, :]
                wacc[j] += prod.reshape(QB // 8, 8, LANE).sum(axis=0)
            bacc[...] += dyc.reshape(QB // 8, 8, LANE).sum(axis=0)
            return c

        lax.fori_loop(0, n_chunk, chunk, 0)
        for j in range(CONV_WIDTH):
            dw_ref[j:j + 1, :] = jnp.sum(wacc[j], axis=0, keepdims=True)
        dw_ref[CONV_WIDTH:CONV_PAD, :] = jnp.zeros((CONV_PAD - CONV_WIDTH, LANE), F32)
        db_ref[...] = jnp.sum(bacc[...], axis=0, keepdims=True)

    col = lambda off: pl.BlockSpec((m, LANE), lambda c: (0, off + c))
    return pl.pallas_call(
        body, name="conv_bwd", grid=(nch,),
        in_specs=[col(0), col(nch), col(0), pl.BlockSpec((CONV_PAD, LANE), lambda c: (0, c))],
        out_specs=[col(0), col(0), pl.BlockSpec((CONV_PAD, LANE), lambda c: (0, c)), pl.BlockSpec((1, LANE), lambda c: (0, c))],
        out_shape=(jax.ShapeDtypeStruct((m, d_model), BF16), jax.ShapeDtypeStruct((m, d_model), BF16),
                   jax.ShapeDtypeStruct((CONV_PAD, d_model), F32), jax.ShapeDtypeStruct((1, d_model), F32)),
        scratch_shapes=[pltpu.VMEM((m + CONV_PAD, LANE), F32), pltpu.VMEM((m + CONV_PAD, LANE), F32),
                        pltpu.VMEM((CONV_PAD, 8, LANE), F32), pltpu.VMEM((8, LANE), F32)],
        compiler_params=_params(1))(p, p, dy, dww)


EXP_CLAMP = 80.0


def _one_plus_exp(z):
    return 1.0 + jnp.exp(jnp.minimum(z, EXP_CLAMP))


def _softplus(z):
    return jnp.maximum(jnp.log(_one_plus_exp(z)), z)


def _softplus_sigmoid(z):
    s = _one_plus_exp(z)
    return jnp.maximum(jnp.log(s), z), 1.0 - 1.0 / s


def _tile_cumsums(x, tri2):
    xb = x.astype(BF16)
    out = []
    for i in range(0, x.shape[1] // QB, 2):
        both = jnp.dot(xb[:, i * QB:(i + 2) * QB], tri2, preferred_element_type=F32)
        out += [both[:, :QB], both[:, QB:]]
    return out


def _tri2(kind):
    jj = lax.broadcasted_iota(I32, (2 * QB, 2 * QB), 0)
    ss = lax.broadcasted_iota(I32, (2 * QB, 2 * QB), 1)
    same = (jj >= QB) == (ss >= QB)
    keep = {"ge": jj >= ss, "le": jj <= ss}[kind]
    return jnp.where(same & keep, 1.0, 0.0).astype(BF16)


def _attn_fwd(p, d_model, shards):
    m = p.shape[0]
    nqb = m // QB
    ngrp = d_model // (LANE_BLOCKS * LANE)
    qo, ko, vo = 2 * ngrp, 3 * ngrp, 4 * ngrp
    scale = HEAD_DIM ** -0.5
    n_sh = len(shards)

    assert nqb >= KEY_TILES

    def body(q_ref, k_ref, v_ref, *rest):
        shard_refs, (o_ref, t_ref), rest = rest[:n_sh], rest[n_sh:n_sh + 2], rest[n_sh + 2:]
        gathered_refs, (acc_ref, car_ref), sems = rest[:n_sh], rest[n_sh:n_sh + 2], rest[n_sh + 2:]
        starts, relays, waits = _gather_copies(shard_refs, gathered_refs, *sems)

        @pl.when((pl.program_id(0) == 0) & (pl.program_id(1) == 0))
        def _():
            for f in starts:
                f()

        @pl.when((pl.program_id(0) == ngrp - 1) & (pl.program_id(1) == (5 * nqb) // 8))
        def _():
            for f in relays:
                f()

        qb = pl.program_id(1)
        lane = lax.broadcasted_iota(I32, (QB, LANE), 1)
        head0 = lane < HEAD_DIM
        row_g = qb * QB + lax.broadcasted_iota(I32, (QB, KEY_CHUNK), 0)
        col_l = lax.broadcasted_iota(I32, (QB, KEY_CHUNK), 1)
        tri = _tri2("ge")
        heads = range(2 * LANE_BLOCKS)
        qh = []
        for lb in range(LANE_BLOCKS):
            q2 = (q_ref[:, lb * LANE:(lb + 1) * LANE] * scale).astype(BF16)
            zero = jnp.zeros_like(q2)
            qh += [jnp.where(head0, q2, zero), jnp.where(head0, zero, q2)]
        acc_ref[...] = jnp.zeros_like(acc_ref)
        car_ref[...] = jnp.zeros_like(car_ref)

        def chunk(first_tile, bound, n_tiles=KEY_TILES):
            r0 = pl.multiple_of(first_tile * QB, QB)
            keys = n_tiles * QB
            kcs = [k_ref[pl.ds(r0, keys), lb * LANE:(lb + 1) * LANE].astype(BF16) for lb in range(LANE_BLOCKS)]
            vcs = [v_ref[pl.ds(r0, keys), lb * LANE:(lb + 1) * LANE].astype(BF16) for lb in range(LANE_BLOCKS)]
            valid = None if bound is None else (col_l[:, :keys] + r0) < bound
            zs = [lax.dot_general(qh[h], kcs[h // 2], NT, preferred_element_type=F32) for h in heads]
            sps = [_softplus(z) for z in zs]
            if valid is not None:
                sps = [jnp.where(valid, sp, 0.0) for sp in sps]
            cums = [_tile_cumsums(sp, tri) for sp in sps]
            cars = [car_ref[h] for h in heads]
            a_tiles = [[None] * n_tiles for h in heads]
            for i in reversed(range(n_tiles)):
                for h in heads:
                    cum = cums[h][i]
                    a_tiles[h][i] = jnp.exp(zs[h][:, i * QB:(i + 1) * QB] - (cum + cars[h]))
                    cars[h] = cars[h] + jnp.broadcast_to(cum[:, 0:1], cum.shape)
            for h in heads:
                a = jnp.concatenate(a_tiles[h], axis=1)
                if valid is not None:
                    a = jnp.where(valid, a, 0.0)
                acc_ref[h] += jnp.dot(a.astype(BF16), vcs[h // 2], preferred_element_type=F32)
                car_ref[h] = cars[h]

        near = jnp.maximum(qb - (KEY_TILES - 1), 0)
        chunk(near, row_g)
        n_full = lax.shift_right_logical(near, KEY_SHIFT)

        def step(i, c):
            chunk(near - KEY_TILES * (i + 1), None)
            return c

        lax.fori_loop(0, n_full, step, 0)
        left = near - KEY_TILES * n_full

        @pl.when((left > 0) & (left <= 2))
        def _():
            chunk(0, left * QB, n_tiles=2)

        @pl.when(left > 2)
        def _():
            chunk(0, left * QB)

        for lb in range(LANE_BLOCKS):
            o_ref[:, lb * LANE:(lb + 1) * LANE] = jnp.where(head0, acc_ref[2 * lb], acc_ref[2 * lb + 1]).astype(o_ref.dtype)
        for h in heads:
            t_ref[:, h * QB:(h + 1) * QB] = car_ref[h]

        @pl.when((pl.program_id(0) == ngrp - 1) & (pl.program_id(1) == nqb - 1))
        def _():
            for f in waits:
                f()

    wide = LANE_BLOCKS * LANE
    outs = pl.pallas_call(
        body, name="attn_fwd", grid=(ngrp, nqb),
        in_specs=[pl.BlockSpec((QB, wide), lambda g, qb: (qb, qo + g)),
                  pl.BlockSpec((m, wide), lambda g, qb: (0, ko + g)),
                  pl.BlockSpec((m, wide), lambda g, qb: (0, vo + g))] + [ANY] * n_sh,
        out_specs=[pl.BlockSpec((QB, wide), lambda g, qb: (qb, g)),
                   pl.BlockSpec((QB, 2 * wide), lambda g, qb: (qb, g))] + [ANY] * n_sh,
        out_shape=[jax.ShapeDtypeStruct((m, d_model), BF16), jax.ShapeDtypeStruct((m, 2 * d_model), F32)]
        + [jax.ShapeDtypeStruct((N_DEV,) + s.shape, s.dtype) for s in shards],
        scratch_shapes=[pltpu.VMEM((2 * LANE_BLOCKS, QB, LANE), F32), pltpu.VMEM((2 * LANE_BLOCKS, QB, LANE), F32)]
        + _exchange_sems(n_sh) + [pltpu.SemaphoreType.DMA((n_sh,))],
        compiler_params=_params(2))(p, p, p, *shards)
    return outs[0], outs[1], outs[2:]


def _attn_bwd(p, d_o, tot, d_model, slabs):
    m = p.shape[0]
    nqb = m // QB
    blocks = BWD_LANE_BLOCKS
    ngrp = d_model // (blocks * LANE)
    qo, ko, vo = 2 * ngrp, 3 * ngrp, 4 * ngrp
    scale = HEAD_DIM ** -0.5
    n_sl = len(slabs)

    assert nqb >= KEY_TILES

    def body(q_ref, k_ref, v_ref, do_ref, t_ref, *rest):
        slab_refs, (dq_ref, dk_ref, dv_ref), rest = rest[:n_sl], rest[n_sl:n_sl + 3], rest[n_sl + 3:]
        recv_refs, (dkacc, dvacc, dqacc, csp, cg), sems = rest[:n_sl], rest[n_sl:n_sl + 5], rest[n_sl + 5:]
        starts, waits = _scatter_copies(slab_refs, recv_refs, *sems)

        @pl.when((pl.program_id(0) == 0) & (pl.program_id(1) == 0))
        def _():
            for f in starts:
                f()

        qb = pl.program_id(1)
        lane = lax.broadcasted_iota(I32, (QB, LANE), 1)
        head0 = lane < HEAD_DIM
        row_g = qb * QB + lax.broadcasted_iota(I32, (QB, KEY_CHUNK), 0)
        col_l = lax.broadcasted_iota(I32, (QB, KEY_CHUNK), 1)
        tri_ge = _tri2("ge")
        tri_le = _tri2("le")
        heads = range(2 * blocks)
        qh, doh = [], []
        for lb in range(blocks):
            q2 = (q_ref[:, lb * LANE:(lb + 1) * LANE] * scale).astype(BF16)
            do2 = do_ref[:, lb * LANE:(lb + 1) * LANE]
            zero = jnp.zeros_like(q2)
            qh += [jnp.where(head0, q2, zero), jnp.where(head0, zero, q2)]
            doh += [jnp.where(head0, do2, zero), jnp.where(head0, zero, do2)]
        q_pairs = [jnp.concatenate(qh[2 * lb:2 * lb + 2], axis=0) for lb in range(blocks)]
        do_pairs = [jnp.concatenate(doh[2 * lb:2 * lb + 2], axis=0) for lb in range(blocks)]

        @pl.when(qb == 0)
        def _():
            dkacc[...] = jnp.zeros_like(dkacc)
            dvacc[...] = jnp.zeros_like(dvacc)

        dqacc[...] = jnp.zeros_like(dqacc)
        for h in heads:
            csp[h] = t_ref[:, h * QB:(h + 1) * QB]
        cg[...] = jnp.zeros_like(cg)

        def chunk(first_tile, bound, n_tiles=KEY_TILES):
            r0 = pl.multiple_of(first_tile * QB, QB)
            keys = n_tiles * QB
            kcs = [k_ref[pl.ds(r0, keys), lb * LANE:(lb + 1) * LANE].astype(BF16) for lb in range(blocks)]
            vcs = [v_ref[pl.ds(r0, keys), lb * LANE:(lb + 1) * LANE].astype(BF16) for lb in range(blocks)]
            valid = None if bound is None else (col_l[:, :keys] + r0) < bound
            tiles = [slice(i * QB, (i + 1) * QB) for i in range(n_tiles)]
            zs = [lax.dot_general(qh[h], kcs[h // 2], NT, preferred_element_type=F32) for h in heads]
            das = [lax.dot_general(doh[h], vcs[h // 2], NT, preferred_element_type=F32) for h in heads]
            sps, sgs = zip(*[_softplus_sigmoid(z) for z in zs])
            if valid is not None:
                sps = [jnp.where(valid, sp, 0.0) for sp in sps]
            cums = [_tile_cumsums(sp, tri_ge) for sp in sps]
            a_tiles, g_tiles = [[] for h in heads], [[] for h in heads]
            for h in heads:
                rest = csp[h]
                for i, c in enumerate(tiles):
                    cum = cums[h][i]
                    rest = rest - jnp.broadcast_to(cum[:, 0:1], cum.shape)
                    a = jnp.exp(zs[h][:, c] - (cum + rest))
                    if valid is not None:
                        a = jnp.where(valid[:, c], a, 0.0)
                    a_tiles[h].append(a)
                    g_tiles[h].append(a * das[h][:, c])
                csp[h] = rest
            gcums = [_tile_cumsums(jnp.concatenate(g_tiles[h], axis=1), tri_le) for h in heads]
            dzbs, abs_ = [], []
            for h in heads:
                g_before = cg[h]
                dz_tiles = []
                for i, c in enumerate(tiles):
                    gcum = gcums[h][i]
                    dz = g_tiles[h][i] - sgs[h][:, c] * (g_before + gcum)
                    if valid is not None:
                        dz = jnp.where(valid[:, c], dz, 0.0)
                    dz_tiles.append(dz)
                    g_before = g_before + jnp.broadcast_to(gcum[:, QB - 1:QB], gcum.shape)
                cg[h] = g_before
                dzbs.append(jnp.concatenate(dz_tiles, axis=1).astype(BF16))
                abs_.append(jnp.concatenate(a_tiles[h], axis=1).astype(BF16))
            for h in heads:
                dqacc[h] += jnp.dot(dzbs[h], kcs[h // 2], preferred_element_type=F32)
            for lb in range(blocks):
                dz_pair = jnp.concatenate(dzbs[2 * lb:2 * lb + 2], axis=0)
                a_pair = jnp.concatenate(abs_[2 * lb:2 * lb + 2], axis=0)
                dkacc[pl.ds(r0, keys), lb * LANE:(lb + 1) * LANE] += lax.dot_general(
                    dz_pair, q_pairs[lb], TN, preferred_element_type=F32)
                dvacc[pl.ds(r0, keys), lb * LANE:(lb + 1) * LANE] += lax.dot_general(
                    a_pair, do_pairs[lb], TN, preferred_element_type=F32)

        near = jnp.maximum(qb - (KEY_TILES - 1), 0)
        n_full = lax.shift_right_logical(near, KEY_SHIFT)

        def step(i, c):
            chunk(KEY_TILES * i, None)
            return c

        lax.fori_loop(0, n_full, step, 0)
        left = near - KEY_TILES * n_full

        @pl.when((left > 0) & (left <= 2))
        def _():
            chunk(KEY_TILES * n_full, near * QB, n_tiles=2)

        @pl.when(left > 2)
        def _():
            chunk(KEY_TILES * n_full, near * QB)

        chunk(near, row_g)
        for lb in range(blocks):
            dq2 = jnp.where(head0, dqacc[2 * lb], dqacc[2 * lb + 1]) * scale
            dq_ref[:, lb * LANE:(lb + 1) * LANE] = dq2.astype(dq_ref.dtype)

        @pl.when(qb == nqb - 1)
        def _():
            dk_ref[...] = dkacc[...].astype(dk_ref.dtype)
            dv_ref[...] = dvacc[...].astype(dv_ref.dtype)

        @pl.when((pl.program_id(0) == ngrp - 1) & (qb == nqb - 1))
        def _():
            for f in waits:
                f()

    out = jax.ShapeDtypeStruct((m, d_model), BF16)
    wide = blocks * LANE
    carry = pltpu.VMEM((2 * blocks, QB, LANE), F32)
    outs = pl.pallas_call(
        body, name="attn_bwd", grid=(ngrp, nqb),
        in_specs=[pl.BlockSpec((QB, wide), lambda g, qb: (qb, qo + g)),
                  pl.BlockSpec((m, wide), lambda g, qb: (0, ko + g)),
                  pl.BlockSpec((m, wide), lambda g, qb: (0, vo + g)),
                  pl.BlockSpec((QB, wide), lambda g, qb: (qb, g)),
                  pl.BlockSpec((QB, 2 * wide), lambda g, qb: (qb, g))] + [ANY] * n_sl,
        out_specs=[pl.BlockSpec((QB, wide), lambda g, qb: (qb, g)),
                   pl.BlockSpec((m, wide), lambda g, qb: (0, g)),
                   pl.BlockSpec((m, wide), lambda g, qb: (0, g))] + [ANY] * n_sl,
        out_shape=[out, out, out] + _received_shapes(slabs),
        scratch_shapes=[pltpu.VMEM((m, wide), F32), pltpu.VMEM((m, wide), F32), carry, carry, carry] + _exchange_sems(n_sl),
        compiler_params=_params(2))(p, p, p, d_o, tot, *slabs)
    return outs[0], outs[1], outs[2], outs[3:]


def _mesh_pos():
    return lax.axis_index("x"), lax.axis_index("y"), lax.axis_index("c")


def _other_chips(x, y):
    return [(1 - x, y), (x, 1 - y), (1 - x, 1 - y)]


def _dev(x, y, c):
    return 4 * x + 2 * y + c


def _all_gather(shards):
    n = len(shards)

    def body(*refs):
        starts, relays, waits = _gather_copies(refs[:n], refs[n:2 * n], *refs[2 * n:])
        for f in starts + relays + waits:
            f()

    return pl.pallas_call(
        body, name="comm_all_gather", in_specs=[ANY] * n, out_specs=[ANY] * n,
        out_shape=[jax.ShapeDtypeStruct((N_DEV,) + s.shape, s.dtype) for s in shards],
        scratch_shapes=[pltpu.SemaphoreType.DMA((n, 7)), pltpu.SemaphoreType.DMA((n, 7)), pltpu.SemaphoreType.DMA((n,))],
    )(*shards)


def _peers(x, y, c):
    out = []
    for mask in range(1, N_DEV):
        px, py, pc = x ^ (mask >> 2), y ^ ((mask >> 1) & 1), c ^ (mask & 1)
        out.append((mask - 1, (px, py, pc), _dev(px, py, pc)))
    return out


def _remote(src, dst, send_sems, recv_sems, k, s, peer):
    return pltpu.make_async_remote_copy(src_ref=src, dst_ref=dst, send_sem=send_sems.at[k, s], recv_sem=recv_sems.at[k, s],
                                        device_id=peer, device_id_type=MESH)


def _gather_copies(ins, outs, send_sems, recv_sems, local_sems):
    x, y, c = _mesh_pos()
    sibling = (x, y, 1 - c)
    chips = _other_chips(x, y)
    starts, relays, waits = [], [], []
    for k in range(len(ins)):
        def slot(block, k=k):
            return outs[k].at[_dev(*block)]

        def copy(s, src, block, to, k=k):
            return _remote(src, slot(block), send_sems, recv_sems, k, s, to)

        own = pltpu.make_async_copy(ins[k], slot((x, y, c)), local_sems.at[k])
        to_sibling = copy(0, ins[k], (x, y, c), sibling)
        starts += [own.start, to_sibling.start]
        waits += [own.wait, to_sibling.wait_send, copy(0, ins[k], (x, y, 1 - c), sibling).wait_recv]
        for j, chip in enumerate(chips):
            out = copy(1 + j, ins[k], (x, y, c), (*chip, c))
            relay = copy(4 + j, slot((*chip, c)), (*chip, c), sibling)
            starts.append(out.start)
            relays += [copy(1 + j, ins[k], (*chip, c), sibling).wait_recv, relay.start]
            waits += [out.wait_send, relay.wait_send, copy(4 + j, ins[k], (*chip, 1 - c), sibling).wait_recv]
    return starts, relays, waits


def _scatter_copies(ins, outs, send_sems, recv_sems):
    x, y, c = _mesh_pos()
    starts, waits = [], []
    for k in range(len(ins)):
        for s, peer, idx in _peers(x, y, c):
            send = _remote(ins[k].at[idx], outs[k].at[s], send_sems, recv_sems, k, s, peer)
            starts.append(send.start)
            waits += [send.wait_recv, send.wait_send]
    return starts, waits


def _exchange_sems(n):
    return [pltpu.SemaphoreType.DMA((n, N_DEV - 1)), pltpu.SemaphoreType.DMA((n, N_DEV - 1))]


def _received_shapes(slabs):
    return [jax.ShapeDtypeStruct((N_DEV - 1,) + a.shape[1:], a.dtype) for a in slabs]


def _chips_and_own(x, y):
    return _other_chips(x, y) + [(x, y)]


def _sibling_exchange(slabs):
    n = len(slabs)

    def body(*refs):
        ins, outs, (send_sems, recv_sems) = refs[:n], refs[n:2 * n], refs[2 * n:]
        x, y, c = _mesh_pos()
        copies = [_remote(ins[k].at[_dev(*chip, 1 - c)], outs[k].at[r], send_sems, recv_sems, k, r, (x, y, 1 - c))
                  for k in range(n) for r, chip in enumerate(_chips_and_own(x, y))]
        for cp in copies:
            cp.start()
        for cp in copies:
            cp.wait_recv()
        for cp in copies:
            cp.wait_send()

    return pl.pallas_call(body, name="comm_rs_sibling", in_specs=[ANY] * n, out_specs=[ANY] * n,
                          out_shape=[jax.ShapeDtypeStruct((4,) + a.shape[1:], a.dtype) for a in slabs],
                          scratch_shapes=[pltpu.SemaphoreType.DMA((n, 4)), pltpu.SemaphoreType.DMA((n, 4))])(*slabs)


def _chip_copies(ins, outs, send_sems, recv_sems):
    x, y, c = _mesh_pos()
    starts, waits = [], []
    for k in range(len(ins)):
        for r, chip in enumerate(_other_chips(x, y)):
            cp = _remote(ins[k].at[r], outs[k].at[r], send_sems, recv_sems, k, r, (*chip, c))
            starts.append(cp.start)
            waits += [cp.wait_recv, cp.wait_send]
    return starts, [], waits


def _pair_sum(slab_idx, grad, from_sibling):
    _, rows, cols = grad.shape
    tr = _shard_tile(rows)

    def body(idx_ref, g_ref, s_ref, o_ref):
        o_ref[...] = (g_ref[...] + s_ref[...].astype(F32)).astype(o_ref.dtype)

    gs = pltpu.PrefetchScalarGridSpec(
        num_scalar_prefetch=1, grid=(3, rows // tr),
        in_specs=[pl.BlockSpec((None, tr, cols), lambda r, i, idx: (idx[r], i, 0)),
                  pl.BlockSpec((None, tr, cols), lambda r, i, idx: (r, i, 0))],
        out_specs=pl.BlockSpec((None, tr, cols), lambda r, i, idx: (r, i, 0)))
    return pl.pallas_call(body, name="rs_pair_sum", grid_spec=gs, out_shape=jax.ShapeDtypeStruct((3, rows, cols), BF16),
                          compiler_params=_params(2))(slab_idx, grad, from_sibling)


def _shard_tile(rows):
    for tr in range(min(rows, 352), 0, -1):
        if rows % tr == 0 and (tr % 16 == 0 or tr == rows):
            return tr


def _adamw_math(w, g, m, v):
    m = ADAM_B1 * m + (1.0 - ADAM_B1) * g
    v = ADAM_B2 * v + (1.0 - ADAM_B2) * (g * g)
    m_hat = m / (1.0 - ADAM_B1 ** ADAM_STEP)
    v_hat = v / (1.0 - ADAM_B2 ** ADAM_STEP)
    delta = -ADAM_LR * (m_hat / (jnp.sqrt(v_hat) + ADAM_EPS) + ADAM_WD * w)
    return delta, m, v


def _adamw_shard(me, grad, received, w, m, v):
    rows, cols = w.shape
    tr = _shard_tile(rows)
    n_rec = len(received)

    def body(me_ref, g_ref, *rest):
        r_refs, (w_ref, m_ref, v_ref, go_ref, do_ref, mo_ref, vo_ref) = rest[:n_rec], rest[n_rec:]
        g = g_ref[...]
        for r_ref in r_refs:
            for s in range(r_ref.shape[0]):
                g = g + r_ref[s].astype(F32)
        delta, m_new, v_new = _adamw_math(w_ref[...], g, m_ref[...], v_ref[...])
        go_ref[...] = g
        do_ref[...] = delta
        mo_ref[...] = m_new
        vo_ref[...] = v_new

    flat = pl.BlockSpec((tr, cols), lambda i, me: (i, 0))
    gs = pltpu.PrefetchScalarGridSpec(
        num_scalar_prefetch=1, grid=(rows // tr,),
        in_specs=[pl.BlockSpec((None, tr, cols), lambda i, me: (me[0], i, 0))]
        + [pl.BlockSpec((r.shape[0], tr, cols), lambda i, me: (0, i, 0)) for r in received] + [flat, flat, flat],
        out_specs=[flat, flat, flat, flat])
    out = jax.ShapeDtypeStruct((rows, cols), F32)
    return pl.pallas_call(body, name="adamw_shard", grid_spec=gs, out_shape=(out, out, out, out),
                          compiler_params=_params(1))(me, grad, *received, w, m, v)


def _small_reduce_adamw(slabs, w, m, v):
    _, rows, _ = slabs.shape

    def body(s_ref, w_ref, m_ref, v_ref, g_ref, d_ref, mo_ref, vo_ref, land, send_sems, recv_sems):
        x, y, c = _mesh_pos()
        me = _dev(x, y, c)
        copies = []
        for mask in range(1, N_DEV):
            px, py, pc = x ^ (mask >> 2), y ^ ((mask >> 1) & 1), c ^ (mask & 1)
            copies.append(pltpu.make_async_remote_copy(
                src_ref=s_ref.at[_dev(px, py, pc)], dst_ref=land.at[me], send_sem=send_sems.at[mask - 1],
                recv_sem=recv_sems.at[mask - 1], device_id=(px, py, pc), device_id_type=MESH))
        for cp in copies:
            cp.start()
        land[me] = s_ref[me]
        for mask in range(1, N_DEV):
            px, py, pc = x ^ (mask >> 2), y ^ ((mask >> 1) & 1), c ^ (mask & 1)
            pltpu.make_async_remote_copy(
                src_ref=s_ref.at[me], dst_ref=land.at[_dev(px, py, pc)], send_sem=send_sems.at[mask - 1],
                recv_sem=recv_sems.at[mask - 1], device_id=(px, py, pc), device_id_type=MESH).wait_recv()
        for cp in copies:
            cp.wait_send()
        g = land[0]
        for d in range(1, N_DEV):
            g = g + land[d]
        delta, m_new, v_new = _adamw_math(w_ref[...], g, m_ref[...], v_ref[...])
        g_ref[...] = g
        d_ref[...] = delta
        mo_ref[...] = m_new
        vo_ref[...] = v_new

    out = jax.ShapeDtypeStruct((rows, LANE), F32)
    return pl.pallas_call(
        body, name="comm_small_reduce_adamw", in_specs=[VMEM_WHOLE] * 4, out_specs=[VMEM_WHOLE] * 4, out_shape=(out, out, out, out),
        scratch_shapes=[pltpu.VMEM((N_DEV, rows, LANE), F32), pltpu.SemaphoreType.DMA((N_DEV - 1,)),
                        pltpu.SemaphoreType.DMA((N_DEV - 1,))],
    )(slabs, w, m, v)


def _cast_bf16(arrs):
    n = len(arrs)

    def body(*refs):
        for i_ref, o_ref in zip(refs[:n], refs[n:]):
            o_ref[...] = i_ref[...].astype(BF16)

    return pl.pallas_call(body, name="cast_bf16", in_specs=[VMEM_WHOLE] * n, out_specs=[VMEM_WHOLE] * n,
                          out_shape=[jax.ShapeDtypeStruct(a.shape, BF16) for a in arrs],
                          compiler_params=pltpu.CompilerParams(vmem_limit_bytes=VMEM_LIMIT))(*arrs)


REPLICATED = ("pre_mix_g", "gate_b", "dw_b", "conv_ln_g", "conv_ln_b", "post_mix_g", "pre_ffn_g", "post_ffn_g")
SHARDED = ("w_in", "w_conv_out", "w_attn_out", "w_o", "w_ffn_in", "w_ffn_out")
WEIGHTS = ("meta_tokens", "pre_mix_g", "w_in", "gate_b", "dw_w", "dw_b", "conv_ln_g", "conv_ln_b", "w_conv_out",
           "w_attn_out", "w_o", "post_mix_g", "pre_ffn_g", "w_ffn_in", "w_ffn_out", "post_ffn_g")


def kernel(x, meta_tokens, pre_mix_g, w_in, gate_b, dw_w, dw_b, conv_ln_g, conv_ln_b, w_conv_out, w_attn_out, w_o, post_mix_g, pre_ffn_g, w_ffn_in, w_ffn_out, post_ffn_g, loss_target, m_meta_tokens, m_pre_mix_g, m_w_in, m_gate_b, m_dw_w, m_dw_b, m_conv_ln_g, m_conv_ln_b, m_w_conv_out, m_w_attn_out, m_w_o, m_post_mix_g, m_pre_ffn_g, m_w_ffn_in, m_w_ffn_out, m_post_ffn_g, v_meta_tokens, v_pre_mix_g, v_w_in, v_gate_b, v_dw_w, v_dw_b, v_conv_ln_g, v_conv_ln_b, v_w_conv_out, v_w_attn_out, v_w_o, v_post_mix_g, v_pre_ffn_g, v_w_ffn_in, v_w_ffn_out, v_post_ffn_g):
    given = dict(locals())
    seq, d = x.shape[1], x.shape[2]
    n_meta = meta_tokens.shape[0]
    length = n_meta + seq
    m_rows = -(-length // QB) * QB
    dc = d // N_DEV
    assert dc == LANE and n_meta % 8 == 0 and seq % 8 == 0
    fs = w_ffn_in.shape[2]
    fr = w_ffn_out.shape[1]
    assert 2 * fr == fs

    local = {k: given[k][0] for k in SHARDED}
    cast = _cast_bf16([local[k] for k in SHARDED])
    dww_pad = jnp.pad(dw_w[0], ((0, CONV_PAD - CONV_WIDTH), (0, 0)))
    wi, meta_g, dww_g = _all_gather([cast[0], meta_tokens, dww_pad])
    meta_full = jnp.concatenate([meta_g[j] for j in range(N_DEV)], axis=1)
    dww_full = jnp.concatenate([dww_g[j] for j in range(N_DEV)], axis=1)
    ns = wi.shape[2]

    tail = jnp.zeros((m_rows - length, d), F32)
    h0 = jnp.concatenate([meta_full, x[0], tail], axis=0)
    target = jnp.concatenate([jnp.zeros((n_meta, d), F32), loss_target[0], tail], axis=0)

    (u,) = _rows("pre_mix_norm", lambda r0, xs, ps: ([_rms(xs[0], ps[0])], []), [h0], [pre_mix_g], [BF16], [])
    square = list(cast[1:4])
    p, p16, *gathered = _matmul(
        "in_proj", NN, u, wi, pl.BlockSpec((m_rows, d), lambda i: (0, 0)), pl.BlockSpec((None, d, ns), lambda i: (i, 0, 0)),
        pl.BlockSpec((m_rows, ns), lambda i: (0, i)), jax.ShapeDtypeStruct((m_rows, N_DEV * ns), F32), (N_DEV,), twin_bf16=True,
        carried=(_gather_copies, square, [jax.ShapeDtypeStruct((N_DEV,) + s.shape, s.dtype) for s in square],
                 _exchange_sems(len(square)) + [pltpu.SemaphoreType.DMA((len(square),))]))
    wco, wao, wo = (g.reshape(d, d) for g in gathered)
    o, tot, (wfi, wfo) = _attn_fwd(p16, d, list(cast[4:6]))
    wfo = wfo.reshape(N_DEV // 2, fs, d)
    y = _conv_fwd(p, dww_full, dw_b, d)
    (yc,) = _rows("conv_norm", lambda r0, xs, ps: ([_ln_silu(xs[0], ps[0], ps[1])], []), [y], [conv_ln_g, conv_ln_b], [BF16], [])
    y_conv = _dense_fwd("conv_out", yc, wco)
    y_attn = _dense_fwd("attn_out", o, wao)
    gate_cols = [(d, 5), (d, 6), None, None]
    (mixin,) = _rows("gate_mix", lambda r0, xs, ps: ([_gate_mix(*xs, ps[0])], []), [p, p, y_conv, y_attn], [gate_b], [BF16], [],
                     row_in_cols=gate_cols, row_out_widths=[d])
    mix = _dense_fwd("mix_out", mixin, wo)
    h1, u2 = _rows("post_mix", lambda r0, xs, ps: (list(_post_mix(xs[0], xs[1], ps[0], ps[1])), []), [h0, mix],
                   [post_mix_g, pre_ffn_g], [F32, BF16], [])
    half = N_DEV // 2
    a_act, b_act, f_in = _ffn_in_swiglu(u2, wfi)
    f = _matmul("ffn_out", NN, f_in, wfo, pl.BlockSpec((None, m_rows, fs), lambda j: (j, 0, 0)), pl.BlockSpec((None, fs, d), lambda j: (j, 0, 0)),
                pl.BlockSpec((m_rows, d), lambda j: (0, 0)), jax.ShapeDtypeStruct((m_rows, d), F32), (half,), acc_axis=0)

    def loss_head(r0, xs, ps):
        h1_, f_, t_ = xs
        r, vjp = jax.vjp(_rms, f_, ps[0])
        rows = r0 + lax.broadcasted_iota(I32, (h1_.shape[0], 1), 0)
        real = (rows >= n_meta) & (rows < length)
        err = jnp.where(real, h1_ + r - t_, 0.0)
        dh2 = err * (1.0 / d)
        d_f, dg = vjp(dh2)
        part = jnp.sum(0.5 * jnp.mean(err * err, axis=-1, keepdims=True), axis=0, keepdims=True)
        return [d_f, dh2], [dg, jnp.broadcast_to(part, (1, LANE))]

    d_f, dh2, g_post_ffn, loss_part = _rows("loss_head", loss_head, [h1, f, target], [post_ffn_g], [BF16, F32], [d, LANE])

    d_ab = _ffn_out_dx_swiglu(d_f, wfo, a_act, b_act).reshape(N_DEV, m_rows, fs)
    g_wfo = _matmul("ffn_out_dw", TN, f_in, d_f, pl.BlockSpec((None, m_rows, fs), lambda j: (j, 0, 0)), pl.BlockSpec((m_rows, d), lambda j: (0, 0)),
                    pl.BlockSpec((None, fs, d), lambda j: (j, 0, 0)), jax.ShapeDtypeStruct((half, fs, d), F32), (half,), twin_bf16=True)

    du2 = _matmul("ffn_in_dx", NT, d_ab, wfi, pl.BlockSpec((None, m_rows, fs), lambda i: (i, 0, 0)), pl.BlockSpec((None, d, fs), lambda i: (i, 0, 0)),
                  pl.BlockSpec((m_rows, d), lambda i: (0, 0)), jax.ShapeDtypeStruct((m_rows, d), F32), (N_DEV,), acc_axis=0)
    g_wfi = _matmul("ffn_in_dw", TN, u2, d_ab, pl.BlockSpec((m_rows, d), lambda i: (0, 0)), pl.BlockSpec((None, m_rows, fs), lambda i: (i, 0, 0)),
                    pl.BlockSpec((None, d, fs), lambda i: (i, 0, 0)), jax.ShapeDtypeStruct((N_DEV, d, fs), F32), (N_DEV,), twin_bf16=True)

    def post_mix_bwd(r0, xs, ps):
        h0_, mix_, dh2_, du2_ = xs
        _, vjp = jax.vjp(_post_mix, h0_, mix_, ps[0], ps[1])
        dh0_, dmix_, dg1, dg2 = vjp((dh2_, du2_))
        return [dmix_, dh0_], [dg1, dg2]

    d_mix, dh1, g_post_mix, g_pre_ffn = _rows("post_mix_bwd", post_mix_bwd, [h0, mix, dh2, du2], [post_mix_g, pre_ffn_g],
                                              [BF16, F32], [d, d])
    d_mixin = _dense_dx("mix_out_dx", d_mix, wo, F32)
    g_wo = _dense_dw("mix_out_dw", mixin, d_mix)

    def gate_mix_bwd(r0, xs, ps):
        _, vjp = jax.vjp(_gate_mix, xs[0], xs[1], xs[2], xs[3], ps[0])
        dpgc, dpga, dyc_, dya_, dgb = vjp(xs[4])
        return [dpgc, dpga, dyc_, dya_], [dgb]

    dp_gc, dp_ga, d_yconv, d_yattn, g_gate_b = _rows(
        "gate_mix_bwd", gate_mix_bwd, [p, p, y_conv, y_attn, d_mixin], [gate_b], [BF16] * 4, [2 * d],
        row_in_cols=gate_cols + [None], row_out_widths=[d] * 4)
    d_o = _dense_dx("attn_out_dx", d_yattn, wao, BF16)
    g_wao = _dense_dw("attn_out_dw", o, d_yattn)
    d_yc = _dense_dx("conv_out_dx", d_yconv, wco, F32)
    g_wco = _dense_dw("conv_out_dw", yc, d_yconv)
    big = {"w_ffn_out": [g.reshape(N_DEV, fr, d) for g in g_wfo], "w_ffn_in": g_wfi,
           "w_o": [g.reshape(N_DEV, dc, d) for g in g_wo], "w_attn_out": [g.reshape(N_DEV, dc, d) for g in g_wao],
           "w_conv_out": [g.reshape(N_DEV, dc, d) for g in g_wco]}
    early = ("w_ffn_out", "w_ffn_in", "w_o", "w_attn_out", "w_conv_out")
    dq, dk, dv, received_early = _attn_bwd(p16, d_o, tot, d, [big[k][1] for k in early])

    def conv_norm_bwd(r0, xs, ps):
        _, vjp = jax.vjp(_ln_silu, xs[0], ps[0], ps[1])
        dy_, dg, db = vjp(xs[1])
        return [dy_], [dg, db]

    d_y, g_ln_g, g_ln_b = _rows("conv_norm_bwd", conv_norm_bwd, [y, d_yc], [conv_ln_g, conv_ln_b], [F32], [d, d])
    dp_a, dp_g, g_dww, g_dwb = _conv_bwd(p, d_y, dww_full, d)
    dp = jnp.concatenate([dp_a, dp_g, dq, dk, dv, dp_gc, dp_ga], axis=1)
    g_wi = _matmul("in_proj_dw", TN, u, dp, pl.BlockSpec((m_rows, d), lambda i: (0, 0)), pl.BlockSpec((m_rows, ns), lambda i: (0, i)),
                   pl.BlockSpec((None, d, ns), lambda i: (i, 0, 0)), jax.ShapeDtypeStruct((N_DEV, d, ns), F32), (N_DEV,), twin_bf16=True)
    x_i, y_i, c_i = _mesh_pos()
    slab_idx = jnp.stack([_dev(*chip, c_i) for chip in _other_chips(x_i, y_i)]).astype(I32)
    (wi_sibling,) = _sibling_exchange([g_wi[1]])
    wi_pairs = _pair_sum(slab_idx, g_wi[0], wi_sibling)
    du, wi_chips = _matmul(
        "in_proj_dx", NT, dp, wi, pl.BlockSpec((m_rows, ns), lambda i: (0, i)), pl.BlockSpec((None, d, ns), lambda i: (i, 0, 0)),
        pl.BlockSpec((m_rows, d), lambda i: (0, 0)), jax.ShapeDtypeStruct((m_rows, d), F32), (N_DEV,), acc_axis=0,
        carried=(_chip_copies, [wi_pairs], [jax.ShapeDtypeStruct(wi_pairs.shape, BF16)],
                 [pltpu.SemaphoreType.DMA((1, 3)), pltpu.SemaphoreType.DMA((1, 3))]))

    def pre_mix_bwd(r0, xs, ps):
        _, vjp = jax.vjp(_rms, xs[0], ps[0])
        dx, dg = vjp(xs[1])
        return [xs[2] + dx], [dg]

    dh0, g_pre_mix = _rows("pre_mix_bwd", pre_mix_bwd, [h0, du, dh1], [pre_mix_g], [F32], [d])
    grad_x = dh0[n_meta:length][None]

    me = _dev(x_i, y_i, c_i)
    me_arr = jnp.reshape(me, (1,)).astype(I32)
    big["w_in"] = g_wi
    received = {k: [r] for k, r in zip(early, received_early)}
    received["w_in"] = [wi_sibling[3:4], wi_chips]
    results = {}
    for k in SHARDED:
        outs = _adamw_shard(me_arr, big[k][0], received[k], local[k], given["m_" + k][0], given["v_" + k][0])
        results[k] = tuple(a[None] for a in outs)

    rep_grads = {"pre_mix_g": g_pre_mix, "gate_b": g_gate_b, "dw_b": g_dwb, "conv_ln_g": g_ln_g, "conv_ln_b": g_ln_b,
                 "post_mix_g": g_post_mix, "pre_ffn_g": g_pre_ffn, "post_ffn_g": g_post_ffn}

    def pack_rep(get):
        return jnp.concatenate([get(k) for k in REPLICATED], axis=1).reshape(-1, LANE)

    rep_rows = pack_rep(lambda k: rep_grads[k])
    n_rep = rep_rows.shape[0]
    loss_rows = jnp.broadcast_to(loss_part, (8, LANE))
    g_meta = dh0[0:n_meta]
    slabs = jnp.stack([jnp.concatenate([rep_rows, loss_rows, g_dww[:, j * LANE:(j + 1) * LANE], g_meta[:, j * LANE:(j + 1) * LANE]], axis=0)
                       for j in range(N_DEV)])

    def pack_small(prefix):
        dww_own = jnp.pad(given[prefix + "dw_w"][0], ((0, CONV_PAD - CONV_WIDTH), (0, 0)))
        return jnp.concatenate([pack_rep(lambda k: given[prefix + k]), jnp.zeros((8, LANE), F32), dww_own,
                                given[prefix + "meta_tokens"]], axis=0)

    small = _small_reduce_adamw(slabs, pack_small(""), pack_small("m_"), pack_small("v_"))
    loss = small[0][n_rep, 0]

    def unpack(arr):
        out = {}
        flat = arr[:n_rep].reshape(1, -1)
        off = 0
        for k in REPLICATED:
            w = given[k].shape[1]
            out[k] = flat[:, off:off + w]
            off += w
        out["dw_w"] = arr[n_rep + 8:n_rep + 8 + CONV_WIDTH][None]
        out["meta_tokens"] = arr[n_rep + 8 + CONV_PAD:n_rep + 8 + CONV_PAD + n_meta]
        return out

    small_out = [unpack(a) for a in small]
    for k in WEIGHTS:
        if k not in results:
            results[k] = tuple(s[k] for s in small_out)
    return (loss, grad_x, *[results[k][0] for k in WEIGHTS], *[results[k][1] for k in WEIGHTS],
            *[results[k][2] for k in WEIGHTS], *[results[k][3] for k in WEIGHTS])
```

```python
import jax
import jax.numpy as jnp
from jax import lax
from jax.experimental import pallas as pl
from jax.experimental.pallas import tpu as pltpu

F32 = jnp.float32
BF16 = jnp.bfloat16
I32 = jnp.int32

N_DEV = 8
LANE = 128
HEAD_DIM = 64
QB = 128
KEY_SHIFT = 2
KEY_TILES = 1 << KEY_SHIFT
KEY_CHUNK = KEY_TILES * QB
LANE_BLOCKS = 4
BWD_LANE_BLOCKS = 4
CONV_WIDTH = 31
CONV_PAD = 32
ROW_CHUNK = 128
RMS_EPS = 1e-6
LN_EPS = 1e-5
ADAM_LR = 0.001
ADAM_B1 = 0.9
ADAM_B2 = 0.999
ADAM_EPS = 1e-08
ADAM_WD = 0.01
ADAM_STEP = 10
VMEM_LIMIT = 56 * 1024 * 1024

NN = (((1,), (0,)), ((), ()))
NT = (((1,), (1,)), ((), ()))
TN = (((0,), (0,)), ((), ()))
MESH = pl.DeviceIdType.MESH
ANY = pl.BlockSpec(memory_space=pl.ANY)
VMEM_WHOLE = pl.BlockSpec(memory_space=pltpu.VMEM)


def _params(n_axes):
    return pltpu.CompilerParams(dimension_semantics=("arbitrary",) * n_axes, vmem_limit_bytes=VMEM_LIMIT)


def _row_tile(m):
    assert m % QB == 0
    return m // 4 if m % 64 == 0 else QB


def _matmul(name, dims, a, b, a_spec, b_spec, o_spec, out_shape, grid, acc_axis=None, twin_bf16=False, carried=None):
    n_car = 0 if carried is None else len(carried[1])
    n_twin = 1 if twin_bf16 else 0

    def body(a_ref, b_ref, *rest):
        car_ins, o_ref, twin, rest = rest[:n_car], rest[n_car], rest[n_car + 1:n_car + 1 + n_twin], rest[n_car + 1 + n_twin:]
        if carried is not None:
            starts, relays, waits = carried[0](car_ins, rest[:n_car], *rest[n_car:])

            @pl.when(pl.program_id(0) == 0)
            def _():
                for f in starts:
                    f()

            if relays:
                @pl.when(pl.program_id(0) == (3 * grid[0]) // 4)
                def _():
                    for f in relays:
                        f()

        r = lax.dot_general(a_ref[...], b_ref[...], dims, preferred_element_type=F32)
        if acc_axis is None:
            o_ref[...] = r.astype(o_ref.dtype)
            for t_ref in twin:
                t_ref[...] = r.astype(BF16)
        else:
            k = pl.program_id(acc_axis)

            @pl.when(k == 0)
            def _():
                o_ref[...] = r

            @pl.when(k > 0)
            def _():
                o_ref[...] += r

        if carried is not None:
            @pl.when(pl.program_id(0) == grid[0] - 1)
            def _():
                for f in waits:
                    f()

    in_specs, out_specs, out_shapes, scratch = [a_spec, b_spec], [o_spec], [out_shape], []
    if twin_bf16:
        assert acc_axis is None
        out_specs.append(o_spec)
        out_shapes.append(jax.ShapeDtypeStruct(out_shape.shape, BF16))
    if carried is not None:
        assert len(grid) == 1
        in_specs += [ANY] * n_car
        out_specs += [ANY] * n_car
        out_shapes += list(carried[2])
        scratch = list(carried[3])
    outs = pl.pallas_call(body, name=name, grid=grid, in_specs=in_specs, out_specs=out_specs, out_shape=out_shapes,
                          scratch_shapes=scratch, compiler_params=_params(len(grid)))(a, b, *(carried[1] if carried else ()))
    return outs[0] if len(outs) == 1 else outs


def _ffn_rows(m):
    return m // 2 if m % 32 == 0 else m


def _ffn_in_swiglu(u2, wfi_t):
    m, d = u2.shape
    half, fs = N_DEV // 2, wfi_t.shape[1]
    tm = _ffn_rows(m)

    def body(u_ref, wa_ref, wb_ref, a_ref, b_ref, f_ref):
        u = u_ref[...]
        a = lax.dot_general(u, wa_ref[...], NT, preferred_element_type=F32)
        b = lax.dot_general(u, wb_ref[...], NT, preferred_element_type=F32)
        a_ref[...] = a.astype(BF16)
        b_ref[...] = b.astype(BF16)
        f_ref[...] = _swiglu(a, b).astype(BF16)

    out = jax.ShapeDtypeStruct((half, m, fs), BF16)
    act = pl.BlockSpec((None, tm, fs), lambda j, i: (j, i, 0))
    return pl.pallas_call(
        body, name="ffn_in_swiglu", grid=(half, m // tm),
        in_specs=[pl.BlockSpec((tm, d), lambda j, i: (i, 0)), pl.BlockSpec((None, fs, d), lambda j, i: (j, 0, 0)),
                  pl.BlockSpec((None, fs, d), lambda j, i: (j + half, 0, 0))],
        out_specs=[act, act, act], out_shape=[out, out, out], compiler_params=_params(2))(u2, wfi_t, wfi_t)


def _ffn_out_dx_swiglu(d_f, wfo, a_act, b_act):
    m, d = d_f.shape
    half, fs = N_DEV // 2, wfo.shape[1]
    tm = _ffn_rows(m)

    def body(df_ref, w_ref, a_ref, b_ref, o_ref):
        d_fin = lax.dot_general(df_ref[...], w_ref[...], NT, preferred_element_type=F32)
        _, vjp = jax.vjp(_swiglu, a_ref[...].astype(F32), b_ref[...].astype(F32))
        d_a, d_b = vjp(d_fin)
        o_ref[0] = d_a.astype(BF16)
        o_ref[1] = d_b.astype(BF16)

    act = pl.BlockSpec((None, tm, fs), lambda j, i: (j, i, 0))
    return pl.pallas_call(
        body, name="ffn_out_dx_swiglu", grid=(half, m // tm),
        in_specs=[pl.BlockSpec((tm, d), lambda j, i: (i, 0)), pl.BlockSpec((None, fs, d), lambda j, i: (j, 0, 0)), act, act],
        out_specs=pl.BlockSpec((2, None, tm, fs), lambda j, i: (0, j, i, 0)),
        out_shape=jax.ShapeDtypeStruct((2, half, m, fs), BF16), compiler_params=_params(2))(d_f, wfo, a_act, b_act)


DENSE_TILE = 256


def _dense_fwd(name, a, w, out_dtype=F32):
    m, k = a.shape
    n = w.shape[1]
    tn = DENSE_TILE
    return _matmul(name, NN, a, w, pl.BlockSpec((m, k), lambda j: (0, 0)), pl.BlockSpec((k, tn), lambda j: (0, j)),
                   pl.BlockSpec((m, tn), lambda j: (0, j)), jax.ShapeDtypeStruct((m, n), out_dtype), (n // tn,))


def _dense_dx(name, dy, w, out_dtype):
    m, n = dy.shape
    k = w.shape[0]
    tk = DENSE_TILE
    return _matmul(name, NT, dy, w, pl.BlockSpec((m, n), lambda j: (0, 0)), pl.BlockSpec((tk, n), lambda j: (j, 0)),
                   pl.BlockSpec((m, tk), lambda j: (0, j)), jax.ShapeDtypeStruct((m, k), out_dtype), (k // tk,))


def _dense_dw(name, a, dy):
    m, k = a.shape
    n = dy.shape[1]
    tn = DENSE_TILE
    return _matmul(name, TN, a, dy, pl.BlockSpec((m, k), lambda j: (0, 0)), pl.BlockSpec((m, tn), lambda j: (0, j)),
                   pl.BlockSpec((k, tn), lambda j: (0, j)), jax.ShapeDtypeStruct((k, n), F32), (n // tn,), twin_bf16=True)


def _rowwise(name, fn, row_ins, par_ins, row_outs, par_outs, *, grid, in_specs, out_specs, tm, row_axis):
    n_ri, n_pi, n_ro, n_po = len(row_ins), len(par_ins), len(row_outs), len(par_outs)
    n_steps, tail = divmod(tm, ROW_CHUNK)
    assert tail % 16 == 0

    def body(*refs):
        ri = refs[:n_ri]
        pi = refs[n_ri:n_ri + n_pi]
        ro = refs[n_ri + n_pi:n_ri + n_pi + n_ro]
        po = refs[n_ri + n_pi + n_ro:]
        ps = [r[...] for r in pi]
        base = pl.program_id(row_axis) * tm

        def chunk(r0, rows, carry):
            xs = [r[pl.ds(r0, rows), :] for r in ri]
            outs, pouts = fn(base + r0, xs, ps)
            for r, o in zip(ro, outs):
                if isinstance(o, (list, tuple)):
                    for j, part in enumerate(o):
                        r[j, pl.ds(r0, rows), :] = part.astype(r.dtype)
                else:
                    r[pl.ds(r0, rows), :] = o.astype(r.dtype)
            return tuple(c + q for c, q in zip(carry, pouts))

        def step(i, carry):
            return chunk(pl.multiple_of(i * ROW_CHUNK, ROW_CHUNK), ROW_CHUNK, carry)

        acc = lax.fori_loop(0, n_steps, step, tuple(jnp.zeros(s.shape, F32) for s in par_outs))
        if tail:
            acc = chunk(n_steps * ROW_CHUNK, tail, acc)
        if n_po:
            first = pl.program_id(0) == 0
            for ax in range(1, len(grid)):
                first = first & (pl.program_id(ax) == 0)

            @pl.when(first)
            def _():
                for r in po:
                    r[...] = jnp.zeros_like(r)

            for r, a in zip(po, acc):
                r[...] += a

    return pl.pallas_call(body, name=name, grid=grid, in_specs=in_specs, out_specs=out_specs,
                          out_shape=tuple(row_outs) + tuple(par_outs),
                          compiler_params=_params(len(grid)))(*row_ins, *par_ins)


def _rows(name, fn, row_ins, par_ins, row_out_dtypes, par_out_widths, row_in_cols=None, row_out_widths=None):
    m = row_ins[0].shape[0]
    tm = _row_tile(m)
    in_specs = []
    for k, a in enumerate(row_ins):
        if row_in_cols is not None and row_in_cols[k] is not None:
            width, cb = row_in_cols[k]
            in_specs.append(pl.BlockSpec((tm, width), lambda i, cb=cb: (i, cb)))
        else:
            in_specs.append(pl.BlockSpec((tm, a.shape[1]), lambda i: (i, 0)))
    for a in par_ins:
        in_specs.append(pl.BlockSpec(a.shape, lambda i: (0, 0)))
    if row_out_widths is None:
        row_out_widths = [row_ins[0].shape[1]] * len(row_out_dtypes)
    row_outs = [jax.ShapeDtypeStruct((m, w), dt) for w, dt in zip(row_out_widths, row_out_dtypes)]
    par_outs = [jax.ShapeDtypeStruct((1, w), F32) for w in par_out_widths]
    out_specs = [pl.BlockSpec((tm, s.shape[1]), lambda i: (i, 0)) for s in row_outs]
    out_specs += [pl.BlockSpec(s.shape, lambda i: (0, 0)) for s in par_outs]
    return _rowwise(name, fn, row_ins, par_ins, row_outs, par_outs, grid=(m // tm,), in_specs=in_specs,
                    out_specs=out_specs, tm=tm, row_axis=0)


def _rms(x, g):
    return x * lax.rsqrt(jnp.mean(x * x, axis=-1, keepdims=True) + RMS_EPS) * g


def _ln_silu(y, g, b):
    mu = jnp.mean(y, axis=-1, keepdims=True)
    yc = y - mu
    var = jnp.mean(yc * yc, axis=-1, keepdims=True)
    return jax.nn.silu(yc * lax.rsqrt(var + LN_EPS) * g + b)


def _gate_mix(pgc, pga, yc, ya, gb):
    d = pgc.shape[1]
    return jax.nn.sigmoid(pgc + gb[:, :d]) * yc + jax.nn.sigmoid(pga + gb[:, d:]) * ya


def _post_mix(h0, mix, g_post, g_pre):
    h1 = h0 + _rms(mix, g_post)
    return h1, _rms(h1, g_pre)


def _swiglu(a, b):
    return jax.nn.silu(a) * b


def _conv_taps():
    taps = []
    for b in range(8):
        for a in range(CONV_PAD // 8):
            s = 8 * a + b
            if s < CONV_WIDTH:
                taps.append((b, a, CONV_WIDTH - 1 - s))
    return taps


def _conv_fwd(p, dww, dwb, d_model):
    m = p.shape[0]
    nch = d_model // LANE
    n_chunk = m // QB
    taps = _conv_taps()

    def body(a_ref, g_ref, w_ref, b_ref, y_ref, upad):
        upad[0:CONV_PAD, :] = jnp.zeros((CONV_PAD, LANE), F32)

        def fill(i, c):
            r0 = pl.multiple_of(i * QB, QB)
            u = a_ref[pl.ds(r0, QB), :] * jax.nn.sigmoid(g_ref[pl.ds(r0, QB), :])
            upad[pl.ds(pl.multiple_of(r0 + CONV_PAD, 8), QB), :] = u
            return c

        lax.fori_loop(0, n_chunk, fill, 0)

        def conv(i, c):
            r0 = pl.multiple_of(i * QB, QB)
            win = upad[pl.ds(r0, QB + CONV_PAD), :]
            acc = jnp.broadcast_to(b_ref[...], (QB, LANE))
            rolled = {}
            for b, a, j in taps:
                if b not in rolled:
                    rolled[b] = win if b == 0 else pltpu.roll(win, b, axis=0)
                lo = CONV_PAD - 8 * a
                acc = acc + w_ref[j:j + 1, :] * rolled[b][lo:lo + QB, :]
            y_ref[pl.ds(r0, QB), :] = acc
            return c

        lax.fori_loop(0, n_chunk, conv, 0)

    col = lambda off: pl.BlockSpec((m, LANE), lambda c: (0, off + c))
    return pl.pallas_call(
        body, name="conv_fwd", grid=(nch,),
        in_specs=[col(0), col(nch), pl.BlockSpec((CONV_PAD, LANE), lambda c: (0, c)), pl.BlockSpec((1, LANE), lambda c: (0, c))],
        out_specs=col(0), out_shape=jax.ShapeDtypeStruct((m, d_model), F32),
        scratch_shapes=[pltpu.VMEM((m + CONV_PAD, LANE), F32)], compiler_params=_params(1))(p, p, dww, dwb)


def _conv_bwd(p, dy, dww, d_model):
    m = p.shape[0]
    nch = d_model // LANE
    n_chunk = m // QB
    taps = _conv_taps()
    win_rows = QB + CONV_PAD

    def body(a_ref, g_ref, dy_ref, w_ref, da_ref, dg_ref, dw_ref, db_ref, upad, dypad, wacc, bacc):
        upad[0:CONV_PAD, :] = jnp.zeros((CONV_PAD, LANE), F32)
        dypad[m:m + CONV_PAD, :] = jnp.zeros((CONV_PAD, LANE), F32)
        wacc[...] = jnp.zeros_like(wacc)
        bacc[...] = jnp.zeros_like(bacc)

        def fill(i, c):
            r0 = pl.multiple_of(i * QB, QB)
            u = a_ref[pl.ds(r0, QB), :] * jax.nn.sigmoid(g_ref[pl.ds(r0, QB), :])
            upad[pl.ds(pl.multiple_of(r0 + CONV_PAD, 8), QB), :] = u
            dypad[pl.ds(r0, QB), :] = dy_ref[pl.ds(r0, QB), :]
            return c

        lax.fori_loop(0, n_chunk, fill, 0)

        def chunk(i, c):
            r0 = pl.multiple_of(i * QB, QB)
            dwin = dypad[pl.ds(r0, win_rows), :]
            du = jnp.zeros((QB, LANE), F32)
            rolled = {}
            for b, a, j in taps:
                if b not in rolled:
                    rolled[b] = dwin if b == 0 else pltpu.roll(dwin, win_rows - b, axis=0)
                du = du + w_ref[j:j + 1, :] * rolled[b][8 * a:8 * a + QB, :]
            av = a_ref[pl.ds(r0, QB), :]
            sg = jax.nn.sigmoid(g_ref[pl.ds(r0, QB), :])
            da_ref[pl.ds(r0, QB), :] = (du * sg).astype(da_ref.dtype)
            dg_ref[pl.ds(r0, QB), :] = (du * av * sg * (1.0 - sg)).astype(dg_ref.dtype)
            dyc = dy_ref[pl.ds(r0, QB), :]
            uwin = upad[pl.ds(r0, win_rows), :]
            rolled = {}
            for b, a, j in taps:
                if b not in rolled:
                    rolled[b] = uwin if b == 0 else pltpu.roll(uwin, b, axis=0)
                lo = CONV_PAD - 8 * a
                prod = dyc * rolled[b][lo:lo + QB, :]
                wacc[j] += prod.reshape(QB // 8, 8, LANE).sum(axis=0)
            bacc[...] += dyc.reshape(QB // 8, 8, LANE).sum(axis=0)
            return c

        lax.fori_loop(0, n_chunk, chunk, 0)
        for j in range(CONV_WIDTH):
            dw_ref[j:j + 1, :] = jnp.sum(wacc[j], axis=0, keepdims=True)
        dw_ref[CONV_WIDTH:CONV_PAD, :] = jnp.zeros((CONV_PAD - CONV_WIDTH, LANE), F32)
        db_ref[...] = jnp.sum(bacc[...], axis=0, keepdims=True)

    col = lambda off: pl.BlockSpec((m, LANE), lambda c: (0, off + c))
    return pl.pallas_call(
        body, name="conv_bwd", grid=(nch,),
        in_specs=[col(0), col(nch), col(0), pl.BlockSpec((CONV_PAD, LANE), lambda c: (0, c))],
        out_specs=[col(0), col(0), pl.BlockSpec((CONV_PAD, LANE), lambda c: (0, c)), pl.BlockSpec((1, LANE), lambda c: (0, c))],
        out_shape=(jax.ShapeDtypeStruct((m, d_model), BF16), jax.ShapeDtypeStruct((m, d_model), BF16),
                   jax.ShapeDtypeStruct((CONV_PAD, d_model), F32), jax.ShapeDtypeStruct((1, d_model), F32)),
        scratch_shapes=[pltpu.VMEM((m + CONV_PAD, LANE), F32), pltpu.VMEM((m + CONV_PAD, LANE), F32),
                        pltpu.VMEM((CONV_PAD, 8, LANE), F32), pltpu.VMEM((8, LANE), F32)],
        compiler_params=_params(1))(p, p, dy, dww)


EXP_CLAMP = 80.0


def _one_plus_exp(z):
    return 1.0 + jnp.exp(jnp.minimum(z, EXP_CLAMP))


def _softplus(z):
    return jnp.maximum(jnp.log(_one_plus_exp(z)), z)


def _softplus_sigmoid(z):
    s = _one_plus_exp(z)
    return jnp.maximum(jnp.log(s), z), 1.0 - 1.0 / s


def _tile_cumsums(x, tri2):
    xb = x.astype(BF16)
    out = []
    for i in range(0, x.shape[1] // QB, 2):
        both = jnp.dot(xb[:, i * QB:(i + 2) * QB], tri2, preferred_element_type=F32)
        out += [both[:, :QB], both[:, QB:]]
    return out


def _tri2(kind):
    jj = lax.broadcasted_iota(I32, (2 * QB, 2 * QB), 0)
    ss = lax.broadcasted_iota(I32, (2 * QB, 2 * QB), 1)
    same = (jj >= QB) == (ss >= QB)
    keep = {"ge": jj >= ss, "le": jj <= ss}[kind]
    return jnp.where(same & keep, 1.0, 0.0).astype(BF16)


def _attn_fwd(p, d_model, shards):
    m = p.shape[0]
    nqb = m // QB
    ngrp = d_model // (LANE_BLOCKS * LANE)
    qo, ko, vo = 2 * ngrp, 3 * ngrp, 4 * ngrp
    scale = HEAD_DIM ** -0.5
    n_sh = len(shards)

    assert nqb >= KEY_TILES

    def body(q_ref, k_ref, v_ref, *rest):
        shard_refs, (o_ref, t_ref), rest = rest[:n_sh], rest[n_sh:n_sh + 2], rest[n_sh + 2:]
        gathered_refs, (acc_ref, car_ref), sems = rest[:n_sh], rest[n_sh:n_sh + 2], rest[n_sh + 2:]
        starts, relays, waits = _gather_copies(shard_refs, gathered_refs, *sems)

        @pl.when((pl.program_id(0) == 0) & (pl.program_id(1) == 0))
        def _():
            for f in starts:
                f()

        @pl.when((pl.program_id(0) == ngrp - 1) & (pl.program_id(1) == (5 * nqb) // 8))
        def _():
            for f in relays:
                f()

        qb = pl.program_id(1)
        lane = lax.broadcasted_iota(I32, (QB, LANE), 1)
        head0 = lane < HEAD_DIM
        row_g = qb * QB + lax.broadcasted_iota(I32, (QB, KEY_CHUNK), 0)
        col_l = lax.broadcasted_iota(I32, (QB, KEY_CHUNK), 1)
        tri = _tri2("ge")
        heads = range(2 * LANE_BLOCKS)
        qh = []
        for lb in range(LANE_BLOCKS):
            q2 = (q_ref[:, lb * LANE:(lb + 1) * LANE] * scale).astype(BF16)
            zero = jnp.zeros_like(q2)
            qh += [jnp.where(head0, q2, zero), jnp.where(head0, zero, q2)]
        acc_ref[...] = jnp.zeros_like(acc_ref)
        car_ref[...] = jnp.zeros_like(car_ref)

        def chunk(first_tile, bound, n_tiles=KEY_TILES):
            r0 = pl.multiple_of(first_tile * QB, QB)
            keys = n_tiles * QB
            kcs = [k_ref[pl.ds(r0, keys), lb * LANE:(lb + 1) * LANE].astype(BF16) for lb in range(LANE_BLOCKS)]
            vcs = [v_ref[pl.ds(r0, keys), lb * LANE:(lb + 1) * LANE].astype(BF16) for lb in range(LANE_BLOCKS)]
            valid = None if bound is None else (col_l[:, :keys] + r0) < bound
            zs = [lax.dot_general(qh[h], kcs[h // 2], NT, preferred_element_type=F32) for h in heads]
            sps = [_softplus(z) for z in zs]
            if valid is not None:
                sps = [jnp.where(valid, sp, 0.0) for sp in sps]
            cums = [_tile_cumsums(sp, tri) for sp in sps]
            cars = [car_ref[h] for h in heads]
            a_tiles = [[None] * n_tiles for h in heads]
            for i in reversed(range(n_tiles)):
                for h in heads:
                    cum = cums[h][i]
                    a_tiles[h][i] = jnp.exp(zs[h][:, i * QB:(i + 1) * QB] - (cum + cars[h]))
                    cars[h] = cars[h] + jnp.broadcast_to(cum[:, 0:1], cum.shape)
            for h in heads:
                a = jnp.concatenate(a_tiles[h], axis=1)
                if valid is not None:
                    a = jnp.where(valid, a, 0.0)
                acc_ref[h] += jnp.dot(a.astype(BF16), vcs[h // 2], preferred_element_type=F32)
                car_ref[h] = cars[h]

        near = jnp.maximum(qb - (KEY_TILES - 1), 0)
        chunk(near, row_g)
        n_full = lax.shift_right_logical(near, KEY_SHIFT)

        def step(i, c):
            chunk(near - KEY_TILES * (i + 1), None)
            return c

        lax.fori_loop(0, n_full, step, 0)
        left = near - KEY_TILES * n_full

        @pl.when((left > 0) & (left <= 2))
        def _():
            chunk(0, left * QB, n_tiles=2)

        @pl.when(left > 2)
        def _():
            chunk(0, left * QB)

        for lb in range(LANE_BLOCKS):
            o_ref[:, lb * LANE:(lb + 1) * LANE] = jnp.where(head0, acc_ref[2 * lb], acc_ref[2 * lb + 1]).astype(o_ref.dtype)
        for h in heads:
            t_ref[:, h * QB:(h + 1) * QB] = car_ref[h]

        @pl.when((pl.program_id(0) == ngrp - 1) & (pl.program_id(1) == nqb - 1))
        def _():
            for f in waits:
                f()

    wide = LANE_BLOCKS * LANE
    outs = pl.pallas_call(
        body, name="attn_fwd", grid=(ngrp, nqb),
        in_specs=[pl.BlockSpec((QB, wide), lambda g, qb: (qb, qo + g)),
                  pl.BlockSpec((m, wide), lambda g, qb: (0, ko + g)),
                  pl.BlockSpec((m, wide), lambda g, qb: (0, vo + g))] + [ANY] * n_sh,
        out_specs=[pl.BlockSpec((QB, wide), lambda g, qb: (qb, g)),
                   pl.BlockSpec((QB, 2 * wide), lambda g, qb: (qb, g))] + [ANY] * n_sh,
        out_shape=[jax.ShapeDtypeStruct((m, d_model), BF16), jax.ShapeDtypeStruct((m, 2 * d_model), F32)]
        + [jax.ShapeDtypeStruct((N_DEV,) + s.shape, s.dtype) for s in shards],
        scratch_shapes=[pltpu.VMEM((2 * LANE_BLOCKS, QB, LANE), F32), pltpu.VMEM((2 * LANE_BLOCKS, QB, LANE), F32)]
        + _exchange_sems(n_sh) + [pltpu.SemaphoreType.DMA((n_sh,))],
        compiler_params=_params(2))(p, p, p, *shards)
    return outs[0], outs[1], outs[2:]


def _attn_bwd(p, d_o, tot, d_model, slabs):
    m = p.shape[0]
    nqb = m // QB
    blocks = BWD_LANE_BLOCKS
    ngrp = d_model // (blocks * LANE)
    qo, ko, vo = 2 * ngrp, 3 * ngrp, 4 * ngrp
    scale = HEAD_DIM ** -0.5
    n_sl = len(slabs)

    assert nqb >= KEY_TILES

    def body(q_ref, k_ref, v_ref, do_ref, t_ref, *rest):
        slab_refs, (dq_ref, dk_ref, dv_ref), rest = rest[:n_sl], rest[n_sl:n_sl + 3], rest[n_sl + 3:]
        recv_refs, (dkacc, dvacc, dqacc, csp, cg), sems = rest[:n_sl], rest[n_sl:n_sl + 5], rest[n_sl + 5:]
        starts, waits = _scatter_copies(slab_refs, recv_refs, *sems)

        @pl.when((pl.program_id(0) == 0) & (pl.program_id(1) == 0))
        def _():
            for f in starts:
                f()

        qb = pl.program_id(1)
        lane = lax.broadcasted_iota(I32, (QB, LANE), 1)
        head0 = lane < HEAD_DIM
        row_g = qb * QB + lax.broadcasted_iota(I32, (QB, KEY_CHUNK), 0)
        col_l = lax.broadcasted_iota(I32, (QB, KEY_CHUNK), 1)
        tri_ge = _tri2("ge")
        tri_le = _tri2("le")
        heads = range(2 * blocks)
        qh, doh = [], []
        for lb in range(blocks):
            q2 = (q_ref[:, lb * LANE:(lb + 1) * LANE] * scale).astype(BF16)
            do2 = do_ref[:, lb * LANE:(lb + 1) * LANE]
            zero = jnp.zeros_like(q2)
            qh += [jnp.where(head0, q2, zero), jnp.where(head0, zero, q2)]
            doh += [jnp.where(head0, do2, zero), jnp.where(head0, zero, do2)]
        q_pairs = [jnp.concatenate(qh[2 * lb:2 * lb + 2], axis=0) for lb in range(blocks)]
        do_pairs = [jnp.concatenate(doh[2 * lb:2 * lb + 2], axis=0) for lb in range(blocks)]

        @pl.when(qb == 0)
        def _():
            dkacc[...] = jnp.zeros_like(dkacc)
            dvacc[...] = jnp.zeros_like(dvacc)

        dqacc[...] = jnp.zeros_like(dqacc)
        for h in heads:
            csp[h] = t_ref[:, h * QB:(h + 1) * QB]
        cg[...] = jnp.zeros_like(cg)

        def chunk(first_tile, bound, n_tiles=KEY_TILES):
            r0 = pl.multiple_of(first_tile * QB, QB)
            keys = n_tiles * QB
            kcs = [k_ref[pl.ds(r0, keys), lb * LANE:(lb + 1) * LANE].astype(BF16) for lb in range(blocks)]
            vcs = [v_ref[pl.ds(r0, keys), lb * LANE:(lb + 1) * LANE].astype(BF16) for lb in range(blocks)]
            valid = None if bound is None else (col_l[:, :keys] + r0) < bound
            tiles = [slice(i * QB, (i + 1) * QB) for i in range(n_tiles)]
            zs = [lax.dot_general(qh[h], kcs[h // 2], NT, preferred_element_type=F32) for h in heads]
            das = [lax.dot_general(doh[h], vcs[h // 2], NT, preferred_element_type=F32) for h in heads]
            sps, sgs = zip(*[_softplus_sigmoid(z) for z in zs])
            if valid is not None:
                sps = [jnp.where(valid, sp, 0.0) for sp in sps]
            cums = [_tile_cumsums(sp, tri_ge) for sp in sps]
            a_tiles, g_tiles = [[] for h in heads], [[] for h in heads]
            for h in heads:
                rest = csp[h]
                for i, c in enumerate(tiles):
                    cum = cums[h][i]
                    rest = rest - jnp.broadcast_to(cum[:, 0:1], cum.shape)
                    a = jnp.exp(zs[h][:, c] - (cum + rest))
                    if valid is not None:
                        a = jnp.where(valid[:, c], a, 0.0)
                    a_tiles[h].append(a)
                    g_tiles[h].append(a * das[h][:, c])
                csp[h] = rest
            gcums = [_tile_cumsums(jnp.concatenate(g_tiles[h], axis=1), tri_le) for h in heads]
            dzbs, abs_ = [], []
            for h in heads:
                g_before = cg[h]
                dz_tiles = []
                for i, c in enumerate(tiles):
                    gcum = gcums[h][i]
                    dz = g_tiles[h][i] - sgs[h][:, c] * (g_before + gcum)
                    if valid is not None:
                        dz = jnp.where(valid[:, c], dz, 0.0)
                    dz_tiles.append(dz)
                    g_before = g_before + jnp.broadcast_to(gcum[:, QB - 1:QB], gcum.shape)
                cg[h] = g_before
                dzbs.append(jnp.concatenate(dz_tiles, axis=1).astype(BF16))
                abs_.append(jnp.concatenate(a_tiles[h], axis=1).astype(BF16))
            for h in heads:
                dqacc[h] += jnp.dot(dzbs[h], kcs[h // 2], preferred_element_type=F32)
            for lb in range(blocks):
                dz_pair = jnp.concatenate(dzbs[2 * lb:2 * lb + 2], axis=0)
                a_pair = jnp.concatenate(abs_[2 * lb:2 * lb + 2], axis=0)
                dkacc[pl.ds(r0, keys), lb * LANE:(lb + 1) * LANE] += lax.dot_general(
                    dz_pair, q_pairs[lb], TN, preferred_element_type=F32)
                dvacc[pl.ds(r0, keys), lb * LANE:(lb + 1) * LANE] += lax.dot_general(
                    a_pair, do_pairs[lb], TN, preferred_element_type=F32)

        near = jnp.maximum(qb - (KEY_TILES - 1), 0)
        n_full = lax.shift_right_logical(near, KEY_SHIFT)

        def step(i, c):
            chunk(KEY_TILES * i, None)
            return c

        lax.fori_loop(0, n_full, step, 0)
        left = near - KEY_TILES * n_full

        @pl.when((left > 0) & (left <= 2))
        def _():
            chunk(KEY_TILES * n_full, near * QB, n_tiles=2)

        @pl.when(left > 2)
        def _():
            chunk(KEY_TILES * n_full, near * QB)

        chunk(near, row_g)
        for lb in range(blocks):
            dq2 = jnp.where(head0, dqacc[2 * lb], dqacc[2 * lb + 1]) * scale
            dq_ref[:, lb * LANE:(lb + 1) * LANE] = dq2.astype(dq_ref.dtype)

        @pl.when(qb == nqb - 1)
        def _():
            dk_ref[...] = dkacc[...].astype(dk_ref.dtype)
            dv_ref[...] = dvacc[...].astype(dv_ref.dtype)

        @pl.when((pl.program_id(0) == ngrp - 1) & (qb == nqb - 1))
        def _():
            for f in waits:
                f()

    out = jax.ShapeDtypeStruct((m, d_model), BF16)
    wide = blocks * LANE
    carry = pltpu.VMEM((2 * blocks, QB, LANE), F32)
    outs = pl.pallas_call(
        body, name="attn_bwd", grid=(ngrp, nqb),
        in_specs=[pl.BlockSpec((QB, wide), lambda g, qb: (qb, qo + g)),
                  pl.BlockSpec((m, wide), lambda g, qb: (0, ko + g)),
                  pl.BlockSpec((m, wide), lambda g, qb: (0, vo + g)),
                  pl.BlockSpec((QB, wide), lambda g, qb: (qb, g)),
                  pl.BlockSpec((QB, 2 * wide), lambda g, qb: (qb, g))] + [ANY] * n_sl,
        out_specs=[pl.BlockSpec((QB, wide), lambda g, qb: (qb, g)),
                   pl.BlockSpec((m, wide), lambda g, qb: (0, g)),
                   pl.BlockSpec((m, wide), lambda g, qb: (0, g))] + [ANY] * n_sl,
        out_shape=[out, out, out] + _received_shapes(slabs),
        scratch_shapes=[pltpu.VMEM((m, wide), F32), pltpu.VMEM((m, wide), F32), carry, carry, carry] + _exchange_sems(n_sl),
        compiler_params=_params(2))(p, p, p, d_o, tot, *slabs)
    return outs[0], outs[1], outs[2], outs[3:]


def _mesh_pos():
    return lax.axis_index("x"), lax.axis_index("y"), lax.axis_index("c")


def _other_chips(x, y):
    return [(1 - x, y), (x, 1 - y), (1 - x, 1 - y)]


def _dev(x, y, c):
    return 4 * x + 2 * y + c


def _all_gather(shards):
    n = len(shards)

    def body(*refs):
        starts, relays, waits = _gather_copies(refs[:n], refs[n:2 * n], *refs[2 * n:])
        for f in starts + relays + waits:
            f()

    return pl.pallas_call(
        body, name="comm_all_gather", in_specs=[ANY] * n, out_specs=[ANY] * n,
        out_shape=[jax.ShapeDtypeStruct((N_DEV,) + s.shape, s.dtype) for s in shards],
        scratch_shapes=[pltpu.SemaphoreType.DMA((n, 7)), pltpu.SemaphoreType.DMA((n, 7)), pltpu.SemaphoreType.DMA((n,))],
    )(*shards)


def _peers(x, y, c):
    out = []
    for mask in range(1, N_DEV):
        px, py, pc = x ^ (mask >> 2), y ^ ((mask >> 1) & 1), c ^ (mask & 1)
        out.append((mask - 1, (px, py, pc), _dev(px, py, pc)))
    return out


def _remote(src, dst, send_sems, recv_sems, k, s, peer):
    return pltpu.make_async_remote_copy(src_ref=src, dst_ref=dst, send_sem=send_sems.at[k, s], recv_sem=recv_sems.at[k, s],
                                        device_id=peer, device_id_type=MESH)


def _gather_copies(ins, outs, send_sems, recv_sems, local_sems):
    x, y, c = _mesh_pos()
    sibling = (x, y, 1 - c)
    chips = _other_chips(x, y)
    starts, relays, waits = [], [], []
    for k in range(len(ins)):
        def slot(block, k=k):
            return outs[k].at[_dev(*block)]

        def copy(s, src, block, to, k=k):
            return _remote(src, slot(block), send_sems, recv_sems, k, s, to)

        own = pltpu.make_async_copy(ins[k], slot((x, y, c)), local_sems.at[k])
        to_sibling = copy(0, ins[k], (x, y, c), sibling)
        starts += [own.start, to_sibling.start]
        waits += [own.wait, to_sibling.wait_send, copy(0, ins[k], (x, y, 1 - c), sibling).wait_recv]
        for j, chip in enumerate(chips):
            out = copy(1 + j, ins[k], (x, y, c), (*chip, c))
            relay = copy(4 + j, slot((*chip, c)), (*chip, c), sibling)
            starts.append(out.start)
            relays += [copy(1 + j, ins[k], (*chip, c), sibling).wait_recv, relay.start]
            waits += [out.wait_send, relay.wait_send, copy(4 + j, ins[k], (*chip, 1 - c), sibling).wait_recv]
    return starts, relays, waits


def _scatter_copies(ins, outs, send_sems, recv_sems):
    x, y, c = _mesh_pos()
    starts, waits = [], []
    for k in range(len(ins)):
        for s, peer, idx in _peers(x, y, c):
            send = _remote(ins[k].at[idx], outs[k].at[s], send_sems, recv_sems, k, s, peer)
            starts.append(send.start)
            waits += [send.wait_recv, send.wait_send]
    return starts, waits


def _exchange_sems(n):
    return [pltpu.SemaphoreType.DMA((n, N_DEV - 1)), pltpu.SemaphoreType.DMA((n, N_DEV - 1))]


def _received_shapes(slabs):
    return [jax.ShapeDtypeStruct((N_DEV - 1,) + a.shape[1:], a.dtype) for a in slabs]


def _chips_and_own(x, y):
    return _other_chips(x, y) + [(x, y)]


def _sibling_exchange(slabs):
    n = len(slabs)

    def body(*refs):
        ins, outs, (send_sems, recv_sems) = refs[:n], refs[n:2 * n], refs[2 * n:]
        x, y, c = _mesh_pos()
        copies = [_remote(ins[k].at[_dev(*chip, 1 - c)], outs[k].at[r], send_sems, recv_sems, k, r, (x, y, 1 - c))
                  for k in range(n) for r, chip in enumerate(_chips_and_own(x, y))]
        for cp in copies:
            cp.start()
        for cp in copies:
            cp.wait_recv()
        for cp in copies:
            cp.wait_send()

    return pl.pallas_call(body, name="comm_rs_sibling", in_specs=[ANY] * n, out_specs=[ANY] * n,
                          out_shape=[jax.ShapeDtypeStruct((4,) + a.shape[1:], a.dtype) for a in slabs],
                          scratch_shapes=[pltpu.SemaphoreType.DMA((n, 4)), pltpu.SemaphoreType.DMA((n, 4))])(*slabs)


def _chip_copies(ins, outs, send_sems, recv_sems):
    x, y, c = _mesh_pos()
    starts, waits = [], []
    for k in range(len(ins)):
        for r, chip in enumerate(_other_chips(x, y)):
            cp = _remote(ins[k].at[r], outs[k].at[r], send_sems, recv_sems, k, r, (*chip, c))
            starts.append(cp.start)
            waits += [cp.wait_recv, cp.wait_send]
    return starts, [], waits


def _pair_sum(slab_idx, grad, from_sibling):
    _, rows, cols = grad.shape
    tr = _shard_tile(rows)

    def body(idx_ref, g_ref, s_ref, o_ref):
        o_ref[...] = (g_ref[...] + s_ref[...].astype(F32)).astype(o_ref.dtype)

    gs = pltpu.PrefetchScalarGridSpec(
        num_scalar_prefetch=1, grid=(3, rows // tr),
        in_specs=[pl.BlockSpec((None, tr, cols), lambda r, i, idx: (idx[r], i, 0)),
                  pl.BlockSpec((None, tr, cols), lambda r, i, idx: (r, i, 0))],
        out_specs=pl.BlockSpec((None, tr, cols), lambda r, i, idx: (r, i, 0)))
    return pl.pallas_call(body, name="rs_pair_sum", grid_spec=gs, out_shape=jax.ShapeDtypeStruct((3, rows, cols), BF16),
                          compiler_params=_params(2))(slab_idx, grad, from_sibling)


def _shard_tile(rows):
    for tr in range(min(rows, 352), 0, -1):
        if rows % tr == 0 and (tr % 16 == 0 or tr == rows):
            return tr


def _adamw_math(w, g, m, v):
    m = ADAM_B1 * m + (1.0 - ADAM_B1) * g
    v = ADAM_B2 * v + (1.0 - ADAM_B2) * (g * g)
    m_hat = m / (1.0 - ADAM_B1 ** ADAM_STEP)
    v_hat = v / (1.0 - ADAM_B2 ** ADAM_STEP)
    delta = -ADAM_LR * (m_hat / (jnp.sqrt(v_hat) + ADAM_EPS) + ADAM_WD * w)
    return delta, m, v


def _adamw_shard(me, grad, received, w, m, v):
    rows, cols = w.shape
    tr = _shard_tile(rows)
    n_rec = len(received)

    def body(me_ref, g_ref, *rest):
        r_refs, (w_ref, m_ref, v_ref, go_ref, do_ref, mo_ref, vo_ref) = rest[:n_rec], rest[n_rec:]
        g = g_ref[...]
        for r_ref in r_refs:
            for s in range(r_ref.shape[0]):
                g = g + r_ref[s].astype(F32)
        delta, m_new, v_new = _adamw_math(w_ref[...], g, m_ref[...], v_ref[...])
        go_ref[...] = g
        do_ref[...] = delta
        mo_ref[...] = m_new
        vo_ref[...] = v_new

    flat = pl.BlockSpec((tr, cols), lambda i, me: (i, 0))
    gs = pltpu.PrefetchScalarGridSpec(
        num_scalar_prefetch=1, grid=(rows // tr,),
        in_specs=[pl.BlockSpec((None, tr, cols), lambda i, me: (me[0], i, 0))]
        + [pl.BlockSpec((r.shape[0], tr, cols), lambda i, me: (0, i, 0)) for r in received] + [flat, flat, flat],
        out_specs=[flat, flat, flat, flat])
    out = jax.ShapeDtypeStruct((rows, cols), F32)
    return pl.pallas_call(body, name="adamw_shard", grid_spec=gs, out_shape=(out, out, out, out),
                          compiler_params=_params(1))(me, grad, *received, w, m, v)


def _small_reduce_adamw(slabs, w, m, v):
    _, rows, _ = slabs.shape

    def body(s_ref, w_ref, m_ref, v_ref, g_ref, d_ref, mo_ref, vo_ref, land, send_sems, recv_sems):
        x, y, c = _mesh_pos()
        me = _dev(x, y, c)
        copies = []
        for mask in range(1, N_DEV):
            px, py, pc = x ^ (mask >> 2), y ^ ((mask >> 1) & 1), c ^ (mask & 1)
            copies.append(pltpu.make_async_remote_copy(
                src_ref=s_ref.at[_dev(px, py, pc)], dst_ref=land.at[me], send_sem=send_sems.at[mask - 1],
                recv_sem=recv_sems.at[mask - 1], device_id=(px, py, pc), device_id_type=MESH))
        for cp in copies:
            cp.start()
        land[me] = s_ref[me]
        for mask in range(1, N_DEV):
            px, py, pc = x ^ (mask >> 2), y ^ ((mask >> 1) & 1), c ^ (mask & 1)
            pltpu.make_async_remote_copy(
                src_ref=s_ref.at[me], dst_ref=land.at[_dev(px, py, pc)], send_sem=send_sems.at[mask - 1],
                recv_sem=recv_sems.at[mask - 1], device_id=(px, py, pc), device_id_type=MESH).wait_recv()
        for cp in copies:
            cp.wait_send()
        g = land[0]
        for d in range(1, N_DEV):
            g = g + land[d]
        delta, m_new, v_new = _adamw_math(w_ref[...], g, m_ref[...], v_ref[...])
        g_ref[...] = g
        d_ref[...] = delta
        mo_ref[...] = m_new
        vo_ref[...] = v_new

    out = jax.ShapeDtypeStruct((rows, LANE), F32)
    return pl.pallas_call(
        body, name="comm_small_reduce_adamw", in_specs=[VMEM_WHOLE] * 4, out_specs=[VMEM_WHOLE] * 4, out_shape=(out, out, out, out),
        scratch_shapes=[pltpu.VMEM((N_DEV, rows, LANE), F32), pltpu.SemaphoreType.DMA((N_DEV - 1,)),
                        pltpu.SemaphoreType.DMA((N_DEV - 1,))],
    )(slabs, w, m, v)


def _cast_bf16(arrs):
    n = len(arrs)

    def body(*refs):
        for i_ref, o_ref in zip(refs[:n], refs[n:]):
            o_ref[...] = i_ref[...].astype(BF16)

    return pl.pallas_call(body, name="cast_bf16", in_specs=[VMEM_WHOLE] * n, out_specs=[VMEM_WHOLE] * n,
                          out_shape=[jax.ShapeDtypeStruct(a.shape, BF16) for a in arrs],
                          compiler_params=pltpu.CompilerParams(vmem_limit_bytes=VMEM_LIMIT))(*arrs)


REPLICATED = ("pre_mix_g", "gate_b", "dw_b", "conv_ln_g", "conv_ln_b", "post_mix_g", "pre_ffn_g", "post_ffn_g")
SHARDED = ("w_in", "w_conv_out", "w_attn_out", "w_o", "w_ffn_in", "w_ffn_out")
WEIGHTS = ("meta_tokens", "pre_mix_g", "w_in", "gate_b", "dw_w", "dw_b", "conv_ln_g", "conv_ln_b", "w_conv_out",
           "w_attn_out", "w_o", "post_mix_g", "pre_ffn_g", "w_ffn_in", "w_ffn_out", "post_ffn_g")


def kernel(x, meta_tokens, pre_mix_g, w_in, gate_b, dw_w, dw_b, conv_ln_g, conv_ln_b, w_conv_out, w_attn_out, w_o, post_mix_g, pre_ffn_g, w_ffn_in, w_ffn_out, post_ffn_g, loss_target, m_meta_tokens, m_pre_mix_g, m_w_in, m_gate_b, m_dw_w, m_dw_b, m_conv_ln_g, m_conv_ln_b, m_w_conv_out, m_w_attn_out, m_w_o, m_post_mix_g, m_pre_ffn_g, m_w_ffn_in, m_w_ffn_out, m_post_ffn_g, v_meta_tokens, v_pre_mix_g, v_w_in, v_gate_b, v_dw_w, v_dw_b, v_conv_ln_g, v_conv_ln_b, v_w_conv_out, v_w_attn_out, v_w_o, v_post_mix_g, v_pre_ffn_g, v_w_ffn_in, v_w_ffn_out, v_post_ffn_g):
    given = dict(locals())
    seq, d = x.shape[1], x.shape[2]
    n_meta = meta_tokens.shape[0]
    length = n_meta + seq
    m_rows = -(-length // QB) * QB
    dc = d // N_DEV
    assert dc == LANE and n_meta % 8 == 0 and seq % 8 == 0
    fs = w_ffn_in.shape[2]
    fr = w_ffn_out.shape[1]
    assert 2 * fr == fs

    transposed = ("w_ffn_in",)

    def shard(name):
        return given[name][0].T if name.endswith(transposed) else given[name][0]

    local = {k: shard(k) for k in SHARDED}
    cast = _cast_bf16([local[k] for k in SHARDED])
    dww_pad = jnp.pad(dw_w[0], ((0, CONV_PAD - CONV_WIDTH), (0, 0)))
    wi, meta_g, dww_g = _all_gather([cast[0], meta_tokens, dww_pad])
    meta_full = jnp.concatenate([meta_g[j] for j in range(N_DEV)], axis=1)
    dww_full = jnp.concatenate([dww_g[j] for j in range(N_DEV)], axis=1)
    ns = wi.shape[2]

    tail = jnp.zeros((m_rows - length, d), F32)
    h0 = jnp.concatenate([meta_full, x[0], tail], axis=0)
    target = jnp.concatenate([jnp.zeros((n_meta, d), F32), loss_target[0], tail], axis=0)

    (u,) = _rows("pre_mix_norm", lambda r0, xs, ps: ([_rms(xs[0], ps[0])], []), [h0], [pre_mix_g], [BF16], [])
    square = list(cast[1:4])
    p, p16, *gathered = _matmul(
        "in_proj", NN, u, wi, pl.BlockSpec((m_rows, d), lambda i: (0, 0)), pl.BlockSpec((None, d, ns), lambda i: (i, 0, 0)),
        pl.BlockSpec((m_rows, ns), lambda i: (0, i)), jax.ShapeDtypeStruct((m_rows, N_DEV * ns), F32), (N_DEV,), twin_bf16=True,
        carried=(_gather_copies, square, [jax.ShapeDtypeStruct((N_DEV,) + s.shape, s.dtype) for s in square],
                 _exchange_sems(len(square)) + [pltpu.SemaphoreType.DMA((len(square),))]))
    wco, wao, wo = (g.reshape(d, d) for g in gathered)
    o, tot, (wfi_t, wfo) = _attn_fwd(p16, d, list(cast[4:6]))
    wfo = wfo.reshape(N_DEV // 2, fs, d)
    y = _conv_fwd(p, dww_full, dw_b, d)
    (yc,) = _rows("conv_norm", lambda r0, xs, ps: ([_ln_silu(xs[0], ps[0], ps[1])], []), [y], [conv_ln_g, conv_ln_b], [BF16], [])
    y_conv = _dense_fwd("conv_out", yc, wco)
    y_attn = _dense_fwd("attn_out", o, wao)
    gate_cols = [(d, 5), (d, 6), None, None]
    (mixin,) = _rows("gate_mix", lambda r0, xs, ps: ([_gate_mix(*xs, ps[0])], []), [p, p, y_conv, y_attn], [gate_b], [BF16], [],
                     row_in_cols=gate_cols, row_out_widths=[d])
    mix = _dense_fwd("mix_out", mixin, wo)
    h1, u2 = _rows("post_mix", lambda r0, xs, ps: (list(_post_mix(xs[0], xs[1], ps[0], ps[1])), []), [h0, mix],
                   [post_mix_g, pre_ffn_g], [F32, BF16], [])
    half = N_DEV // 2
    a_act, b_act, f_in = _ffn_in_swiglu(u2, wfi_t)
    f = _matmul("ffn_out", NN, f_in, wfo, pl.BlockSpec((None, m_rows, fs), lambda j: (j, 0, 0)), pl.BlockSpec((None, fs, d), lambda j: (j, 0, 0)),
                pl.BlockSpec((m_rows, d), lambda j: (0, 0)), jax.ShapeDtypeStruct((m_rows, d), F32), (half,), acc_axis=0)

    def loss_head(r0, xs, ps):
        h1_, f_, t_ = xs
        r, vjp = jax.vjp(_rms, f_, ps[0])
        rows = r0 + lax.broadcasted_iota(I32, (h1_.shape[0], 1), 0)
        real = (rows >= n_meta) & (rows < length)
        err = jnp.where(real, h1_ + r - t_, 0.0)
        dh2 = err * (1.0 / d)
        d_f, dg = vjp(dh2)
        part = jnp.sum(0.5 * jnp.mean(err * err, axis=-1, keepdims=True), axis=0, keepdims=True)
        return [d_f, dh2], [dg, jnp.broadcast_to(part, (1, LANE))]

    d_f, dh2, g_post_ffn, loss_part = _rows("loss_head", loss_head, [h1, f, target], [post_ffn_g], [BF16, F32], [d, LANE])

    d_ab = _ffn_out_dx_swiglu(d_f, wfo, a_act, b_act).reshape(N_DEV, m_rows, fs)
    g_wfo = _matmul("ffn_out_dw", TN, f_in, d_f, pl.BlockSpec((None, m_rows, fs), lambda j: (j, 0, 0)), pl.BlockSpec((m_rows, d), lambda j: (0, 0)),
                    pl.BlockSpec((None, fs, d), lambda j: (j, 0, 0)), jax.ShapeDtypeStruct((half, fs, d), F32), (half,), twin_bf16=True)

    du2 = _matmul("ffn_in_dx", NN, d_ab, wfi_t, pl.BlockSpec((None, m_rows, fs), lambda i: (i, 0, 0)), pl.BlockSpec((None, fs, d), lambda i: (i, 0, 0)),
                  pl.BlockSpec((m_rows, d), lambda i: (0, 0)), jax.ShapeDtypeStruct((m_rows, d), F32), (N_DEV,), acc_axis=0)
    g_wfi = _matmul("ffn_in_dw", TN, d_ab, u2, pl.BlockSpec((None, m_rows, fs), lambda i: (i, 0, 0)), pl.BlockSpec((m_rows, d), lambda i: (0, 0)),
                    pl.BlockSpec((None, fs, d), lambda i: (i, 0, 0)), jax.ShapeDtypeStruct((N_DEV, fs, d), F32), (N_DEV,), twin_bf16=True)

    def post_mix_bwd(r0, xs, ps):
        h0_, mix_, dh2_, du2_ = xs
        _, vjp = jax.vjp(_post_mix, h0_, mix_, ps[0], ps[1])
        dh0_, dmix_, dg1, dg2 = vjp((dh2_, du2_))
        return [dmix_, dh0_], [dg1, dg2]

    d_mix, dh1, g_post_mix, g_pre_ffn = _rows("post_mix_bwd", post_mix_bwd, [h0, mix, dh2, du2], [post_mix_g, pre_ffn_g],
                                              [BF16, F32], [d, d])
    d_mixin = _dense_dx("mix_out_dx", d_mix, wo, F32)
    g_wo = _dense_dw("mix_out_dw", mixin, d_mix)

    def gate_mix_bwd(r0, xs, ps):
        _, vjp = jax.vjp(_gate_mix, xs[0], xs[1], xs[2], xs[3], ps[0])
        dpgc, dpga, dyc_, dya_, dgb = vjp(xs[4])
        return [dpgc, dpga, dyc_, dya_], [dgb]

    dp_gc, dp_ga, d_yconv, d_yattn, g_gate_b = _rows(
        "gate_mix_bwd", gate_mix_bwd, [p, p, y_conv, y_attn, d_mixin], [gate_b], [BF16] * 4, [2 * d],
        row_in_cols=gate_cols + [None], row_out_widths=[d] * 4)
    d_o = _dense_dx("attn_out_dx", d_yattn, wao, BF16)
    g_wao = _dense_dw("attn_out_dw", o, d_yattn)
    d_yc = _dense_dx("conv_out_dx", d_yconv, wco, F32)
    g_wco = _dense_dw("conv_out_dw", yc, d_yconv)
    big = {"w_ffn_out": [g.reshape(N_DEV, fr, d) for g in g_wfo], "w_ffn_in": g_wfi,
           "w_o": [g.reshape(N_DEV, dc, d) for g in g_wo], "w_attn_out": [g.reshape(N_DEV, dc, d) for g in g_wao],
           "w_conv_out": [g.reshape(N_DEV, dc, d) for g in g_wco]}
    early = ("w_ffn_out", "w_ffn_in", "w_o", "w_attn_out", "w_conv_out")
    dq, dk, dv, received_early = _attn_bwd(p16, d_o, tot, d, [big[k][1] for k in early])

    def conv_norm_bwd(r0, xs, ps):
        _, vjp = jax.vjp(_ln_silu, xs[0], ps[0], ps[1])
        dy_, dg, db = vjp(xs[1])
        return [dy_], [dg, db]

    d_y, g_ln_g, g_ln_b = _rows("conv_norm_bwd", conv_norm_bwd, [y, d_yc], [conv_ln_g, conv_ln_b], [F32], [d, d])
    dp_a, dp_g, g_dww, g_dwb = _conv_bwd(p, d_y, dww_full, d)
    dp = jnp.concatenate([dp_a, dp_g, dq, dk, dv, dp_gc, dp_ga], axis=1)
    g_wi = _matmul("in_proj_dw", TN, u, dp, pl.BlockSpec((m_rows, d), lambda i: (0, 0)), pl.BlockSpec((m_rows, ns), lambda i: (0, i)),
                   pl.BlockSpec((None, d, ns), lambda i: (i, 0, 0)), jax.ShapeDtypeStruct((N_DEV, d, ns), F32), (N_DEV,), twin_bf16=True)
    x_i, y_i, c_i = _mesh_pos()
    slab_idx = jnp.stack([_dev(*chip, c_i) for chip in _other_chips(x_i, y_i)]).astype(I32)
    (wi_sibling,) = _sibling_exchange([g_wi[1]])
    wi_pairs = _pair_sum(slab_idx, g_wi[0], wi_sibling)
    du, wi_chips = _matmul(
        "in_proj_dx", NT, dp, wi, pl.BlockSpec((m_rows, ns), lambda i: (0, i)), pl.BlockSpec((None, d, ns), lambda i: (i, 0, 0)),
        pl.BlockSpec((m_rows, d), lambda i: (0, 0)), jax.ShapeDtypeStruct((m_rows, d), F32), (N_DEV,), acc_axis=0,
        carried=(_chip_copies, [wi_pairs], [jax.ShapeDtypeStruct(wi_pairs.shape, BF16)],
                 [pltpu.SemaphoreType.DMA((1, 3)), pltpu.SemaphoreType.DMA((1, 3))]))

    def pre_mix_bwd(r0, xs, ps):
        _, vjp = jax.vjp(_rms, xs[0], ps[0])
        dx, dg = vjp(xs[1])
        return [xs[2] + dx], [dg]

    dh0, g_pre_mix = _rows("pre_mix_bwd", pre_mix_bwd, [h0, du, dh1], [pre_mix_g], [F32], [d])
    grad_x = dh0[n_meta:length][None]

    me = _dev(x_i, y_i, c_i)
    me_arr = jnp.reshape(me, (1,)).astype(I32)
    big["w_in"] = g_wi
    received = {k: [r] for k, r in zip(early, received_early)}
    received["w_in"] = [wi_sibling[3:4], wi_chips]
    results = {}
    for k in SHARDED:
        outs = _adamw_shard(me_arr, big[k][0], received[k], local[k], shard("m_" + k), shard("v_" + k))
        results[k] = tuple((a.T if k in transposed else a)[None] for a in outs)

    rep_grads = {"pre_mix_g": g_pre_mix, "gate_b": g_gate_b, "dw_b": g_dwb, "conv_ln_g": g_ln_g, "conv_ln_b": g_ln_b,
                 "post_mix_g": g_post_mix, "pre_ffn_g": g_pre_ffn, "post_ffn_g": g_post_ffn}

    def pack_rep(get):
        return jnp.concatenate([get(k) for k in REPLICATED], axis=1).reshape(-1, LANE)

    rep_rows = pack_rep(lambda k: rep_grads[k])
    n_rep = rep_rows.shape[0]
    loss_rows = jnp.broadcast_to(loss_part, (8, LANE))
    g_meta = dh0[0:n_meta]
    slabs = jnp.stack([jnp.concatenate([rep_rows, loss_rows, g_dww[:, j * LANE:(j + 1) * LANE], g_meta[:, j * LANE:(j + 1) * LANE]], axis=0)
                       for j in range(N_DEV)])

    def pack_small(prefix):
        dww_own = jnp.pad(given[prefix + "dw_w"][0], ((0, CONV_PAD - CONV_WIDTH), (0, 0)))
        return jnp.concatenate([pack_rep(lambda k: given[prefix + k]), jnp.zeros((8, LANE), F32), dww_own,
                                given[prefix + "meta_tokens"]], axis=0)

    small = _small_reduce_adamw(slabs, pack_small(""), pack_small("m_"), pack_small("v_"))
    loss = small[0][n_rep, 0]

    def unpack(arr):
        out = {}
        flat = arr[:n_rep].reshape(1, -1)
        off = 0
        for k in REPLICATED:
            w = given[k].shape[1]
            out[k] = flat[:, off:off + w]
            off += w
        out["dw_w"] = arr[n_rep + 8:n_rep + 8 + CONV_WIDTH][None]
        out["meta_tokens"] = arr[n_rep + 8 + CONV_PAD:n_rep + 8 + CONV_PAD + n_meta]
        return out

    small_out = [unpack(a) for a in small]
    for k in WEIGHTS:
        if k not in results:
            results[k] = tuple(s[k] for s in small_out)
    return (loss, grad_x, *[results[k][0] for k in WEIGHTS], *[results[k][1] for k in WEIGHTS],
            *[results[k][2] for k in WEIGHTS], *[results[k][3] for k in WEIGHTS])
```

```python
import jax
import jax.numpy as jnp
from jax import lax
from jax.experimental import pallas as pl
from jax.experimental.pallas import tpu as pltpu

F32 = jnp.float32
BF16 = jnp.bfloat16
I32 = jnp.int32

N_DEV = 8
LANE = 128
HEAD_DIM = 64
QB = 128
KEY_SHIFT = 2
KEY_TILES = 1 << KEY_SHIFT
KEY_CHUNK = KEY_TILES * QB
LANE_BLOCKS = 4
BWD_LANE_BLOCKS = 4
CONV_WIDTH = 31
CONV_PAD = 32
ROW_CHUNK = 128
RMS_EPS = 1e-6
LN_EPS = 1e-5
ADAM_LR = 0.001
ADAM_B1 = 0.9
ADAM_B2 = 0.999
ADAM_EPS = 1e-08
ADAM_WD = 0.01
ADAM_STEP = 10
VMEM_LIMIT = 56 * 1024 * 1024

NN = (((1,), (0,)), ((), ()))
NT = (((1,), (1,)), ((), ()))
TN = (((0,), (0,)), ((), ()))
MESH = pl.DeviceIdType.MESH
ANY = pl.BlockSpec(memory_space=pl.ANY)
VMEM_WHOLE = pl.BlockSpec(memory_space=pltpu.VMEM)


def _params(n_axes):
    return pltpu.CompilerParams(dimension_semantics=("arbitrary",) * n_axes, vmem_limit_bytes=VMEM_LIMIT)


def _row_tile(m):
    assert m % QB == 0
    return m // 4 if m % 64 == 0 else QB


def _matmul(name, dims, a, b, a_spec, b_spec, o_spec, out_shape, grid, acc_axis=None, twin_bf16=False, carried=None):
    n_car = 0 if carried is None else len(carried[1])
    n_twin = 1 if twin_bf16 else 0

    def body(a_ref, b_ref, *rest):
        car_ins, o_ref, twin, rest = rest[:n_car], rest[n_car], rest[n_car + 1:n_car + 1 + n_twin], rest[n_car + 1 + n_twin:]
        if carried is not None:
            starts, relays, waits = carried[0](car_ins, rest[:n_car], *rest[n_car:])

            @pl.when(pl.program_id(0) == 0)
            def _():
                for f in starts:
                    f()

            if relays:
                @pl.when(pl.program_id(0) == (3 * grid[0]) // 4)
                def _():
                    for f in relays:
                        f()

        r = lax.dot_general(a_ref[...], b_ref[...], dims, preferred_element_type=F32)
        if acc_axis is None:
            o_ref[...] = r.astype(o_ref.dtype)
            for t_ref in twin:
                t_ref[...] = r.astype(BF16)
        else:
            k = pl.program_id(acc_axis)

            @pl.when(k == 0)
            def _():
                o_ref[...] = r

            @pl.when(k > 0)
            def _():
                o_ref[...] += r

        if carried is not None:
            @pl.when(pl.program_id(0) == grid[0] - 1)
            def _():
                for f in waits:
                    f()

    in_specs, out_specs, out_shapes, scratch = [a_spec, b_spec], [o_spec], [out_shape], []
    if twin_bf16:
        assert acc_axis is None
        out_specs.append(o_spec)
        out_shapes.append(jax.ShapeDtypeStruct(out_shape.shape, BF16))
    if carried is not None:
        assert len(grid) == 1
        in_specs += [ANY] * n_car
        out_specs += [ANY] * n_car
        out_shapes += list(carried[2])
        scratch = list(carried[3])
    outs = pl.pallas_call(body, name=name, grid=grid, in_specs=in_specs, out_specs=out_specs, out_shape=out_shapes,
                          scratch_shapes=scratch, compiler_params=_params(len(grid)))(a, b, *(carried[1] if carried else ()))
    return outs[0] if len(outs) == 1 else outs


def _ffn_rows(m):
    return m // 2 if m % 32 == 0 else m


def _ffn_in_swiglu(u2, wfi_t):
    m, d = u2.shape
    half, fs = N_DEV // 2, wfi_t.shape[1]
    tm = _ffn_rows(m)

    def body(u_ref, wa_ref, wb_ref, a_ref, b_ref, f_ref):
        u = u_ref[...]
        a = lax.dot_general(u, wa_ref[...], NT, preferred_element_type=F32)
        b = lax.dot_general(u, wb_ref[...], NT, preferred_element_type=F32)
        a_ref[...] = a.astype(BF16)
        b_ref[...] = b.astype(BF16)
        f_ref[...] = _swiglu(a, b).astype(BF16)

    out = jax.ShapeDtypeStruct((half, m, fs), BF16)
    act = pl.BlockSpec((None, tm, fs), lambda j, i: (j, i, 0))
    return pl.pallas_call(
        body, name="ffn_in_swiglu", grid=(half, m // tm),
        in_specs=[pl.BlockSpec((tm, d), lambda j, i: (i, 0)), pl.BlockSpec((None, fs, d), lambda j, i: (j, 0, 0)),
                  pl.BlockSpec((None, fs, d), lambda j, i: (j + half, 0, 0))],
        out_specs=[act, act, act], out_shape=[out, out, out], compiler_params=_params(2))(u2, wfi_t, wfi_t)


def _ffn_out_dx_swiglu(d_f, wfo, a_act, b_act):
    m, d = d_f.shape
    half, fs = N_DEV // 2, wfo.shape[1]
    tm = _ffn_rows(m)

    def body(df_ref, w_ref, a_ref, b_ref, o_ref):
        d_fin = lax.dot_general(df_ref[...], w_ref[...], NT, preferred_element_type=F32)
        _, vjp = jax.vjp(_swiglu, a_ref[...].astype(F32), b_ref[...].astype(F32))
        d_a, d_b = vjp(d_fin)
        o_ref[0] = d_a.astype(BF16)
        o_ref[1] = d_b.astype(BF16)

    act = pl.BlockSpec((None, tm, fs), lambda j, i: (j, i, 0))
    return pl.pallas_call(
        body, name="ffn_out_dx_swiglu", grid=(half, m // tm),
        in_specs=[pl.BlockSpec((tm, d), lambda j, i: (i, 0)), pl.BlockSpec((None, fs, d), lambda j, i: (j, 0, 0)), act, act],
        out_specs=pl.BlockSpec((2, None, tm, fs), lambda j, i: (0, j, i, 0)),
        out_shape=jax.ShapeDtypeStruct((2, half, m, fs), BF16), compiler_params=_params(2))(d_f, wfo, a_act, b_act)


DENSE_TILE = 256


def _dense_fwd(name, a, w, out_dtype=F32):
    m, k = a.shape
    n = w.shape[1]
    tn = DENSE_TILE
    return _matmul(name, NN, a, w, pl.BlockSpec((m, k), lambda j: (0, 0)), pl.BlockSpec((k, tn), lambda j: (0, j)),
                   pl.BlockSpec((m, tn), lambda j: (0, j)), jax.ShapeDtypeStruct((m, n), out_dtype), (n // tn,))


def _dense_dx(name, dy, w, out_dtype):
    m, n = dy.shape
    k = w.shape[0]
    tk = DENSE_TILE
    return _matmul(name, NT, dy, w, pl.BlockSpec((m, n), lambda j: (0, 0)), pl.BlockSpec((tk, n), lambda j: (j, 0)),
                   pl.BlockSpec((m, tk), lambda j: (0, j)), jax.ShapeDtypeStruct((m, k), out_dtype), (k // tk,))


def _dense_dw(name, a, dy):
    m, k = a.shape
    n = dy.shape[1]
    tn = DENSE_TILE
    return _matmul(name, TN, a, dy, pl.BlockSpec((m, k), lambda j: (0, 0)), pl.BlockSpec((m, tn), lambda j: (0, j)),
                   pl.BlockSpec((k, tn), lambda j: (0, j)), jax.ShapeDtypeStruct((k, n), F32), (n // tn,), twin_bf16=True)


def _gate_specs(m, d, n_row_ins):
    t = DENSE_TILE
    per = d // t
    cols = lambda base: pl.BlockSpec((m, t), lambda j, base=base: (0, base * per + j))
    bias = lambda base: pl.BlockSpec((1, t), lambda j, base=base: (0, base * per + j))
    return [cols(5), cols(6)] + [cols(0)] * n_row_ins + [bias(0), bias(1)]


def _attn_out_gate(o, wao, p, y_conv, gate_b):
    m, d = o.shape
    t = DENSE_TILE

    def body(o_ref, w_ref, pgc_ref, pga_ref, yc_ref, bc_ref, ba_ref, ya_ref, mix_ref):
        ya = jnp.dot(o_ref[...], w_ref[...], preferred_element_type=F32)
        ya_ref[...] = ya
        gb = jnp.concatenate([bc_ref[...], ba_ref[...]], axis=1)
        mix_ref[...] = _gate_mix(pgc_ref[...], pga_ref[...], yc_ref[...], ya, gb).astype(mix_ref.dtype)

    tile = pl.BlockSpec((m, t), lambda j: (0, j))
    return pl.pallas_call(
        body, name="attn_out_gate", grid=(d // t,),
        in_specs=[pl.BlockSpec((m, d), lambda j: (0, 0)), pl.BlockSpec((d, t), lambda j: (0, j))] + _gate_specs(m, d, 1),
        out_specs=[tile, tile], out_shape=[jax.ShapeDtypeStruct((m, d), F32), jax.ShapeDtypeStruct((m, d), BF16)],
        compiler_params=_params(1))(o, wao, p, p, y_conv, gate_b, gate_b)


def _mix_out_dx_gate(d_mix, wo, p, y_conv, y_attn, gate_b):
    m, d = d_mix.shape
    t = DENSE_TILE

    def body(dm_ref, w_ref, pgc_ref, pga_ref, yc_ref, ya_ref, bc_ref, ba_ref, dgc_ref, dga_ref, dyc_ref, dya_ref, dbc_ref, dba_ref):
        d_mixin = lax.dot_general(dm_ref[...], w_ref[...], NT, preferred_element_type=F32)
        gb = jnp.concatenate([bc_ref[...], ba_ref[...]], axis=1)
        _, vjp = jax.vjp(_gate_mix, pgc_ref[...], pga_ref[...], yc_ref[...], ya_ref[...], gb)
        dgc, dga, dyc, dya, dgb = vjp(d_mixin)
        dgc_ref[...] = dgc.astype(dgc_ref.dtype)
        dga_ref[...] = dga.astype(dga_ref.dtype)
        dyc_ref[...] = dyc.astype(dyc_ref.dtype)
        dya_ref[...] = dya.astype(dya_ref.dtype)
        dbc_ref[...] = dgb[:, :t]
        dba_ref[...] = dgb[:, t:]

    tile = pl.BlockSpec((m, t), lambda j: (0, j))
    vec = pl.BlockSpec((1, t), lambda j: (0, j))
    act = jax.ShapeDtypeStruct((m, d), BF16)
    par = jax.ShapeDtypeStruct((1, d), F32)
    return pl.pallas_call(
        body, name="mix_out_dx_gate", grid=(d // t,),
        in_specs=[pl.BlockSpec((m, d), lambda j: (0, 0)), pl.BlockSpec((t, d), lambda j: (j, 0))] + _gate_specs(m, d, 2),
        out_specs=[tile] * 4 + [vec, vec], out_shape=[act] * 4 + [par, par],
        compiler_params=_params(1))(d_mix, wo, p, p, y_conv, y_attn, gate_b, gate_b)


def _rowwise(name, fn, row_ins, par_ins, row_outs, par_outs, *, grid, in_specs, out_specs, tm, row_axis):
    n_ri, n_pi, n_ro, n_po = len(row_ins), len(par_ins), len(row_outs), len(par_outs)
    n_steps, tail = divmod(tm, ROW_CHUNK)
    assert tail % 16 == 0

    def body(*refs):
        ri = refs[:n_ri]
        pi = refs[n_ri:n_ri + n_pi]
        ro = refs[n_ri + n_pi:n_ri + n_pi + n_ro]
        po = refs[n_ri + n_pi + n_ro:]
        ps = [r[...] for r in pi]
        base = pl.program_id(row_axis) * tm

        def chunk(r0, rows, carry):
            xs = [r[pl.ds(r0, rows), :] for r in ri]
            outs, pouts = fn(base + r0, xs, ps)
            for r, o in zip(ro, outs):
                if isinstance(o, (list, tuple)):
                    for j, part in enumerate(o):
                        r[j, pl.ds(r0, rows), :] = part.astype(r.dtype)
                else:
                    r[pl.ds(r0, rows), :] = o.astype(r.dtype)
            return tuple(c + q for c, q in zip(carry, pouts))

        def step(i, carry):
            return chunk(pl.multiple_of(i * ROW_CHUNK, ROW_CHUNK), ROW_CHUNK, carry)

        acc = lax.fori_loop(0, n_steps, step, tuple(jnp.zeros(s.shape, F32) for s in par_outs))
        if tail:
            acc = chunk(n_steps * ROW_CHUNK, tail, acc)
        if n_po:
            first = pl.program_id(0) == 0
            for ax in range(1, len(grid)):
                first = first & (pl.program_id(ax) == 0)

            @pl.when(first)
            def _():
                for r in po:
                    r[...] = jnp.zeros_like(r)

            for r, a in zip(po, acc):
                r[...] += a

    return pl.pallas_call(body, name=name, grid=grid, in_specs=in_specs, out_specs=out_specs,
                          out_shape=tuple(row_outs) + tuple(par_outs),
                          compiler_params=_params(len(grid)))(*row_ins, *par_ins)


def _rows(name, fn, row_ins, par_ins, row_out_dtypes, par_out_widths, row_in_cols=None, row_out_widths=None):
    m = row_ins[0].shape[0]
    tm = _row_tile(m)
    in_specs = []
    for k, a in enumerate(row_ins):
        if row_in_cols is not None and row_in_cols[k] is not None:
            width, cb = row_in_cols[k]
            in_specs.append(pl.BlockSpec((tm, width), lambda i, cb=cb: (i, cb)))
        else:
            in_specs.append(pl.BlockSpec((tm, a.shape[1]), lambda i: (i, 0)))
    for a in par_ins:
        in_specs.append(pl.BlockSpec(a.shape, lambda i: (0, 0)))
    if row_out_widths is None:
        row_out_widths = [row_ins[0].shape[1]] * len(row_out_dtypes)
    row_outs = [jax.ShapeDtypeStruct((m, w), dt) for w, dt in zip(row_out_widths, row_out_dtypes)]
    par_outs = [jax.ShapeDtypeStruct((1, w), F32) for w in par_out_widths]
    out_specs = [pl.BlockSpec((tm, s.shape[1]), lambda i: (i, 0)) for s in row_outs]
    out_specs += [pl.BlockSpec(s.shape, lambda i: (0, 0)) for s in par_outs]
    return _rowwise(name, fn, row_ins, par_ins, row_outs, par_outs, grid=(m // tm,), in_specs=in_specs,
                    out_specs=out_specs, tm=tm, row_axis=0)


def _rms(x, g):
    return x * lax.rsqrt(jnp.mean(x * x, axis=-1, keepdims=True) + RMS_EPS) * g


def _ln_silu(y, g, b):
    mu = jnp.mean(y, axis=-1, keepdims=True)
    yc = y - mu
    var = jnp.mean(yc * yc, axis=-1, keepdims=True)
    return jax.nn.silu(yc * lax.rsqrt(var + LN_EPS) * g + b)


def _gate_mix(pgc, pga, yc, ya, gb):
    d = pgc.shape[1]
    return jax.nn.sigmoid(pgc + gb[:, :d]) * yc + jax.nn.sigmoid(pga + gb[:, d:]) * ya


def _post_mix(h0, mix, g_post, g_pre):
    h1 = h0 + _rms(mix, g_post)
    return h1, _rms(h1, g_pre)


def _swiglu(a, b):
    return jax.nn.silu(a) * b


def _conv_taps():
    taps = []
    for b in range(8):
        for a in range(CONV_PAD // 8):
            s = 8 * a + b
            if s < CONV_WIDTH:
                taps.append((b, a, CONV_WIDTH - 1 - s))
    return taps


def _conv_fwd(p, dww, dwb, d_model):
    m = p.shape[0]
    nch = d_model // LANE
    n_chunk = m // QB
    taps = _conv_taps()

    def body(a_ref, g_ref, w_ref, b_ref, y_ref, upad):
        upad[0:CONV_PAD, :] = jnp.zeros((CONV_PAD, LANE), F32)

        def fill(i, c):
            r0 = pl.multiple_of(i * QB, QB)
            u = a_ref[pl.ds(r0, QB), :] * jax.nn.sigmoid(g_ref[pl.ds(r0, QB), :])
            upad[pl.ds(pl.multiple_of(r0 + CONV_PAD, 8), QB), :] = u
            return c

        lax.fori_loop(0, n_chunk, fill, 0)

        def conv(i, c):
            r0 = pl.multiple_of(i * QB, QB)
            win = upad[pl.ds(r0, QB + CONV_PAD), :]
            acc = jnp.broadcast_to(b_ref[...], (QB, LANE))
            rolled = {}
            for b, a, j in taps:
                if b not in rolled:
                    rolled[b] = win if b == 0 else pltpu.roll(win, b, axis=0)
                lo = CONV_PAD - 8 * a
                acc = acc + w_ref[j:j + 1, :] * rolled[b][lo:lo + QB, :]
            y_ref[pl.ds(r0, QB), :] = acc
            return c

        lax.fori_loop(0, n_chunk, conv, 0)

    col = lambda off: pl.BlockSpec((m, LANE), lambda c: (0, off + c))
    return pl.pallas_call(
        body, name="conv_fwd", grid=(nch,),
        in_specs=[col(0), col(nch), pl.BlockSpec((CONV_PAD, LANE), lambda c: (0, c)), pl.BlockSpec((1, LANE), lambda c: (0, c))],
        out_specs=col(0), out_shape=jax.ShapeDtypeStruct((m, d_model), F32),
        scratch_shapes=[pltpu.VMEM((m + CONV_PAD, LANE), F32)], compiler_params=_params(1))(p, p, dww, dwb)


def _conv_bwd(p, dy, dww, d_model):
    m = p.shape[0]
    nch = d_model // LANE
    n_chunk = m // QB
    taps = _conv_taps()
    win_rows = QB + CONV_PAD

    def body(a_ref, g_ref, dy_ref, w_ref, da_ref, dg_ref, dw_ref, db_ref, upad, dypad, wacc, bacc):
        upad[0:CONV_PAD, :] = jnp.zeros((CONV_PAD, LANE), F32)
        dypad[m:m + CONV_PAD, :] = jnp.zeros((CONV_PAD, LANE), F32)
        wacc[...] = jnp.zeros_like(wacc)
        bacc[...] = jnp.zeros_like(bacc)

        def fill(i, c):
            r0 = pl.multiple_of(i * QB, QB)
            u = a_ref[pl.ds(r0, QB), :] * jax.nn.sigmoid(g_ref[pl.ds(r0, QB), :])
            upad[pl.ds(pl.multiple_of(r0 + CONV_PAD, 8), QB), :] = u
            dypad[pl.ds(r0, QB), :] = dy_ref[pl.ds(r0, QB), :]
            return c

        lax.fori_loop(0, n_chunk, fill, 0)

        def chunk(i, c):
            r0 = pl.multiple_of(i * QB, QB)
            dwin = dypad[pl.ds(r0, win_rows), :]
            du = jnp.zeros((QB, LANE), F32)
            rolled = {}
            for b, a, j in taps:
                if b not in rolled:
                    rolled[b] = dwin if b == 0 else pltpu.roll(dwin, win_rows - b, axis=0)
                du = du + w_ref[j:j + 1, :] * rolled[b][8 * a:8 * a + QB, :]
            av = a_ref[pl.ds(r0, QB), :]
            sg = jax.nn.sigmoid(g_ref[pl.ds(r0, QB), :])
            da_ref[pl.ds(r0, QB), :] = (du * sg).astype(da_ref.dtype)
            dg_ref[pl.ds(r0, QB), :] = (du * av * sg * (1.0 - sg)).astype(dg_ref.dtype)
            dyc = dy_ref[pl.ds(r0, QB), :]
            uwin = upad[pl.ds(r0, win_rows), :]
            rolled = {}
            for b, a, j in taps:
                if b not in rolled:
                    rolled[b] = uwin if b == 0 else pltpu.roll(uwin, b, axis=0)
                lo = CONV_PAD - 8 * a
                prod = dyc * rolled[b][lo:lo + QB, :]
                wacc[j] += prod.reshape(QB // 8, 8, LANE).sum(axis=0)
            bacc[...] += dyc.reshape(QB // 8, 8, LANE).sum(axis=0)
            return c

        lax.fori_loop(0, n_chunk, chunk, 0)
        for j in range(CONV_WIDTH):
            dw_ref[j:j + 1, :] = jnp.sum(wacc[j], axis=0, keepdims=True)
        dw_ref[CONV_WIDTH:CONV_PAD, :] = jnp.zeros((CONV_PAD - CONV_WIDTH, LANE), F32)
        db_ref[...] = jnp.sum(bacc[...], axis=0, keepdims=True)

    col = lambda off: pl.BlockSpec((m, LANE), lambda c: (0, off + c))
    return pl.pallas_call(
        body, name="conv_bwd", grid=(nch,),
        in_specs=[col(0), col(nch), col(0), pl.BlockSpec((CONV_PAD, LANE), lambda c: (0, c))],
        out_specs=[col(0), col(0), pl.BlockSpec((CONV_PAD, LANE), lambda c: (0, c)), pl.BlockSpec((1, LANE), lambda c: (0, c))],
        out_shape=(jax.ShapeDtypeStruct((m, d_model), BF16), jax.ShapeDtypeStruct((m, d_model), BF16),
                   jax.ShapeDtypeStruct((CONV_PAD, d_model), F32), jax.ShapeDtypeStruct((1, d_model), F32)),
        scratch_shapes=[pltpu.VMEM((m + CONV_PAD, LANE), F32), pltpu.VMEM((m + CONV_PAD, LANE), F32),
                        pltpu.VMEM((CONV_PAD, 8, LANE), F32), pltpu.VMEM((8, LANE), F32)],
        compiler_params=_params(1))(p, p, dy, dww)


EXP_CLAMP = 80.0


def _one_plus_exp(z):
    return 1.0 + jnp.exp(jnp.minimum(z, EXP_CLAMP))


def _softplus(z):
    return jnp.maximum(jnp.log(_one_plus_exp(z)), z)


def _softplus_sigmoid(z):
    s = _one_plus_exp(z)
    return jnp.maximum(jnp.log(s), z), 1.0 - 1.0 / s


def _tile_cumsums(x, tri2):
    xb = x.astype(BF16)
    out = []
    for i in range(0, x.shape[1] // QB, 2):
        both = jnp.dot(xb[:, i * QB:(i + 2) * QB], tri2, preferred_element_type=F32)
        out += [both[:, :QB], both[:, QB:]]
    return out


def _tri2(kind):
    jj = lax.broadcasted_iota(I32, (2 * QB, 2 * QB), 0)
    ss = lax.broadcasted_iota(I32, (2 * QB, 2 * QB), 1)
    same = (jj >= QB) == (ss >= QB)
    keep = {"ge": jj >= ss, "le": jj <= ss}[kind]
    return jnp.where(same & keep, 1.0, 0.0).astype(BF16)


def _attn_fwd(p, d_model, shards):
    m = p.shape[0]
    nqb = m // QB
    ngrp = d_model // (LANE_BLOCKS * LANE)
    qo, ko, vo = 2 * ngrp, 3 * ngrp, 4 * ngrp
    scale = HEAD_DIM ** -0.5
    n_sh = len(shards)

    assert nqb >= KEY_TILES

    def body(q_ref, k_ref, v_ref, *rest):
        shard_refs, (o_ref, t_ref), rest = rest[:n_sh], rest[n_sh:n_sh + 2], rest[n_sh + 2:]
        gathered_refs, (acc_ref, car_ref), sems = rest[:n_sh], rest[n_sh:n_sh + 2], rest[n_sh + 2:]
        starts, relays, waits = _gather_copies(shard_refs, gathered_refs, *sems)

        @pl.when((pl.program_id(0) == 0) & (pl.program_id(1) == 0))
        def _():
            for f in starts:
                f()

        @pl.when((pl.program_id(0) == ngrp - 1) & (pl.program_id(1) == (5 * nqb) // 8))
        def _():
            for f in relays:
                f()

        qb = pl.program_id(1)
        lane = lax.broadcasted_iota(I32, (QB, LANE), 1)
        head0 = lane < HEAD_DIM
        row_g = qb * QB + lax.broadcasted_iota(I32, (QB, KEY_CHUNK), 0)
        col_l = lax.broadcasted_iota(I32, (QB, KEY_CHUNK), 1)
        tri = _tri2("ge")
        heads = range(2 * LANE_BLOCKS)
        qh = []
        for lb in range(LANE_BLOCKS):
            q2 = (q_ref[:, lb * LANE:(lb + 1) * LANE] * scale).astype(BF16)
            zero = jnp.zeros_like(q2)
            qh += [jnp.where(head0, q2, zero), jnp.where(head0, zero, q2)]
        acc_ref[...] = jnp.zeros_like(acc_ref)
        car_ref[...] = jnp.zeros_like(car_ref)

        def chunk(first_tile, bound, n_tiles=KEY_TILES):
            r0 = pl.multiple_of(first_tile * QB, QB)
            keys = n_tiles * QB
            kcs = [k_ref[pl.ds(r0, keys), lb * LANE:(lb + 1) * LANE].astype(BF16) for lb in range(LANE_BLOCKS)]
            vcs = [v_ref[pl.ds(r0, keys), lb * LANE:(lb + 1) * LANE].astype(BF16) for lb in range(LANE_BLOCKS)]
            valid = None if bound is None else (col_l[:, :keys] + r0) < bound
            zs = [lax.dot_general(qh[h], kcs[h // 2], NT, preferred_element_type=F32) for h in heads]
            sps = [_softplus(z) for z in zs]
            if valid is not None:
                sps = [jnp.where(valid, sp, 0.0) for sp in sps]
            cums = [_tile_cumsums(sp, tri) for sp in sps]
            cars = [car_ref[h] for h in heads]
            a_tiles = [[None] * n_tiles for h in heads]
            for i in reversed(range(n_tiles)):
                for h in heads:
                    cum = cums[h][i]
                    a_tiles[h][i] = jnp.exp(zs[h][:, i * QB:(i + 1) * QB] - (cum + cars[h]))
                    cars[h] = cars[h] + jnp.broadcast_to(cum[:, 0:1], cum.shape)
            for h in heads:
                a = jnp.concatenate(a_tiles[h], axis=1)
                if valid is not None:
                    a = jnp.where(valid, a, 0.0)
                acc_ref[h] += jnp.dot(a.astype(BF16), vcs[h // 2], preferred_element_type=F32)
                car_ref[h] = cars[h]

        near = jnp.maximum(qb - (KEY_TILES - 1), 0)
        chunk(near, row_g)
        n_full = lax.shift_right_logical(near, KEY_SHIFT)

        def step(i, c):
            chunk(near - KEY_TILES * (i + 1), None)
            return c

        lax.fori_loop(0, n_full, step, 0)
        left = near - KEY_TILES * n_full

        @pl.when((left > 0) & (left <= 2))
        def _():
            chunk(0, left * QB, n_tiles=2)

        @pl.when(left > 2)
        def _():
            chunk(0, left * QB)

        for lb in range(LANE_BLOCKS):
            o_ref[:, lb * LANE:(lb + 1) * LANE] = jnp.where(head0, acc_ref[2 * lb], acc_ref[2 * lb + 1]).astype(o_ref.dtype)
        for h in heads:
            t_ref[:, h * QB:(h + 1) * QB] = car_ref[h]

        @pl.when((pl.program_id(0) == ngrp - 1) & (pl.program_id(1) == nqb - 1))
        def _():
            for f in waits:
                f()

    wide = LANE_BLOCKS * LANE
    outs = pl.pallas_call(
        body, name="attn_fwd", grid=(ngrp, nqb),
        in_specs=[pl.BlockSpec((QB, wide), lambda g, qb: (qb, qo + g)),
                  pl.BlockSpec((m, wide), lambda g, qb: (0, ko + g)),
                  pl.BlockSpec((m, wide), lambda g, qb: (0, vo + g))] + [ANY] * n_sh,
        out_specs=[pl.BlockSpec((QB, wide), lambda g, qb: (qb, g)),
                   pl.BlockSpec((QB, 2 * wide), lambda g, qb: (qb, g))] + [ANY] * n_sh,
        out_shape=[jax.ShapeDtypeStruct((m, d_model), BF16), jax.ShapeDtypeStruct((m, 2 * d_model), F32)]
        + [jax.ShapeDtypeStruct((N_DEV,) + s.shape, s.dtype) for s in shards],
        scratch_shapes=[pltpu.VMEM((2 * LANE_BLOCKS, QB, LANE), F32), pltpu.VMEM((2 * LANE_BLOCKS, QB, LANE), F32)]
        + _exchange_sems(n_sh) + [pltpu.SemaphoreType.DMA((n_sh,))],
        compiler_params=_params(2))(p, p, p, *shards)
    return outs[0], outs[1], outs[2:]


def _attn_bwd(p, d_o, tot, d_model, slabs):
    m = p.shape[0]
    nqb = m // QB
    blocks = BWD_LANE_BLOCKS
    ngrp = d_model // (blocks * LANE)
    qo, ko, vo = 2 * ngrp, 3 * ngrp, 4 * ngrp
    scale = HEAD_DIM ** -0.5
    n_sl = len(slabs)

    assert nqb >= KEY_TILES

    def body(q_ref, k_ref, v_ref, do_ref, t_ref, *rest):
        slab_refs, (dq_ref, dk_ref, dv_ref), rest = rest[:n_sl], rest[n_sl:n_sl + 3], rest[n_sl + 3:]
        recv_refs, (dkacc, dvacc, dqacc, csp, cg), sems = rest[:n_sl], rest[n_sl:n_sl + 5], rest[n_sl + 5:]
        starts, waits = _scatter_copies(slab_refs, recv_refs, *sems)

        @pl.when((pl.program_id(0) == 0) & (pl.program_id(1) == 0))
        def _():
            for f in starts:
                f()

        qb = pl.program_id(1)
        lane = lax.broadcasted_iota(I32, (QB, LANE), 1)
        head0 = lane < HEAD_DIM
        row_g = qb * QB + lax.broadcasted_iota(I32, (QB, KEY_CHUNK), 0)
        col_l = lax.broadcasted_iota(I32, (QB, KEY_CHUNK), 1)
        tri_ge = _tri2("ge")
        tri_le = _tri2("le")
        heads = range(2 * blocks)
        qh, doh = [], []
        for lb in range(blocks):
            q2 = (q_ref[:, lb * LANE:(lb + 1) * LANE] * scale).astype(BF16)
            do2 = do_ref[:, lb * LANE:(lb + 1) * LANE]
            zero = jnp.zeros_like(q2)
            qh += [jnp.where(head0, q2, zero), jnp.where(head0, zero, q2)]
            doh += [jnp.where(head0, do2, zero), jnp.where(head0, zero, do2)]
        q_pairs = [jnp.concatenate(qh[2 * lb:2 * lb + 2], axis=0) for lb in range(blocks)]
        do_pairs = [jnp.concatenate(doh[2 * lb:2 * lb + 2], axis=0) for lb in range(blocks)]

        @pl.when(qb == 0)
        def _():
            dkacc[...] = jnp.zeros_like(dkacc)
            dvacc[...] = jnp.zeros_like(dvacc)

        dqacc[...] = jnp.zeros_like(dqacc)
        for h in heads:
            csp[h] = t_ref[:, h * QB:(h + 1) * QB]
        cg[...] = jnp.zeros_like(cg)

        def chunk(first_tile, bound, n_tiles=KEY_TILES):
            r0 = pl.multiple_of(first_tile * QB, QB)
            keys = n_tiles * QB
            kcs = [k_ref[pl.ds(r0, keys), lb * LANE:(lb + 1) * LANE].astype(BF16) for lb in range(blocks)]
            vcs = [v_ref[pl.ds(r0, keys), lb * LANE:(lb + 1) * LANE].astype(BF16) for lb in range(blocks)]
            valid = None if bound is None else (col_l[:, :keys] + r0) < bound
            tiles = [slice(i * QB, (i + 1) * QB) for i in range(n_tiles)]
            zs = [lax.dot_general(qh[h], kcs[h // 2], NT, preferred_element_type=F32) for h in heads]
            das = [lax.dot_general(doh[h], vcs[h // 2], NT, preferred_element_type=F32) for h in heads]
            sps, sgs = zip(*[_softplus_sigmoid(z) for z in zs])
            if valid is not None:
                sps = [jnp.where(valid, sp, 0.0) for sp in sps]
            cums = [_tile_cumsums(sp, tri_ge) for sp in sps]
            a_tiles, g_tiles = [[] for h in heads], [[] for h in heads]
            for h in heads:
                rest = csp[h]
                for i, c in enumerate(tiles):
                    cum = cums[h][i]
                    rest = rest - jnp.broadcast_to(cum[:, 0:1], cum.shape)
                    a = jnp.exp(zs[h][:, c] - (cum + rest))
                    if valid is not None:
                        a = jnp.where(valid[:, c], a, 0.0)
                    a_tiles[h].append(a)
                    g_tiles[h].append(a * das[h][:, c])
                csp[h] = rest
            gcums = [_tile_cumsums(jnp.concatenate(g_tiles[h], axis=1), tri_le) for h in heads]
            dzbs, abs_ = [], []
            for h in heads:
                g_before = cg[h]
                dz_tiles = []
                for i, c in enumerate(tiles):
                    gcum = gcums[h][i]
                    dz = g_tiles[h][i] - sgs[h][:, c] * (g_before + gcum)
                    if valid is not None:
                        dz = jnp.where(valid[:, c], dz, 0.0)
                    dz_tiles.append(dz)
                    g_before = g_before + jnp.broadcast_to(gcum[:, QB - 1:QB], gcum.shape)
                cg[h] = g_before
                dzbs.append(jnp.concatenate(dz_tiles, axis=1).astype(BF16))
                abs_.append(jnp.concatenate(a_tiles[h], axis=1).astype(BF16))
            for h in heads:
                dqacc[h] += jnp.dot(dzbs[h], kcs[h // 2], preferred_element_type=F32)
            for lb in range(blocks):
                dz_pair = jnp.concatenate(dzbs[2 * lb:2 * lb + 2], axis=0)
                a_pair = jnp.concatenate(abs_[2 * lb:2 * lb + 2], axis=0)
                dkacc[pl.ds(r0, keys), lb * LANE:(lb + 1) * LANE] += lax.dot_general(
                    dz_pair, q_pairs[lb], TN, preferred_element_type=F32)
                dvacc[pl.ds(r0, keys), lb * LANE:(lb + 1) * LANE] += lax.dot_general(
                    a_pair, do_pairs[lb], TN, preferred_element_type=F32)

        near = jnp.maximum(qb - (KEY_TILES - 1), 0)
        n_full = lax.shift_right_logical(near, KEY_SHIFT)

        def step(i, c):
            chunk(KEY_TILES * i, None)
            return c

        lax.fori_loop(0, n_full, step, 0)
        left = near - KEY_TILES * n_full

        @pl.when((left > 0) & (left <= 2))
        def _():
            chunk(KEY_TILES * n_full, near * QB, n_tiles=2)

        @pl.when(left > 2)
        def _():
            chunk(KEY_TILES * n_full, near * QB)

        chunk(near, row_g)
        for lb in range(blocks):
            dq2 = jnp.where(head0, dqacc[2 * lb], dqacc[2 * lb + 1]) * scale
            dq_ref[:, lb * LANE:(lb + 1) * LANE] = dq2.astype(dq_ref.dtype)

        @pl.when(qb == nqb - 1)
        def _():
            dk_ref[...] = dkacc[...].astype(dk_ref.dtype)
            dv_ref[...] = dvacc[...].astype(dv_ref.dtype)

        @pl.when((pl.program_id(0) == ngrp - 1) & (qb == nqb - 1))
        def _():
            for f in waits:
                f()

    out = jax.ShapeDtypeStruct((m, d_model), BF16)
    wide = blocks * LANE
    carry = pltpu.VMEM((2 * blocks, QB, LANE), F32)
    outs = pl.pallas_call(
        body, name="attn_bwd", grid=(ngrp, nqb),
        in_specs=[pl.BlockSpec((QB, wide), lambda g, qb: (qb, qo + g)),
                  pl.BlockSpec((m, wide), lambda g, qb: (0, ko + g)),
                  pl.BlockSpec((m, wide), lambda g, qb: (0, vo + g)),
                  pl.BlockSpec((QB, wide), lambda g, qb: (qb, g)),
                  pl.BlockSpec((QB, 2 * wide), lambda g, qb: (qb, g))] + [ANY] * n_sl,
        out_specs=[pl.BlockSpec((QB, wide), lambda g, qb: (qb, g)),
                   pl.BlockSpec((m, wide), lambda g, qb: (0, g)),
                   pl.BlockSpec((m, wide), lambda g, qb: (0, g))] + [ANY] * n_sl,
        out_shape=[out, out, out] + _received_shapes(slabs),
        scratch_shapes=[pltpu.VMEM((m, wide), F32), pltpu.VMEM((m, wide), F32), carry, carry, carry] + _exchange_sems(n_sl),
        compiler_params=_params(2))(p, p, p, d_o, tot, *slabs)
    return outs[0], outs[1], outs[2], outs[3:]


def _mesh_pos():
    return lax.axis_index("x"), lax.axis_index("y"), lax.axis_index("c")


def _other_chips(x, y):
    return [(1 - x, y), (x, 1 - y), (1 - x, 1 - y)]


def _dev(x, y, c):
    return 4 * x + 2 * y + c


def _all_gather(shards):
    n = len(shards)

    def body(*refs):
        starts, relays, waits = _gather_copies(refs[:n], refs[n:2 * n], *refs[2 * n:])
        for f in starts + relays + waits:
            f()

    return pl.pallas_call(
        body, name="comm_all_gather", in_specs=[ANY] * n, out_specs=[ANY] * n,
        out_shape=[jax.ShapeDtypeStruct((N_DEV,) + s.shape, s.dtype) for s in shards],
        scratch_shapes=[pltpu.SemaphoreType.DMA((n, 7)), pltpu.SemaphoreType.DMA((n, 7)), pltpu.SemaphoreType.DMA((n,))],
    )(*shards)


def _peers(x, y, c):
    out = []
    for mask in range(1, N_DEV):
        px, py, pc = x ^ (mask >> 2), y ^ ((mask >> 1) & 1), c ^ (mask & 1)
        out.append((mask - 1, (px, py, pc), _dev(px, py, pc)))
    return out


def _remote(src, dst, send_sems, recv_sems, k, s, peer):
    return pltpu.make_async_remote_copy(src_ref=src, dst_ref=dst, send_sem=send_sems.at[k, s], recv_sem=recv_sems.at[k, s],
                                        device_id=peer, device_id_type=MESH)


def _gather_copies(ins, outs, send_sems, recv_sems, local_sems):
    x, y, c = _mesh_pos()
    sibling = (x, y, 1 - c)
    chips = _other_chips(x, y)
    starts, relays, waits = [], [], []
    for k in range(len(ins)):
        def slot(block, k=k):
            return outs[k].at[_dev(*block)]

        def copy(s, src, block, to, k=k):
            return _remote(src, slot(block), send_sems, recv_sems, k, s, to)

        own = pltpu.make_async_copy(ins[k], slot((x, y, c)), local_sems.at[k])
        to_sibling = copy(0, ins[k], (x, y, c), sibling)
        starts += [own.start, to_sibling.start]
        waits += [own.wait, to_sibling.wait_send, copy(0, ins[k], (x, y, 1 - c), sibling).wait_recv]
        for j, chip in enumerate(chips):
            out = copy(1 + j, ins[k], (x, y, c), (*chip, c))
            relay = copy(4 + j, slot((*chip, c)), (*chip, c), sibling)
            starts.append(out.start)
            relays += [copy(1 + j, ins[k], (*chip, c), sibling).wait_recv, relay.start]
            waits += [out.wait_send, relay.wait_send, copy(4 + j, ins[k], (*chip, 1 - c), sibling).wait_recv]
    return starts, relays, waits


def _scatter_copies(ins, outs, send_sems, recv_sems):
    x, y, c = _mesh_pos()
    starts, waits = [], []
    for k in range(len(ins)):
        for s, peer, idx in _peers(x, y, c):
            send = _remote(ins[k].at[idx], outs[k].at[s], send_sems, recv_sems, k, s, peer)
            starts.append(send.start)
            waits += [send.wait_recv, send.wait_send]
    return starts, waits


def _exchange_sems(n):
    return [pltpu.SemaphoreType.DMA((n, N_DEV - 1)), pltpu.SemaphoreType.DMA((n, N_DEV - 1))]


def _received_shapes(slabs):
    return [jax.ShapeDtypeStruct((N_DEV - 1,) + a.shape[1:], a.dtype) for a in slabs]


def _chips_and_own(x, y):
    return _other_chips(x, y) + [(x, y)]


def _sibling_exchange(slabs):
    n = len(slabs)

    def body(*refs):
        ins, outs, (send_sems, recv_sems) = refs[:n], refs[n:2 * n], refs[2 * n:]
        x, y, c = _mesh_pos()
        copies = [_remote(ins[k].at[_dev(*chip, 1 - c)], outs[k].at[r], send_sems, recv_sems, k, r, (x, y, 1 - c))
                  for k in range(n) for r, chip in enumerate(_chips_and_own(x, y))]
        for cp in copies:
            cp.start()
        for cp in copies:
            cp.wait_recv()
        for cp in copies:
            cp.wait_send()

    return pl.pallas_call(body, name="comm_rs_sibling", in_specs=[ANY] * n, out_specs=[ANY] * n,
                          out_shape=[jax.ShapeDtypeStruct((4,) + a.shape[1:], a.dtype) for a in slabs],
                          scratch_shapes=[pltpu.SemaphoreType.DMA((n, 4)), pltpu.SemaphoreType.DMA((n, 4))])(*slabs)


def _chip_copies(ins, outs, send_sems, recv_sems):
    x, y, c = _mesh_pos()
    starts, waits = [], []
    for k in range(len(ins)):
        for r, chip in enumerate(_other_chips(x, y)):
            cp = _remote(ins[k].at[r], outs[k].at[r], send_sems, recv_sems, k, r, (*chip, c))
            starts.append(cp.start)
            waits += [cp.wait_recv, cp.wait_send]
    return starts, [], waits


def _pair_sum(slab_idx, grad, from_sibling):
    _, rows, cols = grad.shape
    tr = _shard_tile(rows)

    def body(idx_ref, g_ref, s_ref, o_ref):
        o_ref[...] = (g_ref[...] + s_ref[...].astype(F32)).astype(o_ref.dtype)

    gs = pltpu.PrefetchScalarGridSpec(
        num_scalar_prefetch=1, grid=(3, rows // tr),
        in_specs=[pl.BlockSpec((None, tr, cols), lambda r, i, idx: (idx[r], i, 0)),
                  pl.BlockSpec((None, tr, cols), lambda r, i, idx: (r, i, 0))],
        out_specs=pl.BlockSpec((None, tr, cols), lambda r, i, idx: (r, i, 0)))
    return pl.pallas_call(body, name="rs_pair_sum", grid_spec=gs, out_shape=jax.ShapeDtypeStruct((3, rows, cols), BF16),
                          compiler_params=_params(2))(slab_idx, grad, from_sibling)


def _shard_tile(rows):
    for tr in range(min(rows, 352), 0, -1):
        if rows % tr == 0 and (tr % 16 == 0 or tr == rows):
            return tr


def _adamw_math(w, g, m, v):
    m = ADAM_B1 * m + (1.0 - ADAM_B1) * g
    v = ADAM_B2 * v + (1.0 - ADAM_B2) * (g * g)
    m_hat = m / (1.0 - ADAM_B1 ** ADAM_STEP)
    v_hat = v / (1.0 - ADAM_B2 ** ADAM_STEP)
    delta = -ADAM_LR * (m_hat / (jnp.sqrt(v_hat) + ADAM_EPS) + ADAM_WD * w)
    return delta, m, v


def _adamw_shard(me, grad, received, w, m, v):
    rows, cols = w.shape
    tr = _shard_tile(rows)
    n_rec = len(received)

    def body(me_ref, g_ref, *rest):
        r_refs, (w_ref, m_ref, v_ref, go_ref, do_ref, mo_ref, vo_ref) = rest[:n_rec], rest[n_rec:]
        g = g_ref[...]
        for r_ref in r_refs:
            for s in range(r_ref.shape[0]):
                g = g + r_ref[s].astype(F32)
        delta, m_new, v_new = _adamw_math(w_ref[...], g, m_ref[...], v_ref[...])
        go_ref[...] = g
        do_ref[...] = delta
        mo_ref[...] = m_new
        vo_ref[...] = v_new

    flat = pl.BlockSpec((tr, cols), lambda i, me: (i, 0))
    gs = pltpu.PrefetchScalarGridSpec(
        num_scalar_prefetch=1, grid=(rows // tr,),
        in_specs=[pl.BlockSpec((None, tr, cols), lambda i, me: (me[0], i, 0))]
        + [pl.BlockSpec((r.shape[0], tr, cols), lambda i, me: (0, i, 0)) for r in received] + [flat, flat, flat],
        out_specs=[flat, flat, flat, flat])
    out = jax.ShapeDtypeStruct((rows, cols), F32)
    return pl.pallas_call(body, name="adamw_shard", grid_spec=gs, out_shape=(out, out, out, out),
                          compiler_params=_params(1))(me, grad, *received, w, m, v)


def _small_reduce_adamw(slabs, w, m, v):
    _, rows, _ = slabs.shape

    def body(s_ref, w_ref, m_ref, v_ref, g_ref, d_ref, mo_ref, vo_ref, land, send_sems, recv_sems):
        x, y, c = _mesh_pos()
        me = _dev(x, y, c)
        copies = []
        for mask in range(1, N_DEV):
            px, py, pc = x ^ (mask >> 2), y ^ ((mask >> 1) & 1), c ^ (mask & 1)
            copies.append(pltpu.make_async_remote_copy(
                src_ref=s_ref.at[_dev(px, py, pc)], dst_ref=land.at[me], send_sem=send_sems.at[mask - 1],
                recv_sem=recv_sems.at[mask - 1], device_id=(px, py, pc), device_id_type=MESH))
        for cp in copies:
            cp.start()
        land[me] = s_ref[me]
        for mask in range(1, N_DEV):
            px, py, pc = x ^ (mask >> 2), y ^ ((mask >> 1) & 1), c ^ (mask & 1)
            pltpu.make_async_remote_copy(
                src_ref=s_ref.at[me], dst_ref=land.at[_dev(px, py, pc)], send_sem=send_sems.at[mask - 1],
                recv_sem=recv_sems.at[mask - 1], device_id=(px, py, pc), device_id_type=MESH).wait_recv()
        for cp in copies:
            cp.wait_send()
        g = land[0]
        for d in range(1, N_DEV):
            g = g + land[d]
        delta, m_new, v_new = _adamw_math(w_ref[...], g, m_ref[...], v_ref[...])
        g_ref[...] = g
        d_ref[...] = delta
        mo_ref[...] = m_new
        vo_ref[...] = v_new

    out = jax.ShapeDtypeStruct((rows, LANE), F32)
    return pl.pallas_call(
        body, name="comm_small_reduce_adamw", in_specs=[VMEM_WHOLE] * 4, out_specs=[VMEM_WHOLE] * 4, out_shape=(out, out, out, out),
        scratch_shapes=[pltpu.VMEM((N_DEV, rows, LANE), F32), pltpu.SemaphoreType.DMA((N_DEV - 1,)),
                        pltpu.SemaphoreType.DMA((N_DEV - 1,))],
    )(slabs, w, m, v)


def _cast_bf16(arrs):
    n = len(arrs)

    def body(*refs):
        for i_ref, o_ref in zip(refs[:n], refs[n:]):
            o_ref[...] = i_ref[...].astype(BF16)

    return pl.pallas_call(body, name="cast_bf16", in_specs=[VMEM_WHOLE] * n, out_specs=[VMEM_WHOLE] * n,
                          out_shape=[jax.ShapeDtypeStruct(a.shape, BF16) for a in arrs],
                          compiler_params=pltpu.CompilerParams(vmem_limit_bytes=VMEM_LIMIT))(*arrs)


REPLICATED = ("pre_mix_g", "gate_b", "dw_b", "conv_ln_g", "conv_ln_b", "post_mix_g", "pre_ffn_g", "post_ffn_g")
SHARDED = ("w_in", "w_conv_out", "w_attn_out", "w_o", "w_ffn_in", "w_ffn_out")
WEIGHTS = ("meta_tokens", "pre_mix_g", "w_in", "gate_b", "dw_w", "dw_b", "conv_ln_g", "conv_ln_b", "w_conv_out",
           "w_attn_out", "w_o", "post_mix_g", "pre_ffn_g", "w_ffn_in", "w_ffn_out", "post_ffn_g")


def kernel(x, meta_tokens, pre_mix_g, w_in, gate_b, dw_w, dw_b, conv_ln_g, conv_ln_b, w_conv_out, w_attn_out, w_o, post_mix_g, pre_ffn_g, w_ffn_in, w_ffn_out, post_ffn_g, loss_target, m_meta_tokens, m_pre_mix_g, m_w_in, m_gate_b, m_dw_w, m_dw_b, m_conv_ln_g, m_conv_ln_b, m_w_conv_out, m_w_attn_out, m_w_o, m_post_mix_g, m_pre_ffn_g, m_w_ffn_in, m_w_ffn_out, m_post_ffn_g, v_meta_tokens, v_pre_mix_g, v_w_in, v_gate_b, v_dw_w, v_dw_b, v_conv_ln_g, v_conv_ln_b, v_w_conv_out, v_w_attn_out, v_w_o, v_post_mix_g, v_pre_ffn_g, v_w_ffn_in, v_w_ffn_out, v_post_ffn_g):
    given = dict(locals())
    seq, d = x.shape[1], x.shape[2]
    n_meta = meta_tokens.shape[0]
    length = n_meta + seq
    m_rows = -(-length // QB) * QB
    dc = d // N_DEV
    assert dc == LANE and n_meta % 8 == 0 and seq % 8 == 0
    fs = w_ffn_in.shape[2]
    fr = w_ffn_out.shape[1]
    assert 2 * fr == fs

    transposed = ("w_ffn_in",)

    def shard(name):
        return given[name][0].T if name.endswith(transposed) else given[name][0]

    local = {k: shard(k) for k in SHARDED}
    cast = _cast_bf16([local[k] for k in SHARDED])
    dww_pad = jnp.pad(dw_w[0], ((0, CONV_PAD - CONV_WIDTH), (0, 0)))
    wi, meta_g, dww_g = _all_gather([cast[0], meta_tokens, dww_pad])
    meta_full = jnp.concatenate([meta_g[j] for j in range(N_DEV)], axis=1)
    dww_full = jnp.concatenate([dww_g[j] for j in range(N_DEV)], axis=1)
    ns = wi.shape[2]

    tail = jnp.zeros((m_rows - length, d), F32)
    h0 = jnp.concatenate([meta_full, x[0], tail], axis=0)
    target = jnp.concatenate([jnp.zeros((n_meta, d), F32), loss_target[0], tail], axis=0)

    (u,) = _rows("pre_mix_norm", lambda r0, xs, ps: ([_rms(xs[0], ps[0])], []), [h0], [pre_mix_g], [BF16], [])
    square = list(cast[1:4])
    p, p16, *gathered = _matmul(
        "in_proj", NN, u, wi, pl.BlockSpec((m_rows, d), lambda i: (0, 0)), pl.BlockSpec((None, d, ns), lambda i: (i, 0, 0)),
        pl.BlockSpec((m_rows, ns), lambda i: (0, i)), jax.ShapeDtypeStruct((m_rows, N_DEV * ns), F32), (N_DEV,), twin_bf16=True,
        carried=(_gather_copies, square, [jax.ShapeDtypeStruct((N_DEV,) + s.shape, s.dtype) for s in square],
                 _exchange_sems(len(square)) + [pltpu.SemaphoreType.DMA((len(square),))]))
    wco, wao, wo = (g.reshape(d, d) for g in gathered)
    o, tot, (wfi_t, wfo) = _attn_fwd(p16, d, list(cast[4:6]))
    wfo = wfo.reshape(N_DEV // 2, fs, d)
    y = _conv_fwd(p, dww_full, dw_b, d)
    (yc,) = _rows("conv_norm", lambda r0, xs, ps: ([_ln_silu(xs[0], ps[0], ps[1])], []), [y], [conv_ln_g, conv_ln_b], [BF16], [])
    y_conv = _dense_fwd("conv_out", yc, wco)
    y_attn, mixin = _attn_out_gate(o, wao, p, y_conv, gate_b)
    mix = _dense_fwd("mix_out", mixin, wo)
    h1, u2 = _rows("post_mix", lambda r0, xs, ps: (list(_post_mix(xs[0], xs[1], ps[0], ps[1])), []), [h0, mix],
                   [post_mix_g, pre_ffn_g], [F32, BF16], [])
    half = N_DEV // 2
    a_act, b_act, f_in = _ffn_in_swiglu(u2, wfi_t)
    f = _matmul("ffn_out", NN, f_in, wfo, pl.BlockSpec((None, m_rows, fs), lambda j: (j, 0, 0)), pl.BlockSpec((None, fs, d), lambda j: (j, 0, 0)),
                pl.BlockSpec((m_rows, d), lambda j: (0, 0)), jax.ShapeDtypeStruct((m_rows, d), F32), (half,), acc_axis=0)

    def loss_head(r0, xs, ps):
        h1_, f_, t_ = xs
        r, vjp = jax.vjp(_rms, f_, ps[0])
        rows = r0 + lax.broadcasted_iota(I32, (h1_.shape[0], 1), 0)
        real = (rows >= n_meta) & (rows < length)
        err = jnp.where(real, h1_ + r - t_, 0.0)
        dh2 = err * (1.0 / d)
        d_f, dg = vjp(dh2)
        part = jnp.sum(0.5 * jnp.mean(err * err, axis=-1, keepdims=True), axis=0, keepdims=True)
        return [d_f, dh2], [dg, jnp.broadcast_to(part, (1, LANE))]

    d_f, dh2, g_post_ffn, loss_part = _rows("loss_head", loss_head, [h1, f, target], [post_ffn_g], [BF16, F32], [d, LANE])

    d_ab = _ffn_out_dx_swiglu(d_f, wfo, a_act, b_act).reshape(N_DEV, m_rows, fs)
    g_wfo = _matmul("ffn_out_dw", TN, f_in, d_f, pl.BlockSpec((None, m_rows, fs), lambda j: (j, 0, 0)), pl.BlockSpec((m_rows, d), lambda j: (0, 0)),
                    pl.BlockSpec((None, fs, d), lambda j: (j, 0, 0)), jax.ShapeDtypeStruct((half, fs, d), F32), (half,), twin_bf16=True)

    du2 = _matmul("ffn_in_dx", NN, d_ab, wfi_t, pl.BlockSpec((None, m_rows, fs), lambda i: (i, 0, 0)), pl.BlockSpec((None, fs, d), lambda i: (i, 0, 0)),
                  pl.BlockSpec((m_rows, d), lambda i: (0, 0)), jax.ShapeDtypeStruct((m_rows, d), F32), (N_DEV,), acc_axis=0)
    g_wfi = _matmul("ffn_in_dw", TN, d_ab, u2, pl.BlockSpec((None, m_rows, fs), lambda i: (i, 0, 0)), pl.BlockSpec((m_rows, d), lambda i: (0, 0)),
                    pl.BlockSpec((None, fs, d), lambda i: (i, 0, 0)), jax.ShapeDtypeStruct((N_DEV, fs, d), F32), (N_DEV,), twin_bf16=True)

    def post_mix_bwd(r0, xs, ps):
        h0_, mix_, dh2_, du2_ = xs
        _, vjp = jax.vjp(_post_mix, h0_, mix_, ps[0], ps[1])
        dh0_, dmix_, dg1, dg2 = vjp((dh2_, du2_))
        return [dmix_, dh0_], [dg1, dg2]

    d_mix, dh1, g_post_mix, g_pre_ffn = _rows("post_mix_bwd", post_mix_bwd, [h0, mix, dh2, du2], [post_mix_g, pre_ffn_g],
                                              [BF16, F32], [d, d])
    g_wo = _dense_dw("mix_out_dw", mixin, d_mix)
    dp_gc, dp_ga, d_yconv, d_yattn, g_gb_c, g_gb_a = _mix_out_dx_gate(d_mix, wo, p, y_conv, y_attn, gate_b)
    g_gate_b = jnp.concatenate([g_gb_c, g_gb_a], axis=1)
    d_o = _dense_dx("attn_out_dx", d_yattn, wao, BF16)
    g_wao = _dense_dw("attn_out_dw", o, d_yattn)
    d_yc = _dense_dx("conv_out_dx", d_yconv, wco, F32)
    g_wco = _dense_dw("conv_out_dw", yc, d_yconv)
    big = {"w_ffn_out": [g.reshape(N_DEV, fr, d) for g in g_wfo], "w_ffn_in": g_wfi,
           "w_o": [g.reshape(N_DEV, dc, d) for g in g_wo], "w_attn_out": [g.reshape(N_DEV, dc, d) for g in g_wao],
           "w_conv_out": [g.reshape(N_DEV, dc, d) for g in g_wco]}
    early = ("w_ffn_out", "w_ffn_in", "w_o", "w_attn_out", "w_conv_out")
    dq, dk, dv, received_early = _attn_bwd(p16, d_o, tot, d, [big[k][1] for k in early])

    def conv_norm_bwd(r0, xs, ps):
        _, vjp = jax.vjp(_ln_silu, xs[0], ps[0], ps[1])
        dy_, dg, db = vjp(xs[1])
        return [dy_], [dg, db]

    d_y, g_ln_g, g_ln_b = _rows("conv_norm_bwd", conv_norm_bwd, [y, d_yc], [conv_ln_g, conv_ln_b], [F32], [d, d])
    dp_a, dp_g, g_dww, g_dwb = _conv_bwd(p, d_y, dww_full, d)
    dp = jnp.concatenate([dp_a, dp_g, dq, dk, dv, dp_gc, dp_ga], axis=1)
    g_wi = _matmul("in_proj_dw", TN, u, dp, pl.BlockSpec((m_rows, d), lambda i: (0, 0)), pl.BlockSpec((m_rows, ns), lambda i: (0, i)),
                   pl.BlockSpec((None, d, ns), lambda i: (i, 0, 0)), jax.ShapeDtypeStruct((N_DEV, d, ns), F32), (N_DEV,), twin_bf16=True)
    x_i, y_i, c_i = _mesh_pos()
    slab_idx = jnp.stack([_dev(*chip, c_i) for chip in _other_chips(x_i, y_i)]).astype(I32)
    (wi_sibling,) = _sibling_exchange([g_wi[1]])
    wi_pairs = _pair_sum(slab_idx, g_wi[0], wi_sibling)
    du, wi_chips = _matmul(
        "in_proj_dx", NT, dp, wi, pl.BlockSpec((m_rows, ns), lambda i: (0, i)), pl.BlockSpec((None, d, ns), lambda i: (i, 0, 0)),
        pl.BlockSpec((m_rows, d), lambda i: (0, 0)), jax.ShapeDtypeStruct((m_rows, d), F32), (N_DEV,), acc_axis=0,
        carried=(_chip_copies, [wi_pairs], [jax.ShapeDtypeStruct(wi_pairs.shape, BF16)],
                 [pltpu.SemaphoreType.DMA((1, 3)), pltpu.SemaphoreType.DMA((1, 3))]))

    def pre_mix_bwd(r0, xs, ps):
        _, vjp = jax.vjp(_rms, xs[0], ps[0])
        dx, dg = vjp(xs[1])
        return [xs[2] + dx], [dg]

    dh0, g_pre_mix = _rows("pre_mix_bwd", pre_mix_bwd, [h0, du, dh1], [pre_mix_g], [F32], [d])
    grad_x = dh0[n_meta:length][None]

    me = _dev(x_i, y_i, c_i)
    me_arr = jnp.reshape(me, (1,)).astype(I32)
    big["w_in"] = g_wi
    received = {k: [r] for k, r in zip(early, received_early)}
    received["w_in"] = [wi_sibling[3:4], wi_chips]
    results = {}
    for k in SHARDED:
        outs = _adamw_shard(me_arr, big[k][0], received[k], local[k], shard("m_" + k), shard("v_" + k))
        results[k] = tuple((a.T if k in transposed else a)[None] for a in outs)

    rep_grads = {"pre_mix_g": g_pre_mix, "gate_b": g_gate_b, "dw_b": g_dwb, "conv_ln_g": g_ln_g, "conv_ln_b": g_ln_b,
                 "post_mix_g": g_post_mix, "pre_ffn_g": g_pre_ffn, "post_ffn_g": g_post_ffn}

    def pack_rep(get):
        return jnp.concatenate([get(k) for k in REPLICATED], axis=1).reshape(-1, LANE)

    rep_rows = pack_rep(lambda k: rep_grads[k])
    n_rep = rep_rows.shape[0]
    loss_rows = jnp.broadcast_to(loss_part, (8, LANE))
    g_meta = dh0[0:n_meta]
    slabs = jnp.stack([jnp.concatenate([rep_rows, loss_rows, g_dww[:, j * LANE:(j + 1) * LANE], g_meta[:, j * LANE:(j + 1) * LANE]], axis=0)
                       for j in range(N_DEV)])

    def pack_small(prefix):
        dww_own = jnp.pad(given[prefix + "dw_w"][0], ((0, CONV_PAD - CONV_WIDTH), (0, 0)))
        return jnp.concatenate([pack_rep(lambda k: given[prefix + k]), jnp.zeros((8, LANE), F32), dww_own,
                                given[prefix + "meta_tokens"]], axis=0)

    small = _small_reduce_adamw(slabs, pack_small(""), pack_small("m_"), pack_small("v_"))
    loss = small[0][n_rep, 0]

    def unpack(arr):
        out = {}
        flat = arr[:n_rep].reshape(1, -1)
        off = 0
        for k in REPLICATED:
            w = given[k].shape[1]
            out[k] = flat[:, off:off + w]
            off += w
        out["dw_w"] = arr[n_rep + 8:n_rep + 8 + CONV_WIDTH][None]
        out["meta_tokens"] = arr[n_rep + 8 + CONV_PAD:n_rep + 8 + CONV_PAD + n_meta]
        return out

    small_out = [unpack(a) for a in small]
    for k in WEIGHTS:
        if k not in results:
            results[k] = tuple(s[k] for s in small_out)
    return (loss, grad_x, *[results[k][0] for k in WEIGHTS], *[results[k][1] for k in WEIGHTS],
            *[results[k][2] for k in WEIGHTS], *[results[k][3] for k in WEIGHTS])
```

```python
import jax
import jax.numpy as jnp
from jax import lax
from jax.experimental import pallas as pl
from jax.experimental.pallas import tpu as pltpu

F32 = jnp.float32
BF16 = jnp.bfloat16
I32 = jnp.int32

N_DEV = 8
LANE = 128
HEAD_DIM = 64
QB = 128
KEY_SHIFT = 2
KEY_TILES = 1 << KEY_SHIFT
KEY_CHUNK = KEY_TILES * QB
LANE_BLOCKS = 4
BWD_LANE_BLOCKS = 4
CONV_WIDTH = 31
CONV_PAD = 32
ROW_CHUNK = 128
RMS_EPS = 1e-6
LN_EPS = 1e-5
ADAM_LR = 0.001
ADAM_B1 = 0.9
ADAM_B2 = 0.999
ADAM_EPS = 1e-08
ADAM_WD = 0.01
ADAM_STEP = 10
VMEM_LIMIT = 56 * 1024 * 1024

NN = (((1,), (0,)), ((), ()))
NT = (((1,), (1,)), ((), ()))
TN = (((0,), (0,)), ((), ()))
MESH = pl.DeviceIdType.MESH
ANY = pl.BlockSpec(memory_space=pl.ANY)
VMEM_WHOLE = pl.BlockSpec(memory_space=pltpu.VMEM)


def _params(n_axes):
    return pltpu.CompilerParams(dimension_semantics=("arbitrary",) * n_axes, vmem_limit_bytes=VMEM_LIMIT)


def _row_tile(m):
    assert m % QB == 0
    return m // 4 if m % 64 == 0 else QB


def _matmul(name, dims, a, b, a_spec, b_spec, o_spec, out_shape, grid, acc_axis=None, twin_bf16=False, carried=None):
    n_car = 0 if carried is None else len(carried[1])
    n_twin = 1 if twin_bf16 else 0

    def body(a_ref, b_ref, *rest):
        car_ins, o_ref, twin, rest = rest[:n_car], rest[n_car], rest[n_car + 1:n_car + 1 + n_twin], rest[n_car + 1 + n_twin:]
        if carried is not None:
            starts, relays, waits = carried[0](car_ins, rest[:n_car], *rest[n_car:])

            @pl.when(pl.program_id(0) == 0)
            def _():
                for f in starts:
                    f()

            if relays:
                @pl.when(pl.program_id(0) == (3 * grid[0]) // 4)
                def _():
                    for f in relays:
                        f()

        r = lax.dot_general(a_ref[...], b_ref[...], dims, preferred_element_type=F32)
        if acc_axis is None:
            o_ref[...] = r.astype(o_ref.dtype)
            for t_ref in twin:
                t_ref[...] = r.astype(BF16)
        else:
            k = pl.program_id(acc_axis)

            @pl.when(k == 0)
            def _():
                o_ref[...] = r

            @pl.when(k > 0)
            def _():
                o_ref[...] += r

        if carried is not None:
            @pl.when(pl.program_id(0) == grid[0] - 1)
            def _():
                for f in waits:
                    f()

    in_specs, out_specs, out_shapes, scratch = [a_spec, b_spec], [o_spec], [out_shape], []
    if twin_bf16:
        assert acc_axis is None
        out_specs.append(o_spec)
        out_shapes.append(jax.ShapeDtypeStruct(out_shape.shape, BF16))
    if carried is not None:
        assert len(grid) == 1
        in_specs += [ANY] * n_car
        out_specs += [ANY] * n_car
        out_shapes += list(carried[2])
        scratch = list(carried[3])
    outs = pl.pallas_call(body, name=name, grid=grid, in_specs=in_specs, out_specs=out_specs, out_shape=out_shapes,
                          scratch_shapes=scratch, compiler_params=_params(len(grid)))(a, b, *(carried[1] if carried else ()))
    return outs[0] if len(outs) == 1 else outs


def _ffn_rows(m):
    return m // 2 if m % 32 == 0 else m


def _ffn_in_swiglu(u2, wfi_t):
    m, d = u2.shape
    half, fs = N_DEV // 2, wfi_t.shape[1]
    tm = _ffn_rows(m)

    def body(u_ref, wa_ref, wb_ref, a_ref, b_ref, f_ref):
        u = u_ref[...]
        a = lax.dot_general(u, wa_ref[...], NT, preferred_element_type=F32)
        b = lax.dot_general(u, wb_ref[...], NT, preferred_element_type=F32)
        a_ref[...] = a.astype(BF16)
        b_ref[...] = b.astype(BF16)
        f_ref[...] = _swiglu(a, b).astype(BF16)

    out = jax.ShapeDtypeStruct((half, m, fs), BF16)
    act = pl.BlockSpec((None, tm, fs), lambda j, i: (j, i, 0))
    return pl.pallas_call(
        body, name="ffn_in_swiglu", grid=(half, m // tm),
        in_specs=[pl.BlockSpec((tm, d), lambda j, i: (i, 0)), pl.BlockSpec((None, fs, d), lambda j, i: (j, 0, 0)),
                  pl.BlockSpec((None, fs, d), lambda j, i: (j + half, 0, 0))],
        out_specs=[act, act, act], out_shape=[out, out, out], compiler_params=_params(2))(u2, wfi_t, wfi_t)


def _ffn_out_dx_swiglu(d_f, wfo, a_act, b_act):
    m, d = d_f.shape
    half, fs = N_DEV // 2, wfo.shape[1]
    tm = _ffn_rows(m)

    def body(df_ref, w_ref, a_ref, b_ref, o_ref):
        d_fin = lax.dot_general(df_ref[...], w_ref[...], NT, preferred_element_type=F32)
        _, vjp = jax.vjp(_swiglu, a_ref[...].astype(F32), b_ref[...].astype(F32))
        d_a, d_b = vjp(d_fin)
        o_ref[0] = d_a.astype(BF16)
        o_ref[1] = d_b.astype(BF16)

    act = pl.BlockSpec((None, tm, fs), lambda j, i: (j, i, 0))
    return pl.pallas_call(
        body, name="ffn_out_dx_swiglu", grid=(half, m // tm),
        in_specs=[pl.BlockSpec((tm, d), lambda j, i: (i, 0)), pl.BlockSpec((None, fs, d), lambda j, i: (j, 0, 0)), act, act],
        out_specs=pl.BlockSpec((2, None, tm, fs), lambda j, i: (0, j, i, 0)),
        out_shape=jax.ShapeDtypeStruct((2, half, m, fs), BF16), compiler_params=_params(2))(d_f, wfo, a_act, b_act)


DENSE_TILE = 256


def _dense_fwd(name, a, w, out_dtype=F32):
    m, k = a.shape
    n = w.shape[1]
    tn = DENSE_TILE
    return _matmul(name, NN, a, w, pl.BlockSpec((m, k), lambda j: (0, 0)), pl.BlockSpec((k, tn), lambda j: (0, j)),
                   pl.BlockSpec((m, tn), lambda j: (0, j)), jax.ShapeDtypeStruct((m, n), out_dtype), (n // tn,))


def _dense_dx(name, dy, w, out_dtype):
    m, n = dy.shape
    k = w.shape[0]
    tk = DENSE_TILE
    return _matmul(name, NT, dy, w, pl.BlockSpec((m, n), lambda j: (0, 0)), pl.BlockSpec((tk, n), lambda j: (j, 0)),
                   pl.BlockSpec((m, tk), lambda j: (0, j)), jax.ShapeDtypeStruct((m, k), out_dtype), (k // tk,))


def _dense_dw(name, a, dy):
    m, k = a.shape
    n = dy.shape[1]
    tn = DENSE_TILE
    return _matmul(name, TN, a, dy, pl.BlockSpec((m, k), lambda j: (0, 0)), pl.BlockSpec((m, tn), lambda j: (0, j)),
                   pl.BlockSpec((k, tn), lambda j: (0, j)), jax.ShapeDtypeStruct((k, n), F32), (n // tn,), twin_bf16=True)


def _gate_specs(m, d, n_row_ins):
    t = DENSE_TILE
    per = d // t
    cols = lambda base: pl.BlockSpec((m, t), lambda j, base=base: (0, base * per + j))
    bias = lambda base: pl.BlockSpec((1, t), lambda j, base=base: (0, base * per + j))
    return [cols(5), cols(6)] + [cols(0)] * n_row_ins + [bias(0), bias(1)]


def _attn_out_gate(o, wao, p, y_conv, gate_b):
    m, d = o.shape
    t = DENSE_TILE

    def body(o_ref, w_ref, pgc_ref, pga_ref, yc_ref, bc_ref, ba_ref, ya_ref, mix_ref):
        ya = jnp.dot(o_ref[...], w_ref[...], preferred_element_type=F32)
        ya_ref[...] = ya
        gb = jnp.concatenate([bc_ref[...], ba_ref[...]], axis=1)
        mix_ref[...] = _gate_mix(pgc_ref[...], pga_ref[...], yc_ref[...], ya, gb).astype(mix_ref.dtype)

    tile = pl.BlockSpec((m, t), lambda j: (0, j))
    return pl.pallas_call(
        body, name="attn_out_gate", grid=(d // t,),
        in_specs=[pl.BlockSpec((m, d), lambda j: (0, 0)), pl.BlockSpec((d, t), lambda j: (0, j))] + _gate_specs(m, d, 1),
        out_specs=[tile, tile], out_shape=[jax.ShapeDtypeStruct((m, d), F32), jax.ShapeDtypeStruct((m, d), BF16)],
        compiler_params=_params(1))(o, wao, p, p, y_conv, gate_b, gate_b)


def _mix_out_dx_gate(d_mix, wo, p, y_conv, y_attn, gate_b):
    m, d = d_mix.shape
    t = DENSE_TILE

    def body(dm_ref, w_ref, pgc_ref, pga_ref, yc_ref, ya_ref, bc_ref, ba_ref, dgc_ref, dga_ref, dyc_ref, dya_ref, dbc_ref, dba_ref):
        d_mixin = lax.dot_general(dm_ref[...], w_ref[...], NT, preferred_element_type=F32)
        gb = jnp.concatenate([bc_ref[...], ba_ref[...]], axis=1)
        _, vjp = jax.vjp(_gate_mix, pgc_ref[...], pga_ref[...], yc_ref[...], ya_ref[...], gb)
        dgc, dga, dyc, dya, dgb = vjp(d_mixin)
        dgc_ref[...] = dgc.astype(dgc_ref.dtype)
        dga_ref[...] = dga.astype(dga_ref.dtype)
        dyc_ref[...] = dyc.astype(dyc_ref.dtype)
        dya_ref[...] = dya.astype(dya_ref.dtype)
        dbc_ref[...] = dgb[:, :t]
        dba_ref[...] = dgb[:, t:]

    tile = pl.BlockSpec((m, t), lambda j: (0, j))
    vec = pl.BlockSpec((1, t), lambda j: (0, j))
    act = jax.ShapeDtypeStruct((m, d), BF16)
    par = jax.ShapeDtypeStruct((1, d), F32)
    return pl.pallas_call(
        body, name="mix_out_dx_gate", grid=(d // t,),
        in_specs=[pl.BlockSpec((m, d), lambda j: (0, 0)), pl.BlockSpec((t, d), lambda j: (j, 0))] + _gate_specs(m, d, 2),
        out_specs=[tile] * 4 + [vec, vec], out_shape=[act] * 4 + [par, par],
        compiler_params=_params(1))(d_mix, wo, p, p, y_conv, y_attn, gate_b, gate_b)


def _rowwise(name, fn, row_ins, par_ins, row_outs, par_outs, *, grid, in_specs, out_specs, tm, row_axis):
    n_ri, n_pi, n_ro, n_po = len(row_ins), len(par_ins), len(row_outs), len(par_outs)
    n_steps, tail = divmod(tm, ROW_CHUNK)
    assert tail % 16 == 0

    def body(*refs):
        ri = refs[:n_ri]
        pi = refs[n_ri:n_ri + n_pi]
        ro = refs[n_ri + n_pi:n_ri + n_pi + n_ro]
        po = refs[n_ri + n_pi + n_ro:]
        ps = [r[...] for r in pi]
        base = pl.program_id(row_axis) * tm

        def chunk(r0, rows, carry):
            xs = [r[pl.ds(r0, rows), :] for r in ri]
            outs, pouts = fn(base + r0, xs, ps)
            for r, o in zip(ro, outs):
                if isinstance(o, (list, tuple)):
                    for j, part in enumerate(o):
                        r[j, pl.ds(r0, rows), :] = part.astype(r.dtype)
                else:
                    r[pl.ds(r0, rows), :] = o.astype(r.dtype)
            return tuple(c + q for c, q in zip(carry, pouts))

        def step(i, carry):
            return chunk(pl.multiple_of(i * ROW_CHUNK, ROW_CHUNK), ROW_CHUNK, carry)

        acc = lax.fori_loop(0, n_steps, step, tuple(jnp.zeros(s.shape, F32) for s in par_outs))
        if tail:
            acc = chunk(n_steps * ROW_CHUNK, tail, acc)
        if n_po:
            first = pl.program_id(0) == 0
            for ax in range(1, len(grid)):
                first = first & (pl.program_id(ax) == 0)

            @pl.when(first)
            def _():
                for r in po:
                    r[...] = jnp.zeros_like(r)

            for r, a in zip(po, acc):
                r[...] += a

    return pl.pallas_call(body, name=name, grid=grid, in_specs=in_specs, out_specs=out_specs,
                          out_shape=tuple(row_outs) + tuple(par_outs),
                          compiler_params=_params(len(grid)))(*row_ins, *par_ins)


def _rows(name, fn, row_ins, par_ins, row_out_dtypes, par_out_widths, row_in_cols=None, row_out_widths=None):
    m = row_ins[0].shape[0]
    tm = _row_tile(m)
    in_specs = []
    for k, a in enumerate(row_ins):
        if row_in_cols is not None and row_in_cols[k] is not None:
            width, cb = row_in_cols[k]
            in_specs.append(pl.BlockSpec((tm, width), lambda i, cb=cb: (i, cb)))
        else:
            in_specs.append(pl.BlockSpec((tm, a.shape[1]), lambda i: (i, 0)))
    for a in par_ins:
        in_specs.append(pl.BlockSpec(a.shape, lambda i: (0, 0)))
    if row_out_widths is None:
        row_out_widths = [row_ins[0].shape[1]] * len(row_out_dtypes)
    row_outs = [jax.ShapeDtypeStruct((m, w), dt) for w, dt in zip(row_out_widths, row_out_dtypes)]
    par_outs = [jax.ShapeDtypeStruct((1, w), F32) for w in par_out_widths]
    out_specs = [pl.BlockSpec((tm, s.shape[1]), lambda i: (i, 0)) for s in row_outs]
    out_specs += [pl.BlockSpec(s.shape, lambda i: (0, 0)) for s in par_outs]
    return _rowwise(name, fn, row_ins, par_ins, row_outs, par_outs, grid=(m // tm,), in_specs=in_specs,
                    out_specs=out_specs, tm=tm, row_axis=0)


def _rms(x, g):
    return x * lax.rsqrt(jnp.mean(x * x, axis=-1, keepdims=True) + RMS_EPS) * g


def _ln_silu(y, g, b):
    mu = jnp.mean(y, axis=-1, keepdims=True)
    yc = y - mu
    var = jnp.mean(yc * yc, axis=-1, keepdims=True)
    return jax.nn.silu(yc * lax.rsqrt(var + LN_EPS) * g + b)


def _gate_mix(pgc, pga, yc, ya, gb):
    d = pgc.shape[1]
    return jax.nn.sigmoid(pgc + gb[:, :d]) * yc + jax.nn.sigmoid(pga + gb[:, d:]) * ya


def _post_mix(h0, mix, g_post, g_pre):
    h1 = h0 + _rms(mix, g_post)
    return h1, _rms(h1, g_pre)


def _swiglu(a, b):
    return jax.nn.silu(a) * b


def _conv_taps():
    taps = []
    for b in range(8):
        for a in range(CONV_PAD // 8):
            s = 8 * a + b
            if s < CONV_WIDTH:
                taps.append((b, a, CONV_WIDTH - 1 - s))
    return taps


def _conv_fwd(p, dww, dwb, d_model):
    m = p.shape[0]
    nch = d_model // LANE
    n_chunk = m // QB
    taps = _conv_taps()

    def body(a_ref, g_ref, w_ref, b_ref, y_ref, upad):
        upad[0:CONV_PAD, :] = jnp.zeros((CONV_PAD, LANE), F32)

        def fill(i, c):
            r0 = pl.multiple_of(i * QB, QB)
            u = a_ref[pl.ds(r0, QB), :] * jax.nn.sigmoid(g_ref[pl.ds(r0, QB), :])
            upad[pl.ds(pl.multiple_of(r0 + CONV_PAD, 8), QB), :] = u
            return c

        lax.fori_loop(0, n_chunk, fill, 0)

        def conv(i, c):
            r0 = pl.multiple_of(i * QB, QB)
            win = upad[pl.ds(r0, QB + CONV_PAD), :]
            acc = jnp.broadcast_to(b_ref[...], (QB, LANE))
            rolled = {}
            for b, a, j in taps:
                if b not in rolled:
                    rolled[b] = win if b == 0 else pltpu.roll(win, b, axis=0)
                lo = CONV_PAD - 8 * a
                acc = acc + w_ref[j:j + 1, :] * rolled[b][lo:lo + QB, :]
            y_ref[pl.ds(r0, QB), :] = acc
            return c

        lax.fori_loop(0, n_chunk, conv, 0)

    col = lambda off: pl.BlockSpec((m, LANE), lambda c: (0, off + c))
    return pl.pallas_call(
        body, name="conv_fwd", grid=(nch,),
        in_specs=[col(0), col(nch), pl.BlockSpec((CONV_PAD, LANE), lambda c: (0, c)), pl.BlockSpec((1, LANE), lambda c: (0, c))],
        out_specs=col(0), out_shape=jax.ShapeDtypeStruct((m, d_model), F32),
        scratch_shapes=[pltpu.VMEM((m + CONV_PAD, LANE), F32)], compiler_params=_params(1))(p, p, dww, dwb)


def _conv_bwd(p, dy, dww, d_model):
    m = p.shape[0]
    nch = d_model // LANE
    n_chunk = m // QB
    taps = _conv_taps()
    win_rows = QB + CONV_PAD

    def body(a_ref, g_ref, dy_ref, w_ref, da_ref, dg_ref, dw_ref, db_ref, upad, dypad, wacc, bacc):
        upad[0:CONV_PAD, :] = jnp.zeros((CONV_PAD, LANE), F32)
        dypad[m:m + CONV_PAD, :] = jnp.zeros((CONV_PAD, LANE), F32)
        wacc[...] = jnp.zeros_like(wacc)
        bacc[...] = jnp.zeros_like(bacc)

        def fill(i, c):
            r0 = pl.multiple_of(i * QB, QB)
            u = a_ref[pl.ds(r0, QB), :] * jax.nn.sigmoid(g_ref[pl.ds(r0, QB), :])
            upad[pl.ds(pl.multiple_of(r0 + CONV_PAD, 8), QB), :] = u
            dypad[pl.ds(r0, QB), :] = dy_ref[pl.ds(r0, QB), :]
            return c

        lax.fori_loop(0, n_chunk, fill, 0)

        def chunk(i, c):
            r0 = pl.multiple_of(i * QB, QB)
            dwin = dypad[pl.ds(r0, win_rows), :]
            du = jnp.zeros((QB, LANE), F32)
            rolled = {}
            for b, a, j in taps:
                if b not in rolled:
                    rolled[b] = dwin if b == 0 else pltpu.roll(dwin, win_rows - b, axis=0)
                du = du + w_ref[j:j + 1, :] * rolled[b][8 * a:8 * a + QB, :]
            av = a_ref[pl.ds(r0, QB), :]
            sg = jax.nn.sigmoid(g_ref[pl.ds(r0, QB), :])
            da_ref[pl.ds(r0, QB), :] = (du * sg).astype(da_ref.dtype)
            dg_ref[pl.ds(r0, QB), :] = (du * av * sg * (1.0 - sg)).astype(dg_ref.dtype)
            dyc = dy_ref[pl.ds(r0, QB), :]
            uwin = upad[pl.ds(r0, win_rows), :]
            rolled = {}
            for b, a, j in taps:
                if b not in rolled:
                    rolled[b] = uwin if b == 0 else pltpu.roll(uwin, b, axis=0)
                lo = CONV_PAD - 8 * a
                prod = dyc * rolled[b][lo:lo + QB, :]
                wacc[j] += prod.reshape(QB // 8, 8, LANE).sum(axis=0)
            bacc[...] += dyc.reshape(QB // 8, 8, LANE).sum(axis=0)
            return c

        lax.fori_loop(0, n_chunk, chunk, 0)
        for j in range(CONV_WIDTH):
            dw_ref[j:j + 1, :] = jnp.sum(wacc[j], axis=0, keepdims=True)
        dw_ref[CONV_WIDTH:CONV_PAD, :] = jnp.zeros((CONV_PAD - CONV_WIDTH, LANE), F32)
        db_ref[...] = jnp.sum(bacc[...], axis=0, keepdims=True)

    col = lambda off: pl.BlockSpec((m, LANE), lambda c: (0, off + c))
    return pl.pallas_call(
        body, name="conv_bwd", grid=(nch,),
        in_specs=[col(0), col(nch), col(0), pl.BlockSpec((CONV_PAD, LANE), lambda c: (0, c))],
        out_specs=[col(0), col(0), pl.BlockSpec((CONV_PAD, LANE), lambda c: (0, c)), pl.BlockSpec((1, LANE), lambda c: (0, c))],
        out_shape=(jax.ShapeDtypeStruct((m, d_model), BF16), jax.ShapeDtypeStruct((m, d_model), BF16),
                   jax.ShapeDtypeStruct((CONV_PAD, d_model), F32), jax.ShapeDtypeStruct((1, d_model), F32)),
        scratch_shapes=[pltpu.VMEM((m + CONV_PAD, LANE), F32), pltpu.VMEM((m + CONV_PAD, LANE), F32),
                        pltpu.VMEM((CONV_PAD, 8, LANE), F32), pltpu.VMEM((8, LANE), F32)],
        compiler_params=_params(1))(p, p, dy, dww)


EXP_CLAMP = 80.0


def _one_plus_exp(z):
    return 1.0 + jnp.exp(jnp.minimum(z, EXP_CLAMP))


def _softplus(z):
    return jnp.maximum(jnp.log(_one_plus_exp(z)), z)


def _softplus_sigmoid(z):
    s = _one_plus_exp(z)
    return jnp.maximum(jnp.log(s), z), 1.0 - 1.0 / s


def _tile_cumsums(x, tri2):
    xb = x.astype(BF16)
    out = []
    for i in range(0, x.shape[1] // QB, 2):
        both = jnp.dot(xb[:, i * QB:(i + 2) * QB], tri2, preferred_element_type=F32)
        out += [both[:, :QB], both[:, QB:]]
    return out


def _tri2(kind):
    jj = lax.broadcasted_iota(I32, (2 * QB, 2 * QB), 0)
    ss = lax.broadcasted_iota(I32, (2 * QB, 2 * QB), 1)
    same = (jj >= QB) == (ss >= QB)
    keep = {"ge": jj >= ss, "le": jj <= ss}[kind]
    return jnp.where(same & keep, 1.0, 0.0).astype(BF16)


def _attn_fwd(p, d_model, shards):
    m = p.shape[0]
    nqb = m // QB
    ngrp = d_model // (LANE_BLOCKS * LANE)
    qo, ko, vo = 2 * ngrp, 3 * ngrp, 4 * ngrp
    scale = HEAD_DIM ** -0.5
    n_sh = len(shards)

    assert nqb >= KEY_TILES

    def body(q_ref, k_ref, v_ref, *rest):
        shard_refs, (o_ref, t_ref), rest = rest[:n_sh], rest[n_sh:n_sh + 2], rest[n_sh + 2:]
        gathered_refs, (acc_ref, car_ref), sems = rest[:n_sh], rest[n_sh:n_sh + 2], rest[n_sh + 2:]
        starts, relays, waits = _gather_copies(shard_refs, gathered_refs, *sems)

        @pl.when((pl.program_id(0) == 0) & (pl.program_id(1) == 0))
        def _():
            for f in starts:
                f()

        @pl.when((pl.program_id(0) == ngrp - 1) & (pl.program_id(1) == (5 * nqb) // 8))
        def _():
            for f in relays:
                f()

        qb = pl.program_id(1)
        lane = lax.broadcasted_iota(I32, (QB, LANE), 1)
        head0 = lane < HEAD_DIM
        row_g = qb * QB + lax.broadcasted_iota(I32, (QB, KEY_CHUNK), 0)
        col_l = lax.broadcasted_iota(I32, (QB, KEY_CHUNK), 1)
        tri = _tri2("ge")
        heads = range(2 * LANE_BLOCKS)
        qh = []
        for lb in range(LANE_BLOCKS):
            q2 = (q_ref[:, lb * LANE:(lb + 1) * LANE] * scale).astype(BF16)
            zero = jnp.zeros_like(q2)
            qh += [jnp.where(head0, q2, zero), jnp.where(head0, zero, q2)]
        acc_ref[...] = jnp.zeros_like(acc_ref)
        car_ref[...] = jnp.zeros_like(car_ref)

        def chunk(first_tile, bound, n_tiles=KEY_TILES):
            r0 = pl.multiple_of(first_tile * QB, QB)
            keys = n_tiles * QB
            kcs = [k_ref[pl.ds(r0, keys), lb * LANE:(lb + 1) * LANE].astype(BF16) for lb in range(LANE_BLOCKS)]
            vcs = [v_ref[pl.ds(r0, keys), lb * LANE:(lb + 1) * LANE].astype(BF16) for lb in range(LANE_BLOCKS)]
            if bound is not None and jnp.ndim(bound) == 2:
                bound = bound[:, :keys]
            valid = None if bound is None else (col_l[:, :keys] + r0) < bound
            zs = [lax.dot_general(qh[h], kcs[h // 2], NT, preferred_element_type=F32) for h in heads]
            sps = [_softplus(z) for z in zs]
            if valid is not None:
                sps = [jnp.where(valid, sp, 0.0) for sp in sps]
            cums = [_tile_cumsums(sp, tri) for sp in sps]
            cars = [car_ref[h] for h in heads]
            a_tiles = [[None] * n_tiles for h in heads]
            for i in reversed(range(n_tiles)):
                for h in heads:
                    cum = cums[h][i]
                    a_tiles[h][i] = jnp.exp(zs[h][:, i * QB:(i + 1) * QB] - (cum + cars[h]))
                    cars[h] = cars[h] + jnp.broadcast_to(cum[:, 0:1], cum.shape)
            for h in heads:
                a = jnp.concatenate(a_tiles[h], axis=1)
                if valid is not None:
                    a = jnp.where(valid, a, 0.0)
                acc_ref[h] += jnp.dot(a.astype(BF16), vcs[h // 2], preferred_element_type=F32)
                car_ref[h] = cars[h]

        near = jnp.maximum(qb - (KEY_TILES - 1), 0)

        @pl.when(qb < 2)
        def _():
            chunk(0, row_g, n_tiles=2)

        @pl.when(qb >= 2)
        def _():
            chunk(near, row_g)

        n_full = lax.shift_right_logical(near, KEY_SHIFT)

        def step(i, c):
            chunk(near - KEY_TILES * (i + 1), None)
            return c

        lax.fori_loop(0, n_full, step, 0)
        left = near - KEY_TILES * n_full

        @pl.when((left > 0) & (left <= 2))
        def _():
            chunk(0, left * QB, n_tiles=2)

        @pl.when(left > 2)
        def _():
            chunk(0, left * QB)

        for lb in range(LANE_BLOCKS):
            o_ref[:, lb * LANE:(lb + 1) * LANE] = jnp.where(head0, acc_ref[2 * lb], acc_ref[2 * lb + 1]).astype(o_ref.dtype)
        for h in heads:
            t_ref[:, h * QB:(h + 1) * QB] = car_ref[h]

        @pl.when((pl.program_id(0) == ngrp - 1) & (pl.program_id(1) == nqb - 1))
        def _():
            for f in waits:
                f()

    wide = LANE_BLOCKS * LANE
    outs = pl.pallas_call(
        body, name="attn_fwd", grid=(ngrp, nqb),
        in_specs=[pl.BlockSpec((QB, wide), lambda g, qb: (qb, qo + g)),
                  pl.BlockSpec((m, wide), lambda g, qb: (0, ko + g)),
                  pl.BlockSpec((m, wide), lambda g, qb: (0, vo + g))] + [ANY] * n_sh,
        out_specs=[pl.BlockSpec((QB, wide), lambda g, qb: (qb, g)),
                   pl.BlockSpec((QB, 2 * wide), lambda g, qb: (qb, g))] + [ANY] * n_sh,
        out_shape=[jax.ShapeDtypeStruct((m, d_model), BF16), jax.ShapeDtypeStruct((m, 2 * d_model), F32)]
        + [jax.ShapeDtypeStruct((N_DEV,) + s.shape, s.dtype) for s in shards],
        scratch_shapes=[pltpu.VMEM((2 * LANE_BLOCKS, QB, LANE), F32), pltpu.VMEM((2 * LANE_BLOCKS, QB, LANE), F32)]
        + _exchange_sems(n_sh) + [pltpu.SemaphoreType.DMA((n_sh,))],
        compiler_params=_params(2))(p, p, p, *shards)
    return outs[0], outs[1], outs[2:]


def _attn_bwd(p, d_o, tot, d_model, slabs):
    m = p.shape[0]
    nqb = m // QB
    blocks = BWD_LANE_BLOCKS
    ngrp = d_model // (blocks * LANE)
    qo, ko, vo = 2 * ngrp, 3 * ngrp, 4 * ngrp
    scale = HEAD_DIM ** -0.5
    n_sl = len(slabs)

    assert nqb >= KEY_TILES

    def body(q_ref, k_ref, v_ref, do_ref, t_ref, *rest):
        slab_refs, (dq_ref, dk_ref, dv_ref), rest = rest[:n_sl], rest[n_sl:n_sl + 3], rest[n_sl + 3:]
        recv_refs, (dkacc, dvacc, dqacc, csp, cg), sems = rest[:n_sl], rest[n_sl:n_sl + 5], rest[n_sl + 5:]
        starts, waits = _scatter_copies(slab_refs, recv_refs, *sems)

        @pl.when((pl.program_id(0) == 0) & (pl.program_id(1) == 0))
        def _():
            for f in starts:
                f()

        qb = pl.program_id(1)
        lane = lax.broadcasted_iota(I32, (QB, LANE), 1)
        head0 = lane < HEAD_DIM
        row_g = qb * QB + lax.broadcasted_iota(I32, (QB, KEY_CHUNK), 0)
        col_l = lax.broadcasted_iota(I32, (QB, KEY_CHUNK), 1)
        tri_ge = _tri2("ge")
        tri_le = _tri2("le")
        heads = range(2 * blocks)
        qh, doh = [], []
        for lb in range(blocks):
            q2 = (q_ref[:, lb * LANE:(lb + 1) * LANE] * scale).astype(BF16)
            do2 = do_ref[:, lb * LANE:(lb + 1) * LANE]
            zero = jnp.zeros_like(q2)
            qh += [jnp.where(head0, q2, zero), jnp.where(head0, zero, q2)]
            doh += [jnp.where(head0, do2, zero), jnp.where(head0, zero, do2)]
        q_pairs = [jnp.concatenate(qh[2 * lb:2 * lb + 2], axis=0) for lb in range(blocks)]
        do_pairs = [jnp.concatenate(doh[2 * lb:2 * lb + 2], axis=0) for lb in range(blocks)]

        @pl.when(qb == 0)
        def _():
            dkacc[...] = jnp.zeros_like(dkacc)
            dvacc[...] = jnp.zeros_like(dvacc)

        dqacc[...] = jnp.zeros_like(dqacc)
        for h in heads:
            csp[h] = t_ref[:, h * QB:(h + 1) * QB]
        cg[...] = jnp.zeros_like(cg)

        def chunk(first_tile, bound, n_tiles=KEY_TILES):
            r0 = pl.multiple_of(first_tile * QB, QB)
            keys = n_tiles * QB
            kcs = [k_ref[pl.ds(r0, keys), lb * LANE:(lb + 1) * LANE].astype(BF16) for lb in range(blocks)]
            vcs = [v_ref[pl.ds(r0, keys), lb * LANE:(lb + 1) * LANE].astype(BF16) for lb in range(blocks)]
            if bound is not None and jnp.ndim(bound) == 2:
                bound = bound[:, :keys]
            valid = None if bound is None else (col_l[:, :keys] + r0) < bound
            tiles = [slice(i * QB, (i + 1) * QB) for i in range(n_tiles)]
            zs = [lax.dot_general(qh[h], kcs[h // 2], NT, preferred_element_type=F32) for h in heads]
            das = [lax.dot_general(doh[h], vcs[h // 2], NT, preferred_element_type=F32) for h in heads]
            sps, sgs = zip(*[_softplus_sigmoid(z) for z in zs])
            if valid is not None:
                sps = [jnp.where(valid, sp, 0.0) for sp in sps]
            cums = [_tile_cumsums(sp, tri_ge) for sp in sps]
            a_tiles, g_tiles = [[] for h in heads], [[] for h in heads]
            for h in heads:
                rest = csp[h]
                for i, c in enumerate(tiles):
                    cum = cums[h][i]
                    rest = rest - jnp.broadcast_to(cum[:, 0:1], cum.shape)
                    a = jnp.exp(zs[h][:, c] - (cum + rest))
                    if valid is not None:
                        a = jnp.where(valid[:, c], a, 0.0)
                    a_tiles[h].append(a)
                    g_tiles[h].append(a * das[h][:, c])
                csp[h] = rest
            gcums = [_tile_cumsums(jnp.concatenate(g_tiles[h], axis=1), tri_le) for h in heads]
            dzbs, abs_ = [], []
            for h in heads:
                g_before = cg[h]
                dz_tiles = []
                for i, c in enumerate(tiles):
                    gcum = gcums[h][i]
                    dz = g_tiles[h][i] - sgs[h][:, c] * (g_before + gcum)
                    if valid is not None:
                        dz = jnp.where(valid[:, c], dz, 0.0)
                    dz_tiles.append(dz)
                    g_before = g_before + jnp.broadcast_to(gcum[:, QB - 1:QB], gcum.shape)
                cg[h] = g_before
                dzbs.append(jnp.concatenate(dz_tiles, axis=1).astype(BF16))
                abs_.append(jnp.concatenate(a_tiles[h], axis=1).astype(BF16))
            for h in heads:
                dqacc[h] += jnp.dot(dzbs[h], kcs[h // 2], preferred_element_type=F32)
            for lb in range(blocks):
                dz_pair = jnp.concatenate(dzbs[2 * lb:2 * lb + 2], axis=0)
                a_pair = jnp.concatenate(abs_[2 * lb:2 * lb + 2], axis=0)
                dkacc[pl.ds(r0, keys), lb * LANE:(lb + 1) * LANE] += lax.dot_general(
                    dz_pair, q_pairs[lb], TN, preferred_element_type=F32)
                dvacc[pl.ds(r0, keys), lb * LANE:(lb + 1) * LANE] += lax.dot_general(
                    a_pair, do_pairs[lb], TN, preferred_element_type=F32)

        near = jnp.maximum(qb - (KEY_TILES - 1), 0)
        n_full = lax.shift_right_logical(near, KEY_SHIFT)

        def step(i, c):
            chunk(KEY_TILES * i, None)
            return c

        lax.fori_loop(0, n_full, step, 0)
        left = near - KEY_TILES * n_full

        @pl.when((left > 0) & (left <= 2))
        def _():
            chunk(KEY_TILES * n_full, near * QB, n_tiles=2)

        @pl.when(left > 2)
        def _():
            chunk(KEY_TILES * n_full, near * QB)

        @pl.when(qb < 2)
        def _():
            chunk(0, row_g, n_tiles=2)

        @pl.when(qb >= 2)
        def _():
            chunk(near, row_g)

        for lb in range(blocks):
            dq2 = jnp.where(head0, dqacc[2 * lb], dqacc[2 * lb + 1]) * scale
            dq_ref[:, lb * LANE:(lb + 1) * LANE] = dq2.astype(dq_ref.dtype)

        @pl.when(qb == nqb - 1)
        def _():
            dk_ref[...] = dkacc[...].astype(dk_ref.dtype)
            dv_ref[...] = dvacc[...].astype(dv_ref.dtype)

        @pl.when((pl.program_id(0) == ngrp - 1) & (qb == nqb - 1))
        def _():
            for f in waits:
                f()

    out = jax.ShapeDtypeStruct((m, d_model), BF16)
    wide = blocks * LANE
    carry = pltpu.VMEM((2 * blocks, QB, LANE), F32)
    outs = pl.pallas_call(
        body, name="attn_bwd", grid=(ngrp, nqb),
        in_specs=[pl.BlockSpec((QB, wide), lambda g, qb: (qb, qo + g)),
                  pl.BlockSpec((m, wide), lambda g, qb: (0, ko + g)),
                  pl.BlockSpec((m, wide), lambda g, qb: (0, vo + g)),
                  pl.BlockSpec((QB, wide), lambda g, qb: (qb, g)),
                  pl.BlockSpec((QB, 2 * wide), lambda g, qb: (qb, g))] + [ANY] * n_sl,
        out_specs=[pl.BlockSpec((QB, wide), lambda g, qb: (qb, g)),
                   pl.BlockSpec((m, wide), lambda g, qb: (0, g)),
                   pl.BlockSpec((m, wide), lambda g, qb: (0, g))] + [ANY] * n_sl,
        out_shape=[out, out, out] + _received_shapes(slabs),
        scratch_shapes=[pltpu.VMEM((m, wide), F32), pltpu.VMEM((m, wide), F32), carry, carry, carry] + _exchange_sems(n_sl),
        compiler_params=_params(2))(p, p, p, d_o, tot, *slabs)
    return outs[0], outs[1], outs[2], outs[3:]


def _mesh_pos():
    return lax.axis_index("x"), lax.axis_index("y"), lax.axis_index("c")


def _other_chips(x, y):
    return [(1 - x, y), (x, 1 - y), (1 - x, 1 - y)]


def _dev(x, y, c):
    return 4 * x + 2 * y + c


def _all_gather(shards):
    n = len(shards)

    def body(*refs):
        starts, relays, waits = _gather_copies(refs[:n], refs[n:2 * n], *refs[2 * n:])
        for f in starts + relays + waits:
            f()

    return pl.pallas_call(
        body, name="comm_all_gather", in_specs=[ANY] * n, out_specs=[ANY] * n,
        out_shape=[jax.ShapeDtypeStruct((N_DEV,) + s.shape, s.dtype) for s in shards],
        scratch_shapes=[pltpu.SemaphoreType.DMA((n, 7)), pltpu.SemaphoreType.DMA((n, 7)), pltpu.SemaphoreType.DMA((n,))],
    )(*shards)


def _peers(x, y, c):
    out = []
    for mask in range(1, N_DEV):
        px, py, pc = x ^ (mask >> 2), y ^ ((mask >> 1) & 1), c ^ (mask & 1)
        out.append((mask - 1, (px, py, pc), _dev(px, py, pc)))
    return out


def _remote(src, dst, send_sems, recv_sems, k, s, peer):
    return pltpu.make_async_remote_copy(src_ref=src, dst_ref=dst, send_sem=send_sems.at[k, s], recv_sem=recv_sems.at[k, s],
                                        device_id=peer, device_id_type=MESH)


def _gather_copies(ins, outs, send_sems, recv_sems, local_sems):
    x, y, c = _mesh_pos()
    sibling = (x, y, 1 - c)
    chips = _other_chips(x, y)
    starts, relays, waits = [], [], []
    for k in range(len(ins)):
        def slot(block, k=k):
            return outs[k].at[_dev(*block)]

        def copy(s, src, block, to, k=k):
            return _remote(src, slot(block), send_sems, recv_sems, k, s, to)

        own = pltpu.make_async_copy(ins[k], slot((x, y, c)), local_sems.at[k])
        to_sibling = copy(0, ins[k], (x, y, c), sibling)
        starts += [own.start, to_sibling.start]
        waits += [own.wait, to_sibling.wait_send, copy(0, ins[k], (x, y, 1 - c), sibling).wait_recv]
        for j, chip in enumerate(chips):
            out = copy(1 + j, ins[k], (x, y, c), (*chip, c))
            relay = copy(4 + j, slot((*chip, c)), (*chip, c), sibling)
            starts.append(out.start)
            relays += [copy(1 + j, ins[k], (*chip, c), sibling).wait_recv, relay.start]
            waits += [out.wait_send, relay.wait_send, copy(4 + j, ins[k], (*chip, 1 - c), sibling).wait_recv]
    return starts, relays, waits


def _scatter_copies(ins, outs, send_sems, recv_sems):
    x, y, c = _mesh_pos()
    starts, waits = [], []
    for k in range(len(ins)):
        for s, peer, idx in _peers(x, y, c):
            send = _remote(ins[k].at[idx], outs[k].at[s], send_sems, recv_sems, k, s, peer)
            starts.append(send.start)
            waits += [send.wait_recv, send.wait_send]
    return starts, waits


def _exchange_sems(n):
    return [pltpu.SemaphoreType.DMA((n, N_DEV - 1)), pltpu.SemaphoreType.DMA((n, N_DEV - 1))]


def _received_shapes(slabs):
    return [jax.ShapeDtypeStruct((N_DEV - 1,) + a.shape[1:], a.dtype) for a in slabs]


def _chips_and_own(x, y):
    return _other_chips(x, y) + [(x, y)]


def _sibling_exchange(slabs):
    n = len(slabs)

    def body(*refs):
        ins, outs, (send_sems, recv_sems) = refs[:n], refs[n:2 * n], refs[2 * n:]
        x, y, c = _mesh_pos()
        copies = [_remote(ins[k].at[_dev(*chip, 1 - c)], outs[k].at[r], send_sems, recv_sems, k, r, (x, y, 1 - c))
                  for k in range(n) for r, chip in enumerate(_chips_and_own(x, y))]
        for cp in copies:
            cp.start()
        for cp in copies:
            cp.wait_recv()
        for cp in copies:
            cp.wait_send()

    return pl.pallas_call(body, name="comm_rs_sibling", in_specs=[ANY] * n, out_specs=[ANY] * n,
                          out_shape=[jax.ShapeDtypeStruct((4,) + a.shape[1:], a.dtype) for a in slabs],
                          scratch_shapes=[pltpu.SemaphoreType.DMA((n, 4)), pltpu.SemaphoreType.DMA((n, 4))])(*slabs)


def _chip_copies(ins, outs, send_sems, recv_sems):
    x, y, c = _mesh_pos()
    starts, waits = [], []
    for k in range(len(ins)):
        for r, chip in enumerate(_other_chips(x, y)):
            cp = _remote(ins[k].at[r], outs[k].at[r], send_sems, recv_sems, k, r, (*chip, c))
            starts.append(cp.start)
            waits += [cp.wait_recv, cp.wait_send]
    return starts, [], waits


def _pair_sum(slab_idx, grad, from_sibling):
    _, rows, cols = grad.shape
    tr = _shard_tile(rows)

    def body(idx_ref, g_ref, s_ref, o_ref):
        o_ref[...] = (g_ref[...] + s_ref[...].astype(F32)).astype(o_ref.dtype)

    gs = pltpu.PrefetchScalarGridSpec(
        num_scalar_prefetch=1, grid=(3, rows // tr),
        in_specs=[pl.BlockSpec((None, tr, cols), lambda r, i, idx: (idx[r], i, 0)),
                  pl.BlockSpec((None, tr, cols), lambda r, i, idx: (r, i, 0))],
        out_specs=pl.BlockSpec((None, tr, cols), lambda r, i, idx: (r, i, 0)))
    return pl.pallas_call(body, name="rs_pair_sum", grid_spec=gs, out_shape=jax.ShapeDtypeStruct((3, rows, cols), BF16),
                          compiler_params=_params(2))(slab_idx, grad, from_sibling)


def _shard_tile(rows):
    for tr in range(min(rows, 352), 0, -1):
        if rows % tr == 0 and (tr % 16 == 0 or tr == rows):
            return tr


def _adamw_math(w, g, m, v):
    m = ADAM_B1 * m + (1.0 - ADAM_B1) * g
    v = ADAM_B2 * v + (1.0 - ADAM_B2) * (g * g)
    m_hat = m / (1.0 - ADAM_B1 ** ADAM_STEP)
    v_hat = v / (1.0 - ADAM_B2 ** ADAM_STEP)
    delta = -ADAM_LR * (m_hat / (jnp.sqrt(v_hat) + ADAM_EPS) + ADAM_WD * w)
    return delta, m, v


def _adamw_shard(me, grad, received, w, m, v):
    rows, cols = w.shape
    tr = _shard_tile(rows)
    n_rec = len(received)

    def body(me_ref, g_ref, *rest):
        r_refs, (w_ref, m_ref, v_ref, go_ref, do_ref, mo_ref, vo_ref) = rest[:n_rec], rest[n_rec:]
        g = g_ref[...]
        for r_ref in r_refs:
            for s in range(r_ref.shape[0]):
                g = g + r_ref[s].astype(F32)
        delta, m_new, v_new = _adamw_math(w_ref[...], g, m_ref[...], v_ref[...])
        go_ref[...] = g
        do_ref[...] = delta
        mo_ref[...] = m_new
        vo_ref[...] = v_new

    flat = pl.BlockSpec((tr, cols), lambda i, me: (i, 0))
    gs = pltpu.PrefetchScalarGridSpec(
        num_scalar_prefetch=1, grid=(rows // tr,),
        in_specs=[pl.BlockSpec((None, tr, cols), lambda i, me: (me[0], i, 0))]
        + [pl.BlockSpec((r.shape[0], tr, cols), lambda i, me: (0, i, 0)) for r in received] + [flat, flat, flat],
        out_specs=[flat, flat, flat, flat])
    out = jax.ShapeDtypeStruct((rows, cols), F32)
    return pl.pallas_call(body, name="adamw_shard", grid_spec=gs, out_shape=(out, out, out, out),
                          compiler_params=_params(1))(me, grad, *received, w, m, v)


def _small_reduce_adamw(slabs, w, m, v):
    _, rows, _ = slabs.shape

    def body(s_ref, w_ref, m_ref, v_ref, g_ref, d_ref, mo_ref, vo_ref, land, send_sems, recv_sems):
        x, y, c = _mesh_pos()
        me = _dev(x, y, c)
        copies = []
        for mask in range(1, N_DEV):
            px, py, pc = x ^ (mask >> 2), y ^ ((mask >> 1) & 1), c ^ (mask & 1)
            copies.append(pltpu.make_async_remote_copy(
                src_ref=s_ref.at[_dev(px, py, pc)], dst_ref=land.at[me], send_sem=send_sems.at[mask - 1],
                recv_sem=recv_sems.at[mask - 1], device_id=(px, py, pc), device_id_type=MESH))
        for cp in copies:
            cp.start()
        land[me] = s_ref[me]
        for mask in range(1, N_DEV):
            px, py, pc = x ^ (mask >> 2), y ^ ((mask >> 1) & 1), c ^ (mask & 1)
            pltpu.make_async_remote_copy(
                src_ref=s_ref.at[me], dst_ref=land.at[_dev(px, py, pc)], send_sem=send_sems.at[mask - 1],
                recv_sem=recv_sems.at[mask - 1], device_id=(px, py, pc), device_id_type=MESH).wait_recv()
        for cp in copies:
            cp.wait_send()
        g = land[0]
        for d in range(1, N_DEV):
            g = g + land[d]
        delta, m_new, v_new = _adamw_math(w_ref[...], g, m_ref[...], v_ref[...])
        g_ref[...] = g
        d_ref[...] = delta
        mo_ref[...] = m_new
        vo_ref[...] = v_new

    out = jax.ShapeDtypeStruct((rows, LANE), F32)
    return pl.pallas_call(
        body, name="comm_small_reduce_adamw", in_specs=[VMEM_WHOLE] * 4, out_specs=[VMEM_WHOLE] * 4, out_shape=(out, out, out, out),
        scratch_shapes=[pltpu.VMEM((N_DEV, rows, LANE), F32), pltpu.SemaphoreType.DMA((N_DEV - 1,)),
                        pltpu.SemaphoreType.DMA((N_DEV - 1,))],
    )(slabs, w, m, v)


def _cast_bf16(arrs):
    n = len(arrs)

    def body(*refs):
        for i_ref, o_ref in zip(refs[:n], refs[n:]):
            o_ref[...] = i_ref[...].astype(BF16)

    return pl.pallas_call(body, name="cast_bf16", in_specs=[VMEM_WHOLE] * n, out_specs=[VMEM_WHOLE] * n,
                          out_shape=[jax.ShapeDtypeStruct(a.shape, BF16) for a in arrs],
                          compiler_params=pltpu.CompilerParams(vmem_limit_bytes=VMEM_LIMIT))(*arrs)


REPLICATED = ("pre_mix_g", "gate_b", "dw_b", "conv_ln_g", "conv_ln_b", "post_mix_g", "pre_ffn_g", "post_ffn_g")
SHARDED = ("w_in", "w_conv_out", "w_attn_out", "w_o", "w_ffn_in", "w_ffn_out")
WEIGHTS = ("meta_tokens", "pre_mix_g", "w_in", "gate_b", "dw_w", "dw_b", "conv_ln_g", "conv_ln_b", "w_conv_out",
           "w_attn_out", "w_o", "post_mix_g", "pre_ffn_g", "w_ffn_in", "w_ffn_out", "post_ffn_g")


def kernel(x, meta_tokens, pre_mix_g, w_in, gate_b, dw_w, dw_b, conv_ln_g, conv_ln_b, w_conv_out, w_attn_out, w_o, post_mix_g, pre_ffn_g, w_ffn_in, w_ffn_out, post_ffn_g, loss_target, m_meta_tokens, m_pre_mix_g, m_w_in, m_gate_b, m_dw_w, m_dw_b, m_conv_ln_g, m_conv_ln_b, m_w_conv_out, m_w_attn_out, m_w_o, m_post_mix_g, m_pre_ffn_g, m_w_ffn_in, m_w_ffn_out, m_post_ffn_g, v_meta_tokens, v_pre_mix_g, v_w_in, v_gate_b, v_dw_w, v_dw_b, v_conv_ln_g, v_conv_ln_b, v_w_conv_out, v_w_attn_out, v_w_o, v_post_mix_g, v_pre_ffn_g, v_w_ffn_in, v_w_ffn_out, v_post_ffn_g):
    given = dict(locals())
    seq, d = x.shape[1], x.shape[2]
    n_meta = meta_tokens.shape[0]
    length = n_meta + seq
    m_rows = -(-length // QB) * QB
    dc = d // N_DEV
    assert dc == LANE and n_meta % 8 == 0 and seq % 8 == 0
    fs = w_ffn_in.shape[2]
    fr = w_ffn_out.shape[1]
    assert 2 * fr == fs

    transposed = ("w_ffn_in",)

    def shard(name):
        return given[name][0].T if name.endswith(transposed) else given[name][0]

    local = {k: shard(k) for k in SHARDED}
    cast = _cast_bf16([local[k] for k in SHARDED])
    dww_pad = jnp.pad(dw_w[0], ((0, CONV_PAD - CONV_WIDTH), (0, 0)))
    wi, meta_g, dww_g = _all_gather([cast[0], meta_tokens, dww_pad])
    meta_full = jnp.concatenate([meta_g[j] for j in range(N_DEV)], axis=1)
    dww_full = jnp.concatenate([dww_g[j] for j in range(N_DEV)], axis=1)
    ns = wi.shape[2]

    tail = jnp.zeros((m_rows - length, d), F32)
    h0 = jnp.concatenate([meta_full, x[0], tail], axis=0)
    target = jnp.concatenate([jnp.zeros((n_meta, d), F32), loss_target[0], tail], axis=0)

    (u,) = _rows("pre_mix_norm", lambda r0, xs, ps: ([_rms(xs[0], ps[0])], []), [h0], [pre_mix_g], [BF16], [])
    square = list(cast[1:4])
    p, p16, *gathered = _matmul(
        "in_proj", NN, u, wi, pl.BlockSpec((m_rows, d), lambda i: (0, 0)), pl.BlockSpec((None, d, ns), lambda i: (i, 0, 0)),
        pl.BlockSpec((m_rows, ns), lambda i: (0, i)), jax.ShapeDtypeStruct((m_rows, N_DEV * ns), F32), (N_DEV,), twin_bf16=True,
        carried=(_gather_copies, square, [jax.ShapeDtypeStruct((N_DEV,) + s.shape, s.dtype) for s in square],
                 _exchange_sems(len(square)) + [pltpu.SemaphoreType.DMA((len(square),))]))
    wco, wao, wo = (g.reshape(d, d) for g in gathered)
    o, tot, (wfi_t, wfo) = _attn_fwd(p16, d, list(cast[4:6]))
    wfo = wfo.reshape(N_DEV // 2, fs, d)
    y = _conv_fwd(p, dww_full, dw_b, d)
    (yc,) = _rows("conv_norm", lambda r0, xs, ps: ([_ln_silu(xs[0], ps[0], ps[1])], []), [y], [conv_ln_g, conv_ln_b], [BF16], [])
    y_conv = _dense_fwd("conv_out", yc, wco)
    y_attn, mixin = _attn_out_gate(o, wao, p, y_conv, gate_b)
    mix = _dense_fwd("mix_out", mixin, wo)
    h1, u2 = _rows("post_mix", lambda r0, xs, ps: (list(_post_mix(xs[0], xs[1], ps[0], ps[1])), []), [h0, mix],
                   [post_mix_g, pre_ffn_g], [F32, BF16], [])
    half = N_DEV // 2
    a_act, b_act, f_in = _ffn_in_swiglu(u2, wfi_t)
    f = _matmul("ffn_out", NN, f_in, wfo, pl.BlockSpec((None, m_rows, fs), lambda j: (j, 0, 0)), pl.BlockSpec((None, fs, d), lambda j: (j, 0, 0)),
                pl.BlockSpec((m_rows, d), lambda j: (0, 0)), jax.ShapeDtypeStruct((m_rows, d), F32), (half,), acc_axis=0)

    def loss_head(r0, xs, ps):
        h1_, f_, t_ = xs
        r, vjp = jax.vjp(_rms, f_, ps[0])
        rows = r0 + lax.broadcasted_iota(I32, (h1_.shape[0], 1), 0)
        real = (rows >= n_meta) & (rows < length)
        err = jnp.where(real, h1_ + r - t_, 0.0)
        dh2 = err * (1.0 / d)
        d_f, dg = vjp(dh2)
        part = jnp.sum(0.5 * jnp.mean(err * err, axis=-1, keepdims=True), axis=0, keepdims=True)
        return [d_f, dh2], [dg, jnp.broadcast_to(part, (1, LANE))]

    d_f, dh2, g_post_ffn, loss_part = _rows("loss_head", loss_head, [h1, f, target], [post_ffn_g], [BF16, F32], [d, LANE])

    d_ab = _ffn_out_dx_swiglu(d_f, wfo, a_act, b_act).reshape(N_DEV, m_rows, fs)
    g_wfo = _matmul("ffn_out_dw", TN, f_in, d_f, pl.BlockSpec((None, m_rows, fs), lambda j: (j, 0, 0)), pl.BlockSpec((m_rows, d), lambda j: (0, 0)),
                    pl.BlockSpec((None, fs, d), lambda j: (j, 0, 0)), jax.ShapeDtypeStruct((half, fs, d), F32), (half,), twin_bf16=True)

    du2 = _matmul("ffn_in_dx", NN, d_ab, wfi_t, pl.BlockSpec((None, m_rows, fs), lambda i: (i, 0, 0)), pl.BlockSpec((None, fs, d), lambda i: (i, 0, 0)),
                  pl.BlockSpec((m_rows, d), lambda i: (0, 0)), jax.ShapeDtypeStruct((m_rows, d), F32), (N_DEV,), acc_axis=0)
    g_wfi = _matmul("ffn_in_dw", TN, d_ab, u2, pl.BlockSpec((None, m_rows, fs), lambda i: (i, 0, 0)), pl.BlockSpec((m_rows, d), lambda i: (0, 0)),
                    pl.BlockSpec((None, fs, d), lambda i: (i, 0, 0)), jax.ShapeDtypeStruct((N_DEV, fs, d), F32), (N_DEV,), twin_bf16=True)

    def post_mix_bwd(r0, xs, ps):
        h0_, mix_, dh2_, du2_ = xs
        _, vjp = jax.vjp(_post_mix, h0_, mix_, ps[0], ps[1])
        dh0_, dmix_, dg1, dg2 = vjp((dh2_, du2_))
        return [dmix_, dh0_], [dg1, dg2]

    d_mix, dh1, g_post_mix, g_pre_ffn = _rows("post_mix_bwd", post_mix_bwd, [h0, mix, dh2, du2], [post_mix_g, pre_ffn_g],
                                              [BF16, F32], [d, d])
    g_wo = _dense_dw("mix_out_dw", mixin, d_mix)
    dp_gc, dp_ga, d_yconv, d_yattn, g_gb_c, g_gb_a = _mix_out_dx_gate(d_mix, wo, p, y_conv, y_attn, gate_b)
    g_gate_b = jnp.concatenate([g_gb_c, g_gb_a], axis=1)
    d_o = _dense_dx("attn_out_dx", d_yattn, wao, BF16)
    g_wao = _dense_dw("attn_out_dw", o, d_yattn)
    d_yc = _dense_dx("conv_out_dx", d_yconv, wco, F32)
    g_wco = _dense_dw("conv_out_dw", yc, d_yconv)
    big = {"w_ffn_out": [g.reshape(N_DEV, fr, d) for g in g_wfo], "w_ffn_in": g_wfi,
           "w_o": [g.reshape(N_DEV, dc, d) for g in g_wo], "w_attn_out": [g.reshape(N_DEV, dc, d) for g in g_wao],
           "w_conv_out": [g.reshape(N_DEV, dc, d) for g in g_wco]}
    early = ("w_ffn_out", "w_ffn_in", "w_o", "w_attn_out", "w_conv_out")
    dq, dk, dv, received_early = _attn_bwd(p16, d_o, tot, d, [big[k][1] for k in early])

    def conv_norm_bwd(r0, xs, ps):
        _, vjp = jax.vjp(_ln_silu, xs[0], ps[0], ps[1])
        dy_, dg, db = vjp(xs[1])
        return [dy_], [dg, db]

    d_y, g_ln_g, g_ln_b = _rows("conv_norm_bwd", conv_norm_bwd, [y, d_yc], [conv_ln_g, conv_ln_b], [F32], [d, d])
    dp_a, dp_g, g_dww, g_dwb = _conv_bwd(p, d_y, dww_full, d)
    dp = jnp.concatenate([dp_a, dp_g, dq, dk, dv, dp_gc, dp_ga], axis=1)
    g_wi = _matmul("in_proj_dw", TN, u, dp, pl.BlockSpec((m_rows, d), lambda i: (0, 0)), pl.BlockSpec((m_rows, ns), lambda i: (0, i)),
                   pl.BlockSpec((None, d, ns), lambda i: (i, 0, 0)), jax.ShapeDtypeStruct((N_DEV, d, ns), F32), (N_DEV,), twin_bf16=True)
    x_i, y_i, c_i = _mesh_pos()
    slab_idx = jnp.stack([_dev(*chip, c_i) for chip in _other_chips(x_i, y_i)]).astype(I32)
    (wi_sibling,) = _sibling_exchange([g_wi[1]])
    wi_pairs = _pair_sum(slab_idx, g_wi[0], wi_sibling)
    du, wi_chips = _matmul(
        "in_proj_dx", NT, dp, wi, pl.BlockSpec((m_rows, ns), lambda i: (0, i)), pl.BlockSpec((None, d, ns), lambda i: (i, 0, 0)),
        pl.BlockSpec((m_rows, d), lambda i: (0, 0)), jax.ShapeDtypeStruct((m_rows, d), F32), (N_DEV,), acc_axis=0,
        carried=(_chip_copies, [wi_pairs], [jax.ShapeDtypeStruct(wi_pairs.shape, BF16)],
                 [pltpu.SemaphoreType.DMA((1, 3)), pltpu.SemaphoreType.DMA((1, 3))]))

    def pre_mix_bwd(r0, xs, ps):
        _, vjp = jax.vjp(_rms, xs[0], ps[0])
        dx, dg = vjp(xs[1])
        return [xs[2] + dx], [dg]

    dh0, g_pre_mix = _rows("pre_mix_bwd", pre_mix_bwd, [h0, du, dh1], [pre_mix_g], [F32], [d])
    grad_x = dh0[n_meta:length][None]

    me = _dev(x_i, y_i, c_i)
    me_arr = jnp.reshape(me, (1,)).astype(I32)
    big["w_in"] = g_wi
    received = {k: [r] for k, r in zip(early, received_early)}
    received["w_in"] = [wi_sibling[3:4], wi_chips]
    results = {}
    for k in SHARDED:
        outs = _adamw_shard(me_arr, big[k][0], received[k], local[k], shard("m_" + k), shard("v_" + k))
        results[k] = tuple((a.T if k in transposed else a)[None] for a in outs)

    rep_grads = {"pre_mix_g": g_pre_mix, "gate_b": g_gate_b, "dw_b": g_dwb, "conv_ln_g": g_ln_g, "conv_ln_b": g_ln_b,
                 "post_mix_g": g_post_mix, "pre_ffn_g": g_pre_ffn, "post_ffn_g": g_post_ffn}

    def pack_rep(get):
        return jnp.concatenate([get(k) for k in REPLICATED], axis=1).reshape(-1, LANE)

    rep_rows = pack_rep(lambda k: rep_grads[k])
    n_rep = rep_rows.shape[0]
    loss_rows = jnp.broadcast_to(loss_part, (8, LANE))
    g_meta = dh0[0:n_meta]
    slabs = jnp.stack([jnp.concatenate([rep_rows, loss_rows, g_dww[:, j * LANE:(j + 1) * LANE], g_meta[:, j * LANE:(j + 1) * LANE]], axis=0)
                       for j in range(N_DEV)])

    def pack_small(prefix):
        dww_own = jnp.pad(given[prefix + "dw_w"][0], ((0, CONV_PAD - CONV_WIDTH), (0, 0)))
        return jnp.concatenate([pack_rep(lambda k: given[prefix + k]), jnp.zeros((8, LANE), F32), dww_own,
                                given[prefix + "meta_tokens"]], axis=0)

    small = _small_reduce_adamw(slabs, pack_small(""), pack_small("m_"), pack_small("v_"))
    loss = small[0][n_rep, 0]

    def unpack(arr):
        out = {}
        flat = arr[:n_rep].reshape(1, -1)
        off = 0
        for k in REPLICATED:
            w = given[k].shape[1]
            out[k] = flat[:, off:off + w]
            off += w
        out["dw_w"] = arr[n_rep + 8:n_rep + 8 + CONV_WIDTH][None]
        out["meta_tokens"] = arr[n_rep + 8 + CONV_PAD:n_rep + 8 + CONV_PAD + n_meta]
        return out

    small_out = [unpack(a) for a in small]
    for k in WEIGHTS:
        if k not in results:
            results[k] = tuple(s[k] for s in small_out)
    return (loss, grad_x, *[results[k][0] for k in WEIGHTS], *[results[k][1] for k in WEIGHTS],
            *[results[k][2] for k in WEIGHTS], *[results[k][3] for k in WEIGHTS])
```

```python
import jax
import jax.numpy as jnp
from jax import lax
from jax.experimental import pallas as pl
from jax.experimental.pallas import tpu as pltpu

F32 = jnp.float32
BF16 = jnp.bfloat16
I32 = jnp.int32

N_DEV = 8
LANE = 128
HEAD_DIM = 64
QB = 128
KEY_SHIFT = 2
KEY_TILES = 1 << KEY_SHIFT
KEY_CHUNK = KEY_TILES * QB
LANE_BLOCKS = 4
BWD_LANE_BLOCKS = 4
CONV_WIDTH = 31
CONV_PAD = 32
ROW_CHUNK = 128
RMS_EPS = 1e-6
LN_EPS = 1e-5
ADAM_LR = 0.001
ADAM_B1 = 0.9
ADAM_B2 = 0.999
ADAM_EPS = 1e-08
ADAM_WD = 0.01
ADAM_STEP = 10
VMEM_LIMIT = 56 * 1024 * 1024

NN = (((1,), (0,)), ((), ()))
NT = (((1,), (1,)), ((), ()))
TN = (((0,), (0,)), ((), ()))
MESH = pl.DeviceIdType.MESH
ANY = pl.BlockSpec(memory_space=pl.ANY)
VMEM_WHOLE = pl.BlockSpec(memory_space=pltpu.VMEM)


def _params(n_axes):
    return pltpu.CompilerParams(dimension_semantics=("arbitrary",) * n_axes, vmem_limit_bytes=VMEM_LIMIT)


def _row_tile(m):
    assert m % QB == 0
    return m // 8 if m % 128 == 0 else QB


def _matmul(name, dims, a, b, a_spec, b_spec, o_spec, out_shape, grid, acc_axis=None, twin_bf16=False, carried=None):
    n_car = 0 if carried is None else len(carried[1])
    n_twin = 1 if twin_bf16 else 0

    def body(a_ref, b_ref, *rest):
        car_ins, o_ref, twin, rest = rest[:n_car], rest[n_car], rest[n_car + 1:n_car + 1 + n_twin], rest[n_car + 1 + n_twin:]
        if carried is not None:
            starts, relays, waits = carried[0](car_ins, rest[:n_car], *rest[n_car:])

            @pl.when(pl.program_id(0) == 0)
            def _():
                for f in starts:
                    f()

            if relays:
                @pl.when(pl.program_id(0) == (3 * grid[0]) // 4)
                def _():
                    for f in relays:
                        f()

        r = lax.dot_general(a_ref[...], b_ref[...], dims, preferred_element_type=F32)
        if acc_axis is None:
            o_ref[...] = r.astype(o_ref.dtype)
            for t_ref in twin:
                t_ref[...] = r.astype(BF16)
        else:
            k = pl.program_id(acc_axis)

            @pl.when(k == 0)
            def _():
                o_ref[...] = r

            @pl.when(k > 0)
            def _():
                o_ref[...] += r

        if carried is not None:
            @pl.when(pl.program_id(0) == grid[0] - 1)
            def _():
                for f in waits:
                    f()

    in_specs, out_specs, out_shapes, scratch = [a_spec, b_spec], [o_spec], [out_shape], []
    if twin_bf16:
        assert acc_axis is None
        out_specs.append(o_spec)
        out_shapes.append(jax.ShapeDtypeStruct(out_shape.shape, BF16))
    if carried is not None:
        assert len(grid) == 1
        in_specs += [ANY] * n_car
        out_specs += [ANY] * n_car
        out_shapes += list(carried[2])
        scratch = list(carried[3])
    outs = pl.pallas_call(body, name=name, grid=grid, in_specs=in_specs, out_specs=out_specs, out_shape=out_shapes,
                          scratch_shapes=scratch, compiler_params=_params(len(grid)))(a, b, *(carried[1] if carried else ()))
    return outs[0] if len(outs) == 1 else outs


def _ffn_rows(m):
    return m // 2 if m % 32 == 0 else m


def _ffn_in_swiglu(u2, wfi_t):
    m, d = u2.shape
    half, fs = N_DEV // 2, wfi_t.shape[1]
    tm = _ffn_rows(m)

    def body(u_ref, wa_ref, wb_ref, a_ref, b_ref, f_ref):
        u = u_ref[...]
        a = lax.dot_general(u, wa_ref[...], NT, preferred_element_type=F32)
        b = lax.dot_general(u, wb_ref[...], NT, preferred_element_type=F32)
        a_ref[...] = a.astype(BF16)
        b_ref[...] = b.astype(BF16)
        f_ref[...] = _swiglu(a, b).astype(BF16)

    out = jax.ShapeDtypeStruct((half, m, fs), BF16)
    act = pl.BlockSpec((None, tm, fs), lambda j, i: (j, i, 0))
    return pl.pallas_call(
        body, name="ffn_in_swiglu", grid=(half, m // tm),
        in_specs=[pl.BlockSpec((tm, d), lambda j, i: (i, 0)), pl.BlockSpec((None, fs, d), lambda j, i: (j, 0, 0)),
                  pl.BlockSpec((None, fs, d), lambda j, i: (j + half, 0, 0))],
        out_specs=[act, act, act], out_shape=[out, out, out], compiler_params=_params(2))(u2, wfi_t, wfi_t)


def _ffn_out_dx_swiglu(d_f, wfo, a_act, b_act):
    m, d = d_f.shape
    half, fs = N_DEV // 2, wfo.shape[1]
    tm = _ffn_rows(m)

    def body(df_ref, w_ref, a_ref, b_ref, o_ref):
        d_fin = lax.dot_general(df_ref[...], w_ref[...], NT, preferred_element_type=F32)
        _, vjp = jax.vjp(_swiglu, a_ref[...].astype(F32), b_ref[...].astype(F32))
        d_a, d_b = vjp(d_fin)
        o_ref[0] = d_a.astype(BF16)
        o_ref[1] = d_b.astype(BF16)

    act = pl.BlockSpec((None, tm, fs), lambda j, i: (j, i, 0))
    return pl.pallas_call(
        body, name="ffn_out_dx_swiglu", grid=(half, m // tm),
        in_specs=[pl.BlockSpec((tm, d), lambda j, i: (i, 0)), pl.BlockSpec((None, fs, d), lambda j, i: (j, 0, 0)), act, act],
        out_specs=pl.BlockSpec((2, None, tm, fs), lambda j, i: (0, j, i, 0)),
        out_shape=jax.ShapeDtypeStruct((2, half, m, fs), BF16), compiler_params=_params(2))(d_f, wfo, a_act, b_act)


DENSE_TILE = 256


def _dense_fwd(name, a, w, out_dtype=F32):
    m, k = a.shape
    n = w.shape[1]
    tn = DENSE_TILE
    return _matmul(name, NN, a, w, pl.BlockSpec((m, k), lambda j: (0, 0)), pl.BlockSpec((k, tn), lambda j: (0, j)),
                   pl.BlockSpec((m, tn), lambda j: (0, j)), jax.ShapeDtypeStruct((m, n), out_dtype), (n // tn,))


def _dense_dx(name, dy, w, out_dtype):
    m, n = dy.shape
    k = w.shape[0]
    tk = DENSE_TILE
    return _matmul(name, NT, dy, w, pl.BlockSpec((m, n), lambda j: (0, 0)), pl.BlockSpec((tk, n), lambda j: (j, 0)),
                   pl.BlockSpec((m, tk), lambda j: (0, j)), jax.ShapeDtypeStruct((m, k), out_dtype), (k // tk,))


def _dense_dw(name, a, dy):
    m, k = a.shape
    n = dy.shape[1]
    tn = DENSE_TILE
    return _matmul(name, TN, a, dy, pl.BlockSpec((m, k), lambda j: (0, 0)), pl.BlockSpec((m, tn), lambda j: (0, j)),
                   pl.BlockSpec((k, tn), lambda j: (0, j)), jax.ShapeDtypeStruct((k, n), F32), (n // tn,), twin_bf16=True)


def _gate_specs(m, d, n_row_ins):
    t = DENSE_TILE
    per = d // t
    cols = lambda base: pl.BlockSpec((m, t), lambda j, base=base: (0, base * per + j))
    bias = lambda base: pl.BlockSpec((1, t), lambda j, base=base: (0, base * per + j))
    return [cols(5), cols(6)] + [cols(0)] * n_row_ins + [bias(0), bias(1)]


def _attn_out_gate(o, wao, p, y_conv, gate_b):
    m, d = o.shape
    t = DENSE_TILE

    def body(o_ref, w_ref, pgc_ref, pga_ref, yc_ref, bc_ref, ba_ref, ya_ref, mix_ref):
        ya = jnp.dot(o_ref[...], w_ref[...], preferred_element_type=F32)
        ya_ref[...] = ya
        gb = jnp.concatenate([bc_ref[...], ba_ref[...]], axis=1)
        mix_ref[...] = _gate_mix(pgc_ref[...], pga_ref[...], yc_ref[...], ya, gb).astype(mix_ref.dtype)

    tile = pl.BlockSpec((m, t), lambda j: (0, j))
    return pl.pallas_call(
        body, name="attn_out_gate", grid=(d // t,),
        in_specs=[pl.BlockSpec((m, d), lambda j: (0, 0)), pl.BlockSpec((d, t), lambda j: (0, j))] + _gate_specs(m, d, 1),
        out_specs=[tile, tile], out_shape=[jax.ShapeDtypeStruct((m, d), F32), jax.ShapeDtypeStruct((m, d), BF16)],
        compiler_params=_params(1))(o, wao, p, p, y_conv, gate_b, gate_b)


def _mix_out_dx_gate(d_mix, wo, p, y_conv, y_attn, gate_b):
    m, d = d_mix.shape
    t = DENSE_TILE

    def body(dm_ref, w_ref, pgc_ref, pga_ref, yc_ref, ya_ref, bc_ref, ba_ref, dgc_ref, dga_ref, dyc_ref, dya_ref, dbc_ref, dba_ref):
        d_mixin = lax.dot_general(dm_ref[...], w_ref[...], NT, preferred_element_type=F32)
        gb = jnp.concatenate([bc_ref[...], ba_ref[...]], axis=1)
        _, vjp = jax.vjp(_gate_mix, pgc_ref[...], pga_ref[...], yc_ref[...], ya_ref[...], gb)
        dgc, dga, dyc, dya, dgb = vjp(d_mixin)
        dgc_ref[...] = dgc.astype(dgc_ref.dtype)
        dga_ref[...] = dga.astype(dga_ref.dtype)
        dyc_ref[...] = dyc.astype(dyc_ref.dtype)
        dya_ref[...] = dya.astype(dya_ref.dtype)
        dbc_ref[...] = dgb[:, :t]
        dba_ref[...] = dgb[:, t:]

    tile = pl.BlockSpec((m, t), lambda j: (0, j))
    vec = pl.BlockSpec((1, t), lambda j: (0, j))
    act = jax.ShapeDtypeStruct((m, d), BF16)
    par = jax.ShapeDtypeStruct((1, d), F32)
    return pl.pallas_call(
        body, name="mix_out_dx_gate", grid=(d // t,),
        in_specs=[pl.BlockSpec((m, d), lambda j: (0, 0)), pl.BlockSpec((t, d), lambda j: (j, 0))] + _gate_specs(m, d, 2),
        out_specs=[tile] * 4 + [vec, vec], out_shape=[act] * 4 + [par, par],
        compiler_params=_params(1))(d_mix, wo, p, p, y_conv, y_attn, gate_b, gate_b)


def _rowwise(name, fn, row_ins, par_ins, row_outs, par_outs, *, grid, in_specs, out_specs, tm, row_axis):
    n_ri, n_pi, n_ro, n_po = len(row_ins), len(par_ins), len(row_outs), len(par_outs)
    n_steps, tail = divmod(tm, ROW_CHUNK)
    assert tail % 16 == 0

    def body(*refs):
        ri = refs[:n_ri]
        pi = refs[n_ri:n_ri + n_pi]
        ro = refs[n_ri + n_pi:n_ri + n_pi + n_ro]
        po = refs[n_ri + n_pi + n_ro:]
        ps = [r[...] for r in pi]
        base = pl.program_id(row_axis) * tm

        def chunk(r0, rows, carry):
            xs = [r[pl.ds(r0, rows), :] for r in ri]
            outs, pouts = fn(base + r0, xs, ps)
            for r, o in zip(ro, outs):
                if isinstance(o, (list, tuple)):
                    for j, part in enumerate(o):
                        r[j, pl.ds(r0, rows), :] = part.astype(r.dtype)
                else:
                    r[pl.ds(r0, rows), :] = o.astype(r.dtype)
            return tuple(c + q for c, q in zip(carry, pouts))

        def step(i, carry):
            return chunk(pl.multiple_of(i * ROW_CHUNK, ROW_CHUNK), ROW_CHUNK, carry)

        acc = tuple(jnp.zeros(s.shape, F32) for s in par_outs)
        if n_steps:
            acc = lax.fori_loop(0, n_steps, step, acc)
        if tail:
            acc = chunk(n_steps * ROW_CHUNK, tail, acc)
        if n_po:
            first = pl.program_id(0) == 0
            for ax in range(1, len(grid)):
                first = first & (pl.program_id(ax) == 0)

            @pl.when(first)
            def _():
                for r in po:
                    r[...] = jnp.zeros_like(r)

            for r, a in zip(po, acc):
                r[...] += a

    return pl.pallas_call(body, name=name, grid=grid, in_specs=in_specs, out_specs=out_specs,
                          out_shape=tuple(row_outs) + tuple(par_outs),
                          compiler_params=_params(len(grid)))(*row_ins, *par_ins)


def _rows(name, fn, row_ins, par_ins, row_out_dtypes, par_out_widths):
    m = row_ins[0].shape[0]
    tm = _row_tile(m)
    in_specs = [pl.BlockSpec((tm, a.shape[1]), lambda i: (i, 0)) for a in row_ins]
    in_specs += [pl.BlockSpec(a.shape, lambda i: (0, 0)) for a in par_ins]
    row_outs = [jax.ShapeDtypeStruct((m, row_ins[0].shape[1]), dt) for dt in row_out_dtypes]
    par_outs = [jax.ShapeDtypeStruct((1, w), F32) for w in par_out_widths]
    out_specs = [pl.BlockSpec((tm, s.shape[1]), lambda i: (i, 0)) for s in row_outs]
    out_specs += [pl.BlockSpec(s.shape, lambda i: (0, 0)) for s in par_outs]
    return _rowwise(name, fn, row_ins, par_ins, row_outs, par_outs, grid=(m // tm,), in_specs=in_specs,
                    out_specs=out_specs, tm=tm, row_axis=0)


def _rms(x, g):
    return x * lax.rsqrt(jnp.mean(x * x, axis=-1, keepdims=True) + RMS_EPS) * g


def _ln_silu(y, g, b):
    mu = jnp.mean(y, axis=-1, keepdims=True)
    yc = y - mu
    var = jnp.mean(yc * yc, axis=-1, keepdims=True)
    return jax.nn.silu(yc * lax.rsqrt(var + LN_EPS) * g + b)


def _gate_mix(pgc, pga, yc, ya, gb):
    d = pgc.shape[1]
    return jax.nn.sigmoid(pgc + gb[:, :d]) * yc + jax.nn.sigmoid(pga + gb[:, d:]) * ya


def _post_mix(h0, mix, g_post, g_pre):
    h1 = h0 + _rms(mix, g_post)
    return h1, _rms(h1, g_pre)


def _swiglu(a, b):
    return jax.nn.silu(a) * b


def _conv_taps():
    taps = []
    for b in range(8):
        for a in range(CONV_PAD // 8):
            s = 8 * a + b
            if s < CONV_WIDTH:
                taps.append((b, a, CONV_WIDTH - 1 - s))
    return taps


def _conv_fwd(p, dww, dwb, d_model):
    m = p.shape[0]
    nch = d_model // LANE
    n_chunk = m // QB
    taps = _conv_taps()

    def body(a_ref, g_ref, w_ref, b_ref, y_ref, upad):
        upad[0:CONV_PAD, :] = jnp.zeros((CONV_PAD, LANE), F32)

        def fill(i, c):
            r0 = pl.multiple_of(i * QB, QB)
            u = a_ref[pl.ds(r0, QB), :] * jax.nn.sigmoid(g_ref[pl.ds(r0, QB), :])
            upad[pl.ds(pl.multiple_of(r0 + CONV_PAD, 8), QB), :] = u
            return c

        lax.fori_loop(0, n_chunk, fill, 0)

        def conv(i, c):
            r0 = pl.multiple_of(i * QB, QB)
            win = upad[pl.ds(r0, QB + CONV_PAD), :]
            acc = jnp.broadcast_to(b_ref[...], (QB, LANE))
            rolled = {}
            for b, a, j in taps:
                if b not in rolled:
                    rolled[b] = win if b == 0 else pltpu.roll(win, b, axis=0)
                lo = CONV_PAD - 8 * a
                acc = acc + w_ref[j:j + 1, :] * rolled[b][lo:lo + QB, :]
            y_ref[pl.ds(r0, QB), :] = acc
            return c

        lax.fori_loop(0, n_chunk, conv, 0)

    col = lambda off: pl.BlockSpec((m, LANE), lambda c: (0, off + c))
    return pl.pallas_call(
        body, name="conv_fwd", grid=(nch,),
        in_specs=[col(0), col(nch), pl.BlockSpec((CONV_PAD, LANE), lambda c: (0, c)), pl.BlockSpec((1, LANE), lambda c: (0, c))],
        out_specs=col(0), out_shape=jax.ShapeDtypeStruct((m, d_model), F32),
        scratch_shapes=[pltpu.VMEM((m + CONV_PAD, LANE), F32)], compiler_params=_params(1))(p, p, dww, dwb)


def _conv_bwd(p, dy, dww, d_model):
    m = p.shape[0]
    nch = d_model // LANE
    n_chunk = m // QB
    taps = _conv_taps()
    win_rows = QB + CONV_PAD

    def body(a_ref, g_ref, dy_ref, w_ref, da_ref, dg_ref, dw_ref, db_ref, upad, dypad, wacc, bacc):
        upad[0:CONV_PAD, :] = jnp.zeros((CONV_PAD, LANE), F32)
        dypad[m:m + CONV_PAD, :] = jnp.zeros((CONV_PAD, LANE), F32)
        wacc[...] = jnp.zeros_like(wacc)
        bacc[...] = jnp.zeros_like(bacc)

        def fill(i, c):
            r0 = pl.multiple_of(i * QB, QB)
            u = a_ref[pl.ds(r0, QB), :] * jax.nn.sigmoid(g_ref[pl.ds(r0, QB), :])
            upad[pl.ds(pl.multiple_of(r0 + CONV_PAD, 8), QB), :] = u
            dypad[pl.ds(r0, QB), :] = dy_ref[pl.ds(r0, QB), :]
            return c

        lax.fori_loop(0, n_chunk, fill, 0)

        def chunk(i, c):
            r0 = pl.multiple_of(i * QB, QB)
            dwin = dypad[pl.ds(r0, win_rows), :]
            du = jnp.zeros((QB, LANE), F32)
            rolled = {}
            for b, a, j in taps:
                if b not in rolled:
                    rolled[b] = dwin if b == 0 else pltpu.roll(dwin, win_rows - b, axis=0)
                du = du + w_ref[j:j + 1, :] * rolled[b][8 * a:8 * a + QB, :]
            av = a_ref[pl.ds(r0, QB), :]
            sg = jax.nn.sigmoid(g_ref[pl.ds(r0, QB), :])
            da_ref[pl.ds(r0, QB), :] = (du * sg).astype(da_ref.dtype)
            dg_ref[pl.ds(r0, QB), :] = (du * av * sg * (1.0 - sg)).astype(dg_ref.dtype)
            dyc = dy_ref[pl.ds(r0, QB), :]
            uwin = upad[pl.ds(r0, win_rows), :]
            rolled = {}
            for b, a, j in taps:
                if b not in rolled:
                    rolled[b] = uwin if b == 0 else pltpu.roll(uwin, b, axis=0)
                lo = CONV_PAD - 8 * a
                prod = dyc * rolled[b][lo:lo + QB, :]
                wacc[j] += prod.reshape(QB // 8, 8, LANE).sum(axis=0)
            bacc[...] += dyc.reshape(QB // 8, 8, LANE).sum(axis=0)
            return c

        lax.fori_loop(0, n_chunk, chunk, 0)
        for j in range(CONV_WIDTH):
            dw_ref[j:j + 1, :] = jnp.sum(wacc[j], axis=0, keepdims=True)
        dw_ref[CONV_WIDTH:CONV_PAD, :] = jnp.zeros((CONV_PAD - CONV_WIDTH, LANE), F32)
        db_ref[...] = jnp.sum(bacc[...], axis=0, keepdims=True)

    col = lambda off: pl.BlockSpec((m, LANE), lambda c: (0, off + c))
    return pl.pallas_call(
        body, name="conv_bwd", grid=(nch,),
        in_specs=[col(0), col(nch), col(0), pl.BlockSpec((CONV_PAD, LANE), lambda c: (0, c))],
        out_specs=[col(0), col(0), pl.BlockSpec((CONV_PAD, LANE), lambda c: (0, c)), pl.BlockSpec((1, LANE), lambda c: (0, c))],
        out_shape=(jax.ShapeDtypeStruct((m, d_model), BF16), jax.ShapeDtypeStruct((m, d_model), BF16),
                   jax.ShapeDtypeStruct((CONV_PAD, d_model), F32), jax.ShapeDtypeStruct((1, d_model), F32)),
        scratch_shapes=[pltpu.VMEM((m + CONV_PAD, LANE), F32), pltpu.VMEM((m + CONV_PAD, LANE), F32),
                        pltpu.VMEM((CONV_PAD, 8, LANE), F32), pltpu.VMEM((8, LANE), F32)],
        compiler_params=_params(1))(p, p, dy, dww)


EXP_CLAMP = 80.0


def _one_plus_exp(z):
    return 1.0 + jnp.exp(jnp.minimum(z, EXP_CLAMP))


def _softplus(z):
    return jnp.maximum(jnp.log(_one_plus_exp(z)), z)


def _softplus_sigmoid(z):
    s = _one_plus_exp(z)
    return jnp.maximum(jnp.log(s), z), 1.0 - 1.0 / s


def _tile_cumsums(x, tri2):
    xb = x.astype(BF16)
    out = []
    for i in range(0, x.shape[1] // QB, 2):
        both = jnp.dot(xb[:, i * QB:(i + 2) * QB], tri2, preferred_element_type=F32)
        out += [both[:, :QB], both[:, QB:]]
    return out


def _tri2(kind):
    jj = lax.broadcasted_iota(I32, (2 * QB, 2 * QB), 0)
    ss = lax.broadcasted_iota(I32, (2 * QB, 2 * QB), 1)
    same = (jj >= QB) == (ss >= QB)
    keep = {"ge": jj >= ss, "le": jj <= ss}[kind]
    return jnp.where(same & keep, 1.0, 0.0).astype(BF16)


def _attn_fwd(p, d_model, shards):
    m = p.shape[0]
    nqb = m // QB
    ngrp = d_model // (LANE_BLOCKS * LANE)
    qo, ko, vo = 2 * ngrp, 3 * ngrp, 4 * ngrp
    scale = HEAD_DIM ** -0.5
    n_sh = len(shards)

    assert nqb >= KEY_TILES

    def body(q_ref, k_ref, v_ref, *rest):
        shard_refs, (o_ref, t_ref), rest = rest[:n_sh], rest[n_sh:n_sh + 2], rest[n_sh + 2:]
        gathered_refs, (acc_ref, car_ref), sems = rest[:n_sh], rest[n_sh:n_sh + 2], rest[n_sh + 2:]
        starts, relays, waits = _gather_copies(shard_refs, gathered_refs, *sems)

        @pl.when((pl.program_id(0) == 0) & (pl.program_id(1) == 0))
        def _():
            for f in starts:
                f()

        @pl.when((pl.program_id(0) == ngrp - 1) & (pl.program_id(1) == (5 * nqb) // 8))
        def _():
            for f in relays:
                f()

        qb = pl.program_id(1)
        lane = lax.broadcasted_iota(I32, (QB, LANE), 1)
        head0 = lane < HEAD_DIM
        row_g = qb * QB + lax.broadcasted_iota(I32, (QB, KEY_CHUNK), 0)
        col_l = lax.broadcasted_iota(I32, (QB, KEY_CHUNK), 1)
        tri = _tri2("ge")
        heads = range(2 * LANE_BLOCKS)
        qh = []
        for lb in range(LANE_BLOCKS):
            q2 = (q_ref[:, lb * LANE:(lb + 1) * LANE] * scale).astype(BF16)
            zero = jnp.zeros_like(q2)
            qh += [jnp.where(head0, q2, zero), jnp.where(head0, zero, q2)]
        acc_ref[...] = jnp.zeros_like(acc_ref)
        car_ref[...] = jnp.zeros_like(car_ref)

        def chunk(first_tile, bound, n_tiles=KEY_TILES):
            r0 = pl.multiple_of(first_tile * QB, QB)
            keys = n_tiles * QB
            kcs = [k_ref[pl.ds(r0, keys), lb * LANE:(lb + 1) * LANE].astype(BF16) for lb in range(LANE_BLOCKS)]
            vcs = [v_ref[pl.ds(r0, keys), lb * LANE:(lb + 1) * LANE].astype(BF16) for lb in range(LANE_BLOCKS)]
            valid = None if bound is None else (col_l[:, :keys] + r0) < bound
            zs = [lax.dot_general(qh[h], kcs[h // 2], NT, preferred_element_type=F32) for h in heads]
            sps = [_softplus(z) for z in zs]
            if valid is not None:
                sps = [jnp.where(valid, sp, 0.0) for sp in sps]
            cums = [_tile_cumsums(sp, tri) for sp in sps]
            cars = [car_ref[h] for h in heads]
            a_tiles = [[None] * n_tiles for h in heads]
            for i in reversed(range(n_tiles)):
                for h in heads:
                    cum = cums[h][i]
                    a_tiles[h][i] = jnp.exp(zs[h][:, i * QB:(i + 1) * QB] - (cum + cars[h]))
                    cars[h] = cars[h] + jnp.broadcast_to(cum[:, 0:1], cum.shape)
            for h in heads:
                a = jnp.concatenate(a_tiles[h], axis=1)
                if valid is not None:
                    a = jnp.where(valid, a, 0.0)
                acc_ref[h] += jnp.dot(a.astype(BF16), vcs[h // 2], preferred_element_type=F32)
                car_ref[h] = cars[h]

        near = jnp.maximum(qb - (KEY_TILES - 1), 0)
        chunk(near, row_g)
        n_full = lax.shift_right_logical(near, KEY_SHIFT)

        def step(i, c):
            chunk(near - KEY_TILES * (i + 1), None)
            return c

        lax.fori_loop(0, n_full, step, 0)
        left = near - KEY_TILES * n_full

        @pl.when((left > 0) & (left <= 2))
        def _():
            chunk(0, left * QB, n_tiles=2)

        @pl.when(left > 2)
        def _():
            chunk(0, left * QB)

        for lb in range(LANE_BLOCKS):
            o_ref[:, lb * LANE:(lb + 1) * LANE] = jnp.where(head0, acc_ref[2 * lb], acc_ref[2 * lb + 1]).astype(o_ref.dtype)
        for h in heads:
            t_ref[:, h * QB:(h + 1) * QB] = car_ref[h]

        @pl.when((pl.program_id(0) == ngrp - 1) & (pl.program_id(1) == nqb - 1))
        def _():
            for f in waits:
                f()

    wide = LANE_BLOCKS * LANE
    outs = pl.pallas_call(
        body, name="attn_fwd", grid=(ngrp, nqb),
        in_specs=[pl.BlockSpec((QB, wide), lambda g, qb: (qb, qo + g)),
                  pl.BlockSpec((m, wide), lambda g, qb: (0, ko + g)),
                  pl.BlockSpec((m, wide), lambda g, qb: (0, vo + g))] + [ANY] * n_sh,
        out_specs=[pl.BlockSpec((QB, wide), lambda g, qb: (qb, g)),
                   pl.BlockSpec((QB, 2 * wide), lambda g, qb: (qb, g))] + [ANY] * n_sh,
        out_shape=[jax.ShapeDtypeStruct((m, d_model), BF16), jax.ShapeDtypeStruct((m, 2 * d_model), F32)]
        + [jax.ShapeDtypeStruct((N_DEV,) + s.shape, s.dtype) for s in shards],
        scratch_shapes=[pltpu.VMEM((2 * LANE_BLOCKS, QB, LANE), F32), pltpu.VMEM((2 * LANE_BLOCKS, QB, LANE), F32)]
        + _exchange_sems(n_sh) + [pltpu.SemaphoreType.DMA((n_sh,))],
        compiler_params=_params(2))(p, p, p, *shards)
    return outs[0], outs[1], outs[2:]


def _attn_bwd(p, d_o, tot, d_model, slabs):
    m = p.shape[0]
    nqb = m // QB
    blocks = BWD_LANE_BLOCKS
    ngrp = d_model // (blocks * LANE)
    qo, ko, vo = 2 * ngrp, 3 * ngrp, 4 * ngrp
    scale = HEAD_DIM ** -0.5
    n_sl = len(slabs)

    assert nqb >= KEY_TILES

    def body(q_ref, k_ref, v_ref, do_ref, t_ref, *rest):
        slab_refs, (dq_ref, dk_ref, dv_ref), rest = rest[:n_sl], rest[n_sl:n_sl + 3], rest[n_sl + 3:]
        recv_refs, (dkacc, dvacc, dqacc, csp, cg), sems = rest[:n_sl], rest[n_sl:n_sl + 5], rest[n_sl + 5:]
        starts, waits = _scatter_copies(slab_refs, recv_refs, *sems)

        @pl.when((pl.program_id(0) == 0) & (pl.program_id(1) == 0))
        def _():
            for f in starts:
                f()

        qb = pl.program_id(1)
        lane = lax.broadcasted_iota(I32, (QB, LANE), 1)
        head0 = lane < HEAD_DIM
        row_g = qb * QB + lax.broadcasted_iota(I32, (QB, KEY_CHUNK), 0)
        col_l = lax.broadcasted_iota(I32, (QB, KEY_CHUNK), 1)
        tri_ge = _tri2("ge")
        tri_le = _tri2("le")
        heads = range(2 * blocks)
        qh, doh = [], []
        for lb in range(blocks):
            q2 = (q_ref[:, lb * LANE:(lb + 1) * LANE] * scale).astype(BF16)
            do2 = do_ref[:, lb * LANE:(lb + 1) * LANE]
            zero = jnp.zeros_like(q2)
            qh += [jnp.where(head0, q2, zero), jnp.where(head0, zero, q2)]
            doh += [jnp.where(head0, do2, zero), jnp.where(head0, zero, do2)]
        q_pairs = [jnp.concatenate(qh[2 * lb:2 * lb + 2], axis=0) for lb in range(blocks)]
        do_pairs = [jnp.concatenate(doh[2 * lb:2 * lb + 2], axis=0) for lb in range(blocks)]

        @pl.when(qb == 0)
        def _():
            dkacc[...] = jnp.zeros_like(dkacc)
            dvacc[...] = jnp.zeros_like(dvacc)

        dqacc[...] = jnp.zeros_like(dqacc)
        for h in heads:
            csp[h] = t_ref[:, h * QB:(h + 1) * QB]
        cg[...] = jnp.zeros_like(cg)

        def chunk(first_tile, bound, n_tiles=KEY_TILES):
            r0 = pl.multiple_of(first_tile * QB, QB)
            keys = n_tiles * QB
            kcs = [k_ref[pl.ds(r0, keys), lb * LANE:(lb + 1) * LANE].astype(BF16) for lb in range(blocks)]
            vcs = [v_ref[pl.ds(r0, keys), lb * LANE:(lb + 1) * LANE].astype(BF16) for lb in range(blocks)]
            valid = None if bound is None else (col_l[:, :keys] + r0) < bound
            tiles = [slice(i * QB, (i + 1) * QB) for i in range(n_tiles)]
            zs = [lax.dot_general(qh[h], kcs[h // 2], NT, preferred_element_type=F32) for h in heads]
            das = [lax.dot_general(doh[h], vcs[h // 2], NT, preferred_element_type=F32) for h in heads]
            sps, sgs = zip(*[_softplus_sigmoid(z) for z in zs])
            if valid is not None:
                sps = [jnp.where(valid, sp, 0.0) for sp in sps]
            cums = [_tile_cumsums(sp, tri_ge) for sp in sps]
            a_tiles, g_tiles = [[] for h in heads], [[] for h in heads]
            for h in heads:
                rest = csp[h]
                for i, c in enumerate(tiles):
                    cum = cums[h][i]
                    rest = rest - jnp.broadcast_to(cum[:, 0:1], cum.shape)
                    a = jnp.exp(zs[h][:, c] - (cum + rest))
                    if valid is not None:
                        a = jnp.where(valid[:, c], a, 0.0)
                    a_tiles[h].append(a)
                    g_tiles[h].append(a * das[h][:, c])
                csp[h] = rest
            gcums = [_tile_cumsums(jnp.concatenate(g_tiles[h], axis=1), tri_le) for h in heads]
            dzbs, abs_ = [], []
            for h in heads:
                g_before = cg[h]
                dz_tiles = []
                for i, c in enumerate(tiles):
                    gcum = gcums[h][i]
                    dz = g_tiles[h][i] - sgs[h][:, c] * (g_before + gcum)
                    if valid is not None:
                        dz = jnp.where(valid[:, c], dz, 0.0)
                    dz_tiles.append(dz)
                    g_before = g_before + jnp.broadcast_to(gcum[:, QB - 1:QB], gcum.shape)
                cg[h] = g_before
                dzbs.append(jnp.concatenate(dz_tiles, axis=1).astype(BF16))
                abs_.append(jnp.concatenate(a_tiles[h], axis=1).astype(BF16))
            for h in heads:
                dqacc[h] += jnp.dot(dzbs[h], kcs[h // 2], preferred_element_type=F32)
            for lb in range(blocks):
                dz_pair = jnp.concatenate(dzbs[2 * lb:2 * lb + 2], axis=0)
                a_pair = jnp.concatenate(abs_[2 * lb:2 * lb + 2], axis=0)
                dkacc[pl.ds(r0, keys), lb * LANE:(lb + 1) * LANE] += lax.dot_general(
                    dz_pair, q_pairs[lb], TN, preferred_element_type=F32)
                dvacc[pl.ds(r0, keys), lb * LANE:(lb + 1) * LANE] += lax.dot_general(
                    a_pair, do_pairs[lb], TN, preferred_element_type=F32)

        near = jnp.maximum(qb - (KEY_TILES - 1), 0)
        n_full = lax.shift_right_logical(near, KEY_SHIFT)

        def step(i, c):
            chunk(KEY_TILES * i, None)
            return c

        lax.fori_loop(0, n_full, step, 0)
        left = near - KEY_TILES * n_full

        @pl.when((left > 0) & (left <= 2))
        def _():
            chunk(KEY_TILES * n_full, near * QB, n_tiles=2)

        @pl.when(left > 2)
        def _():
            chunk(KEY_TILES * n_full, near * QB)

        chunk(near, row_g)
        for lb in range(blocks):
            dq2 = jnp.where(head0, dqacc[2 * lb], dqacc[2 * lb + 1]) * scale
            dq_ref[:, lb * LANE:(lb + 1) * LANE] = dq2.astype(dq_ref.dtype)

        @pl.when(qb == nqb - 1)
        def _():
            dk_ref[...] = dkacc[...].astype(dk_ref.dtype)
            dv_ref[...] = dvacc[...].astype(dv_ref.dtype)

        @pl.when((pl.program_id(0) == ngrp - 1) & (qb == nqb - 1))
        def _():
            for f in waits:
                f()

    out = jax.ShapeDtypeStruct((m, d_model), BF16)
    wide = blocks * LANE
    carry = pltpu.VMEM((2 * blocks, QB, LANE), F32)
    outs = pl.pallas_call(
        body, name="attn_bwd", grid=(ngrp, nqb),
        in_specs=[pl.BlockSpec((QB, wide), lambda g, qb: (qb, qo + g)),
                  pl.BlockSpec((m, wide), lambda g, qb: (0, ko + g)),
                  pl.BlockSpec((m, wide), lambda g, qb: (0, vo + g)),
                  pl.BlockSpec((QB, wide), lambda g, qb: (qb, g)),
                  pl.BlockSpec((QB, 2 * wide), lambda g, qb: (qb, g))] + [ANY] * n_sl,
        out_specs=[pl.BlockSpec((QB, wide), lambda g, qb: (qb, g)),
                   pl.BlockSpec((m, wide), lambda g, qb: (0, g)),
                   pl.BlockSpec((m, wide), lambda g, qb: (0, g))] + [ANY] * n_sl,
        out_shape=[out, out, out] + _received_shapes(slabs),
        scratch_shapes=[pltpu.VMEM((m, wide), F32), pltpu.VMEM((m, wide), F32), carry, carry, carry] + _exchange_sems(n_sl),
        compiler_params=_params(2))(p, p, p, d_o, tot, *slabs)
    return outs[0], outs[1], outs[2], outs[3:]


def _mesh_pos():
    return lax.axis_index("x"), lax.axis_index("y"), lax.axis_index("c")


def _other_chips(x, y):
    return [(1 - x, y), (x, 1 - y), (1 - x, 1 - y)]


def _dev(x, y, c):
    return 4 * x + 2 * y + c


def _all_gather(shards):
    n = len(shards)

    def body(*refs):
        starts, relays, waits = _gather_copies(refs[:n], refs[n:2 * n], *refs[2 * n:])
        for f in starts + relays + waits:
            f()

    return pl.pallas_call(
        body, name="comm_all_gather", in_specs=[ANY] * n, out_specs=[ANY] * n,
        out_shape=[jax.ShapeDtypeStruct((N_DEV,) + s.shape, s.dtype) for s in shards],
        scratch_shapes=[pltpu.SemaphoreType.DMA((n, 7)), pltpu.SemaphoreType.DMA((n, 7)), pltpu.SemaphoreType.DMA((n,))],
    )(*shards)


def _peers(x, y, c):
    out = []
    for mask in range(1, N_DEV):
        px, py, pc = x ^ (mask >> 2), y ^ ((mask >> 1) & 1), c ^ (mask & 1)
        out.append((mask - 1, (px, py, pc), _dev(px, py, pc)))
    return out


def _remote(src, dst, send_sems, recv_sems, k, s, peer):
    return pltpu.make_async_remote_copy(src_ref=src, dst_ref=dst, send_sem=send_sems.at[k, s], recv_sem=recv_sems.at[k, s],
                                        device_id=peer, device_id_type=MESH)


def _gather_copies(ins, outs, send_sems, recv_sems, local_sems):
    x, y, c = _mesh_pos()
    sibling = (x, y, 1 - c)
    chips = _other_chips(x, y)
    starts, relays, waits = [], [], []
    for k in range(len(ins)):
        def slot(block, k=k):
            return outs[k].at[_dev(*block)]

        def copy(s, src, block, to, k=k):
            return _remote(src, slot(block), send_sems, recv_sems, k, s, to)

        own = pltpu.make_async_copy(ins[k], slot((x, y, c)), local_sems.at[k])
        to_sibling = copy(0, ins[k], (x, y, c), sibling)
        starts += [own.start, to_sibling.start]
        waits += [own.wait, to_sibling.wait_send, copy(0, ins[k], (x, y, 1 - c), sibling).wait_recv]
        for j, chip in enumerate(chips):
            out = copy(1 + j, ins[k], (x, y, c), (*chip, c))
            relay = copy(4 + j, slot((*chip, c)), (*chip, c), sibling)
            starts.append(out.start)
            relays += [copy(1 + j, ins[k], (*chip, c), sibling).wait_recv, relay.start]
            waits += [out.wait_send, relay.wait_send, copy(4 + j, ins[k], (*chip, 1 - c), sibling).wait_recv]
    return starts, relays, waits


def _scatter_copies(ins, outs, send_sems, recv_sems):
    x, y, c = _mesh_pos()
    starts, waits = [], []
    for k in range(len(ins)):
        for s, peer, idx in _peers(x, y, c):
            send = _remote(ins[k].at[idx], outs[k].at[s], send_sems, recv_sems, k, s, peer)
            starts.append(send.start)
            waits += [send.wait_recv, send.wait_send]
    return starts, waits


def _exchange_sems(n):
    return [pltpu.SemaphoreType.DMA((n, N_DEV - 1)), pltpu.SemaphoreType.DMA((n, N_DEV - 1))]


def _received_shapes(slabs):
    return [jax.ShapeDtypeStruct((N_DEV - 1,) + a.shape[1:], a.dtype) for a in slabs]


def _chips_and_own(x, y):
    return _other_chips(x, y) + [(x, y)]


def _sibling_exchange(slabs):
    n = len(slabs)

    def body(*refs):
        ins, outs, (send_sems, recv_sems) = refs[:n], refs[n:2 * n], refs[2 * n:]
        x, y, c = _mesh_pos()
        copies = [_remote(ins[k].at[_dev(*chip, 1 - c)], outs[k].at[r], send_sems, recv_sems, k, r, (x, y, 1 - c))
                  for k in range(n) for r, chip in enumerate(_chips_and_own(x, y))]
        for cp in copies:
            cp.start()
        for cp in copies:
            cp.wait_recv()
        for cp in copies:
            cp.wait_send()

    return pl.pallas_call(body, name="comm_rs_sibling", in_specs=[ANY] * n, out_specs=[ANY] * n,
                          out_shape=[jax.ShapeDtypeStruct((4,) + a.shape[1:], a.dtype) for a in slabs],
                          scratch_shapes=[pltpu.SemaphoreType.DMA((n, 4)), pltpu.SemaphoreType.DMA((n, 4))])(*slabs)


def _chip_copies(ins, outs, send_sems, recv_sems):
    x, y, c = _mesh_pos()
    starts, waits = [], []
    for k in range(len(ins)):
        for r, chip in enumerate(_other_chips(x, y)):
            cp = _remote(ins[k].at[r], outs[k].at[r], send_sems, recv_sems, k, r, (*chip, c))
            starts.append(cp.start)
            waits += [cp.wait_recv, cp.wait_send]
    return starts, [], waits


def _pair_sum(slab_idx, grad, from_sibling):
    _, rows, cols = grad.shape
    tr = _shard_tile(rows)

    def body(idx_ref, g_ref, s_ref, o_ref):
        o_ref[...] = (g_ref[...] + s_ref[...].astype(F32)).astype(o_ref.dtype)

    gs = pltpu.PrefetchScalarGridSpec(
        num_scalar_prefetch=1, grid=(3, rows // tr),
        in_specs=[pl.BlockSpec((None, tr, cols), lambda r, i, idx: (idx[r], i, 0)),
                  pl.BlockSpec((None, tr, cols), lambda r, i, idx: (r, i, 0))],
        out_specs=pl.BlockSpec((None, tr, cols), lambda r, i, idx: (r, i, 0)))
    return pl.pallas_call(body, name="rs_pair_sum", grid_spec=gs, out_shape=jax.ShapeDtypeStruct((3, rows, cols), BF16),
                          compiler_params=_params(2))(slab_idx, grad, from_sibling)


def _shard_tile(rows):
    for tr in range(min(rows, 352), 0, -1):
        if rows % tr == 0 and (tr % 16 == 0 or tr == rows):
            return tr


def _adamw_math(w, g, m, v):
    m = ADAM_B1 * m + (1.0 - ADAM_B1) * g
    v = ADAM_B2 * v + (1.0 - ADAM_B2) * (g * g)
    m_hat = m / (1.0 - ADAM_B1 ** ADAM_STEP)
    v_hat = v / (1.0 - ADAM_B2 ** ADAM_STEP)
    delta = -ADAM_LR * (m_hat / (jnp.sqrt(v_hat) + ADAM_EPS) + ADAM_WD * w)
    return delta, m, v


def _adamw_shard(me, grad, received, w, m, v):
    rows, cols = w.shape
    tr = _shard_tile(rows)
    n_rec = len(received)

    def body(me_ref, g_ref, *rest):
        r_refs, (w_ref, m_ref, v_ref, go_ref, do_ref, mo_ref, vo_ref) = rest[:n_rec], rest[n_rec:]
        g = g_ref[...]
        for r_ref in r_refs:
            for s in range(r_ref.shape[0]):
                g = g + r_ref[s].astype(F32)
        delta, m_new, v_new = _adamw_math(w_ref[...], g, m_ref[...], v_ref[...])
        go_ref[...] = g
        do_ref[...] = delta
        mo_ref[...] = m_new
        vo_ref[...] = v_new

    flat = pl.BlockSpec((tr, cols), lambda i, me: (i, 0))
    gs = pltpu.PrefetchScalarGridSpec(
        num_scalar_prefetch=1, grid=(rows // tr,),
        in_specs=[pl.BlockSpec((None, tr, cols), lambda i, me: (me[0], i, 0))]
        + [pl.BlockSpec((r.shape[0], tr, cols), lambda i, me: (0, i, 0)) for r in received] + [flat, flat, flat],
        out_specs=[flat, flat, flat, flat])
    out = jax.ShapeDtypeStruct((rows, cols), F32)
    return pl.pallas_call(body, name="adamw_shard", grid_spec=gs, out_shape=(out, out, out, out),
                          compiler_params=_params(1))(me, grad, *received, w, m, v)


def _small_reduce_adamw(slabs, w, m, v):
    _, rows, _ = slabs.shape

    def body(s_ref, w_ref, m_ref, v_ref, g_ref, d_ref, mo_ref, vo_ref, land, send_sems, recv_sems):
        x, y, c = _mesh_pos()
        me = _dev(x, y, c)
        copies = []
        for mask in range(1, N_DEV):
            px, py, pc = x ^ (mask >> 2), y ^ ((mask >> 1) & 1), c ^ (mask & 1)
            copies.append(pltpu.make_async_remote_copy(
                src_ref=s_ref.at[_dev(px, py, pc)], dst_ref=land.at[me], send_sem=send_sems.at[mask - 1],
                recv_sem=recv_sems.at[mask - 1], device_id=(px, py, pc), device_id_type=MESH))
        for cp in copies:
            cp.start()
        land[me] = s_ref[me]
        for mask in range(1, N_DEV):
            px, py, pc = x ^ (mask >> 2), y ^ ((mask >> 1) & 1), c ^ (mask & 1)
            pltpu.make_async_remote_copy(
                src_ref=s_ref.at[me], dst_ref=land.at[_dev(px, py, pc)], send_sem=send_sems.at[mask - 1],
                recv_sem=recv_sems.at[mask - 1], device_id=(px, py, pc), device_id_type=MESH).wait_recv()
        for cp in copies:
            cp.wait_send()
        g = land[0]
        for d in range(1, N_DEV):
            g = g + land[d]
        delta, m_new, v_new = _adamw_math(w_ref[...], g, m_ref[...], v_ref[...])
        g_ref[...] = g
        d_ref[...] = delta
        mo_ref[...] = m_new
        vo_ref[...] = v_new

    out = jax.ShapeDtypeStruct((rows, LANE), F32)
    return pl.pallas_call(
        body, name="comm_small_reduce_adamw", in_specs=[VMEM_WHOLE] * 4, out_specs=[VMEM_WHOLE] * 4, out_shape=(out, out, out, out),
        scratch_shapes=[pltpu.VMEM((N_DEV, rows, LANE), F32), pltpu.SemaphoreType.DMA((N_DEV - 1,)),
                        pltpu.SemaphoreType.DMA((N_DEV - 1,))],
    )(slabs, w, m, v)


def _cast_bf16(arrs):
    n = len(arrs)

    def body(*refs):
        for i_ref, o_ref in zip(refs[:n], refs[n:]):
            o_ref[...] = i_ref[...].astype(BF16)

    return pl.pallas_call(body, name="cast_bf16", in_specs=[VMEM_WHOLE] * n, out_specs=[VMEM_WHOLE] * n,
                          out_shape=[jax.ShapeDtypeStruct(a.shape, BF16) for a in arrs],
                          compiler_params=pltpu.CompilerParams(vmem_limit_bytes=VMEM_LIMIT))(*arrs)


REPLICATED = ("pre_mix_g", "gate_b", "dw_b", "conv_ln_g", "conv_ln_b", "post_mix_g", "pre_ffn_g", "post_ffn_g")
SHARDED = ("w_in", "w_conv_out", "w_attn_out", "w_o", "w_ffn_in", "w_ffn_out")
WEIGHTS = ("meta_tokens", "pre_mix_g", "w_in", "gate_b", "dw_w", "dw_b", "conv_ln_g", "conv_ln_b", "w_conv_out",
           "w_attn_out", "w_o", "post_mix_g", "pre_ffn_g", "w_ffn_in", "w_ffn_out", "post_ffn_g")


def kernel(x, meta_tokens, pre_mix_g, w_in, gate_b, dw_w, dw_b, conv_ln_g, conv_ln_b, w_conv_out, w_attn_out, w_o, post_mix_g, pre_ffn_g, w_ffn_in, w_ffn_out, post_ffn_g, loss_target, m_meta_tokens, m_pre_mix_g, m_w_in, m_gate_b, m_dw_w, m_dw_b, m_conv_ln_g, m_conv_ln_b, m_w_conv_out, m_w_attn_out, m_w_o, m_post_mix_g, m_pre_ffn_g, m_w_ffn_in, m_w_ffn_out, m_post_ffn_g, v_meta_tokens, v_pre_mix_g, v_w_in, v_gate_b, v_dw_w, v_dw_b, v_conv_ln_g, v_conv_ln_b, v_w_conv_out, v_w_attn_out, v_w_o, v_post_mix_g, v_pre_ffn_g, v_w_ffn_in, v_w_ffn_out, v_post_ffn_g):
    given = dict(locals())
    seq, d = x.shape[1], x.shape[2]
    n_meta = meta_tokens.shape[0]
    length = n_meta + seq
    m_rows = -(-length // QB) * QB
    dc = d // N_DEV
    assert dc == LANE and n_meta % 8 == 0 and seq % 8 == 0
    fs = w_ffn_in.shape[2]
    fr = w_ffn_out.shape[1]
    assert 2 * fr == fs

    transposed = ("w_ffn_in",)

    def shard(name):
        return given[name][0].T if name.endswith(transposed) else given[name][0]

    local = {k: shard(k) for k in SHARDED}
    cast = _cast_bf16([local[k] for k in SHARDED])
    dww_pad = jnp.pad(dw_w[0], ((0, CONV_PAD - CONV_WIDTH), (0, 0)))
    wi, meta_g, dww_g = _all_gather([cast[0], meta_tokens, dww_pad])
    meta_full = jnp.concatenate([meta_g[j] for j in range(N_DEV)], axis=1)
    dww_full = jnp.concatenate([dww_g[j] for j in range(N_DEV)], axis=1)
    ns = wi.shape[2]

    tail = jnp.zeros((m_rows - length, d), F32)
    h0 = jnp.concatenate([meta_full, x[0], tail], axis=0)
    target = jnp.concatenate([jnp.zeros((n_meta, d), F32), loss_target[0], tail], axis=0)

    (u,) = _rows("pre_mix_norm", lambda r0, xs, ps: ([_rms(xs[0], ps[0])], []), [h0], [pre_mix_g], [BF16], [])
    square = list(cast[1:4])
    p, p16, *gathered = _matmul(
        "in_proj", NN, u, wi, pl.BlockSpec((m_rows, d), lambda i: (0, 0)), pl.BlockSpec((None, d, ns), lambda i: (i, 0, 0)),
        pl.BlockSpec((m_rows, ns), lambda i: (0, i)), jax.ShapeDtypeStruct((m_rows, N_DEV * ns), F32), (N_DEV,), twin_bf16=True,
        carried=(_gather_copies, square, [jax.ShapeDtypeStruct((N_DEV,) + s.shape, s.dtype) for s in square],
                 _exchange_sems(len(square)) + [pltpu.SemaphoreType.DMA((len(square),))]))
    wco, wao, wo = (g.reshape(d, d) for g in gathered)
    o, tot, (wfi_t, wfo) = _attn_fwd(p16, d, list(cast[4:6]))
    wfo = wfo.reshape(N_DEV // 2, fs, d)
    y = _conv_fwd(p, dww_full, dw_b, d)
    (yc,) = _rows("conv_norm", lambda r0, xs, ps: ([_ln_silu(xs[0], ps[0], ps[1])], []), [y], [conv_ln_g, conv_ln_b], [BF16], [])
    y_conv = _dense_fwd("conv_out", yc, wco)
    y_attn, mixin = _attn_out_gate(o, wao, p, y_conv, gate_b)
    mix = _dense_fwd("mix_out", mixin, wo)
    h1, u2 = _rows("post_mix", lambda r0, xs, ps: (list(_post_mix(xs[0], xs[1], ps[0], ps[1])), []), [h0, mix],
                   [post_mix_g, pre_ffn_g], [F32, BF16], [])
    half = N_DEV // 2
    a_act, b_act, f_in = _ffn_in_swiglu(u2, wfi_t)
    f = _matmul("ffn_out", NN, f_in, wfo, pl.BlockSpec((None, m_rows, fs), lambda j: (j, 0, 0)), pl.BlockSpec((None, fs, d), lambda j: (j, 0, 0)),
                pl.BlockSpec((m_rows, d), lambda j: (0, 0)), jax.ShapeDtypeStruct((m_rows, d), F32), (half,), acc_axis=0)

    def loss_head(r0, xs, ps):
        h1_, f_, t_ = xs
        r, vjp = jax.vjp(_rms, f_, ps[0])
        rows = r0 + lax.broadcasted_iota(I32, (h1_.shape[0], 1), 0)
        real = (rows >= n_meta) & (rows < length)
        err = jnp.where(real, h1_ + r - t_, 0.0)
        dh2 = err * (1.0 / d)
        d_f, dg = vjp(dh2)
        part = jnp.sum(0.5 * jnp.mean(err * err, axis=-1, keepdims=True), axis=0, keepdims=True)
        return [d_f, dh2], [dg, jnp.broadcast_to(part, (1, LANE))]

    d_f, dh2, g_post_ffn, loss_part = _rows("loss_head", loss_head, [h1, f, target], [post_ffn_g], [BF16, F32], [d, LANE])

    d_ab = _ffn_out_dx_swiglu(d_f, wfo, a_act, b_act).reshape(N_DEV, m_rows, fs)
    g_wfo = _matmul("ffn_out_dw", TN, f_in, d_f, pl.BlockSpec((None, m_rows, fs), lambda j: (j, 0, 0)), pl.BlockSpec((m_rows, d), lambda j: (0, 0)),
                    pl.BlockSpec((None, fs, d), lambda j: (j, 0, 0)), jax.ShapeDtypeStruct((half, fs, d), F32), (half,), twin_bf16=True)

    du2 = _matmul("ffn_in_dx", NN, d_ab, wfi_t, pl.BlockSpec((None, m_rows, fs), lambda i: (i, 0, 0)), pl.BlockSpec((None, fs, d), lambda i: (i, 0, 0)),
                  pl.BlockSpec((m_rows, d), lambda i: (0, 0)), jax.ShapeDtypeStruct((m_rows, d), F32), (N_DEV,), acc_axis=0)
    g_wfi = _matmul("ffn_in_dw", TN, d_ab, u2, pl.BlockSpec((None, m_rows, fs), lambda i: (i, 0, 0)), pl.BlockSpec((m_rows, d), lambda i: (0, 0)),
                    pl.BlockSpec((None, fs, d), lambda i: (i, 0, 0)), jax.ShapeDtypeStruct((N_DEV, fs, d), F32), (N_DEV,), twin_bf16=True)

    def post_mix_bwd(r0, xs, ps):
        h0_, mix_, dh2_, du2_ = xs
        _, vjp = jax.vjp(_post_mix, h0_, mix_, ps[0], ps[1])
        dh0_, dmix_, dg1, dg2 = vjp((dh2_, du2_))
        return [dmix_, dh0_], [dg1, dg2]

    d_mix, dh1, g_post_mix, g_pre_ffn = _rows("post_mix_bwd", post_mix_bwd, [h0, mix, dh2, du2], [post_mix_g, pre_ffn_g],
                                              [BF16, F32], [d, d])
    g_wo = _dense_dw("mix_out_dw", mixin, d_mix)
    dp_gc, dp_ga, d_yconv, d_yattn, g_gb_c, g_gb_a = _mix_out_dx_gate(d_mix, wo, p, y_conv, y_attn, gate_b)
    g_gate_b = jnp.concatenate([g_gb_c, g_gb_a], axis=1)
    d_o = _dense_dx("attn_out_dx", d_yattn, wao, BF16)
    g_wao = _dense_dw("attn_out_dw", o, d_yattn)
    d_yc = _dense_dx("conv_out_dx", d_yconv, wco, F32)
    g_wco = _dense_dw("conv_out_dw", yc, d_yconv)
    big = {"w_ffn_out": [g.reshape(N_DEV, fr, d) for g in g_wfo], "w_ffn_in": g_wfi,
           "w_o": [g.reshape(N_DEV, dc, d) for g in g_wo], "w_attn_out": [g.reshape(N_DEV, dc, d) for g in g_wao],
           "w_conv_out": [g.reshape(N_DEV, dc, d) for g in g_wco]}
    early = ("w_ffn_out", "w_ffn_in", "w_o", "w_attn_out", "w_conv_out")
    dq, dk, dv, received_early = _attn_bwd(p16, d_o, tot, d, [big[k][1] for k in early])

    def conv_norm_bwd(r0, xs, ps):
        _, vjp = jax.vjp(_ln_silu, xs[0], ps[0], ps[1])
        dy_, dg, db = vjp(xs[1])
        return [dy_], [dg, db]

    d_y, g_ln_g, g_ln_b = _rows("conv_norm_bwd", conv_norm_bwd, [y, d_yc], [conv_ln_g, conv_ln_b], [F32], [d, d])
    dp_a, dp_g, g_dww, g_dwb = _conv_bwd(p, d_y, dww_full, d)
    dp = jnp.concatenate([dp_a, dp_g, dq, dk, dv, dp_gc, dp_ga], axis=1)
    g_wi = _matmul("in_proj_dw", TN, u, dp, pl.BlockSpec((m_rows, d), lambda i: (0, 0)), pl.BlockSpec((m_rows, ns), lambda i: (0, i)),
                   pl.BlockSpec((None, d, ns), lambda i: (i, 0, 0)), jax.ShapeDtypeStruct((N_DEV, d, ns), F32), (N_DEV,), twin_bf16=True)
    x_i, y_i, c_i = _mesh_pos()
    slab_idx = jnp.stack([_dev(*chip, c_i) for chip in _other_chips(x_i, y_i)]).astype(I32)
    (wi_sibling,) = _sibling_exchange([g_wi[1]])
    wi_pairs = _pair_sum(slab_idx, g_wi[0], wi_sibling)
    du, wi_chips = _matmul(
        "in_proj_dx", NT, dp, wi, pl.BlockSpec((m_rows, ns), lambda i: (0, i)), pl.BlockSpec((None, d, ns), lambda i: (i, 0, 0)),
        pl.BlockSpec((m_rows, d), lambda i: (0, 0)), jax.ShapeDtypeStruct((m_rows, d), F32), (N_DEV,), acc_axis=0,
        carried=(_chip_copies, [wi_pairs], [jax.ShapeDtypeStruct(wi_pairs.shape, BF16)],
                 [pltpu.SemaphoreType.DMA((1, 3)), pltpu.SemaphoreType.DMA((1, 3))]))

    def pre_mix_bwd(r0, xs, ps):
        _, vjp = jax.vjp(_rms, xs[0], ps[0])
        dx, dg = vjp(xs[1])
        return [xs[2] + dx], [dg]

    dh0, g_pre_mix = _rows("pre_mix_bwd", pre_mix_bwd, [h0, du, dh1], [pre_mix_g], [F32], [d])
    grad_x = dh0[n_meta:length][None]

    me = _dev(x_i, y_i, c_i)
    me_arr = jnp.reshape(me, (1,)).astype(I32)
    big["w_in"] = g_wi
    received = {k: [r] for k, r in zip(early, received_early)}
    received["w_in"] = [wi_sibling[3:4], wi_chips]
    results = {}
    for k in SHARDED:
        outs = _adamw_shard(me_arr, big[k][0], received[k], local[k], shard("m_" + k), shard("v_" + k))
        results[k] = tuple((a.T if k in transposed else a)[None] for a in outs)

    rep_grads = {"pre_mix_g": g_pre_mix, "gate_b": g_gate_b, "dw_b": g_dwb, "conv_ln_g": g_ln_g, "conv_ln_b": g_ln_b,
                 "post_mix_g": g_post_mix, "pre_ffn_g": g_pre_ffn, "post_ffn_g": g_post_ffn}

    def pack_rep(get):
        return jnp.concatenate([get(k) for k in REPLICATED], axis=1).reshape(-1, LANE)

    rep_rows = pack_rep(lambda k: rep_grads[k])
    n_rep = rep_rows.shape[0]
    loss_rows = jnp.broadcast_to(loss_part, (8, LANE))
    g_meta = dh0[0:n_meta]
    slabs = jnp.stack([jnp.concatenate([rep_rows, loss_rows, g_dww[:, j * LANE:(j + 1) * LANE], g_meta[:, j * LANE:(j + 1) * LANE]], axis=0)
                       for j in range(N_DEV)])

    def pack_small(prefix):
        dww_own = jnp.pad(given[prefix + "dw_w"][0], ((0, CONV_PAD - CONV_WIDTH), (0, 0)))
        return jnp.concatenate([pack_rep(lambda k: given[prefix + k]), jnp.zeros((8, LANE), F32), dww_own,
                                given[prefix + "meta_tokens"]], axis=0)

    small = _small_reduce_adamw(slabs, pack_small(""), pack_small("m_"), pack_small("v_"))
    loss = small[0][n_rep, 0]

    def unpack(arr):
        out = {}
        flat = arr[:n_rep].reshape(1, -1)
        off = 0
        for k in REPLICATED:
            w = given[k].shape[1]
            out[k] = flat[:, off:off + w]
            off += w
        out["dw_w"] = arr[n_rep + 8:n_rep + 8 + CONV_WIDTH][None]
        out["meta_tokens"] = arr[n_rep + 8 + CONV_PAD:n_rep + 8 + CONV_PAD + n_meta]
        return out

    small_out = [unpack(a) for a in small]
    for k in WEIGHTS:
        if k not in results:
            results[k] = tuple(s[k] for s in small_out)
    return (loss, grad_x, *[results[k][0] for k in WEIGHTS], *[results[k][1] for k in WEIGHTS],
            *[results[k][2] for k in WEIGHTS], *[results[k][3] for k in WEIGHTS])
```

```python
import jax
import jax.numpy as jnp
from jax import lax
from jax.experimental import pallas as pl
from jax.experimental.pallas import tpu as pltpu

F32 = jnp.float32
BF16 = jnp.bfloat16
I32 = jnp.int32

N_DEV = 8
LANE = 128
HEAD_DIM = 64
QB = 128
KEY_SHIFT = 2
KEY_TILES = 1 << KEY_SHIFT
KEY_CHUNK = KEY_TILES * QB
LANE_BLOCKS = 4
BWD_LANE_BLOCKS = 4
CONV_WIDTH = 31
CONV_PAD = 32
ROW_CHUNK = 128
RMS_EPS = 1e-6
LN_EPS = 1e-5
ADAM_LR = 0.001
ADAM_B1 = 0.9
ADAM_B2 = 0.999
ADAM_EPS = 1e-08
ADAM_WD = 0.01
ADAM_STEP = 10
VMEM_LIMIT = 56 * 1024 * 1024

NN = (((1,), (0,)), ((), ()))
NT = (((1,), (1,)), ((), ()))
TN = (((0,), (0,)), ((), ()))
MESH = pl.DeviceIdType.MESH
ANY = pl.BlockSpec(memory_space=pl.ANY)
VMEM_WHOLE = pl.BlockSpec(memory_space=pltpu.VMEM)


def _params(n_axes):
    return pltpu.CompilerParams(dimension_semantics=("arbitrary",) * n_axes, vmem_limit_bytes=VMEM_LIMIT)


def _row_tile(m):
    assert m % QB == 0
    return m // 4 if m % 64 == 0 else QB


def _matmul(name, dims, a, b, a_spec, b_spec, o_spec, out_shape, grid, acc_axis=None, twin_bf16=False, carried=None):
    n_car = 0 if carried is None else len(carried[1])
    n_twin = 1 if twin_bf16 else 0

    def body(a_ref, b_ref, *rest):
        car_ins, o_ref, twin, rest = rest[:n_car], rest[n_car], rest[n_car + 1:n_car + 1 + n_twin], rest[n_car + 1 + n_twin:]
        if carried is not None:
            starts, relays, waits = carried[0](car_ins, rest[:n_car], *rest[n_car:])

            @pl.when(pl.program_id(0) == 0)
            def _():
                for f in starts:
                    f()

            if relays:
                @pl.when(pl.program_id(0) == (3 * grid[0]) // 4)
                def _():
                    for f in relays:
                        f()

        r = lax.dot_general(a_ref[...], b_ref[...], dims, preferred_element_type=F32)
        if acc_axis is None:
            o_ref[...] = r.astype(o_ref.dtype)
            for t_ref in twin:
                t_ref[...] = r.astype(BF16)
        else:
            k = pl.program_id(acc_axis)

            @pl.when(k == 0)
            def _():
                o_ref[...] = r

            @pl.when(k > 0)
            def _():
                o_ref[...] += r

        if carried is not None:
            @pl.when(pl.program_id(0) == grid[0] - 1)
            def _():
                for f in waits:
                    f()

    in_specs, out_specs, out_shapes, scratch = [a_spec, b_spec], [o_spec], [out_shape], []
    if twin_bf16:
        assert acc_axis is None
        out_specs.append(o_spec)
        out_shapes.append(jax.ShapeDtypeStruct(out_shape.shape, BF16))
    if carried is not None:
        assert len(grid) == 1
        in_specs += [ANY] * n_car
        out_specs += [ANY] * n_car
        out_shapes += list(carried[2])
        scratch = list(carried[3])
    outs = pl.pallas_call(body, name=name, grid=grid, in_specs=in_specs, out_specs=out_specs, out_shape=out_shapes,
                          scratch_shapes=scratch, compiler_params=_params(len(grid)))(a, b, *(carried[1] if carried else ()))
    return outs[0] if len(outs) == 1 else outs


def _ffn_rows(m):
    return m // 2 if m % 32 == 0 else m


def _ffn_in_swiglu(u2, wfi_t):
    m, d = u2.shape
    half, fs = N_DEV // 2, wfi_t.shape[1]
    tm = _ffn_rows(m)

    def body(u_ref, wa_ref, wb_ref, a_ref, b_ref, f_ref):
        u = u_ref[...]
        a = lax.dot_general(u, wa_ref[...], NT, preferred_element_type=F32)
        b = lax.dot_general(u, wb_ref[...], NT, preferred_element_type=F32)
        a_ref[...] = a.astype(BF16)
        b_ref[...] = b.astype(BF16)
        f_ref[...] = _swiglu(a, b).astype(BF16)

    out = jax.ShapeDtypeStruct((half, m, fs), BF16)
    act = pl.BlockSpec((None, tm, fs), lambda j, i: (j, i, 0))
    return pl.pallas_call(
        body, name="ffn_in_swiglu", grid=(half, m // tm),
        in_specs=[pl.BlockSpec((tm, d), lambda j, i: (i, 0)), pl.BlockSpec((None, fs, d), lambda j, i: (j, 0, 0)),
                  pl.BlockSpec((None, fs, d), lambda j, i: (j + half, 0, 0))],
        out_specs=[act, act, act], out_shape=[out, out, out], compiler_params=_params(2))(u2, wfi_t, wfi_t)


def _ffn_out_dx_swiglu(d_f, wfo, a_act, b_act):
    m, d = d_f.shape
    half, fs = N_DEV // 2, wfo.shape[1]
    tm = _ffn_rows(m)

    def body(df_ref, w_ref, a_ref, b_ref, o_ref):
        d_fin = lax.dot_general(df_ref[...], w_ref[...], NT, preferred_element_type=F32)
        _, vjp = jax.vjp(_swiglu, a_ref[...].astype(F32), b_ref[...].astype(F32))
        d_a, d_b = vjp(d_fin)
        o_ref[0] = d_a.astype(BF16)
        o_ref[1] = d_b.astype(BF16)

    act = pl.BlockSpec((None, tm, fs), lambda j, i: (j, i, 0))
    return pl.pallas_call(
        body, name="ffn_out_dx_swiglu", grid=(half, m // tm),
        in_specs=[pl.BlockSpec((tm, d), lambda j, i: (i, 0)), pl.BlockSpec((None, fs, d), lambda j, i: (j, 0, 0)), act, act],
        out_specs=pl.BlockSpec((2, None, tm, fs), lambda j, i: (0, j, i, 0)),
        out_shape=jax.ShapeDtypeStruct((2, half, m, fs), BF16), compiler_params=_params(2))(d_f, wfo, a_act, b_act)


DENSE_TILE = 256


def _dense_fwd(name, a, w, out_dtype=F32):
    m, k = a.shape
    n = w.shape[1]
    tn = DENSE_TILE
    return _matmul(name, NN, a, w, pl.BlockSpec((m, k), lambda j: (0, 0)), pl.BlockSpec((k, tn), lambda j: (0, j)),
                   pl.BlockSpec((m, tn), lambda j: (0, j)), jax.ShapeDtypeStruct((m, n), out_dtype), (n // tn,))


def _dense_dx(name, dy, w, out_dtype):
    m, n = dy.shape
    k = w.shape[0]
    tk = DENSE_TILE
    return _matmul(name, NT, dy, w, pl.BlockSpec((m, n), lambda j: (0, 0)), pl.BlockSpec((tk, n), lambda j: (j, 0)),
                   pl.BlockSpec((m, tk), lambda j: (0, j)), jax.ShapeDtypeStruct((m, k), out_dtype), (k // tk,))


def _dense_dw(name, a, dy):
    m, k = a.shape
    n = dy.shape[1]
    tn = DENSE_TILE
    return _matmul(name, TN, a, dy, pl.BlockSpec((m, k), lambda j: (0, 0)), pl.BlockSpec((m, tn), lambda j: (0, j)),
                   pl.BlockSpec((k, tn), lambda j: (0, j)), jax.ShapeDtypeStruct((k, n), F32), (n // tn,), twin_bf16=True)


def _gate_specs(m, d, n_row_ins):
    t = DENSE_TILE
    per = d // t
    cols = lambda base: pl.BlockSpec((m, t), lambda j, base=base: (0, base * per + j))
    bias = lambda base: pl.BlockSpec((1, t), lambda j, base=base: (0, base * per + j))
    return [cols(5), cols(6)] + [cols(0)] * n_row_ins + [bias(0), bias(1)]


def _attn_out_gate(o, wao, p, y_conv, gate_b):
    m, d = o.shape
    t = DENSE_TILE

    def body(o_ref, w_ref, pgc_ref, pga_ref, yc_ref, bc_ref, ba_ref, ya_ref, mix_ref):
        ya = jnp.dot(o_ref[...], w_ref[...], preferred_element_type=F32)
        ya_ref[...] = ya
        gb = jnp.concatenate([bc_ref[...], ba_ref[...]], axis=1)
        mix_ref[...] = _gate_mix(pgc_ref[...], pga_ref[...], yc_ref[...], ya, gb).astype(mix_ref.dtype)

    tile = pl.BlockSpec((m, t), lambda j: (0, j))
    return pl.pallas_call(
        body, name="attn_out_gate", grid=(d // t,),
        in_specs=[pl.BlockSpec((m, d), lambda j: (0, 0)), pl.BlockSpec((d, t), lambda j: (0, j))] + _gate_specs(m, d, 1),
        out_specs=[tile, tile], out_shape=[jax.ShapeDtypeStruct((m, d), F32), jax.ShapeDtypeStruct((m, d), BF16)],
        compiler_params=_params(1))(o, wao, p, p, y_conv, gate_b, gate_b)


def _mix_out_dx_gate(d_mix, wo, p, y_conv, y_attn, gate_b):
    m, d = d_mix.shape
    t = DENSE_TILE

    def body(dm_ref, w_ref, pgc_ref, pga_ref, yc_ref, ya_ref, bc_ref, ba_ref, dgc_ref, dga_ref, dyc_ref, dya_ref, dbc_ref, dba_ref):
        d_mixin = lax.dot_general(dm_ref[...], w_ref[...], NT, preferred_element_type=F32)
        gb = jnp.concatenate([bc_ref[...], ba_ref[...]], axis=1)
        _, vjp = jax.vjp(_gate_mix, pgc_ref[...], pga_ref[...], yc_ref[...], ya_ref[...], gb)
        dgc, dga, dyc, dya, dgb = vjp(d_mixin)
        dgc_ref[...] = dgc.astype(dgc_ref.dtype)
        dga_ref[...] = dga.astype(dga_ref.dtype)
        dyc_ref[...] = dyc.astype(dyc_ref.dtype)
        dya_ref[...] = dya.astype(dya_ref.dtype)
        dbc_ref[...] = dgb[:, :t]
        dba_ref[...] = dgb[:, t:]

    tile = pl.BlockSpec((m, t), lambda j: (0, j))
    vec = pl.BlockSpec((1, t), lambda j: (0, j))
    act = jax.ShapeDtypeStruct((m, d), BF16)
    par = jax.ShapeDtypeStruct((1, d), F32)
    return pl.pallas_call(
        body, name="mix_out_dx_gate", grid=(d // t,),
        in_specs=[pl.BlockSpec((m, d), lambda j: (0, 0)), pl.BlockSpec((t, d), lambda j: (j, 0))] + _gate_specs(m, d, 2),
        out_specs=[tile] * 4 + [vec, vec], out_shape=[act] * 4 + [par, par],
        compiler_params=_params(1))(d_mix, wo, p, p, y_conv, y_attn, gate_b, gate_b)


def _rowwise(name, fn, row_ins, par_ins, row_outs, par_outs, *, grid, in_specs, out_specs, tm, row_axis):
    n_ri, n_pi, n_ro, n_po = len(row_ins), len(par_ins), len(row_outs), len(par_outs)
    n_steps, tail = divmod(tm, ROW_CHUNK)
    assert tail % 16 == 0

    def body(*refs):
        ri = refs[:n_ri]
        pi = refs[n_ri:n_ri + n_pi]
        ro = refs[n_ri + n_pi:n_ri + n_pi + n_ro]
        po = refs[n_ri + n_pi + n_ro:]
        ps = [r[...] for r in pi]
        base = pl.program_id(row_axis) * tm

        def chunk(r0, rows, carry):
            xs = [r[pl.ds(r0, rows), :] for r in ri]
            outs, pouts = fn(base + r0, xs, ps)
            for r, o in zip(ro, outs):
                if isinstance(o, (list, tuple)):
                    for j, part in enumerate(o):
                        r[j, pl.ds(r0, rows), :] = part.astype(r.dtype)
                else:
                    r[pl.ds(r0, rows), :] = o.astype(r.dtype)
            return tuple(c + q for c, q in zip(carry, pouts))

        def step(i, carry):
            return chunk(pl.multiple_of(i * ROW_CHUNK, ROW_CHUNK), ROW_CHUNK, carry)

        acc = lax.fori_loop(0, n_steps, step, tuple(jnp.zeros(s.shape, F32) for s in par_outs))
        if tail:
            acc = chunk(n_steps * ROW_CHUNK, tail, acc)
        if n_po:
            first = pl.program_id(0) == 0
            for ax in range(1, len(grid)):
                first = first & (pl.program_id(ax) == 0)

            @pl.when(first)
            def _():
                for r in po:
                    r[...] = jnp.zeros_like(r)

            for r, a in zip(po, acc):
                r[...] += a

    return pl.pallas_call(body, name=name, grid=grid, in_specs=in_specs, out_specs=out_specs,
                          out_shape=tuple(row_outs) + tuple(par_outs),
                          compiler_params=_params(len(grid)))(*row_ins, *par_ins)


def _rows(name, fn, row_ins, par_ins, row_out_dtypes, par_out_widths, row_in_cols=None, row_out_widths=None):
    m = row_ins[0].shape[0]
    tm = _row_tile(m)
    in_specs = []
    for k, a in enumerate(row_ins):
        if row_in_cols is not None and row_in_cols[k] is not None:
            width, cb = row_in_cols[k]
            in_specs.append(pl.BlockSpec((tm, width), lambda i, cb=cb: (i, cb)))
        else:
            in_specs.append(pl.BlockSpec((tm, a.shape[1]), lambda i: (i, 0)))
    for a in par_ins:
        in_specs.append(pl.BlockSpec(a.shape, lambda i: (0, 0)))
    if row_out_widths is None:
        row_out_widths = [row_ins[0].shape[1]] * len(row_out_dtypes)
    row_outs = [jax.ShapeDtypeStruct((m, w), dt) for w, dt in zip(row_out_widths, row_out_dtypes)]
    par_outs = [jax.ShapeDtypeStruct((1, w), F32) for w in par_out_widths]
    out_specs = [pl.BlockSpec((tm, s.shape[1]), lambda i: (i, 0)) for s in row_outs]
    out_specs += [pl.BlockSpec(s.shape, lambda i: (0, 0)) for s in par_outs]
    return _rowwise(name, fn, row_ins, par_ins, row_outs, par_outs, grid=(m // tm,), in_specs=in_specs,
                    out_specs=out_specs, tm=tm, row_axis=0)


def _rms(x, g):
    return x * lax.rsqrt(jnp.mean(x * x, axis=-1, keepdims=True) + RMS_EPS) * g


def _ln_silu(y, g, b):
    mu = jnp.mean(y, axis=-1, keepdims=True)
    yc = y - mu
    var = jnp.mean(yc * yc, axis=-1, keepdims=True)
    return jax.nn.silu(yc * lax.rsqrt(var + LN_EPS) * g + b)


def _gate_mix(pgc, pga, yc, ya, gb):
    d = pgc.shape[1]
    return jax.nn.sigmoid(pgc + gb[:, :d]) * yc + jax.nn.sigmoid(pga + gb[:, d:]) * ya


def _post_mix(h0, mix, g_post, g_pre):
    h1 = h0 + _rms(mix, g_post)
    return h1, _rms(h1, g_pre)


def _swiglu(a, b):
    return jax.nn.silu(a) * b


def _conv_taps():
    taps = []
    for b in range(8):
        for a in range(CONV_PAD // 8):
            s = 8 * a + b
            if s < CONV_WIDTH:
                taps.append((b, a, CONV_WIDTH - 1 - s))
    return taps


def _conv_fwd(p, dww, dwb, d_model):
    m = p.shape[0]
    nch = d_model // LANE
    n_chunk = m // QB
    taps = _conv_taps()

    def body(a_ref, g_ref, w_ref, b_ref, y_ref, upad):
        upad[0:CONV_PAD, :] = jnp.zeros((CONV_PAD, LANE), F32)

        def fill(i, c):
            r0 = pl.multiple_of(i * QB, QB)
            u = a_ref[pl.ds(r0, QB), :] * jax.nn.sigmoid(g_ref[pl.ds(r0, QB), :])
            upad[pl.ds(pl.multiple_of(r0 + CONV_PAD, 8), QB), :] = u
            return c

        lax.fori_loop(0, n_chunk, fill, 0)

        def conv(i, c):
            r0 = pl.multiple_of(i * QB, QB)
            win = upad[pl.ds(r0, QB + CONV_PAD), :]
            acc = jnp.broadcast_to(b_ref[...], (QB, LANE))
            rolled = {}
            for b, a, j in taps:
                if b not in rolled:
                    rolled[b] = win if b == 0 else pltpu.roll(win, b, axis=0)
                lo = CONV_PAD - 8 * a
                acc = acc + w_ref[j:j + 1, :] * rolled[b][lo:lo + QB, :]
            y_ref[pl.ds(r0, QB), :] = acc
            return c

        lax.fori_loop(0, n_chunk, conv, 0)

    col = lambda off: pl.BlockSpec((m, LANE), lambda c: (0, off + c))
    return pl.pallas_call(
        body, name="conv_fwd", grid=(nch,),
        in_specs=[col(0), col(nch), pl.BlockSpec((CONV_PAD, LANE), lambda c: (0, c)), pl.BlockSpec((1, LANE), lambda c: (0, c))],
        out_specs=col(0), out_shape=jax.ShapeDtypeStruct((m, d_model), F32),
        scratch_shapes=[pltpu.VMEM((m + CONV_PAD, LANE), F32)], compiler_params=_params(1))(p, p, dww, dwb)


def _conv_bwd(p, dy, dww, d_model):
    m = p.shape[0]
    nch = d_model // LANE
    n_chunk = m // QB
    taps = _conv_taps()
    win_rows = QB + CONV_PAD

    def body(a_ref, g_ref, dy_ref, w_ref, da_ref, dg_ref, dw_ref, db_ref, upad, dypad, wacc, bacc):
        upad[0:CONV_PAD, :] = jnp.zeros((CONV_PAD, LANE), F32)
        dypad[m:m + CONV_PAD, :] = jnp.zeros((CONV_PAD, LANE), F32)
        wacc[...] = jnp.zeros_like(wacc)
        bacc[...] = jnp.zeros_like(bacc)

        def fill(i, c):
            r0 = pl.multiple_of(i * QB, QB)
            u = a_ref[pl.ds(r0, QB), :] * jax.nn.sigmoid(g_ref[pl.ds(r0, QB), :])
            upad[pl.ds(pl.multiple_of(r0 + CONV_PAD, 8), QB), :] = u
            dypad[pl.ds(r0, QB), :] = dy_ref[pl.ds(r0, QB), :]
            return c

        lax.fori_loop(0, n_chunk, fill, 0)

        def chunk(i, c):
            r0 = pl.multiple_of(i * QB, QB)
            dwin = dypad[pl.ds(r0, win_rows), :]
            du = jnp.zeros((QB, LANE), F32)
            rolled = {}
            for b, a, j in taps:
                if b not in rolled:
                    rolled[b] = dwin if b == 0 else pltpu.roll(dwin, win_rows - b, axis=0)
                du = du + w_ref[j:j + 1, :] * rolled[b][8 * a:8 * a + QB, :]
            av = a_ref[pl.ds(r0, QB), :]
            sg = jax.nn.sigmoid(g_ref[pl.ds(r0, QB), :])
            da_ref[pl.ds(r0, QB), :] = (du * sg).astype(da_ref.dtype)
            dg_ref[pl.ds(r0, QB), :] = (du * av * sg * (1.0 - sg)).astype(dg_ref.dtype)
            dyc = dy_ref[pl.ds(r0, QB), :]
            uwin = upad[pl.ds(r0, win_rows), :]
            rolled = {}
            for b, a, j in taps:
                if b not in rolled:
                    rolled[b] = uwin if b == 0 else pltpu.roll(uwin, b, axis=0)
                lo = CONV_PAD - 8 * a
                prod = dyc * rolled[b][lo:lo + QB, :]
                wacc[j] += prod.reshape(QB // 8, 8, LANE).sum(axis=0)
            bacc[...] += dyc.reshape(QB // 8, 8, LANE).sum(axis=0)
            return c

        lax.fori_loop(0, n_chunk, chunk, 0)
        for j in range(CONV_WIDTH):
            dw_ref[j:j + 1, :] = jnp.sum(wacc[j], axis=0, keepdims=True)
        dw_ref[CONV_WIDTH:CONV_PAD, :] = jnp.zeros((CONV_PAD - CONV_WIDTH, LANE), F32)
        db_ref[...] = jnp.sum(bacc[...], axis=0, keepdims=True)

    col = lambda off: pl.BlockSpec((m, LANE), lambda c: (0, off + c))
    return pl.pallas_call(
        body, name="conv_bwd", grid=(nch,),
        in_specs=[col(0), col(nch), col(0), pl.BlockSpec((CONV_PAD, LANE), lambda c: (0, c))],
        out_specs=[col(0), col(0), pl.BlockSpec((CONV_PAD, LANE), lambda c: (0, c)), pl.BlockSpec((1, LANE), lambda c: (0, c))],
        out_shape=(jax.ShapeDtypeStruct((m, d_model), BF16), jax.ShapeDtypeStruct((m, d_model), BF16),
                   jax.ShapeDtypeStruct((CONV_PAD, d_model), F32), jax.ShapeDtypeStruct((1, d_model), F32)),
        scratch_shapes=[pltpu.VMEM((m + CONV_PAD, LANE), F32), pltpu.VMEM((m + CONV_PAD, LANE), F32),
                        pltpu.VMEM((CONV_PAD, 8, LANE), F32), pltpu.VMEM((8, LANE), F32)],
        compiler_params=_params(1))(p, p, dy, dww)


EXP_CLAMP = 80.0


def _one_plus_exp(z):
    return 1.0 + jnp.exp(jnp.minimum(z, EXP_CLAMP))


def _softplus(z):
    return jnp.maximum(jnp.log(_one_plus_exp(z)), z)


def _softplus_sigmoid(z):
    s = _one_plus_exp(z)
    return jnp.maximum(jnp.log(s), z), 1.0 - 1.0 / s


def _tile_cumsums(x, tri2):
    xb = x.astype(BF16)
    out = []
    for i in range(0, x.shape[1] // QB, 2):
        both = jnp.dot(xb[:, i * QB:(i + 2) * QB], tri2, preferred_element_type=F32)
        out += [both[:, :QB], both[:, QB:]]
    return out


def _tri2(kind):
    jj = lax.broadcasted_iota(I32, (2 * QB, 2 * QB), 0)
    ss = lax.broadcasted_iota(I32, (2 * QB, 2 * QB), 1)
    same = (jj >= QB) == (ss >= QB)
    keep = {"ge": jj >= ss, "le": jj <= ss}[kind]
    return jnp.where(same & keep, 1.0, 0.0).astype(BF16)


def _attn_fwd(p, d_model, shards):
    m = p.shape[0]
    nqb = m // QB
    ngrp = d_model // (LANE_BLOCKS * LANE)
    qo, ko, vo = 2 * ngrp, 3 * ngrp, 4 * ngrp
    scale = HEAD_DIM ** -0.5
    n_sh = len(shards)

    assert nqb >= KEY_TILES

    def body(q_ref, k_ref, v_ref, *rest):
        shard_refs, (o_ref, t_ref), rest = rest[:n_sh], rest[n_sh:n_sh + 2], rest[n_sh + 2:]
        gathered_refs, (acc_ref, car_ref), sems = rest[:n_sh], rest[n_sh:n_sh + 2], rest[n_sh + 2:]
        starts, relays, waits = _gather_copies(shard_refs, gathered_refs, *sems)

        @pl.when((pl.program_id(0) == 0) & (pl.program_id(1) == 0))
        def _():
            for f in starts:
                f()

        @pl.when((pl.program_id(0) == ngrp - 1) & (pl.program_id(1) == (5 * nqb) // 8))
        def _():
            for f in relays:
                f()

        qb = pl.program_id(1)
        lane = lax.broadcasted_iota(I32, (QB, LANE), 1)
        head0 = lane < HEAD_DIM
        row_g = qb * QB + lax.broadcasted_iota(I32, (QB, KEY_CHUNK), 0)
        col_l = lax.broadcasted_iota(I32, (QB, KEY_CHUNK), 1)
        tri = _tri2("ge")
        heads = range(2 * LANE_BLOCKS)
        qh = []
        for lb in range(LANE_BLOCKS):
            q2 = (q_ref[:, lb * LANE:(lb + 1) * LANE] * scale).astype(BF16)
            zero = jnp.zeros_like(q2)
            qh += [jnp.where(head0, q2, zero), jnp.where(head0, zero, q2)]
        acc_ref[...] = jnp.zeros_like(acc_ref)
        car_ref[...] = jnp.zeros_like(car_ref)

        def chunk(first_tile, bound, n_tiles=KEY_TILES):
            r0 = pl.multiple_of(first_tile * QB, QB)
            keys = n_tiles * QB
            kcs = [k_ref[pl.ds(r0, keys), lb * LANE:(lb + 1) * LANE].astype(BF16) for lb in range(LANE_BLOCKS)]
            vcs = [v_ref[pl.ds(r0, keys), lb * LANE:(lb + 1) * LANE].astype(BF16) for lb in range(LANE_BLOCKS)]
            valid = None if bound is None else (col_l[:, :keys] + r0) < bound
            zs = [lax.dot_general(qh[h], kcs[h // 2], NT, preferred_element_type=F32) for h in heads]
            sps = [_softplus(z) for z in zs]
            if valid is not None:
                sps = [jnp.where(valid, sp, 0.0) for sp in sps]
            cums = [_tile_cumsums(sp, tri) for sp in sps]
            cars = [car_ref[h] for h in heads]
            a_tiles = [[None] * n_tiles for h in heads]
            for i in reversed(range(n_tiles)):
                for h in heads:
                    cum = cums[h][i]
                    a_tiles[h][i] = jnp.exp(zs[h][:, i * QB:(i + 1) * QB] - (cum + cars[h]))
                    cars[h] = cars[h] + jnp.broadcast_to(cum[:, 0:1], cum.shape)
            for h in heads:
                a = jnp.concatenate(a_tiles[h], axis=1)
                if valid is not None:
                    a = jnp.where(valid, a, 0.0)
                acc_ref[h] += jnp.dot(a.astype(BF16), vcs[h // 2], preferred_element_type=F32)
                car_ref[h] = cars[h]

        near = jnp.maximum(qb - (KEY_TILES - 1), 0)
        chunk(near, row_g)
        n_full = lax.shift_right_logical(near, KEY_SHIFT)

        def step(i, c):
            chunk(near - KEY_TILES * (i + 1), None)
            return c

        lax.fori_loop(0, n_full, step, 0)
        left = near - KEY_TILES * n_full

        @pl.when((left > 0) & (left <= 2))
        def _():
            chunk(0, left * QB, n_tiles=2)

        @pl.when(left > 2)
        def _():
            chunk(0, left * QB)

        for lb in range(LANE_BLOCKS):
            o_ref[:, lb * LANE:(lb + 1) * LANE] = jnp.where(head0, acc_ref[2 * lb], acc_ref[2 * lb + 1]).astype(o_ref.dtype)
        for h in heads:
            t_ref[:, h * QB:(h + 1) * QB] = car_ref[h]

        @pl.when((pl.program_id(0) == ngrp - 1) & (pl.program_id(1) == nqb - 1))
        def _():
            for f in waits:
                f()

    wide = LANE_BLOCKS * LANE
    outs = pl.pallas_call(
        body, name="attn_fwd", grid=(ngrp, nqb),
        in_specs=[pl.BlockSpec((QB, wide), lambda g, qb: (qb, qo + g)),
                  pl.BlockSpec((m, wide), lambda g, qb: (0, ko + g)),
                  pl.BlockSpec((m, wide), lambda g, qb: (0, vo + g))] + [ANY] * n_sh,
        out_specs=[pl.BlockSpec((QB, wide), lambda g, qb: (qb, g)),
                   pl.BlockSpec((QB, 2 * wide), lambda g, qb: (qb, g))] + [ANY] * n_sh,
        out_shape=[jax.ShapeDtypeStruct((m, d_model), BF16), jax.ShapeDtypeStruct((m, 2 * d_model), F32)]
        + [jax.ShapeDtypeStruct((N_DEV,) + s.shape, s.dtype) for s in shards],
        scratch_shapes=[pltpu.VMEM((2 * LANE_BLOCKS, QB, LANE), F32), pltpu.VMEM((2 * LANE_BLOCKS, QB, LANE), F32)]
        + _exchange_sems(n_sh) + [pltpu.SemaphoreType.DMA((n_sh,))],
        compiler_params=_params(2))(p, p, p, *shards)
    return outs[0], outs[1], outs[2:]


def _attn_bwd(p, d_o, tot, d_model, slabs):
    m = p.shape[0]
    nqb = m // QB
    blocks = BWD_LANE_BLOCKS
    ngrp = d_model // (blocks * LANE)
    qo, ko, vo = 2 * ngrp, 3 * ngrp, 4 * ngrp
    scale = HEAD_DIM ** -0.5
    n_sl = len(slabs)

    assert nqb >= KEY_TILES

    def body(q_ref, k_ref, v_ref, do_ref, t_ref, *rest):
        slab_refs, (dq_ref, dk_ref, dv_ref), rest = rest[:n_sl], rest[n_sl:n_sl + 3], rest[n_sl + 3:]
        recv_refs, (dkacc, dvacc, dqacc, csp, cg), sems = rest[:n_sl], rest[n_sl:n_sl + 5], rest[n_sl + 5:]
        starts, waits = _scatter_copies(slab_refs, recv_refs, *sems)

        @pl.when((pl.program_id(0) == 0) & (pl.program_id(1) == 0))
        def _():
            for f in starts:
                f()

        qb = pl.program_id(1)
        lane = lax.broadcasted_iota(I32, (QB, LANE), 1)
        head0 = lane < HEAD_DIM
        row_g = qb * QB + lax.broadcasted_iota(I32, (QB, KEY_CHUNK), 0)
        col_l = lax.broadcasted_iota(I32, (QB, KEY_CHUNK), 1)
        tri_ge = _tri2("ge")
        tri_le = _tri2("le")
        heads = range(2 * blocks)
        qh, doh = [], []
        for lb in range(blocks):
            q2 = (q_ref[:, lb * LANE:(lb + 1) * LANE] * scale).astype(BF16)
            do2 = do_ref[:, lb * LANE:(lb + 1) * LANE]
            zero = jnp.zeros_like(q2)
            qh += [jnp.where(head0, q2, zero), jnp.where(head0, zero, q2)]
            doh += [jnp.where(head0, do2, zero), jnp.where(head0, zero, do2)]
        q_pairs = [jnp.concatenate(qh[2 * lb:2 * lb + 2], axis=0) for lb in range(blocks)]
        do_pairs = [jnp.concatenate(doh[2 * lb:2 * lb + 2], axis=0) for lb in range(blocks)]

        @pl.when(qb == 0)
        def _():
            dkacc[...] = jnp.zeros_like(dkacc)
            dvacc[...] = jnp.zeros_like(dvacc)

        dqacc[...] = jnp.zeros_like(dqacc)
        for h in heads:
            csp[h] = t_ref[:, h * QB:(h + 1) * QB]
        cg[...] = jnp.zeros_like(cg)

        def chunk(first_tile, bound, n_tiles=KEY_TILES):
            r0 = pl.multiple_of(first_tile * QB, QB)
            keys = n_tiles * QB
            kcs = [k_ref[pl.ds(r0, keys), lb * LANE:(lb + 1) * LANE].astype(BF16) for lb in range(blocks)]
            vcs = [v_ref[pl.ds(r0, keys), lb * LANE:(lb + 1) * LANE].astype(BF16) for lb in range(blocks)]
            valid = None if bound is None else (col_l[:, :keys] + r0) < bound
            tiles = [slice(i * QB, (i + 1) * QB) for i in range(n_tiles)]
            zs = [lax.dot_general(qh[h], kcs[h // 2], NT, preferred_element_type=F32) for h in heads]
            das = [lax.dot_general(doh[h], vcs[h // 2], NT, preferred_element_type=F32) for h in heads]
            sps, sgs = zip(*[_softplus_sigmoid(z) for z in zs])
            if valid is not None:
                sps = [jnp.where(valid, sp, 0.0) for sp in sps]
            cums = [_tile_cumsums(sp, tri_ge) for sp in sps]
            a_tiles, g_tiles = [[] for h in heads], [[] for h in heads]
            for h in heads:
                rest = csp[h]
                for i, c in enumerate(tiles):
                    cum = cums[h][i]
                    rest = rest - jnp.broadcast_to(cum[:, 0:1], cum.shape)
                    a = jnp.exp(zs[h][:, c] - (cum + rest))
                    if valid is not None:
                        a = jnp.where(valid[:, c], a, 0.0)
                    a_tiles[h].append(a)
                    g_tiles[h].append(a * das[h][:, c])
                csp[h] = rest
            gcums = [_tile_cumsums(jnp.concatenate(g_tiles[h], axis=1), tri_le) for h in heads]
            dzbs, abs_ = [], []
            for h in heads:
                g_before = cg[h]
                dz_tiles = []
                for i, c in enumerate(tiles):
                    gcum = gcums[h][i]
                    dz = g_tiles[h][i] - sgs[h][:, c] * (g_before + gcum)
                    if valid is not None:
                        dz = jnp.where(valid[:, c], dz, 0.0)
                    dz_tiles.append(dz)
                    g_before = g_before + jnp.broadcast_to(gcum[:, QB - 1:QB], gcum.shape)
                cg[h] = g_before
                dzbs.append(jnp.concatenate(dz_tiles, axis=1).astype(BF16))
                abs_.append(jnp.concatenate(a_tiles[h], axis=1).astype(BF16))
            for h in heads:
                dqacc[h] += jnp.dot(dzbs[h], kcs[h // 2], preferred_element_type=F32)
            for lb in range(blocks):
                dz_pair = jnp.concatenate(dzbs[2 * lb:2 * lb + 2], axis=0)
                a_pair = jnp.concatenate(abs_[2 * lb:2 * lb + 2], axis=0)
                dkacc[pl.ds(r0, keys), lb * LANE:(lb + 1) * LANE] += lax.dot_general(
                    dz_pair, q_pairs[lb], TN, preferred_element_type=F32)
                dvacc[pl.ds(r0, keys), lb * LANE:(lb + 1) * LANE] += lax.dot_general(
                    a_pair, do_pairs[lb], TN, preferred_element_type=F32)

        near = jnp.maximum(qb - (KEY_TILES - 1), 0)
        n_full = lax.shift_right_logical(near, KEY_SHIFT)

        def step(i, c):
            chunk(KEY_TILES * i, None)
            return c

        lax.fori_loop(0, n_full, step, 0)
        left = near - KEY_TILES * n_full

        @pl.when((left > 0) & (left <= 2))
        def _():
            chunk(KEY_TILES * n_full, near * QB, n_tiles=2)

        @pl.when(left > 2)
        def _():
            chunk(KEY_TILES * n_full, near * QB)

        chunk(near, row_g)
        for lb in range(blocks):
            dq2 = jnp.where(head0, dqacc[2 * lb], dqacc[2 * lb + 1]) * scale
            dq_ref[:, lb * LANE:(lb + 1) * LANE] = dq2.astype(dq_ref.dtype)

        @pl.when(qb == nqb - 1)
        def _():
            dk_ref[...] = dkacc[...].astype(dk_ref.dtype)
            dv_ref[...] = dvacc[...].astype(dv_ref.dtype)

        @pl.when((pl.program_id(0) == ngrp - 1) & (qb == nqb - 1))
        def _():
            for f in waits:
                f()

    out = jax.ShapeDtypeStruct((m, d_model), BF16)
    wide = blocks * LANE
    carry = pltpu.VMEM((2 * blocks, QB, LANE), F32)
    outs = pl.pallas_call(
        body, name="attn_bwd", grid=(ngrp, nqb),
        in_specs=[pl.BlockSpec((QB, wide), lambda g, qb: (qb, qo + g)),
                  pl.BlockSpec((m, wide), lambda g, qb: (0, ko + g)),
                  pl.BlockSpec((m, wide), lambda g, qb: (0, vo + g)),
                  pl.BlockSpec((QB, wide), lambda g, qb: (qb, g)),
                  pl.BlockSpec((QB, 2 * wide), lambda g, qb: (qb, g))] + [ANY] * n_sl,
        out_specs=[pl.BlockSpec((QB, wide), lambda g, qb: (qb, g)),
                   pl.BlockSpec((m, wide), lambda g, qb: (0, g)),
                   pl.BlockSpec((m, wide), lambda g, qb: (0, g))] + [ANY] * n_sl,
        out_shape=[out, out, out] + _received_shapes(slabs),
        scratch_shapes=[pltpu.VMEM((m, wide), F32), pltpu.VMEM((m, wide), F32), carry, carry, carry] + _exchange_sems(n_sl),
        compiler_params=_params(2))(p, p, p, d_o, tot, *slabs)
    return outs[0], outs[1], outs[2], outs[3:]


def _mesh_pos():
    return lax.axis_index("x"), lax.axis_index("y"), lax.axis_index("c")


def _other_chips(x, y):
    return [(1 - x, y), (x, 1 - y), (1 - x, 1 - y)]


def _dev(x, y, c):
    return 4 * x + 2 * y + c


def _all_gather(shards):
    n = len(shards)

    def body(*refs):
        starts, relays, waits = _gather_copies(refs[:n], refs[n:2 * n], *refs[2 * n:])
        for f in starts + relays + waits:
            f()

    return pl.pallas_call(
        body, name="comm_all_gather", in_specs=[ANY] * n, out_specs=[ANY] * n,
        out_shape=[jax.ShapeDtypeStruct((N_DEV,) + s.shape, s.dtype) for s in shards],
        scratch_shapes=[pltpu.SemaphoreType.DMA((n, 7)), pltpu.SemaphoreType.DMA((n, 7)), pltpu.SemaphoreType.DMA((n,))],
    )(*shards)


def _peers(x, y, c):
    out = []
    for mask in range(1, N_DEV):
        px, py, pc = x ^ (mask >> 2), y ^ ((mask >> 1) & 1), c ^ (mask & 1)
        out.append((mask - 1, (px, py, pc), _dev(px, py, pc)))
    return out


def _remote(src, dst, send_sems, recv_sems, k, s, peer):
    return pltpu.make_async_remote_copy(src_ref=src, dst_ref=dst, send_sem=send_sems.at[k, s], recv_sem=recv_sems.at[k, s],
                                        device_id=peer, device_id_type=MESH)


def _gather_copies(ins, outs, send_sems, recv_sems, local_sems):
    x, y, c = _mesh_pos()
    sibling = (x, y, 1 - c)
    chips = _other_chips(x, y)
    starts, relays, waits = [], [], []
    for k in range(len(ins)):
        def slot(block, k=k):
            return outs[k].at[_dev(*block)]

        def copy(s, src, block, to, k=k):
            return _remote(src, slot(block), send_sems, recv_sems, k, s, to)

        own = pltpu.make_async_copy(ins[k], slot((x, y, c)), local_sems.at[k])
        to_sibling = copy(0, ins[k], (x, y, c), sibling)
        starts += [own.start, to_sibling.start]
        waits += [own.wait, to_sibling.wait_send, copy(0, ins[k], (x, y, 1 - c), sibling).wait_recv]
        for j, chip in enumerate(chips):
            out = copy(1 + j, ins[k], (x, y, c), (*chip, c))
            relay = copy(4 + j, slot((*chip, c)), (*chip, c), sibling)
            starts.append(out.start)
            relays += [copy(1 + j, ins[k], (*chip, c), sibling).wait_recv, relay.start]
            waits += [out.wait_send, relay.wait_send, copy(4 + j, ins[k], (*chip, 1 - c), sibling).wait_recv]
    return starts, relays, waits


def _scatter_copies(ins, outs, send_sems, recv_sems):
    x, y, c = _mesh_pos()
    starts, waits = [], []
    for k in range(len(ins)):
        for s, peer, idx in _peers(x, y, c):
            send = _remote(ins[k].at[idx], outs[k].at[s], send_sems, recv_sems, k, s, peer)
            starts.append(send.start)
            waits += [send.wait_recv, send.wait_send]
    return starts, waits


def _exchange_sems(n):
    return [pltpu.SemaphoreType.DMA((n, N_DEV - 1)), pltpu.SemaphoreType.DMA((n, N_DEV - 1))]


def _received_shapes(slabs):
    return [jax.ShapeDtypeStruct((N_DEV - 1,) + a.shape[1:], a.dtype) for a in slabs]


def _chips_and_own(x, y):
    return _other_chips(x, y) + [(x, y)]


def _sibling_exchange(slabs):
    n = len(slabs)

    def body(*refs):
        ins, outs, (send_sems, recv_sems) = refs[:n], refs[n:2 * n], refs[2 * n:]
        x, y, c = _mesh_pos()
        copies = [_remote(ins[k].at[_dev(*chip, 1 - c)], outs[k].at[r], send_sems, recv_sems, k, r, (x, y, 1 - c))
                  for k in range(n) for r, chip in enumerate(_chips_and_own(x, y))]
        for cp in copies:
            cp.start()
        for cp in copies:
            cp.wait_recv()
        for cp in copies:
            cp.wait_send()

    return pl.pallas_call(body, name="comm_rs_sibling", in_specs=[ANY] * n, out_specs=[ANY] * n,
                          out_shape=[jax.ShapeDtypeStruct((4,) + a.shape[1:], a.dtype) for a in slabs],
                          scratch_shapes=[pltpu.SemaphoreType.DMA((n, 4)), pltpu.SemaphoreType.DMA((n, 4))])(*slabs)


def _chip_copies(ins, outs, send_sems, recv_sems):
    x, y, c = _mesh_pos()
    starts, waits = [], []
    for k in range(len(ins)):
        for r, chip in enumerate(_other_chips(x, y)):
            cp = _remote(ins[k].at[r], outs[k].at[r], send_sems, recv_sems, k, r, (*chip, c))
            starts.append(cp.start)
            waits += [cp.wait_recv, cp.wait_send]
    return starts, [], waits


def _pair_sum(slab_idx, grad, from_sibling):
    _, rows, cols = grad.shape
    tr = _shard_tile(rows)

    def body(idx_ref, g_ref, s_ref, o_ref):
        o_ref[...] = (g_ref[...] + s_ref[...].astype(F32)).astype(o_ref.dtype)

    gs = pltpu.PrefetchScalarGridSpec(
        num_scalar_prefetch=1, grid=(3, rows // tr),
        in_specs=[pl.BlockSpec((None, tr, cols), lambda r, i, idx: (idx[r], i, 0)),
                  pl.BlockSpec((None, tr, cols), lambda r, i, idx: (r, i, 0))],
        out_specs=pl.BlockSpec((None, tr, cols), lambda r, i, idx: (r, i, 0)))
    return pl.pallas_call(body, name="rs_pair_sum", grid_spec=gs, out_shape=jax.ShapeDtypeStruct((3, rows, cols), BF16),
                          compiler_params=_params(2))(slab_idx, grad, from_sibling)


def _shard_tile(rows):
    for tr in range(min(rows, 512), 0, -1):
        if rows % tr == 0 and (tr % 16 == 0 or tr == rows):
            return tr


def _adamw_math(w, g, m, v):
    m = ADAM_B1 * m + (1.0 - ADAM_B1) * g
    v = ADAM_B2 * v + (1.0 - ADAM_B2) * (g * g)
    m_hat = m / (1.0 - ADAM_B1 ** ADAM_STEP)
    v_hat = v / (1.0 - ADAM_B2 ** ADAM_STEP)
    delta = -ADAM_LR * (m_hat / (jnp.sqrt(v_hat) + ADAM_EPS) + ADAM_WD * w)
    return delta, m, v


def _adamw_shard(me, grad, received, w, m, v):
    rows, cols = w.shape
    tr = _shard_tile(rows)
    n_rec = len(received)

    def body(me_ref, g_ref, *rest):
        r_refs, (w_ref, m_ref, v_ref, go_ref, do_ref, mo_ref, vo_ref) = rest[:n_rec], rest[n_rec:]
        g = g_ref[...]
        for r_ref in r_refs:
            for s in range(r_ref.shape[0]):
                g = g + r_ref[s].astype(F32)
        delta, m_new, v_new = _adamw_math(w_ref[...], g, m_ref[...], v_ref[...])
        go_ref[...] = g
        do_ref[...] = delta
        mo_ref[...] = m_new
        vo_ref[...] = v_new

    flat = pl.BlockSpec((tr, cols), lambda i, me: (i, 0))
    gs = pltpu.PrefetchScalarGridSpec(
        num_scalar_prefetch=1, grid=(rows // tr,),
        in_specs=[pl.BlockSpec((None, tr, cols), lambda i, me: (me[0], i, 0))]
        + [pl.BlockSpec((r.shape[0], tr, cols), lambda i, me: (0, i, 0)) for r in received] + [flat, flat, flat],
        out_specs=[flat, flat, flat, flat])
    out = jax.ShapeDtypeStruct((rows, cols), F32)
    return pl.pallas_call(body, name="adamw_shard", grid_spec=gs, out_shape=(out, out, out, out),
                          compiler_params=_params(1))(me, grad, *received, w, m, v)


def _small_reduce_adamw(slabs, w, m, v):
    _, rows, _ = slabs.shape

    def body(s_ref, w_ref, m_ref, v_ref, g_ref, d_ref, mo_ref, vo_ref, land, send_sems, recv_sems):
        x, y, c = _mesh_pos()
        me = _dev(x, y, c)
        copies = []
        for mask in range(1, N_DEV):
            px, py, pc = x ^ (mask >> 2), y ^ ((mask >> 1) & 1), c ^ (mask & 1)
            copies.append(pltpu.make_async_remote_copy(
                src_ref=s_ref.at[_dev(px, py, pc)], dst_ref=land.at[me], send_sem=send_sems.at[mask - 1],
                recv_sem=recv_sems.at[mask - 1], device_id=(px, py, pc), device_id_type=MESH))
        for cp in copies:
            cp.start()
        land[me] = s_ref[me]
        for mask in range(1, N_DEV):
            px, py, pc = x ^ (mask >> 2), y ^ ((mask >> 1) & 1), c ^ (mask & 1)
            pltpu.make_async_remote_copy(
                src_ref=s_ref.at[me], dst_ref=land.at[_dev(px, py, pc)], send_sem=send_sems.at[mask - 1],
                recv_sem=recv_sems.at[mask - 1], device_id=(px, py, pc), device_id_type=MESH).wait_recv()
        for cp in copies:
            cp.wait_send()
        g = land[0]
        for d in range(1, N_DEV):
            g = g + land[d]
        delta, m_new, v_new = _adamw_math(w_ref[...], g, m_ref[...], v_ref[...])
        g_ref[...] = g
        d_ref[...] = delta
        mo_ref[...] = m_new
        vo_ref[...] = v_new

    out = jax.ShapeDtypeStruct((rows, LANE), F32)
    return pl.pallas_call(
        body, name="comm_small_reduce_adamw", in_specs=[VMEM_WHOLE] * 4, out_specs=[VMEM_WHOLE] * 4, out_shape=(out, out, out, out),
        scratch_shapes=[pltpu.VMEM((N_DEV, rows, LANE), F32), pltpu.SemaphoreType.DMA((N_DEV - 1,)),
                        pltpu.SemaphoreType.DMA((N_DEV - 1,))],
    )(slabs, w, m, v)


def _cast_bf16(arrs):
    n = len(arrs)

    def body(*refs):
        for i_ref, o_ref in zip(refs[:n], refs[n:]):
            o_ref[...] = i_ref[...].astype(BF16)

    return pl.pallas_call(body, name="cast_bf16", in_specs=[VMEM_WHOLE] * n, out_specs=[VMEM_WHOLE] * n,
                          out_shape=[jax.ShapeDtypeStruct(a.shape, BF16) for a in arrs],
                          compiler_params=pltpu.CompilerParams(vmem_limit_bytes=VMEM_LIMIT))(*arrs)


REPLICATED = ("pre_mix_g", "gate_b", "dw_b", "conv_ln_g", "conv_ln_b", "post_mix_g", "pre_ffn_g", "post_ffn_g")
SHARDED = ("w_in", "w_conv_out", "w_attn_out", "w_o", "w_ffn_in", "w_ffn_out")
WEIGHTS = ("meta_tokens", "pre_mix_g", "w_in", "gate_b", "dw_w", "dw_b", "conv_ln_g", "conv_ln_b", "w_conv_out",
           "w_attn_out", "w_o", "post_mix_g", "pre_ffn_g", "w_ffn_in", "w_ffn_out", "post_ffn_g")


def kernel(x, meta_tokens, pre_mix_g, w_in, gate_b, dw_w, dw_b, conv_ln_g, conv_ln_b, w_conv_out, w_attn_out, w_o, post_mix_g, pre_ffn_g, w_ffn_in, w_ffn_out, post_ffn_g, loss_target, m_meta_tokens, m_pre_mix_g, m_w_in, m_gate_b, m_dw_w, m_dw_b, m_conv_ln_g, m_conv_ln_b, m_w_conv_out, m_w_attn_out, m_w_o, m_post_mix_g, m_pre_ffn_g, m_w_ffn_in, m_w_ffn_out, m_post_ffn_g, v_meta_tokens, v_pre_mix_g, v_w_in, v_gate_b, v_dw_w, v_dw_b, v_conv_ln_g, v_conv_ln_b, v_w_conv_out, v_w_attn_out, v_w_o, v_post_mix_g, v_pre_ffn_g, v_w_ffn_in, v_w_ffn_out, v_post_ffn_g):
    given = dict(locals())
    seq, d = x.shape[1], x.shape[2]
    n_meta = meta_tokens.shape[0]
    length = n_meta + seq
    m_rows = -(-length // QB) * QB
    dc = d // N_DEV
    assert dc == LANE and n_meta % 8 == 0 and seq % 8 == 0
    fs = w_ffn_in.shape[2]
    fr = w_ffn_out.shape[1]
    assert 2 * fr == fs

    transposed = ("w_ffn_in",)

    def shard(name):
        return given[name][0].T if name.endswith(transposed) else given[name][0]

    local = {k: shard(k) for k in SHARDED}
    cast = _cast_bf16([local[k] for k in SHARDED])
    dww_pad = jnp.pad(dw_w[0], ((0, CONV_PAD - CONV_WIDTH), (0, 0)))
    wi, meta_g, dww_g = _all_gather([cast[0], meta_tokens, dww_pad])
    meta_full = jnp.concatenate([meta_g[j] for j in range(N_DEV)], axis=1)
    dww_full = jnp.concatenate([dww_g[j] for j in range(N_DEV)], axis=1)
    ns = wi.shape[2]

    tail = jnp.zeros((m_rows - length, d), F32)
    h0 = jnp.concatenate([meta_full, x[0], tail], axis=0)
    target = jnp.concatenate([jnp.zeros((n_meta, d), F32), loss_target[0], tail], axis=0)

    (u,) = _rows("pre_mix_norm", lambda r0, xs, ps: ([_rms(xs[0], ps[0])], []), [h0], [pre_mix_g], [BF16], [])
    square = list(cast[1:4])
    p, p16, *gathered = _matmul(
        "in_proj", NN, u, wi, pl.BlockSpec((m_rows, d), lambda i: (0, 0)), pl.BlockSpec((None, d, ns), lambda i: (i, 0, 0)),
        pl.BlockSpec((m_rows, ns), lambda i: (0, i)), jax.ShapeDtypeStruct((m_rows, N_DEV * ns), F32), (N_DEV,), twin_bf16=True,
        carried=(_gather_copies, square, [jax.ShapeDtypeStruct((N_DEV,) + s.shape, s.dtype) for s in square],
                 _exchange_sems(len(square)) + [pltpu.SemaphoreType.DMA((len(square),))]))
    wco, wao, wo = (g.reshape(d, d) for g in gathered)
    o, tot, (wfi_t, wfo) = _attn_fwd(p16, d, list(cast[4:6]))
    wfo = wfo.reshape(N_DEV // 2, fs, d)
    y = _conv_fwd(p, dww_full, dw_b, d)
    (yc,) = _rows("conv_norm", lambda r0, xs, ps: ([_ln_silu(xs[0], ps[0], ps[1])], []), [y], [conv_ln_g, conv_ln_b], [BF16], [])
    y_conv = _dense_fwd("conv_out", yc, wco)
    y_attn, mixin = _attn_out_gate(o, wao, p, y_conv, gate_b)
    mix = _dense_fwd("mix_out", mixin, wo)
    h1, u2 = _rows("post_mix", lambda r0, xs, ps: (list(_post_mix(xs[0], xs[1], ps[0], ps[1])), []), [h0, mix],
                   [post_mix_g, pre_ffn_g], [F32, BF16], [])
    half = N_DEV // 2
    a_act, b_act, f_in = _ffn_in_swiglu(u2, wfi_t)
    f = _matmul("ffn_out", NN, f_in, wfo, pl.BlockSpec((None, m_rows, fs), lambda j: (j, 0, 0)), pl.BlockSpec((None, fs, d), lambda j: (j, 0, 0)),
                pl.BlockSpec((m_rows, d), lambda j: (0, 0)), jax.ShapeDtypeStruct((m_rows, d), F32), (half,), acc_axis=0)

    def loss_head(r0, xs, ps):
        h1_, f_, t_ = xs
        r, vjp = jax.vjp(_rms, f_, ps[0])
        rows = r0 + lax.broadcasted_iota(I32, (h1_.shape[0], 1), 0)
        real = (rows >= n_meta) & (rows < length)
        err = jnp.where(real, h1_ + r - t_, 0.0)
        dh2 = err * (1.0 / d)
        d_f, dg = vjp(dh2)
        part = jnp.sum(0.5 * jnp.mean(err * err, axis=-1, keepdims=True), axis=0, keepdims=True)
        return [d_f, dh2], [dg, jnp.broadcast_to(part, (1, LANE))]

    d_f, dh2, g_post_ffn, loss_part = _rows("loss_head", loss_head, [h1, f, target], [post_ffn_g], [BF16, F32], [d, LANE])

    d_ab = _ffn_out_dx_swiglu(d_f, wfo, a_act, b_act).reshape(N_DEV, m_rows, fs)
    g_wfo = _matmul("ffn_out_dw", TN, f_in, d_f, pl.BlockSpec((None, m_rows, fs), lambda j: (j, 0, 0)), pl.BlockSpec((m_rows, d), lambda j: (0, 0)),
                    pl.BlockSpec((None, fs, d), lambda j: (j, 0, 0)), jax.ShapeDtypeStruct((half, fs, d), F32), (half,), twin_bf16=True)

    du2 = _matmul("ffn_in_dx", NN, d_ab, wfi_t, pl.BlockSpec((None, m_rows, fs), lambda i: (i, 0, 0)), pl.BlockSpec((None, fs, d), lambda i: (i, 0, 0)),
                  pl.BlockSpec((m_rows, d), lambda i: (0, 0)), jax.ShapeDtypeStruct((m_rows, d), F32), (N_DEV,), acc_axis=0)
    g_wfi = _matmul("ffn_in_dw", TN, d_ab, u2, pl.BlockSpec((None, m_rows, fs), lambda i: (i, 0, 0)), pl.BlockSpec((m_rows, d), lambda i: (0, 0)),
                    pl.BlockSpec((None, fs, d), lambda i: (i, 0, 0)), jax.ShapeDtypeStruct((N_DEV, fs, d), F32), (N_DEV,), twin_bf16=True)

    def post_mix_bwd(r0, xs, ps):
        h0_, mix_, dh2_, du2_ = xs
        _, vjp = jax.vjp(_post_mix, h0_, mix_, ps[0], ps[1])
        dh0_, dmix_, dg1, dg2 = vjp((dh2_, du2_))
        return [dmix_, dh0_], [dg1, dg2]

    d_mix, dh1, g_post_mix, g_pre_ffn = _rows("post_mix_bwd", post_mix_bwd, [h0, mix, dh2, du2], [post_mix_g, pre_ffn_g],
                                              [BF16, F32], [d, d])
    g_wo = _dense_dw("mix_out_dw", mixin, d_mix)
    dp_gc, dp_ga, d_yconv, d_yattn, g_gb_c, g_gb_a = _mix_out_dx_gate(d_mix, wo, p, y_conv, y_attn, gate_b)
    g_gate_b = jnp.concatenate([g_gb_c, g_gb_a], axis=1)
    d_o = _dense_dx("attn_out_dx", d_yattn, wao, BF16)
    g_wao = _dense_dw("attn_out_dw", o, d_yattn)
    d_yc = _dense_dx("conv_out_dx", d_yconv, wco, F32)
    g_wco = _dense_dw("conv_out_dw", yc, d_yconv)
    big = {"w_ffn_out": [g.reshape(N_DEV, fr, d) for g in g_wfo], "w_ffn_in": g_wfi,
           "w_o": [g.reshape(N_DEV, dc, d) for g in g_wo], "w_attn_out": [g.reshape(N_DEV, dc, d) for g in g_wao],
           "w_conv_out": [g.reshape(N_DEV, dc, d) for g in g_wco]}
    early = ("w_ffn_out", "w_ffn_in", "w_o", "w_attn_out", "w_conv_out")
    dq, dk, dv, received_early = _attn_bwd(p16, d_o, tot, d, [big[k][1] for k in early])

    def conv_norm_bwd(r0, xs, ps):
        _, vjp = jax.vjp(_ln_silu, xs[0], ps[0], ps[1])
        dy_, dg, db = vjp(xs[1])
        return [dy_], [dg, db]

    d_y, g_ln_g, g_ln_b = _rows("conv_norm_bwd", conv_norm_bwd, [y, d_yc], [conv_ln_g, conv_ln_b], [F32], [d, d])
    dp_a, dp_g, g_dww, g_dwb = _conv_bwd(p, d_y, dww_full, d)
    dp = jnp.concatenate([dp_a, dp_g, dq, dk, dv, dp_gc, dp_ga], axis=1)
    g_wi = _matmul("in_proj_dw", TN, u, dp, pl.BlockSpec((m_rows, d), lambda i: (0, 0)), pl.BlockSpec((m_rows, ns), lambda i: (0, i)),
                   pl.BlockSpec((None, d, ns), lambda i: (i, 0, 0)), jax.ShapeDtypeStruct((N_DEV, d, ns), F32), (N_DEV,), twin_bf16=True)
    x_i, y_i, c_i = _mesh_pos()
    slab_idx = jnp.stack([_dev(*chip, c_i) for chip in _other_chips(x_i, y_i)]).astype(I32)
    (wi_sibling,) = _sibling_exchange([g_wi[1]])
    wi_pairs = _pair_sum(slab_idx, g_wi[0], wi_sibling)
    du, wi_chips = _matmul(
        "in_proj_dx", NT, dp, wi, pl.BlockSpec((m_rows, ns), lambda i: (0, i)), pl.BlockSpec((None, d, ns), lambda i: (i, 0, 0)),
        pl.BlockSpec((m_rows, d), lambda i: (0, 0)), jax.ShapeDtypeStruct((m_rows, d), F32), (N_DEV,), acc_axis=0,
        carried=(_chip_copies, [wi_pairs], [jax.ShapeDtypeStruct(wi_pairs.shape, BF16)],
                 [pltpu.SemaphoreType.DMA((1, 3)), pltpu.SemaphoreType.DMA((1, 3))]))

    def pre_mix_bwd(r0, xs, ps):
        _, vjp = jax.vjp(_rms, xs[0], ps[0])
        dx, dg = vjp(xs[1])
        return [xs[2] + dx], [dg]

    dh0, g_pre_mix = _rows("pre_mix_bwd", pre_mix_bwd, [h0, du, dh1], [pre_mix_g], [F32], [d])
    grad_x = dh0[n_meta:length][None]

    me = _dev(x_i, y_i, c_i)
    me_arr = jnp.reshape(me, (1,)).astype(I32)
    big["w_in"] = g_wi
    received = {k: [r] for k, r in zip(early, received_early)}
    received["w_in"] = [wi_sibling[3:4], wi_chips]
    results = {}
    for k in SHARDED:
        outs = _adamw_shard(me_arr, big[k][0], received[k], local[k], shard("m_" + k), shard("v_" + k))
        results[k] = tuple((a.T if k in transposed else a)[None] for a in outs)

    rep_grads = {"pre_mix_g": g_pre_mix, "gate_b": g_gate_b, "dw_b": g_dwb, "conv_ln_g": g_ln_g, "conv_ln_b": g_ln_b,
                 "post_mix_g": g_post_mix, "pre_ffn_g": g_pre_ffn, "post_ffn_g": g_post_ffn}

    def pack_rep(get):
        return jnp.concatenate([get(k) for k in REPLICATED], axis=1).reshape(-1, LANE)

    rep_rows = pack_rep(lambda k: rep_grads[k])
    n_rep = rep_rows.shape[0]
    loss_rows = jnp.broadcast_to(loss_part, (8, LANE))
    g_meta = dh0[0:n_meta]
    slabs = jnp.stack([jnp.concatenate([rep_rows, loss_rows, g_dww[:, j * LANE:(j + 1) * LANE], g_meta[:, j * LANE:(j + 1) * LANE]], axis=0)
                       for j in range(N_DEV)])

    def pack_small(prefix):
        dww_own = jnp.pad(given[prefix + "dw_w"][0], ((0, CONV_PAD - CONV_WIDTH), (0, 0)))
        return jnp.concatenate([pack_rep(lambda k: given[prefix + k]), jnp.zeros((8, LANE), F32), dww_own,
                                given[prefix + "meta_tokens"]], axis=0)

    small = _small_reduce_adamw(slabs, pack_small(""), pack_small("m_"), pack_small("v_"))
    loss = small[0][n_rep, 0]

    def unpack(arr):
        out = {}
        flat = arr[:n_rep].reshape(1, -1)
        off = 0
        for k in REPLICATED:
            w = given[k].shape[1]
            out[k] = flat[:, off:off + w]
            off += w
        out["dw_w"] = arr[n_rep + 8:n_rep + 8 + CONV_WIDTH][None]
        out["meta_tokens"] = arr[n_rep + 8 + CONV_PAD:n_rep + 8 + CONV_PAD + n_meta]
        return out

    small_out = [unpack(a) for a in small]
    for k in WEIGHTS:
        if k not in results:
            results[k] = tuple(s[k] for s in small_out)
    return (loss, grad_x, *[results[k][0] for k in WEIGHTS], *[results[k][1] for k in WEIGHTS],
            *[results[k][2] for k in WEIGHTS], *[results[k][3] for k in WEIGHTS])
```
